```python
import jax, jax.numpy as jnp
from jax import lax
import numpy as np

D_MODEL = 1024
BATCH = 8
SEQ = 8192
DEPTH = 1

HEAD_DIM = 64
N_Q_HEADS = 16
N_KV_HEADS = 4
GROUP = N_Q_HEADS // N_KV_HEADS
ATTN_WIDTH = N_Q_HEADS * HEAD_DIM
KV_WIDTH = N_KV_HEADS * HEAD_DIM
WINDOW = 128
BLOCK = 128
CONV_CH = D_MODEL
CONV_WIDTH = 31
D_FF = 4 * D_MODEL
N_BUCKETS = 32
MAX_DISTANCE = 128
EPS = 1e-6
NEG = -1e30
Q_END = ATTN_WIDTH
K_END = Q_END + KV_WIDTH
V_END = K_END + KV_WIDTH
GLU_END = V_END + 2 * CONV_CH
IN_WIDTH = GLU_END + 2 * D_MODEL

kernel_name = "hybrid_swa_sink_conformer_gated_block"


def rms_norm(x, g):
    xf = x.astype(jnp.float32)
    y = xf * lax.rsqrt(jnp.mean(xf * xf, axis=-1, keepdims=True) + EPS)
    return (y * g.astype(jnp.float32)).astype(x.dtype)


def layer_norm(x, g, b):
    xf = x.astype(jnp.float32)
    mu = jnp.mean(xf, axis=-1, keepdims=True)
    xc = xf - mu
    var = jnp.mean(xc * xc, axis=-1, keepdims=True)
    y = xc * lax.rsqrt(var + EPS) * g.astype(jnp.float32) + b.astype(jnp.float32)
    return y.astype(x.dtype)


def t5_causal_bucket(dist):
    n = jnp.maximum(dist, 0)
    max_exact = N_BUCKETS // 2
    nf = jnp.maximum(n, 1).astype(jnp.float32)
    large = max_exact + (jnp.log(nf / max_exact) / np.float32(np.log(MAX_DISTANCE / max_exact))
                         * (N_BUCKETS - max_exact)).astype(jnp.int32)
    large = jnp.minimum(large, N_BUCKETS - 1)
    return jnp.where(n < max_exact, n, large)


def band_blocks(t, nb):
    b = t.shape[0]
    tp = jnp.pad(t, ((0, 0), (BLOCK, 0), (0, 0), (0, 0))).reshape(b, nb + 1, BLOCK, t.shape[2], t.shape[3])
    return jnp.concatenate([tp[:, :-1], tp[:, 1:]], axis=2)


def sliding_window_attention(q, k, v, sinks, rel_bias):
    b, s = q.shape[0], q.shape[1]
    nb = s // BLOCK
    qb = q.reshape(b, nb, BLOCK, N_KV_HEADS, GROUP, HEAD_DIM)
    kb = band_blocks(k, nb)
    vb = band_blocks(v, nb)
    scores = jnp.einsum('bnqhgd,bnkhd->bnhgqk', qb, kb,
                        preferred_element_type=jnp.float32)
    qi = jnp.arange(BLOCK, dtype=jnp.int32)[:, None]
    kj = jnp.arange(2 * BLOCK, dtype=jnp.int32)[None, :]
    dist = qi + BLOCK - kj
    bias = rel_bias[t5_causal_bucket(dist)].astype(jnp.float32)
    bias = jnp.transpose(bias, (2, 0, 1)).reshape(N_KV_HEADS, GROUP, BLOCK, 2 * BLOCK)
    scores = scores + bias
    key_pos = jnp.arange(nb, dtype=jnp.int32)[:, None] * BLOCK - BLOCK + kj
    valid = ((dist >= 0) & (dist < WINDOW))[None] & (key_pos >= 0)[:, None, :]
    scores = jnp.where(valid[None, :, None, None], scores, NEG)
    sink = sinks.astype(jnp.float32).reshape(N_KV_HEADS, GROUP)[None, None, :, :, None, None]
    sink = jnp.broadcast_to(sink, scores.shape[:-1] + (1,))
    probs = jax.nn.softmax(jnp.concatenate([scores, sink], axis=-1), axis=-1)[..., :-1]
    o = jnp.einsum('bnhgqk,bnkhd->bnqhgd', probs.astype(v.dtype), vb)
    return o.reshape(b, s, ATTN_WIDTH)


def conformer_conv(glu_in, w_dw, b_dw, ln_g, ln_b, w_conv_out):
    a, gate = jnp.split(glu_in, 2, axis=-1)
    h = a * jax.nn.sigmoid(gate)
    h = lax.conv_general_dilated(
        h, w_dw[:, None, :].astype(h.dtype), window_strides=(1,),
        padding=[(CONV_WIDTH - 1, 0)],
        dimension_numbers=('NWC', 'WIO', 'NWC'),
        feature_group_count=CONV_CH) + b_dw
    h = jax.nn.silu(layer_norm(h, ln_g, ln_b))
    return h @ w_conv_out


def _fwd_setup_inputs(seed: int = 0) -> dict:
    key = jax.random.key(seed)
    ks = jax.random.split(key, 20)
    f32 = jnp.float32

    def nrm(k, shape, scale):
        return jax.random.normal(k, shape, f32) * scale

    L = DEPTH
    return {
        "x": nrm(ks[0], (BATCH, SEQ, D_MODEL), 1.0),
        "norm_mix_g": 1.0 + nrm(ks[1], (L, D_MODEL), 0.02),
        "w_in": nrm(ks[2], (L, D_MODEL, IN_WIDTH), D_MODEL ** -0.5),
        "q_norm_g": 1.0 + nrm(ks[3], (L, HEAD_DIM), 0.02),
        "k_norm_g": 1.0 + nrm(ks[4], (L, HEAD_DIM), 0.02),
        "attn_sinks": nrm(ks[5], (L, N_Q_HEADS), 0.5),
        "rel_bias": nrm(ks[6], (N_BUCKETS, N_Q_HEADS), 0.5),
        "w_attn_o": nrm(ks[7], (L, ATTN_WIDTH, D_MODEL), ATTN_WIDTH ** -0.5),
        "w_dw": nrm(ks[8], (L, CONV_WIDTH, CONV_CH), CONV_WIDTH ** -0.5),
        "b_dw": nrm(ks[9], (L, CONV_CH), 0.02),
        "conv_ln_g": 1.0 + nrm(ks[10], (L, CONV_CH), 0.02),
        "conv_ln_b": nrm(ks[11], (L, CONV_CH), 0.02),
        "w_conv_out": nrm(ks[12], (L, CONV_CH, D_MODEL), CONV_CH ** -0.5),
        "w_out": nrm(ks[13], (L, D_MODEL, D_MODEL), D_MODEL ** -0.5),
        "norm_mlp_g": 1.0 + nrm(ks[14], (L, D_MODEL), 0.02),
        "w_ff1": nrm(ks[15], (L, D_MODEL, D_FF), D_MODEL ** -0.5),
        "w_ff2": nrm(ks[16], (L, D_FF, D_MODEL), D_FF ** -0.5),
    }


def _fwd_reference(x, norm_mix_g, w_in, q_norm_g, k_norm_g, attn_sinks, rel_bias, w_attn_o,
              w_dw, b_dw, conv_ln_g, conv_ln_b, w_conv_out, w_out, norm_mlp_g, w_ff1, w_ff2):
    b, s, _ = x.shape
    for l in range(DEPTH):
        u = rms_norm(x, norm_mix_g[l])
        proj = u @ w_in[l]
        q = proj[..., :Q_END].reshape(b, s, N_Q_HEADS, HEAD_DIM)
        k = proj[..., Q_END:K_END].reshape(b, s, N_KV_HEADS, HEAD_DIM)
        v = proj[..., K_END:V_END].reshape(b, s, N_KV_HEADS, HEAD_DIM)
        glu_in = proj[..., V_END:GLU_END]
        gate_attn, gate_conv = jnp.split(proj[..., GLU_END:], 2, axis=-1)

        q = rms_norm(q, q_norm_g[l]) * (HEAD_DIM ** -0.5)
        k = rms_norm(k, k_norm_g[l])
        attn = sliding_window_attention(q, k, v, attn_sinks[l], rel_bias) @ w_attn_o[l]
        conv = conformer_conv(glu_in, w_dw[l], b_dw[l], conv_ln_g[l], conv_ln_b[l], w_conv_out[l])

        merged = jax.nn.sigmoid(gate_attn) * attn + jax.nn.sigmoid(gate_conv) * conv
        x = x + merged @ w_out[l]

        hmid = jnp.square(jax.nn.relu(rms_norm(x, norm_mlp_g[l]) @ w_ff1[l]))
        x = x + hmid @ w_ff2[l]
    return x


import jax as _jax
import jax.numpy as _jnp

TWIN_FORMAT = 'train_step'
FWD_PARAMS = ['x', 'norm_mix_g', 'w_in', 'q_norm_g', 'k_norm_g', 'attn_sinks', 'rel_bias', 'w_attn_o', 'w_dw', 'b_dw', 'conv_ln_g', 'conv_ln_b', 'w_conv_out', 'w_out', 'norm_mlp_g', 'w_ff1', 'w_ff2']
TWIN_WEIGHTS = ['norm_mix_g', 'w_in', 'q_norm_g', 'k_norm_g', 'attn_sinks', 'rel_bias', 'w_attn_o', 'w_dw', 'b_dw', 'conv_ln_g', 'conv_ln_b', 'w_conv_out', 'w_out', 'norm_mlp_g', 'w_ff1', 'w_ff2']
TWIN_DIFF_INPUT = 'x'
TWIN_INPUTS = ['x', 'norm_mix_g', 'w_in', 'q_norm_g', 'k_norm_g', 'attn_sinks', 'rel_bias', 'w_attn_o', 'w_dw', 'b_dw', 'conv_ln_g', 'conv_ln_b', 'w_conv_out', 'w_out', 'norm_mlp_g', 'w_ff1', 'w_ff2', 'loss_target', 'm_norm_mix_g', 'm_w_in', 'm_q_norm_g', 'm_k_norm_g', 'm_attn_sinks', 'm_rel_bias', 'm_w_attn_o', 'm_w_dw', 'm_b_dw', 'm_conv_ln_g', 'm_conv_ln_b', 'm_w_conv_out', 'm_w_out', 'm_norm_mlp_g', 'm_w_ff1', 'm_w_ff2', 'v_norm_mix_g', 'v_w_in', 'v_q_norm_g', 'v_k_norm_g', 'v_attn_sinks', 'v_rel_bias', 'v_w_attn_o', 'v_w_dw', 'v_b_dw', 'v_conv_ln_g', 'v_conv_ln_b', 'v_w_conv_out', 'v_w_out', 'v_norm_mlp_g', 'v_w_ff1', 'v_w_ff2']
TWIN_OUTPUTS = ['loss', 'grad_x', 'grad_norm_mix_g', 'grad_w_in', 'grad_q_norm_g', 'grad_k_norm_g', 'grad_attn_sinks', 'grad_rel_bias', 'grad_w_attn_o', 'grad_w_dw', 'grad_b_dw', 'grad_conv_ln_g', 'grad_conv_ln_b', 'grad_w_conv_out', 'grad_w_out', 'grad_norm_mlp_g', 'grad_w_ff1', 'grad_w_ff2', 'delta_norm_mix_g', 'delta_w_in', 'delta_q_norm_g', 'delta_k_norm_g', 'delta_attn_sinks', 'delta_rel_bias', 'delta_w_attn_o', 'delta_w_dw', 'delta_b_dw', 'delta_conv_ln_g', 'delta_conv_ln_b', 'delta_w_conv_out', 'delta_w_out', 'delta_norm_mlp_g', 'delta_w_ff1', 'delta_w_ff2', 'new_m_norm_mix_g', 'new_m_w_in', 'new_m_q_norm_g', 'new_m_k_norm_g', 'new_m_attn_sinks', 'new_m_rel_bias', 'new_m_w_attn_o', 'new_m_w_dw', 'new_m_b_dw', 'new_m_conv_ln_g', 'new_m_conv_ln_b', 'new_m_w_conv_out', 'new_m_w_out', 'new_m_norm_mlp_g', 'new_m_w_ff1', 'new_m_w_ff2', 'new_v_norm_mix_g', 'new_v_w_in', 'new_v_q_norm_g', 'new_v_k_norm_g', 'new_v_attn_sinks', 'new_v_rel_bias', 'new_v_w_attn_o', 'new_v_w_dw', 'new_v_b_dw', 'new_v_conv_ln_g', 'new_v_conv_ln_b', 'new_v_w_conv_out', 'new_v_w_out', 'new_v_norm_mlp_g', 'new_v_w_ff1', 'new_v_w_ff2']
TWIN_LEAF_KINDS = {'loss': 'loss', 'grad_x': 'grad_x', 'grad_norm_mix_g': 'grad_w', 'grad_w_in': 'grad_w', 'grad_q_norm_g': 'grad_w', 'grad_k_norm_g': 'grad_w', 'grad_attn_sinks': 'grad_w', 'grad_rel_bias': 'grad_w', 'grad_w_attn_o': 'grad_w', 'grad_w_dw': 'grad_w', 'grad_b_dw': 'grad_w', 'grad_conv_ln_g': 'grad_w', 'grad_conv_ln_b': 'grad_w', 'grad_w_conv_out': 'grad_w', 'grad_w_out': 'grad_w', 'grad_norm_mlp_g': 'grad_w', 'grad_w_ff1': 'grad_w', 'grad_w_ff2': 'grad_w', 'delta_norm_mix_g': 'delta_w', 'delta_w_in': 'delta_w', 'delta_q_norm_g': 'delta_w', 'delta_k_norm_g': 'delta_w', 'delta_attn_sinks': 'delta_w', 'delta_rel_bias': 'delta_w', 'delta_w_attn_o': 'delta_w', 'delta_w_dw': 'delta_w', 'delta_b_dw': 'delta_w', 'delta_conv_ln_g': 'delta_w', 'delta_conv_ln_b': 'delta_w', 'delta_w_conv_out': 'delta_w', 'delta_w_out': 'delta_w', 'delta_norm_mlp_g': 'delta_w', 'delta_w_ff1': 'delta_w', 'delta_w_ff2': 'delta_w', 'new_m_norm_mix_g': 'new_m', 'new_m_w_in': 'new_m', 'new_m_q_norm_g': 'new_m', 'new_m_k_norm_g': 'new_m', 'new_m_attn_sinks': 'new_m', 'new_m_rel_bias': 'new_m', 'new_m_w_attn_o': 'new_m', 'new_m_w_dw': 'new_m', 'new_m_b_dw': 'new_m', 'new_m_conv_ln_g': 'new_m', 'new_m_conv_ln_b': 'new_m', 'new_m_w_conv_out': 'new_m', 'new_m_w_out': 'new_m', 'new_m_norm_mlp_g': 'new_m', 'new_m_w_ff1': 'new_m', 'new_m_w_ff2': 'new_m', 'new_v_norm_mix_g': 'new_v', 'new_v_w_in': 'new_v', 'new_v_q_norm_g': 'new_v', 'new_v_k_norm_g': 'new_v', 'new_v_attn_sinks': 'new_v', 'new_v_rel_bias': 'new_v', 'new_v_w_attn_o': 'new_v', 'new_v_w_dw': 'new_v', 'new_v_b_dw': 'new_v', 'new_v_conv_ln_g': 'new_v', 'new_v_conv_ln_b': 'new_v', 'new_v_w_conv_out': 'new_v', 'new_v_w_out': 'new_v', 'new_v_norm_mlp_g': 'new_v', 'new_v_w_ff1': 'new_v', 'new_v_w_ff2': 'new_v'}


def _forward(args):
    return _fwd_reference(*[args[k] for k in FWD_PARAMS])


def _output_shape():
    out = _jax.eval_shape(lambda: _forward(_fwd_setup_inputs(0)))
    return out.shape, out.dtype

N_MICROBATCH = 1
ADAM_LR = 0.001
ADAM_B1 = 0.9
ADAM_B2 = 0.999
ADAM_EPS = 1e-08
ADAM_WD = 0.01
ADAM_STEP = 10
PER_EXAMPLE_BATCH_AXIS = {'x': 0, 'loss_target': 0}
SHARED_INPUTS = []
_WEIGHT_DTYPES = {'norm_mix_g': _jnp.float32, 'w_in': _jnp.float32, 'q_norm_g': _jnp.float32, 'k_norm_g': _jnp.float32, 'attn_sinks': _jnp.float32, 'rel_bias': _jnp.float32, 'w_attn_o': _jnp.float32, 'w_dw': _jnp.float32, 'b_dw': _jnp.float32, 'conv_ln_g': _jnp.float32, 'conv_ln_b': _jnp.float32, 'w_conv_out': _jnp.float32, 'w_out': _jnp.float32, 'norm_mlp_g': _jnp.float32, 'w_ff1': _jnp.float32, 'w_ff2': _jnp.float32}
MOMENT_SCALE = {'norm_mix_g': 7.721714e-01, 'w_in': 2.088166e-01, 'q_norm_g': 3.048470e+00, 'k_norm_g': 3.035640e+00, 'attn_sinks': 3.492187e-01, 'rel_bias': 2.027210e-01, 'w_attn_o': 2.171116e-01, 'w_dw': 1.360934e+00, 'b_dw': 2.842538e+01, 'conv_ln_g': 1.464716e+01, 'conv_ln_b': 1.798026e+01, 'w_conv_out': 6.637231e+00, 'w_out': 6.618209e+00, 'norm_mlp_g': 1.916451e+02, 'w_ff1': 2.461718e+00, 'w_ff2': 1.670132e+01}


def _to_microbatches(a, axis):
    t = _jnp.moveaxis(a, axis, 0)
    t = t.reshape((N_MICROBATCH, t.shape[0] // N_MICROBATCH) + t.shape[1:])
    return _jnp.moveaxis(t, 1, axis + 1)


def setup_inputs(seed: int = 0) -> dict:
    inp = _fwd_setup_inputs(seed)
    key = _jax.random.fold_in(_jax.random.key(seed), 7919)
    shape, _ = _output_shape()
    out = dict(inp)
    out["loss_target"] = _jax.random.normal(_jax.random.fold_in(key, 0), shape, _jnp.float32)
    for i, name in enumerate(TWIN_WEIGHTS):
        w = inp[name].astype(_jnp.float32)
        if MOMENT_SCALE is None:
            s = _jnp.sqrt(_jnp.mean(_jnp.square(w)) + 1e-30)
        else:
            s = MOMENT_SCALE[name]
        km, kv = _jax.random.split(_jax.random.fold_in(key, i + 1))
        out[name] = w
        out["m_" + name] = s * _jax.random.normal(km, w.shape, _jnp.float32)
        out["v_" + name] = (s * s) * _jax.random.uniform(kv, w.shape, _jnp.float32, 0.5, 1.5)
    if N_MICROBATCH > 1:
        for name, axis in PER_EXAMPLE_BATCH_AXIS.items():
            out[name] = _to_microbatches(out[name], axis)
    return {'x': out['x'], 'norm_mix_g': out['norm_mix_g'], 'w_in': out['w_in'], 'q_norm_g': out['q_norm_g'], 'k_norm_g': out['k_norm_g'], 'attn_sinks': out['attn_sinks'], 'rel_bias': out['rel_bias'], 'w_attn_o': out['w_attn_o'], 'w_dw': out['w_dw'], 'b_dw': out['b_dw'], 'conv_ln_g': out['conv_ln_g'], 'conv_ln_b': out['conv_ln_b'], 'w_conv_out': out['w_conv_out'], 'w_out': out['w_out'], 'norm_mlp_g': out['norm_mlp_g'], 'w_ff1': out['w_ff1'], 'w_ff2': out['w_ff2'], 'loss_target': out['loss_target'], 'm_norm_mix_g': out['m_norm_mix_g'], 'm_w_in': out['m_w_in'], 'm_q_norm_g': out['m_q_norm_g'], 'm_k_norm_g': out['m_k_norm_g'], 'm_attn_sinks': out['m_attn_sinks'], 'm_rel_bias': out['m_rel_bias'], 'm_w_attn_o': out['m_w_attn_o'], 'm_w_dw': out['m_w_dw'], 'm_b_dw': out['m_b_dw'], 'm_conv_ln_g': out['m_conv_ln_g'], 'm_conv_ln_b': out['m_conv_ln_b'], 'm_w_conv_out': out['m_w_conv_out'], 'm_w_out': out['m_w_out'], 'm_norm_mlp_g': out['m_norm_mlp_g'], 'm_w_ff1': out['m_w_ff1'], 'm_w_ff2': out['m_w_ff2'], 'v_norm_mix_g': out['v_norm_mix_g'], 'v_w_in': out['v_w_in'], 'v_q_norm_g': out['v_q_norm_g'], 'v_k_norm_g': out['v_k_norm_g'], 'v_attn_sinks': out['v_attn_sinks'], 'v_rel_bias': out['v_rel_bias'], 'v_w_attn_o': out['v_w_attn_o'], 'v_w_dw': out['v_w_dw'], 'v_b_dw': out['v_b_dw'], 'v_conv_ln_g': out['v_conv_ln_g'], 'v_conv_ln_b': out['v_conv_ln_b'], 'v_w_conv_out': out['v_w_conv_out'], 'v_w_out': out['v_w_out'], 'v_norm_mlp_g': out['v_norm_mlp_g'], 'v_w_ff1': out['v_w_ff1'], 'v_w_ff2': out['v_w_ff2']}


def _loss(weights, diff, rest, loss_target):
    with _jax.named_scope("forward"):
        args = {**rest, TWIN_DIFF_INPUT: diff, **{k: w.astype(_WEIGHT_DTYPES[k]) for k, w in weights.items()}}
        y = _forward(args)
    with _jax.named_scope("loss_head"):
        err = _jnp.square(y.astype(_jnp.float32) - loss_target)
        return 0.5 * _jnp.sum(_jnp.mean(err, axis=-1)) if err.ndim else 0.5 * err


def _adamw(w, g, m, v):
    m = ADAM_B1 * m + (1.0 - ADAM_B1) * g
    v = ADAM_B2 * v + (1.0 - ADAM_B2) * _jnp.square(g)
    m_hat = m / (1.0 - ADAM_B1 ** ADAM_STEP)
    v_hat = v / (1.0 - ADAM_B2 ** ADAM_STEP)
    delta = -ADAM_LR * (m_hat / (_jnp.sqrt(v_hat) + ADAM_EPS) + ADAM_WD * w)
    return delta, m, v


def reference(x, norm_mix_g, w_in, q_norm_g, k_norm_g, attn_sinks, rel_bias, w_attn_o, w_dw, b_dw, conv_ln_g, conv_ln_b, w_conv_out, w_out, norm_mlp_g, w_ff1, w_ff2, loss_target, m_norm_mix_g, m_w_in, m_q_norm_g, m_k_norm_g, m_attn_sinks, m_rel_bias, m_w_attn_o, m_w_dw, m_b_dw, m_conv_ln_g, m_conv_ln_b, m_w_conv_out, m_w_out, m_norm_mlp_g, m_w_ff1, m_w_ff2, v_norm_mix_g, v_w_in, v_q_norm_g, v_k_norm_g, v_attn_sinks, v_rel_bias, v_w_attn_o, v_w_dw, v_b_dw, v_conv_ln_g, v_conv_ln_b, v_w_conv_out, v_w_out, v_norm_mlp_g, v_w_ff1, v_w_ff2):
    given = dict(x=x, norm_mix_g=norm_mix_g, w_in=w_in, q_norm_g=q_norm_g, k_norm_g=k_norm_g, attn_sinks=attn_sinks, rel_bias=rel_bias, w_attn_o=w_attn_o, w_dw=w_dw, b_dw=b_dw, conv_ln_g=conv_ln_g, conv_ln_b=conv_ln_b, w_conv_out=w_conv_out, w_out=w_out, norm_mlp_g=norm_mlp_g, w_ff1=w_ff1, w_ff2=w_ff2, loss_target=loss_target, m_norm_mix_g=m_norm_mix_g, m_w_in=m_w_in, m_q_norm_g=m_q_norm_g, m_k_norm_g=m_k_norm_g, m_attn_sinks=m_attn_sinks, m_rel_bias=m_rel_bias, m_w_attn_o=m_w_attn_o, m_w_dw=m_w_dw, m_b_dw=m_b_dw, m_conv_ln_g=m_conv_ln_g, m_conv_ln_b=m_conv_ln_b, m_w_conv_out=m_w_conv_out, m_w_out=m_w_out, m_norm_mlp_g=m_norm_mlp_g, m_w_ff1=m_w_ff1, m_w_ff2=m_w_ff2, v_norm_mix_g=v_norm_mix_g, v_w_in=v_w_in, v_q_norm_g=v_q_norm_g, v_k_norm_g=v_k_norm_g, v_attn_sinks=v_attn_sinks, v_rel_bias=v_rel_bias, v_w_attn_o=v_w_attn_o, v_w_dw=v_w_dw, v_b_dw=v_b_dw, v_conv_ln_g=v_conv_ln_g, v_conv_ln_b=v_conv_ln_b, v_w_conv_out=v_w_conv_out, v_w_out=v_w_out, v_norm_mlp_g=v_norm_mlp_g, v_w_ff1=v_w_ff1, v_w_ff2=v_w_ff2)
    weights = {n: given[n] for n in TWIN_WEIGHTS}
    shared = {n: given[n] for n in SHARED_INPUTS}
    per_example = {n: given[n] for n in ['x']}
    grad_fn = _jax.value_and_grad(_loss, argnums=(0, 1))

    def one_microbatch(ex, loss_target):
        ex = dict(ex)
        diff = ex.pop(TWIN_DIFF_INPUT)
        return grad_fn(weights, diff, {**shared, **ex}, loss_target)

    if N_MICROBATCH == 1:
        loss, (grad_w, grad_x) = one_microbatch(per_example, given["loss_target"])
    else:
        def body(carry, xs):
            loss_sum, grad_sum = carry
            l_k, (gw_k, gx_k) = one_microbatch(xs[0], xs[1])
            with _jax.named_scope("update"):
                return (loss_sum + l_k, _jax.tree.map(_jnp.add, grad_sum, gw_k)), gx_k

        init = (_jnp.zeros((), _jnp.float32), _jax.tree.map(_jnp.zeros_like, weights))
        (loss, grad_w), grad_x = _jax.lax.scan(body, init, (per_example, given["loss_target"]))
    with _jax.named_scope("update"):
        delta_w, new_m, new_v = {}, {}, {}
        for n in TWIN_WEIGHTS:
            delta_w[n], new_m[n], new_v[n] = _adamw(weights[n], grad_w[n], given["m_" + n], given["v_" + n])
    return (loss, grad_x, *[grad_w[n] for n in TWIN_WEIGHTS], *[delta_w[n] for n in TWIN_WEIGHTS],
            *[new_m[n] for n in TWIN_WEIGHTS], *[new_v[n] for n in TWIN_WEIGHTS])
```

```python
import functools

import numpy as np
import jax
import jax.numpy as jnp
from jax import lax
from jax.experimental import pallas as pl
from jax.experimental.pallas import tpu as pltpu

f32 = jnp.float32
bf16 = jnp.bfloat16
S = jax.ShapeDtypeStruct

N_DEV = 8
D_MODEL = 1024
HEAD_DIM = 64
N_Q_HEADS = 16
N_KV_HEADS = 4
GROUP = N_Q_HEADS // N_KV_HEADS
ATTN_WIDTH = N_Q_HEADS * HEAD_DIM
KV_WIDTH = N_KV_HEADS * HEAD_DIM
QBLOCK = 128
CONV_WIDTH = 31
CONV_HALO = 32
D_FF = 4 * D_MODEL
N_BUCKETS = 32
MAX_DISTANCE = 128
EPS = 1e-6
NEG = -1e30
Q_END = ATTN_WIDTH
K_END = Q_END + KV_WIDTH
V_END = K_END + KV_WIDTH
GLU_END = V_END + 2 * D_MODEL
IN_WIDTH = GLU_END + 2 * D_MODEL
COL = 512
FF_CHUNK = D_FF // N_DEV

ADAM_LR = 0.001
ADAM_B1 = 0.9
ADAM_B2 = 0.999
ADAM_EPS = 1e-08
ADAM_WD = 0.01
ADAM_STEP = 10

VMEM_LIMIT = 56 * 1024 * 1024

MESH_ID = pl.DeviceIdType.MESH
ANY = pl.BlockSpec(memory_space=pl.ANY)
SMEM = pl.BlockSpec(memory_space=pltpu.SMEM)


def _params(*sem):
    return pltpu.CompilerParams(dimension_semantics=sem, vmem_limit_bytes=VMEM_LIMIT)


def _nt(a, b):
    return lax.dot_general(a, b, (((1,), (1,)), ((), ())), preferred_element_type=f32)


def _tn(a, b):
    return lax.dot_general(a, b, (((0,), (0,)), ((), ())), preferred_element_type=f32)


def _sigmoid(z):
    return 1.0 / (1.0 + jnp.exp(-z))


def _t5_bucket_table():
    qi = np.arange(QBLOCK, dtype=np.int32)[:, None]
    kj = np.arange(2 * QBLOCK, dtype=np.int32)[None, :]
    dist = qi + QBLOCK - kj
    n = np.maximum(dist, 0)
    max_exact = N_BUCKETS // 2
    nf = np.maximum(n, 1).astype(np.float32)
    large = max_exact + (np.log(nf / np.float32(max_exact)) / np.float32(np.log(MAX_DISTANCE / max_exact))
                         * np.float32(N_BUCKETS - max_exact)).astype(np.int32)
    large = np.minimum(large, N_BUCKETS - 1)
    bucket = np.where(n < max_exact, n, large)
    valid = (dist >= 0) & (dist < QBLOCK)
    return np.where(valid, bucket, -1).astype(np.int32)


def _peer(d):
    x, y, c = lax.axis_index("x"), lax.axis_index("y"), lax.axis_index("c")
    dx, dy, dc = (d >> 2) & 1, (d >> 1) & 1, d & 1
    px, py, pc = x ^ dx, y ^ dy, c ^ dc
    return (px, py, pc), 4 * px + 2 * py + pc


def _exchange(name, arrays, gather):
    n = len(arrays)

    def body(*refs):
        ins, outs = refs[:n], refs[n:2 * n]
        send_sems, recv_sems, local_sems = refs[2 * n:]
        _, me = _peer(0)
        local = []
        for k in range(n):
            src = ins[k] if gather else ins[k].at[me]
            cp = pltpu.make_async_copy(src, outs[k].at[me], local_sems.at[k])
            cp.start()
            local.append(cp)
        sends = []
        for d in range(1, N_DEV):
            peer, pidx = _peer(d)
            for k in range(n):
                src = ins[k] if gather else ins[k].at[pidx]
                cp = pltpu.make_async_remote_copy(src_ref=src, dst_ref=outs[k].at[me], send_sem=send_sems.at[k, d - 1],
                                                  recv_sem=recv_sems.at[k, d - 1], device_id=peer, device_id_type=MESH_ID)
                cp.start()
                sends.append(cp)
        for d in range(1, N_DEV):
            peer, pidx = _peer(d)
            for k in range(n):
                src = ins[k] if gather else ins[k].at[pidx]
                pltpu.make_async_remote_copy(src_ref=src, dst_ref=outs[k].at[pidx], send_sem=send_sems.at[k, d - 1],
                                             recv_sem=recv_sems.at[k, d - 1], device_id=peer, device_id_type=MESH_ID).wait_recv()
        for cp in sends:
            cp.wait_send()
        for cp in local:
            cp.wait()

    if gather:
        out_shape = [S((N_DEV,) + a.shape, a.dtype) for a in arrays]
    else:
        out_shape = [S(a.shape, a.dtype) for a in arrays]
    return pl.pallas_call(
        body, name=name, out_shape=out_shape, in_specs=[ANY] * n, out_specs=[ANY] * n,
        scratch_shapes=[pltpu.SemaphoreType.DMA((n, N_DEV - 1)), pltpu.SemaphoreType.DMA((n, N_DEV - 1)),
                        pltpu.SemaphoreType.DMA((n,))],
    )(*arrays)


def _proj_fwd(x, g, w, tm, tn):
    T, K = x.shape
    N = w.shape[1]

    def body(x_ref, g_ref, w_ref, o_ref, u_ref):
        @pl.when(pl.program_id(1) == 0)
        def _():
            xv = x_ref[...]
            r = lax.rsqrt(jnp.mean(xv * xv, axis=-1, keepdims=True) + EPS)
            u_ref[...] = (xv * r * g_ref[...]).astype(bf16)

        o_ref[...] = jnp.dot(u_ref[...], w_ref[...], preferred_element_type=f32)

    return pl.pallas_call(
        body, name="proj_fwd", grid=(T // tm, N // tn),
        in_specs=[pl.BlockSpec((tm, K), lambda i, j: (i, 0)), pl.BlockSpec((1, K), lambda i, j: (0, 0)),
                  pl.BlockSpec((K, tn), lambda i, j: (0, j))],
        out_specs=[pl.BlockSpec((tm, tn), lambda i, j: (i, j)), pl.BlockSpec((tm, K), lambda i, j: (i, 0))],
        out_shape=[S((T, N), f32), S((T, K), bf16)],
        compiler_params=_params("parallel", "arbitrary"),
    )(x, g, w)


def _prep_fwd(proj, qg, kg, tm):
    T = proj.shape[0]

    def body(p_ref, qg_ref, kg_ref, qn_ref, kn_ref, vb_ref, h0_ref):
        qgv = qg_ref[...] * (HEAD_DIM ** -0.5)
        kgv = kg_ref[...]
        for h in range(N_Q_HEADS):
            seg = p_ref[:, h * HEAD_DIM:(h + 1) * HEAD_DIM]
            r = lax.rsqrt(jnp.mean(seg * seg, axis=-1, keepdims=True) + EPS)
            qn_ref[:, h * HEAD_DIM:(h + 1) * HEAD_DIM] = (seg * r * qgv).astype(bf16)
        for h in range(N_KV_HEADS):
            seg = p_ref[:, Q_END + h * HEAD_DIM:Q_END + (h + 1) * HEAD_DIM]
            r = lax.rsqrt(jnp.mean(seg * seg, axis=-1, keepdims=True) + EPS)
            kn_ref[:, h * HEAD_DIM:(h + 1) * HEAD_DIM] = (seg * r * kgv).astype(bf16)
        vb_ref[...] = p_ref[:, K_END:V_END].astype(bf16)
        h0_ref[...] = p_ref[:, V_END:V_END + D_MODEL] * _sigmoid(p_ref[:, V_END + D_MODEL:GLU_END])

    return pl.pallas_call(
        body, name="prep_fwd", grid=(T // tm,),
        in_specs=[pl.BlockSpec((tm, GLU_END), lambda i: (i, 0)), pl.BlockSpec((1, HEAD_DIM), lambda i: (0, 0)),
                  pl.BlockSpec((1, HEAD_DIM), lambda i: (0, 0))],
        out_specs=[pl.BlockSpec((tm, ATTN_WIDTH), lambda i: (i, 0)), pl.BlockSpec((tm, KV_WIDTH), lambda i: (i, 0)),
                   pl.BlockSpec((tm, KV_WIDTH), lambda i: (i, 0)), pl.BlockSpec((tm, D_MODEL), lambda i: (i, 0))],
        out_shape=[S((T, ATTN_WIDTH), bf16), S((T, KV_WIDTH), bf16), S((T, KV_WIDTH), bf16), S((T, D_MODEL), f32)],
        compiler_params=_params("parallel"),
    )(proj, qg, kg)


def _bias_table(rel_bias, bucket):
    def body(rb_ref, bk_ref, o_ref):
        b = bk_ref[...]
        for h in range(N_Q_HEADS):
            acc = jnp.full((QBLOCK, 2 * QBLOCK), NEG, f32)
            for k in range(N_BUCKETS):
                acc = jnp.where(b == k, rb_ref[k, h], acc)
            o_ref[h * QBLOCK:(h + 1) * QBLOCK, :] = acc

    return pl.pallas_call(
        body, name="bias_table", out_shape=S((N_Q_HEADS * QBLOCK, 2 * QBLOCK), f32),
        in_specs=[SMEM, pl.BlockSpec(memory_space=pltpu.VMEM)],
    )(rel_bias, bucket)


def _first_block_mask(n):
    kj = lax.broadcasted_iota(jnp.int32, (1, 2 * QBLOCK), 1)
    return jnp.where((kj < QBLOCK) & (n == 0), NEG, 0.0).astype(f32)


def _attn_fwd(qn, kn, vb, bias, sinks):
    T = qn.shape[0]
    nb = T // QBLOCK

    def body(q_ref, kc_ref, kp_ref, vc_ref, vp_ref, b_ref, s_ref, o_ref, lse_ref):
        n = pl.program_id(0)
        pm = _first_block_mask(n)
        qv = q_ref[...]
        kband = jnp.concatenate([kp_ref[...], kc_ref[...]], axis=0)
        vband = jnp.concatenate([vp_ref[...], vc_ref[...]], axis=0)
        for h in range(N_KV_HEADS):
            kh = kband[:, h * HEAD_DIM:(h + 1) * HEAD_DIM]
            vh = vband[:, h * HEAD_DIM:(h + 1) * HEAD_DIM]
            q4 = jnp.concatenate([qv[:, (h * GROUP + g) * HEAD_DIM:(h * GROUP + g + 1) * HEAD_DIM] for g in range(GROUP)], axis=0)
            s = _nt(q4, kh) + b_ref[h * GROUP * QBLOCK:(h + 1) * GROUP * QBLOCK, :] + pm
            sink = jnp.concatenate([jnp.full((QBLOCK, 1), s_ref[0, h * GROUP + g], f32) for g in range(GROUP)], axis=0)
            m = jnp.maximum(jnp.max(s, axis=-1, keepdims=True), sink)
            p = jnp.exp(s - m)
            l = jnp.sum(p, axis=-1, keepdims=True) + jnp.exp(sink - m)
            o4 = jnp.dot((p * (1.0 / l)).astype(bf16), vh, preferred_element_type=f32)
            lse4 = m + jnp.log(l)
            for g in range(GROUP):
                hq = h * GROUP + g
                o_ref[:, hq * HEAD_DIM:(hq + 1) * HEAD_DIM] = o4[g * QBLOCK:(g + 1) * QBLOCK, :].astype(bf16)
                lse_ref[:, hq:hq + 1] = lse4[g * QBLOCK:(g + 1) * QBLOCK, :]

    cur = lambda n: (n, 0)
    prev = lambda n: (jnp.maximum(n - 1, 0), 0)
    return pl.pallas_call(
        body, name="attn_fwd", grid=(nb,),
        in_specs=[pl.BlockSpec((QBLOCK, ATTN_WIDTH), cur), pl.BlockSpec((QBLOCK, KV_WIDTH), cur),
                  pl.BlockSpec((QBLOCK, KV_WIDTH), prev), pl.BlockSpec((QBLOCK, KV_WIDTH), cur),
                  pl.BlockSpec((QBLOCK, KV_WIDTH), prev),
                  pl.BlockSpec((N_Q_HEADS * QBLOCK, 2 * QBLOCK), lambda n: (0, 0)), SMEM],
        out_specs=[pl.BlockSpec((QBLOCK, ATTN_WIDTH), cur), pl.BlockSpec((QBLOCK, N_Q_HEADS), cur)],
        out_shape=[S((T, ATTN_WIDTH), bf16), S((T, N_Q_HEADS), f32)],
        compiler_params=_params("parallel"),
    )(qn, kn, kn, vb, vb, bias, sinks)


def _layer_norm_stats(h1):
    mu = jnp.mean(h1, axis=-1, keepdims=True)
    xc = h1 - mu
    rstd = lax.rsqrt(jnp.mean(xc * xc, axis=-1, keepdims=True) + EPS)
    return xc * rstd, rstd


def _conv_fwd(h0, w_dw, b_dw, ln_g, ln_b, tm):
    T = h0.shape[0]
    per = tm // CONV_HALO

    def body(hc_ref, hp_ref, w_ref, b_ref, g_ref, bb_ref, h1_ref, h3_ref, cat):
        i = pl.program_id(0)
        cat[0:CONV_HALO, :] = jnp.where(i == 0, 0.0, hp_ref[...])
        cat[CONV_HALO:, :] = hc_ref[...]
        acc = jnp.zeros((tm, D_MODEL), f32) + b_ref[...]
        for k in range(CONV_WIDTH):
            acc = acc + cat[pl.ds(CONV_HALO - (CONV_WIDTH - 1) + k, tm), :] * w_ref[k:k + 1, :]
        h1_ref[...] = acc
        xhat, _ = _layer_norm_stats(acc)
        h2 = xhat * g_ref[...] + bb_ref[...]
        h3_ref[...] = (h2 * _sigmoid(h2)).astype(bf16)

    vec = pl.BlockSpec((1, D_MODEL), lambda i: (0, 0))
    return pl.pallas_call(
        body, name="conv_fwd", grid=(T // tm,),
        in_specs=[pl.BlockSpec((tm, D_MODEL), lambda i: (i, 0)),
                  pl.BlockSpec((CONV_HALO, D_MODEL), lambda i: (jnp.maximum(i * per - 1, 0), 0)),
                  pl.BlockSpec((CONV_WIDTH, D_MODEL), lambda i: (0, 0)), vec, vec, vec],
        out_specs=[pl.BlockSpec((tm, D_MODEL), lambda i: (i, 0)), pl.BlockSpec((tm, D_MODEL), lambda i: (i, 0))],
        out_shape=[S((T, D_MODEL), f32), S((T, D_MODEL), bf16)],
        scratch_shapes=[pltpu.VMEM((tm + CONV_HALO, D_MODEL), f32)],
        compiler_params=_params("parallel"),
    )(h0, h0, w_dw, b_dw, ln_g, ln_b)


def _mix_fwd(x, o, h3, proj, w_ao, w_co, w_o, tm):
    T = x.shape[0]
    row = pl.BlockSpec((tm, D_MODEL), lambda i: (i, 0))
    wsp = pl.BlockSpec((D_MODEL, D_MODEL), lambda i: (0, 0))
    g0 = GLU_END // COL

    def gate_spec(off):
        return pl.BlockSpec((tm, COL), lambda i: (i, g0 + off))

    def body(x_ref, o_ref, h3_ref, ga0, ga1, gc0, gc1, wa_ref, wc_ref, wo_ref, x1_ref, at_ref, cv_ref, mg_ref):
        attn = jnp.dot(o_ref[...], wa_ref[...], preferred_element_type=f32)
        conv = jnp.dot(h3_ref[...], wc_ref[...], preferred_element_type=f32)
        ga = jnp.concatenate([ga0[...], ga1[...]], axis=-1)
        gc = jnp.concatenate([gc0[...], gc1[...]], axis=-1)
        merged = (_sigmoid(ga) * attn + _sigmoid(gc) * conv).astype(bf16)
        at_ref[...] = attn.astype(bf16)
        cv_ref[...] = conv.astype(bf16)
        mg_ref[...] = merged
        x1_ref[...] = x_ref[...] + jnp.dot(merged, wo_ref[...], preferred_element_type=f32)

    return pl.pallas_call(
        body, name="mix_fwd", grid=(T // tm,),
        in_specs=[row, row, row, gate_spec(0), gate_spec(1), gate_spec(2), gate_spec(3), wsp, wsp, wsp],
        out_specs=[row, row, row, row],
        out_shape=[S((T, D_MODEL), f32), S((T, D_MODEL), bf16), S((T, D_MODEL), bf16), S((T, D_MODEL), bf16)],
        compiler_params=_params("parallel"),
    )(x, o, h3, proj, proj, proj, proj, w_ao, w_co, w_o)


def _ffn_fwd(x1, g, w1, w2, target, tm):
    T = x1.shape[0]
    nj = w1.shape[0]

    def body(x_ref, g_ref, w1_ref, w2_ref, t_ref, a_ref, u_ref, dy_ref, ls_ref, acc):
        j = pl.program_id(1)

        @pl.when(j == 0)
        def _():
            xv = x_ref[...]
            r = lax.rsqrt(jnp.mean(xv * xv, axis=-1, keepdims=True) + EPS)
            u_ref[...] = (xv * r * g_ref[...]).astype(bf16)
            acc[...] = jnp.zeros_like(acc)

        a = jnp.dot(u_ref[...], w1_ref[...], preferred_element_type=f32)
        a_ref[...] = a.astype(bf16)
        hm = jnp.square(jnp.maximum(a, 0.0)).astype(bf16)
        acc[...] += jnp.dot(hm, w2_ref[...], preferred_element_type=f32)

        @pl.when(j == nj - 1)
        def _():
            err = x_ref[...] + acc[...] - t_ref[...]
            dy_ref[...] = err * (1.0 / D_MODEL)
            ls_ref[...] = jnp.zeros((8, 128), f32) + jnp.sum(err * err) * (0.5 / D_MODEL)

    row = pl.BlockSpec((tm, D_MODEL), lambda i, j: (i, 0))
    return pl.pallas_call(
        body, name="ffn_fwd", grid=(T // tm, nj),
        in_specs=[row, pl.BlockSpec((1, D_MODEL), lambda i, j: (0, 0)),
                  pl.BlockSpec((None, D_MODEL, FF_CHUNK), lambda i, j: (j, 0, 0)),
                  pl.BlockSpec((None, FF_CHUNK, D_MODEL), lambda i, j: (j, 0, 0)), row],
        out_specs=[pl.BlockSpec((tm, FF_CHUNK), lambda i, j: (i, j)), row, row,
                   pl.BlockSpec((None, 8, 128), lambda i, j: (i, 0, 0))],
        out_shape=[S((T, D_FF), bf16), S((T, D_MODEL), bf16), S((T, D_MODEL), f32), S((T // tm, 8, 128), f32)],
        scratch_shapes=[pltpu.VMEM((tm, D_MODEL), f32)],
        compiler_params=_params("parallel", "arbitrary"),
    )(x1, g, w1, w2, target)


def _rms_bwd(du, xv, gv):
    r = lax.rsqrt(jnp.mean(xv * xv, axis=-1, keepdims=True) + EPS)
    xn = xv * r
    dg = jnp.sum(du * xn, axis=0, keepdims=True)
    dxn = du * gv
    dx = r * (dxn - xn * jnp.mean(dxn * xn, axis=-1, keepdims=True))
    return dx, dg


def _ffn_bwd(dy, a, x1, g, w1, w2, tm):
    T = dy.shape[0]
    nj = w1.shape[0]

    def body(dy_ref, a_ref, x_ref, g_ref, w1_ref, w2_ref, da_ref, dx_ref, dg_ref, acc, dyb):
        i, j = pl.program_id(0), pl.program_id(1)

        @pl.when(j == 0)
        def _():
            dyb[...] = dy_ref[...].astype(bf16)
            acc[...] = jnp.zeros_like(acc)

        @pl.when((i == 0) & (j == 0))
        def _():
            dg_ref[...] = jnp.zeros_like(dg_ref)

        dh = _nt(dyb[...], w2_ref[...])
        da = (dh * (2.0 * jnp.maximum(a_ref[...].astype(f32), 0.0))).astype(bf16)
        da_ref[...] = da
        acc[...] += _nt(da, w1_ref[...])

        @pl.when(j == nj - 1)
        def _():
            dx, dg = _rms_bwd(acc[...], x_ref[...], g_ref[...])
            dx_ref[...] = dy_ref[...] + dx
            dg_ref[...] += dg

    row = pl.BlockSpec((tm, D_MODEL), lambda i, j: (i, 0))
    vec = pl.BlockSpec((1, D_MODEL), lambda i, j: (0, 0))
    return pl.pallas_call(
        body, name="ffn_bwd", grid=(T // tm, nj),
        in_specs=[row, pl.BlockSpec((tm, FF_CHUNK), lambda i, j: (i, j)), row, vec,
                  pl.BlockSpec((None, D_MODEL, FF_CHUNK), lambda i, j: (j, 0, 0)),
                  pl.BlockSpec((None, FF_CHUNK, D_MODEL), lambda i, j: (j, 0, 0))],
        out_specs=[pl.BlockSpec((tm, FF_CHUNK), lambda i, j: (i, j)), row, vec],
        out_shape=[S((T, D_FF), bf16), S((T, D_MODEL), f32), S((1, D_MODEL), f32)],
        scratch_shapes=[pltpu.VMEM((tm, D_MODEL), f32), pltpu.VMEM((tm, D_MODEL), bf16)],
        compiler_params=_params("arbitrary", "arbitrary"),
    )(dy, a, x1, g, w1, w2)


def _wgrad(name, a, b, tk, tn, tt, relu2=False, col_shard=False):
    T, Ka = a.shape
    Nb = b.shape[1]

    def body(a_ref, b_ref, o_ref):
        t = pl.program_id(2)
        av = a_ref[...]
        if relu2:
            av = jnp.square(jnp.maximum(av.astype(f32), 0.0))
        prod = _tn(av.astype(bf16), b_ref[...].astype(bf16))

        @pl.when(t == 0)
        def _():
            o_ref[...] = prod

        @pl.when(t > 0)
        def _():
            o_ref[...] += prod

    if col_shard:
        out_shape = S((Nb // tn, Ka, tn), f32)
        out_spec = pl.BlockSpec((None, tk, tn), lambda i, j, t: (j, i, 0))
    else:
        out_shape = S((Ka, Nb), f32)
        out_spec = pl.BlockSpec((tk, tn), lambda i, j, t: (i, j))
    return pl.pallas_call(
        body, name=name, grid=(Ka // tk, Nb // tn, T // tt),
        in_specs=[pl.BlockSpec((tt, tk), lambda i, j, t: (t, i)), pl.BlockSpec((tt, tn), lambda i, j, t: (t, j))],
        out_specs=out_spec, out_shape=out_shape,
        compiler_params=_params("parallel", "parallel", "arbitrary"),
    )(a, b)


def _mix_bwd(dx1, proj, attn, conv, w_ao, w_co, w_o, tm):
    T = dx1.shape[0]
    g0 = GLU_END // COL

    def gate_spec(off):
        return pl.BlockSpec((tm, COL), lambda i: (i, g0 + off))

    def body(dx_ref, ga0, ga1, gc0, gc1, at_ref, cv_ref, wa_ref, wc_ref, wo_ref, da_ref, dc_ref, do_ref, dh3_ref, dg_ref):
        dm = _nt(dx_ref[...].astype(bf16), wo_ref[...])
        sa = _sigmoid(jnp.concatenate([ga0[...], ga1[...]], axis=-1))
        sc = _sigmoid(jnp.concatenate([gc0[...], gc1[...]], axis=-1))
        dattn = (dm * sa).astype(bf16)
        dconv = (dm * sc).astype(bf16)
        da_ref[...] = dattn
        dc_ref[...] = dconv
        dg_ref[:, 0:D_MODEL] = (dm * at_ref[...].astype(f32) * sa * (1.0 - sa)).astype(bf16)
        dg_ref[:, D_MODEL:2 * D_MODEL] = (dm * cv_ref[...].astype(f32) * sc * (1.0 - sc)).astype(bf16)
        do_ref[...] = _nt(dattn, wa_ref[...]).astype(bf16)
        dh3_ref[...] = _nt(dconv, wc_ref[...])

    row = pl.BlockSpec((tm, D_MODEL), lambda i: (i, 0))
    wsp = pl.BlockSpec((D_MODEL, D_MODEL), lambda i: (0, 0))
    return pl.pallas_call(
        body, name="mix_bwd", grid=(T // tm,),
        in_specs=[row, gate_spec(0), gate_spec(1), gate_spec(2), gate_spec(3), row, row, wsp, wsp, wsp],
        out_specs=[row, row, row, row, pl.BlockSpec((tm, 2 * D_MODEL), lambda i: (i, 0))],
        out_shape=[S((T, D_MODEL), bf16), S((T, D_MODEL), bf16), S((T, D_MODEL), bf16), S((T, D_MODEL), f32),
                   S((T, 2 * D_MODEL), bf16)],
        compiler_params=_params("parallel"),
    )(dx1, proj, proj, proj, proj, attn, conv, w_ao, w_co, w_o)


def _norm_act_bwd(dh3, h1, ln_g, ln_b, tm):
    T = dh3.shape[0]

    def body(d_ref, h_ref, g_ref, b_ref, dh1_ref, dg_ref, db_ref, dbd_ref):
        @pl.when(pl.program_id(0) == 0)
        def _():
            dg_ref[...] = jnp.zeros_like(dg_ref)
            db_ref[...] = jnp.zeros_like(db_ref)
            dbd_ref[...] = jnp.zeros_like(dbd_ref)

        xhat, rstd = _layer_norm_stats(h_ref[...])
        h2 = xhat * g_ref[...] + b_ref[...]
        sg = _sigmoid(h2)
        dh2 = d_ref[...] * (sg * (1.0 + h2 * (1.0 - sg)))
        dg_ref[...] += jnp.sum(dh2 * xhat, axis=0, keepdims=True)
        db_ref[...] += jnp.sum(dh2, axis=0, keepdims=True)
        dxh = dh2 * g_ref[...]
        dh1 = rstd * (dxh - jnp.mean(dxh, axis=-1, keepdims=True) - xhat * jnp.mean(dxh * xhat, axis=-1, keepdims=True))
        dh1_ref[...] = dh1
        dbd_ref[...] += jnp.sum(dh1, axis=0, keepdims=True)

    row = pl.BlockSpec((tm, D_MODEL), lambda i: (i, 0))
    vec = pl.BlockSpec((1, D_MODEL), lambda i: (0, 0))
    return pl.pallas_call(
        body, name="norm_act_bwd", grid=(T // tm,),
        in_specs=[row, row, vec, vec], out_specs=[row, vec, vec, vec],
        out_shape=[S((T, D_MODEL), f32), S((1, D_MODEL), f32), S((1, D_MODEL), f32), S((1, D_MODEL), f32)],
        compiler_params=_params("arbitrary"),
    )(dh3, h1, ln_g, ln_b)


def _conv_bwd(dh1, h0, proj, w_dw, tm):
    T = dh1.shape[0]
    per = tm // CONV_HALO
    nh = T // CONV_HALO
    nt = T // tm
    a0 = V_END // COL

    def body(dc_ref, dn_ref, hc_ref, hp_ref, a0_ref, a1_ref, g0_ref, g1_ref, w_ref, dglu_ref, dw_ref, dcat, hcat, wacc):
        i = pl.program_id(0)

        @pl.when(i == 0)
        def _():
            wacc[...] = jnp.zeros_like(wacc)

        dcat[0:tm, :] = dc_ref[...]
        dcat[tm:, :] = jnp.where(i == nt - 1, 0.0, dn_ref[...])
        hcat[0:CONV_HALO, :] = jnp.where(i == 0, 0.0, hp_ref[...])
        hcat[CONV_HALO:, :] = hc_ref[...]
        dcur = dc_ref[...]
        dh0 = jnp.zeros((tm, D_MODEL), f32)
        for k in range(CONV_WIDTH):
            sh = hcat[pl.ds(CONV_HALO - (CONV_WIDTH - 1) + k, tm), :]
            wacc[k] += jnp.sum((dcur * sh).reshape(tm // 8, 8, D_MODEL), axis=0)
            dh0 = dh0 + dcat[pl.ds(CONV_WIDTH - 1 - k, tm), :] * w_ref[k:k + 1, :]
        av = jnp.concatenate([a0_ref[...], a1_ref[...]], axis=-1)
        sg = _sigmoid(jnp.concatenate([g0_ref[...], g1_ref[...]], axis=-1))
        dglu_ref[:, 0:D_MODEL] = (dh0 * sg).astype(bf16)
        dglu_ref[:, D_MODEL:2 * D_MODEL] = (dh0 * av * sg * (1.0 - sg)).astype(bf16)

        @pl.when(i == nt - 1)
        def _():
            for k in range(CONV_WIDTH):
                dw_ref[k:k + 1, :] = jnp.sum(wacc[k], axis=0, keepdims=True)
            dw_ref[CONV_WIDTH:CONV_WIDTH + 1, :] = jnp.zeros((1, D_MODEL), f32)

    row = pl.BlockSpec((tm, D_MODEL), lambda i: (i, 0))

    def col_spec(off):
        return pl.BlockSpec((tm, COL), lambda i: (i, a0 + off))

    return pl.pallas_call(
        body, name="conv_bwd", grid=(nt,),
        in_specs=[row, pl.BlockSpec((CONV_HALO, D_MODEL), lambda i: (jnp.minimum((i + 1) * per, nh - 1), 0)),
                  row, pl.BlockSpec((CONV_HALO, D_MODEL), lambda i: (jnp.maximum(i * per - 1, 0), 0)),
                  col_spec(0), col_spec(1), col_spec(2), col_spec(3),
                  pl.BlockSpec((CONV_WIDTH, D_MODEL), lambda i: (0, 0))],
        out_specs=[pl.BlockSpec((tm, 2 * D_MODEL), lambda i: (i, 0)), pl.BlockSpec((CONV_WIDTH + 1, D_MODEL), lambda i: (0, 0))],
        out_shape=[S((T, 2 * D_MODEL), bf16), S((CONV_WIDTH + 1, D_MODEL), f32)],
        scratch_shapes=[pltpu.VMEM((tm + CONV_HALO, D_MODEL), f32), pltpu.VMEM((tm + CONV_HALO, D_MODEL), f32),
                        pltpu.VMEM((CONV_WIDTH, 8, D_MODEL), f32)],
        compiler_params=_params("arbitrary"),
    )(dh1, dh1, h0, h0, proj, proj, proj, proj, w_dw)


def _attn_bwd(qn, kn, vb, o, do, lse, bias, sinks):
    T = qn.shape[0]
    nb = T // QBLOCK

    def body(q_ref, kc_ref, kp_ref, vc_ref, vp_ref, o_ref, do_ref, lse_ref, b_ref, s_ref,
             dq_ref, dk_ref, dv_ref, db_ref, ds_ref, kcar, vcar):
        n = pl.program_id(0)

        @pl.when(n == 0)
        def _():
            db_ref[...] = jnp.zeros_like(db_ref)
            ds_ref[...] = jnp.zeros_like(ds_ref)
            kcar[...] = jnp.zeros_like(kcar)
            vcar[...] = jnp.zeros_like(vcar)

        @pl.when(n < nb)
        def _():
            pm = _first_block_mask(n)
            qv = q_ref[...]
            dov = do_ref[...]
            dl = dov.astype(f32) * o_ref[...].astype(f32)
            kband = jnp.concatenate([kp_ref[...], kc_ref[...]], axis=0)
            vband = jnp.concatenate([vp_ref[...], vc_ref[...]], axis=0)
            for h in range(N_KV_HEADS):
                hs = slice(h * HEAD_DIM, (h + 1) * HEAD_DIM)
                kh, vh = kband[:, hs], vband[:, hs]

                def stack(v):
                    return jnp.concatenate([v[:, (h * GROUP + g) * HEAD_DIM:(h * GROUP + g + 1) * HEAD_DIM]
                                            for g in range(GROUP)], axis=0)

                q4, do4 = stack(qv), stack(dov)
                delta = jnp.sum(stack(dl), axis=-1, keepdims=True)
                lse4 = jnp.concatenate([lse_ref[:, h * GROUP + g:h * GROUP + g + 1] for g in range(GROUP)], axis=0)
                rows = slice(h * GROUP * QBLOCK, (h + 1) * GROUP * QBLOCK)
                s = _nt(q4, kh) + b_ref[rows, :] + pm
                p = jnp.exp(s - lse4)
                dp = _nt(do4, vh)
                ds = p * (dp - delta)
                db_ref[rows, :] += ds
                dsb = ds.astype(bf16)
                dq4 = jnp.dot(dsb, kh, preferred_element_type=f32)
                dk = _tn(dsb, q4)
                dv = _tn(p.astype(bf16), do4)
                for g in range(GROUP):
                    hq = h * GROUP + g
                    dq_ref[:, hq * HEAD_DIM:(hq + 1) * HEAD_DIM] = dq4[g * QBLOCK:(g + 1) * QBLOCK, :]
                    psink = jnp.exp(s_ref[0, hq] - lse4[g * QBLOCK:(g + 1) * QBLOCK, :])
                    ds_ref[:, hq:hq + 1] -= jnp.sum(psink * delta[g * QBLOCK:(g + 1) * QBLOCK, :], axis=0, keepdims=True)
                dk_ref[:, hs] = kcar[:, hs] + dk[0:QBLOCK, :]
                dv_ref[:, hs] = vcar[:, hs] + dv[0:QBLOCK, :]
                kcar[:, hs] = dk[QBLOCK:, :]
                vcar[:, hs] = dv[QBLOCK:, :]

        @pl.when(n == nb)
        def _():
            dk_ref[...] = kcar[...]
            dv_ref[...] = vcar[...]

    cur = lambda n: (jnp.minimum(n, nb - 1), 0)
    prev = lambda n: (jnp.clip(n - 1, 0, nb - 1), 0)
    qspec = pl.BlockSpec((QBLOCK, ATTN_WIDTH), cur)
    kcur, kprev = pl.BlockSpec((QBLOCK, KV_WIDTH), cur), pl.BlockSpec((QBLOCK, KV_WIDTH), prev)
    bspec = pl.BlockSpec((N_Q_HEADS * QBLOCK, 2 * QBLOCK), lambda n: (0, 0))
    return pl.pallas_call(
        body, name="attn_bwd", grid=(nb + 1,),
        in_specs=[qspec, kcur, kprev, kcur, kprev, qspec, qspec, pl.BlockSpec((QBLOCK, N_Q_HEADS), cur), bspec, SMEM],
        out_specs=[qspec, kprev, kprev, bspec, pl.BlockSpec((1, N_Q_HEADS), lambda n: (0, 0))],
        out_shape=[S((T, ATTN_WIDTH), f32), S((T, KV_WIDTH), f32), S((T, KV_WIDTH), f32),
                   S((N_Q_HEADS * QBLOCK, 2 * QBLOCK), f32), S((1, N_Q_HEADS), f32)],
        scratch_shapes=[pltpu.VMEM((QBLOCK, KV_WIDTH), f32), pltpu.VMEM((QBLOCK, KV_WIDTH), f32)],
        compiler_params=_params("arbitrary"),
    )(qn, kn, kn, vb, vb, o, do, lse, bias, sinks)


def _rel_bias_bwd(dbias, bucket):
    def body(d_ref, bk_ref, o_ref):
        b = bk_ref[...]
        for k in range(N_BUCKETS):
            mk = b == k
            for h in range(N_Q_HEADS):
                o_ref[k, h] = jnp.sum(jnp.where(mk, d_ref[h * QBLOCK:(h + 1) * QBLOCK, :], 0.0))

    return pl.pallas_call(body, name="rel_bias_bwd", out_shape=S((N_BUCKETS, N_Q_HEADS), f32), out_specs=SMEM)(dbias, bucket)


def _qk_norm_bwd(dq, dk, dv, proj, qg, kg, tm):
    T = dq.shape[0]
    scale = HEAD_DIM ** -0.5

    def head_bwd(dseg, seg, gv):
        r = lax.rsqrt(jnp.mean(seg * seg, axis=-1, keepdims=True) + EPS)
        xn = seg * r
        dg = jnp.sum(dseg * xn, axis=0, keepdims=True)
        dxn = dseg * gv
        return r * (dxn - xn * jnp.mean(dxn * xn, axis=-1, keepdims=True)), dg

    def body(dq_ref, dk_ref, dv_ref, p_ref, qg_ref, kg_ref, out_ref, dqg_ref, dkg_ref):
        @pl.when(pl.program_id(0) == 0)
        def _():
            dqg_ref[...] = jnp.zeros_like(dqg_ref)
            dkg_ref[...] = jnp.zeros_like(dkg_ref)

        qgv, kgv = qg_ref[...], kg_ref[...]
        dqg = jnp.zeros((1, HEAD_DIM), f32)
        for h in range(N_Q_HEADS):
            hs = slice(h * HEAD_DIM, (h + 1) * HEAD_DIM)
            dx, dg = head_bwd(dq_ref[:, hs] * scale, p_ref[:, hs], qgv)
            out_ref[:, hs] = dx.astype(bf16)
            dqg = dqg + dg
        dkg = jnp.zeros((1, HEAD_DIM), f32)
        for h in range(N_KV_HEADS):
            hs = slice(h * HEAD_DIM, (h + 1) * HEAD_DIM)
            ps = slice(Q_END + h * HEAD_DIM, Q_END + (h + 1) * HEAD_DIM)
            dx, dg = head_bwd(dk_ref[:, hs], p_ref[:, ps], kgv)
            out_ref[:, ps] = dx.astype(bf16)
            dkg = dkg + dg
        out_ref[:, K_END:V_END] = dv_ref[...].astype(bf16)
        dqg_ref[...] += dqg
        dkg_ref[...] += dkg

    vec = pl.BlockSpec((1, HEAD_DIM), lambda i: (0, 0))
    return pl.pallas_call(
        body, name="qk_norm_bwd", grid=(T // tm,),
        in_specs=[pl.BlockSpec((tm, ATTN_WIDTH), lambda i: (i, 0)), pl.BlockSpec((tm, KV_WIDTH), lambda i: (i, 0)),
                  pl.BlockSpec((tm, KV_WIDTH), lambda i: (i, 0)), pl.BlockSpec((tm, V_END), lambda i: (i, 0)), vec, vec],
        out_specs=[pl.BlockSpec((tm, V_END), lambda i: (i, 0)), vec, vec],
        out_shape=[S((T, V_END), bf16), S((1, HEAD_DIM), f32), S((1, HEAD_DIM), f32)],
        compiler_params=_params("arbitrary"),
    )(dq, dk, dv, proj, qg, kg)


def _in_bwd(dqkv, dglu, dgates, w_in, x, g, dx1, tm):
    T = x.shape[0]
    n0, n1, n2 = dqkv.shape[1] // COL, dglu.shape[1] // COL, dgates.shape[1] // COL
    nc = n0 + n1 + n2

    def body(a0_ref, a1_ref, a2_ref, w_ref, x_ref, g_ref, d_ref, gx_ref, dg_ref, acc):
        i, c = pl.program_id(0), pl.program_id(1)

        @pl.when(c == 0)
        def _():
            acc[...] = jnp.zeros_like(acc)

        @pl.when((i == 0) & (c == 0))
        def _():
            dg_ref[...] = jnp.zeros_like(dg_ref)

        @pl.when(c < n0)
        def _():
            acc[...] += _nt(a0_ref[...], w_ref[...])

        @pl.when((c >= n0) & (c < n0 + n1))
        def _():
            acc[...] += _nt(a1_ref[...], w_ref[...])

        @pl.when(c >= n0 + n1)
        def _():
            acc[...] += _nt(a2_ref[...], w_ref[...])

        @pl.when(c == nc - 1)
        def _():
            dx, dg = _rms_bwd(acc[...], x_ref[...], g_ref[...])
            gx_ref[...] = d_ref[...] + dx
            dg_ref[...] += dg

    row = pl.BlockSpec((tm, D_MODEL), lambda i, c: (i, 0))
    vec = pl.BlockSpec((1, D_MODEL), lambda i, c: (0, 0))
    return pl.pallas_call(
        body, name="in_bwd", grid=(T // tm, nc),
        in_specs=[pl.BlockSpec((tm, COL), lambda i, c: (i, jnp.clip(c, 0, n0 - 1))),
                  pl.BlockSpec((tm, COL), lambda i, c: (i, jnp.clip(c - n0, 0, n1 - 1))),
                  pl.BlockSpec((tm, COL), lambda i, c: (i, jnp.clip(c - n0 - n1, 0, n2 - 1))),
                  pl.BlockSpec((D_MODEL, COL), lambda i, c: (0, c)), row, vec, row],
        out_specs=[row, vec],
        out_shape=[S((T, D_MODEL), f32), S((1, D_MODEL), f32)],
        scratch_shapes=[pltpu.VMEM((tm, D_MODEL), f32)],
        compiler_params=_params("arbitrary", "arbitrary"),
    )(dqkv, dglu, dgates, w_in, x, g, dx1)


def _adamw(name, parts, w, m, v, tr):
    R, C = w.shape
    bc1 = 1.0 - ADAM_B1 ** ADAM_STEP
    bc2 = 1.0 - ADAM_B2 ** ADAM_STEP

    def body(p_ref, w_ref, m_ref, v_ref, g_ref, d_ref, nm_ref, nv_ref):
        g = p_ref[0]
        for k in range(1, N_DEV):
            g = g + p_ref[k]
        nm = ADAM_B1 * m_ref[...] + (1.0 - ADAM_B1) * g
        nv = ADAM_B2 * v_ref[...] + (1.0 - ADAM_B2) * (g * g)
        g_ref[...] = g
        nm_ref[...] = nm
        nv_ref[...] = nv
        d_ref[...] = -ADAM_LR * ((nm / bc1) / (jnp.sqrt(nv / bc2) + ADAM_EPS) + ADAM_WD * w_ref[...])

    blk = pl.BlockSpec((tr, C), lambda i: (i, 0))
    return pl.pallas_call(
        body, name=name, grid=(R // tr,),
        in_specs=[pl.BlockSpec((N_DEV, tr, C), lambda i: (0, i, 0)), blk, blk, blk],
        out_specs=[blk, blk, blk, blk], out_shape=[S((R, C), f32)] * 4,
        compiler_params=_params("parallel"),
    )(parts, w, m, v)


def _tile(T, pref):
    return min(T, pref)


def _pad_rows(a, rows):
    return jnp.pad(a, ((0, rows - a.shape[0]), (0, 0)))


def kernel(x, norm_mix_g, w_in, q_norm_g, k_norm_g, attn_sinks, rel_bias, w_attn_o, w_dw, b_dw, conv_ln_g, conv_ln_b, w_conv_out, w_out, norm_mlp_g, w_ff1, w_ff2, loss_target, m_norm_mix_g, m_w_in, m_q_norm_g, m_k_norm_g, m_attn_sinks, m_rel_bias, m_w_attn_o, m_w_dw, m_b_dw, m_conv_ln_g, m_conv_ln_b, m_w_conv_out, m_w_out, m_norm_mlp_g, m_w_ff1, m_w_ff2, v_norm_mix_g, v_w_in, v_q_norm_g, v_k_norm_g, v_attn_sinks, v_rel_bias, v_w_attn_o, v_w_dw, v_b_dw, v_conv_ln_g, v_conv_ln_b, v_w_conv_out, v_w_out, v_norm_mlp_g, v_w_ff1, v_w_ff2):
    T = x.shape[1]
    xs = x[0]
    tgt = loss_target[0]
    in_shard = IN_WIDTH // N_DEV
    dw_rows = CONV_WIDTH + 1
    ch_shard = D_MODEL // N_DEV
    tm = _tile(T, 512)
    tc = _tile(T, 256)
    bucket = jnp.asarray(_t5_bucket_table())

    shards = [w_in[0].astype(bf16), w_attn_o[0].astype(bf16), w_conv_out[0].astype(bf16), w_out[0].astype(bf16),
              w_ff1[0].astype(bf16), w_ff2[0].astype(bf16), _pad_rows(w_dw[0], dw_rows)]
    g_in, g_ao, g_co, g_o, g_f1, g_f2, g_dw = _exchange("gather_weights", shards, gather=True)
    W_in = jnp.transpose(g_in, (1, 0, 2)).reshape(D_MODEL, IN_WIDTH)
    W_ao = g_ao.reshape(D_MODEL, D_MODEL)
    W_co = g_co.reshape(D_MODEL, D_MODEL)
    W_o = g_o.reshape(D_MODEL, D_MODEL)
    W_dw = jnp.transpose(g_dw, (1, 0, 2)).reshape(dw_rows, D_MODEL)[:CONV_WIDTH]

    proj, u = _proj_fwd(xs, norm_mix_g, W_in, tm, COL)
    qn, kn, vb, h0 = _prep_fwd(proj, q_norm_g, k_norm_g, tc)
    bias = _bias_table(rel_bias, bucket)
    o, lse = _attn_fwd(qn, kn, vb, bias, attn_sinks)
    h1, h3 = _conv_fwd(h0, W_dw, b_dw, conv_ln_g, conv_ln_b, tc)
    x1, attn, conv, merged = _mix_fwd(xs, o, h3, proj, W_ao, W_co, W_o, tc)
    a, u2, dy, loss_parts = _ffn_fwd(x1, norm_mlp_g, g_f1, g_f2, tgt, tm)
    loss = lax.psum(jnp.sum(loss_parts[:, 0, 0]), ("x", "y", "c"))

    da, dx1, d_norm_mlp_g = _ffn_bwd(dy, a, x1, norm_mlp_g, g_f1, g_f2, tm)
    gw_f2 = _wgrad("wgrad_ff2", a, dy, FF_CHUNK, D_MODEL, tm, relu2=True).reshape(N_DEV, FF_CHUNK, D_MODEL)
    gw_f1 = _wgrad("wgrad_ff1", u2, da, D_MODEL, FF_CHUNK, tm, col_shard=True)
    dattn, dconv, do, dh3, dgates = _mix_bwd(dx1, proj, attn, conv, W_ao, W_co, W_o, tc)
    gw_o = _wgrad("wgrad_out", merged, dx1, D_MODEL, D_MODEL, tm).reshape(N_DEV, ch_shard, D_MODEL)
    gw_ao = _wgrad("wgrad_attn_o", o, dattn, D_MODEL, D_MODEL, tm).reshape(N_DEV, ch_shard, D_MODEL)
    gw_co = _wgrad("wgrad_conv_out", h3, dconv, D_MODEL, D_MODEL, tm).reshape(N_DEV, ch_shard, D_MODEL)
    dh1, d_ln_g, d_ln_b, d_b_dw = _norm_act_bwd(dh3, h1, conv_ln_g, conv_ln_b, tc)
    dglu, d_w_dw = _conv_bwd(dh1, h0, proj, W_dw, tc)
    dq, dk, dv, dbias, d_sinks = _attn_bwd(qn, kn, vb, o, do, lse, bias, attn_sinks)
    d_rel_bias = _rel_bias_bwd(dbias, bucket)
    dqkv, d_qg, d_kg = _qk_norm_bwd(dq, dk, dv, proj, q_norm_g, k_norm_g, tc)
    grad_x, d_norm_mix_g = _in_bwd(dqkv, dglu, dgates, W_in, xs, norm_mix_g, dx1, tm)
    gw_in = jnp.concatenate([_wgrad("wgrad_in_qkv", u, dqkv, D_MODEL, COL, tm),
                             _wgrad("wgrad_in_glu", u, dglu, D_MODEL, COL, tm),
                             _wgrad("wgrad_in_gates", u, dgates, D_MODEL, COL, tm)], axis=1)
    gw_in = jnp.transpose(gw_in.reshape(D_MODEL, N_DEV, in_shard), (1, 0, 2))
    gw_dw = jnp.transpose(d_w_dw.reshape(dw_rows, N_DEV, ch_shard), (1, 0, 2))

    def row(vec):
        flat = vec.reshape(1, -1)
        return jnp.pad(flat, ((0, 0), (0, D_MODEL - flat.shape[1])))

    def pack_small(nm, qg, kg, sk, rb, bd, lg, lb, nl):
        tail = jnp.concatenate([qg.reshape(1, -1), kg.reshape(1, -1), sk.reshape(1, -1), rb.reshape(1, -1)], axis=1)
        return jnp.concatenate([row(nm), row(bd), row(lg), row(lb), row(nl), row(tail), jnp.zeros((2, D_MODEL), f32)], axis=0)

    def unpack_small(p):
        t = p[5]
        o0, o1, o2 = HEAD_DIM, 2 * HEAD_DIM, 2 * HEAD_DIM + N_Q_HEADS
        return dict(norm_mix_g=p[0:1], b_dw=p[1:2], conv_ln_g=p[2:3], conv_ln_b=p[3:4], norm_mlp_g=p[4:5],
                    q_norm_g=t[0:o0].reshape(1, HEAD_DIM), k_norm_g=t[o0:o1].reshape(1, HEAD_DIM),
                    attn_sinks=t[o1:o2].reshape(1, N_Q_HEADS),
                    rel_bias=t[o2:o2 + N_BUCKETS * N_Q_HEADS].reshape(N_BUCKETS, N_Q_HEADS))

    small_g = pack_small(d_norm_mix_g, d_qg, d_kg, d_sinks, d_rel_bias, d_b_dw, d_ln_g, d_ln_b, d_norm_mlp_g)
    small_send = jnp.broadcast_to(small_g[None], (N_DEV,) + small_g.shape)

    lands = _exchange("exchange_grads", [gw_in, gw_ao, gw_co, gw_o, gw_f1, gw_f2, gw_dw, small_send], gather=False)
    l_in, l_ao, l_co, l_o, l_f1, l_f2, l_dw, l_small = lands

    res = {}
    res["w_in"] = _adamw("adamw_in", l_in, w_in[0], m_w_in[0], v_w_in[0], 256)
    res["w_attn_o"] = _adamw("adamw_attn_o", l_ao, w_attn_o[0], m_w_attn_o[0], v_w_attn_o[0], ch_shard)
    res["w_conv_out"] = _adamw("adamw_conv_out", l_co, w_conv_out[0], m_w_conv_out[0], v_w_conv_out[0], ch_shard)
    res["w_out"] = _adamw("adamw_out", l_o, w_out[0], m_w_out[0], v_w_out[0], ch_shard)
    res["w_ff1"] = _adamw("adamw_ff1", l_f1, w_ff1[0], m_w_ff1[0], v_w_ff1[0], 256)
    res["w_ff2"] = _adamw("adamw_ff2", l_f2, w_ff2[0], m_w_ff2[0], v_w_ff2[0], 256)
    dw4 = _adamw("adamw_dw", l_dw, _pad_rows(w_dw[0], dw_rows), _pad_rows(m_w_dw[0], dw_rows), _pad_rows(v_w_dw[0], dw_rows), dw_rows)
    res["w_dw"] = [t[:CONV_WIDTH] for t in dw4]
    small_w = pack_small(norm_mix_g, q_norm_g, k_norm_g, attn_sinks, rel_bias, b_dw, conv_ln_g, conv_ln_b, norm_mlp_g)
    small_m = pack_small(m_norm_mix_g, m_q_norm_g, m_k_norm_g, m_attn_sinks, m_rel_bias, m_b_dw, m_conv_ln_g, m_conv_ln_b, m_norm_mlp_g)
    small_v = pack_small(v_norm_mix_g, v_q_norm_g, v_k_norm_g, v_attn_sinks, v_rel_bias, v_b_dw, v_conv_ln_g, v_conv_ln_b, v_norm_mlp_g)
    small4 = [unpack_small(t) for t in _adamw("adamw_small", l_small, small_w, small_m, small_v, 8)]

    order = ["norm_mix_g", "w_in", "q_norm_g", "k_norm_g", "attn_sinks", "rel_bias", "w_attn_o", "w_dw", "b_dw",
             "conv_ln_g", "conv_ln_b", "w_conv_out", "w_out", "norm_mlp_g", "w_ff1", "w_ff2"]
    stacked = {"w_in", "w_attn_o", "w_dw", "w_conv_out", "w_out", "w_ff1", "w_ff2"}
    outs = [loss, grad_x[None]]
    for k in range(4):
        for nme in order:
            if nme in stacked:
                outs.append(res[nme][k][None])
            else:
                outs.append(small4[k][nme])
    return tuple(outs)
```

```python
import functools

import numpy as np
import jax
import jax.numpy as jnp
from jax import lax
from jax.experimental import pallas as pl
from jax.experimental.pallas import tpu as pltpu

f32 = jnp.float32
bf16 = jnp.bfloat16
S = jax.ShapeDtypeStruct

N_DEV = 8
D_MODEL = 1024
HEAD_DIM = 64
N_Q_HEADS = 16
N_KV_HEADS = 4
GROUP = N_Q_HEADS // N_KV_HEADS
ATTN_WIDTH = N_Q_HEADS * HEAD_DIM
KV_WIDTH = N_KV_HEADS * HEAD_DIM
QBLOCK = 128
CONV_WIDTH = 31
CONV_HALO = 32
D_FF = 4 * D_MODEL
N_BUCKETS = 32
MAX_DISTANCE = 128
EPS = 1e-6
NEG = -1e30
Q_END = ATTN_WIDTH
K_END = Q_END + KV_WIDTH
V_END = K_END + KV_WIDTH
GLU_END = V_END + 2 * D_MODEL
IN_WIDTH = GLU_END + 2 * D_MODEL
COL = 512
FF_CHUNK = D_FF // N_DEV

ADAM_LR = 0.001
ADAM_B1 = 0.9
ADAM_B2 = 0.999
ADAM_EPS = 1e-08
ADAM_WD = 0.01
ADAM_STEP = 10

VMEM_LIMIT = 56 * 1024 * 1024

MESH_ID = pl.DeviceIdType.MESH
ANY = pl.BlockSpec(memory_space=pl.ANY)
SMEM = pl.BlockSpec(memory_space=pltpu.SMEM)


def _params(*sem):
    return pltpu.CompilerParams(dimension_semantics=sem, vmem_limit_bytes=VMEM_LIMIT)


def _nt(a, b):
    return lax.dot_general(a, b, (((1,), (1,)), ((), ())), preferred_element_type=f32)


def _tn(a, b):
    return lax.dot_general(a, b, (((0,), (0,)), ((), ())), preferred_element_type=f32)


def _sigmoid(z):
    return 1.0 / (1.0 + jnp.exp(-z))


def _t5_bucket_table():
    qi = np.arange(QBLOCK, dtype=np.int32)[:, None]
    kj = np.arange(2 * QBLOCK, dtype=np.int32)[None, :]
    dist = qi + QBLOCK - kj
    n = np.maximum(dist, 0)
    max_exact = N_BUCKETS // 2
    nf = np.maximum(n, 1).astype(np.float32)
    large = max_exact + (np.log(nf / np.float32(max_exact)) / np.float32(np.log(MAX_DISTANCE / max_exact))
                         * np.float32(N_BUCKETS - max_exact)).astype(np.int32)
    large = np.minimum(large, N_BUCKETS - 1)
    bucket = np.where(n < max_exact, n, large)
    valid = (dist >= 0) & (dist < QBLOCK)
    return np.where(valid, bucket, -1).astype(np.int32)


def _peer(d):
    x, y, c = lax.axis_index("x"), lax.axis_index("y"), lax.axis_index("c")
    dx, dy, dc = (d >> 2) & 1, (d >> 1) & 1, d & 1
    px, py, pc = x ^ dx, y ^ dy, c ^ dc
    return (px, py, pc), 4 * px + 2 * py + pc


class _Exchange:
    def __init__(self, arrays, gather):
        self.arrays, self.gather, self.n = list(arrays), gather, len(arrays)
        self.out_shape = [S(((N_DEV,) + a.shape) if gather else a.shape, a.dtype) for a in self.arrays]
        self.scratch = [pltpu.SemaphoreType.DMA((self.n, N_DEV - 1)), pltpu.SemaphoreType.DMA((self.n, N_DEV - 1)),
                        pltpu.SemaphoreType.DMA((self.n,))]

    def _copies(self, ins, outs, sems):
        send_sems, recv_sems, local_sems = sems
        _, me = _peer(0)
        local, sends, recvs = [], [], []
        for k in range(self.n):
            src = ins[k] if self.gather else ins[k].at[me]
            local.append(pltpu.make_async_copy(src, outs[k].at[me], local_sems.at[k]))
        for d in range(1, N_DEV):
            peer, pidx = _peer(d)
            for k in range(self.n):
                src = ins[k] if self.gather else ins[k].at[pidx]
                common = dict(src_ref=src, send_sem=send_sems.at[k, d - 1], recv_sem=recv_sems.at[k, d - 1],
                              device_id=peer, device_id_type=MESH_ID)
                sends.append(pltpu.make_async_remote_copy(dst_ref=outs[k].at[me], **common))
                recvs.append(pltpu.make_async_remote_copy(dst_ref=outs[k].at[pidx], **common))
        return local, sends, recvs

    def start(self, ins, outs, sems):
        local, sends, _ = self._copies(ins, outs, sems)
        for cp in local + sends:
            cp.start()

    def wait(self, ins, outs, sems):
        local, sends, recvs = self._copies(ins, outs, sems)
        for cp in recvs:
            cp.wait_recv()
        for cp in sends:
            cp.wait_send()
        for cp in local:
            cp.wait()


def _exchange(name, arrays, gather):
    ex = _Exchange(arrays, gather)
    n = ex.n

    def body(*refs):
        ins, outs, sems = refs[:n], refs[n:2 * n], refs[2 * n:]
        ex.start(ins, outs, sems)
        ex.wait(ins, outs, sems)

    return pl.pallas_call(body, name=name, out_shape=ex.out_shape, in_specs=[ANY] * n, out_specs=[ANY] * n,
                          scratch_shapes=ex.scratch)(*arrays)


def _call(body, *, name, grid, in_specs, out_specs, out_shape, args, scratch_shapes=(), ride=None):
    n_in, n_out, n_sc = len(in_specs), len(out_specs), len(scratch_shapes)
    sem = ("arbitrary",) * len(grid)
    if ride is None:
        res = pl.pallas_call(body, name=name, grid=grid, in_specs=list(in_specs), out_specs=list(out_specs),
                             out_shape=list(out_shape), scratch_shapes=list(scratch_shapes), compiler_params=_params(*sem))(*args)
        return list(res), []
    nx = ride.n

    def riding(*refs):
        ins, xin = refs[:n_in], refs[n_in:n_in + nx]
        outs, xout = refs[n_in + nx:n_in + nx + n_out], refs[n_in + nx + n_out:n_in + 2 * nx + n_out]
        rest = refs[n_in + 2 * nx + n_out:]
        scratch, sems = rest[:n_sc], rest[n_sc:]
        ids = [pl.program_id(ax) for ax in range(len(grid))]
        first = functools.reduce(jnp.logical_and, [i == 0 for i in ids])
        last = functools.reduce(jnp.logical_and, [i == g - 1 for i, g in zip(ids, grid)])

        @pl.when(first)
        def _():
            ride.start(xin, xout, sems)

        body(*ins, *outs, *scratch)

        @pl.when(last)
        def _():
            ride.wait(xin, xout, sems)

    res = pl.pallas_call(
        riding, name=name, grid=grid, in_specs=list(in_specs) + [ANY] * nx, out_specs=list(out_specs) + [ANY] * nx,
        out_shape=list(out_shape) + ride.out_shape, scratch_shapes=list(scratch_shapes) + ride.scratch,
        compiler_params=_params(*sem))(*args, *ride.arrays)
    return list(res[:n_out]), list(res[n_out:])


def _proj_fwd(x, g, w, tm, tn, ride=None):
    T, K = x.shape
    N = w.shape[1]

    def body(x_ref, g_ref, w_ref, o_ref, u_ref):
        @pl.when(pl.program_id(1) == 0)
        def _():
            xv = x_ref[...]
            r = lax.rsqrt(jnp.mean(xv * xv, axis=-1, keepdims=True) + EPS)
            u_ref[...] = (xv * r * g_ref[...]).astype(bf16)

        o_ref[...] = jnp.dot(u_ref[...], w_ref[...], preferred_element_type=f32)

    return _call(
        body, name="proj_fwd", grid=(T // tm, N // tn),
        in_specs=[pl.BlockSpec((tm, K), lambda i, j: (i, 0)), pl.BlockSpec((1, K), lambda i, j: (0, 0)),
                  pl.BlockSpec((K, tn), lambda i, j: (0, j))],
        out_specs=[pl.BlockSpec((tm, tn), lambda i, j: (i, j)), pl.BlockSpec((tm, K), lambda i, j: (i, 0))],
        out_shape=[S((T, N), f32), S((T, K), bf16)], args=(x, g, w), ride=ride)


def _prep_fwd(proj, qg, kg, tm):
    T = proj.shape[0]

    def body(p_ref, qg_ref, kg_ref, qn_ref, kn_ref, vb_ref, h0_ref):
        qgv = qg_ref[...] * (HEAD_DIM ** -0.5)
        kgv = kg_ref[...]
        for h in range(N_Q_HEADS):
            seg = p_ref[:, h * HEAD_DIM:(h + 1) * HEAD_DIM]
            r = lax.rsqrt(jnp.mean(seg * seg, axis=-1, keepdims=True) + EPS)
            qn_ref[:, h * HEAD_DIM:(h + 1) * HEAD_DIM] = (seg * r * qgv).astype(bf16)
        for h in range(N_KV_HEADS):
            seg = p_ref[:, Q_END + h * HEAD_DIM:Q_END + (h + 1) * HEAD_DIM]
            r = lax.rsqrt(jnp.mean(seg * seg, axis=-1, keepdims=True) + EPS)
            kn_ref[:, h * HEAD_DIM:(h + 1) * HEAD_DIM] = (seg * r * kgv).astype(bf16)
        vb_ref[...] = p_ref[:, K_END:V_END].astype(bf16)
        h0_ref[...] = p_ref[:, V_END:V_END + D_MODEL] * _sigmoid(p_ref[:, V_END + D_MODEL:GLU_END])

    return pl.pallas_call(
        body, name="prep_fwd", grid=(T // tm,),
        in_specs=[pl.BlockSpec((tm, GLU_END), lambda i: (i, 0)), pl.BlockSpec((1, HEAD_DIM), lambda i: (0, 0)),
                  pl.BlockSpec((1, HEAD_DIM), lambda i: (0, 0))],
        out_specs=[pl.BlockSpec((tm, ATTN_WIDTH), lambda i: (i, 0)), pl.BlockSpec((tm, KV_WIDTH), lambda i: (i, 0)),
                   pl.BlockSpec((tm, KV_WIDTH), lambda i: (i, 0)), pl.BlockSpec((tm, D_MODEL), lambda i: (i, 0))],
        out_shape=[S((T, ATTN_WIDTH), bf16), S((T, KV_WIDTH), bf16), S((T, KV_WIDTH), bf16), S((T, D_MODEL), f32)],
        compiler_params=_params("parallel"),
    )(proj, qg, kg)


def _bias_table(rel_bias, bucket):
    def body(rb_ref, bk_ref, o_ref):
        b = bk_ref[...]
        for h in range(N_Q_HEADS):
            acc = jnp.full((QBLOCK, 2 * QBLOCK), NEG, f32)
            for k in range(N_BUCKETS):
                acc = jnp.where(b == k, rb_ref[k, h], acc)
            o_ref[h * QBLOCK:(h + 1) * QBLOCK, :] = acc

    return pl.pallas_call(
        body, name="bias_table", out_shape=S((N_Q_HEADS * QBLOCK, 2 * QBLOCK), f32),
        in_specs=[SMEM, pl.BlockSpec(memory_space=pltpu.VMEM)],
    )(rel_bias, bucket)


def _first_block_mask(n):
    kj = lax.broadcasted_iota(jnp.int32, (1, 2 * QBLOCK), 1)
    return jnp.where((kj < QBLOCK) & (n == 0), NEG, 0.0).astype(f32)


def _attn_fwd(qn, kn, vb, bias, sinks, ride=None):
    T = qn.shape[0]
    nb = T // QBLOCK

    def body(q_ref, kc_ref, kp_ref, vc_ref, vp_ref, b_ref, s_ref, o_ref, lse_ref):
        n = pl.program_id(0)
        pm = _first_block_mask(n)
        qv = q_ref[...]
        kband = jnp.concatenate([kp_ref[...], kc_ref[...]], axis=0)
        vband = jnp.concatenate([vp_ref[...], vc_ref[...]], axis=0)
        for h in range(N_KV_HEADS):
            kh = kband[:, h * HEAD_DIM:(h + 1) * HEAD_DIM]
            vh = vband[:, h * HEAD_DIM:(h + 1) * HEAD_DIM]
            q4 = jnp.concatenate([qv[:, (h * GROUP + g) * HEAD_DIM:(h * GROUP + g + 1) * HEAD_DIM] for g in range(GROUP)], axis=0)
            s = _nt(q4, kh) + b_ref[h * GROUP * QBLOCK:(h + 1) * GROUP * QBLOCK, :] + pm
            sink = jnp.concatenate([jnp.full((QBLOCK, 1), s_ref[0, h * GROUP + g], f32) for g in range(GROUP)], axis=0)
            m = jnp.maximum(jnp.max(s, axis=-1, keepdims=True), sink)
            p = jnp.exp(s - m)
            l = jnp.sum(p, axis=-1, keepdims=True) + jnp.exp(sink - m)
            o4 = jnp.dot((p * (1.0 / l)).astype(bf16), vh, preferred_element_type=f32)
            lse4 = m + jnp.log(l)
            for g in range(GROUP):
                hq = h * GROUP + g
                o_ref[:, hq * HEAD_DIM:(hq + 1) * HEAD_DIM] = o4[g * QBLOCK:(g + 1) * QBLOCK, :].astype(bf16)
                lse_ref[:, hq:hq + 1] = lse4[g * QBLOCK:(g + 1) * QBLOCK, :]

    cur = lambda n: (n, 0)
    prev = lambda n: (jnp.maximum(n - 1, 0), 0)
    return _call(
        body, name="attn_fwd", grid=(nb,),
        in_specs=[pl.BlockSpec((QBLOCK, ATTN_WIDTH), cur), pl.BlockSpec((QBLOCK, KV_WIDTH), cur),
                  pl.BlockSpec((QBLOCK, KV_WIDTH), prev), pl.BlockSpec((QBLOCK, KV_WIDTH), cur),
                  pl.BlockSpec((QBLOCK, KV_WIDTH), prev),
                  pl.BlockSpec((N_Q_HEADS * QBLOCK, 2 * QBLOCK), lambda n: (0, 0)), SMEM],
        out_specs=[pl.BlockSpec((QBLOCK, ATTN_WIDTH), cur), pl.BlockSpec((QBLOCK, N_Q_HEADS), cur)],
        out_shape=[S((T, ATTN_WIDTH), bf16), S((T, N_Q_HEADS), f32)],
        args=(qn, kn, kn, vb, vb, bias, sinks), ride=ride)


def _layer_norm_stats(h1):
    mu = jnp.mean(h1, axis=-1, keepdims=True)
    xc = h1 - mu
    rstd = lax.rsqrt(jnp.mean(xc * xc, axis=-1, keepdims=True) + EPS)
    return xc * rstd, rstd


def _conv_fwd(h0, w_dw, b_dw, ln_g, ln_b, tm):
    T = h0.shape[0]
    per = tm // CONV_HALO

    def body(hc_ref, hp_ref, w_ref, b_ref, g_ref, bb_ref, h1_ref, h3_ref, cat):
        i = pl.program_id(0)
        cat[0:CONV_HALO, :] = jnp.where(i == 0, 0.0, hp_ref[...])
        cat[CONV_HALO:, :] = hc_ref[...]
        acc = jnp.zeros((tm, D_MODEL), f32) + b_ref[...]
        for k in range(CONV_WIDTH):
            acc = acc + cat[pl.ds(CONV_HALO - (CONV_WIDTH - 1) + k, tm), :] * w_ref[k:k + 1, :]
        h1_ref[...] = acc
        xhat, _ = _layer_norm_stats(acc)
        h2 = xhat * g_ref[...] + bb_ref[...]
        h3_ref[...] = (h2 * _sigmoid(h2)).astype(bf16)

    vec = pl.BlockSpec((1, D_MODEL), lambda i: (0, 0))
    return pl.pallas_call(
        body, name="conv_fwd", grid=(T // tm,),
        in_specs=[pl.BlockSpec((tm, D_MODEL), lambda i: (i, 0)),
                  pl.BlockSpec((CONV_HALO, D_MODEL), lambda i: (jnp.maximum(i * per - 1, 0), 0)),
                  pl.BlockSpec((CONV_WIDTH, D_MODEL), lambda i: (0, 0)), vec, vec, vec],
        out_specs=[pl.BlockSpec((tm, D_MODEL), lambda i: (i, 0)), pl.BlockSpec((tm, D_MODEL), lambda i: (i, 0))],
        out_shape=[S((T, D_MODEL), f32), S((T, D_MODEL), bf16)],
        scratch_shapes=[pltpu.VMEM((tm + CONV_HALO, D_MODEL), f32)],
        compiler_params=_params("parallel"),
    )(h0, h0, w_dw, b_dw, ln_g, ln_b)


def _mix_fwd(x, o, h3, proj, w_ao, w_co, w_o, tm):
    T = x.shape[0]
    row = pl.BlockSpec((tm, D_MODEL), lambda i: (i, 0))
    wsp = pl.BlockSpec((D_MODEL, D_MODEL), lambda i: (0, 0))
    g0 = GLU_END // COL

    def gate_spec(off):
        return pl.BlockSpec((tm, COL), lambda i: (i, g0 + off))

    def body(x_ref, o_ref, h3_ref, ga0, ga1, gc0, gc1, wa_ref, wc_ref, wo_ref, x1_ref, at_ref, cv_ref, mg_ref):
        attn = jnp.dot(o_ref[...], wa_ref[...], preferred_element_type=f32)
        conv = jnp.dot(h3_ref[...], wc_ref[...], preferred_element_type=f32)
        ga = jnp.concatenate([ga0[...], ga1[...]], axis=-1)
        gc = jnp.concatenate([gc0[...], gc1[...]], axis=-1)
        merged = (_sigmoid(ga) * attn + _sigmoid(gc) * conv).astype(bf16)
        at_ref[...] = attn.astype(bf16)
        cv_ref[...] = conv.astype(bf16)
        mg_ref[...] = merged
        x1_ref[...] = x_ref[...] + jnp.dot(merged, wo_ref[...], preferred_element_type=f32)

    return pl.pallas_call(
        body, name="mix_fwd", grid=(T // tm,),
        in_specs=[row, row, row, gate_spec(0), gate_spec(1), gate_spec(2), gate_spec(3), wsp, wsp, wsp],
        out_specs=[row, row, row, row],
        out_shape=[S((T, D_MODEL), f32), S((T, D_MODEL), bf16), S((T, D_MODEL), bf16), S((T, D_MODEL), bf16)],
        compiler_params=_params("parallel"),
    )(x, o, h3, proj, proj, proj, proj, w_ao, w_co, w_o)


def _ffn_fwd(x1, g, w1, w2, target, tm):
    T = x1.shape[0]
    nj = w1.shape[0]

    def body(x_ref, g_ref, w1_ref, w2_ref, t_ref, a_ref, u_ref, dy_ref, ls_ref, acc):
        j = pl.program_id(1)

        @pl.when(j == 0)
        def _():
            xv = x_ref[...]
            r = lax.rsqrt(jnp.mean(xv * xv, axis=-1, keepdims=True) + EPS)
            u_ref[...] = (xv * r * g_ref[...]).astype(bf16)
            acc[...] = jnp.zeros_like(acc)

        a = jnp.dot(u_ref[...], w1_ref[...], preferred_element_type=f32)
        a_ref[...] = a.astype(bf16)
        hm = jnp.square(jnp.maximum(a, 0.0)).astype(bf16)
        acc[...] += jnp.dot(hm, w2_ref[...], preferred_element_type=f32)

        @pl.when(j == nj - 1)
        def _():
            err = x_ref[...] + acc[...] - t_ref[...]
            dy_ref[...] = err * (1.0 / D_MODEL)
            ls_ref[...] = jnp.zeros((8, 128), f32) + jnp.sum(err * err) * (0.5 / D_MODEL)

    row = pl.BlockSpec((tm, D_MODEL), lambda i, j: (i, 0))
    return pl.pallas_call(
        body, name="ffn_fwd", grid=(T // tm, nj),
        in_specs=[row, pl.BlockSpec((1, D_MODEL), lambda i, j: (0, 0)),
                  pl.BlockSpec((None, D_MODEL, FF_CHUNK), lambda i, j: (j, 0, 0)),
                  pl.BlockSpec((None, FF_CHUNK, D_MODEL), lambda i, j: (j, 0, 0)), row],
        out_specs=[pl.BlockSpec((tm, FF_CHUNK), lambda i, j: (i, j)), row, row,
                   pl.BlockSpec((None, 8, 128), lambda i, j: (i, 0, 0))],
        out_shape=[S((T, D_FF), bf16), S((T, D_MODEL), bf16), S((T, D_MODEL), f32), S((T // tm, 8, 128), f32)],
        scratch_shapes=[pltpu.VMEM((tm, D_MODEL), f32)],
        compiler_params=_params("parallel", "arbitrary"),
    )(x1, g, w1, w2, target)


def _rms_bwd(du, xv, gv):
    r = lax.rsqrt(jnp.mean(xv * xv, axis=-1, keepdims=True) + EPS)
    xn = xv * r
    dg = jnp.sum(du * xn, axis=0, keepdims=True)
    dxn = du * gv
    dx = r * (dxn - xn * jnp.mean(dxn * xn, axis=-1, keepdims=True))
    return dx, dg


def _ffn_bwd(dy, a, x1, g, w1, w2, tm, ride=None):
    T = dy.shape[0]
    nj = w1.shape[0]

    def body(dy_ref, a_ref, x_ref, g_ref, w1_ref, w2_ref, da_ref, dx_ref, dg_ref, acc, dyb):
        i, j = pl.program_id(0), pl.program_id(1)

        @pl.when(j == 0)
        def _():
            dyb[...] = dy_ref[...].astype(bf16)
            acc[...] = jnp.zeros_like(acc)

        @pl.when((i == 0) & (j == 0))
        def _():
            dg_ref[...] = jnp.zeros_like(dg_ref)

        dh = _nt(dyb[...], w2_ref[...])
        da = (dh * (2.0 * jnp.maximum(a_ref[...].astype(f32), 0.0))).astype(bf16)
        da_ref[...] = da
        acc[...] += _nt(da, w1_ref[...])

        @pl.when(j == nj - 1)
        def _():
            dx, dg = _rms_bwd(acc[...], x_ref[...], g_ref[...])
            dx_ref[...] = dy_ref[...] + dx
            dg_ref[...] += dg

    row = pl.BlockSpec((tm, D_MODEL), lambda i, j: (i, 0))
    vec = pl.BlockSpec((1, D_MODEL), lambda i, j: (0, 0))
    return _call(
        body, name="ffn_bwd", grid=(T // tm, nj),
        in_specs=[row, pl.BlockSpec((tm, FF_CHUNK), lambda i, j: (i, j)), row, vec,
                  pl.BlockSpec((None, D_MODEL, FF_CHUNK), lambda i, j: (j, 0, 0)),
                  pl.BlockSpec((None, FF_CHUNK, D_MODEL), lambda i, j: (j, 0, 0))],
        out_specs=[pl.BlockSpec((tm, FF_CHUNK), lambda i, j: (i, j)), row, vec],
        out_shape=[S((T, D_FF), bf16), S((T, D_MODEL), f32), S((1, D_MODEL), f32)],
        scratch_shapes=[pltpu.VMEM((tm, D_MODEL), f32), pltpu.VMEM((tm, D_MODEL), bf16)],
        args=(dy, a, x1, g, w1, w2), ride=ride)


def _wgrad(name, a, b, tk, tn, tt, relu2=False, col_shard=False, out_dtype=bf16):
    T, Ka = a.shape
    Nb = b.shape[1]
    nt = T // tt

    def body(a_ref, b_ref, o_ref, acc):
        t = pl.program_id(2)
        av = a_ref[...]
        if relu2:
            av = jnp.square(jnp.maximum(av.astype(f32), 0.0))
        prod = _tn(av.astype(bf16), b_ref[...].astype(bf16))

        @pl.when(t == 0)
        def _():
            acc[...] = prod

        @pl.when(t > 0)
        def _():
            acc[...] += prod

        @pl.when(t == nt - 1)
        def _():
            o_ref[...] = acc[...].astype(out_dtype)

    if col_shard:
        out_shape = S((Nb // tn, Ka, tn), out_dtype)
        out_spec = pl.BlockSpec((None, tk, tn), lambda i, j, t: (j, i, 0))
    else:
        out_shape = S((Ka, Nb), out_dtype)
        out_spec = pl.BlockSpec((tk, tn), lambda i, j, t: (i, j))
    return pl.pallas_call(
        body, name=name, grid=(Ka // tk, Nb // tn, nt),
        in_specs=[pl.BlockSpec((tt, tk), lambda i, j, t: (t, i)), pl.BlockSpec((tt, tn), lambda i, j, t: (t, j))],
        out_specs=out_spec, out_shape=out_shape, scratch_shapes=[pltpu.VMEM((tk, tn), f32)],
        compiler_params=_params("parallel", "parallel", "arbitrary"),
    )(a, b)


def _mix_bwd(dx1, proj, attn, conv, w_ao, w_co, w_o, tm, ride=None):
    T = dx1.shape[0]
    g0 = GLU_END // COL

    def gate_spec(off):
        return pl.BlockSpec((tm, COL), lambda i: (i, g0 + off))

    def body(dx_ref, ga0, ga1, gc0, gc1, at_ref, cv_ref, wa_ref, wc_ref, wo_ref, da_ref, dc_ref, do_ref, dh3_ref, dg_ref):
        dm = _nt(dx_ref[...].astype(bf16), wo_ref[...])
        sa = _sigmoid(jnp.concatenate([ga0[...], ga1[...]], axis=-1))
        sc = _sigmoid(jnp.concatenate([gc0[...], gc1[...]], axis=-1))
        dattn = (dm * sa).astype(bf16)
        dconv = (dm * sc).astype(bf16)
        da_ref[...] = dattn
        dc_ref[...] = dconv
        dg_ref[:, 0:D_MODEL] = (dm * at_ref[...].astype(f32) * sa * (1.0 - sa)).astype(bf16)
        dg_ref[:, D_MODEL:2 * D_MODEL] = (dm * cv_ref[...].astype(f32) * sc * (1.0 - sc)).astype(bf16)
        do_ref[...] = _nt(dattn, wa_ref[...]).astype(bf16)
        dh3_ref[...] = _nt(dconv, wc_ref[...])

    row = pl.BlockSpec((tm, D_MODEL), lambda i: (i, 0))
    wsp = pl.BlockSpec((D_MODEL, D_MODEL), lambda i: (0, 0))
    return _call(
        body, name="mix_bwd", grid=(T // tm,),
        in_specs=[row, gate_spec(0), gate_spec(1), gate_spec(2), gate_spec(3), row, row, wsp, wsp, wsp],
        out_specs=[row, row, row, row, pl.BlockSpec((tm, 2 * D_MODEL), lambda i: (i, 0))],
        out_shape=[S((T, D_MODEL), bf16), S((T, D_MODEL), bf16), S((T, D_MODEL), bf16), S((T, D_MODEL), f32),
                   S((T, 2 * D_MODEL), bf16)],
        args=(dx1, proj, proj, proj, proj, attn, conv, w_ao, w_co, w_o), ride=ride)


def _norm_act_bwd(dh3, h1, ln_g, ln_b, tm):
    T = dh3.shape[0]

    def body(d_ref, h_ref, g_ref, b_ref, dh1_ref, dg_ref, db_ref, dbd_ref):
        @pl.when(pl.program_id(0) == 0)
        def _():
            dg_ref[...] = jnp.zeros_like(dg_ref)
            db_ref[...] = jnp.zeros_like(db_ref)
            dbd_ref[...] = jnp.zeros_like(dbd_ref)

        xhat, rstd = _layer_norm_stats(h_ref[...])
        h2 = xhat * g_ref[...] + b_ref[...]
        sg = _sigmoid(h2)
        dh2 = d_ref[...] * (sg * (1.0 + h2 * (1.0 - sg)))
        dg_ref[...] += jnp.sum(dh2 * xhat, axis=0, keepdims=True)
        db_ref[...] += jnp.sum(dh2, axis=0, keepdims=True)
        dxh = dh2 * g_ref[...]
        dh1 = rstd * (dxh - jnp.mean(dxh, axis=-1, keepdims=True) - xhat * jnp.mean(dxh * xhat, axis=-1, keepdims=True))
        dh1_ref[...] = dh1
        dbd_ref[...] += jnp.sum(dh1, axis=0, keepdims=True)

    row = pl.BlockSpec((tm, D_MODEL), lambda i: (i, 0))
    vec = pl.BlockSpec((1, D_MODEL), lambda i: (0, 0))
    return pl.pallas_call(
        body, name="norm_act_bwd", grid=(T // tm,),
        in_specs=[row, row, vec, vec], out_specs=[row, vec, vec, vec],
        out_shape=[S((T, D_MODEL), f32), S((1, D_MODEL), f32), S((1, D_MODEL), f32), S((1, D_MODEL), f32)],
        compiler_params=_params("arbitrary"),
    )(dh3, h1, ln_g, ln_b)


def _conv_bwd(dh1, h0, proj, w_dw, tm, ride=None):
    T = dh1.shape[0]
    per = tm // CONV_HALO
    nh = T // CONV_HALO
    nt = T // tm
    a0 = V_END // COL

    def body(dc_ref, dn_ref, hc_ref, hp_ref, a0_ref, a1_ref, g0_ref, g1_ref, w_ref, dglu_ref, dw_ref, dcat, hcat, wacc):
        i = pl.program_id(0)

        @pl.when(i == 0)
        def _():
            wacc[...] = jnp.zeros_like(wacc)

        dcat[0:tm, :] = dc_ref[...]
        dcat[tm:, :] = jnp.where(i == nt - 1, 0.0, dn_ref[...])
        hcat[0:CONV_HALO, :] = jnp.where(i == 0, 0.0, hp_ref[...])
        hcat[CONV_HALO:, :] = hc_ref[...]
        dcur = dc_ref[...]
        dh0 = jnp.zeros((tm, D_MODEL), f32)
        for k in range(CONV_WIDTH):
            sh = hcat[pl.ds(CONV_HALO - (CONV_WIDTH - 1) + k, tm), :]
            wacc[k] += jnp.sum((dcur * sh).reshape(tm // 8, 8, D_MODEL), axis=0)
            dh0 = dh0 + dcat[pl.ds(CONV_WIDTH - 1 - k, tm), :] * w_ref[k:k + 1, :]
        av = jnp.concatenate([a0_ref[...], a1_ref[...]], axis=-1)
        sg = _sigmoid(jnp.concatenate([g0_ref[...], g1_ref[...]], axis=-1))
        dglu_ref[:, 0:D_MODEL] = (dh0 * sg).astype(bf16)
        dglu_ref[:, D_MODEL:2 * D_MODEL] = (dh0 * av * sg * (1.0 - sg)).astype(bf16)

        @pl.when(i == nt - 1)
        def _():
            for k in range(CONV_WIDTH):
                dw_ref[k:k + 1, :] = jnp.sum(wacc[k], axis=0, keepdims=True)
            dw_ref[CONV_WIDTH:CONV_WIDTH + 1, :] = jnp.zeros((1, D_MODEL), f32)

    row = pl.BlockSpec((tm, D_MODEL), lambda i: (i, 0))

    def col_spec(off):
        return pl.BlockSpec((tm, COL), lambda i: (i, a0 + off))

    return _call(
        body, name="conv_bwd", grid=(nt,),
        in_specs=[row, pl.BlockSpec((CONV_HALO, D_MODEL), lambda i: (jnp.minimum((i + 1) * per, nh - 1), 0)),
                  row, pl.BlockSpec((CONV_HALO, D_MODEL), lambda i: (jnp.maximum(i * per - 1, 0), 0)),
                  col_spec(0), col_spec(1), col_spec(2), col_spec(3),
                  pl.BlockSpec((CONV_WIDTH, D_MODEL), lambda i: (0, 0))],
        out_specs=[pl.BlockSpec((tm, 2 * D_MODEL), lambda i: (i, 0)), pl.BlockSpec((CONV_WIDTH + 1, D_MODEL), lambda i: (0, 0))],
        out_shape=[S((T, 2 * D_MODEL), bf16), S((CONV_WIDTH + 1, D_MODEL), f32)],
        scratch_shapes=[pltpu.VMEM((tm + CONV_HALO, D_MODEL), f32), pltpu.VMEM((tm + CONV_HALO, D_MODEL), f32),
                        pltpu.VMEM((CONV_WIDTH, 8, D_MODEL), f32)],
        args=(dh1, dh1, h0, h0, proj, proj, proj, proj, w_dw), ride=ride)


def _attn_bwd(qn, kn, vb, o, do, lse, bias, sinks):
    T = qn.shape[0]
    nb = T // QBLOCK

    def body(q_ref, kc_ref, kp_ref, vc_ref, vp_ref, o_ref, do_ref, lse_ref, b_ref, s_ref,
             dq_ref, dk_ref, dv_ref, db_ref, ds_ref, kcar, vcar):
        n = pl.program_id(0)

        @pl.when(n == 0)
        def _():
            db_ref[...] = jnp.zeros_like(db_ref)
            ds_ref[...] = jnp.zeros_like(ds_ref)
            kcar[...] = jnp.zeros_like(kcar)
            vcar[...] = jnp.zeros_like(vcar)

        @pl.when(n < nb)
        def _():
            pm = _first_block_mask(n)
            qv = q_ref[...]
            dov = do_ref[...]
            dl = dov.astype(f32) * o_ref[...].astype(f32)
            kband = jnp.concatenate([kp_ref[...], kc_ref[...]], axis=0)
            vband = jnp.concatenate([vp_ref[...], vc_ref[...]], axis=0)
            for h in range(N_KV_HEADS):
                hs = slice(h * HEAD_DIM, (h + 1) * HEAD_DIM)
                kh, vh = kband[:, hs], vband[:, hs]

                def stack(v):
                    return jnp.concatenate([v[:, (h * GROUP + g) * HEAD_DIM:(h * GROUP + g + 1) * HEAD_DIM]
                                            for g in range(GROUP)], axis=0)

                q4, do4 = stack(qv), stack(dov)
                delta = jnp.sum(stack(dl), axis=-1, keepdims=True)
                lse4 = jnp.concatenate([lse_ref[:, h * GROUP + g:h * GROUP + g + 1] for g in range(GROUP)], axis=0)
                rows = slice(h * GROUP * QBLOCK, (h + 1) * GROUP * QBLOCK)
                s = _nt(q4, kh) + b_ref[rows, :] + pm
                p = jnp.exp(s - lse4)
                dp = _nt(do4, vh)
                ds = p * (dp - delta)
                db_ref[rows, :] += ds
                dsb = ds.astype(bf16)
                dq4 = jnp.dot(dsb, kh, preferred_element_type=f32)
                dk = _tn(dsb, q4)
                dv = _tn(p.astype(bf16), do4)
                for g in range(GROUP):
                    hq = h * GROUP + g
                    dq_ref[:, hq * HEAD_DIM:(hq + 1) * HEAD_DIM] = dq4[g * QBLOCK:(g + 1) * QBLOCK, :]
                    psink = jnp.exp(s_ref[0, hq] - lse4[g * QBLOCK:(g + 1) * QBLOCK, :])
                    ds_ref[:, hq:hq + 1] -= jnp.sum(psink * delta[g * QBLOCK:(g + 1) * QBLOCK, :], axis=0, keepdims=True)
                dk_ref[:, hs] = kcar[:, hs] + dk[0:QBLOCK, :]
                dv_ref[:, hs] = vcar[:, hs] + dv[0:QBLOCK, :]
                kcar[:, hs] = dk[QBLOCK:, :]
                vcar[:, hs] = dv[QBLOCK:, :]

        @pl.when(n == nb)
        def _():
            dk_ref[...] = kcar[...]
            dv_ref[...] = vcar[...]

    cur = lambda n: (jnp.minimum(n, nb - 1), 0)
    prev = lambda n: (jnp.clip(n - 1, 0, nb - 1), 0)
    qspec = pl.BlockSpec((QBLOCK, ATTN_WIDTH), cur)
    kcur, kprev = pl.BlockSpec((QBLOCK, KV_WIDTH), cur), pl.BlockSpec((QBLOCK, KV_WIDTH), prev)
    bspec = pl.BlockSpec((N_Q_HEADS * QBLOCK, 2 * QBLOCK), lambda n: (0, 0))
    return pl.pallas_call(
        body, name="attn_bwd", grid=(nb + 1,),
        in_specs=[qspec, kcur, kprev, kcur, kprev, qspec, qspec, pl.BlockSpec((QBLOCK, N_Q_HEADS), cur), bspec, SMEM],
        out_specs=[qspec, kprev, kprev, bspec, pl.BlockSpec((1, N_Q_HEADS), lambda n: (0, 0))],
        out_shape=[S((T, ATTN_WIDTH), f32), S((T, KV_WIDTH), f32), S((T, KV_WIDTH), f32),
                   S((N_Q_HEADS * QBLOCK, 2 * QBLOCK), f32), S((1, N_Q_HEADS), f32)],
        scratch_shapes=[pltpu.VMEM((QBLOCK, KV_WIDTH), f32), pltpu.VMEM((QBLOCK, KV_WIDTH), f32)],
        compiler_params=_params("arbitrary"),
    )(qn, kn, kn, vb, vb, o, do, lse, bias, sinks)


def _rel_bias_bwd(dbias, bucket):
    def body(d_ref, bk_ref, o_ref):
        b = bk_ref[...]
        for k in range(N_BUCKETS):
            mk = b == k
            for h in range(N_Q_HEADS):
                o_ref[k, h] = jnp.sum(jnp.where(mk, d_ref[h * QBLOCK:(h + 1) * QBLOCK, :], 0.0))

    return pl.pallas_call(body, name="rel_bias_bwd", out_shape=S((N_BUCKETS, N_Q_HEADS), f32), out_specs=SMEM)(dbias, bucket)


def _qk_norm_bwd(dq, dk, dv, proj, qg, kg, tm):
    T = dq.shape[0]
    scale = HEAD_DIM ** -0.5

    def head_bwd(dseg, seg, gv):
        r = lax.rsqrt(jnp.mean(seg * seg, axis=-1, keepdims=True) + EPS)
        xn = seg * r
        dg = jnp.sum(dseg * xn, axis=0, keepdims=True)
        dxn = dseg * gv
        return r * (dxn - xn * jnp.mean(dxn * xn, axis=-1, keepdims=True)), dg

    def body(dq_ref, dk_ref, dv_ref, p_ref, qg_ref, kg_ref, out_ref, dqg_ref, dkg_ref):
        @pl.when(pl.program_id(0) == 0)
        def _():
            dqg_ref[...] = jnp.zeros_like(dqg_ref)
            dkg_ref[...] = jnp.zeros_like(dkg_ref)

        qgv, kgv = qg_ref[...], kg_ref[...]
        dqg = jnp.zeros((1, HEAD_DIM), f32)
        for h in range(N_Q_HEADS):
            hs = slice(h * HEAD_DIM, (h + 1) * HEAD_DIM)
            dx, dg = head_bwd(dq_ref[:, hs] * scale, p_ref[:, hs], qgv)
            out_ref[:, hs] = dx.astype(bf16)
            dqg = dqg + dg
        dkg = jnp.zeros((1, HEAD_DIM), f32)
        for h in range(N_KV_HEADS):
            hs = slice(h * HEAD_DIM, (h + 1) * HEAD_DIM)
            ps = slice(Q_END + h * HEAD_DIM, Q_END + (h + 1) * HEAD_DIM)
            dx, dg = head_bwd(dk_ref[:, hs], p_ref[:, ps], kgv)
            out_ref[:, ps] = dx.astype(bf16)
            dkg = dkg + dg
        out_ref[:, K_END:V_END] = dv_ref[...].astype(bf16)
        dqg_ref[...] += dqg
        dkg_ref[...] += dkg

    vec = pl.BlockSpec((1, HEAD_DIM), lambda i: (0, 0))
    return pl.pallas_call(
        body, name="qk_norm_bwd", grid=(T // tm,),
        in_specs=[pl.BlockSpec((tm, ATTN_WIDTH), lambda i: (i, 0)), pl.BlockSpec((tm, KV_WIDTH), lambda i: (i, 0)),
                  pl.BlockSpec((tm, KV_WIDTH), lambda i: (i, 0)), pl.BlockSpec((tm, V_END), lambda i: (i, 0)), vec, vec],
        out_specs=[pl.BlockSpec((tm, V_END), lambda i: (i, 0)), vec, vec],
        out_shape=[S((T, V_END), bf16), S((1, HEAD_DIM), f32), S((1, HEAD_DIM), f32)],
        compiler_params=_params("arbitrary"),
    )(dq, dk, dv, proj, qg, kg)


def _in_bwd(dqkv, dglu, dgates, w_in, x, g, dx1, tm, ride=None):
    T = x.shape[0]
    n0, n1, n2 = dqkv.shape[1] // COL, dglu.shape[1] // COL, dgates.shape[1] // COL
    nc = n0 + n1 + n2

    def body(a0_ref, a1_ref, a2_ref, w_ref, x_ref, g_ref, d_ref, gx_ref, dg_ref, acc):
        i, c = pl.program_id(0), pl.program_id(1)

        @pl.when(c == 0)
        def _():
            acc[...] = jnp.zeros_like(acc)

        @pl.when((i == 0) & (c == 0))
        def _():
            dg_ref[...] = jnp.zeros_like(dg_ref)

        @pl.when(c < n0)
        def _():
            acc[...] += _nt(a0_ref[...], w_ref[...])

        @pl.when((c >= n0) & (c < n0 + n1))
        def _():
            acc[...] += _nt(a1_ref[...], w_ref[...])

        @pl.when(c >= n0 + n1)
        def _():
            acc[...] += _nt(a2_ref[...], w_ref[...])

        @pl.when(c == nc - 1)
        def _():
            dx, dg = _rms_bwd(acc[...], x_ref[...], g_ref[...])
            gx_ref[...] = d_ref[...] + dx
            dg_ref[...] += dg

    row = pl.BlockSpec((tm, D_MODEL), lambda i, c: (i, 0))
    vec = pl.BlockSpec((1, D_MODEL), lambda i, c: (0, 0))
    return _call(
        body, name="in_bwd", grid=(T // tm, nc),
        in_specs=[pl.BlockSpec((tm, COL), lambda i, c: (i, jnp.clip(c, 0, n0 - 1))),
                  pl.BlockSpec((tm, COL), lambda i, c: (i, jnp.clip(c - n0, 0, n1 - 1))),
                  pl.BlockSpec((tm, COL), lambda i, c: (i, jnp.clip(c - n0 - n1, 0, n2 - 1))),
                  pl.BlockSpec((D_MODEL, COL), lambda i, c: (0, c)), row, vec, row],
        out_specs=[row, vec],
        out_shape=[S((T, D_MODEL), f32), S((1, D_MODEL), f32)],
        scratch_shapes=[pltpu.VMEM((tm, D_MODEL), f32)],
        args=(dqkv, dglu, dgates, w_in, x, g, dx1), ride=ride)


def _adamw(name, parts, w, m, v, tr):
    R, C = w.shape
    bc1 = 1.0 - ADAM_B1 ** ADAM_STEP
    bc2 = 1.0 - ADAM_B2 ** ADAM_STEP

    def body(p_ref, w_ref, m_ref, v_ref, g_ref, d_ref, nm_ref, nv_ref):
        g = p_ref[0].astype(f32)
        for k in range(1, N_DEV):
            g = g + p_ref[k].astype(f32)
        nm = ADAM_B1 * m_ref[...] + (1.0 - ADAM_B1) * g
        nv = ADAM_B2 * v_ref[...] + (1.0 - ADAM_B2) * (g * g)
        g_ref[...] = g
        nm_ref[...] = nm
        nv_ref[...] = nv
        d_ref[...] = -ADAM_LR * ((nm / bc1) / (jnp.sqrt(nv / bc2) + ADAM_EPS) + ADAM_WD * w_ref[...])

    blk = pl.BlockSpec((tr, C), lambda i: (i, 0))
    return pl.pallas_call(
        body, name=name, grid=(R // tr,),
        in_specs=[pl.BlockSpec((N_DEV, tr, C), lambda i: (0, i, 0)), blk, blk, blk],
        out_specs=[blk, blk, blk, blk], out_shape=[S((R, C), f32)] * 4,
        compiler_params=_params("parallel"),
    )(parts, w, m, v)


def _tile(T, pref):
    return min(T, pref)


def _pad_rows(a, rows):
    return jnp.pad(a, ((0, rows - a.shape[0]), (0, 0)))


def kernel(x, norm_mix_g, w_in, q_norm_g, k_norm_g, attn_sinks, rel_bias, w_attn_o, w_dw, b_dw, conv_ln_g, conv_ln_b, w_conv_out, w_out, norm_mlp_g, w_ff1, w_ff2, loss_target, m_norm_mix_g, m_w_in, m_q_norm_g, m_k_norm_g, m_attn_sinks, m_rel_bias, m_w_attn_o, m_w_dw, m_b_dw, m_conv_ln_g, m_conv_ln_b, m_w_conv_out, m_w_out, m_norm_mlp_g, m_w_ff1, m_w_ff2, v_norm_mix_g, v_w_in, v_q_norm_g, v_k_norm_g, v_attn_sinks, v_rel_bias, v_w_attn_o, v_w_dw, v_b_dw, v_conv_ln_g, v_conv_ln_b, v_w_conv_out, v_w_out, v_norm_mlp_g, v_w_ff1, v_w_ff2):
    T = x.shape[1]
    xs = x[0]
    tgt = loss_target[0]
    in_shard = IN_WIDTH // N_DEV
    dw_rows = CONV_WIDTH + 1
    ch_shard = D_MODEL // N_DEV
    tm = _tile(T, 512)
    tc = _tile(T, 256)
    bucket = jnp.asarray(_t5_bucket_table())

    g_in, g_dw = _exchange("gather_w_in", [w_in[0].astype(bf16), _pad_rows(w_dw[0], dw_rows)], gather=True)
    W_in = jnp.transpose(g_in, (1, 0, 2)).reshape(D_MODEL, IN_WIDTH)
    W_dw = jnp.transpose(g_dw, (1, 0, 2)).reshape(dw_rows, D_MODEL)[:CONV_WIDTH]

    mix_shards = _Exchange([w_attn_o[0].astype(bf16), w_conv_out[0].astype(bf16), w_out[0].astype(bf16)], gather=True)
    (proj, u), (g_ao, g_co, g_o) = _proj_fwd(xs, norm_mix_g, W_in, tm, COL, ride=mix_shards)
    W_ao = g_ao.reshape(D_MODEL, D_MODEL)
    W_co = g_co.reshape(D_MODEL, D_MODEL)
    W_o = g_o.reshape(D_MODEL, D_MODEL)
    qn, kn, vb, h0 = _prep_fwd(proj, q_norm_g, k_norm_g, tc)
    bias = _bias_table(rel_bias, bucket)
    ffn_shards = _Exchange([w_ff1[0].astype(bf16), w_ff2[0].astype(bf16)], gather=True)
    (o, lse), (g_f1, g_f2) = _attn_fwd(qn, kn, vb, bias, attn_sinks, ride=ffn_shards)
    h1, h3 = _conv_fwd(h0, W_dw, b_dw, conv_ln_g, conv_ln_b, tc)
    x1, attn, conv, merged = _mix_fwd(xs, o, h3, proj, W_ao, W_co, W_o, tc)
    a, u2, dy, loss_parts = _ffn_fwd(x1, norm_mlp_g, g_f1, g_f2, tgt, tm)
    loss = lax.psum(jnp.sum(loss_parts[:, 0, 0]), ("x", "y", "c"))

    gw_f2 = _wgrad("wgrad_ff2", a, dy, FF_CHUNK, D_MODEL, tm, relu2=True).reshape(N_DEV, FF_CHUNK, D_MODEL)
    (da, dx1, d_norm_mlp_g), (l_f2,) = _ffn_bwd(dy, a, x1, norm_mlp_g, g_f1, g_f2, tm, ride=_Exchange([gw_f2], gather=False))
    gw_f1 = _wgrad("wgrad_ff1", u2, da, D_MODEL, FF_CHUNK, tm, col_shard=True)
    gw_o = _wgrad("wgrad_out", merged, dx1, D_MODEL, D_MODEL, tm).reshape(N_DEV, ch_shard, D_MODEL)
    (dattn, dconv, do, dh3, dgates), (l_f1, l_o) = _mix_bwd(dx1, proj, attn, conv, W_ao, W_co, W_o, tc,
                                                             ride=_Exchange([gw_f1, gw_o], gather=False))
    gw_ao = _wgrad("wgrad_attn_o", o, dattn, D_MODEL, D_MODEL, tm).reshape(N_DEV, ch_shard, D_MODEL)
    gw_co = _wgrad("wgrad_conv_out", h3, dconv, D_MODEL, D_MODEL, tm).reshape(N_DEV, ch_shard, D_MODEL)
    dh1, d_ln_g, d_ln_b, d_b_dw = _norm_act_bwd(dh3, h1, conv_ln_g, conv_ln_b, tc)
    (dglu, d_w_dw), (l_ao, l_co) = _conv_bwd(dh1, h0, proj, W_dw, tc, ride=_Exchange([gw_ao, gw_co], gather=False))
    dq, dk, dv, dbias, d_sinks = _attn_bwd(qn, kn, vb, o, do, lse, bias, attn_sinks)
    d_rel_bias = _rel_bias_bwd(dbias, bucket)
    dqkv, d_qg, d_kg = _qk_norm_bwd(dq, dk, dv, proj, q_norm_g, k_norm_g, tc)
    gw_in = jnp.concatenate([_wgrad("wgrad_in_qkv", u, dqkv, D_MODEL, COL, tm),
                             _wgrad("wgrad_in_glu", u, dglu, D_MODEL, COL, tm),
                             _wgrad("wgrad_in_gates", u, dgates, D_MODEL, COL, tm)], axis=1)
    gw_in = jnp.transpose(gw_in.reshape(D_MODEL, N_DEV, in_shard), (1, 0, 2))
    gw_dw = jnp.transpose(d_w_dw.reshape(dw_rows, N_DEV, ch_shard), (1, 0, 2))
    (grad_x, d_norm_mix_g), (l_in, l_dw) = _in_bwd(dqkv, dglu, dgates, W_in, xs, norm_mix_g, dx1, tm,
                                                    ride=_Exchange([gw_in, gw_dw], gather=False))

    def row(vec):
        flat = vec.reshape(1, -1)
        return jnp.pad(flat, ((0, 0), (0, D_MODEL - flat.shape[1])))

    def pack_small(nm, qg, kg, sk, rb, bd, lg, lb, nl):
        tail = jnp.concatenate([qg.reshape(1, -1), kg.reshape(1, -1), sk.reshape(1, -1), rb.reshape(1, -1)], axis=1)
        return jnp.concatenate([row(nm), row(bd), row(lg), row(lb), row(nl), row(tail), jnp.zeros((2, D_MODEL), f32)], axis=0)

    def unpack_small(p):
        t = p[5]
        o0, o1, o2 = HEAD_DIM, 2 * HEAD_DIM, 2 * HEAD_DIM + N_Q_HEADS
        return dict(norm_mix_g=p[0:1], b_dw=p[1:2], conv_ln_g=p[2:3], conv_ln_b=p[3:4], norm_mlp_g=p[4:5],
                    q_norm_g=t[0:o0].reshape(1, HEAD_DIM), k_norm_g=t[o0:o1].reshape(1, HEAD_DIM),
                    attn_sinks=t[o1:o2].reshape(1, N_Q_HEADS),
                    rel_bias=t[o2:o2 + N_BUCKETS * N_Q_HEADS].reshape(N_BUCKETS, N_Q_HEADS))

    small_g = pack_small(d_norm_mix_g, d_qg, d_kg, d_sinks, d_rel_bias, d_b_dw, d_ln_g, d_ln_b, d_norm_mlp_g)
    (l_small,) = _exchange("gather_small_grads", [small_g], gather=True)


    res = {}
    res["w_in"] = _adamw("adamw_in", l_in, w_in[0], m_w_in[0], v_w_in[0], 256)
    res["w_attn_o"] = _adamw("adamw_attn_o", l_ao, w_attn_o[0], m_w_attn_o[0], v_w_attn_o[0], ch_shard)
    res["w_conv_out"] = _adamw("adamw_conv_out", l_co, w_conv_out[0], m_w_conv_out[0], v_w_conv_out[0], ch_shard)
    res["w_out"] = _adamw("adamw_out", l_o, w_out[0], m_w_out[0], v_w_out[0], ch_shard)
    res["w_ff1"] = _adamw("adamw_ff1", l_f1, w_ff1[0], m_w_ff1[0], v_w_ff1[0], 256)
    res["w_ff2"] = _adamw("adamw_ff2", l_f2, w_ff2[0], m_w_ff2[0], v_w_ff2[0], 256)
    dw4 = _adamw("adamw_dw", l_dw, _pad_rows(w_dw[0], dw_rows), _pad_rows(m_w_dw[0], dw_rows), _pad_rows(v_w_dw[0], dw_rows), dw_rows)
    res["w_dw"] = [t[:CONV_WIDTH] for t in dw4]
    small_w = pack_small(norm_mix_g, q_norm_g, k_norm_g, attn_sinks, rel_bias, b_dw, conv_ln_g, conv_ln_b, norm_mlp_g)
    small_m = pack_small(m_norm_mix_g, m_q_norm_g, m_k_norm_g, m_attn_sinks, m_rel_bias, m_b_dw, m_conv_ln_g, m_conv_ln_b, m_norm_mlp_g)
    small_v = pack_small(v_norm_mix_g, v_q_norm_g, v_k_norm_g, v_attn_sinks, v_rel_bias, v_b_dw, v_conv_ln_g, v_conv_ln_b, v_norm_mlp_g)
    small4 = [unpack_small(t) for t in _adamw("adamw_small", l_small, small_w, small_m, small_v, 8)]

    order = ["norm_mix_g", "w_in", "q_norm_g", "k_norm_g", "attn_sinks", "rel_bias", "w_attn_o", "w_dw", "b_dw",
             "conv_ln_g", "conv_ln_b", "w_conv_out", "w_out", "norm_mlp_g", "w_ff1", "w_ff2"]
    stacked = {"w_in", "w_attn_o", "w_dw", "w_conv_out", "w_out", "w_ff1", "w_ff2"}
    outs = [loss, grad_x[None]]
    for k in range(4):
        for nme in order:
            if nme in stacked:
                outs.append(res[nme][k][None])
            else:
                outs.append(small4[k][nme])
    return tuple(outs)
```

```python
import functools

import numpy as np
import jax
import jax.numpy as jnp
from jax import lax
from jax.experimental import pallas as pl
from jax.experimental.pallas import tpu as pltpu

f32 = jnp.float32
bf16 = jnp.bfloat16
S = jax.ShapeDtypeStruct

N_DEV = 8
D_MODEL = 1024
HEAD_DIM = 64
N_Q_HEADS = 16
N_KV_HEADS = 4
GROUP = N_Q_HEADS // N_KV_HEADS
ATTN_WIDTH = N_Q_HEADS * HEAD_DIM
KV_WIDTH = N_KV_HEADS * HEAD_DIM
QBLOCK = 128
CONV_WIDTH = 31
CONV_HALO = 32
CONV_UNIT = 64
D_FF = 4 * D_MODEL
N_BUCKETS = 32
MAX_DISTANCE = 128
EPS = 1e-6
NEG = -1e30
Q_END = ATTN_WIDTH
K_END = Q_END + KV_WIDTH
V_END = K_END + KV_WIDTH
GLU_END = V_END + 2 * D_MODEL
IN_WIDTH = GLU_END + 2 * D_MODEL
COL = 512
FF_CHUNK = D_FF // N_DEV

ADAM_LR = 0.001
ADAM_B1 = 0.9
ADAM_B2 = 0.999
ADAM_EPS = 1e-08
ADAM_WD = 0.01
ADAM_STEP = 10

VMEM_LIMIT = 56 * 1024 * 1024

MESH_ID = pl.DeviceIdType.MESH
ANY = pl.BlockSpec(memory_space=pl.ANY)
SMEM = pl.BlockSpec(memory_space=pltpu.SMEM)


def _params(*sem):
    return pltpu.CompilerParams(dimension_semantics=sem, vmem_limit_bytes=VMEM_LIMIT)


def _nt(a, b):
    return lax.dot_general(a, b, (((1,), (1,)), ((), ())), preferred_element_type=f32)


def _tn(a, b):
    return lax.dot_general(a, b, (((0,), (0,)), ((), ())), preferred_element_type=f32)


def _sigmoid(z):
    return 1.0 / (1.0 + jnp.exp(-z))


def _t5_bucket_table():
    qi = np.arange(QBLOCK, dtype=np.int32)[:, None]
    kj = np.arange(2 * QBLOCK, dtype=np.int32)[None, :]
    dist = qi + QBLOCK - kj
    n = np.maximum(dist, 0)
    max_exact = N_BUCKETS // 2
    nf = np.maximum(n, 1).astype(np.float32)
    large = max_exact + (np.log(nf / np.float32(max_exact)) / np.float32(np.log(MAX_DISTANCE / max_exact))
                         * np.float32(N_BUCKETS - max_exact)).astype(np.int32)
    large = np.minimum(large, N_BUCKETS - 1)
    bucket = np.where(n < max_exact, n, large)
    valid = (dist >= 0) & (dist < QBLOCK)
    return np.where(valid, bucket, -1).astype(np.int32)


def _peer(d):
    x, y, c = lax.axis_index("x"), lax.axis_index("y"), lax.axis_index("c")
    dx, dy, dc = (d >> 2) & 1, (d >> 1) & 1, d & 1
    px, py, pc = x ^ dx, y ^ dy, c ^ dc
    return (px, py, pc), 4 * px + 2 * py + pc


class _Exchange:
    def __init__(self, arrays, gather):
        self.arrays, self.gather, self.n = list(arrays), gather, len(arrays)
        self.out_shape = [S(((N_DEV,) + a.shape) if gather else a.shape, a.dtype) for a in self.arrays]
        self.scratch = [pltpu.SemaphoreType.DMA((self.n, N_DEV - 1)), pltpu.SemaphoreType.DMA((self.n, N_DEV - 1)),
                        pltpu.SemaphoreType.DMA((self.n,))]

    def _copies(self, ins, outs, sems):
        send_sems, recv_sems, local_sems = sems
        _, me = _peer(0)
        local, sends, recvs = [], [], []
        for k in range(self.n):
            src = ins[k] if self.gather else ins[k].at[me]
            local.append(pltpu.make_async_copy(src, outs[k].at[me], local_sems.at[k]))
        for d in range(1, N_DEV):
            peer, pidx = _peer(d)
            for k in range(self.n):
                src = ins[k] if self.gather else ins[k].at[pidx]
                common = dict(src_ref=src, send_sem=send_sems.at[k, d - 1], recv_sem=recv_sems.at[k, d - 1],
                              device_id=peer, device_id_type=MESH_ID)
                sends.append(pltpu.make_async_remote_copy(dst_ref=outs[k].at[me], **common))
                recvs.append(pltpu.make_async_remote_copy(dst_ref=outs[k].at[pidx], **common))
        return local, sends, recvs

    def start(self, ins, outs, sems):
        local, sends, _ = self._copies(ins, outs, sems)
        for cp in local + sends:
            cp.start()

    def wait(self, ins, outs, sems):
        local, sends, recvs = self._copies(ins, outs, sems)
        for cp in recvs:
            cp.wait_recv()
        for cp in sends:
            cp.wait_send()
        for cp in local:
            cp.wait()


def _exchange(name, arrays, gather):
    ex = _Exchange(arrays, gather)
    n = ex.n

    def body(*refs):
        ins, outs, sems = refs[:n], refs[n:2 * n], refs[2 * n:]
        ex.start(ins, outs, sems)
        ex.wait(ins, outs, sems)

    return pl.pallas_call(body, name=name, out_shape=ex.out_shape, in_specs=[ANY] * n, out_specs=[ANY] * n,
                          scratch_shapes=ex.scratch)(*arrays)


def _call(body, *, name, grid, in_specs, out_specs, out_shape, args, scratch_shapes=(), ride=None):
    n_in, n_out, n_sc = len(in_specs), len(out_specs), len(scratch_shapes)
    sem = ("arbitrary",) * len(grid)
    if ride is None:
        res = pl.pallas_call(body, name=name, grid=grid, in_specs=list(in_specs), out_specs=list(out_specs),
                             out_shape=list(out_shape), scratch_shapes=list(scratch_shapes), compiler_params=_params(*sem))(*args)
        return list(res), []
    nx = ride.n

    def riding(*refs):
        ins, xin = refs[:n_in], refs[n_in:n_in + nx]
        outs, xout = refs[n_in + nx:n_in + nx + n_out], refs[n_in + nx + n_out:n_in + 2 * nx + n_out]
        rest = refs[n_in + 2 * nx + n_out:]
        scratch, sems = rest[:n_sc], rest[n_sc:]
        ids = [pl.program_id(ax) for ax in range(len(grid))]
        first = functools.reduce(jnp.logical_and, [i == 0 for i in ids])
        last = functools.reduce(jnp.logical_and, [i == g - 1 for i, g in zip(ids, grid)])

        @pl.when(first)
        def _():
            ride.start(xin, xout, sems)

        body(*ins, *outs, *scratch)

        @pl.when(last)
        def _():
            ride.wait(xin, xout, sems)

    res = pl.pallas_call(
        riding, name=name, grid=grid, in_specs=list(in_specs) + [ANY] * nx, out_specs=list(out_specs) + [ANY] * nx,
        out_shape=list(out_shape) + ride.out_shape, scratch_shapes=list(scratch_shapes) + ride.scratch,
        compiler_params=_params(*sem))(*args, *ride.arrays)
    return list(res[:n_out]), list(res[n_out:])


def _resident(shape):
    return pl.BlockSpec(shape, lambda *_: (0,) * len(shape), pipeline_mode=pl.Buffered(1))


def _proj_fwd(x, g, w, tm, ride=None):
    T, K = x.shape
    N = w.shape[1]

    def body(x_ref, g_ref, w_ref, o_ref, u_ref):
        xv = x_ref[...]
        r = lax.rsqrt(jnp.mean(xv * xv, axis=-1, keepdims=True) + EPS)
        u = (xv * r * g_ref[...]).astype(bf16)
        u_ref[...] = u
        for c in range(N // COL):
            cs = slice(c * COL, (c + 1) * COL)
            o_ref[:, cs] = jnp.dot(u, w_ref[:, cs], preferred_element_type=f32)

    return _call(
        body, name="proj_fwd", grid=(T // tm,),
        in_specs=[pl.BlockSpec((tm, K), lambda i: (i, 0)), _resident((1, K)), _resident((K, N))],
        out_specs=[pl.BlockSpec((tm, N), lambda i: (i, 0)), pl.BlockSpec((tm, K), lambda i: (i, 0))],
        out_shape=[S((T, N), f32), S((T, K), bf16)], args=(x, g, w), ride=ride)


def _prep_fwd(proj, qg, kg, tm):
    T = proj.shape[0]

    def body(p_ref, qg_ref, kg_ref, qn_ref, kn_ref, vb_ref, h0_ref):
        qgv = qg_ref[...] * (HEAD_DIM ** -0.5)
        kgv = kg_ref[...]
        for h in range(N_Q_HEADS):
            seg = p_ref[:, h * HEAD_DIM:(h + 1) * HEAD_DIM]
            r = lax.rsqrt(jnp.mean(seg * seg, axis=-1, keepdims=True) + EPS)
            qn_ref[:, h * HEAD_DIM:(h + 1) * HEAD_DIM] = (seg * r * qgv).astype(bf16)
        for h in range(N_KV_HEADS):
            seg = p_ref[:, Q_END + h * HEAD_DIM:Q_END + (h + 1) * HEAD_DIM]
            r = lax.rsqrt(jnp.mean(seg * seg, axis=-1, keepdims=True) + EPS)
            kn_ref[:, h * HEAD_DIM:(h + 1) * HEAD_DIM] = (seg * r * kgv).astype(bf16)
        vb_ref[...] = p_ref[:, K_END:V_END].astype(bf16)
        h0_ref[...] = p_ref[:, V_END:V_END + D_MODEL] * _sigmoid(p_ref[:, V_END + D_MODEL:GLU_END])

    return pl.pallas_call(
        body, name="prep_fwd", grid=(T // tm,),
        in_specs=[pl.BlockSpec((tm, GLU_END), lambda i: (i, 0)), pl.BlockSpec((1, HEAD_DIM), lambda i: (0, 0)),
                  pl.BlockSpec((1, HEAD_DIM), lambda i: (0, 0))],
        out_specs=[pl.BlockSpec((tm, ATTN_WIDTH), lambda i: (i, 0)), pl.BlockSpec((tm, KV_WIDTH), lambda i: (i, 0)),
                   pl.BlockSpec((tm, KV_WIDTH), lambda i: (i, 0)), pl.BlockSpec((tm, D_MODEL), lambda i: (i, 0))],
        out_shape=[S((T, ATTN_WIDTH), bf16), S((T, KV_WIDTH), bf16), S((T, KV_WIDTH), bf16), S((T, D_MODEL), f32)],
        compiler_params=_params("parallel"),
    )(proj, qg, kg)


def _bias_table(rel_bias, bucket):
    def body(rb_ref, bk_ref, o_ref):
        b = bk_ref[...]
        for h in range(N_Q_HEADS):
            acc = jnp.full((QBLOCK, 2 * QBLOCK), NEG, f32)
            for k in range(N_BUCKETS):
                acc = jnp.where(b == k, rb_ref[k, h], acc)
            o_ref[h * QBLOCK:(h + 1) * QBLOCK, :] = acc

    return pl.pallas_call(
        body, name="bias_table", out_shape=S((N_Q_HEADS * QBLOCK, 2 * QBLOCK), f32),
        in_specs=[SMEM, pl.BlockSpec(memory_space=pltpu.VMEM)],
    )(rel_bias, bucket)


def _first_block_mask(n):
    kj = lax.broadcasted_iota(jnp.int32, (1, 2 * QBLOCK), 1)
    return jnp.where((kj < QBLOCK) & (n == 0), NEG, 0.0).astype(f32)


def _attn_fwd(qn, kn, vb, bias, sinks, ride=None):
    T = qn.shape[0]
    nb = T // QBLOCK

    def body(q_ref, kc_ref, kp_ref, vc_ref, vp_ref, b_ref, s_ref, o_ref, lse_ref):
        n = pl.program_id(0)
        pm = _first_block_mask(n)
        qv = q_ref[...]
        kband = jnp.concatenate([kp_ref[...], kc_ref[...]], axis=0)
        vband = jnp.concatenate([vp_ref[...], vc_ref[...]], axis=0)
        for h in range(N_KV_HEADS):
            kh = kband[:, h * HEAD_DIM:(h + 1) * HEAD_DIM]
            vh = vband[:, h * HEAD_DIM:(h + 1) * HEAD_DIM]
            q4 = jnp.concatenate([qv[:, (h * GROUP + g) * HEAD_DIM:(h * GROUP + g + 1) * HEAD_DIM] for g in range(GROUP)], axis=0)
            s = _nt(q4, kh) + b_ref[h * GROUP * QBLOCK:(h + 1) * GROUP * QBLOCK, :] + pm
            sink = jnp.concatenate([jnp.full((QBLOCK, 1), s_ref[0, h * GROUP + g], f32) for g in range(GROUP)], axis=0)
            m = jnp.maximum(jnp.max(s, axis=-1, keepdims=True), sink)
            p = jnp.exp(s - m)
            l = jnp.sum(p, axis=-1, keepdims=True) + jnp.exp(sink - m)
            o4 = jnp.dot((p * (1.0 / l)).astype(bf16), vh, preferred_element_type=f32)
            lse4 = m + jnp.log(l)
            for g in range(GROUP):
                hq = h * GROUP + g
                o_ref[:, hq * HEAD_DIM:(hq + 1) * HEAD_DIM] = o4[g * QBLOCK:(g + 1) * QBLOCK, :].astype(bf16)
                lse_ref[:, hq:hq + 1] = lse4[g * QBLOCK:(g + 1) * QBLOCK, :]

    cur = lambda n: (n, 0)
    prev = lambda n: (jnp.maximum(n - 1, 0), 0)
    return _call(
        body, name="attn_fwd", grid=(nb,),
        in_specs=[pl.BlockSpec((QBLOCK, ATTN_WIDTH), cur), pl.BlockSpec((QBLOCK, KV_WIDTH), cur),
                  pl.BlockSpec((QBLOCK, KV_WIDTH), prev), pl.BlockSpec((QBLOCK, KV_WIDTH), cur),
                  pl.BlockSpec((QBLOCK, KV_WIDTH), prev),
                  pl.BlockSpec((N_Q_HEADS * QBLOCK, 2 * QBLOCK), lambda n: (0, 0)), SMEM],
        out_specs=[pl.BlockSpec((QBLOCK, ATTN_WIDTH), cur), pl.BlockSpec((QBLOCK, N_Q_HEADS), cur)],
        out_shape=[S((T, ATTN_WIDTH), bf16), S((T, N_Q_HEADS), f32)],
        args=(qn, kn, kn, vb, vb, bias, sinks), ride=ride)


def _layer_norm_stats(h1):
    mu = jnp.mean(h1, axis=-1, keepdims=True)
    xc = h1 - mu
    rstd = lax.rsqrt(jnp.mean(xc * xc, axis=-1, keepdims=True) + EPS)
    return xc * rstd, rstd


def _advanced_windows(win):
    rows = win.shape[0]
    for r in range(8):
        yield r, (win if r == 0 else pltpu.roll(win, rows - r, 0))


def _tap_offsets(r, rows):
    for q in range((rows - CONV_UNIT) // 8 + 1):
        if r == 0 or 8 * q + r + CONV_UNIT <= rows:
            yield q, 8 * q + r


def _conv_fwd(h0, w_dw, b_dw, ln_g, ln_b, tm):
    T = h0.shape[0]
    per = tm // CONV_HALO
    lead = CONV_HALO - (CONV_WIDTH - 1)

    def body(hc_ref, hp_ref, w_ref, b_ref, g_ref, bb_ref, h1_ref, h3_ref, cat):
        i = pl.program_id(0)
        cat[0:CONV_HALO, :] = jnp.where(i == 0, 0.0, hp_ref[...])
        cat[CONV_HALO:, :] = hc_ref[...]

        def unit_rows(c, carry):
            r0 = pl.multiple_of(c * CONV_UNIT, CONV_UNIT)
            for j in range(D_MODEL // 128):
                ls = slice(j * 128, (j + 1) * 128)
                win = cat[pl.ds(r0, CONV_UNIT + CONV_HALO), ls]
                acc = jnp.zeros((CONV_UNIT, 128), f32) + b_ref[:, ls]
                for r, adv in _advanced_windows(win):
                    for q, off in _tap_offsets(r, CONV_UNIT + CONV_HALO):
                        k = off - lead
                        if 0 <= k < CONV_WIDTH:
                            acc = acc + adv[8 * q:8 * q + CONV_UNIT] * w_ref[k:k + 1, ls]
                h1_ref[pl.ds(r0, CONV_UNIT), ls] = acc
            return carry

        lax.fori_loop(0, tm // CONV_UNIT, unit_rows, 0)
        acc = h1_ref[...]
        xhat, _ = _layer_norm_stats(acc)
        h2 = xhat * g_ref[...] + bb_ref[...]
        h3_ref[...] = (h2 * _sigmoid(h2)).astype(bf16)

    vec = pl.BlockSpec((1, D_MODEL), lambda i: (0, 0))
    return pl.pallas_call(
        body, name="conv_fwd", grid=(T // tm,),
        in_specs=[pl.BlockSpec((tm, D_MODEL), lambda i: (i, 0)),
                  pl.BlockSpec((CONV_HALO, D_MODEL), lambda i: (jnp.maximum(i * per - 1, 0), 0)),
                  pl.BlockSpec((CONV_WIDTH, D_MODEL), lambda i: (0, 0)), vec, vec, vec],
        out_specs=[pl.BlockSpec((tm, D_MODEL), lambda i: (i, 0)), pl.BlockSpec((tm, D_MODEL), lambda i: (i, 0))],
        out_shape=[S((T, D_MODEL), f32), S((T, D_MODEL), bf16)],
        scratch_shapes=[pltpu.VMEM((tm + CONV_HALO, D_MODEL), f32)],
        compiler_params=_params("parallel"),
    )(h0, h0, w_dw, b_dw, ln_g, ln_b)


def _mix_fwd(x, o, h3, proj, w_ao, w_co, w_o, tm):
    T = x.shape[0]
    row = pl.BlockSpec((tm, D_MODEL), lambda i: (i, 0))
    wsp = _resident((D_MODEL, D_MODEL))
    g0 = GLU_END // COL

    def gate_spec(off):
        return pl.BlockSpec((tm, COL), lambda i: (i, g0 + off))

    def body(x_ref, o_ref, h3_ref, ga0, ga1, gc0, gc1, wa_ref, wc_ref, wo_ref, x1_ref, at_ref, cv_ref, mg_ref):
        attn = jnp.dot(o_ref[...], wa_ref[...], preferred_element_type=f32)
        conv = jnp.dot(h3_ref[...], wc_ref[...], preferred_element_type=f32)
        ga = jnp.concatenate([ga0[...], ga1[...]], axis=-1)
        gc = jnp.concatenate([gc0[...], gc1[...]], axis=-1)
        merged = (_sigmoid(ga) * attn + _sigmoid(gc) * conv).astype(bf16)
        at_ref[...] = attn.astype(bf16)
        cv_ref[...] = conv.astype(bf16)
        mg_ref[...] = merged
        x1_ref[...] = x_ref[...] + jnp.dot(merged, wo_ref[...], preferred_element_type=f32)

    return pl.pallas_call(
        body, name="mix_fwd", grid=(T // tm,),
        in_specs=[row, row, row, gate_spec(0), gate_spec(1), gate_spec(2), gate_spec(3), wsp, wsp, wsp],
        out_specs=[row, row, row, row],
        out_shape=[S((T, D_MODEL), f32), S((T, D_MODEL), bf16), S((T, D_MODEL), bf16), S((T, D_MODEL), bf16)],
        compiler_params=_params("parallel"),
    )(x, o, h3, proj, proj, proj, proj, w_ao, w_co, w_o)


def _ffn_fwd(x1, g, w1, w2, target, tm):
    T = x1.shape[0]
    nj = w1.shape[0]

    def body(x_ref, g_ref, w1_ref, w2_ref, t_ref, a_ref, u_ref, dy_ref, dyb_ref, ls_ref, hm):
        xv = x_ref[...]
        r = lax.rsqrt(jnp.mean(xv * xv, axis=-1, keepdims=True) + EPS)
        u = (xv * r * g_ref[...]).astype(bf16)
        u_ref[...] = u
        for j in range(nj):
            js = slice(j * FF_CHUNK, (j + 1) * FF_CHUNK)
            a = jnp.dot(u, w1_ref[j], preferred_element_type=f32)
            a_ref[:, js] = a.astype(bf16)
            hm[:, js] = jnp.square(jnp.maximum(a, 0.0)).astype(bf16)
        err = xv + jnp.dot(hm[...], w2_ref[...], preferred_element_type=f32) - t_ref[...]
        dy = err * (1.0 / D_MODEL)
        dy_ref[...] = dy
        dyb_ref[...] = dy.astype(bf16)
        ls_ref[...] = jnp.zeros((8, 128), f32) + jnp.sum(err * err) * (0.5 / D_MODEL)

    row = pl.BlockSpec((tm, D_MODEL), lambda i: (i, 0))
    wide = pl.BlockSpec((tm, D_FF), lambda i: (i, 0))
    return pl.pallas_call(
        body, name="ffn_fwd", grid=(T // tm,),
        in_specs=[row, _resident((1, D_MODEL)), _resident(w1.shape), _resident(w2.shape), row],
        out_specs=[wide, row, row, row, pl.BlockSpec((None, 8, 128), lambda i: (i, 0, 0))],
        out_shape=[S((T, D_FF), bf16), S((T, D_MODEL), bf16), S((T, D_MODEL), f32), S((T, D_MODEL), bf16),
                   S((T // tm, 8, 128), f32)],
        scratch_shapes=[pltpu.VMEM((tm, D_FF), bf16)],
        compiler_params=_params("parallel"),
    )(x1, g, w1, w2, target)


def _rms_bwd(du, xv, gv):
    r = lax.rsqrt(jnp.mean(xv * xv, axis=-1, keepdims=True) + EPS)
    xn = xv * r
    dg = jnp.sum(du * xn, axis=0, keepdims=True)
    dxn = du * gv
    dx = r * (dxn - xn * jnp.mean(dxn * xn, axis=-1, keepdims=True))
    return dx, dg


def _ffn_bwd(dy, dyb, a, x1, g, w1, w2, tm, ride=None):
    T = dy.shape[0]
    nj = w1.shape[0]

    def body(dy_ref, dyb_ref, a_ref, x_ref, g_ref, w1_ref, w2_ref, da_ref, dx_ref, dxb_ref, dg_ref):
        @pl.when(pl.program_id(0) == 0)
        def _():
            dg_ref[...] = jnp.zeros_like(dg_ref)

        dyb_v = dyb_ref[...]
        du = jnp.zeros((tm, D_MODEL), f32)
        for j in range(nj):
            js = slice(j * FF_CHUNK, (j + 1) * FF_CHUNK)
            dh = _nt(dyb_v, w2_ref[js, :])
            da = (dh * (2.0 * jnp.maximum(a_ref[:, js].astype(f32), 0.0))).astype(bf16)
            da_ref[:, js] = da
            du = du + _nt(da, w1_ref[j])
        dx, dg = _rms_bwd(du, x_ref[...], g_ref[...])
        dx1 = dy_ref[...] + dx
        dx_ref[...] = dx1
        dxb_ref[...] = dx1.astype(bf16)
        dg_ref[...] += dg

    row = pl.BlockSpec((tm, D_MODEL), lambda i: (i, 0))
    wide = pl.BlockSpec((tm, D_FF), lambda i: (i, 0))
    vec = pl.BlockSpec((1, D_MODEL), lambda i: (0, 0))
    return _call(
        body, name="ffn_bwd", grid=(T // tm,),
        in_specs=[row, row, wide, row, _resident((1, D_MODEL)), _resident(w1.shape), _resident(w2.shape)],
        out_specs=[wide, row, row, vec],
        out_shape=[S((T, D_FF), bf16), S((T, D_MODEL), f32), S((T, D_MODEL), bf16), S((1, D_MODEL), f32)],
        args=(dy, dyb, a, x1, g, w1, w2), ride=ride)


def _wgrad(name, a, b, tk, tn, tt, relu2=False, col_shard=False, out_dtype=bf16):
    T, Ka = a.shape
    Nb = b.shape[1]
    nt = T // tt

    def body(a_ref, b_ref, o_ref, acc):
        t = pl.program_id(2)
        av = a_ref[...]
        if relu2:
            av = jnp.square(jnp.maximum(av.astype(f32), 0.0))
        prod = _tn(av.astype(bf16), b_ref[...].astype(bf16))

        @pl.when(t == 0)
        def _():
            acc[...] = prod

        @pl.when(t > 0)
        def _():
            acc[...] += prod

        @pl.when(t == nt - 1)
        def _():
            o_ref[...] = acc[...].astype(out_dtype)

    if col_shard:
        out_shape = S((Nb // tn, Ka, tn), out_dtype)
        out_spec = pl.BlockSpec((None, tk, tn), lambda i, j, t: (j, i, 0))
    else:
        out_shape = S((Ka, Nb), out_dtype)
        out_spec = pl.BlockSpec((tk, tn), lambda i, j, t: (i, j))
    return pl.pallas_call(
        body, name=name, grid=(Ka // tk, Nb // tn, nt),
        in_specs=[pl.BlockSpec((tt, tk), lambda i, j, t: (t, i)), pl.BlockSpec((tt, tn), lambda i, j, t: (t, j))],
        out_specs=out_spec, out_shape=out_shape, scratch_shapes=[pltpu.VMEM((tk, tn), f32)],
        compiler_params=_params("parallel", "parallel", "arbitrary"),
    )(a, b)


def _mix_bwd(dx1, proj, attn, conv, w_ao, w_co, w_o, tm, ride=None):
    T = dx1.shape[0]
    g0 = GLU_END // COL

    def gate_spec(off):
        return pl.BlockSpec((tm, COL), lambda i: (i, g0 + off))

    def body(dx_ref, ga0, ga1, gc0, gc1, at_ref, cv_ref, wa_ref, wc_ref, wo_ref, da_ref, dc_ref, do_ref, dh3_ref, dg_ref):
        dm = _nt(dx_ref[...].astype(bf16), wo_ref[...])
        sa = _sigmoid(jnp.concatenate([ga0[...], ga1[...]], axis=-1))
        sc = _sigmoid(jnp.concatenate([gc0[...], gc1[...]], axis=-1))
        dattn = (dm * sa).astype(bf16)
        dconv = (dm * sc).astype(bf16)
        da_ref[...] = dattn
        dc_ref[...] = dconv
        dg_ref[:, 0:D_MODEL] = (dm * at_ref[...].astype(f32) * sa * (1.0 - sa)).astype(bf16)
        dg_ref[:, D_MODEL:2 * D_MODEL] = (dm * cv_ref[...].astype(f32) * sc * (1.0 - sc)).astype(bf16)
        do_ref[...] = _nt(dattn, wa_ref[...]).astype(bf16)
        dh3_ref[...] = _nt(dconv, wc_ref[...])

    row = pl.BlockSpec((tm, D_MODEL), lambda i: (i, 0))
    wsp = _resident((D_MODEL, D_MODEL))
    return _call(
        body, name="mix_bwd", grid=(T // tm,),
        in_specs=[row, gate_spec(0), gate_spec(1), gate_spec(2), gate_spec(3), row, row, wsp, wsp, wsp],
        out_specs=[row, row, row, row, pl.BlockSpec((tm, 2 * D_MODEL), lambda i: (i, 0))],
        out_shape=[S((T, D_MODEL), bf16), S((T, D_MODEL), bf16), S((T, D_MODEL), bf16), S((T, D_MODEL), f32),
                   S((T, 2 * D_MODEL), bf16)],
        args=(dx1, proj, proj, proj, proj, attn, conv, w_ao, w_co, w_o), ride=ride)


def _norm_act_bwd(dh3, h1, ln_g, ln_b, tm):
    T = dh3.shape[0]

    def body(d_ref, h_ref, g_ref, b_ref, dh1_ref, dg_ref, db_ref, dbd_ref):
        @pl.when(pl.program_id(0) == 0)
        def _():
            dg_ref[...] = jnp.zeros_like(dg_ref)
            db_ref[...] = jnp.zeros_like(db_ref)
            dbd_ref[...] = jnp.zeros_like(dbd_ref)

        xhat, rstd = _layer_norm_stats(h_ref[...])
        h2 = xhat * g_ref[...] + b_ref[...]
        sg = _sigmoid(h2)
        dh2 = d_ref[...] * (sg * (1.0 + h2 * (1.0 - sg)))
        dg_ref[...] += jnp.sum(dh2 * xhat, axis=0, keepdims=True)
        db_ref[...] += jnp.sum(dh2, axis=0, keepdims=True)
        dxh = dh2 * g_ref[...]
        dh1 = rstd * (dxh - jnp.mean(dxh, axis=-1, keepdims=True) - xhat * jnp.mean(dxh * xhat, axis=-1, keepdims=True))
        dh1_ref[...] = dh1
        dbd_ref[...] += jnp.sum(dh1, axis=0, keepdims=True)

    row = pl.BlockSpec((tm, D_MODEL), lambda i: (i, 0))
    vec = pl.BlockSpec((1, D_MODEL), lambda i: (0, 0))
    return pl.pallas_call(
        body, name="norm_act_bwd", grid=(T // tm,),
        in_specs=[row, row, vec, vec], out_specs=[row, vec, vec, vec],
        out_shape=[S((T, D_MODEL), f32), S((1, D_MODEL), f32), S((1, D_MODEL), f32), S((1, D_MODEL), f32)],
        compiler_params=_params("arbitrary"),
    )(dh3, h1, ln_g, ln_b)


def _conv_bwd(dh1, h0, proj, w_dw, tm, ride=None):
    T = dh1.shape[0]
    per = tm // CONV_HALO
    nh = T // CONV_HALO
    nt = T // tm
    a0 = V_END // COL
    lead = CONV_HALO - (CONV_WIDTH - 1)

    def body(dc_ref, dn_ref, hc_ref, hp_ref, a0_ref, a1_ref, g0_ref, g1_ref, w_ref, dglu_ref, dw_ref, dcat, hcat, wacc, dh0):
        i = pl.program_id(0)

        @pl.when(i == 0)
        def _():
            wacc[...] = jnp.zeros_like(wacc)

        dcat[0:tm, :] = dc_ref[...]
        dcat[tm:, :] = jnp.where(i == nt - 1, 0.0, dn_ref[...])
        hcat[0:CONV_HALO, :] = jnp.where(i == 0, 0.0, hp_ref[...])
        hcat[CONV_HALO:, :] = hc_ref[...]
        span = CONV_UNIT + CONV_HALO

        def unit_rows(c, carry):
            r0 = pl.multiple_of(c * CONV_UNIT, CONV_UNIT)
            for j in range(D_MODEL // 128):
                ls = slice(j * 128, (j + 1) * 128)
                dwin = dcat[pl.ds(r0, span), ls]
                acc = jnp.zeros((CONV_UNIT, 128), f32)
                for r, adv in _advanced_windows(dwin):
                    for q, off in _tap_offsets(r, span):
                        k = CONV_WIDTH - 1 - off
                        if 0 <= k < CONV_WIDTH:
                            acc = acc + adv[8 * q:8 * q + CONV_UNIT] * w_ref[k:k + 1, ls]
                dh0[pl.ds(r0, CONV_UNIT), ls] = acc
                dcur = dwin[0:CONV_UNIT]
                for r, adv in _advanced_windows(hcat[pl.ds(r0, span), ls]):
                    for q, off in _tap_offsets(r, span):
                        k = off - lead
                        if 0 <= k < CONV_WIDTH:
                            prod = dcur * adv[8 * q:8 * q + CONV_UNIT]
                            wacc[k, :, ls] += jnp.sum(prod.reshape(CONV_UNIT // 8, 8, 128), axis=0)
            return carry

        lax.fori_loop(0, tm // CONV_UNIT, unit_rows, 0)
        dh0v = dh0[...]
        av = jnp.concatenate([a0_ref[...], a1_ref[...]], axis=-1)
        sg = _sigmoid(jnp.concatenate([g0_ref[...], g1_ref[...]], axis=-1))
        dglu_ref[:, 0:D_MODEL] = (dh0v * sg).astype(bf16)
        dglu_ref[:, D_MODEL:2 * D_MODEL] = (dh0v * av * sg * (1.0 - sg)).astype(bf16)

        @pl.when(i == nt - 1)
        def _():
            for k in range(CONV_WIDTH):
                dw_ref[k:k + 1, :] = jnp.sum(wacc[k], axis=0, keepdims=True)
            dw_ref[CONV_WIDTH:CONV_WIDTH + 1, :] = jnp.zeros((1, D_MODEL), f32)

    row = pl.BlockSpec((tm, D_MODEL), lambda i: (i, 0))

    def col_spec(off):
        return pl.BlockSpec((tm, COL), lambda i: (i, a0 + off))

    return _call(
        body, name="conv_bwd", grid=(nt,),
        in_specs=[row, pl.BlockSpec((CONV_HALO, D_MODEL), lambda i: (jnp.minimum((i + 1) * per, nh - 1), 0)),
                  row, pl.BlockSpec((CONV_HALO, D_MODEL), lambda i: (jnp.maximum(i * per - 1, 0), 0)),
                  col_spec(0), col_spec(1), col_spec(2), col_spec(3),
                  pl.BlockSpec((CONV_WIDTH, D_MODEL), lambda i: (0, 0))],
        out_specs=[pl.BlockSpec((tm, 2 * D_MODEL), lambda i: (i, 0)), pl.BlockSpec((CONV_WIDTH + 1, D_MODEL), lambda i: (0, 0))],
        out_shape=[S((T, 2 * D_MODEL), bf16), S((CONV_WIDTH + 1, D_MODEL), f32)],
        scratch_shapes=[pltpu.VMEM((tm + CONV_HALO, D_MODEL), f32), pltpu.VMEM((tm + CONV_HALO, D_MODEL), f32),
                        pltpu.VMEM((CONV_WIDTH, 8, D_MODEL), f32), pltpu.VMEM((tm, D_MODEL), f32)],
        args=(dh1, dh1, h0, h0, proj, proj, proj, proj, w_dw), ride=ride)


def _attn_bwd(qn, kn, vb, o, do, lse, bias, sinks):
    T = qn.shape[0]
    nb = T // QBLOCK

    def body(q_ref, kc_ref, kp_ref, vc_ref, vp_ref, o_ref, do_ref, lse_ref, b_ref, s_ref,
             dq_ref, dk_ref, dv_ref, db_ref, ds_ref, kcar, vcar):
        n = pl.program_id(0)

        @pl.when(n == 0)
        def _():
            db_ref[...] = jnp.zeros_like(db_ref)
            ds_ref[...] = jnp.zeros_like(ds_ref)
            kcar[...] = jnp.zeros_like(kcar)
            vcar[...] = jnp.zeros_like(vcar)

        @pl.when(n < nb)
        def _():
            pm = _first_block_mask(n)
            qv = q_ref[...]
            dov = do_ref[...]
            dl = dov.astype(f32) * o_ref[...].astype(f32)
            kband = jnp.concatenate([kp_ref[...], kc_ref[...]], axis=0)
            vband = jnp.concatenate([vp_ref[...], vc_ref[...]], axis=0)
            for h in range(N_KV_HEADS):
                hs = slice(h * HEAD_DIM, (h + 1) * HEAD_DIM)
                kh, vh = kband[:, hs], vband[:, hs]

                def stack(v):
                    return jnp.concatenate([v[:, (h * GROUP + g) * HEAD_DIM:(h * GROUP + g + 1) * HEAD_DIM]
                                            for g in range(GROUP)], axis=0)

                q4, do4 = stack(qv), stack(dov)
                delta = jnp.sum(stack(dl), axis=-1, keepdims=True)
                lse4 = jnp.concatenate([lse_ref[:, h * GROUP + g:h * GROUP + g + 1] for g in range(GROUP)], axis=0)
                rows = slice(h * GROUP * QBLOCK, (h + 1) * GROUP * QBLOCK)
                s = _nt(q4, kh) + b_ref[rows, :] + pm
                p = jnp.exp(s - lse4)
                dp = _nt(do4, vh)
                ds = p * (dp - delta)
                db_ref[rows, :] += ds
                dsb = ds.astype(bf16)
                dq4 = jnp.dot(dsb, kh, preferred_element_type=f32)
                dk = _tn(dsb, q4)
                dv = _tn(p.astype(bf16), do4)
                for g in range(GROUP):
                    hq = h * GROUP + g
                    dq_ref[:, hq * HEAD_DIM:(hq + 1) * HEAD_DIM] = dq4[g * QBLOCK:(g + 1) * QBLOCK, :]
                    psink = jnp.exp(s_ref[0, hq] - lse4[g * QBLOCK:(g + 1) * QBLOCK, :])
                    ds_ref[:, hq:hq + 1] -= jnp.sum(psink * delta[g * QBLOCK:(g + 1) * QBLOCK, :], axis=0, keepdims=True)
                dk_ref[:, hs] = kcar[:, hs] + dk[0:QBLOCK, :]
                dv_ref[:, hs] = vcar[:, hs] + dv[0:QBLOCK, :]
                kcar[:, hs] = dk[QBLOCK:, :]
                vcar[:, hs] = dv[QBLOCK:, :]

        @pl.when(n == nb)
        def _():
            dk_ref[...] = kcar[...]
            dv_ref[...] = vcar[...]

    cur = lambda n: (jnp.minimum(n, nb - 1), 0)
    prev = lambda n: (jnp.clip(n - 1, 0, nb - 1), 0)
    qspec = pl.BlockSpec((QBLOCK, ATTN_WIDTH), cur)
    kcur, kprev = pl.BlockSpec((QBLOCK, KV_WIDTH), cur), pl.BlockSpec((QBLOCK, KV_WIDTH), prev)
    bspec = pl.BlockSpec((N_Q_HEADS * QBLOCK, 2 * QBLOCK), lambda n: (0, 0))
    return pl.pallas_call(
        body, name="attn_bwd", grid=(nb + 1,),
        in_specs=[qspec, kcur, kprev, kcur, kprev, qspec, qspec, pl.BlockSpec((QBLOCK, N_Q_HEADS), cur), bspec, SMEM],
        out_specs=[qspec, kprev, kprev, bspec, pl.BlockSpec((1, N_Q_HEADS), lambda n: (0, 0))],
        out_shape=[S((T, ATTN_WIDTH), f32), S((T, KV_WIDTH), f32), S((T, KV_WIDTH), f32),
                   S((N_Q_HEADS * QBLOCK, 2 * QBLOCK), f32), S((1, N_Q_HEADS), f32)],
        scratch_shapes=[pltpu.VMEM((QBLOCK, KV_WIDTH), f32), pltpu.VMEM((QBLOCK, KV_WIDTH), f32)],
        compiler_params=_params("arbitrary"),
    )(qn, kn, kn, vb, vb, o, do, lse, bias, sinks)


def _rel_bias_bwd(dbias, bucket):
    def body(d_ref, bk_ref, o_ref):
        b = bk_ref[...]
        for k in range(N_BUCKETS):
            mk = b == k
            for h in range(N_Q_HEADS):
                o_ref[k, h] = jnp.sum(jnp.where(mk, d_ref[h * QBLOCK:(h + 1) * QBLOCK, :], 0.0))

    return pl.pallas_call(body, name="rel_bias_bwd", out_shape=S((N_BUCKETS, N_Q_HEADS), f32), out_specs=SMEM)(dbias, bucket)


def _qk_norm_bwd(dq, dk, dv, proj, qg, kg, tm):
    T = dq.shape[0]
    scale = HEAD_DIM ** -0.5

    def head_bwd(dseg, seg, gv):
        r = lax.rsqrt(jnp.mean(seg * seg, axis=-1, keepdims=True) + EPS)
        xn = seg * r
        dg = jnp.sum(dseg * xn, axis=0, keepdims=True)
        dxn = dseg * gv
        return r * (dxn - xn * jnp.mean(dxn * xn, axis=-1, keepdims=True)), dg

    def body(dq_ref, dk_ref, dv_ref, p_ref, qg_ref, kg_ref, out_ref, dqg_ref, dkg_ref):
        @pl.when(pl.program_id(0) == 0)
        def _():
            dqg_ref[...] = jnp.zeros_like(dqg_ref)
            dkg_ref[...] = jnp.zeros_like(dkg_ref)

        qgv, kgv = qg_ref[...], kg_ref[...]
        dqg = jnp.zeros((1, HEAD_DIM), f32)
        for h in range(N_Q_HEADS):
            hs = slice(h * HEAD_DIM, (h + 1) * HEAD_DIM)
            dx, dg = head_bwd(dq_ref[:, hs] * scale, p_ref[:, hs], qgv)
            out_ref[:, hs] = dx.astype(bf16)
            dqg = dqg + dg
        dkg = jnp.zeros((1, HEAD_DIM), f32)
        for h in range(N_KV_HEADS):
            hs = slice(h * HEAD_DIM, (h + 1) * HEAD_DIM)
            ps = slice(Q_END + h * HEAD_DIM, Q_END + (h + 1) * HEAD_DIM)
            dx, dg = head_bwd(dk_ref[:, hs], p_ref[:, ps], kgv)
            out_ref[:, ps] = dx.astype(bf16)
            dkg = dkg + dg
        out_ref[:, K_END:V_END] = dv_ref[...].astype(bf16)
        dqg_ref[...] += dqg
        dkg_ref[...] += dkg

    vec = pl.BlockSpec((1, HEAD_DIM), lambda i: (0, 0))
    return pl.pallas_call(
        body, name="qk_norm_bwd", grid=(T // tm,),
        in_specs=[pl.BlockSpec((tm, ATTN_WIDTH), lambda i: (i, 0)), pl.BlockSpec((tm, KV_WIDTH), lambda i: (i, 0)),
                  pl.BlockSpec((tm, KV_WIDTH), lambda i: (i, 0)), pl.BlockSpec((tm, V_END), lambda i: (i, 0)), vec, vec],
        out_specs=[pl.BlockSpec((tm, V_END), lambda i: (i, 0)), vec, vec],
        out_shape=[S((T, V_END), bf16), S((1, HEAD_DIM), f32), S((1, HEAD_DIM), f32)],
        compiler_params=_params("arbitrary"),
    )(dq, dk, dv, proj, qg, kg)


def _in_bwd(dqkv, dglu, dgates, w_in, x, g, dx1, tm, ride=None):
    T = x.shape[0]
    pieces = (dqkv, dglu, dgates)
    starts = [0, dqkv.shape[1], dqkv.shape[1] + dglu.shape[1]]

    def body(a0_ref, a1_ref, a2_ref, w_ref, x_ref, g_ref, d_ref, gx_ref, dg_ref):
        @pl.when(pl.program_id(0) == 0)
        def _():
            dg_ref[...] = jnp.zeros_like(dg_ref)

        du = jnp.zeros((tm, D_MODEL), f32)
        for a_ref, c0 in zip((a0_ref, a1_ref, a2_ref), starts):
            du = du + _nt(a_ref[...], w_ref[:, c0:c0 + a_ref.shape[1]])
        dx, dg = _rms_bwd(du, x_ref[...], g_ref[...])
        gx_ref[...] = d_ref[...] + dx
        dg_ref[...] += dg

    row = pl.BlockSpec((tm, D_MODEL), lambda i: (i, 0))
    return _call(
        body, name="in_bwd", grid=(T // tm,),
        in_specs=[pl.BlockSpec((tm, p.shape[1]), lambda i: (i, 0)) for p in pieces]
        + [_resident(w_in.shape), row, _resident((1, D_MODEL)), row],
        out_specs=[row, pl.BlockSpec((1, D_MODEL), lambda i: (0, 0))],
        out_shape=[S((T, D_MODEL), f32), S((1, D_MODEL), f32)],
        args=(dqkv, dglu, dgates, w_in, x, g, dx1), ride=ride)


def _adamw(name, parts, w, m, v, tr):
    R, C = w.shape
    bc1 = 1.0 - ADAM_B1 ** ADAM_STEP
    bc2 = 1.0 - ADAM_B2 ** ADAM_STEP

    def body(p_ref, w_ref, m_ref, v_ref, g_ref, d_ref, nm_ref, nv_ref):
        g = p_ref[0].astype(f32)
        for k in range(1, N_DEV):
            g = g + p_ref[k].astype(f32)
        nm = ADAM_B1 * m_ref[...] + (1.0 - ADAM_B1) * g
        nv = ADAM_B2 * v_ref[...] + (1.0 - ADAM_B2) * (g * g)
        g_ref[...] = g
        nm_ref[...] = nm
        nv_ref[...] = nv
        d_ref[...] = -ADAM_LR * ((nm / bc1) / (jnp.sqrt(nv / bc2) + ADAM_EPS) + ADAM_WD * w_ref[...])

    blk = pl.BlockSpec((tr, C), lambda i: (i, 0))
    return pl.pallas_call(
        body, name=name, grid=(R // tr,),
        in_specs=[pl.BlockSpec((N_DEV, tr, C), lambda i: (0, i, 0)), blk, blk, blk],
        out_specs=[blk, blk, blk, blk], out_shape=[S((R, C), f32)] * 4,
        compiler_params=_params("parallel"),
    )(parts, w, m, v)


def _tile(T, pref):
    return min(T, pref)


def _pad_rows(a, rows):
    return jnp.pad(a, ((0, rows - a.shape[0]), (0, 0)))


def kernel(x, norm_mix_g, w_in, q_norm_g, k_norm_g, attn_sinks, rel_bias, w_attn_o, w_dw, b_dw, conv_ln_g, conv_ln_b, w_conv_out, w_out, norm_mlp_g, w_ff1, w_ff2, loss_target, m_norm_mix_g, m_w_in, m_q_norm_g, m_k_norm_g, m_attn_sinks, m_rel_bias, m_w_attn_o, m_w_dw, m_b_dw, m_conv_ln_g, m_conv_ln_b, m_w_conv_out, m_w_out, m_norm_mlp_g, m_w_ff1, m_w_ff2, v_norm_mix_g, v_w_in, v_q_norm_g, v_k_norm_g, v_attn_sinks, v_rel_bias, v_w_attn_o, v_w_dw, v_b_dw, v_conv_ln_g, v_conv_ln_b, v_w_conv_out, v_w_out, v_norm_mlp_g, v_w_ff1, v_w_ff2):
    T = x.shape[1]
    xs = x[0]
    tgt = loss_target[0]
    in_shard = IN_WIDTH // N_DEV
    dw_rows = CONV_WIDTH + 1
    ch_shard = D_MODEL // N_DEV
    tc = _tile(T, 256)
    tt = _tile(T, 2048)
    bucket = jnp.asarray(_t5_bucket_table())

    g_in, g_dw = _exchange("gather_w_in", [w_in[0].astype(bf16), _pad_rows(w_dw[0], dw_rows)], gather=True)
    W_in = jnp.transpose(g_in, (1, 0, 2)).reshape(D_MODEL, IN_WIDTH)
    W_dw = jnp.transpose(g_dw, (1, 0, 2)).reshape(dw_rows, D_MODEL)[:CONV_WIDTH]

    mix_shards = _Exchange([w_attn_o[0].astype(bf16), w_conv_out[0].astype(bf16), w_out[0].astype(bf16)], gather=True)
    (proj, u), (g_ao, g_co, g_o) = _proj_fwd(xs, norm_mix_g, W_in, tc, ride=mix_shards)
    W_ao = g_ao.reshape(D_MODEL, D_MODEL)
    W_co = g_co.reshape(D_MODEL, D_MODEL)
    W_o = g_o.reshape(D_MODEL, D_MODEL)
    qn, kn, vb, h0 = _prep_fwd(proj, q_norm_g, k_norm_g, tc)
    bias = _bias_table(rel_bias, bucket)
    ffn_shards = _Exchange([w_ff1[0].astype(bf16), w_ff2[0].astype(bf16)], gather=True)
    (o, lse), (g_f1, g_f2) = _attn_fwd(qn, kn, vb, bias, attn_sinks, ride=ffn_shards)
    h1, h3 = _conv_fwd(h0, W_dw, b_dw, conv_ln_g, conv_ln_b, tc)
    x1, attn, conv, merged = _mix_fwd(xs, o, h3, proj, W_ao, W_co, W_o, tc)
    W_f2 = g_f2.reshape(D_FF, D_MODEL)
    a, u2, dy, dyb, loss_parts = _ffn_fwd(x1, norm_mlp_g, g_f1, W_f2, tgt, tc)
    loss = lax.psum(jnp.sum(loss_parts[:, 0, 0]), ("x", "y", "c"))

    gw_f2 = _wgrad("wgrad_ff2", a, dyb, D_MODEL, D_MODEL, tt, relu2=True).reshape(N_DEV, FF_CHUNK, D_MODEL)
    (da, dx1, dx1b, d_norm_mlp_g), (l_f2,) = _ffn_bwd(dy, dyb, a, x1, norm_mlp_g, g_f1, W_f2, tc,
                                                      ride=_Exchange([gw_f2], gather=False))
    gw_f1 = _wgrad("wgrad_ff1", u2, da, D_MODEL, FF_CHUNK, tt, col_shard=True)
    gw_o = _wgrad("wgrad_out", merged, dx1b, D_MODEL, D_MODEL, tt).reshape(N_DEV, ch_shard, D_MODEL)
    (dattn, dconv, do, dh3, dgates), (l_f1, l_o) = _mix_bwd(dx1b, proj, attn, conv, W_ao, W_co, W_o, tc,
                                                             ride=_Exchange([gw_f1, gw_o], gather=False))
    gw_ao = _wgrad("wgrad_attn_o", o, dattn, D_MODEL, D_MODEL, tt).reshape(N_DEV, ch_shard, D_MODEL)
    gw_co = _wgrad("wgrad_conv_out", h3, dconv, D_MODEL, D_MODEL, tt).reshape(N_DEV, ch_shard, D_MODEL)
    dh1, d_ln_g, d_ln_b, d_b_dw = _norm_act_bwd(dh3, h1, conv_ln_g, conv_ln_b, tc)
    (dglu, d_w_dw), (l_ao, l_co) = _conv_bwd(dh1, h0, proj, W_dw, tc, ride=_Exchange([gw_ao, gw_co], gather=False))
    dq, dk, dv, dbias, d_sinks = _attn_bwd(qn, kn, vb, o, do, lse, bias, attn_sinks)
    d_rel_bias = _rel_bias_bwd(dbias, bucket)
    dqkv, d_qg, d_kg = _qk_norm_bwd(dq, dk, dv, proj, q_norm_g, k_norm_g, tc)
    gw_in = jnp.concatenate([_wgrad("wgrad_in_qkv", u, dqkv, D_MODEL, COL, tt),
                             _wgrad("wgrad_in_glu", u, dglu, D_MODEL, D_MODEL, tt),
                             _wgrad("wgrad_in_gates", u, dgates, D_MODEL, D_MODEL, tt)], axis=1)
    gw_in = jnp.transpose(gw_in.reshape(D_MODEL, N_DEV, in_shard), (1, 0, 2))
    gw_dw = jnp.transpose(d_w_dw.reshape(dw_rows, N_DEV, ch_shard), (1, 0, 2))
    (grad_x, d_norm_mix_g), (l_in, l_dw) = _in_bwd(dqkv, dglu, dgates, W_in, xs, norm_mix_g, dx1, tc,
                                                    ride=_Exchange([gw_in, gw_dw], gather=False))

    def row(vec):
        flat = vec.reshape(1, -1)
        return jnp.pad(flat, ((0, 0), (0, D_MODEL - flat.shape[1])))

    def pack_small(nm, qg, kg, sk, rb, bd, lg, lb, nl):
        tail = jnp.concatenate([qg.reshape(1, -1), kg.reshape(1, -1), sk.reshape(1, -1), rb.reshape(1, -1)], axis=1)
        return jnp.concatenate([row(nm), row(bd), row(lg), row(lb), row(nl), row(tail), jnp.zeros((2, D_MODEL), f32)], axis=0)

    def unpack_small(p):
        t = p[5]
        o0, o1, o2 = HEAD_DIM, 2 * HEAD_DIM, 2 * HEAD_DIM + N_Q_HEADS
        return dict(norm_mix_g=p[0:1], b_dw=p[1:2], conv_ln_g=p[2:3], conv_ln_b=p[3:4], norm_mlp_g=p[4:5],
                    q_norm_g=t[0:o0].reshape(1, HEAD_DIM), k_norm_g=t[o0:o1].reshape(1, HEAD_DIM),
                    attn_sinks=t[o1:o2].reshape(1, N_Q_HEADS),
                    rel_bias=t[o2:o2 + N_BUCKETS * N_Q_HEADS].reshape(N_BUCKETS, N_Q_HEADS))

    small_g = pack_small(d_norm_mix_g, d_qg, d_kg, d_sinks, d_rel_bias, d_b_dw, d_ln_g, d_ln_b, d_norm_mlp_g)
    (l_small,) = _exchange("gather_small_grads", [small_g], gather=True)


    res = {}
    res["w_in"] = _adamw("adamw_in", l_in, w_in[0], m_w_in[0], v_w_in[0], 256)
    res["w_attn_o"] = _adamw("adamw_attn_o", l_ao, w_attn_o[0], m_w_attn_o[0], v_w_attn_o[0], ch_shard)
    res["w_conv_out"] = _adamw("adamw_conv_out", l_co, w_conv_out[0], m_w_conv_out[0], v_w_conv_out[0], ch_shard)
    res["w_out"] = _adamw("adamw_out", l_o, w_out[0], m_w_out[0], v_w_out[0], ch_shard)
    res["w_ff1"] = _adamw("adamw_ff1", l_f1, w_ff1[0], m_w_ff1[0], v_w_ff1[0], 256)
    res["w_ff2"] = _adamw("adamw_ff2", l_f2, w_ff2[0], m_w_ff2[0], v_w_ff2[0], 256)
    dw4 = _adamw("adamw_dw", l_dw, _pad_rows(w_dw[0], dw_rows), _pad_rows(m_w_dw[0], dw_rows), _pad_rows(v_w_dw[0], dw_rows), dw_rows)
    res["w_dw"] = [t[:CONV_WIDTH] for t in dw4]
    small_w = pack_small(norm_mix_g, q_norm_g, k_norm_g, attn_sinks, rel_bias, b_dw, conv_ln_g, conv_ln_b, norm_mlp_g)
    small_m = pack_small(m_norm_mix_g, m_q_norm_g, m_k_norm_g, m_attn_sinks, m_rel_bias, m_b_dw, m_conv_ln_g, m_conv_ln_b, m_norm_mlp_g)
    small_v = pack_small(v_norm_mix_g, v_q_norm_g, v_k_norm_g, v_attn_sinks, v_rel_bias, v_b_dw, v_conv_ln_g, v_conv_ln_b, v_norm_mlp_g)
    small4 = [unpack_small(t) for t in _adamw("adamw_small", l_small, small_w, small_m, small_v, 8)]

    order = ["norm_mix_g", "w_in", "q_norm_g", "k_norm_g", "attn_sinks", "rel_bias", "w_attn_o", "w_dw", "b_dw",
             "conv_ln_g", "conv_ln_b", "w_conv_out", "w_out", "norm_mlp_g", "w_ff1", "w_ff2"]
    stacked = {"w_in", "w_attn_o", "w_dw", "w_conv_out", "w_out", "w_ff1", "w_ff2"}
    outs = [loss, grad_x[None]]
    for k in range(4):
        for nme in order:
            if nme in stacked:
                outs.append(res[nme][k][None])
            else:
                outs.append(small4[k][nme])
    return tuple(outs)
```

```python
import functools

import numpy as np
import jax
import jax.numpy as jnp
from jax import lax
from jax.experimental import pallas as pl
from jax.experimental.pallas import tpu as pltpu

f32 = jnp.float32
bf16 = jnp.bfloat16
S = jax.ShapeDtypeStruct

N_DEV = 8
D_MODEL = 1024
HEAD_DIM = 64
N_Q_HEADS = 16
N_KV_HEADS = 4
GROUP = N_Q_HEADS // N_KV_HEADS
ATTN_WIDTH = N_Q_HEADS * HEAD_DIM
KV_WIDTH = N_KV_HEADS * HEAD_DIM
QBLOCK = 128
CONV_WIDTH = 31
CONV_HALO = 32
CONV_UNIT = 64
D_FF = 4 * D_MODEL
N_BUCKETS = 32
MAX_DISTANCE = 128
EPS = 1e-6
NEG = -1e30
Q_END = ATTN_WIDTH
K_END = Q_END + KV_WIDTH
V_END = K_END + KV_WIDTH
GLU_END = V_END + 2 * D_MODEL
IN_WIDTH = GLU_END + 2 * D_MODEL
COL = 512
FF_CHUNK = D_FF // N_DEV

ADAM_LR = 0.001
ADAM_B1 = 0.9
ADAM_B2 = 0.999
ADAM_EPS = 1e-08
ADAM_WD = 0.01
ADAM_STEP = 10

VMEM_LIMIT = 56 * 1024 * 1024

MESH_ID = pl.DeviceIdType.MESH
ANY = pl.BlockSpec(memory_space=pl.ANY)
SMEM = pl.BlockSpec(memory_space=pltpu.SMEM)


def _params(*sem):
    return pltpu.CompilerParams(dimension_semantics=sem, vmem_limit_bytes=VMEM_LIMIT)


def _nt(a, b):
    return lax.dot_general(a, b, (((1,), (1,)), ((), ())), preferred_element_type=f32)


def _tn(a, b):
    return lax.dot_general(a, b, (((0,), (0,)), ((), ())), preferred_element_type=f32)


def _sigmoid(z):
    return 1.0 / (1.0 + jnp.exp(-z))


def _t5_bucket_table():
    qi = np.arange(QBLOCK, dtype=np.int32)[:, None]
    kj = np.arange(2 * QBLOCK, dtype=np.int32)[None, :]
    dist = qi + QBLOCK - kj
    n = np.maximum(dist, 0)
    max_exact = N_BUCKETS // 2
    nf = np.maximum(n, 1).astype(np.float32)
    large = max_exact + (np.log(nf / np.float32(max_exact)) / np.float32(np.log(MAX_DISTANCE / max_exact))
                         * np.float32(N_BUCKETS - max_exact)).astype(np.int32)
    large = np.minimum(large, N_BUCKETS - 1)
    bucket = np.where(n < max_exact, n, large)
    valid = (dist >= 0) & (dist < QBLOCK)
    return np.where(valid, bucket, -1).astype(np.int32)


def _peer(d):
    x, y, c = lax.axis_index("x"), lax.axis_index("y"), lax.axis_index("c")
    dx, dy, dc = (d >> 2) & 1, (d >> 1) & 1, d & 1
    px, py, pc = x ^ dx, y ^ dy, c ^ dc
    return (px, py, pc), 4 * px + 2 * py + pc


class _Exchange:
    def __init__(self, arrays, gather):
        self.arrays, self.gather, self.n = list(arrays), gather, len(arrays)
        self.out_shape = [S(((N_DEV,) + a.shape) if gather else a.shape, a.dtype) for a in self.arrays]
        self.scratch = [pltpu.SemaphoreType.DMA((self.n, N_DEV - 1)), pltpu.SemaphoreType.DMA((self.n, N_DEV - 1)),
                        pltpu.SemaphoreType.DMA((self.n,))]

    def _copies(self, ins, outs, sems):
        send_sems, recv_sems, local_sems = sems
        _, me = _peer(0)
        local, sends, recvs = [], [], []
        for k in range(self.n):
            src = ins[k] if self.gather else ins[k].at[me]
            local.append(pltpu.make_async_copy(src, outs[k].at[me], local_sems.at[k]))
        for d in range(1, N_DEV):
            peer, pidx = _peer(d)
            for k in range(self.n):
                src = ins[k] if self.gather else ins[k].at[pidx]
                common = dict(src_ref=src, send_sem=send_sems.at[k, d - 1], recv_sem=recv_sems.at[k, d - 1],
                              device_id=peer, device_id_type=MESH_ID)
                sends.append(pltpu.make_async_remote_copy(dst_ref=outs[k].at[me], **common))
                recvs.append(pltpu.make_async_remote_copy(dst_ref=outs[k].at[pidx], **common))
        return local, sends, recvs

    def start(self, ins, outs, sems):
        local, sends, _ = self._copies(ins, outs, sems)
        for cp in local + sends:
            cp.start()

    def wait(self, ins, outs, sems):
        local, sends, recvs = self._copies(ins, outs, sems)
        for cp in recvs:
            cp.wait_recv()
        for cp in sends:
            cp.wait_send()
        for cp in local:
            cp.wait()


def _exchange(name, arrays, gather):
    ex = _Exchange(arrays, gather)
    n = ex.n

    def body(*refs):
        ins, outs, sems = refs[:n], refs[n:2 * n], refs[2 * n:]
        ex.start(ins, outs, sems)
        ex.wait(ins, outs, sems)

    return pl.pallas_call(body, name=name, out_shape=ex.out_shape, in_specs=[ANY] * n, out_specs=[ANY] * n,
                          scratch_shapes=ex.scratch)(*arrays)


def _call(body, *, name, grid, in_specs, out_specs, out_shape, args, scratch_shapes=(), ride=None):
    n_in, n_out, n_sc = len(in_specs), len(out_specs), len(scratch_shapes)
    sem = ("arbitrary",) * len(grid)
    if ride is None:
        res = pl.pallas_call(body, name=name, grid=grid, in_specs=list(in_specs), out_specs=list(out_specs),
                             out_shape=list(out_shape), scratch_shapes=list(scratch_shapes), compiler_params=_params(*sem))(*args)
        return list(res), []
    nx = ride.n

    def riding(*refs):
        ins, xin = refs[:n_in], refs[n_in:n_in + nx]
        outs, xout = refs[n_in + nx:n_in + nx + n_out], refs[n_in + nx + n_out:n_in + 2 * nx + n_out]
        rest = refs[n_in + 2 * nx + n_out:]
        scratch, sems = rest[:n_sc], rest[n_sc:]
        ids = [pl.program_id(ax) for ax in range(len(grid))]
        first = functools.reduce(jnp.logical_and, [i == 0 for i in ids])
        last = functools.reduce(jnp.logical_and, [i == g - 1 for i, g in zip(ids, grid)])

        @pl.when(first)
        def _():
            ride.start(xin, xout, sems)

        body(*ins, *outs, *scratch)

        @pl.when(last)
        def _():
            ride.wait(xin, xout, sems)

    res = pl.pallas_call(
        riding, name=name, grid=grid, in_specs=list(in_specs) + [ANY] * nx, out_specs=list(out_specs) + [ANY] * nx,
        out_shape=list(out_shape) + ride.out_shape, scratch_shapes=list(scratch_shapes) + ride.scratch,
        compiler_params=_params(*sem))(*args, *ride.arrays)
    return list(res[:n_out]), list(res[n_out:])


def _resident(shape):
    return pl.BlockSpec(shape, lambda *_: (0,) * len(shape), pipeline_mode=pl.Buffered(1))


def _proj_fwd(x, g, w, tm, ride=None):
    T, K = x.shape
    N = w.shape[1]

    def body(x_ref, g_ref, w_ref, o_ref, u_ref):
        xv = x_ref[...]
        r = lax.rsqrt(jnp.mean(xv * xv, axis=-1, keepdims=True) + EPS)
        u = (xv * r * g_ref[...]).astype(bf16)
        u_ref[...] = u
        for c in range(N // COL):
            cs = slice(c * COL, (c + 1) * COL)
            o_ref[:, cs] = jnp.dot(u, w_ref[:, cs], preferred_element_type=f32)

    return _call(
        body, name="proj_fwd", grid=(T // tm,),
        in_specs=[pl.BlockSpec((tm, K), lambda i: (i, 0)), _resident((1, K)), _resident((K, N))],
        out_specs=[pl.BlockSpec((tm, N), lambda i: (i, 0)), pl.BlockSpec((tm, K), lambda i: (i, 0))],
        out_shape=[S((T, N), f32), S((T, K), bf16)], args=(x, g, w), ride=ride)


def _prep_fwd(proj, qg, kg, tm):
    T = proj.shape[0]

    def body(p_ref, qg_ref, kg_ref, qn_ref, kn_ref, vb_ref, h0_ref):
        qgv = qg_ref[...] * (HEAD_DIM ** -0.5)
        kgv = kg_ref[...]
        for pr in range(N_Q_HEADS // 2):
            x = p_ref[:, _pair_cols(pr)]
            qn_ref[:, _pair_cols(pr)] = (x * _pair_rstd(x) * qgv).astype(bf16)
        for pr in range(N_KV_HEADS // 2):
            x = p_ref[:, Q_END + pr * 2 * HEAD_DIM:Q_END + (pr + 1) * 2 * HEAD_DIM]
            kn_ref[:, _pair_cols(pr)] = (x * _pair_rstd(x) * kgv).astype(bf16)
        vb_ref[...] = p_ref[:, K_END:V_END].astype(bf16)
        h0_ref[...] = p_ref[:, V_END:V_END + D_MODEL] * _sigmoid(p_ref[:, V_END + D_MODEL:GLU_END])

    return pl.pallas_call(
        body, name="prep_fwd", grid=(T // tm,),
        in_specs=[pl.BlockSpec((tm, GLU_END), lambda i: (i, 0)), pl.BlockSpec((1, 2 * HEAD_DIM), lambda i: (0, 0)),
                  pl.BlockSpec((1, 2 * HEAD_DIM), lambda i: (0, 0))],
        out_specs=[pl.BlockSpec((tm, ATTN_WIDTH), lambda i: (i, 0)), pl.BlockSpec((tm, KV_WIDTH), lambda i: (i, 0)),
                   pl.BlockSpec((tm, KV_WIDTH), lambda i: (i, 0)), pl.BlockSpec((tm, D_MODEL), lambda i: (i, 0))],
        out_shape=[S((T, ATTN_WIDTH), bf16), S((T, KV_WIDTH), bf16), S((T, KV_WIDTH), bf16), S((T, D_MODEL), f32)],
        compiler_params=_params("parallel"),
    )(proj, qg, kg)


def _bias_table(rel_bias, bucket):
    def body(rb_ref, bk_ref, o_ref):
        b = bk_ref[...]
        absent = lax.broadcasted_iota(jnp.int32, (QBLOCK, 2 * QBLOCK), 1) < QBLOCK
        for h in range(N_Q_HEADS):
            acc = jnp.full((QBLOCK, 2 * QBLOCK), NEG, f32)
            for k in range(N_BUCKETS):
                acc = jnp.where(b == k, rb_ref[k, h], acc)
            o_ref[0, h * QBLOCK:(h + 1) * QBLOCK, :] = acc
            o_ref[1, h * QBLOCK:(h + 1) * QBLOCK, :] = jnp.where(absent, NEG, acc)

    return pl.pallas_call(
        body, name="bias_table", out_shape=S((2, N_Q_HEADS * QBLOCK, 2 * QBLOCK), f32),
        in_specs=[SMEM, pl.BlockSpec(memory_space=pltpu.VMEM)],
    )(rel_bias, bucket)


def _bias_spec():
    return pl.BlockSpec((None, N_Q_HEADS * QBLOCK, 2 * QBLOCK), lambda n: (jnp.where(n == 0, 1, 0), 0, 0))


def _swap_halves(t):
    return jnp.concatenate([t[:, HEAD_DIM:], t[:, :HEAD_DIM]], axis=1)


def _low_lanes():
    return lax.broadcasted_iota(jnp.int32, (1, 2 * HEAD_DIM), 1) < HEAD_DIM


def _one_head(pair, side):
    zero = jnp.zeros((), pair.dtype)
    return jnp.where(_low_lanes(), pair, zero) if side == 0 else jnp.where(_low_lanes(), zero, pair)


def _pair_mean(t):
    low = _low_lanes()
    m_lo = jnp.sum(_one_head(t, 0), axis=-1, keepdims=True) * (1.0 / HEAD_DIM)
    m_hi = jnp.sum(_one_head(t, 1), axis=-1, keepdims=True) * (1.0 / HEAD_DIM)
    return jnp.where(low, m_lo, m_hi)


def _pair_rstd(x):
    return lax.rsqrt(_pair_mean(x * x) + EPS)


def _kv_placements(band):
    out = {}
    for m in range(N_KV_HEADS // 2):
        pair = band[:, m * 2 * HEAD_DIM:(m + 1) * 2 * HEAD_DIM]
        swapped = _swap_halves(pair)
        for hh in range(2):
            out[2 * m + hh, 0] = _one_head(pair if hh == 0 else swapped, 0)
            out[2 * m + hh, 1] = _one_head(swapped if hh == 0 else pair, 1)
    return out


def _head_rows(hq):
    return slice(hq * QBLOCK, (hq + 1) * QBLOCK)


def _pair_cols(pr):
    return slice(pr * 2 * HEAD_DIM, (pr + 1) * 2 * HEAD_DIM)


def _attn_fwd(qn, kn, vb, bias, sinks, ride=None):
    T = qn.shape[0]
    nb = T // QBLOCK

    def body(q_ref, kc_ref, kp_ref, vc_ref, vp_ref, b_ref, s_ref, o_ref, lse_ref, s_scr, p_scr):
        lane = lax.broadcasted_iota(jnp.int32, (QBLOCK, 2 * HEAD_DIM), 1)
        kx = _kv_placements(jnp.concatenate([kp_ref[...], kc_ref[...]], axis=0))
        vx = _kv_placements(jnp.concatenate([vp_ref[...], vc_ref[...]], axis=0))
        for hq in range(N_Q_HEADS):
            qm = _one_head(q_ref[:, _pair_cols(hq // 2)], hq % 2)
            s_scr[_head_rows(hq), :] = _nt(qm, kx[hq // GROUP, hq % 2]) + b_ref[_head_rows(hq), :]
        lse_tile = jnp.zeros((QBLOCK, 2 * HEAD_DIM), f32)
        for hq in range(N_Q_HEADS):
            s = s_scr[_head_rows(hq), :]
            sink = s_ref[0, hq]
            m = jnp.maximum(jnp.max(s, axis=-1, keepdims=True), sink)
            p = jnp.exp(s - m)
            l = jnp.sum(p, axis=-1, keepdims=True) + jnp.exp(sink - m)
            p_scr[_head_rows(hq), :] = (p * (1.0 / l)).astype(bf16)
            lse_tile = jnp.where(lane == hq, m + jnp.log(l), lse_tile)
        lse_ref[...] = lse_tile
        for pr in range(N_Q_HEADS // 2):
            h = 2 * pr // GROUP
            o_pair = (jnp.dot(p_scr[_head_rows(2 * pr), :], vx[h, 0], preferred_element_type=f32)
                      + jnp.dot(p_scr[_head_rows(2 * pr + 1), :], vx[h, 1], preferred_element_type=f32))
            o_ref[:, _pair_cols(pr)] = o_pair.astype(bf16)

    cur = lambda n: (n, 0)
    prev = lambda n: (jnp.maximum(n - 1, 0), 0)
    return _call(
        body, name="attn_fwd", grid=(nb,),
        in_specs=[pl.BlockSpec((QBLOCK, ATTN_WIDTH), cur), pl.BlockSpec((QBLOCK, KV_WIDTH), cur),
                  pl.BlockSpec((QBLOCK, KV_WIDTH), prev), pl.BlockSpec((QBLOCK, KV_WIDTH), cur),
                  pl.BlockSpec((QBLOCK, KV_WIDTH), prev), _bias_spec(), SMEM],
        out_specs=[pl.BlockSpec((QBLOCK, ATTN_WIDTH), cur), pl.BlockSpec((QBLOCK, 2 * HEAD_DIM), cur)],
        out_shape=[S((T, ATTN_WIDTH), bf16), S((T, 2 * HEAD_DIM), f32)],
        scratch_shapes=[pltpu.VMEM((N_Q_HEADS * QBLOCK, 2 * QBLOCK), f32), pltpu.VMEM((N_Q_HEADS * QBLOCK, 2 * QBLOCK), bf16)],
        args=(qn, kn, kn, vb, vb, bias, sinks), ride=ride)


def _layer_norm_stats(h1):
    mu = jnp.mean(h1, axis=-1, keepdims=True)
    xc = h1 - mu
    rstd = lax.rsqrt(jnp.mean(xc * xc, axis=-1, keepdims=True) + EPS)
    return xc * rstd, rstd


def _advanced_windows(win):
    rows = win.shape[0]
    for r in range(8):
        yield r, (win if r == 0 else pltpu.roll(win, rows - r, 0))


def _tap_offsets(r, rows):
    for q in range((rows - CONV_UNIT) // 8 + 1):
        if r == 0 or 8 * q + r + CONV_UNIT <= rows:
            yield q, 8 * q + r


def _conv_fwd(h0, w_dw, b_dw, ln_g, ln_b, tm):
    T = h0.shape[0]
    per = tm // CONV_HALO
    lead = CONV_HALO - (CONV_WIDTH - 1)

    def body(hc_ref, hp_ref, w_ref, b_ref, g_ref, bb_ref, h1_ref, h3_ref, cat):
        i = pl.program_id(0)
        cat[0:CONV_HALO, :] = jnp.where(i == 0, 0.0, hp_ref[...])
        cat[CONV_HALO:, :] = hc_ref[...]

        def unit_rows(c, carry):
            r0 = pl.multiple_of(c * CONV_UNIT, CONV_UNIT)
            for j in range(D_MODEL // 128):
                ls = slice(j * 128, (j + 1) * 128)
                win = cat[pl.ds(r0, CONV_UNIT + CONV_HALO), ls]
                acc = jnp.zeros((CONV_UNIT, 128), f32) + b_ref[:, ls]
                for r, adv in _advanced_windows(win):
                    for q, off in _tap_offsets(r, CONV_UNIT + CONV_HALO):
                        k = off - lead
                        if 0 <= k < CONV_WIDTH:
                            acc = acc + adv[8 * q:8 * q + CONV_UNIT] * w_ref[k:k + 1, ls]
                h1_ref[pl.ds(r0, CONV_UNIT), ls] = acc
            return carry

        lax.fori_loop(0, tm // CONV_UNIT, unit_rows, 0)
        acc = h1_ref[...]
        xhat, _ = _layer_norm_stats(acc)
        h2 = xhat * g_ref[...] + bb_ref[...]
        h3_ref[...] = (h2 * _sigmoid(h2)).astype(bf16)

    vec = pl.BlockSpec((1, D_MODEL), lambda i: (0, 0))
    return pl.pallas_call(
        body, name="conv_fwd", grid=(T // tm,),
        in_specs=[pl.BlockSpec((tm, D_MODEL), lambda i: (i, 0)),
                  pl.BlockSpec((CONV_HALO, D_MODEL), lambda i: (jnp.maximum(i * per - 1, 0), 0)),
                  pl.BlockSpec((CONV_WIDTH, D_MODEL), lambda i: (0, 0)), vec, vec, vec],
        out_specs=[pl.BlockSpec((tm, D_MODEL), lambda i: (i, 0)), pl.BlockSpec((tm, D_MODEL), lambda i: (i, 0))],
        out_shape=[S((T, D_MODEL), f32), S((T, D_MODEL), bf16)],
        scratch_shapes=[pltpu.VMEM((tm + CONV_HALO, D_MODEL), f32)],
        compiler_params=_params("parallel"),
    )(h0, h0, w_dw, b_dw, ln_g, ln_b)


def _mix_fwd(x, o, h3, proj, w_ao, w_co, w_o, tm):
    T = x.shape[0]
    row = pl.BlockSpec((tm, D_MODEL), lambda i: (i, 0))
    wsp = _resident((D_MODEL, D_MODEL))
    g0 = GLU_END // COL

    def gate_spec(off):
        return pl.BlockSpec((tm, COL), lambda i: (i, g0 + off))

    def body(x_ref, o_ref, h3_ref, ga0, ga1, gc0, gc1, wa_ref, wc_ref, wo_ref, x1_ref, at_ref, cv_ref, mg_ref):
        attn = jnp.dot(o_ref[...], wa_ref[...], preferred_element_type=f32)
        conv = jnp.dot(h3_ref[...], wc_ref[...], preferred_element_type=f32)
        ga = jnp.concatenate([ga0[...], ga1[...]], axis=-1)
        gc = jnp.concatenate([gc0[...], gc1[...]], axis=-1)
        merged = (_sigmoid(ga) * attn + _sigmoid(gc) * conv).astype(bf16)
        at_ref[...] = attn.astype(bf16)
        cv_ref[...] = conv.astype(bf16)
        mg_ref[...] = merged
        x1_ref[...] = x_ref[...] + jnp.dot(merged, wo_ref[...], preferred_element_type=f32)

    return pl.pallas_call(
        body, name="mix_fwd", grid=(T // tm,),
        in_specs=[row, row, row, gate_spec(0), gate_spec(1), gate_spec(2), gate_spec(3), wsp, wsp, wsp],
        out_specs=[row, row, row, row],
        out_shape=[S((T, D_MODEL), f32), S((T, D_MODEL), bf16), S((T, D_MODEL), bf16), S((T, D_MODEL), bf16)],
        compiler_params=_params("parallel"),
    )(x, o, h3, proj, proj, proj, proj, w_ao, w_co, w_o)


def _ffn_fwd(x1, g, w1, w2, target, tm):
    T = x1.shape[0]
    nj = w1.shape[0]

    def body(x_ref, g_ref, w1_ref, w2_ref, t_ref, a_ref, u_ref, dy_ref, dyb_ref, ls_ref, hm):
        xv = x_ref[...]
        r = lax.rsqrt(jnp.mean(xv * xv, axis=-1, keepdims=True) + EPS)
        u = (xv * r * g_ref[...]).astype(bf16)
        u_ref[...] = u
        for j in range(nj):
            js = slice(j * FF_CHUNK, (j + 1) * FF_CHUNK)
            a = jnp.dot(u, w1_ref[j], preferred_element_type=f32)
            a_ref[:, js] = a.astype(bf16)
            hm[:, js] = jnp.square(jnp.maximum(a, 0.0)).astype(bf16)
        err = xv + jnp.dot(hm[...], w2_ref[...], preferred_element_type=f32) - t_ref[...]
        dy = err * (1.0 / D_MODEL)
        dy_ref[...] = dy
        dyb_ref[...] = dy.astype(bf16)
        ls_ref[...] = jnp.zeros((8, 128), f32) + jnp.sum(err * err) * (0.5 / D_MODEL)

    row = pl.BlockSpec((tm, D_MODEL), lambda i: (i, 0))
    wide = pl.BlockSpec((tm, D_FF), lambda i: (i, 0))
    return pl.pallas_call(
        body, name="ffn_fwd", grid=(T // tm,),
        in_specs=[row, _resident((1, D_MODEL)), _resident(w1.shape), _resident(w2.shape), row],
        out_specs=[wide, row, row, row, pl.BlockSpec((None, 8, 128), lambda i: (i, 0, 0))],
        out_shape=[S((T, D_FF), bf16), S((T, D_MODEL), bf16), S((T, D_MODEL), f32), S((T, D_MODEL), bf16),
                   S((T // tm, 8, 128), f32)],
        scratch_shapes=[pltpu.VMEM((tm, D_FF), bf16)],
        compiler_params=_params("parallel"),
    )(x1, g, w1, w2, target)


def _rms_bwd(du, xv, gv):
    r = lax.rsqrt(jnp.mean(xv * xv, axis=-1, keepdims=True) + EPS)
    xn = xv * r
    dg = jnp.sum(du * xn, axis=0, keepdims=True)
    dxn = du * gv
    dx = r * (dxn - xn * jnp.mean(dxn * xn, axis=-1, keepdims=True))
    return dx, dg


def _ffn_bwd(dy, dyb, a, x1, g, w1, w2, tm, ride=None):
    T = dy.shape[0]
    nj = w1.shape[0]

    def body(dy_ref, dyb_ref, a_ref, x_ref, g_ref, w1_ref, w2_ref, da_ref, dx_ref, dxb_ref, dg_ref):
        @pl.when(pl.program_id(0) == 0)
        def _():
            dg_ref[...] = jnp.zeros_like(dg_ref)

        dyb_v = dyb_ref[...]
        du = jnp.zeros((tm, D_MODEL), f32)
        for j in range(nj):
            js = slice(j * FF_CHUNK, (j + 1) * FF_CHUNK)
            dh = _nt(dyb_v, w2_ref[js, :])
            da = (dh * (2.0 * jnp.maximum(a_ref[:, js].astype(f32), 0.0))).astype(bf16)
            da_ref[:, js] = da
            du = du + _nt(da, w1_ref[j])
        dx, dg = _rms_bwd(du, x_ref[...], g_ref[...])
        dx1 = dy_ref[...] + dx
        dx_ref[...] = dx1
        dxb_ref[...] = dx1.astype(bf16)
        dg_ref[...] += dg

    row = pl.BlockSpec((tm, D_MODEL), lambda i: (i, 0))
    wide = pl.BlockSpec((tm, D_FF), lambda i: (i, 0))
    vec = pl.BlockSpec((1, D_MODEL), lambda i: (0, 0))
    return _call(
        body, name="ffn_bwd", grid=(T // tm,),
        in_specs=[row, row, wide, row, _resident((1, D_MODEL)), _resident(w1.shape), _resident(w2.shape)],
        out_specs=[wide, row, row, vec],
        out_shape=[S((T, D_FF), bf16), S((T, D_MODEL), f32), S((T, D_MODEL), bf16), S((1, D_MODEL), f32)],
        args=(dy, dyb, a, x1, g, w1, w2), ride=ride)


def _wgrad(name, a, b, tk, tn, tt, relu2=False, col_shard=False, out_dtype=bf16):
    T, Ka = a.shape
    Nb = b.shape[1]
    nt = T // tt

    def body(a_ref, b_ref, o_ref, acc):
        t = pl.program_id(2)
        av = a_ref[...]
        if relu2:
            av = jnp.square(jnp.maximum(av.astype(f32), 0.0))
        prod = _tn(av.astype(bf16), b_ref[...].astype(bf16))

        @pl.when(t == 0)
        def _():
            acc[...] = prod

        @pl.when(t > 0)
        def _():
            acc[...] += prod

        @pl.when(t == nt - 1)
        def _():
            o_ref[...] = acc[...].astype(out_dtype)

    if col_shard:
        out_shape = S((Nb // tn, Ka, tn), out_dtype)
        out_spec = pl.BlockSpec((None, tk, tn), lambda i, j, t: (j, i, 0))
    else:
        out_shape = S((Ka, Nb), out_dtype)
        out_spec = pl.BlockSpec((tk, tn), lambda i, j, t: (i, j))
    return pl.pallas_call(
        body, name=name, grid=(Ka // tk, Nb // tn, nt),
        in_specs=[pl.BlockSpec((tt, tk), lambda i, j, t: (t, i)), pl.BlockSpec((tt, tn), lambda i, j, t: (t, j))],
        out_specs=out_spec, out_shape=out_shape, scratch_shapes=[pltpu.VMEM((tk, tn), f32)],
        compiler_params=_params("parallel", "parallel", "arbitrary"),
    )(a, b)


def _mix_bwd(dx1, proj, attn, conv, w_ao, w_co, w_o, tm, ride=None):
    T = dx1.shape[0]
    g0 = GLU_END // COL

    def gate_spec(off):
        return pl.BlockSpec((tm, COL), lambda i: (i, g0 + off))

    def body(dx_ref, ga0, ga1, gc0, gc1, at_ref, cv_ref, wa_ref, wc_ref, wo_ref, da_ref, dc_ref, do_ref, dh3_ref, dg_ref):
        dm = _nt(dx_ref[...].astype(bf16), wo_ref[...])
        sa = _sigmoid(jnp.concatenate([ga0[...], ga1[...]], axis=-1))
        sc = _sigmoid(jnp.concatenate([gc0[...], gc1[...]], axis=-1))
        dattn = (dm * sa).astype(bf16)
        dconv = (dm * sc).astype(bf16)
        da_ref[...] = dattn
        dc_ref[...] = dconv
        dg_ref[:, 0:D_MODEL] = (dm * at_ref[...].astype(f32) * sa * (1.0 - sa)).astype(bf16)
        dg_ref[:, D_MODEL:2 * D_MODEL] = (dm * cv_ref[...].astype(f32) * sc * (1.0 - sc)).astype(bf16)
        do_ref[...] = _nt(dattn, wa_ref[...]).astype(bf16)
        dh3_ref[...] = _nt(dconv, wc_ref[...])

    row = pl.BlockSpec((tm, D_MODEL), lambda i: (i, 0))
    wsp = _resident((D_MODEL, D_MODEL))
    return _call(
        body, name="mix_bwd", grid=(T // tm,),
        in_specs=[row, gate_spec(0), gate_spec(1), gate_spec(2), gate_spec(3), row, row, wsp, wsp, wsp],
        out_specs=[row, row, row, row, pl.BlockSpec((tm, 2 * D_MODEL), lambda i: (i, 0))],
        out_shape=[S((T, D_MODEL), bf16), S((T, D_MODEL), bf16), S((T, D_MODEL), bf16), S((T, D_MODEL), f32),
                   S((T, 2 * D_MODEL), bf16)],
        args=(dx1, proj, proj, proj, proj, attn, conv, w_ao, w_co, w_o), ride=ride)


def _norm_act_bwd(dh3, h1, ln_g, ln_b, tm):
    T = dh3.shape[0]

    def body(d_ref, h_ref, g_ref, b_ref, dh1_ref, dg_ref, db_ref, dbd_ref):
        @pl.when(pl.program_id(0) == 0)
        def _():
            dg_ref[...] = jnp.zeros_like(dg_ref)
            db_ref[...] = jnp.zeros_like(db_ref)
            dbd_ref[...] = jnp.zeros_like(dbd_ref)

        xhat, rstd = _layer_norm_stats(h_ref[...])
        h2 = xhat * g_ref[...] + b_ref[...]
        sg = _sigmoid(h2)
        dh2 = d_ref[...] * (sg * (1.0 + h2 * (1.0 - sg)))
        dg_ref[...] += jnp.sum(dh2 * xhat, axis=0, keepdims=True)
        db_ref[...] += jnp.sum(dh2, axis=0, keepdims=True)
        dxh = dh2 * g_ref[...]
        dh1 = rstd * (dxh - jnp.mean(dxh, axis=-1, keepdims=True) - xhat * jnp.mean(dxh * xhat, axis=-1, keepdims=True))
        dh1_ref[...] = dh1
        dbd_ref[...] += jnp.sum(dh1, axis=0, keepdims=True)

    row = pl.BlockSpec((tm, D_MODEL), lambda i: (i, 0))
    vec = pl.BlockSpec((1, D_MODEL), lambda i: (0, 0))
    return pl.pallas_call(
        body, name="norm_act_bwd", grid=(T // tm,),
        in_specs=[row, row, vec, vec], out_specs=[row, vec, vec, vec],
        out_shape=[S((T, D_MODEL), f32), S((1, D_MODEL), f32), S((1, D_MODEL), f32), S((1, D_MODEL), f32)],
        compiler_params=_params("arbitrary"),
    )(dh3, h1, ln_g, ln_b)


def _conv_bwd(dh1, h0, proj, w_dw, tm, ride=None):
    T = dh1.shape[0]
    per = tm // CONV_HALO
    nh = T // CONV_HALO
    nt = T // tm
    a0 = V_END // COL
    lead = CONV_HALO - (CONV_WIDTH - 1)

    def body(dc_ref, dn_ref, hc_ref, hp_ref, a0_ref, a1_ref, g0_ref, g1_ref, w_ref, dglu_ref, dw_ref, dcat, hcat, wacc, dh0):
        i = pl.program_id(0)

        @pl.when(i == 0)
        def _():
            wacc[...] = jnp.zeros_like(wacc)

        dcat[0:tm, :] = dc_ref[...]
        dcat[tm:, :] = jnp.where(i == nt - 1, 0.0, dn_ref[...])
        hcat[0:CONV_HALO, :] = jnp.where(i == 0, 0.0, hp_ref[...])
        hcat[CONV_HALO:, :] = hc_ref[...]
        span = CONV_UNIT + CONV_HALO

        def unit_rows(c, carry):
            r0 = pl.multiple_of(c * CONV_UNIT, CONV_UNIT)
            for j in range(D_MODEL // 128):
                ls = slice(j * 128, (j + 1) * 128)
                dwin = dcat[pl.ds(r0, span), ls]
                acc = jnp.zeros((CONV_UNIT, 128), f32)
                for r, adv in _advanced_windows(dwin):
                    for q, off in _tap_offsets(r, span):
                        k = CONV_WIDTH - 1 - off
                        if 0 <= k < CONV_WIDTH:
                            acc = acc + adv[8 * q:8 * q + CONV_UNIT] * w_ref[k:k + 1, ls]
                dh0[pl.ds(r0, CONV_UNIT), ls] = acc
                dcur = dwin[0:CONV_UNIT]
                for r, adv in _advanced_windows(hcat[pl.ds(r0, span), ls]):
                    for q, off in _tap_offsets(r, span):
                        k = off - lead
                        if 0 <= k < CONV_WIDTH:
                            prod = dcur * adv[8 * q:8 * q + CONV_UNIT]
                            wacc[k, :, ls] += jnp.sum(prod.reshape(CONV_UNIT // 8, 8, 128), axis=0)
            return carry

        lax.fori_loop(0, tm // CONV_UNIT, unit_rows, 0)
        dh0v = dh0[...]
        av = jnp.concatenate([a0_ref[...], a1_ref[...]], axis=-1)
        sg = _sigmoid(jnp.concatenate([g0_ref[...], g1_ref[...]], axis=-1))
        dglu_ref[:, 0:D_MODEL] = (dh0v * sg).astype(bf16)
        dglu_ref[:, D_MODEL:2 * D_MODEL] = (dh0v * av * sg * (1.0 - sg)).astype(bf16)

        @pl.when(i == nt - 1)
        def _():
            for k in range(CONV_WIDTH):
                dw_ref[k:k + 1, :] = jnp.sum(wacc[k], axis=0, keepdims=True)
            dw_ref[CONV_WIDTH:CONV_WIDTH + 1, :] = jnp.zeros((1, D_MODEL), f32)

    row = pl.BlockSpec((tm, D_MODEL), lambda i: (i, 0))

    def col_spec(off):
        return pl.BlockSpec((tm, COL), lambda i: (i, a0 + off))

    return _call(
        body, name="conv_bwd", grid=(nt,),
        in_specs=[row, pl.BlockSpec((CONV_HALO, D_MODEL), lambda i: (jnp.minimum((i + 1) * per, nh - 1), 0)),
                  row, pl.BlockSpec((CONV_HALO, D_MODEL), lambda i: (jnp.maximum(i * per - 1, 0), 0)),
                  col_spec(0), col_spec(1), col_spec(2), col_spec(3),
                  pl.BlockSpec((CONV_WIDTH, D_MODEL), lambda i: (0, 0))],
        out_specs=[pl.BlockSpec((tm, 2 * D_MODEL), lambda i: (i, 0)), pl.BlockSpec((CONV_WIDTH + 1, D_MODEL), lambda i: (0, 0))],
        out_shape=[S((T, 2 * D_MODEL), bf16), S((CONV_WIDTH + 1, D_MODEL), f32)],
        scratch_shapes=[pltpu.VMEM((tm + CONV_HALO, D_MODEL), f32), pltpu.VMEM((tm + CONV_HALO, D_MODEL), f32),
                        pltpu.VMEM((CONV_WIDTH, 8, D_MODEL), f32), pltpu.VMEM((tm, D_MODEL), f32)],
        args=(dh1, dh1, h0, h0, proj, proj, proj, proj, w_dw), ride=ride)


def _attn_bwd(qn, kn, vb, o, do, lse, bias, sinks):
    T = qn.shape[0]
    nb = T // QBLOCK

    def body(q_ref, kc_ref, kp_ref, vc_ref, vp_ref, o_ref, do_ref, lse_ref, b_ref, s_ref,
             dq_ref, dk_ref, dv_ref, db_ref, dsk_ref, kcar, vcar, s_scr, dp_scr, p_scr, ds_scr):
        n = pl.program_id(0)

        @pl.when(n == 0)
        def _():
            db_ref[...] = jnp.zeros_like(db_ref)
            dsk_ref[...] = jnp.zeros_like(dsk_ref)
            kcar[...] = jnp.zeros_like(kcar)
            vcar[...] = jnp.zeros_like(vcar)

        @pl.when(n < nb)
        def _():
            lane = lax.broadcasted_iota(jnp.int32, (QBLOCK, 2 * HEAD_DIM), 1)
            lane_row = lax.broadcasted_iota(jnp.int32, (1, 2 * HEAD_DIM), 1)
            kx = _kv_placements(jnp.concatenate([kp_ref[...], kc_ref[...]], axis=0))
            vx = _kv_placements(jnp.concatenate([vp_ref[...], vc_ref[...]], axis=0))
            lse_tile = lse_ref[...]
            delta, lse_c = {}, {}
            for pr in range(N_Q_HEADS // 2):
                dop = do_ref[:, _pair_cols(pr)]
                dl = dop.astype(f32) * o_ref[:, _pair_cols(pr)].astype(f32)
                for side in range(2):
                    hq = 2 * pr + side
                    h = hq // GROUP
                    qm = _one_head(q_ref[:, _pair_cols(pr)], side)
                    s_scr[_head_rows(hq), :] = _nt(qm, kx[h, side]) + b_ref[_head_rows(hq), :]
                    dp_scr[_head_rows(hq), :] = _nt(_one_head(dop, side), vx[h, side])
                    delta[hq] = jnp.sum(_one_head(dl, side), axis=-1, keepdims=True)
                    lse_c[hq] = jnp.sum(jnp.where(lane == hq, lse_tile, 0.0), axis=-1, keepdims=True)
            dsk = jnp.zeros((1, 2 * HEAD_DIM), f32)
            for hq in range(N_Q_HEADS):
                p = jnp.exp(s_scr[_head_rows(hq), :] - lse_c[hq])
                ds = p * (dp_scr[_head_rows(hq), :] - delta[hq])
                db_ref[_head_rows(hq), :] += ds
                p_scr[_head_rows(hq), :] = p.astype(bf16)
                ds_scr[_head_rows(hq), :] = ds.astype(bf16)
                psink = jnp.exp(s_ref[0, hq] - lse_c[hq])
                dsk = dsk - jnp.where(lane_row == hq, jnp.sum(psink * delta[hq], axis=0, keepdims=True), 0.0)
            dsk_ref[...] += dsk
            for pr in range(N_Q_HEADS // 2):
                h = 2 * pr // GROUP
                dq_ref[:, _pair_cols(pr)] = (jnp.dot(ds_scr[_head_rows(2 * pr), :], kx[h, 0], preferred_element_type=f32)
                                             + jnp.dot(ds_scr[_head_rows(2 * pr + 1), :], kx[h, 1], preferred_element_type=f32))
            folded_k, folded_v = [], []
            for h in range(N_KV_HEADS):
                ka = jnp.zeros((2 * QBLOCK, 2 * HEAD_DIM), f32)
                va = jnp.zeros((2 * QBLOCK, 2 * HEAD_DIM), f32)
                for g in range(GROUP):
                    hq = h * GROUP + g
                    ka = ka + _tn(ds_scr[_head_rows(hq), :], _one_head(q_ref[:, _pair_cols(hq // 2)], hq % 2))
                    va = va + _tn(p_scr[_head_rows(hq), :], _one_head(do_ref[:, _pair_cols(hq // 2)], hq % 2))
                folded_k.append(ka + _swap_halves(ka))
                folded_v.append(va + _swap_halves(va))
            low = _low_lanes()
            for m in range(N_KV_HEADS // 2):
                cs = _pair_cols(m)
                for folded, out_ref, car in ((folded_k, dk_ref, kcar), (folded_v, dv_ref, vcar)):
                    band = jnp.where(low, folded[2 * m], folded[2 * m + 1])
                    out_ref[:, cs] = car[:, cs] + band[0:QBLOCK, :]
                    car[:, cs] = band[QBLOCK:, :]

        @pl.when(n == nb)
        def _():
            dk_ref[...] = kcar[...]
            dv_ref[...] = vcar[...]

    cur = lambda n: (jnp.minimum(n, nb - 1), 0)
    prev = lambda n: (jnp.clip(n - 1, 0, nb - 1), 0)
    qspec = pl.BlockSpec((QBLOCK, ATTN_WIDTH), cur)
    kcur, kprev = pl.BlockSpec((QBLOCK, KV_WIDTH), cur), pl.BlockSpec((QBLOCK, KV_WIDTH), prev)
    whole = lambda shape: pl.BlockSpec(shape, lambda n: (0,) * len(shape))
    scores = (N_Q_HEADS * QBLOCK, 2 * QBLOCK)
    return pl.pallas_call(
        body, name="attn_bwd", grid=(nb + 1,),
        in_specs=[qspec, kcur, kprev, kcur, kprev, qspec, qspec, pl.BlockSpec((QBLOCK, 2 * HEAD_DIM), cur), _bias_spec(), SMEM],
        out_specs=[qspec, kprev, kprev, whole(scores), whole((1, 2 * HEAD_DIM))],
        out_shape=[S((T, ATTN_WIDTH), f32), S((T, KV_WIDTH), f32), S((T, KV_WIDTH), f32), S(scores, f32),
                   S((1, 2 * HEAD_DIM), f32)],
        scratch_shapes=[pltpu.VMEM((QBLOCK, KV_WIDTH), f32), pltpu.VMEM((QBLOCK, KV_WIDTH), f32),
                        pltpu.VMEM(scores, f32), pltpu.VMEM(scores, f32), pltpu.VMEM(scores, bf16), pltpu.VMEM(scores, bf16)],
        compiler_params=_params("arbitrary"),
    )(qn, kn, kn, vb, vb, o, do, lse, bias, sinks)


def _rel_bias_bwd(dbias, bucket):
    def body(d_ref, bk_ref, o_ref):
        b = bk_ref[...]
        for k in range(N_BUCKETS):
            mk = b == k
            for h in range(N_Q_HEADS):
                o_ref[k, h] = jnp.sum(jnp.where(mk, d_ref[h * QBLOCK:(h + 1) * QBLOCK, :], 0.0))

    return pl.pallas_call(body, name="rel_bias_bwd", out_shape=S((N_BUCKETS, N_Q_HEADS), f32), out_specs=SMEM)(dbias, bucket)


def _qk_norm_bwd(dq, dk, dv, proj, qg, kg, tm):
    T = dq.shape[0]
    scale = HEAD_DIM ** -0.5

    def pair_bwd(dy, x, gv):
        r = _pair_rstd(x)
        xn = x * r
        dxn = dy * gv
        dx = r * (dxn - xn * _pair_mean(dxn * xn))
        return dx, jnp.sum(dy * xn, axis=0, keepdims=True)

    def body(dq_ref, dk_ref, dv_ref, p_ref, qg_ref, kg_ref, out_ref, dqg_ref, dkg_ref):
        @pl.when(pl.program_id(0) == 0)
        def _():
            dqg_ref[...] = jnp.zeros_like(dqg_ref)
            dkg_ref[...] = jnp.zeros_like(dkg_ref)

        qgv, kgv = qg_ref[...], kg_ref[...]
        dqg = jnp.zeros((1, 2 * HEAD_DIM), f32)
        for pr in range(N_Q_HEADS // 2):
            dx, dg = pair_bwd(dq_ref[:, _pair_cols(pr)] * scale, p_ref[:, _pair_cols(pr)], qgv)
            out_ref[:, _pair_cols(pr)] = dx.astype(bf16)
            dqg = dqg + dg
        dkg = jnp.zeros((1, 2 * HEAD_DIM), f32)
        for pr in range(N_KV_HEADS // 2):
            ps = slice(Q_END + pr * 2 * HEAD_DIM, Q_END + (pr + 1) * 2 * HEAD_DIM)
            dx, dg = pair_bwd(dk_ref[:, _pair_cols(pr)], p_ref[:, ps], kgv)
            out_ref[:, ps] = dx.astype(bf16)
            dkg = dkg + dg
        out_ref[:, K_END:V_END] = dv_ref[...].astype(bf16)
        dqg_ref[...] += dqg
        dkg_ref[...] += dkg

    vec = pl.BlockSpec((1, 2 * HEAD_DIM), lambda i: (0, 0))
    return pl.pallas_call(
        body, name="qk_norm_bwd", grid=(T // tm,),
        in_specs=[pl.BlockSpec((tm, ATTN_WIDTH), lambda i: (i, 0)), pl.BlockSpec((tm, KV_WIDTH), lambda i: (i, 0)),
                  pl.BlockSpec((tm, KV_WIDTH), lambda i: (i, 0)), pl.BlockSpec((tm, V_END), lambda i: (i, 0)), vec, vec],
        out_specs=[pl.BlockSpec((tm, V_END), lambda i: (i, 0)), vec, vec],
        out_shape=[S((T, V_END), bf16), S((1, 2 * HEAD_DIM), f32), S((1, 2 * HEAD_DIM), f32)],
        compiler_params=_params("arbitrary"),
    )(dq, dk, dv, proj, qg, kg)


def _in_bwd(dqkv, dglu, dgates, w_in, x, g, dx1, tm, ride=None):
    T = x.shape[0]
    pieces = (dqkv, dglu, dgates)
    starts = [0, dqkv.shape[1], dqkv.shape[1] + dglu.shape[1]]

    def body(a0_ref, a1_ref, a2_ref, w_ref, x_ref, g_ref, d_ref, gx_ref, dg_ref):
        @pl.when(pl.program_id(0) == 0)
        def _():
            dg_ref[...] = jnp.zeros_like(dg_ref)

        du = jnp.zeros((tm, D_MODEL), f32)
        for a_ref, c0 in zip((a0_ref, a1_ref, a2_ref), starts):
            du = du + _nt(a_ref[...], w_ref[:, c0:c0 + a_ref.shape[1]])
        dx, dg = _rms_bwd(du, x_ref[...], g_ref[...])
        gx_ref[...] = d_ref[...] + dx
        dg_ref[...] += dg

    row = pl.BlockSpec((tm, D_MODEL), lambda i: (i, 0))
    return _call(
        body, name="in_bwd", grid=(T // tm,),
        in_specs=[pl.BlockSpec((tm, p.shape[1]), lambda i: (i, 0)) for p in pieces]
        + [_resident(w_in.shape), row, _resident((1, D_MODEL)), row],
        out_specs=[row, pl.BlockSpec((1, D_MODEL), lambda i: (0, 0))],
        out_shape=[S((T, D_MODEL), f32), S((1, D_MODEL), f32)],
        args=(dqkv, dglu, dgates, w_in, x, g, dx1), ride=ride)


def _adamw(name, parts, w, m, v, tr):
    R, C = w.shape
    bc1 = 1.0 - ADAM_B1 ** ADAM_STEP
    bc2 = 1.0 - ADAM_B2 ** ADAM_STEP

    def body(p_ref, w_ref, m_ref, v_ref, g_ref, d_ref, nm_ref, nv_ref):
        g = p_ref[0].astype(f32)
        for k in range(1, N_DEV):
            g = g + p_ref[k].astype(f32)
        nm = ADAM_B1 * m_ref[...] + (1.0 - ADAM_B1) * g
        nv = ADAM_B2 * v_ref[...] + (1.0 - ADAM_B2) * (g * g)
        g_ref[...] = g
        nm_ref[...] = nm
        nv_ref[...] = nv
        d_ref[...] = -ADAM_LR * ((nm / bc1) / (jnp.sqrt(nv / bc2) + ADAM_EPS) + ADAM_WD * w_ref[...])

    blk = pl.BlockSpec((tr, C), lambda i: (i, 0))
    return pl.pallas_call(
        body, name=name, grid=(R // tr,),
        in_specs=[pl.BlockSpec((N_DEV, tr, C), lambda i: (0, i, 0)), blk, blk, blk],
        out_specs=[blk, blk, blk, blk], out_shape=[S((R, C), f32)] * 4,
        compiler_params=_params("parallel"),
    )(parts, w, m, v)


def _tile(T, pref):
    return min(T, pref)


def _pad_rows(a, rows):
    return jnp.pad(a, ((0, rows - a.shape[0]), (0, 0)))


def kernel(x, norm_mix_g, w_in, q_norm_g, k_norm_g, attn_sinks, rel_bias, w_attn_o, w_dw, b_dw, conv_ln_g, conv_ln_b, w_conv_out, w_out, norm_mlp_g, w_ff1, w_ff2, loss_target, m_norm_mix_g, m_w_in, m_q_norm_g, m_k_norm_g, m_attn_sinks, m_rel_bias, m_w_attn_o, m_w_dw, m_b_dw, m_conv_ln_g, m_conv_ln_b, m_w_conv_out, m_w_out, m_norm_mlp_g, m_w_ff1, m_w_ff2, v_norm_mix_g, v_w_in, v_q_norm_g, v_k_norm_g, v_attn_sinks, v_rel_bias, v_w_attn_o, v_w_dw, v_b_dw, v_conv_ln_g, v_conv_ln_b, v_w_conv_out, v_w_out, v_norm_mlp_g, v_w_ff1, v_w_ff2):
    T = x.shape[1]
    xs = x[0]
    tgt = loss_target[0]
    in_shard = IN_WIDTH // N_DEV
    dw_rows = CONV_WIDTH + 1
    ch_shard = D_MODEL // N_DEV
    tc = _tile(T, 256)
    tt = _tile(T, 2048)
    bucket = jnp.asarray(_t5_bucket_table())

    g_in, g_dw = _exchange("gather_w_in", [w_in[0].astype(bf16), _pad_rows(w_dw[0], dw_rows)], gather=True)
    W_in = jnp.transpose(g_in, (1, 0, 2)).reshape(D_MODEL, IN_WIDTH)
    W_dw = jnp.transpose(g_dw, (1, 0, 2)).reshape(dw_rows, D_MODEL)[:CONV_WIDTH]

    mix_shards = _Exchange([w_attn_o[0].astype(bf16), w_conv_out[0].astype(bf16), w_out[0].astype(bf16)], gather=True)
    (proj, u), (g_ao, g_co, g_o) = _proj_fwd(xs, norm_mix_g, W_in, tc, ride=mix_shards)
    W_ao = g_ao.reshape(D_MODEL, D_MODEL)
    W_co = g_co.reshape(D_MODEL, D_MODEL)
    W_o = g_o.reshape(D_MODEL, D_MODEL)
    qg2, kg2 = jnp.tile(q_norm_g, (1, 2)), jnp.tile(k_norm_g, (1, 2))
    qn, kn, vb, h0 = _prep_fwd(proj, qg2, kg2, tc)
    bias = _bias_table(rel_bias, bucket)
    ffn_shards = _Exchange([w_ff1[0].astype(bf16), w_ff2[0].astype(bf16)], gather=True)
    (o, lse), (g_f1, g_f2) = _attn_fwd(qn, kn, vb, bias, attn_sinks, ride=ffn_shards)
    h1, h3 = _conv_fwd(h0, W_dw, b_dw, conv_ln_g, conv_ln_b, tc)
    x1, attn, conv, merged = _mix_fwd(xs, o, h3, proj, W_ao, W_co, W_o, tc)
    W_f2 = g_f2.reshape(D_FF, D_MODEL)
    a, u2, dy, dyb, loss_parts = _ffn_fwd(x1, norm_mlp_g, g_f1, W_f2, tgt, tc)
    loss = lax.psum(jnp.sum(loss_parts[:, 0, 0]), ("x", "y", "c"))

    gw_f2 = _wgrad("wgrad_ff2", a, dyb, D_MODEL, D_MODEL, tt, relu2=True).reshape(N_DEV, FF_CHUNK, D_MODEL)
    (da, dx1, dx1b, d_norm_mlp_g), (l_f2,) = _ffn_bwd(dy, dyb, a, x1, norm_mlp_g, g_f1, W_f2, tc,
                                                      ride=_Exchange([gw_f2], gather=False))
    gw_f1 = _wgrad("wgrad_ff1", u2, da, D_MODEL, FF_CHUNK, tt, col_shard=True)
    gw_o = _wgrad("wgrad_out", merged, dx1b, D_MODEL, D_MODEL, tt).reshape(N_DEV, ch_shard, D_MODEL)
    (dattn, dconv, do, dh3, dgates), (l_f1, l_o) = _mix_bwd(dx1b, proj, attn, conv, W_ao, W_co, W_o, tc,
                                                             ride=_Exchange([gw_f1, gw_o], gather=False))
    gw_ao = _wgrad("wgrad_attn_o", o, dattn, D_MODEL, D_MODEL, tt).reshape(N_DEV, ch_shard, D_MODEL)
    gw_co = _wgrad("wgrad_conv_out", h3, dconv, D_MODEL, D_MODEL, tt).reshape(N_DEV, ch_shard, D_MODEL)
    dh1, d_ln_g, d_ln_b, d_b_dw = _norm_act_bwd(dh3, h1, conv_ln_g, conv_ln_b, tc)
    (dglu, d_w_dw), (l_ao, l_co) = _conv_bwd(dh1, h0, proj, W_dw, tc, ride=_Exchange([gw_ao, gw_co], gather=False))
    dq, dk, dv, dbias, d_sinks = _attn_bwd(qn, kn, vb, o, do, lse, bias, attn_sinks)
    d_sinks = d_sinks[:, :N_Q_HEADS]
    d_rel_bias = _rel_bias_bwd(dbias, bucket)
    dqkv, d_qg, d_kg = _qk_norm_bwd(dq, dk, dv, proj, qg2, kg2, tc)
    d_qg = d_qg[:, :HEAD_DIM] + d_qg[:, HEAD_DIM:]
    d_kg = d_kg[:, :HEAD_DIM] + d_kg[:, HEAD_DIM:]
    gw_in = jnp.concatenate([_wgrad("wgrad_in_qkv", u, dqkv, D_MODEL, COL, tt),
                             _wgrad("wgrad_in_glu", u, dglu, D_MODEL, D_MODEL, tt),
                             _wgrad("wgrad_in_gates", u, dgates, D_MODEL, D_MODEL, tt)], axis=1)
    gw_in = jnp.transpose(gw_in.reshape(D_MODEL, N_DEV, in_shard), (1, 0, 2))
    gw_dw = jnp.transpose(d_w_dw.reshape(dw_rows, N_DEV, ch_shard), (1, 0, 2))
    (grad_x, d_norm_mix_g), (l_in, l_dw) = _in_bwd(dqkv, dglu, dgates, W_in, xs, norm_mix_g, dx1, tc,
                                                    ride=_Exchange([gw_in, gw_dw], gather=False))

    def row(vec):
        flat = vec.reshape(1, -1)
        return jnp.pad(flat, ((0, 0), (0, D_MODEL - flat.shape[1])))

    def pack_small(nm, qg, kg, sk, rb, bd, lg, lb, nl):
        tail = jnp.concatenate([qg.reshape(1, -1), kg.reshape(1, -1), sk.reshape(1, -1), rb.reshape(1, -1)], axis=1)
        return jnp.concatenate([row(nm), row(bd), row(lg), row(lb), row(nl), row(tail), jnp.zeros((2, D_MODEL), f32)], axis=0)

    def unpack_small(p):
        t = p[5]
        o0, o1, o2 = HEAD_DIM, 2 * HEAD_DIM, 2 * HEAD_DIM + N_Q_HEADS
        return dict(norm_mix_g=p[0:1], b_dw=p[1:2], conv_ln_g=p[2:3], conv_ln_b=p[3:4], norm_mlp_g=p[4:5],
                    q_norm_g=t[0:o0].reshape(1, HEAD_DIM), k_norm_g=t[o0:o1].reshape(1, HEAD_DIM),
                    attn_sinks=t[o1:o2].reshape(1, N_Q_HEADS),
                    rel_bias=t[o2:o2 + N_BUCKETS * N_Q_HEADS].reshape(N_BUCKETS, N_Q_HEADS))

    small_g = pack_small(d_norm_mix_g, d_qg, d_kg, d_sinks, d_rel_bias, d_b_dw, d_ln_g, d_ln_b, d_norm_mlp_g)
    (l_small,) = _exchange("gather_small_grads", [small_g], gather=True)


    res = {}
    res["w_in"] = _adamw("adamw_in", l_in, w_in[0], m_w_in[0], v_w_in[0], 256)
    res["w_attn_o"] = _adamw("adamw_attn_o", l_ao, w_attn_o[0], m_w_attn_o[0], v_w_attn_o[0], ch_shard)
    res["w_conv_out"] = _adamw("adamw_conv_out", l_co, w_conv_out[0], m_w_conv_out[0], v_w_conv_out[0], ch_shard)
    res["w_out"] = _adamw("adamw_out", l_o, w_out[0], m_w_out[0], v_w_out[0], ch_shard)
    res["w_ff1"] = _adamw("adamw_ff1", l_f1, w_ff1[0], m_w_ff1[0], v_w_ff1[0], 256)
    res["w_ff2"] = _adamw("adamw_ff2", l_f2, w_ff2[0], m_w_ff2[0], v_w_ff2[0], 256)
    dw4 = _adamw("adamw_dw", l_dw, _pad_rows(w_dw[0], dw_rows), _pad_rows(m_w_dw[0], dw_rows), _pad_rows(v_w_dw[0], dw_rows), dw_rows)
    res["w_dw"] = [t[:CONV_WIDTH] for t in dw4]
    small_w = pack_small(norm_mix_g, q_norm_g, k_norm_g, attn_sinks, rel_bias, b_dw, conv_ln_g, conv_ln_b, norm_mlp_g)
    small_m = pack_small(m_norm_mix_g, m_q_norm_g, m_k_norm_g, m_attn_sinks, m_rel_bias, m_b_dw, m_conv_ln_g, m_conv_ln_b, m_norm_mlp_g)
    small_v = pack_small(v_norm_mix_g, v_q_norm_g, v_k_norm_g, v_attn_sinks, v_rel_bias, v_b_dw, v_conv_ln_g, v_conv_ln_b, v_norm_mlp_g)
    small4 = [unpack_small(t) for t in _adamw("adamw_small", l_small, small_w, small_m, small_v, 8)]

    order = ["norm_mix_g", "w_in", "q_norm_g", "k_norm_g", "attn_sinks", "rel_bias", "w_attn_o", "w_dw", "b_dw",
             "conv_ln_g", "conv_ln_b", "w_conv_out", "w_out", "norm_mlp_g", "w_ff1", "w_ff2"]
    stacked = {"w_in", "w_attn_o", "w_dw", "w_conv_out", "w_out", "w_ff1", "w_ff2"}
    outs = [loss, grad_x[None]]
    for k in range(4):
        for nme in order:
            if nme in stacked:
                outs.append(res[nme][k][None])
            else:
                outs.append(small4[k][nme])
    return tuple(outs)
```

```python
import functools

import numpy as np
import jax
import jax.numpy as jnp
from jax import lax
from jax.experimental import pallas as pl
from jax.experimental.pallas import tpu as pltpu

f32 = jnp.float32
bf16 = jnp.bfloat16
S = jax.ShapeDtypeStruct

N_DEV = 8
D_MODEL = 1024
HEAD_DIM = 64
N_Q_HEADS = 16
N_KV_HEADS = 4
GROUP = N_Q_HEADS // N_KV_HEADS
ATTN_WIDTH = N_Q_HEADS * HEAD_DIM
KV_WIDTH = N_KV_HEADS * HEAD_DIM
QBLOCK = 128
CONV_WIDTH = 31
CONV_HALO = 32
CONV_UNIT = 64
D_FF = 4 * D_MODEL
N_BUCKETS = 32
MAX_DISTANCE = 128
EPS = 1e-6
NEG = -1e30
Q_END = ATTN_WIDTH
K_END = Q_END + KV_WIDTH
V_END = K_END + KV_WIDTH
GLU_END = V_END + 2 * D_MODEL
IN_WIDTH = GLU_END + 2 * D_MODEL
COL = 512
FF_CHUNK = D_FF // N_DEV

ADAM_LR = 0.001
ADAM_B1 = 0.9
ADAM_B2 = 0.999
ADAM_EPS = 1e-08
ADAM_WD = 0.01
ADAM_STEP = 10

VMEM_LIMIT = 56 * 1024 * 1024

MESH_ID = pl.DeviceIdType.MESH
ANY = pl.BlockSpec(memory_space=pl.ANY)
SMEM = pl.BlockSpec(memory_space=pltpu.SMEM)


def _params(*sem):
    return pltpu.CompilerParams(dimension_semantics=sem, vmem_limit_bytes=VMEM_LIMIT)


def _nt(a, b):
    return lax.dot_general(a, b, (((1,), (1,)), ((), ())), preferred_element_type=f32)


def _tn(a, b):
    return lax.dot_general(a, b, (((0,), (0,)), ((), ())), preferred_element_type=f32)


def _sigmoid(z):
    return 1.0 / (1.0 + jnp.exp(-z))


def _t5_bucket_table():
    qi = np.arange(QBLOCK, dtype=np.int32)[:, None]
    kj = np.arange(2 * QBLOCK, dtype=np.int32)[None, :]
    dist = qi + QBLOCK - kj
    n = np.maximum(dist, 0)
    max_exact = N_BUCKETS // 2
    nf = np.maximum(n, 1).astype(np.float32)
    large = max_exact + (np.log(nf / np.float32(max_exact)) / np.float32(np.log(MAX_DISTANCE / max_exact))
                         * np.float32(N_BUCKETS - max_exact)).astype(np.int32)
    large = np.minimum(large, N_BUCKETS - 1)
    bucket = np.where(n < max_exact, n, large)
    valid = (dist >= 0) & (dist < QBLOCK)
    return np.where(valid, bucket, -1).astype(np.int32)


def _peer(d):
    x, y, c = lax.axis_index("x"), lax.axis_index("y"), lax.axis_index("c")
    dx, dy, dc = (d >> 2) & 1, (d >> 1) & 1, d & 1
    px, py, pc = x ^ dx, y ^ dy, c ^ dc
    return (px, py, pc), 4 * px + 2 * py + pc


class _Exchange:
    def __init__(self, arrays, gather):
        self.arrays, self.gather, self.n = list(arrays), gather, len(arrays)
        self.out_shape = [S(((N_DEV,) + a.shape) if gather else a.shape, a.dtype) for a in self.arrays]
        self.scratch = [pltpu.SemaphoreType.DMA((self.n, N_DEV - 1)), pltpu.SemaphoreType.DMA((self.n, N_DEV - 1)),
                        pltpu.SemaphoreType.DMA((self.n,))]

    def _copies(self, ins, outs, sems):
        send_sems, recv_sems, local_sems = sems
        _, me = _peer(0)
        local, sends, recvs = [], [], []
        for k in range(self.n):
            src = ins[k] if self.gather else ins[k].at[me]
            local.append(pltpu.make_async_copy(src, outs[k].at[me], local_sems.at[k]))
        for d in range(1, N_DEV):
            peer, pidx = _peer(d)
            for k in range(self.n):
                src = ins[k] if self.gather else ins[k].at[pidx]
                common = dict(src_ref=src, send_sem=send_sems.at[k, d - 1], recv_sem=recv_sems.at[k, d - 1],
                              device_id=peer, device_id_type=MESH_ID)
                sends.append(pltpu.make_async_remote_copy(dst_ref=outs[k].at[me], **common))
                recvs.append(pltpu.make_async_remote_copy(dst_ref=outs[k].at[pidx], **common))
        return local, sends, recvs

    def start(self, ins, outs, sems):
        local, sends, _ = self._copies(ins, outs, sems)
        for cp in local + sends:
            cp.start()

    def wait(self, ins, outs, sems):
        local, sends, recvs = self._copies(ins, outs, sems)
        for cp in recvs:
            cp.wait_recv()
        for cp in sends:
            cp.wait_send()
        for cp in local:
            cp.wait()


class _Gather:
    CHIPS = (4, 2, 6)
    SLOTS = 1 + 2 * len(CHIPS)

    def __init__(self, arrays):
        self.arrays, self.n = list(arrays), len(arrays)
        self.out_shape = [S((N_DEV,) + a.shape, a.dtype) for a in self.arrays]
        self.scratch = [pltpu.SemaphoreType.DMA((self.n, self.SLOTS)), pltpu.SemaphoreType.DMA((self.n, self.SLOTS)),
                        pltpu.SemaphoreType.DMA((self.n,))]

    @staticmethod
    def _copy(outs, sems, k, slot, src, block, to):
        return pltpu.make_async_remote_copy(src_ref=src, dst_ref=outs[k].at[block], send_sem=sems[0].at[k, slot],
                                            recv_sem=sems[1].at[k, slot], device_id=to, device_id_type=MESH_ID)

    def _local(self, ins, outs, sems):
        _, me = _peer(0)
        return [pltpu.make_async_copy(ins[k], outs[k].at[me], sems[2].at[k]) for k in range(self.n)]

    def start(self, ins, outs, sems):
        _, me = _peer(0)
        sibling, _ = _peer(1)
        for cp in self._local(ins, outs, sems):
            cp.start()
        for k in range(self.n):
            self._copy(outs, sems, k, 0, ins[k], me, sibling).start()
            for j, d in enumerate(self.CHIPS):
                self._copy(outs, sems, k, 1 + j, ins[k], me, _peer(d)[0]).start()

    def mid(self, ins, outs, sems):
        sibling, _ = _peer(1)
        for j, d in enumerate(self.CHIPS):
            chip, block = _peer(d)
            for k in range(self.n):
                self._copy(outs, sems, k, 1 + j, ins[k], block, chip).wait_recv()
                self._copy(outs, sems, k, 4 + j, outs[k].at[block], block, sibling).start()

    def wait(self, ins, outs, sems):
        _, me = _peer(0)
        sibling, sib_block = _peer(1)
        for k in range(self.n):
            self._copy(outs, sems, k, 0, ins[k], sib_block, sibling).wait_recv()
            for j, d in enumerate(self.CHIPS):
                self._copy(outs, sems, k, 4 + j, ins[k], _peer(d ^ 1)[1], sibling).wait_recv()
        for k in range(self.n):
            self._copy(outs, sems, k, 0, ins[k], me, sibling).wait_send()
            for j, d in enumerate(self.CHIPS):
                chip, block = _peer(d)
                self._copy(outs, sems, k, 1 + j, ins[k], me, chip).wait_send()
                self._copy(outs, sems, k, 4 + j, outs[k].at[block], block, sibling).wait_send()
        for cp in self._local(ins, outs, sems):
            cp.wait()


def _exchange(name, arrays, gather, two_level=False):
    ex = _Gather(arrays) if two_level else _Exchange(arrays, gather)
    n = ex.n

    def body(*refs):
        ins, outs, sems = refs[:n], refs[n:2 * n], refs[2 * n:]
        ex.start(ins, outs, sems)
        if two_level:
            ex.mid(ins, outs, sems)
        ex.wait(ins, outs, sems)

    return pl.pallas_call(body, name=name, out_shape=ex.out_shape, in_specs=[ANY] * n, out_specs=[ANY] * n,
                          scratch_shapes=ex.scratch)(*arrays)


def _call(body, *, name, grid, in_specs, out_specs, out_shape, args, scratch_shapes=(), ride=None):
    n_in, n_out, n_sc = len(in_specs), len(out_specs), len(scratch_shapes)
    sem = ("arbitrary",) * len(grid)
    if ride is None:
        res = pl.pallas_call(body, name=name, grid=grid, in_specs=list(in_specs), out_specs=list(out_specs),
                             out_shape=list(out_shape), scratch_shapes=list(scratch_shapes), compiler_params=_params(*sem))(*args)
        return list(res), []
    nx = ride.n

    def riding(*refs):
        ins, xin = refs[:n_in], refs[n_in:n_in + nx]
        outs, xout = refs[n_in + nx:n_in + nx + n_out], refs[n_in + nx + n_out:n_in + 2 * nx + n_out]
        rest = refs[n_in + 2 * nx + n_out:]
        scratch, sems = rest[:n_sc], rest[n_sc:]
        ids = [pl.program_id(ax) for ax in range(len(grid))]
        first = functools.reduce(jnp.logical_and, [i == 0 for i in ids])
        last = functools.reduce(jnp.logical_and, [i == g - 1 for i, g in zip(ids, grid)])

        @pl.when(first)
        def _():
            ride.start(xin, xout, sems)

        if hasattr(ride, "mid"):
            halfway = functools.reduce(jnp.logical_and, [ids[0] == grid[0] // 2] + [i == 0 for i in ids[1:]])

            @pl.when(halfway)
            def _():
                ride.mid(xin, xout, sems)

        body(*ins, *outs, *scratch)

        @pl.when(last)
        def _():
            ride.wait(xin, xout, sems)

    res = pl.pallas_call(
        riding, name=name, grid=grid, in_specs=list(in_specs) + [ANY] * nx, out_specs=list(out_specs) + [ANY] * nx,
        out_shape=list(out_shape) + ride.out_shape, scratch_shapes=list(scratch_shapes) + ride.scratch,
        compiler_params=_params(*sem))(*args, *ride.arrays)
    return list(res[:n_out]), list(res[n_out:])


def _resident(shape):
    return pl.BlockSpec(shape, lambda *_: (0,) * len(shape), pipeline_mode=pl.Buffered(1))


def _proj_fwd(x, g, w, qg, kg, tm, ride=None):
    T, K = x.shape
    N = w.shape[1]
    per = COL // (2 * HEAD_DIM)
    assert Q_END % COL == 0 and V_END == Q_END + COL and (GLU_END - V_END) == 4 * COL and KV_WIDTH == COL // 2

    def body(x_ref, g_ref, w_ref, qg_ref, kg_ref, o_ref, u_ref, qn_ref, kn_ref, vb_ref, h0_ref):
        xv = x_ref[...]
        r = lax.rsqrt(jnp.mean(xv * xv, axis=-1, keepdims=True) + EPS)
        u = (xv * r * g_ref[...]).astype(bf16)
        u_ref[...] = u

        def block(c):
            cs = slice(c * COL, (c + 1) * COL)
            pc = jnp.dot(u, w_ref[:, cs], preferred_element_type=f32)
            o_ref[:, cs] = pc.astype(bf16)
            return pc

        qgv = qg_ref[...] * (HEAD_DIM ** -0.5)
        for c in range(Q_END // COL):
            pc = block(c)
            for t in range(per):
                xq = pc[:, _pair_cols(t)]
                qn_ref[:, _pair_cols(c * per + t)] = (xq * _pair_rstd(xq) * qgv).astype(bf16)
        pc = block(Q_END // COL)
        for t in range(KV_WIDTH // (2 * HEAD_DIM)):
            xk = pc[:, _pair_cols(t)]
            kn_ref[:, _pair_cols(t)] = (xk * _pair_rstd(xk) * kg_ref[...]).astype(bf16)
        vb_ref[...] = pc[:, KV_WIDTH:].astype(bf16)
        a0 = V_END // COL
        for half in range(2):
            gate = block(a0 + 2 + half)
            h0_ref[:, half * COL:(half + 1) * COL] = block(a0 + half) * _sigmoid(gate)
        for c in range(GLU_END // COL, N // COL):
            block(c)

    row = lambda width: pl.BlockSpec((tm, width), lambda i: (i, 0))
    return _call(
        body, name="proj_fwd", grid=(T // tm,),
        in_specs=[row(K), _resident((1, K)), _resident((K, N)), _resident((1, 2 * HEAD_DIM)), _resident((1, 2 * HEAD_DIM))],
        out_specs=[row(N), row(K), row(ATTN_WIDTH), row(KV_WIDTH), row(KV_WIDTH), row(D_MODEL)],
        out_shape=[S((T, N), bf16), S((T, K), bf16), S((T, ATTN_WIDTH), bf16), S((T, KV_WIDTH), bf16), S((T, KV_WIDTH), bf16),
                   S((T, D_MODEL), f32)],
        args=(x, g, w, qg, kg), ride=ride)


def _bias_table(rel_bias, bucket):
    def body(rb_ref, bk_ref, o_ref):
        b = bk_ref[...]
        absent = lax.broadcasted_iota(jnp.int32, (QBLOCK, 2 * QBLOCK), 1) < QBLOCK
        for h in range(N_Q_HEADS):
            acc = jnp.full((QBLOCK, 2 * QBLOCK), NEG, f32)
            for k in range(N_BUCKETS):
                acc = jnp.where(b == k, rb_ref[k, h], acc)
            o_ref[0, h * QBLOCK:(h + 1) * QBLOCK, :] = acc
            o_ref[1, h * QBLOCK:(h + 1) * QBLOCK, :] = jnp.where(absent, NEG, acc)

    return pl.pallas_call(
        body, name="bias_table", out_shape=S((2, N_Q_HEADS * QBLOCK, 2 * QBLOCK), f32),
        in_specs=[SMEM, pl.BlockSpec(memory_space=pltpu.VMEM)],
    )(rel_bias, bucket)


def _bias_spec():
    return pl.BlockSpec((None, N_Q_HEADS * QBLOCK, 2 * QBLOCK), lambda n: (jnp.where(n == 0, 1, 0), 0, 0))


def _swap_halves(t):
    return jnp.concatenate([t[:, HEAD_DIM:], t[:, :HEAD_DIM]], axis=1)


def _low_lanes():
    return lax.broadcasted_iota(jnp.int32, (1, 2 * HEAD_DIM), 1) < HEAD_DIM


def _one_head(pair, side):
    zero = jnp.zeros((), pair.dtype)
    return jnp.where(_low_lanes(), pair, zero) if side == 0 else jnp.where(_low_lanes(), zero, pair)


def _pair_mean(t):
    low = _low_lanes()
    m_lo = jnp.sum(_one_head(t, 0), axis=-1, keepdims=True) * (1.0 / HEAD_DIM)
    m_hi = jnp.sum(_one_head(t, 1), axis=-1, keepdims=True) * (1.0 / HEAD_DIM)
    return jnp.where(low, m_lo, m_hi)


def _pair_rstd(x):
    return lax.rsqrt(_pair_mean(x * x) + EPS)


def _kv_placements(band):
    out = {}
    for m in range(N_KV_HEADS // 2):
        pair = band[:, m * 2 * HEAD_DIM:(m + 1) * 2 * HEAD_DIM]
        swapped = _swap_halves(pair)
        for hh in range(2):
            out[2 * m + hh, 0] = _one_head(pair if hh == 0 else swapped, 0)
            out[2 * m + hh, 1] = _one_head(swapped if hh == 0 else pair, 1)
    return out


def _head_rows(hq):
    return slice(hq * QBLOCK, (hq + 1) * QBLOCK)


def _pair_cols(pr):
    return slice(pr * 2 * HEAD_DIM, (pr + 1) * 2 * HEAD_DIM)


def _attn_fwd(qn, kn, vb, bias, sinks, ride=None):
    T = qn.shape[0]
    nb = T // QBLOCK

    def body(q_ref, kc_ref, kp_ref, vc_ref, vp_ref, b_ref, s_ref, o_ref, lse_ref, s_scr, p_scr):
        lane = lax.broadcasted_iota(jnp.int32, (QBLOCK, 2 * HEAD_DIM), 1)
        kx = _kv_placements(jnp.concatenate([kp_ref[...], kc_ref[...]], axis=0))
        vx = _kv_placements(jnp.concatenate([vp_ref[...], vc_ref[...]], axis=0))
        for hq in range(N_Q_HEADS):
            qm = _one_head(q_ref[:, _pair_cols(hq // 2)], hq % 2)
            s_scr[_head_rows(hq), :] = _nt(qm, kx[hq // GROUP, hq % 2]) + b_ref[_head_rows(hq), :]
        lse_tile = jnp.zeros((QBLOCK, 2 * HEAD_DIM), f32)
        for hq in range(N_Q_HEADS):
            s = s_scr[_head_rows(hq), :]
            sink = s_ref[0, hq]
            m = jnp.maximum(jnp.max(s, axis=-1, keepdims=True), sink)
            p = jnp.exp(s - m)
            l = jnp.sum(p, axis=-1, keepdims=True) + jnp.exp(sink - m)
            p_scr[_head_rows(hq), :] = (p * (1.0 / l)).astype(bf16)
            lse_tile = jnp.where(lane == hq, m + jnp.log(l), lse_tile)
        lse_ref[...] = lse_tile
        for pr in range(N_Q_HEADS // 2):
            h = 2 * pr // GROUP
            o_pair = (jnp.dot(p_scr[_head_rows(2 * pr), :], vx[h, 0], preferred_element_type=f32)
                      + jnp.dot(p_scr[_head_rows(2 * pr + 1), :], vx[h, 1], preferred_element_type=f32))
            o_ref[:, _pair_cols(pr)] = o_pair.astype(bf16)

    cur = lambda n: (n, 0)
    prev = lambda n: (jnp.maximum(n - 1, 0), 0)
    return _call(
        body, name="attn_fwd", grid=(nb,),
        in_specs=[pl.BlockSpec((QBLOCK, ATTN_WIDTH), cur), pl.BlockSpec((QBLOCK, KV_WIDTH), cur),
                  pl.BlockSpec((QBLOCK, KV_WIDTH), prev), pl.BlockSpec((QBLOCK, KV_WIDTH), cur),
                  pl.BlockSpec((QBLOCK, KV_WIDTH), prev), _bias_spec(), SMEM],
        out_specs=[pl.BlockSpec((QBLOCK, ATTN_WIDTH), cur), pl.BlockSpec((QBLOCK, 2 * HEAD_DIM), cur)],
        out_shape=[S((T, ATTN_WIDTH), bf16), S((T, 2 * HEAD_DIM), f32)],
        scratch_shapes=[pltpu.VMEM((N_Q_HEADS * QBLOCK, 2 * QBLOCK), f32), pltpu.VMEM((N_Q_HEADS * QBLOCK, 2 * QBLOCK), bf16)],
        args=(qn, kn, kn, vb, vb, bias, sinks), ride=ride)


def _layer_norm_stats(h1):
    mu = jnp.mean(h1, axis=-1, keepdims=True)
    xc = h1 - mu
    rstd = lax.rsqrt(jnp.mean(xc * xc, axis=-1, keepdims=True) + EPS)
    return xc * rstd, rstd


def _advanced_windows(win):
    rows = win.shape[0]
    for r in range(8):
        yield r, (win if r == 0 else pltpu.roll(win, rows - r, 0))


def _tap_offsets(r, rows):
    for q in range((rows - CONV_UNIT) // 8 + 1):
        if r == 0 or 8 * q + r + CONV_UNIT <= rows:
            yield q, 8 * q + r


def _conv_fwd(h0, w_dw, b_dw, ln_g, ln_b, tm, ride=None):
    T = h0.shape[0]
    per = tm // CONV_HALO
    lead = CONV_HALO - (CONV_WIDTH - 1)

    def body(hc_ref, hp_ref, w_ref, b_ref, g_ref, bb_ref, h1_ref, h3_ref, cat):
        i = pl.program_id(0)
        cat[0:CONV_HALO, :] = jnp.where(i == 0, 0.0, hp_ref[...])
        cat[CONV_HALO:, :] = hc_ref[...]

        def unit_rows(c, carry):
            r0 = pl.multiple_of(c * CONV_UNIT, CONV_UNIT)
            for j in range(D_MODEL // 128):
                ls = slice(j * 128, (j + 1) * 128)
                win = cat[pl.ds(r0, CONV_UNIT + CONV_HALO), ls]
                acc = jnp.zeros((CONV_UNIT, 128), f32) + b_ref[:, ls]
                for r, adv in _advanced_windows(win):
                    for q, off in _tap_offsets(r, CONV_UNIT + CONV_HALO):
                        k = off - lead
                        if 0 <= k < CONV_WIDTH:
                            acc = acc + adv[8 * q:8 * q + CONV_UNIT] * w_ref[k:k + 1, ls]
                h1_ref[pl.ds(r0, CONV_UNIT), ls] = acc
            return carry

        lax.fori_loop(0, tm // CONV_UNIT, unit_rows, 0)
        acc = h1_ref[...]
        xhat, _ = _layer_norm_stats(acc)
        h2 = xhat * g_ref[...] + bb_ref[...]
        h3_ref[...] = (h2 * _sigmoid(h2)).astype(bf16)

    vec = pl.BlockSpec((1, D_MODEL), lambda i: (0, 0))
    return _call(
        body, name="conv_fwd", grid=(T // tm,),
        in_specs=[pl.BlockSpec((tm, D_MODEL), lambda i: (i, 0)),
                  pl.BlockSpec((CONV_HALO, D_MODEL), lambda i: (jnp.maximum(i * per - 1, 0), 0)),
                  pl.BlockSpec((CONV_WIDTH, D_MODEL), lambda i: (0, 0)), vec, vec, vec],
        out_specs=[pl.BlockSpec((tm, D_MODEL), lambda i: (i, 0)), pl.BlockSpec((tm, D_MODEL), lambda i: (i, 0))],
        out_shape=[S((T, D_MODEL), f32), S((T, D_MODEL), bf16)],
        scratch_shapes=[pltpu.VMEM((tm + CONV_HALO, D_MODEL), f32)],
        args=(h0, h0, w_dw, b_dw, ln_g, ln_b), ride=ride)


def _mix_fwd(x, o, h3, proj, w_ao, w_co, w_o, tm):
    T = x.shape[0]
    row = pl.BlockSpec((tm, D_MODEL), lambda i: (i, 0))
    wsp = _resident((D_MODEL, D_MODEL))
    g0 = GLU_END // COL

    def gate_spec(off):
        return pl.BlockSpec((tm, COL), lambda i: (i, g0 + off))

    def body(x_ref, o_ref, h3_ref, ga0, ga1, gc0, gc1, wa_ref, wc_ref, wo_ref, x1_ref, at_ref, cv_ref, mg_ref):
        attn = jnp.dot(o_ref[...], wa_ref[...], preferred_element_type=f32)
        conv = jnp.dot(h3_ref[...], wc_ref[...], preferred_element_type=f32)
        ga = jnp.concatenate([ga0[...], ga1[...]], axis=-1).astype(f32)
        gc = jnp.concatenate([gc0[...], gc1[...]], axis=-1).astype(f32)
        merged = (_sigmoid(ga) * attn + _sigmoid(gc) * conv).astype(bf16)
        at_ref[...] = attn.astype(bf16)
        cv_ref[...] = conv.astype(bf16)
        mg_ref[...] = merged
        x1_ref[...] = x_ref[...] + jnp.dot(merged, wo_ref[...], preferred_element_type=f32)

    return pl.pallas_call(
        body, name="mix_fwd", grid=(T // tm,),
        in_specs=[row, row, row, gate_spec(0), gate_spec(1), gate_spec(2), gate_spec(3), wsp, wsp, wsp],
        out_specs=[row, row, row, row],
        out_shape=[S((T, D_MODEL), f32), S((T, D_MODEL), bf16), S((T, D_MODEL), bf16), S((T, D_MODEL), bf16)],
        compiler_params=_params("parallel"),
    )(x, o, h3, proj, proj, proj, proj, w_ao, w_co, w_o)


def _ffn_fwd(x1, g, w1, w2, target, tm):
    T = x1.shape[0]
    nj = w1.shape[0]

    def body(x_ref, g_ref, w1_ref, w2_ref, t_ref, a_ref, u_ref, dy_ref, dyb_ref, ls_ref, hm):
        xv = x_ref[...]
        r = lax.rsqrt(jnp.mean(xv * xv, axis=-1, keepdims=True) + EPS)
        u = (xv * r * g_ref[...]).astype(bf16)
        u_ref[...] = u
        for j in range(nj):
            js = slice(j * FF_CHUNK, (j + 1) * FF_CHUNK)
            a = jnp.dot(u, w1_ref[j], preferred_element_type=f32)
            a_ref[:, js] = a.astype(bf16)
            hm[:, js] = jnp.square(jnp.maximum(a, 0.0)).astype(bf16)
        err = xv + jnp.dot(hm[...], w2_ref[...], preferred_element_type=f32) - t_ref[...]
        dy = err * (1.0 / D_MODEL)
        dy_ref[...] = dy
        dyb_ref[...] = dy.astype(bf16)
        ls_ref[...] = jnp.zeros((8, 128), f32) + jnp.sum(err * err) * (0.5 / D_MODEL)

    row = pl.BlockSpec((tm, D_MODEL), lambda i: (i, 0))
    wide = pl.BlockSpec((tm, D_FF), lambda i: (i, 0))
    return pl.pallas_call(
        body, name="ffn_fwd", grid=(T // tm,),
        in_specs=[row, _resident((1, D_MODEL)), _resident(w1.shape), _resident(w2.shape), row],
        out_specs=[wide, row, row, row, pl.BlockSpec((None, 8, 128), lambda i: (i, 0, 0))],
        out_shape=[S((T, D_FF), bf16), S((T, D_MODEL), bf16), S((T, D_MODEL), f32), S((T, D_MODEL), bf16),
                   S((T // tm, 8, 128), f32)],
        scratch_shapes=[pltpu.VMEM((tm, D_FF), bf16)],
        compiler_params=_params("parallel"),
    )(x1, g, w1, w2, target)


def _rms_bwd(du, xv, gv):
    r = lax.rsqrt(jnp.mean(xv * xv, axis=-1, keepdims=True) + EPS)
    xn = xv * r
    dg = jnp.sum(du * xn, axis=0, keepdims=True)
    dxn = du * gv
    dx = r * (dxn - xn * jnp.mean(dxn * xn, axis=-1, keepdims=True))
    return dx, dg


def _ffn_bwd(dy, dyb, a, x1, g, w1, w2, tm, ride=None):
    T = dy.shape[0]
    nj = w1.shape[0]

    def body(dy_ref, dyb_ref, a_ref, x_ref, g_ref, w1_ref, w2_ref, da_ref, dx_ref, dxb_ref, dg_ref):
        @pl.when(pl.program_id(0) == 0)
        def _():
            dg_ref[...] = jnp.zeros_like(dg_ref)

        dyb_v = dyb_ref[...]
        du = jnp.zeros((tm, D_MODEL), f32)
        for j in range(nj):
            js = slice(j * FF_CHUNK, (j + 1) * FF_CHUNK)
            dh = _nt(dyb_v, w2_ref[js, :])
            da = (dh * (2.0 * jnp.maximum(a_ref[:, js].astype(f32), 0.0))).astype(bf16)
            da_ref[:, js] = da
            du = du + _nt(da, w1_ref[j])
        dx, dg = _rms_bwd(du, x_ref[...], g_ref[...])
        dx1 = dy_ref[...] + dx
        dx_ref[...] = dx1
        dxb_ref[...] = dx1.astype(bf16)
        dg_ref[...] += dg

    row = pl.BlockSpec((tm, D_MODEL), lambda i: (i, 0))
    wide = pl.BlockSpec((tm, D_FF), lambda i: (i, 0))
    vec = pl.BlockSpec((1, D_MODEL), lambda i: (0, 0))
    return _call(
        body, name="ffn_bwd", grid=(T // tm,),
        in_specs=[row, row, wide, row, _resident((1, D_MODEL)), _resident(w1.shape), _resident(w2.shape)],
        out_specs=[wide, row, row, vec],
        out_shape=[S((T, D_FF), bf16), S((T, D_MODEL), f32), S((T, D_MODEL), bf16), S((1, D_MODEL), f32)],
        args=(dy, dyb, a, x1, g, w1, w2), ride=ride)


def _wgrad(name, a, b, tk, tn, tt, relu2=False, col_shard=False, out_dtype=bf16):
    T, Ka = a.shape
    Nb = b.shape[1]
    nt = T // tt

    def body(a_ref, b_ref, o_ref, acc):
        t = pl.program_id(2)
        av = a_ref[...]
        if relu2:
            av = jnp.square(jnp.maximum(av.astype(f32), 0.0))
        prod = _tn(av.astype(bf16), b_ref[...].astype(bf16))

        @pl.when(t == 0)
        def _():
            acc[...] = prod

        @pl.when(t > 0)
        def _():
            acc[...] += prod

        @pl.when(t == nt - 1)
        def _():
            o_ref[...] = acc[...].astype(out_dtype)

    if col_shard:
        out_shape = S((Nb // tn, Ka, tn), out_dtype)
        out_spec = pl.BlockSpec((None, tk, tn), lambda i, j, t: (j, i, 0))
    else:
        out_shape = S((Ka, Nb), out_dtype)
        out_spec = pl.BlockSpec((tk, tn), lambda i, j, t: (i, j))
    return pl.pallas_call(
        body, name=name, grid=(Ka // tk, Nb // tn, nt),
        in_specs=[pl.BlockSpec((tt, tk), lambda i, j, t: (t, i)), pl.BlockSpec((tt, tn), lambda i, j, t: (t, j))],
        out_specs=out_spec, out_shape=out_shape, scratch_shapes=[pltpu.VMEM((tk, tn), f32)],
        compiler_params=_params("parallel", "parallel", "arbitrary"),
    )(a, b)


def _mix_bwd(dx1, proj, attn, conv, h1, ln_g, ln_b, w_ao, w_co, w_o, tm, ride=None):
    T = dx1.shape[0]
    g0 = GLU_END // COL

    def gate_spec(off):
        return pl.BlockSpec((tm, COL), lambda i: (i, g0 + off))

    def body(dx_ref, ga0, ga1, gc0, gc1, at_ref, cv_ref, h_ref, g_ref, b_ref, wa_ref, wc_ref, wo_ref,
             da_ref, dc_ref, do_ref, dh1_ref, dg_ref, dlg_ref, dlb_ref, dbd_ref):
        @pl.when(pl.program_id(0) == 0)
        def _():
            dlg_ref[...] = jnp.zeros_like(dlg_ref)
            dlb_ref[...] = jnp.zeros_like(dlb_ref)
            dbd_ref[...] = jnp.zeros_like(dbd_ref)

        dm = _nt(dx_ref[...].astype(bf16), wo_ref[...])
        sa = _sigmoid(jnp.concatenate([ga0[...], ga1[...]], axis=-1).astype(f32))
        sc = _sigmoid(jnp.concatenate([gc0[...], gc1[...]], axis=-1).astype(f32))
        dattn = (dm * sa).astype(bf16)
        dconv = (dm * sc).astype(bf16)
        da_ref[...] = dattn
        dc_ref[...] = dconv
        dg_ref[:, 0:D_MODEL] = (dm * at_ref[...].astype(f32) * sa * (1.0 - sa)).astype(bf16)
        dg_ref[:, D_MODEL:2 * D_MODEL] = (dm * cv_ref[...].astype(f32) * sc * (1.0 - sc)).astype(bf16)
        do_ref[...] = _nt(dattn, wa_ref[...]).astype(bf16)
        dh3 = _nt(dconv, wc_ref[...])
        xhat, rstd = _layer_norm_stats(h_ref[...])
        h2 = xhat * g_ref[...] + b_ref[...]
        sg = _sigmoid(h2)
        dh2 = dh3 * (sg * (1.0 + h2 * (1.0 - sg)))
        dlg_ref[...] += jnp.sum(dh2 * xhat, axis=0, keepdims=True)
        dlb_ref[...] += jnp.sum(dh2, axis=0, keepdims=True)
        dxh = dh2 * g_ref[...]
        dh1 = rstd * (dxh - jnp.mean(dxh, axis=-1, keepdims=True) - xhat * jnp.mean(dxh * xhat, axis=-1, keepdims=True))
        dh1_ref[...] = dh1
        dbd_ref[...] += jnp.sum(dh1, axis=0, keepdims=True)

    row = pl.BlockSpec((tm, D_MODEL), lambda i: (i, 0))
    vec = pl.BlockSpec((1, D_MODEL), lambda i: (0, 0))
    par = _resident((1, D_MODEL))
    wsp = _resident((D_MODEL, D_MODEL))
    return _call(
        body, name="mix_bwd", grid=(T // tm,),
        in_specs=[row, gate_spec(0), gate_spec(1), gate_spec(2), gate_spec(3), row, row, row, par, par, wsp, wsp, wsp],
        out_specs=[row, row, row, row, pl.BlockSpec((tm, 2 * D_MODEL), lambda i: (i, 0)), vec, vec, vec],
        out_shape=[S((T, D_MODEL), bf16), S((T, D_MODEL), bf16), S((T, D_MODEL), bf16), S((T, D_MODEL), f32),
                   S((T, 2 * D_MODEL), bf16), S((1, D_MODEL), f32), S((1, D_MODEL), f32), S((1, D_MODEL), f32)],
        args=(dx1, proj, proj, proj, proj, attn, conv, h1, ln_g, ln_b, w_ao, w_co, w_o), ride=ride)


def _conv_bwd(dh1, h0, proj, w_dw, tm, ride=None):
    T = dh1.shape[0]
    per = tm // CONV_HALO
    nh = T // CONV_HALO
    nt = T // tm
    a0 = V_END // COL
    lead = CONV_HALO - (CONV_WIDTH - 1)

    def body(dc_ref, dn_ref, hc_ref, hp_ref, a0_ref, a1_ref, g0_ref, g1_ref, w_ref, dglu_ref, dw_ref, dcat, hcat, wacc, dh0):
        i = pl.program_id(0)

        @pl.when(i == 0)
        def _():
            wacc[...] = jnp.zeros_like(wacc)

        dcat[0:tm, :] = dc_ref[...]
        dcat[tm:, :] = jnp.where(i == nt - 1, 0.0, dn_ref[...])
        hcat[0:CONV_HALO, :] = jnp.where(i == 0, 0.0, hp_ref[...])
        hcat[CONV_HALO:, :] = hc_ref[...]
        span = CONV_UNIT + CONV_HALO

        def unit_rows(c, carry):
            r0 = pl.multiple_of(c * CONV_UNIT, CONV_UNIT)
            for j in range(D_MODEL // 128):
                ls = slice(j * 128, (j + 1) * 128)
                dwin = dcat[pl.ds(r0, span), ls]
                acc = jnp.zeros((CONV_UNIT, 128), f32)
                for r, adv in _advanced_windows(dwin):
                    for q, off in _tap_offsets(r, span):
                        k = CONV_WIDTH - 1 - off
                        if 0 <= k < CONV_WIDTH:
                            acc = acc + adv[8 * q:8 * q + CONV_UNIT] * w_ref[k:k + 1, ls]
                dh0[pl.ds(r0, CONV_UNIT), ls] = acc
                dcur = dwin[0:CONV_UNIT]
                for r, adv in _advanced_windows(hcat[pl.ds(r0, span), ls]):
                    for q, off in _tap_offsets(r, span):
                        k = off - lead
                        if 0 <= k < CONV_WIDTH:
                            prod = dcur * adv[8 * q:8 * q + CONV_UNIT]
                            wacc[k, :, ls] += jnp.sum(prod.reshape(CONV_UNIT // 8, 8, 128), axis=0)
            return carry

        lax.fori_loop(0, tm // CONV_UNIT, unit_rows, 0)
        dh0v = dh0[...]
        av = jnp.concatenate([a0_ref[...], a1_ref[...]], axis=-1).astype(f32)
        sg = _sigmoid(jnp.concatenate([g0_ref[...], g1_ref[...]], axis=-1).astype(f32))
        dglu_ref[:, 0:D_MODEL] = (dh0v * sg).astype(bf16)
        dglu_ref[:, D_MODEL:2 * D_MODEL] = (dh0v * av * sg * (1.0 - sg)).astype(bf16)

        @pl.when(i == nt - 1)
        def _():
            for k in range(CONV_WIDTH):
                dw_ref[k:k + 1, :] = jnp.sum(wacc[k], axis=0, keepdims=True)
            dw_ref[CONV_WIDTH:CONV_WIDTH + 1, :] = jnp.zeros((1, D_MODEL), f32)

    row = pl.BlockSpec((tm, D_MODEL), lambda i: (i, 0))

    def col_spec(off):
        return pl.BlockSpec((tm, COL), lambda i: (i, a0 + off))

    return _call(
        body, name="conv_bwd", grid=(nt,),
        in_specs=[row, pl.BlockSpec((CONV_HALO, D_MODEL), lambda i: (jnp.minimum((i + 1) * per, nh - 1), 0)),
                  row, pl.BlockSpec((CONV_HALO, D_MODEL), lambda i: (jnp.maximum(i * per - 1, 0), 0)),
                  col_spec(0), col_spec(1), col_spec(2), col_spec(3),
                  pl.BlockSpec((CONV_WIDTH, D_MODEL), lambda i: (0, 0))],
        out_specs=[pl.BlockSpec((tm, 2 * D_MODEL), lambda i: (i, 0)), pl.BlockSpec((CONV_WIDTH + 1, D_MODEL), lambda i: (0, 0))],
        out_shape=[S((T, 2 * D_MODEL), bf16), S((CONV_WIDTH + 1, D_MODEL), f32)],
        scratch_shapes=[pltpu.VMEM((tm + CONV_HALO, D_MODEL), f32), pltpu.VMEM((tm + CONV_HALO, D_MODEL), f32),
                        pltpu.VMEM((CONV_WIDTH, 8, D_MODEL), f32), pltpu.VMEM((tm, D_MODEL), f32)],
        args=(dh1, dh1, h0, h0, proj, proj, proj, proj, w_dw), ride=ride)


def _attn_bwd(qn, kn, vb, o, do, lse, bias, sinks):
    T = qn.shape[0]
    nb = T // QBLOCK

    def body(q_ref, kc_ref, kp_ref, vc_ref, vp_ref, o_ref, do_ref, lse_ref, b_ref, s_ref,
             dq_ref, dk_ref, dv_ref, db_ref, dsk_ref, kcar, vcar, s_scr, dp_scr, p_scr, ds_scr):
        n = pl.program_id(0)

        @pl.when(n == 0)
        def _():
            db_ref[...] = jnp.zeros_like(db_ref)
            dsk_ref[...] = jnp.zeros_like(dsk_ref)
            kcar[...] = jnp.zeros_like(kcar)
            vcar[...] = jnp.zeros_like(vcar)

        @pl.when(n < nb)
        def _():
            lane = lax.broadcasted_iota(jnp.int32, (QBLOCK, 2 * HEAD_DIM), 1)
            lane_row = lax.broadcasted_iota(jnp.int32, (1, 2 * HEAD_DIM), 1)
            kx = _kv_placements(jnp.concatenate([kp_ref[...], kc_ref[...]], axis=0))
            vx = _kv_placements(jnp.concatenate([vp_ref[...], vc_ref[...]], axis=0))
            lse_tile = lse_ref[...]
            delta, lse_c = {}, {}
            for pr in range(N_Q_HEADS // 2):
                dop = do_ref[:, _pair_cols(pr)]
                dl = dop.astype(f32) * o_ref[:, _pair_cols(pr)].astype(f32)
                for side in range(2):
                    hq = 2 * pr + side
                    h = hq // GROUP
                    qm = _one_head(q_ref[:, _pair_cols(pr)], side)
                    s_scr[_head_rows(hq), :] = _nt(qm, kx[h, side]) + b_ref[_head_rows(hq), :]
                    dp_scr[_head_rows(hq), :] = _nt(_one_head(dop, side), vx[h, side])
                    delta[hq] = jnp.sum(_one_head(dl, side), axis=-1, keepdims=True)
                    lse_c[hq] = jnp.sum(jnp.where(lane == hq, lse_tile, 0.0), axis=-1, keepdims=True)
            dsk = jnp.zeros((1, 2 * HEAD_DIM), f32)
            for hq in range(N_Q_HEADS):
                p = jnp.exp(s_scr[_head_rows(hq), :] - lse_c[hq])
                ds = p * (dp_scr[_head_rows(hq), :] - delta[hq])
                db_ref[_head_rows(hq), :] += ds
                p_scr[_head_rows(hq), :] = p.astype(bf16)
                ds_scr[_head_rows(hq), :] = ds.astype(bf16)
                psink = jnp.exp(s_ref[0, hq] - lse_c[hq])
                dsk = dsk - jnp.where(lane_row == hq, jnp.sum(psink * delta[hq], axis=0, keepdims=True), 0.0)
            dsk_ref[...] += dsk
            for pr in range(N_Q_HEADS // 2):
                h = 2 * pr // GROUP
                dq_ref[:, _pair_cols(pr)] = (jnp.dot(ds_scr[_head_rows(2 * pr), :], kx[h, 0], preferred_element_type=f32)
                                             + jnp.dot(ds_scr[_head_rows(2 * pr + 1), :], kx[h, 1], preferred_element_type=f32))
            folded_k, folded_v = [], []
            for h in range(N_KV_HEADS):
                ka = jnp.zeros((2 * QBLOCK, 2 * HEAD_DIM), f32)
                va = jnp.zeros((2 * QBLOCK, 2 * HEAD_DIM), f32)
                for g in range(GROUP):
                    hq = h * GROUP + g
                    ka = ka + _tn(ds_scr[_head_rows(hq), :], _one_head(q_ref[:, _pair_cols(hq // 2)], hq % 2))
                    va = va + _tn(p_scr[_head_rows(hq), :], _one_head(do_ref[:, _pair_cols(hq // 2)], hq % 2))
                folded_k.append(ka + _swap_halves(ka))
                folded_v.append(va + _swap_halves(va))
            low = _low_lanes()
            for m in range(N_KV_HEADS // 2):
                cs = _pair_cols(m)
                for folded, out_ref, car in ((folded_k, dk_ref, kcar), (folded_v, dv_ref, vcar)):
                    band = jnp.where(low, folded[2 * m], folded[2 * m + 1])
                    out_ref[:, cs] = car[:, cs] + band[0:QBLOCK, :]
                    car[:, cs] = band[QBLOCK:, :]

        @pl.when(n == nb)
        def _():
            dk_ref[...] = kcar[...]
            dv_ref[...] = vcar[...]

    cur = lambda n: (jnp.minimum(n, nb - 1), 0)
    prev = lambda n: (jnp.clip(n - 1, 0, nb - 1), 0)
    qspec = pl.BlockSpec((QBLOCK, ATTN_WIDTH), cur)
    kcur, kprev = pl.BlockSpec((QBLOCK, KV_WIDTH), cur), pl.BlockSpec((QBLOCK, KV_WIDTH), prev)
    whole = lambda shape: pl.BlockSpec(shape, lambda n: (0,) * len(shape))
    scores = (N_Q_HEADS * QBLOCK, 2 * QBLOCK)
    return pl.pallas_call(
        body, name="attn_bwd", grid=(nb + 1,),
        in_specs=[qspec, kcur, kprev, kcur, kprev, qspec, qspec, pl.BlockSpec((QBLOCK, 2 * HEAD_DIM), cur), _bias_spec(), SMEM],
        out_specs=[qspec, kprev, kprev, whole(scores), whole((1, 2 * HEAD_DIM))],
        out_shape=[S((T, ATTN_WIDTH), f32), S((T, KV_WIDTH), f32), S((T, KV_WIDTH), f32), S(scores, f32),
                   S((1, 2 * HEAD_DIM), f32)],
        scratch_shapes=[pltpu.VMEM((QBLOCK, KV_WIDTH), f32), pltpu.VMEM((QBLOCK, KV_WIDTH), f32),
                        pltpu.VMEM(scores, f32), pltpu.VMEM(scores, f32), pltpu.VMEM(scores, bf16), pltpu.VMEM(scores, bf16)],
        compiler_params=_params("arbitrary"),
    )(qn, kn, kn, vb, vb, o, do, lse, bias, sinks)


def _rel_bias_bwd(dbias, bucket):
    def body(d_ref, bk_ref, o_ref):
        b = bk_ref[...]
        for k in range(N_BUCKETS):
            mk = b == k
            for h in range(N_Q_HEADS):
                o_ref[k, h] = jnp.sum(jnp.where(mk, d_ref[h * QBLOCK:(h + 1) * QBLOCK, :], 0.0))

    return pl.pallas_call(body, name="rel_bias_bwd", out_shape=S((N_BUCKETS, N_Q_HEADS), f32), out_specs=SMEM)(dbias, bucket)


def _qk_norm_bwd(dq, dk, dv, proj, qg, kg, tm):
    T = dq.shape[0]
    scale = HEAD_DIM ** -0.5

    def pair_bwd(dy, x, gv):
        r = _pair_rstd(x)
        xn = x * r
        dxn = dy * gv
        dx = r * (dxn - xn * _pair_mean(dxn * xn))
        return dx, jnp.sum(dy * xn, axis=0, keepdims=True)

    def body(dq_ref, dk_ref, dv_ref, p_ref, qg_ref, kg_ref, out_ref, dqg_ref, dkg_ref):
        @pl.when(pl.program_id(0) == 0)
        def _():
            dqg_ref[...] = jnp.zeros_like(dqg_ref)
            dkg_ref[...] = jnp.zeros_like(dkg_ref)

        qgv, kgv = qg_ref[...], kg_ref[...]
        dqg = jnp.zeros((1, 2 * HEAD_DIM), f32)
        for pr in range(N_Q_HEADS // 2):
            dx, dg = pair_bwd(dq_ref[:, _pair_cols(pr)] * scale, p_ref[:, _pair_cols(pr)].astype(f32), qgv)
            out_ref[:, _pair_cols(pr)] = dx.astype(bf16)
            dqg = dqg + dg
        dkg = jnp.zeros((1, 2 * HEAD_DIM), f32)
        for pr in range(N_KV_HEADS // 2):
            ps = slice(Q_END + pr * 2 * HEAD_DIM, Q_END + (pr + 1) * 2 * HEAD_DIM)
            dx, dg = pair_bwd(dk_ref[:, _pair_cols(pr)], p_ref[:, ps].astype(f32), kgv)
            out_ref[:, ps] = dx.astype(bf16)
            dkg = dkg + dg
        out_ref[:, K_END:V_END] = dv_ref[...].astype(bf16)
        dqg_ref[...] += dqg
        dkg_ref[...] += dkg

    vec = pl.BlockSpec((1, 2 * HEAD_DIM), lambda i: (0, 0))
    return pl.pallas_call(
        body, name="qk_norm_bwd", grid=(T // tm,),
        in_specs=[pl.BlockSpec((tm, ATTN_WIDTH), lambda i: (i, 0)), pl.BlockSpec((tm, KV_WIDTH), lambda i: (i, 0)),
                  pl.BlockSpec((tm, KV_WIDTH), lambda i: (i, 0)), pl.BlockSpec((tm, V_END), lambda i: (i, 0)), vec, vec],
        out_specs=[pl.BlockSpec((tm, V_END), lambda i: (i, 0)), vec, vec],
        out_shape=[S((T, V_END), bf16), S((1, 2 * HEAD_DIM), f32), S((1, 2 * HEAD_DIM), f32)],
        compiler_params=_params("arbitrary"),
    )(dq, dk, dv, proj, qg, kg)


def _in_bwd(dqkv, dglu, dgates, w_in, x, g, dx1, tm, ride=None):
    T = x.shape[0]
    pieces = (dqkv, dglu, dgates)
    starts = [0, dqkv.shape[1], dqkv.shape[1] + dglu.shape[1]]

    def body(a0_ref, a1_ref, a2_ref, w_ref, x_ref, g_ref, d_ref, gx_ref, dg_ref):
        @pl.when(pl.program_id(0) == 0)
        def _():
            dg_ref[...] = jnp.zeros_like(dg_ref)

        du = jnp.zeros((tm, D_MODEL), f32)
        for a_ref, c0 in zip((a0_ref, a1_ref, a2_ref), starts):
            du = du + _nt(a_ref[...], w_ref[:, c0:c0 + a_ref.shape[1]])
        dx, dg = _rms_bwd(du, x_ref[...], g_ref[...])
        gx_ref[...] = d_ref[...] + dx
        dg_ref[...] += dg

    row = pl.BlockSpec((tm, D_MODEL), lambda i: (i, 0))
    return _call(
        body, name="in_bwd", grid=(T // tm,),
        in_specs=[pl.BlockSpec((tm, p.shape[1]), lambda i: (i, 0)) for p in pieces]
        + [_resident(w_in.shape), row, _resident((1, D_MODEL)), row],
        out_specs=[row, pl.BlockSpec((1, D_MODEL), lambda i: (0, 0))],
        out_shape=[S((T, D_MODEL), f32), S((1, D_MODEL), f32)],
        args=(dqkv, dglu, dgates, w_in, x, g, dx1), ride=ride)


def _adamw(name, parts, w, m, v, tr):
    R, C = w.shape
    bc1 = 1.0 - ADAM_B1 ** ADAM_STEP
    bc2 = 1.0 - ADAM_B2 ** ADAM_STEP

    def body(p_ref, w_ref, m_ref, v_ref, g_ref, d_ref, nm_ref, nv_ref):
        g = p_ref[0].astype(f32)
        for k in range(1, N_DEV):
            g = g + p_ref[k].astype(f32)
        nm = ADAM_B1 * m_ref[...] + (1.0 - ADAM_B1) * g
        nv = ADAM_B2 * v_ref[...] + (1.0 - ADAM_B2) * (g * g)
        g_ref[...] = g
        nm_ref[...] = nm
        nv_ref[...] = nv
        d_ref[...] = -ADAM_LR * ((nm / bc1) / (jnp.sqrt(nv / bc2) + ADAM_EPS) + ADAM_WD * w_ref[...])

    blk = pl.BlockSpec((tr, C), lambda i: (i, 0))
    return pl.pallas_call(
        body, name=name, grid=(R // tr,),
        in_specs=[pl.BlockSpec((N_DEV, tr, C), lambda i: (0, i, 0)), blk, blk, blk],
        out_specs=[blk, blk, blk, blk], out_shape=[S((R, C), f32)] * 4,
        compiler_params=_params("parallel"),
    )(parts, w, m, v)


def _tile(T, pref):
    return min(T, pref)


def _pad_rows(a, rows):
    return jnp.pad(a, ((0, rows - a.shape[0]), (0, 0)))


def kernel(x, norm_mix_g, w_in, q_norm_g, k_norm_g, attn_sinks, rel_bias, w_attn_o, w_dw, b_dw, conv_ln_g, conv_ln_b, w_conv_out, w_out, norm_mlp_g, w_ff1, w_ff2, loss_target, m_norm_mix_g, m_w_in, m_q_norm_g, m_k_norm_g, m_attn_sinks, m_rel_bias, m_w_attn_o, m_w_dw, m_b_dw, m_conv_ln_g, m_conv_ln_b, m_w_conv_out, m_w_out, m_norm_mlp_g, m_w_ff1, m_w_ff2, v_norm_mix_g, v_w_in, v_q_norm_g, v_k_norm_g, v_attn_sinks, v_rel_bias, v_w_attn_o, v_w_dw, v_b_dw, v_conv_ln_g, v_conv_ln_b, v_w_conv_out, v_w_out, v_norm_mlp_g, v_w_ff1, v_w_ff2):
    T = x.shape[1]
    xs = x[0]
    tgt = loss_target[0]
    in_shard = IN_WIDTH // N_DEV
    dw_rows = CONV_WIDTH + 1
    ch_shard = D_MODEL // N_DEV
    tc = _tile(T, 256)
    tt = _tile(T, 2048)
    bucket = jnp.asarray(_t5_bucket_table())

    g_in, g_dw = _exchange("gather_w_in", [w_in[0].astype(bf16), _pad_rows(w_dw[0], dw_rows)], gather=True, two_level=True)
    W_in = jnp.transpose(g_in, (1, 0, 2)).reshape(D_MODEL, IN_WIDTH)
    W_dw = jnp.transpose(g_dw, (1, 0, 2)).reshape(dw_rows, D_MODEL)[:CONV_WIDTH]

    mix_shards = _Gather([w_attn_o[0].astype(bf16), w_conv_out[0].astype(bf16), w_out[0].astype(bf16)])
    qg2, kg2 = jnp.tile(q_norm_g, (1, 2)), jnp.tile(k_norm_g, (1, 2))
    (proj, u, qn, kn, vb, h0), (g_ao, g_co, g_o) = _proj_fwd(xs, norm_mix_g, W_in, qg2, kg2, tc, ride=mix_shards)
    W_ao = g_ao.reshape(D_MODEL, D_MODEL)
    W_co = g_co.reshape(D_MODEL, D_MODEL)
    W_o = g_o.reshape(D_MODEL, D_MODEL)
    bias = _bias_table(rel_bias, bucket)
    (o, lse), (g_f1,) = _attn_fwd(qn, kn, vb, bias, attn_sinks, ride=_Gather([w_ff1[0].astype(bf16)]))
    (h1, h3), (g_f2,) = _conv_fwd(h0, W_dw, b_dw, conv_ln_g, conv_ln_b, tc, ride=_Gather([w_ff2[0].astype(bf16)]))
    x1, attn, conv, merged = _mix_fwd(xs, o, h3, proj, W_ao, W_co, W_o, tc)
    W_f2 = g_f2.reshape(D_FF, D_MODEL)
    a, u2, dy, dyb, loss_parts = _ffn_fwd(x1, norm_mlp_g, g_f1, W_f2, tgt, tc)
    loss = lax.psum(jnp.sum(loss_parts[:, 0, 0]), ("x", "y", "c"))

    gw_f2 = _wgrad("wgrad_ff2", a, dyb, D_MODEL, D_MODEL, tt, relu2=True).reshape(N_DEV, FF_CHUNK, D_MODEL)
    (da, dx1, dx1b, d_norm_mlp_g), (l_f2,) = _ffn_bwd(dy, dyb, a, x1, norm_mlp_g, g_f1, W_f2, tc,
                                                      ride=_Exchange([gw_f2], gather=False))
    gw_f1 = _wgrad("wgrad_ff1", u2, da, D_MODEL, FF_CHUNK, tt, col_shard=True)
    gw_o = _wgrad("wgrad_out", merged, dx1b, D_MODEL, D_MODEL, tt).reshape(N_DEV, ch_shard, D_MODEL)
    (dattn, dconv, do, dh1, dgates, d_ln_g, d_ln_b, d_b_dw), (l_o,) = _mix_bwd(
        dx1b, proj, attn, conv, h1, conv_ln_g, conv_ln_b, W_ao, W_co, W_o, tc, ride=_Exchange([gw_o], gather=False))
    gw_ao = _wgrad("wgrad_attn_o", o, dattn, D_MODEL, D_MODEL, tt).reshape(N_DEV, ch_shard, D_MODEL)
    gw_co = _wgrad("wgrad_conv_out", h3, dconv, D_MODEL, D_MODEL, tt).reshape(N_DEV, ch_shard, D_MODEL)
    (dglu, d_w_dw), (l_f1, l_ao, l_co) = _conv_bwd(dh1, h0, proj, W_dw, tc,
                                                   ride=_Exchange([gw_f1, gw_ao, gw_co], gather=False))
    dq, dk, dv, dbias, d_sinks = _attn_bwd(qn, kn, vb, o, do, lse, bias, attn_sinks)
    d_sinks = d_sinks[:, :N_Q_HEADS]
    d_rel_bias = _rel_bias_bwd(dbias, bucket)
    dqkv, d_qg, d_kg = _qk_norm_bwd(dq, dk, dv, proj, qg2, kg2, tc)
    d_qg = d_qg[:, :HEAD_DIM] + d_qg[:, HEAD_DIM:]
    d_kg = d_kg[:, :HEAD_DIM] + d_kg[:, HEAD_DIM:]
    gw_in = jnp.concatenate([_wgrad("wgrad_in_qkv", u, dqkv, D_MODEL, COL, tt),
                             _wgrad("wgrad_in_glu", u, dglu, D_MODEL, D_MODEL, tt),
                             _wgrad("wgrad_in_gates", u, dgates, D_MODEL, D_MODEL, tt)], axis=1)
    gw_in = jnp.transpose(gw_in.reshape(D_MODEL, N_DEV, in_shard), (1, 0, 2))
    gw_dw = jnp.transpose(d_w_dw.reshape(dw_rows, N_DEV, ch_shard), (1, 0, 2))
    (grad_x, d_norm_mix_g), (l_in, l_dw) = _in_bwd(dqkv, dglu, dgates, W_in, xs, norm_mix_g, dx1, tc,
                                                    ride=_Exchange([gw_in, gw_dw], gather=False))

    def row(vec):
        flat = vec.reshape(1, -1)
        return jnp.pad(flat, ((0, 0), (0, D_MODEL - flat.shape[1])))

    def pack_small(nm, qg, kg, sk, rb, bd, lg, lb, nl):
        tail = jnp.concatenate([qg.reshape(1, -1), kg.reshape(1, -1), sk.reshape(1, -1), rb.reshape(1, -1)], axis=1)
        return jnp.concatenate([row(nm), row(bd), row(lg), row(lb), row(nl), row(tail), jnp.zeros((2, D_MODEL), f32)], axis=0)

    def unpack_small(p):
        t = p[5]
        o0, o1, o2 = HEAD_DIM, 2 * HEAD_DIM, 2 * HEAD_DIM + N_Q_HEADS
        return dict(norm_mix_g=p[0:1], b_dw=p[1:2], conv_ln_g=p[2:3], conv_ln_b=p[3:4], norm_mlp_g=p[4:5],
                    q_norm_g=t[0:o0].reshape(1, HEAD_DIM), k_norm_g=t[o0:o1].reshape(1, HEAD_DIM),
                    attn_sinks=t[o1:o2].reshape(1, N_Q_HEADS),
                    rel_bias=t[o2:o2 + N_BUCKETS * N_Q_HEADS].reshape(N_BUCKETS, N_Q_HEADS))

    small_g = pack_small(d_norm_mix_g, d_qg, d_kg, d_sinks, d_rel_bias, d_b_dw, d_ln_g, d_ln_b, d_norm_mlp_g)
    (l_small,) = _exchange("gather_small_grads", [small_g], gather=True)


    res = {}
    res["w_in"] = _adamw("adamw_in", l_in, w_in[0], m_w_in[0], v_w_in[0], 256)
    res["w_attn_o"] = _adamw("adamw_attn_o", l_ao, w_attn_o[0], m_w_attn_o[0], v_w_attn_o[0], ch_shard)
    res["w_conv_out"] = _adamw("adamw_conv_out", l_co, w_conv_out[0], m_w_conv_out[0], v_w_conv_out[0], ch_shard)
    res["w_out"] = _adamw("adamw_out", l_o, w_out[0], m_w_out[0], v_w_out[0], ch_shard)
    res["w_ff1"] = _adamw("adamw_ff1", l_f1, w_ff1[0], m_w_ff1[0], v_w_ff1[0], 256)
    res["w_ff2"] = _adamw("adamw_ff2", l_f2, w_ff2[0], m_w_ff2[0], v_w_ff2[0], 256)
    dw4 = _adamw("adamw_dw", l_dw, _pad_rows(w_dw[0], dw_rows), _pad_rows(m_w_dw[0], dw_rows), _pad_rows(v_w_dw[0], dw_rows), dw_rows)
    res["w_dw"] = [t[:CONV_WIDTH] for t in dw4]
    small_w = pack_small(norm_mix_g, q_norm_g, k_norm_g, attn_sinks, rel_bias, b_dw, conv_ln_g, conv_ln_b, norm_mlp_g)
    small_m = pack_small(m_norm_mix_g, m_q_norm_g, m_k_norm_g, m_attn_sinks, m_rel_bias, m_b_dw, m_conv_ln_g, m_conv_ln_b, m_norm_mlp_g)
    small_v = pack_small(v_norm_mix_g, v_q_norm_g, v_k_norm_g, v_attn_sinks, v_rel_bias, v_b_dw, v_conv_ln_g, v_conv_ln_b, v_norm_mlp_g)
    small4 = [unpack_small(t) for t in _adamw("adamw_small", l_small, small_w, small_m, small_v, 8)]

    order = ["norm_mix_g", "w_in", "q_norm_g", "k_norm_g", "attn_sinks", "rel_bias", "w_attn_o", "w_dw", "b_dw",
             "conv_ln_g", "conv_ln_b", "w_conv_out", "w_out", "norm_mlp_g", "w_ff1", "w_ff2"]
    stacked = {"w_in", "w_attn_o", "w_dw", "w_conv_out", "w_out", "w_ff1", "w_ff2"}
    outs = [loss, grad_x[None]]
    for k in range(4):
        for nme in order:
            if nme in stacked:
                outs.append(res[nme][k][None])
            else:
                outs.append(small4[k][nme])
    return tuple(outs)
```

```python
import functools

import numpy as np
import jax
import jax.numpy as jnp
from jax import lax
from jax.experimental import pallas as pl
from jax.experimental.pallas import tpu as pltpu

f32 = jnp.float32
bf16 = jnp.bfloat16
S = jax.ShapeDtypeStruct

N_DEV = 8
D_MODEL = 1024
HEAD_DIM = 64
N_Q_HEADS = 16
N_KV_HEADS = 4
GROUP = N_Q_HEADS // N_KV_HEADS
ATTN_WIDTH = N_Q_HEADS * HEAD_DIM
KV_WIDTH = N_KV_HEADS * HEAD_DIM
QBLOCK = 128
CONV_WIDTH = 31
CONV_HALO = 32
CONV_UNIT = 64
D_FF = 4 * D_MODEL
N_BUCKETS = 32
MAX_DISTANCE = 128
EPS = 1e-6
NEG = -1e30
Q_END = ATTN_WIDTH
K_END = Q_END + KV_WIDTH
V_END = K_END + KV_WIDTH
GLU_END = V_END + 2 * D_MODEL
IN_WIDTH = GLU_END + 2 * D_MODEL
COL = 512
FF_CHUNK = D_FF // N_DEV

ADAM_LR = 0.001
ADAM_B1 = 0.9
ADAM_B2 = 0.999
ADAM_EPS = 1e-08
ADAM_WD = 0.01
ADAM_STEP = 10

VMEM_LIMIT = 56 * 1024 * 1024

MESH_ID = pl.DeviceIdType.MESH
ANY = pl.BlockSpec(memory_space=pl.ANY)
SMEM = pl.BlockSpec(memory_space=pltpu.SMEM)


def _params(*sem):
    return pltpu.CompilerParams(dimension_semantics=sem, vmem_limit_bytes=VMEM_LIMIT)


def _nt(a, b):
    return lax.dot_general(a, b, (((1,), (1,)), ((), ())), preferred_element_type=f32)


def _tn(a, b):
    return lax.dot_general(a, b, (((0,), (0,)), ((), ())), preferred_element_type=f32)


def _sigmoid(z):
    return 1.0 / (1.0 + jnp.exp(-z))


def _t5_bucket_table():
    qi = np.arange(QBLOCK, dtype=np.int32)[:, None]
    kj = np.arange(2 * QBLOCK, dtype=np.int32)[None, :]
    dist = qi + QBLOCK - kj
    n = np.maximum(dist, 0)
    max_exact = N_BUCKETS // 2
    nf = np.maximum(n, 1).astype(np.float32)
    large = max_exact + (np.log(nf / np.float32(max_exact)) / np.float32(np.log(MAX_DISTANCE / max_exact))
                         * np.float32(N_BUCKETS - max_exact)).astype(np.int32)
    large = np.minimum(large, N_BUCKETS - 1)
    bucket = np.where(n < max_exact, n, large)
    valid = (dist >= 0) & (dist < QBLOCK)
    return np.where(valid, bucket, -1).astype(np.int32)


def _peer(d):
    x, y, c = lax.axis_index("x"), lax.axis_index("y"), lax.axis_index("c")
    dx, dy, dc = (d >> 2) & 1, (d >> 1) & 1, d & 1
    px, py, pc = x ^ dx, y ^ dy, c ^ dc
    return (px, py, pc), 4 * px + 2 * py + pc


class _Exchange:
    def __init__(self, arrays, gather):
        self.arrays, self.gather, self.n = list(arrays), gather, len(arrays)
        self.out_shape = [S(((N_DEV,) + a.shape) if gather else a.shape, a.dtype) for a in self.arrays]
        self.scratch = [pltpu.SemaphoreType.DMA((self.n, N_DEV - 1)), pltpu.SemaphoreType.DMA((self.n, N_DEV - 1)),
                        pltpu.SemaphoreType.DMA((self.n,))]

    def _copies(self, ins, outs, sems):
        send_sems, recv_sems, local_sems = sems
        _, me = _peer(0)
        local, sends, recvs = [], [], []
        for k in range(self.n):
            src = ins[k] if self.gather else ins[k].at[me]
            local.append(pltpu.make_async_copy(src, outs[k].at[me], local_sems.at[k]))
        for d in range(1, N_DEV):
            peer, pidx = _peer(d)
            for k in range(self.n):
                src = ins[k] if self.gather else ins[k].at[pidx]
                common = dict(src_ref=src, send_sem=send_sems.at[k, d - 1], recv_sem=recv_sems.at[k, d - 1],
                              device_id=peer, device_id_type=MESH_ID)
                sends.append(pltpu.make_async_remote_copy(dst_ref=outs[k].at[me], **common))
                recvs.append(pltpu.make_async_remote_copy(dst_ref=outs[k].at[pidx], **common))
        return local, sends, recvs

    def start(self, ins, outs, sems):
        local, sends, _ = self._copies(ins, outs, sems)
        for cp in local + sends:
            cp.start()

    def wait(self, ins, outs, sems):
        local, sends, recvs = self._copies(ins, outs, sems)
        for cp in recvs:
            cp.wait_recv()
        for cp in sends:
            cp.wait_send()
        for cp in local:
            cp.wait()


class _Gather:
    CHIPS = (4, 2, 6)
    SLOTS = 1 + 2 * len(CHIPS)

    def __init__(self, arrays):
        self.arrays, self.n = list(arrays), len(arrays)
        self.out_shape = [S((N_DEV,) + a.shape, a.dtype) for a in self.arrays]
        self.scratch = [pltpu.SemaphoreType.DMA((self.n, self.SLOTS)), pltpu.SemaphoreType.DMA((self.n, self.SLOTS)),
                        pltpu.SemaphoreType.DMA((self.n,))]

    @staticmethod
    def _copy(outs, sems, k, slot, src, block, to):
        return pltpu.make_async_remote_copy(src_ref=src, dst_ref=outs[k].at[block], send_sem=sems[0].at[k, slot],
                                            recv_sem=sems[1].at[k, slot], device_id=to, device_id_type=MESH_ID)

    def _local(self, ins, outs, sems):
        _, me = _peer(0)
        return [pltpu.make_async_copy(ins[k], outs[k].at[me], sems[2].at[k]) for k in range(self.n)]

    def start(self, ins, outs, sems):
        _, me = _peer(0)
        sibling, _ = _peer(1)
        for cp in self._local(ins, outs, sems):
            cp.start()
        for k in range(self.n):
            self._copy(outs, sems, k, 0, ins[k], me, sibling).start()
            for j, d in enumerate(self.CHIPS):
                self._copy(outs, sems, k, 1 + j, ins[k], me, _peer(d)[0]).start()

    def mid(self, ins, outs, sems):
        sibling, _ = _peer(1)
        for j, d in enumerate(self.CHIPS):
            chip, block = _peer(d)
            for k in range(self.n):
                self._copy(outs, sems, k, 1 + j, ins[k], block, chip).wait_recv()
                self._copy(outs, sems, k, 4 + j, outs[k].at[block], block, sibling).start()

    def wait(self, ins, outs, sems):
        _, me = _peer(0)
        sibling, sib_block = _peer(1)
        for k in range(self.n):
            self._copy(outs, sems, k, 0, ins[k], sib_block, sibling).wait_recv()
            for j, d in enumerate(self.CHIPS):
                self._copy(outs, sems, k, 4 + j, ins[k], _peer(d ^ 1)[1], sibling).wait_recv()
        for k in range(self.n):
            self._copy(outs, sems, k, 0, ins[k], me, sibling).wait_send()
            for j, d in enumerate(self.CHIPS):
                chip, block = _peer(d)
                self._copy(outs, sems, k, 1 + j, ins[k], me, chip).wait_send()
                self._copy(outs, sems, k, 4 + j, outs[k].at[block], block, sibling).wait_send()
        for cp in self._local(ins, outs, sems):
            cp.wait()


def _exchange(name, arrays, gather, two_level=False):
    ex = _Gather(arrays) if two_level else _Exchange(arrays, gather)
    n = ex.n

    def body(*refs):
        ins, outs, sems = refs[:n], refs[n:2 * n], refs[2 * n:]
        ex.start(ins, outs, sems)
        if two_level:
            ex.mid(ins, outs, sems)
        ex.wait(ins, outs, sems)

    return pl.pallas_call(body, name=name, out_shape=ex.out_shape, in_specs=[ANY] * n, out_specs=[ANY] * n,
                          scratch_shapes=ex.scratch)(*arrays)


def _call(body, *, name, grid, in_specs, out_specs, out_shape, args, scratch_shapes=(), ride=None):
    n_in, n_out, n_sc = len(in_specs), len(out_specs), len(scratch_shapes)
    sem = ("arbitrary",) * len(grid)
    if ride is None:
        res = pl.pallas_call(body, name=name, grid=grid, in_specs=list(in_specs), out_specs=list(out_specs),
                             out_shape=list(out_shape), scratch_shapes=list(scratch_shapes), compiler_params=_params(*sem))(*args)
        return list(res), []
    nx = ride.n

    def riding(*refs):
        ins, xin = refs[:n_in], refs[n_in:n_in + nx]
        outs, xout = refs[n_in + nx:n_in + nx + n_out], refs[n_in + nx + n_out:n_in + 2 * nx + n_out]
        rest = refs[n_in + 2 * nx + n_out:]
        scratch, sems = rest[:n_sc], rest[n_sc:]
        ids = [pl.program_id(ax) for ax in range(len(grid))]
        first = functools.reduce(jnp.logical_and, [i == 0 for i in ids])
        last = functools.reduce(jnp.logical_and, [i == g - 1 for i, g in zip(ids, grid)])

        @pl.when(first)
        def _():
            ride.start(xin, xout, sems)

        if hasattr(ride, "mid"):
            halfway = functools.reduce(jnp.logical_and, [ids[0] == grid[0] // 2] + [i == 0 for i in ids[1:]])

            @pl.when(halfway)
            def _():
                ride.mid(xin, xout, sems)

        body(*ins, *outs, *scratch)

        @pl.when(last)
        def _():
            ride.wait(xin, xout, sems)

    res = pl.pallas_call(
        riding, name=name, grid=grid, in_specs=list(in_specs) + [ANY] * nx, out_specs=list(out_specs) + [ANY] * nx,
        out_shape=list(out_shape) + ride.out_shape, scratch_shapes=list(scratch_shapes) + ride.scratch,
        compiler_params=_params(*sem))(*args, *ride.arrays)
    return list(res[:n_out]), list(res[n_out:])


def _resident(shape):
    return pl.BlockSpec(shape, lambda *_: (0,) * len(shape), pipeline_mode=pl.Buffered(1))


def _proj_fwd(x, g, w, qg, kg, tm, ride=None):
    T, K = x.shape
    N = w.shape[1]
    per = COL // (2 * HEAD_DIM)
    assert Q_END % COL == 0 and V_END == Q_END + COL and (GLU_END - V_END) == 4 * COL and KV_WIDTH == COL // 2

    def body(x_ref, g_ref, w_ref, qg_ref, kg_ref, o_ref, u_ref, qn_ref, kn_ref, vb_ref, h0_ref):
        xv = x_ref[...]
        r = lax.rsqrt(jnp.mean(xv * xv, axis=-1, keepdims=True) + EPS)
        u = (xv * r * g_ref[...]).astype(bf16)
        u_ref[...] = u

        def block(c):
            cs = slice(c * COL, (c + 1) * COL)
            pc = jnp.dot(u, w_ref[:, cs], preferred_element_type=f32)
            o_ref[:, cs] = pc.astype(bf16)
            return pc

        qgv = qg_ref[...] * (HEAD_DIM ** -0.5)
        for c in range(Q_END // COL):
            pc = block(c)
            for t in range(per):
                xq = pc[:, _pair_cols(t)]
                qn_ref[:, _pair_cols(c * per + t)] = (xq * _pair_rstd(xq) * qgv).astype(bf16)
        pc = block(Q_END // COL)
        for t in range(KV_WIDTH // (2 * HEAD_DIM)):
            xk = pc[:, _pair_cols(t)]
            kn_ref[:, _pair_cols(t)] = (xk * _pair_rstd(xk) * kg_ref[...]).astype(bf16)
        vb_ref[...] = pc[:, KV_WIDTH:].astype(bf16)
        a0 = V_END // COL
        for half in range(2):
            gate = block(a0 + 2 + half)
            h0_ref[:, half * COL:(half + 1) * COL] = block(a0 + half) * _sigmoid(gate)
        for c in range(GLU_END // COL, N // COL):
            block(c)

    row = lambda width: pl.BlockSpec((tm, width), lambda i: (i, 0))
    return _call(
        body, name="proj_fwd", grid=(T // tm,),
        in_specs=[row(K), _resident((1, K)), _resident((K, N)), _resident((1, 2 * HEAD_DIM)), _resident((1, 2 * HEAD_DIM))],
        out_specs=[row(N), row(K), row(ATTN_WIDTH), row(KV_WIDTH), row(KV_WIDTH), row(D_MODEL)],
        out_shape=[S((T, N), bf16), S((T, K), bf16), S((T, ATTN_WIDTH), bf16), S((T, KV_WIDTH), bf16), S((T, KV_WIDTH), bf16),
                   S((T, D_MODEL), f32)],
        args=(x, g, w, qg, kg), ride=ride)


def _bias_table(rel_bias, bucket):
    def body(rb_ref, bk_ref, o_ref):
        b = bk_ref[...]
        absent = lax.broadcasted_iota(jnp.int32, (QBLOCK, 2 * QBLOCK), 1) < QBLOCK
        for h in range(N_Q_HEADS):
            acc = jnp.full((QBLOCK, 2 * QBLOCK), NEG, f32)
            for k in range(N_BUCKETS):
                acc = jnp.where(b == k, rb_ref[k, h], acc)
            o_ref[0, h * QBLOCK:(h + 1) * QBLOCK, :] = acc
            o_ref[1, h * QBLOCK:(h + 1) * QBLOCK, :] = jnp.where(absent, NEG, acc)

    return pl.pallas_call(
        body, name="bias_table", out_shape=S((2, N_Q_HEADS * QBLOCK, 2 * QBLOCK), f32),
        in_specs=[SMEM, pl.BlockSpec(memory_space=pltpu.VMEM)],
    )(rel_bias, bucket)


def _bias_spec():
    return pl.BlockSpec((None, N_Q_HEADS * QBLOCK, 2 * QBLOCK), lambda n: (jnp.where(n == 0, 1, 0), 0, 0))


def _swap_halves(t):
    return jnp.concatenate([t[:, HEAD_DIM:], t[:, :HEAD_DIM]], axis=1)


def _low_lanes():
    return lax.broadcasted_iota(jnp.int32, (1, 2 * HEAD_DIM), 1) < HEAD_DIM


def _one_head(pair, side):
    zero = jnp.zeros((), pair.dtype)
    return jnp.where(_low_lanes(), pair, zero) if side == 0 else jnp.where(_low_lanes(), zero, pair)


def _pair_mean(t):
    width = 2 * HEAD_DIM
    same_head = ((lax.broadcasted_iota(jnp.int32, (width, width), 0) < HEAD_DIM)
                 == (lax.broadcasted_iota(jnp.int32, (width, width), 1) < HEAD_DIM))
    e = jnp.where(same_head, 1.0 / HEAD_DIM, 0.0).astype(bf16)
    hi = t.astype(bf16)
    lo = (t - hi.astype(f32)).astype(bf16)
    return jnp.dot(hi, e, preferred_element_type=f32) + jnp.dot(lo, e, preferred_element_type=f32)


def _pair_rstd(x):
    return lax.rsqrt(_pair_mean(x * x) + EPS)


def _kv_placements(band):
    out = {}
    for m in range(N_KV_HEADS // 2):
        pair = band[:, m * 2 * HEAD_DIM:(m + 1) * 2 * HEAD_DIM]
        swapped = _swap_halves(pair)
        for hh in range(2):
            out[2 * m + hh, 0] = _one_head(pair if hh == 0 else swapped, 0)
            out[2 * m + hh, 1] = _one_head(swapped if hh == 0 else pair, 1)
    return out


def _head_rows(hq):
    return slice(hq * QBLOCK, (hq + 1) * QBLOCK)


def _pair_cols(pr):
    return slice(pr * 2 * HEAD_DIM, (pr + 1) * 2 * HEAD_DIM)


def _attn_fwd(qn, kn, vb, bias, sinks, ride=None):
    T = qn.shape[0]
    nb = T // QBLOCK

    def body(q_ref, kc_ref, kp_ref, vc_ref, vp_ref, b_ref, s_ref, o_ref, lse_ref, s_scr, p_scr):
        lane = lax.broadcasted_iota(jnp.int32, (QBLOCK, 2 * HEAD_DIM), 1)
        kx = _kv_placements(jnp.concatenate([kp_ref[...], kc_ref[...]], axis=0))
        vx = _kv_placements(jnp.concatenate([vp_ref[...], vc_ref[...]], axis=0))
        for hq in range(N_Q_HEADS):
            qm = _one_head(q_ref[:, _pair_cols(hq // 2)], hq % 2)
            s_scr[_head_rows(hq), :] = _nt(qm, kx[hq // GROUP, hq % 2]) + b_ref[_head_rows(hq), :]
        lse_tile = jnp.zeros((QBLOCK, 2 * HEAD_DIM), f32)
        for hq in range(N_Q_HEADS):
            s = s_scr[_head_rows(hq), :]
            sink = s_ref[0, hq]
            m = jnp.maximum(jnp.max(s, axis=-1, keepdims=True), sink)
            p = jnp.exp(s - m)
            l = jnp.sum(p, axis=-1, keepdims=True) + jnp.exp(sink - m)
            p_scr[_head_rows(hq), :] = (p * (1.0 / l)).astype(bf16)
            lse_tile = jnp.where(lane == hq, m + jnp.log(l), lse_tile)
        lse_ref[...] = lse_tile
        for pr in range(N_Q_HEADS // 2):
            h = 2 * pr // GROUP
            o_pair = (jnp.dot(p_scr[_head_rows(2 * pr), :], vx[h, 0], preferred_element_type=f32)
                      + jnp.dot(p_scr[_head_rows(2 * pr + 1), :], vx[h, 1], preferred_element_type=f32))
            o_ref[:, _pair_cols(pr)] = o_pair.astype(bf16)

    cur = lambda n: (n, 0)
    prev = lambda n: (jnp.maximum(n - 1, 0), 0)
    return _call(
        body, name="attn_fwd", grid=(nb,),
        in_specs=[pl.BlockSpec((QBLOCK, ATTN_WIDTH), cur), pl.BlockSpec((QBLOCK, KV_WIDTH), cur),
                  pl.BlockSpec((QBLOCK, KV_WIDTH), prev), pl.BlockSpec((QBLOCK, KV_WIDTH), cur),
                  pl.BlockSpec((QBLOCK, KV_WIDTH), prev), _bias_spec(), SMEM],
        out_specs=[pl.BlockSpec((QBLOCK, ATTN_WIDTH), cur), pl.BlockSpec((QBLOCK, 2 * HEAD_DIM), cur)],
        out_shape=[S((T, ATTN_WIDTH), bf16), S((T, 2 * HEAD_DIM), f32)],
        scratch_shapes=[pltpu.VMEM((N_Q_HEADS * QBLOCK, 2 * QBLOCK), f32), pltpu.VMEM((N_Q_HEADS * QBLOCK, 2 * QBLOCK), bf16)],
        args=(qn, kn, kn, vb, vb, bias, sinks), ride=ride)


def _layer_norm_stats(h1):
    mu = jnp.mean(h1, axis=-1, keepdims=True)
    xc = h1 - mu
    rstd = lax.rsqrt(jnp.mean(xc * xc, axis=-1, keepdims=True) + EPS)
    return xc * rstd, rstd


def _advanced_windows(win):
    rows = win.shape[0]
    for r in range(8):
        yield r, (win if r == 0 else pltpu.roll(win, rows - r, 0))


def _tap_offsets(r, rows):
    for q in range((rows - CONV_UNIT) // 8 + 1):
        if r == 0 or 8 * q + r + CONV_UNIT <= rows:
            yield q, 8 * q + r


def _conv_fwd(h0, w_dw, b_dw, ln_g, ln_b, tm, ride=None):
    T = h0.shape[0]
    per = tm // CONV_HALO
    lead = CONV_HALO - (CONV_WIDTH - 1)

    def body(hc_ref, hp_ref, w_ref, b_ref, g_ref, bb_ref, h1_ref, h3_ref, cat):
        i = pl.program_id(0)
        cat[0:CONV_HALO, :] = jnp.where(i == 0, 0.0, hp_ref[...])
        cat[CONV_HALO:, :] = hc_ref[...]

        def unit_rows(c, carry):
            r0 = pl.multiple_of(c * CONV_UNIT, CONV_UNIT)
            for j in range(D_MODEL // 128):
                ls = slice(j * 128, (j + 1) * 128)
                win = cat[pl.ds(r0, CONV_UNIT + CONV_HALO), ls]
                acc = jnp.zeros((CONV_UNIT, 128), f32) + b_ref[:, ls]
                for r, adv in _advanced_windows(win):
                    for q, off in _tap_offsets(r, CONV_UNIT + CONV_HALO):
                        k = off - lead
                        if 0 <= k < CONV_WIDTH:
                            acc = acc + adv[8 * q:8 * q + CONV_UNIT] * w_ref[k:k + 1, ls]
                h1_ref[pl.ds(r0, CONV_UNIT), ls] = acc
            return carry

        lax.fori_loop(0, tm // CONV_UNIT, unit_rows, 0)
        acc = h1_ref[...]
        xhat, _ = _layer_norm_stats(acc)
        h2 = xhat * g_ref[...] + bb_ref[...]
        h3_ref[...] = (h2 * _sigmoid(h2)).astype(bf16)

    vec = pl.BlockSpec((1, D_MODEL), lambda i: (0, 0))
    return _call(
        body, name="conv_fwd", grid=(T // tm,),
        in_specs=[pl.BlockSpec((tm, D_MODEL), lambda i: (i, 0)),
                  pl.BlockSpec((CONV_HALO, D_MODEL), lambda i: (jnp.maximum(i * per - 1, 0), 0)),
                  pl.BlockSpec((CONV_WIDTH, D_MODEL), lambda i: (0, 0)), vec, vec, vec],
        out_specs=[pl.BlockSpec((tm, D_MODEL), lambda i: (i, 0)), pl.BlockSpec((tm, D_MODEL), lambda i: (i, 0))],
        out_shape=[S((T, D_MODEL), f32), S((T, D_MODEL), bf16)],
        scratch_shapes=[pltpu.VMEM((tm + CONV_HALO, D_MODEL), f32)],
        args=(h0, h0, w_dw, b_dw, ln_g, ln_b), ride=ride)


def _mix_fwd(x, o, h3, proj, w_ao, w_co, w_o, tm):
    T = x.shape[0]
    row = pl.BlockSpec((tm, D_MODEL), lambda i: (i, 0))
    wsp = _resident((D_MODEL, D_MODEL))
    g0 = GLU_END // COL

    def gate_spec(off):
        return pl.BlockSpec((tm, COL), lambda i: (i, g0 + off))

    def body(x_ref, o_ref, h3_ref, ga0, ga1, gc0, gc1, wa_ref, wc_ref, wo_ref, x1_ref, at_ref, cv_ref, mg_ref):
        attn = jnp.dot(o_ref[...], wa_ref[...], preferred_element_type=f32)
        conv = jnp.dot(h3_ref[...], wc_ref[...], preferred_element_type=f32)
        ga = jnp.concatenate([ga0[...], ga1[...]], axis=-1).astype(f32)
        gc = jnp.concatenate([gc0[...], gc1[...]], axis=-1).astype(f32)
        merged = (_sigmoid(ga) * attn + _sigmoid(gc) * conv).astype(bf16)
        at_ref[...] = attn.astype(bf16)
        cv_ref[...] = conv.astype(bf16)
        mg_ref[...] = merged
        x1_ref[...] = x_ref[...] + jnp.dot(merged, wo_ref[...], preferred_element_type=f32)

    return pl.pallas_call(
        body, name="mix_fwd", grid=(T // tm,),
        in_specs=[row, row, row, gate_spec(0), gate_spec(1), gate_spec(2), gate_spec(3), wsp, wsp, wsp],
        out_specs=[row, row, row, row],
        out_shape=[S((T, D_MODEL), f32), S((T, D_MODEL), bf16), S((T, D_MODEL), bf16), S((T, D_MODEL), bf16)],
        compiler_params=_params("parallel"),
    )(x, o, h3, proj, proj, proj, proj, w_ao, w_co, w_o)


def _ffn_fwd(x1, g, w1, w2, target, tm):
    T = x1.shape[0]
    nj = w1.shape[0]

    def body(x_ref, g_ref, w1_ref, w2_ref, t_ref, a_ref, u_ref, dy_ref, dyb_ref, ls_ref, hm):
        xv = x_ref[...]
        r = lax.rsqrt(jnp.mean(xv * xv, axis=-1, keepdims=True) + EPS)
        u = (xv * r * g_ref[...]).astype(bf16)
        u_ref[...] = u
        for j in range(nj):
            js = slice(j * FF_CHUNK, (j + 1) * FF_CHUNK)
            a = jnp.dot(u, w1_ref[j], preferred_element_type=f32)
            a_ref[:, js] = a.astype(bf16)
            hm[:, js] = jnp.square(jnp.maximum(a, 0.0)).astype(bf16)
        err = xv + jnp.dot(hm[...], w2_ref[...], preferred_element_type=f32) - t_ref[...]
        dy = err * (1.0 / D_MODEL)
        dy_ref[...] = dy
        dyb_ref[...] = dy.astype(bf16)
        ls_ref[...] = jnp.zeros((8, 128), f32) + jnp.sum(err * err) * (0.5 / D_MODEL)

    row = pl.BlockSpec((tm, D_MODEL), lambda i: (i, 0))
    wide = pl.BlockSpec((tm, D_FF), lambda i: (i, 0))
    return pl.pallas_call(
        body, name="ffn_fwd", grid=(T // tm,),
        in_specs=[row, _resident((1, D_MODEL)), _resident(w1.shape), _resident(w2.shape), row],
        out_specs=[wide, row, row, row, pl.BlockSpec((None, 8, 128), lambda i: (i, 0, 0))],
        out_shape=[S((T, D_FF), bf16), S((T, D_MODEL), bf16), S((T, D_MODEL), f32), S((T, D_MODEL), bf16),
                   S((T // tm, 8, 128), f32)],
        scratch_shapes=[pltpu.VMEM((tm, D_FF), bf16)],
        compiler_params=_params("parallel"),
    )(x1, g, w1, w2, target)


def _rms_bwd(du, xv, gv):
    r = lax.rsqrt(jnp.mean(xv * xv, axis=-1, keepdims=True) + EPS)
    xn = xv * r
    dg = jnp.sum(du * xn, axis=0, keepdims=True)
    dxn = du * gv
    dx = r * (dxn - xn * jnp.mean(dxn * xn, axis=-1, keepdims=True))
    return dx, dg


def _ffn_bwd(dy, dyb, a, x1, g, w1, w2, tm, ride=None):
    T = dy.shape[0]
    nj = w1.shape[0]

    def body(dy_ref, dyb_ref, a_ref, x_ref, g_ref, w1_ref, w2_ref, da_ref, dx_ref, dxb_ref, dg_ref):
        @pl.when(pl.program_id(0) == 0)
        def _():
            dg_ref[...] = jnp.zeros_like(dg_ref)

        dyb_v = dyb_ref[...]
        du = jnp.zeros((tm, D_MODEL), f32)
        for j in range(nj):
            js = slice(j * FF_CHUNK, (j + 1) * FF_CHUNK)
            dh = _nt(dyb_v, w2_ref[js, :])
            da = (dh * (2.0 * jnp.maximum(a_ref[:, js].astype(f32), 0.0))).astype(bf16)
            da_ref[:, js] = da
            du = du + _nt(da, w1_ref[j])
        dx, dg = _rms_bwd(du, x_ref[...], g_ref[...])
        dx1 = dy_ref[...] + dx
        dx_ref[...] = dx1
        dxb_ref[...] = dx1.astype(bf16)
        dg_ref[...] += dg

    row = pl.BlockSpec((tm, D_MODEL), lambda i: (i, 0))
    wide = pl.BlockSpec((tm, D_FF), lambda i: (i, 0))
    vec = pl.BlockSpec((1, D_MODEL), lambda i: (0, 0))
    return _call(
        body, name="ffn_bwd", grid=(T // tm,),
        in_specs=[row, row, wide, row, _resident((1, D_MODEL)), _resident(w1.shape), _resident(w2.shape)],
        out_specs=[wide, row, row, vec],
        out_shape=[S((T, D_FF), bf16), S((T, D_MODEL), f32), S((T, D_MODEL), bf16), S((1, D_MODEL), f32)],
        args=(dy, dyb, a, x1, g, w1, w2), ride=ride)


def _wgrad(name, a, b, tk, tn, tt, relu2=False, col_shard=False, out_dtype=bf16):
    T, Ka = a.shape
    Nb = b.shape[1]
    nt = T // tt

    def body(a_ref, b_ref, o_ref, acc):
        t = pl.program_id(2)
        av = a_ref[...]
        if relu2:
            av = jnp.square(jnp.maximum(av.astype(f32), 0.0))
        prod = _tn(av.astype(bf16), b_ref[...].astype(bf16))

        @pl.when(t == 0)
        def _():
            acc[...] = prod

        @pl.when(t > 0)
        def _():
            acc[...] += prod

        @pl.when(t == nt - 1)
        def _():
            o_ref[...] = acc[...].astype(out_dtype)

    if col_shard:
        out_shape = S((Nb // tn, Ka, tn), out_dtype)
        out_spec = pl.BlockSpec((None, tk, tn), lambda i, j, t: (j, i, 0))
    else:
        out_shape = S((Ka, Nb), out_dtype)
        out_spec = pl.BlockSpec((tk, tn), lambda i, j, t: (i, j))
    return pl.pallas_call(
        body, name=name, grid=(Ka // tk, Nb // tn, nt),
        in_specs=[pl.BlockSpec((tt, tk), lambda i, j, t: (t, i)), pl.BlockSpec((tt, tn), lambda i, j, t: (t, j))],
        out_specs=out_spec, out_shape=out_shape, scratch_shapes=[pltpu.VMEM((tk, tn), f32)],
        compiler_params=_params("parallel", "parallel", "arbitrary"),
    )(a, b)


def _mix_bwd(dx1, proj, attn, conv, h1, ln_g, ln_b, w_ao, w_co, w_o, tm, ride=None):
    T = dx1.shape[0]
    g0 = GLU_END // COL

    def gate_spec(off):
        return pl.BlockSpec((tm, COL), lambda i: (i, g0 + off))

    def body(dx_ref, ga0, ga1, gc0, gc1, at_ref, cv_ref, h_ref, g_ref, b_ref, wa_ref, wc_ref, wo_ref,
             da_ref, dc_ref, do_ref, dh1_ref, dg_ref, dlg_ref, dlb_ref, dbd_ref):
        @pl.when(pl.program_id(0) == 0)
        def _():
            dlg_ref[...] = jnp.zeros_like(dlg_ref)
            dlb_ref[...] = jnp.zeros_like(dlb_ref)
            dbd_ref[...] = jnp.zeros_like(dbd_ref)

        dm = _nt(dx_ref[...].astype(bf16), wo_ref[...])
        sa = _sigmoid(jnp.concatenate([ga0[...], ga1[...]], axis=-1).astype(f32))
        sc = _sigmoid(jnp.concatenate([gc0[...], gc1[...]], axis=-1).astype(f32))
        dattn = (dm * sa).astype(bf16)
        dconv = (dm * sc).astype(bf16)
        da_ref[...] = dattn
        dc_ref[...] = dconv
        dg_ref[:, 0:D_MODEL] = (dm * at_ref[...].astype(f32) * sa * (1.0 - sa)).astype(bf16)
        dg_ref[:, D_MODEL:2 * D_MODEL] = (dm * cv_ref[...].astype(f32) * sc * (1.0 - sc)).astype(bf16)
        do_ref[...] = _nt(dattn, wa_ref[...]).astype(bf16)
        dh3 = _nt(dconv, wc_ref[...])
        xhat, rstd = _layer_norm_stats(h_ref[...])
        h2 = xhat * g_ref[...] + b_ref[...]
        sg = _sigmoid(h2)
        dh2 = dh3 * (sg * (1.0 + h2 * (1.0 - sg)))
        dlg_ref[...] += jnp.sum(dh2 * xhat, axis=0, keepdims=True)
        dlb_ref[...] += jnp.sum(dh2, axis=0, keepdims=True)
        dxh = dh2 * g_ref[...]
        dh1 = rstd * (dxh - jnp.mean(dxh, axis=-1, keepdims=True) - xhat * jnp.mean(dxh * xhat, axis=-1, keepdims=True))
        dh1_ref[...] = dh1
        dbd_ref[...] += jnp.sum(dh1, axis=0, keepdims=True)

    row = pl.BlockSpec((tm, D_MODEL), lambda i: (i, 0))
    vec = pl.BlockSpec((1, D_MODEL), lambda i: (0, 0))
    par = _resident((1, D_MODEL))
    wsp = _resident((D_MODEL, D_MODEL))
    return _call(
        body, name="mix_bwd", grid=(T // tm,),
        in_specs=[row, gate_spec(0), gate_spec(1), gate_spec(2), gate_spec(3), row, row, row, par, par, wsp, wsp, wsp],
        out_specs=[row, row, row, row, pl.BlockSpec((tm, 2 * D_MODEL), lambda i: (i, 0)), vec, vec, vec],
        out_shape=[S((T, D_MODEL), bf16), S((T, D_MODEL), bf16), S((T, D_MODEL), bf16), S((T, D_MODEL), f32),
                   S((T, 2 * D_MODEL), bf16), S((1, D_MODEL), f32), S((1, D_MODEL), f32), S((1, D_MODEL), f32)],
        args=(dx1, proj, proj, proj, proj, attn, conv, h1, ln_g, ln_b, w_ao, w_co, w_o), ride=ride)


def _conv_bwd(dh1, h0, proj, w_dw, tm, ride=None):
    T = dh1.shape[0]
    per = tm // CONV_HALO
    nh = T // CONV_HALO
    nt = T // tm
    a0 = V_END // COL
    lead = CONV_HALO - (CONV_WIDTH - 1)

    def body(dc_ref, dn_ref, hc_ref, hp_ref, a0_ref, a1_ref, g0_ref, g1_ref, w_ref, dglu_ref, dw_ref, dcat, hcat, wacc, dh0):
        i = pl.program_id(0)

        @pl.when(i == 0)
        def _():
            wacc[...] = jnp.zeros_like(wacc)

        dcat[0:tm, :] = dc_ref[...]
        dcat[tm:, :] = jnp.where(i == nt - 1, 0.0, dn_ref[...])
        hcat[0:CONV_HALO, :] = jnp.where(i == 0, 0.0, hp_ref[...])
        hcat[CONV_HALO:, :] = hc_ref[...]
        span = CONV_UNIT + CONV_HALO

        def unit_rows(c, carry):
            r0 = pl.multiple_of(c * CONV_UNIT, CONV_UNIT)
            for j in range(D_MODEL // 128):
                ls = slice(j * 128, (j + 1) * 128)
                dwin = dcat[pl.ds(r0, span), ls]
                acc = jnp.zeros((CONV_UNIT, 128), f32)
                for r, adv in _advanced_windows(dwin):
                    for q, off in _tap_offsets(r, span):
                        k = CONV_WIDTH - 1 - off
                        if 0 <= k < CONV_WIDTH:
                            acc = acc + adv[8 * q:8 * q + CONV_UNIT] * w_ref[k:k + 1, ls]
                dh0[pl.ds(r0, CONV_UNIT), ls] = acc
                dcur = dwin[0:CONV_UNIT]
                for r, adv in _advanced_windows(hcat[pl.ds(r0, span), ls]):
                    for q, off in _tap_offsets(r, span):
                        k = off - lead
                        if 0 <= k < CONV_WIDTH:
                            prod = dcur * adv[8 * q:8 * q + CONV_UNIT]
                            wacc[k, :, ls] += jnp.sum(prod.reshape(CONV_UNIT // 8, 8, 128), axis=0)
            return carry

        lax.fori_loop(0, tm // CONV_UNIT, unit_rows, 0)
        dh0v = dh0[...]
        av = jnp.concatenate([a0_ref[...], a1_ref[...]], axis=-1).astype(f32)
        sg = _sigmoid(jnp.concatenate([g0_ref[...], g1_ref[...]], axis=-1).astype(f32))
        dglu_ref[:, 0:D_MODEL] = (dh0v * sg).astype(bf16)
        dglu_ref[:, D_MODEL:2 * D_MODEL] = (dh0v * av * sg * (1.0 - sg)).astype(bf16)

        @pl.when(i == nt - 1)
        def _():
            for k in range(CONV_WIDTH):
                dw_ref[k:k + 1, :] = jnp.sum(wacc[k], axis=0, keepdims=True)
            dw_ref[CONV_WIDTH:CONV_WIDTH + 1, :] = jnp.zeros((1, D_MODEL), f32)

    row = pl.BlockSpec((tm, D_MODEL), lambda i: (i, 0))

    def col_spec(off):
        return pl.BlockSpec((tm, COL), lambda i: (i, a0 + off))

    return _call(
        body, name="conv_bwd", grid=(nt,),
        in_specs=[row, pl.BlockSpec((CONV_HALO, D_MODEL), lambda i: (jnp.minimum((i + 1) * per, nh - 1), 0)),
                  row, pl.BlockSpec((CONV_HALO, D_MODEL), lambda i: (jnp.maximum(i * per - 1, 0), 0)),
                  col_spec(0), col_spec(1), col_spec(2), col_spec(3),
                  pl.BlockSpec((CONV_WIDTH, D_MODEL), lambda i: (0, 0))],
        out_specs=[pl.BlockSpec((tm, 2 * D_MODEL), lambda i: (i, 0)), pl.BlockSpec((CONV_WIDTH + 1, D_MODEL), lambda i: (0, 0))],
        out_shape=[S((T, 2 * D_MODEL), bf16), S((CONV_WIDTH + 1, D_MODEL), f32)],
        scratch_shapes=[pltpu.VMEM((tm + CONV_HALO, D_MODEL), f32), pltpu.VMEM((tm + CONV_HALO, D_MODEL), f32),
                        pltpu.VMEM((CONV_WIDTH, 8, D_MODEL), f32), pltpu.VMEM((tm, D_MODEL), f32)],
        args=(dh1, dh1, h0, h0, proj, proj, proj, proj, w_dw), ride=ride)


def _attn_bwd(qn, kn, vb, o, do, lse, bias, sinks, ride=None):
    T = qn.shape[0]
    nb = T // QBLOCK

    def body(q_ref, kc_ref, kp_ref, vc_ref, vp_ref, o_ref, do_ref, lse_ref, b_ref, s_ref,
             dq_ref, dk_ref, dv_ref, db_ref, dsk_ref, kcar, vcar, s_scr, dp_scr, p_scr, ds_scr):
        n = pl.program_id(0)

        @pl.when(n == 0)
        def _():
            db_ref[...] = jnp.zeros_like(db_ref)
            dsk_ref[...] = jnp.zeros_like(dsk_ref)
            kcar[...] = jnp.zeros_like(kcar)
            vcar[...] = jnp.zeros_like(vcar)

        @pl.when(n < nb)
        def _():
            lane = lax.broadcasted_iota(jnp.int32, (QBLOCK, 2 * HEAD_DIM), 1)
            lane_row = lax.broadcasted_iota(jnp.int32, (1, 2 * HEAD_DIM), 1)
            kx = _kv_placements(jnp.concatenate([kp_ref[...], kc_ref[...]], axis=0))
            vx = _kv_placements(jnp.concatenate([vp_ref[...], vc_ref[...]], axis=0))
            lse_tile = lse_ref[...]
            delta, lse_c = {}, {}
            for pr in range(N_Q_HEADS // 2):
                dop = do_ref[:, _pair_cols(pr)]
                dl = dop.astype(f32) * o_ref[:, _pair_cols(pr)].astype(f32)
                for side in range(2):
                    hq = 2 * pr + side
                    h = hq // GROUP
                    qm = _one_head(q_ref[:, _pair_cols(pr)], side)
                    s_scr[_head_rows(hq), :] = _nt(qm, kx[h, side]) + b_ref[_head_rows(hq), :]
                    dp_scr[_head_rows(hq), :] = _nt(_one_head(dop, side), vx[h, side])
                    delta[hq] = jnp.sum(_one_head(dl, side), axis=-1, keepdims=True)
                    lse_c[hq] = jnp.sum(jnp.where(lane == hq, lse_tile, 0.0), axis=-1, keepdims=True)
            dsk = jnp.zeros((1, 2 * HEAD_DIM), f32)
            for hq in range(N_Q_HEADS):
                p = jnp.exp(s_scr[_head_rows(hq), :] - lse_c[hq])
                ds = p * (dp_scr[_head_rows(hq), :] - delta[hq])
                db_ref[_head_rows(hq), :] += ds
                p_scr[_head_rows(hq), :] = p.astype(bf16)
                ds_scr[_head_rows(hq), :] = ds.astype(bf16)
                psink = jnp.exp(s_ref[0, hq] - lse_c[hq])
                dsk = dsk - jnp.where(lane_row == hq, jnp.sum(psink * delta[hq], axis=0, keepdims=True), 0.0)
            dsk_ref[...] += dsk
            for pr in range(N_Q_HEADS // 2):
                h = 2 * pr // GROUP
                dq_ref[:, _pair_cols(pr)] = (jnp.dot(ds_scr[_head_rows(2 * pr), :], kx[h, 0], preferred_element_type=f32)
                                             + jnp.dot(ds_scr[_head_rows(2 * pr + 1), :], kx[h, 1], preferred_element_type=f32))
            folded_k, folded_v = [], []
            for h in range(N_KV_HEADS):
                ka = jnp.zeros((2 * QBLOCK, 2 * HEAD_DIM), f32)
                va = jnp.zeros((2 * QBLOCK, 2 * HEAD_DIM), f32)
                for g in range(GROUP):
                    hq = h * GROUP + g
                    ka = ka + _tn(ds_scr[_head_rows(hq), :], _one_head(q_ref[:, _pair_cols(hq // 2)], hq % 2))
                    va = va + _tn(p_scr[_head_rows(hq), :], _one_head(do_ref[:, _pair_cols(hq // 2)], hq % 2))
                folded_k.append(ka + _swap_halves(ka))
                folded_v.append(va + _swap_halves(va))
            low = _low_lanes()
            for m in range(N_KV_HEADS // 2):
                cs = _pair_cols(m)
                for folded, out_ref, car in ((folded_k, dk_ref, kcar), (folded_v, dv_ref, vcar)):
                    band = jnp.where(low, folded[2 * m], folded[2 * m + 1])
                    out_ref[:, cs] = car[:, cs] + band[0:QBLOCK, :]
                    car[:, cs] = band[QBLOCK:, :]

        @pl.when(n == nb)
        def _():
            dk_ref[...] = kcar[...]
            dv_ref[...] = vcar[...]

    cur = lambda n: (jnp.minimum(n, nb - 1), 0)
    prev = lambda n: (jnp.clip(n - 1, 0, nb - 1), 0)
    qspec = pl.BlockSpec((QBLOCK, ATTN_WIDTH), cur)
    kcur, kprev = pl.BlockSpec((QBLOCK, KV_WIDTH), cur), pl.BlockSpec((QBLOCK, KV_WIDTH), prev)
    whole = lambda shape: pl.BlockSpec(shape, lambda n: (0,) * len(shape))
    scores = (N_Q_HEADS * QBLOCK, 2 * QBLOCK)
    return _call(
        body, name="attn_bwd", grid=(nb + 1,),
        in_specs=[qspec, kcur, kprev, kcur, kprev, qspec, qspec, pl.BlockSpec((QBLOCK, 2 * HEAD_DIM), cur), _bias_spec(), SMEM],
        out_specs=[qspec, kprev, kprev, whole(scores), whole((1, 2 * HEAD_DIM))],
        out_shape=[S((T, ATTN_WIDTH), f32), S((T, KV_WIDTH), f32), S((T, KV_WIDTH), f32), S(scores, f32),
                   S((1, 2 * HEAD_DIM), f32)],
        scratch_shapes=[pltpu.VMEM((QBLOCK, KV_WIDTH), f32), pltpu.VMEM((QBLOCK, KV_WIDTH), f32),
                        pltpu.VMEM(scores, f32), pltpu.VMEM(scores, f32), pltpu.VMEM(scores, bf16), pltpu.VMEM(scores, bf16)],
        args=(qn, kn, kn, vb, vb, o, do, lse, bias, sinks), ride=ride)


def _rel_bias_bwd(dbias, bucket):
    def body(d_ref, bk_ref, o_ref):
        b = bk_ref[...]
        for k in range(N_BUCKETS):
            mk = b == k
            for h in range(N_Q_HEADS):
                o_ref[k, h] = jnp.sum(jnp.where(mk, d_ref[h * QBLOCK:(h + 1) * QBLOCK, :], 0.0))

    return pl.pallas_call(body, name="rel_bias_bwd", out_shape=S((N_BUCKETS, N_Q_HEADS), f32), out_specs=SMEM)(dbias, bucket)


def _qk_norm_bwd(dq, dk, dv, proj, qg, kg, tm):
    T = dq.shape[0]
    scale = HEAD_DIM ** -0.5

    def pair_bwd(dy, x, gv):
        r = _pair_rstd(x)
        xn = x * r
        dxn = dy * gv
        dx = r * (dxn - xn * _pair_mean(dxn * xn))
        return dx, jnp.sum(dy * xn, axis=0, keepdims=True)

    def body(dq_ref, dk_ref, dv_ref, p_ref, qg_ref, kg_ref, out_ref, dqg_ref, dkg_ref):
        @pl.when(pl.program_id(0) == 0)
        def _():
            dqg_ref[...] = jnp.zeros_like(dqg_ref)
            dkg_ref[...] = jnp.zeros_like(dkg_ref)

        qgv, kgv = qg_ref[...], kg_ref[...]
        dqg = jnp.zeros((1, 2 * HEAD_DIM), f32)
        for pr in range(N_Q_HEADS // 2):
            dx, dg = pair_bwd(dq_ref[:, _pair_cols(pr)] * scale, p_ref[:, _pair_cols(pr)].astype(f32), qgv)
            out_ref[:, _pair_cols(pr)] = dx.astype(bf16)
            dqg = dqg + dg
        dkg = jnp.zeros((1, 2 * HEAD_DIM), f32)
        for pr in range(N_KV_HEADS // 2):
            ps = slice(Q_END + pr * 2 * HEAD_DIM, Q_END + (pr + 1) * 2 * HEAD_DIM)
            dx, dg = pair_bwd(dk_ref[:, _pair_cols(pr)], p_ref[:, ps].astype(f32), kgv)
            out_ref[:, ps] = dx.astype(bf16)
            dkg = dkg + dg
        out_ref[:, K_END:V_END] = dv_ref[...].astype(bf16)
        dqg_ref[...] += dqg
        dkg_ref[...] += dkg

    vec = pl.BlockSpec((1, 2 * HEAD_DIM), lambda i: (0, 0))
    return pl.pallas_call(
        body, name="qk_norm_bwd", grid=(T // tm,),
        in_specs=[pl.BlockSpec((tm, ATTN_WIDTH), lambda i: (i, 0)), pl.BlockSpec((tm, KV_WIDTH), lambda i: (i, 0)),
                  pl.BlockSpec((tm, KV_WIDTH), lambda i: (i, 0)), pl.BlockSpec((tm, V_END), lambda i: (i, 0)), vec, vec],
        out_specs=[pl.BlockSpec((tm, V_END), lambda i: (i, 0)), vec, vec],
        out_shape=[S((T, V_END), bf16), S((1, 2 * HEAD_DIM), f32), S((1, 2 * HEAD_DIM), f32)],
        compiler_params=_params("arbitrary"),
    )(dq, dk, dv, proj, qg, kg)


def _in_bwd(dqkv, dglu, dgates, w_in, x, g, dx1, tm, ride=None):
    T = x.shape[0]
    pieces = (dqkv, dglu, dgates)
    starts = [0, dqkv.shape[1], dqkv.shape[1] + dglu.shape[1]]

    def body(a0_ref, a1_ref, a2_ref, w_ref, x_ref, g_ref, d_ref, gx_ref, dg_ref):
        @pl.when(pl.program_id(0) == 0)
        def _():
            dg_ref[...] = jnp.zeros_like(dg_ref)

        du = jnp.zeros((tm, D_MODEL), f32)
        for a_ref, c0 in zip((a0_ref, a1_ref, a2_ref), starts):
            du = du + _nt(a_ref[...], w_ref[:, c0:c0 + a_ref.shape[1]])
        dx, dg = _rms_bwd(du, x_ref[...], g_ref[...])
        gx_ref[...] = d_ref[...] + dx
        dg_ref[...] += dg

    row = pl.BlockSpec((tm, D_MODEL), lambda i: (i, 0))
    return _call(
        body, name="in_bwd", grid=(T // tm,),
        in_specs=[pl.BlockSpec((tm, p.shape[1]), lambda i: (i, 0)) for p in pieces]
        + [_resident(w_in.shape), row, _resident((1, D_MODEL)), row],
        out_specs=[row, pl.BlockSpec((1, D_MODEL), lambda i: (0, 0))],
        out_shape=[S((T, D_MODEL), f32), S((1, D_MODEL), f32)],
        args=(dqkv, dglu, dgates, w_in, x, g, dx1), ride=ride)


def _adamw(name, parts, w, m, v, tr):
    R, C = w.shape
    bc1 = 1.0 - ADAM_B1 ** ADAM_STEP
    bc2 = 1.0 - ADAM_B2 ** ADAM_STEP

    def body(p_ref, w_ref, m_ref, v_ref, g_ref, d_ref, nm_ref, nv_ref):
        g = p_ref[0].astype(f32)
        for k in range(1, N_DEV):
            g = g + p_ref[k].astype(f32)
        nm = ADAM_B1 * m_ref[...] + (1.0 - ADAM_B1) * g
        nv = ADAM_B2 * v_ref[...] + (1.0 - ADAM_B2) * (g * g)
        g_ref[...] = g
        nm_ref[...] = nm
        nv_ref[...] = nv
        d_ref[...] = -ADAM_LR * ((nm / bc1) / (jnp.sqrt(nv / bc2) + ADAM_EPS) + ADAM_WD * w_ref[...])

    blk = pl.BlockSpec((tr, C), lambda i: (i, 0))
    return pl.pallas_call(
        body, name=name, grid=(R // tr,),
        in_specs=[pl.BlockSpec((N_DEV, tr, C), lambda i: (0, i, 0)), blk, blk, blk],
        out_specs=[blk, blk, blk, blk], out_shape=[S((R, C), f32)] * 4,
        compiler_params=_params("parallel"),
    )(parts, w, m, v)


def _tile(T, pref):
    return min(T, pref)


def _pad_rows(a, rows):
    return jnp.pad(a, ((0, rows - a.shape[0]), (0, 0)))


def kernel(x, norm_mix_g, w_in, q_norm_g, k_norm_g, attn_sinks, rel_bias, w_attn_o, w_dw, b_dw, conv_ln_g, conv_ln_b, w_conv_out, w_out, norm_mlp_g, w_ff1, w_ff2, loss_target, m_norm_mix_g, m_w_in, m_q_norm_g, m_k_norm_g, m_attn_sinks, m_rel_bias, m_w_attn_o, m_w_dw, m_b_dw, m_conv_ln_g, m_conv_ln_b, m_w_conv_out, m_w_out, m_norm_mlp_g, m_w_ff1, m_w_ff2, v_norm_mix_g, v_w_in, v_q_norm_g, v_k_norm_g, v_attn_sinks, v_rel_bias, v_w_attn_o, v_w_dw, v_b_dw, v_conv_ln_g, v_conv_ln_b, v_w_conv_out, v_w_out, v_norm_mlp_g, v_w_ff1, v_w_ff2):
    T = x.shape[1]
    xs = x[0]
    tgt = loss_target[0]
    in_shard = IN_WIDTH // N_DEV
    dw_rows = CONV_WIDTH + 1
    ch_shard = D_MODEL // N_DEV
    tc = _tile(T, 256)
    tb = _tile(T, 512)
    tt = _tile(T, 2048)
    bucket = jnp.asarray(_t5_bucket_table())

    g_in, g_dw = _exchange("gather_w_in", [w_in[0].astype(bf16), _pad_rows(w_dw[0], dw_rows)], gather=True, two_level=True)
    W_in = jnp.transpose(g_in, (1, 0, 2)).reshape(D_MODEL, IN_WIDTH)
    W_dw = jnp.transpose(g_dw, (1, 0, 2)).reshape(dw_rows, D_MODEL)[:CONV_WIDTH]

    mix_shards = _Gather([w_attn_o[0].astype(bf16), w_conv_out[0].astype(bf16), w_out[0].astype(bf16)])
    qg2, kg2 = jnp.tile(q_norm_g, (1, 2)), jnp.tile(k_norm_g, (1, 2))
    (proj, u, qn, kn, vb, h0), (g_ao, g_co, g_o) = _proj_fwd(xs, norm_mix_g, W_in, qg2, kg2, tb, ride=mix_shards)
    W_ao = g_ao.reshape(D_MODEL, D_MODEL)
    W_co = g_co.reshape(D_MODEL, D_MODEL)
    W_o = g_o.reshape(D_MODEL, D_MODEL)
    bias = _bias_table(rel_bias, bucket)
    (o, lse), (g_f1,) = _attn_fwd(qn, kn, vb, bias, attn_sinks, ride=_Gather([w_ff1[0].astype(bf16)]))
    (h1, h3), (g_f2,) = _conv_fwd(h0, W_dw, b_dw, conv_ln_g, conv_ln_b, tc, ride=_Gather([w_ff2[0].astype(bf16)]))
    x1, attn, conv, merged = _mix_fwd(xs, o, h3, proj, W_ao, W_co, W_o, tb)
    W_f2 = g_f2.reshape(D_FF, D_MODEL)
    a, u2, dy, dyb, loss_parts = _ffn_fwd(x1, norm_mlp_g, g_f1, W_f2, tgt, tb)
    loss = lax.psum(jnp.sum(loss_parts[:, 0, 0]), ("x", "y", "c"))

    gw_f2 = _wgrad("wgrad_ff2", a, dyb, D_MODEL, D_MODEL, tt, relu2=True).reshape(N_DEV, FF_CHUNK, D_MODEL)
    (da, dx1, dx1b, d_norm_mlp_g), (l_f2,) = _ffn_bwd(dy, dyb, a, x1, norm_mlp_g, g_f1, W_f2, tc,
                                                      ride=_Exchange([gw_f2], gather=False))
    gw_f1 = _wgrad("wgrad_ff1", u2, da, D_MODEL, FF_CHUNK, tt, col_shard=True)
    gw_o = _wgrad("wgrad_out", merged, dx1b, D_MODEL, D_MODEL, tt).reshape(N_DEV, ch_shard, D_MODEL)
    (dattn, dconv, do, dh1, dgates, d_ln_g, d_ln_b, d_b_dw), (l_o,) = _mix_bwd(
        dx1b, proj, attn, conv, h1, conv_ln_g, conv_ln_b, W_ao, W_co, W_o, tb, ride=_Exchange([gw_o], gather=False))
    gw_ao = _wgrad("wgrad_attn_o", o, dattn, D_MODEL, D_MODEL, tt).reshape(N_DEV, ch_shard, D_MODEL)
    gw_co = _wgrad("wgrad_conv_out", h3, dconv, D_MODEL, D_MODEL, tt).reshape(N_DEV, ch_shard, D_MODEL)
    (dglu, d_w_dw), (l_f1, l_ao, l_co) = _conv_bwd(dh1, h0, proj, W_dw, tc,
                                                   ride=_Exchange([gw_f1, gw_ao, gw_co], gather=False))
    (dq, dk, dv, dbias, d_sinks), _ = _attn_bwd(qn, kn, vb, o, do, lse, bias, attn_sinks)
    d_sinks = d_sinks[:, :N_Q_HEADS]
    d_rel_bias = _rel_bias_bwd(dbias, bucket)
    dqkv, d_qg, d_kg = _qk_norm_bwd(dq, dk, dv, proj, qg2, kg2, tb)
    d_qg = d_qg[:, :HEAD_DIM] + d_qg[:, HEAD_DIM:]
    d_kg = d_kg[:, :HEAD_DIM] + d_kg[:, HEAD_DIM:]
    gw_in = jnp.concatenate([_wgrad("wgrad_in_qkv", u, dqkv, D_MODEL, COL, tt),
                             _wgrad("wgrad_in_glu", u, dglu, D_MODEL, D_MODEL, tt),
                             _wgrad("wgrad_in_gates", u, dgates, D_MODEL, D_MODEL, tt)], axis=1)
    gw_in = jnp.transpose(gw_in.reshape(D_MODEL, N_DEV, in_shard), (1, 0, 2))
    gw_dw = jnp.transpose(d_w_dw.reshape(dw_rows, N_DEV, ch_shard), (1, 0, 2))
    (grad_x, d_norm_mix_g), (l_in, l_dw) = _in_bwd(dqkv, dglu, dgates, W_in, xs, norm_mix_g, dx1, tb,
                                                    ride=_Exchange([gw_in, gw_dw], gather=False))

    def row(vec):
        flat = vec.reshape(1, -1)
        return jnp.pad(flat, ((0, 0), (0, D_MODEL - flat.shape[1])))

    def pack_small(nm, qg, kg, sk, rb, bd, lg, lb, nl):
        tail = jnp.concatenate([qg.reshape(1, -1), kg.reshape(1, -1), sk.reshape(1, -1), rb.reshape(1, -1)], axis=1)
        return jnp.concatenate([row(nm), row(bd), row(lg), row(lb), row(nl), row(tail), jnp.zeros((2, D_MODEL), f32)], axis=0)

    def unpack_small(p):
        t = p[5]
        o0, o1, o2 = HEAD_DIM, 2 * HEAD_DIM, 2 * HEAD_DIM + N_Q_HEADS
        return dict(norm_mix_g=p[0:1], b_dw=p[1:2], conv_ln_g=p[2:3], conv_ln_b=p[3:4], norm_mlp_g=p[4:5],
                    q_norm_g=t[0:o0].reshape(1, HEAD_DIM), k_norm_g=t[o0:o1].reshape(1, HEAD_DIM),
                    attn_sinks=t[o1:o2].reshape(1, N_Q_HEADS),
                    rel_bias=t[o2:o2 + N_BUCKETS * N_Q_HEADS].reshape(N_BUCKETS, N_Q_HEADS))

    small_g = pack_small(d_norm_mix_g, d_qg, d_kg, d_sinks, d_rel_bias, d_b_dw, d_ln_g, d_ln_b, d_norm_mlp_g)
    (l_small,) = _exchange("gather_small_grads", [small_g], gather=True)


    res = {}
    res["w_in"] = _adamw("adamw_in", l_in, w_in[0], m_w_in[0], v_w_in[0], 256)
    res["w_attn_o"] = _adamw("adamw_attn_o", l_ao, w_attn_o[0], m_w_attn_o[0], v_w_attn_o[0], ch_shard)
    res["w_conv_out"] = _adamw("adamw_conv_out", l_co, w_conv_out[0], m_w_conv_out[0], v_w_conv_out[0], ch_shard)
    res["w_out"] = _adamw("adamw_out", l_o, w_out[0], m_w_out[0], v_w_out[0], ch_shard)
    res["w_ff1"] = _adamw("adamw_ff1", l_f1, w_ff1[0], m_w_ff1[0], v_w_ff1[0], 256)
    res["w_ff2"] = _adamw("adamw_ff2", l_f2, w_ff2[0], m_w_ff2[0], v_w_ff2[0], 256)
    dw4 = _adamw("adamw_dw", l_dw, _pad_rows(w_dw[0], dw_rows), _pad_rows(m_w_dw[0], dw_rows), _pad_rows(v_w_dw[0], dw_rows), dw_rows)
    res["w_dw"] = [t[:CONV_WIDTH] for t in dw4]
    small_w = pack_small(norm_mix_g, q_norm_g, k_norm_g, attn_sinks, rel_bias, b_dw, conv_ln_g, conv_ln_b, norm_mlp_g)
    small_m = pack_small(m_norm_mix_g, m_q_norm_g, m_k_norm_g, m_attn_sinks, m_rel_bias, m_b_dw, m_conv_ln_g, m_conv_ln_b, m_norm_mlp_g)
    small_v = pack_small(v_norm_mix_g, v_q_norm_g, v_k_norm_g, v_attn_sinks, v_rel_bias, v_b_dw, v_conv_ln_g, v_conv_ln_b, v_norm_mlp_g)
    small4 = [unpack_small(t) for t in _adamw("adamw_small", l_small, small_w, small_m, small_v, 8)]

    order = ["norm_mix_g", "w_in", "q_norm_g", "k_norm_g", "attn_sinks", "rel_bias", "w_attn_o", "w_dw", "b_dw",
             "conv_ln_g", "conv_ln_b", "w_conv_out", "w_out", "norm_mlp_g", "w_ff1", "w_ff2"]
    stacked = {"w_in", "w_attn_o", "w_dw", "w_conv_out", "w_out", "w_ff1", "w_ff2"}
    outs = [loss, grad_x[None]]
    for k in range(4):
        for nme in order:
            if nme in stacked:
                outs.append(res[nme][k][None])
            else:
                outs.append(small4[k][nme])
    return tuple(outs)
```

```python
import functools

import numpy as np
import jax
import jax.numpy as jnp
from jax import lax
from jax.experimental import pallas as pl
from jax.experimental.pallas import tpu as pltpu

f32 = jnp.float32
bf16 = jnp.bfloat16
S = jax.ShapeDtypeStruct

N_DEV = 8
D_MODEL = 1024
HEAD_DIM = 64
N_Q_HEADS = 16
N_KV_HEADS = 4
GROUP = N_Q_HEADS // N_KV_HEADS
ATTN_WIDTH = N_Q_HEADS * HEAD_DIM
KV_WIDTH = N_KV_HEADS * HEAD_DIM
QBLOCK = 128
CONV_WIDTH = 31
CONV_HALO = 32
CONV_UNIT = 64
D_FF = 4 * D_MODEL
N_BUCKETS = 32
MAX_DISTANCE = 128
EPS = 1e-6
NEG = -1e30
Q_END = ATTN_WIDTH
K_END = Q_END + KV_WIDTH
V_END = K_END + KV_WIDTH
GLU_END = V_END + 2 * D_MODEL
IN_WIDTH = GLU_END + 2 * D_MODEL
COL = 512
FF_CHUNK = D_FF // N_DEV

ADAM_LR = 0.001
ADAM_B1 = 0.9
ADAM_B2 = 0.999
ADAM_EPS = 1e-08
ADAM_WD = 0.01
ADAM_STEP = 10

VMEM_LIMIT = 56 * 1024 * 1024

MESH_ID = pl.DeviceIdType.MESH
ANY = pl.BlockSpec(memory_space=pl.ANY)
SMEM = pl.BlockSpec(memory_space=pltpu.SMEM)


def _params(*sem):
    return pltpu.CompilerParams(dimension_semantics=sem, vmem_limit_bytes=VMEM_LIMIT)


def _nt(a, b):
    return lax.dot_general(a, b, (((1,), (1,)), ((), ())), preferred_element_type=f32)


def _tn(a, b):
    return lax.dot_general(a, b, (((0,), (0,)), ((), ())), preferred_element_type=f32)


def _sigmoid(z):
    return 1.0 / (1.0 + jnp.exp(-z))


def _t5_bucket_table():
    qi = np.arange(QBLOCK, dtype=np.int32)[:, None]
    kj = np.arange(2 * QBLOCK, dtype=np.int32)[None, :]
    dist = qi + QBLOCK - kj
    n = np.maximum(dist, 0)
    max_exact = N_BUCKETS // 2
    nf = np.maximum(n, 1).astype(np.float32)
    large = max_exact + (np.log(nf / np.float32(max_exact)) / np.float32(np.log(MAX_DISTANCE / max_exact))
                         * np.float32(N_BUCKETS - max_exact)).astype(np.int32)
    large = np.minimum(large, N_BUCKETS - 1)
    bucket = np.where(n < max_exact, n, large)
    valid = (dist >= 0) & (dist < QBLOCK)
    return np.where(valid, bucket, -1).astype(np.int32)


def _peer(d):
    x, y, c = lax.axis_index("x"), lax.axis_index("y"), lax.axis_index("c")
    dx, dy, dc = (d >> 2) & 1, (d >> 1) & 1, d & 1
    px, py, pc = x ^ dx, y ^ dy, c ^ dc
    return (px, py, pc), 4 * px + 2 * py + pc


class _Exchange:
    def __init__(self, arrays, gather):
        self.arrays, self.gather, self.n = list(arrays), gather, len(arrays)
        self.out_shape = [S(((N_DEV,) + a.shape) if gather else a.shape, a.dtype) for a in self.arrays]
        self.scratch = [pltpu.SemaphoreType.DMA((self.n, N_DEV - 1)), pltpu.SemaphoreType.DMA((self.n, N_DEV - 1)),
                        pltpu.SemaphoreType.DMA((self.n,))]

    def _copies(self, ins, outs, sems):
        send_sems, recv_sems, local_sems = sems
        _, me = _peer(0)
        local, sends, recvs = [], [], []
        for k in range(self.n):
            src = ins[k] if self.gather else ins[k].at[me]
            local.append(pltpu.make_async_copy(src, outs[k].at[me], local_sems.at[k]))
        for d in range(1, N_DEV):
            peer, pidx = _peer(d)
            for k in range(self.n):
                src = ins[k] if self.gather else ins[k].at[pidx]
                common = dict(src_ref=src, send_sem=send_sems.at[k, d - 1], recv_sem=recv_sems.at[k, d - 1],
                              device_id=peer, device_id_type=MESH_ID)
                sends.append(pltpu.make_async_remote_copy(dst_ref=outs[k].at[me], **common))
                recvs.append(pltpu.make_async_remote_copy(dst_ref=outs[k].at[pidx], **common))
        return local, sends, recvs

    def start(self, ins, outs, sems):
        local, sends, _ = self._copies(ins, outs, sems)
        for cp in local + sends:
            cp.start()

    def wait(self, ins, outs, sems):
        local, sends, recvs = self._copies(ins, outs, sems)
        for cp in recvs:
            cp.wait_recv()
        for cp in sends:
            cp.wait_send()
        for cp in local:
            cp.wait()


class _Gather:
    CHIPS = (4, 2, 6)
    SLOTS = 1 + 2 * len(CHIPS)

    def __init__(self, arrays):
        self.arrays, self.n = list(arrays), len(arrays)
        self.out_shape = [S((N_DEV,) + a.shape, a.dtype) for a in self.arrays]
        self.scratch = [pltpu.SemaphoreType.DMA((self.n, self.SLOTS)), pltpu.SemaphoreType.DMA((self.n, self.SLOTS)),
                        pltpu.SemaphoreType.DMA((self.n,))]

    @staticmethod
    def _copy(outs, sems, k, slot, src, block, to):
        return pltpu.make_async_remote_copy(src_ref=src, dst_ref=outs[k].at[block], send_sem=sems[0].at[k, slot],
                                            recv_sem=sems[1].at[k, slot], device_id=to, device_id_type=MESH_ID)

    def _local(self, ins, outs, sems):
        _, me = _peer(0)
        return [pltpu.make_async_copy(ins[k], outs[k].at[me], sems[2].at[k]) for k in range(self.n)]

    def start(self, ins, outs, sems):
        _, me = _peer(0)
        sibling, _ = _peer(1)
        for cp in self._local(ins, outs, sems):
            cp.start()
        for k in range(self.n):
            self._copy(outs, sems, k, 0, ins[k], me, sibling).start()
            for j, d in enumerate(self.CHIPS):
                self._copy(outs, sems, k, 1 + j, ins[k], me, _peer(d)[0]).start()

    def mid(self, ins, outs, sems):
        sibling, _ = _peer(1)
        for j, d in enumerate(self.CHIPS):
            chip, block = _peer(d)
            for k in range(self.n):
                self._copy(outs, sems, k, 1 + j, ins[k], block, chip).wait_recv()
                self._copy(outs, sems, k, 4 + j, outs[k].at[block], block, sibling).start()

    def wait(self, ins, outs, sems):
        _, me = _peer(0)
        sibling, sib_block = _peer(1)
        for k in range(self.n):
            self._copy(outs, sems, k, 0, ins[k], sib_block, sibling).wait_recv()
            for j, d in enumerate(self.CHIPS):
                self._copy(outs, sems, k, 4 + j, ins[k], _peer(d ^ 1)[1], sibling).wait_recv()
        for k in range(self.n):
            self._copy(outs, sems, k, 0, ins[k], me, sibling).wait_send()
            for j, d in enumerate(self.CHIPS):
                chip, block = _peer(d)
                self._copy(outs, sems, k, 1 + j, ins[k], me, chip).wait_send()
                self._copy(outs, sems, k, 4 + j, outs[k].at[block], block, sibling).wait_send()
        for cp in self._local(ins, outs, sems):
            cp.wait()


def _exchange(name, arrays, gather, two_level=False):
    ex = _Gather(arrays) if two_level else _Exchange(arrays, gather)
    n = ex.n

    def body(*refs):
        ins, outs, sems = refs[:n], refs[n:2 * n], refs[2 * n:]
        ex.start(ins, outs, sems)
        if two_level:
            ex.mid(ins, outs, sems)
        ex.wait(ins, outs, sems)

    return pl.pallas_call(body, name=name, out_shape=ex.out_shape, in_specs=[ANY] * n, out_specs=[ANY] * n,
                          scratch_shapes=ex.scratch)(*arrays)


def _call(body, *, name, grid, in_specs, out_specs, out_shape, args, scratch_shapes=(), ride=None):
    n_in, n_out, n_sc = len(in_specs), len(out_specs), len(scratch_shapes)
    sem = ("arbitrary",) * len(grid)
    if ride is None:
        res = pl.pallas_call(body, name=name, grid=grid, in_specs=list(in_specs), out_specs=list(out_specs),
                             out_shape=list(out_shape), scratch_shapes=list(scratch_shapes), compiler_params=_params(*sem))(*args)
        return list(res), []
    nx = ride.n

    def riding(*refs):
        ins, xin = refs[:n_in], refs[n_in:n_in + nx]
        outs, xout = refs[n_in + nx:n_in + nx + n_out], refs[n_in + nx + n_out:n_in + 2 * nx + n_out]
        rest = refs[n_in + 2 * nx + n_out:]
        scratch, sems = rest[:n_sc], rest[n_sc:]
        ids = [pl.program_id(ax) for ax in range(len(grid))]
        first = functools.reduce(jnp.logical_and, [i == 0 for i in ids])
        last = functools.reduce(jnp.logical_and, [i == g - 1 for i, g in zip(ids, grid)])

        @pl.when(first)
        def _():
            ride.start(xin, xout, sems)

        if hasattr(ride, "mid"):
            halfway = functools.reduce(jnp.logical_and, [ids[0] == grid[0] // 2] + [i == 0 for i in ids[1:]])

            @pl.when(halfway)
            def _():
                ride.mid(xin, xout, sems)

        body(*ins, *outs, *scratch)

        @pl.when(last)
        def _():
            ride.wait(xin, xout, sems)

    res = pl.pallas_call(
        riding, name=name, grid=grid, in_specs=list(in_specs) + [ANY] * nx, out_specs=list(out_specs) + [ANY] * nx,
        out_shape=list(out_shape) + ride.out_shape, scratch_shapes=list(scratch_shapes) + ride.scratch,
        compiler_params=_params(*sem))(*args, *ride.arrays)
    return list(res[:n_out]), list(res[n_out:])


def _resident(shape):
    return pl.BlockSpec(shape, lambda *_: (0,) * len(shape), pipeline_mode=pl.Buffered(1))


def _proj_fwd(x, g, w, qg, kg, tm, ride=None):
    T, K = x.shape
    N = w.shape[1]
    per = COL // (2 * HEAD_DIM)
    assert Q_END % COL == 0 and V_END == Q_END + COL and (GLU_END - V_END) == 4 * COL and KV_WIDTH == COL // 2

    def body(x_ref, g_ref, w_ref, qg_ref, kg_ref, o_ref, u_ref, qn_ref, kn_ref, vb_ref, h0_ref):
        xv = x_ref[...]
        r = lax.rsqrt(jnp.mean(xv * xv, axis=-1, keepdims=True) + EPS)
        u = (xv * r * g_ref[...]).astype(bf16)
        u_ref[...] = u

        def block(c):
            cs = slice(c * COL, (c + 1) * COL)
            pc = jnp.dot(u, w_ref[:, cs], preferred_element_type=f32)
            o_ref[:, cs] = pc.astype(bf16)
            return pc

        qgv = qg_ref[...] * (HEAD_DIM ** -0.5)
        for c in range(Q_END // COL):
            pc = block(c)
            for t in range(per):
                xq = pc[:, _pair_cols(t)]
                qn_ref[:, _pair_cols(c * per + t)] = (xq * _pair_rstd(xq, False) * qgv).astype(bf16)
        pc = block(Q_END // COL)
        for t in range(KV_WIDTH // (2 * HEAD_DIM)):
            xk = pc[:, _pair_cols(t)]
            kn_ref[:, _pair_cols(t)] = (xk * _pair_rstd(xk, False) * kg_ref[...]).astype(bf16)
        vb_ref[...] = pc[:, KV_WIDTH:].astype(bf16)
        a0 = V_END // COL
        for half in range(2):
            gate = block(a0 + 2 + half)
            h0_ref[:, half * COL:(half + 1) * COL] = block(a0 + half) * _sigmoid(gate)
        for c in range(GLU_END // COL, N // COL):
            block(c)

    row = lambda width: pl.BlockSpec((tm, width), lambda i: (i, 0))
    return _call(
        body, name="proj_fwd", grid=(T // tm,),
        in_specs=[row(K), _resident((1, K)), _resident((K, N)), _resident((1, 2 * HEAD_DIM)), _resident((1, 2 * HEAD_DIM))],
        out_specs=[row(N), row(K), row(ATTN_WIDTH), row(KV_WIDTH), row(KV_WIDTH), row(D_MODEL)],
        out_shape=[S((T, N), bf16), S((T, K), bf16), S((T, ATTN_WIDTH), bf16), S((T, KV_WIDTH), bf16), S((T, KV_WIDTH), bf16),
                   S((T, D_MODEL), f32)],
        args=(x, g, w, qg, kg), ride=ride)


def _bias_table(rel_bias, bucket):
    def body(rb_ref, bk_ref, o_ref):
        b = bk_ref[...]
        absent = lax.broadcasted_iota(jnp.int32, (QBLOCK, 2 * QBLOCK), 1) < QBLOCK
        for h in range(N_Q_HEADS):
            acc = jnp.full((QBLOCK, 2 * QBLOCK), NEG, f32)
            for k in range(N_BUCKETS):
                acc = jnp.where(b == k, rb_ref[k, h], acc)
            o_ref[0, h * QBLOCK:(h + 1) * QBLOCK, :] = acc
            o_ref[1, h * QBLOCK:(h + 1) * QBLOCK, :] = jnp.where(absent, NEG, acc)

    return pl.pallas_call(
        body, name="bias_table", out_shape=S((2, N_Q_HEADS * QBLOCK, 2 * QBLOCK), f32),
        in_specs=[SMEM, pl.BlockSpec(memory_space=pltpu.VMEM)],
    )(rel_bias, bucket)


def _bias_spec():
    return pl.BlockSpec((None, N_Q_HEADS * QBLOCK, 2 * QBLOCK), lambda n: (jnp.where(n == 0, 1, 0), 0, 0))


def _swap_halves(t):
    return jnp.concatenate([t[:, HEAD_DIM:], t[:, :HEAD_DIM]], axis=1)


def _low_lanes():
    return lax.broadcasted_iota(jnp.int32, (1, 2 * HEAD_DIM), 1) < HEAD_DIM


def _one_head(pair, side):
    zero = jnp.zeros((), pair.dtype)
    return jnp.where(_low_lanes(), pair, zero) if side == 0 else jnp.where(_low_lanes(), zero, pair)


def _pair_mean(t, on_mxu):
    if not on_mxu:
        m_lo = jnp.sum(_one_head(t, 0), axis=-1, keepdims=True) * (1.0 / HEAD_DIM)
        m_hi = jnp.sum(_one_head(t, 1), axis=-1, keepdims=True) * (1.0 / HEAD_DIM)
        return jnp.where(_low_lanes(), m_lo, m_hi)
    width = 2 * HEAD_DIM
    same_head = ((lax.broadcasted_iota(jnp.int32, (width, width), 0) < HEAD_DIM)
                 == (lax.broadcasted_iota(jnp.int32, (width, width), 1) < HEAD_DIM))
    e = jnp.where(same_head, 1.0 / HEAD_DIM, 0.0).astype(bf16)
    hi = t.astype(bf16)
    lo = (t - hi.astype(f32)).astype(bf16)
    return jnp.dot(hi, e, preferred_element_type=f32) + jnp.dot(lo, e, preferred_element_type=f32)


def _pair_rstd(x, on_mxu):
    return lax.rsqrt(_pair_mean(x * x, on_mxu) + EPS)


def _kv_placements(band):
    out = {}
    for m in range(N_KV_HEADS // 2):
        pair = band[:, m * 2 * HEAD_DIM:(m + 1) * 2 * HEAD_DIM]
        swapped = _swap_halves(pair)
        for hh in range(2):
            out[2 * m + hh, 0] = _one_head(pair if hh == 0 else swapped, 0)
            out[2 * m + hh, 1] = _one_head(swapped if hh == 0 else pair, 1)
    return out


def _head_rows(hq):
    return slice(hq * QBLOCK, (hq + 1) * QBLOCK)


def _pair_cols(pr):
    return slice(pr * 2 * HEAD_DIM, (pr + 1) * 2 * HEAD_DIM)


def _attn_fwd(qn, kn, vb, bias, sinks, ride=None):
    T = qn.shape[0]
    nb = T // QBLOCK

    def body(q_ref, kc_ref, kp_ref, vc_ref, vp_ref, b_ref, s_ref, o_ref, lse_ref, s_scr, p_scr):
        lane = lax.broadcasted_iota(jnp.int32, (QBLOCK, 2 * HEAD_DIM), 1)
        kx = _kv_placements(jnp.concatenate([kp_ref[...], kc_ref[...]], axis=0))
        vx = _kv_placements(jnp.concatenate([vp_ref[...], vc_ref[...]], axis=0))
        for hq in range(N_Q_HEADS):
            qm = _one_head(q_ref[:, _pair_cols(hq // 2)], hq % 2)
            s_scr[_head_rows(hq), :] = _nt(qm, kx[hq // GROUP, hq % 2]) + b_ref[_head_rows(hq), :]
        lse_tile = jnp.zeros((QBLOCK, 2 * HEAD_DIM), f32)
        for hq in range(N_Q_HEADS):
            s = s_scr[_head_rows(hq), :]
            sink = s_ref[0, hq]
            m = jnp.maximum(jnp.max(s, axis=-1, keepdims=True), sink)
            p = jnp.exp(s - m)
            l = jnp.sum(p, axis=-1, keepdims=True) + jnp.exp(sink - m)
            p_scr[_head_rows(hq), :] = (p * (1.0 / l)).astype(bf16)
            lse_tile = jnp.where(lane == hq, m + jnp.log(l), lse_tile)
        lse_ref[...] = lse_tile
        for pr in range(N_Q_HEADS // 2):
            h = 2 * pr // GROUP
            o_pair = (jnp.dot(p_scr[_head_rows(2 * pr), :], vx[h, 0], preferred_element_type=f32)
                      + jnp.dot(p_scr[_head_rows(2 * pr + 1), :], vx[h, 1], preferred_element_type=f32))
            o_ref[:, _pair_cols(pr)] = o_pair.astype(bf16)

    cur = lambda n: (n, 0)
    prev = lambda n: (jnp.maximum(n - 1, 0), 0)
    return _call(
        body, name="attn_fwd", grid=(nb,),
        in_specs=[pl.BlockSpec((QBLOCK, ATTN_WIDTH), cur), pl.BlockSpec((QBLOCK, KV_WIDTH), cur),
                  pl.BlockSpec((QBLOCK, KV_WIDTH), prev), pl.BlockSpec((QBLOCK, KV_WIDTH), cur),
                  pl.BlockSpec((QBLOCK, KV_WIDTH), prev), _bias_spec(), SMEM],
        out_specs=[pl.BlockSpec((QBLOCK, ATTN_WIDTH), cur), pl.BlockSpec((QBLOCK, 2 * HEAD_DIM), cur)],
        out_shape=[S((T, ATTN_WIDTH), bf16), S((T, 2 * HEAD_DIM), f32)],
        scratch_shapes=[pltpu.VMEM((N_Q_HEADS * QBLOCK, 2 * QBLOCK), f32), pltpu.VMEM((N_Q_HEADS * QBLOCK, 2 * QBLOCK), bf16)],
        args=(qn, kn, kn, vb, vb, bias, sinks), ride=ride)


def _layer_norm_stats(h1):
    mu = jnp.mean(h1, axis=-1, keepdims=True)
    xc = h1 - mu
    rstd = lax.rsqrt(jnp.mean(xc * xc, axis=-1, keepdims=True) + EPS)
    return xc * rstd, rstd


def _advanced_windows(win):
    rows = win.shape[0]
    for r in range(8):
        yield r, (win if r == 0 else pltpu.roll(win, rows - r, 0))


def _tap_offsets(r, rows):
    for q in range((rows - CONV_UNIT) // 8 + 1):
        if r == 0 or 8 * q + r + CONV_UNIT <= rows:
            yield q, 8 * q + r


def _conv_fwd(h0, w_dw, b_dw, ln_g, ln_b, tm, ride=None):
    T = h0.shape[0]
    per = tm // CONV_HALO
    lead = CONV_HALO - (CONV_WIDTH - 1)

    def body(hc_ref, hp_ref, w_ref, b_ref, g_ref, bb_ref, h1_ref, h3_ref, cat):
        i = pl.program_id(0)
        cat[0:CONV_HALO, :] = jnp.where(i == 0, 0.0, hp_ref[...])
        cat[CONV_HALO:, :] = hc_ref[...]

        def unit_rows(c, carry):
            r0 = pl.multiple_of(c * CONV_UNIT, CONV_UNIT)
            for j in range(D_MODEL // 128):
                ls = slice(j * 128, (j + 1) * 128)
                win = cat[pl.ds(r0, CONV_UNIT + CONV_HALO), ls]
                acc = jnp.zeros((CONV_UNIT, 128), f32) + b_ref[:, ls]
                for r, adv in _advanced_windows(win):
                    for q, off in _tap_offsets(r, CONV_UNIT + CONV_HALO):
                        k = off - lead
                        if 0 <= k < CONV_WIDTH:
                            acc = acc + adv[8 * q:8 * q + CONV_UNIT] * w_ref[k:k + 1, ls]
                h1_ref[pl.ds(r0, CONV_UNIT), ls] = acc
            return carry

        lax.fori_loop(0, tm // CONV_UNIT, unit_rows, 0)
        acc = h1_ref[...]
        xhat, _ = _layer_norm_stats(acc)
        h2 = xhat * g_ref[...] + bb_ref[...]
        h3_ref[...] = (h2 * _sigmoid(h2)).astype(bf16)

    vec = pl.BlockSpec((1, D_MODEL), lambda i: (0, 0))
    return _call(
        body, name="conv_fwd", grid=(T // tm,),
        in_specs=[pl.BlockSpec((tm, D_MODEL), lambda i: (i, 0)),
                  pl.BlockSpec((CONV_HALO, D_MODEL), lambda i: (jnp.maximum(i * per - 1, 0), 0)),
                  pl.BlockSpec((CONV_WIDTH, D_MODEL), lambda i: (0, 0)), vec, vec, vec],
        out_specs=[pl.BlockSpec((tm, D_MODEL), lambda i: (i, 0)), pl.BlockSpec((tm, D_MODEL), lambda i: (i, 0))],
        out_shape=[S((T, D_MODEL), f32), S((T, D_MODEL), bf16)],
        scratch_shapes=[pltpu.VMEM((tm + CONV_HALO, D_MODEL), f32)],
        args=(h0, h0, w_dw, b_dw, ln_g, ln_b), ride=ride)


def _mix_fwd(x, o, h3, proj, w_ao, w_co, w_o, tm):
    T = x.shape[0]
    row = pl.BlockSpec((tm, D_MODEL), lambda i: (i, 0))
    wsp = _resident((D_MODEL, D_MODEL))
    g0 = GLU_END // COL

    def gate_spec(off):
        return pl.BlockSpec((tm, COL), lambda i: (i, g0 + off))

    def body(x_ref, o_ref, h3_ref, ga0, ga1, gc0, gc1, wa_ref, wc_ref, wo_ref, x1_ref, at_ref, cv_ref, mg_ref):
        attn = jnp.dot(o_ref[...], wa_ref[...], preferred_element_type=f32)
        conv = jnp.dot(h3_ref[...], wc_ref[...], preferred_element_type=f32)
        ga = jnp.concatenate([ga0[...], ga1[...]], axis=-1).astype(f32)
        gc = jnp.concatenate([gc0[...], gc1[...]], axis=-1).astype(f32)
        merged = (_sigmoid(ga) * attn + _sigmoid(gc) * conv).astype(bf16)
        at_ref[...] = attn.astype(bf16)
        cv_ref[...] = conv.astype(bf16)
        mg_ref[...] = merged
        x1_ref[...] = x_ref[...] + jnp.dot(merged, wo_ref[...], preferred_element_type=f32)

    return pl.pallas_call(
        body, name="mix_fwd", grid=(T // tm,),
        in_specs=[row, row, row, gate_spec(0), gate_spec(1), gate_spec(2), gate_spec(3), wsp, wsp, wsp],
        out_specs=[row, row, row, row],
        out_shape=[S((T, D_MODEL), f32), S((T, D_MODEL), bf16), S((T, D_MODEL), bf16), S((T, D_MODEL), bf16)],
        compiler_params=_params("parallel"),
    )(x, o, h3, proj, proj, proj, proj, w_ao, w_co, w_o)


def _ffn_fwd(x1, g, w1, w2, target, tm):
    T = x1.shape[0]
    nj = w1.shape[0] // FF_CHUNK

    def body(x_ref, g_ref, w1_ref, w2_ref, t_ref, a_ref, u_ref, dy_ref, dyb_ref, ls_ref, hm):
        xv = x_ref[...]
        r = lax.rsqrt(jnp.mean(xv * xv, axis=-1, keepdims=True) + EPS)
        u = (xv * r * g_ref[...]).astype(bf16)
        u_ref[...] = u
        for j in range(nj):
            js = slice(j * FF_CHUNK, (j + 1) * FF_CHUNK)
            a = _nt(u, w1_ref[js, :])
            a_ref[:, js] = a.astype(bf16)
            hm[:, js] = jnp.square(jnp.maximum(a, 0.0)).astype(bf16)
        err = xv + jnp.dot(hm[...], w2_ref[...], preferred_element_type=f32) - t_ref[...]
        dy = err * (1.0 / D_MODEL)
        dy_ref[...] = dy
        dyb_ref[...] = dy.astype(bf16)
        ls_ref[...] = jnp.zeros((8, 128), f32) + jnp.sum(err * err) * (0.5 / D_MODEL)

    row = pl.BlockSpec((tm, D_MODEL), lambda i: (i, 0))
    wide = pl.BlockSpec((tm, D_FF), lambda i: (i, 0))
    return pl.pallas_call(
        body, name="ffn_fwd", grid=(T // tm,),
        in_specs=[row, _resident((1, D_MODEL)), _resident(w1.shape), _resident(w2.shape), row],
        out_specs=[wide, row, row, row, pl.BlockSpec((None, 8, 128), lambda i: (i, 0, 0))],
        out_shape=[S((T, D_FF), bf16), S((T, D_MODEL), bf16), S((T, D_MODEL), f32), S((T, D_MODEL), bf16),
                   S((T // tm, 8, 128), f32)],
        scratch_shapes=[pltpu.VMEM((tm, D_FF), bf16)],
        compiler_params=_params("parallel"),
    )(x1, g, w1, w2, target)


def _rms_bwd(du, xv, gv):
    r = lax.rsqrt(jnp.mean(xv * xv, axis=-1, keepdims=True) + EPS)
    xn = xv * r
    dg = jnp.sum(du * xn, axis=0, keepdims=True)
    dxn = du * gv
    dx = r * (dxn - xn * jnp.mean(dxn * xn, axis=-1, keepdims=True))
    return dx, dg


def _ffn_bwd(dy, dyb, a, x1, g, w1, w2, tm, ride=None):
    T = dy.shape[0]
    nj = w1.shape[0] // FF_CHUNK

    def body(dy_ref, dyb_ref, a_ref, x_ref, g_ref, w1_ref, w2_ref, da_ref, dx_ref, dxb_ref, dg_ref):
        @pl.when(pl.program_id(0) == 0)
        def _():
            dg_ref[...] = jnp.zeros_like(dg_ref)

        dyb_v = dyb_ref[...]
        for j in range(nj):
            js = slice(j * FF_CHUNK, (j + 1) * FF_CHUNK)
            dh = _nt(dyb_v, w2_ref[js, :])
            da_ref[:, js] = (dh * (2.0 * jnp.maximum(a_ref[:, js].astype(f32), 0.0))).astype(bf16)
        du = jnp.dot(da_ref[...], w1_ref[...], preferred_element_type=f32)
        dx, dg = _rms_bwd(du, x_ref[...], g_ref[...])
        dx1 = dy_ref[...] + dx
        dx_ref[...] = dx1
        dxb_ref[...] = dx1.astype(bf16)
        dg_ref[...] += dg

    row = pl.BlockSpec((tm, D_MODEL), lambda i: (i, 0))
    wide = pl.BlockSpec((tm, D_FF), lambda i: (i, 0))
    vec = pl.BlockSpec((1, D_MODEL), lambda i: (0, 0))
    return _call(
        body, name="ffn_bwd", grid=(T // tm,),
        in_specs=[row, row, wide, row, _resident((1, D_MODEL)), _resident(w1.shape), _resident(w2.shape)],
        out_specs=[wide, row, row, vec],
        out_shape=[S((T, D_FF), bf16), S((T, D_MODEL), f32), S((T, D_MODEL), bf16), S((1, D_MODEL), f32)],
        args=(dy, dyb, a, x1, g, w1, w2), ride=ride)


def _wgrad(name, a, b, tk, tn, tt, relu2=False, col_shard=False, out_dtype=bf16):
    T, Ka = a.shape
    Nb = b.shape[1]
    nt = T // tt

    def body(a_ref, b_ref, o_ref, acc):
        t = pl.program_id(2)
        av = a_ref[...]
        if relu2:
            av = jnp.square(jnp.maximum(av.astype(f32), 0.0))
        @pl.when(t == 0)
        def _():
            acc[...] = jnp.zeros_like(acc)

        total = acc[...] + _tn(av.astype(bf16), b_ref[...].astype(bf16))
        acc[...] = total
        o_ref[...] = total.astype(out_dtype)

    if col_shard:
        out_shape = S((Nb // tn, Ka, tn), out_dtype)
        out_spec = pl.BlockSpec((None, tk, tn), lambda i, j, t: (j, i, 0))
    else:
        out_shape = S((Ka, Nb), out_dtype)
        out_spec = pl.BlockSpec((tk, tn), lambda i, j, t: (i, j))
    return pl.pallas_call(
        body, name=name, grid=(Ka // tk, Nb // tn, nt),
        in_specs=[pl.BlockSpec((tt, tk), lambda i, j, t: (t, i)), pl.BlockSpec((tt, tn), lambda i, j, t: (t, j))],
        out_specs=out_spec, out_shape=out_shape, scratch_shapes=[pltpu.VMEM((tk, tn), f32)],
        compiler_params=_params("parallel", "parallel", "arbitrary"),
    )(a, b)


def _mix_bwd(dx1, proj, attn, conv, h1, ln_g, ln_b, w_ao, w_co, w_o, tm, ride=None):
    T = dx1.shape[0]
    g0 = GLU_END // COL

    def gate_spec(off):
        return pl.BlockSpec((tm, COL), lambda i: (i, g0 + off))

    def body(dx_ref, ga0, ga1, gc0, gc1, at_ref, cv_ref, h_ref, g_ref, b_ref, wa_ref, wc_ref, wo_ref,
             da_ref, dc_ref, do_ref, dh1_ref, dg_ref, dlg_ref, dlb_ref, dbd_ref):
        @pl.when(pl.program_id(0) == 0)
        def _():
            dlg_ref[...] = jnp.zeros_like(dlg_ref)
            dlb_ref[...] = jnp.zeros_like(dlb_ref)
            dbd_ref[...] = jnp.zeros_like(dbd_ref)

        dm = _nt(dx_ref[...].astype(bf16), wo_ref[...])
        sa = _sigmoid(jnp.concatenate([ga0[...], ga1[...]], axis=-1).astype(f32))
        sc = _sigmoid(jnp.concatenate([gc0[...], gc1[...]], axis=-1).astype(f32))
        dattn = (dm * sa).astype(bf16)
        dconv = (dm * sc).astype(bf16)
        da_ref[...] = dattn
        dc_ref[...] = dconv
        dg_ref[:, 0:D_MODEL] = (dm * at_ref[...].astype(f32) * sa * (1.0 - sa)).astype(bf16)
        dg_ref[:, D_MODEL:2 * D_MODEL] = (dm * cv_ref[...].astype(f32) * sc * (1.0 - sc)).astype(bf16)
        do_ref[...] = _nt(dattn, wa_ref[...]).astype(bf16)
        dh3 = _nt(dconv, wc_ref[...])
        xhat, rstd = _layer_norm_stats(h_ref[...])
        h2 = xhat * g_ref[...] + b_ref[...]
        sg = _sigmoid(h2)
        dh2 = dh3 * (sg * (1.0 + h2 * (1.0 - sg)))
        dlg_ref[...] += jnp.sum(dh2 * xhat, axis=0, keepdims=True)
        dlb_ref[...] += jnp.sum(dh2, axis=0, keepdims=True)
        dxh = dh2 * g_ref[...]
        dh1 = rstd * (dxh - jnp.mean(dxh, axis=-1, keepdims=True) - xhat * jnp.mean(dxh * xhat, axis=-1, keepdims=True))
        dh1_ref[...] = dh1
        dbd_ref[...] += jnp.sum(dh1, axis=0, keepdims=True)

    row = pl.BlockSpec((tm, D_MODEL), lambda i: (i, 0))
    vec = pl.BlockSpec((1, D_MODEL), lambda i: (0, 0))
    par = _resident((1, D_MODEL))
    wsp = _resident((D_MODEL, D_MODEL))
    return _call(
        body, name="mix_bwd", grid=(T // tm,),
        in_specs=[row, gate_spec(0), gate_spec(1), gate_spec(2), gate_spec(3), row, row, row, par, par, wsp, wsp, wsp],
        out_specs=[row, row, row, row, pl.BlockSpec((tm, 2 * D_MODEL), lambda i: (i, 0)), vec, vec, vec],
        out_shape=[S((T, D_MODEL), bf16), S((T, D_MODEL), bf16), S((T, D_MODEL), bf16), S((T, D_MODEL), f32),
                   S((T, 2 * D_MODEL), bf16), S((1, D_MODEL), f32), S((1, D_MODEL), f32), S((1, D_MODEL), f32)],
        args=(dx1, proj, proj, proj, proj, attn, conv, h1, ln_g, ln_b, w_ao, w_co, w_o), ride=ride)


def _conv_bwd(dh1, h0, proj, w_dw, tm, ride=None):
    T = dh1.shape[0]
    per = tm // CONV_HALO
    nh = T // CONV_HALO
    nt = T // tm
    a0 = V_END // COL
    lead = CONV_HALO - (CONV_WIDTH - 1)

    def body(dc_ref, dn_ref, hc_ref, hp_ref, a0_ref, a1_ref, g0_ref, g1_ref, w_ref, dglu_ref, dw_ref, dcat, hcat, wacc, dh0):
        i = pl.program_id(0)

        @pl.when(i == 0)
        def _():
            wacc[...] = jnp.zeros_like(wacc)

        dcat[0:tm, :] = dc_ref[...]
        dcat[tm:, :] = jnp.where(i == nt - 1, 0.0, dn_ref[...])
        hcat[0:CONV_HALO, :] = jnp.where(i == 0, 0.0, hp_ref[...])
        hcat[CONV_HALO:, :] = hc_ref[...]
        span = CONV_UNIT + CONV_HALO

        def unit_rows(c, carry):
            r0 = pl.multiple_of(c * CONV_UNIT, CONV_UNIT)
            for j in range(D_MODEL // 128):
                ls = slice(j * 128, (j + 1) * 128)
                dwin = dcat[pl.ds(r0, span), ls]
                acc = jnp.zeros((CONV_UNIT, 128), f32)
                for r, adv in _advanced_windows(dwin):
                    for q, off in _tap_offsets(r, span):
                        k = CONV_WIDTH - 1 - off
                        if 0 <= k < CONV_WIDTH:
                            acc = acc + adv[8 * q:8 * q + CONV_UNIT] * w_ref[k:k + 1, ls]
                dh0[pl.ds(r0, CONV_UNIT), ls] = acc
                dcur = dwin[0:CONV_UNIT]
                for r, adv in _advanced_windows(hcat[pl.ds(r0, span), ls]):
                    for q, off in _tap_offsets(r, span):
                        k = off - lead
                        if 0 <= k < CONV_WIDTH:
                            prod = dcur * adv[8 * q:8 * q + CONV_UNIT]
                            wacc[k, :, ls] += jnp.sum(prod.reshape(CONV_UNIT // 8, 8, 128), axis=0)
            return carry

        lax.fori_loop(0, tm // CONV_UNIT, unit_rows, 0)
        dh0v = dh0[...]
        av = jnp.concatenate([a0_ref[...], a1_ref[...]], axis=-1).astype(f32)
        sg = _sigmoid(jnp.concatenate([g0_ref[...], g1_ref[...]], axis=-1).astype(f32))
        dglu_ref[:, 0:D_MODEL] = (dh0v * sg).astype(bf16)
        dglu_ref[:, D_MODEL:2 * D_MODEL] = (dh0v * av * sg * (1.0 - sg)).astype(bf16)

        @pl.when(i == nt - 1)
        def _():
            for k in range(CONV_WIDTH):
                dw_ref[k:k + 1, :] = jnp.sum(wacc[k], axis=0, keepdims=True)
            dw_ref[CONV_WIDTH:CONV_WIDTH + 1, :] = jnp.zeros((1, D_MODEL), f32)

    row = pl.BlockSpec((tm, D_MODEL), lambda i: (i, 0))

    def col_spec(off):
        return pl.BlockSpec((tm, COL), lambda i: (i, a0 + off))

    return _call(
        body, name="conv_bwd", grid=(nt,),
        in_specs=[row, pl.BlockSpec((CONV_HALO, D_MODEL), lambda i: (jnp.minimum((i + 1) * per, nh - 1), 0)),
                  row, pl.BlockSpec((CONV_HALO, D_MODEL), lambda i: (jnp.maximum(i * per - 1, 0), 0)),
                  col_spec(0), col_spec(1), col_spec(2), col_spec(3),
                  pl.BlockSpec((CONV_WIDTH, D_MODEL), lambda i: (0, 0))],
        out_specs=[pl.BlockSpec((tm, 2 * D_MODEL), lambda i: (i, 0)), pl.BlockSpec((CONV_WIDTH + 1, D_MODEL), lambda i: (0, 0))],
        out_shape=[S((T, 2 * D_MODEL), bf16), S((CONV_WIDTH + 1, D_MODEL), f32)],
        scratch_shapes=[pltpu.VMEM((tm + CONV_HALO, D_MODEL), f32), pltpu.VMEM((tm + CONV_HALO, D_MODEL), f32),
                        pltpu.VMEM((CONV_WIDTH, 8, D_MODEL), f32), pltpu.VMEM((tm, D_MODEL), f32)],
        args=(dh1, dh1, h0, h0, proj, proj, proj, proj, w_dw), ride=ride)


def _attn_bwd(qn, kn, vb, o, do, lse, bias, sinks, ride=None):
    T = qn.shape[0]
    nb = T // QBLOCK

    def body(q_ref, kc_ref, kp_ref, vc_ref, vp_ref, o_ref, do_ref, lse_ref, b_ref, s_ref,
             dq_ref, dk_ref, dv_ref, db_ref, dsk_ref, kcar, vcar, s_scr, dp_scr, p_scr, ds_scr):
        n = pl.program_id(0)

        @pl.when(n == 0)
        def _():
            db_ref[...] = jnp.zeros_like(db_ref)
            dsk_ref[...] = jnp.zeros_like(dsk_ref)
            kcar[...] = jnp.zeros_like(kcar)
            vcar[...] = jnp.zeros_like(vcar)

        @pl.when(n < nb)
        def _():
            lane = lax.broadcasted_iota(jnp.int32, (QBLOCK, 2 * HEAD_DIM), 1)
            lane_row = lax.broadcasted_iota(jnp.int32, (1, 2 * HEAD_DIM), 1)
            kx = _kv_placements(jnp.concatenate([kp_ref[...], kc_ref[...]], axis=0))
            vx = _kv_placements(jnp.concatenate([vp_ref[...], vc_ref[...]], axis=0))
            lse_tile = lse_ref[...]
            delta, lse_c = {}, {}
            for pr in range(N_Q_HEADS // 2):
                dop = do_ref[:, _pair_cols(pr)]
                dl = dop.astype(f32) * o_ref[:, _pair_cols(pr)].astype(f32)
                for side in range(2):
                    hq = 2 * pr + side
                    h = hq // GROUP
                    qm = _one_head(q_ref[:, _pair_cols(pr)], side)
                    s_scr[_head_rows(hq), :] = _nt(qm, kx[h, side]) + b_ref[_head_rows(hq), :]
                    dp_scr[_head_rows(hq), :] = _nt(_one_head(dop, side), vx[h, side])
                    delta[hq] = jnp.sum(_one_head(dl, side), axis=-1, keepdims=True)
                    lse_c[hq] = jnp.sum(jnp.where(lane == hq, lse_tile, 0.0), axis=-1, keepdims=True)
            dsk = jnp.zeros((1, 2 * HEAD_DIM), f32)
            for hq in range(N_Q_HEADS):
                p = jnp.exp(s_scr[_head_rows(hq), :] - lse_c[hq])
                ds = p * (dp_scr[_head_rows(hq), :] - delta[hq])
                db_ref[_head_rows(hq), :] += ds
                p_scr[_head_rows(hq), :] = p.astype(bf16)
                ds_scr[_head_rows(hq), :] = ds.astype(bf16)
                psink = jnp.exp(s_ref[0, hq] - lse_c[hq])
                dsk = dsk - jnp.where(lane_row == hq, jnp.sum(psink * delta[hq], axis=0, keepdims=True), 0.0)
            dsk_ref[...] += dsk
            for pr in range(N_Q_HEADS // 2):
                h = 2 * pr // GROUP
                dq_ref[:, _pair_cols(pr)] = (jnp.dot(ds_scr[_head_rows(2 * pr), :], kx[h, 0], preferred_element_type=f32)
                                             + jnp.dot(ds_scr[_head_rows(2 * pr + 1), :], kx[h, 1], preferred_element_type=f32))
            folded_k, folded_v = [], []
            for h in range(N_KV_HEADS):
                ka = jnp.zeros((2 * QBLOCK, 2 * HEAD_DIM), f32)
                va = jnp.zeros((2 * QBLOCK, 2 * HEAD_DIM), f32)
                for g in range(GROUP):
                    hq = h * GROUP + g
                    ka = ka + _tn(ds_scr[_head_rows(hq), :], _one_head(q_ref[:, _pair_cols(hq // 2)], hq % 2))
                    va = va + _tn(p_scr[_head_rows(hq), :], _one_head(do_ref[:, _pair_cols(hq // 2)], hq % 2))
                folded_k.append(ka + _swap_halves(ka))
                folded_v.append(va + _swap_halves(va))
            low = _low_lanes()
            for m in range(N_KV_HEADS // 2):
                cs = _pair_cols(m)
                for folded, out_ref, car in ((folded_k, dk_ref, kcar), (folded_v, dv_ref, vcar)):
                    band = jnp.where(low, folded[2 * m], folded[2 * m + 1])
                    out_ref[:, cs] = car[:, cs] + band[0:QBLOCK, :]
                    car[:, cs] = band[QBLOCK:, :]

        @pl.when(n == nb)
        def _():
            dk_ref[...] = kcar[...]
            dv_ref[...] = vcar[...]

    cur = lambda n: (jnp.minimum(n, nb - 1), 0)
    prev = lambda n: (jnp.clip(n - 1, 0, nb - 1), 0)
    qspec = pl.BlockSpec((QBLOCK, ATTN_WIDTH), cur)
    kcur, kprev = pl.BlockSpec((QBLOCK, KV_WIDTH), cur), pl.BlockSpec((QBLOCK, KV_WIDTH), prev)
    whole = lambda shape: pl.BlockSpec(shape, lambda n: (0,) * len(shape))
    scores = (N_Q_HEADS * QBLOCK, 2 * QBLOCK)
    return _call(
        body, name="attn_bwd", grid=(nb + 1,),
        in_specs=[qspec, kcur, kprev, kcur, kprev, qspec, qspec, pl.BlockSpec((QBLOCK, 2 * HEAD_DIM), cur), _bias_spec(), SMEM],
        out_specs=[qspec, kprev, kprev, whole(scores), whole((1, 2 * HEAD_DIM))],
        out_shape=[S((T, ATTN_WIDTH), f32), S((T, KV_WIDTH), f32), S((T, KV_WIDTH), f32), S(scores, f32),
                   S((1, 2 * HEAD_DIM), f32)],
        scratch_shapes=[pltpu.VMEM((QBLOCK, KV_WIDTH), f32), pltpu.VMEM((QBLOCK, KV_WIDTH), f32),
                        pltpu.VMEM(scores, f32), pltpu.VMEM(scores, f32), pltpu.VMEM(scores, bf16), pltpu.VMEM(scores, bf16)],
        args=(qn, kn, kn, vb, vb, o, do, lse, bias, sinks), ride=ride)


def _rel_bias_bwd(dbias, bucket):
    def body(d_ref, bk_ref, o_ref):
        b = bk_ref[...]
        for k in range(N_BUCKETS):
            mk = b == k
            for h in range(N_Q_HEADS):
                o_ref[k, h] = jnp.sum(jnp.where(mk, d_ref[h * QBLOCK:(h + 1) * QBLOCK, :], 0.0))

    return pl.pallas_call(body, name="rel_bias_bwd", out_shape=S((N_BUCKETS, N_Q_HEADS), f32), out_specs=SMEM)(dbias, bucket)


def _qk_norm_bwd(dq, dk, dv, proj, qg, kg, tm):
    T = dq.shape[0]
    scale = HEAD_DIM ** -0.5

    def pair_bwd(dy, x, gv):
        r = _pair_rstd(x, True)
        xn = x * r
        dxn = dy * gv
        dx = r * (dxn - xn * _pair_mean(dxn * xn, True))
        return dx, jnp.sum(dy * xn, axis=0, keepdims=True)

    def body(dq_ref, dk_ref, dv_ref, p_ref, qg_ref, kg_ref, out_ref, dqg_ref, dkg_ref):
        @pl.when(pl.program_id(0) == 0)
        def _():
            dqg_ref[...] = jnp.zeros_like(dqg_ref)
            dkg_ref[...] = jnp.zeros_like(dkg_ref)

        qgv, kgv = qg_ref[...], kg_ref[...]
        dqg = jnp.zeros((1, 2 * HEAD_DIM), f32)
        for pr in range(N_Q_HEADS // 2):
            dx, dg = pair_bwd(dq_ref[:, _pair_cols(pr)] * scale, p_ref[:, _pair_cols(pr)].astype(f32), qgv)
            out_ref[:, _pair_cols(pr)] = dx.astype(bf16)
            dqg = dqg + dg
        dkg = jnp.zeros((1, 2 * HEAD_DIM), f32)
        for pr in range(N_KV_HEADS // 2):
            ps = slice(Q_END + pr * 2 * HEAD_DIM, Q_END + (pr + 1) * 2 * HEAD_DIM)
            dx, dg = pair_bwd(dk_ref[:, _pair_cols(pr)], p_ref[:, ps].astype(f32), kgv)
            out_ref[:, ps] = dx.astype(bf16)
            dkg = dkg + dg
        out_ref[:, K_END:V_END] = dv_ref[...].astype(bf16)
        dqg_ref[...] += dqg
        dkg_ref[...] += dkg

    vec = pl.BlockSpec((1, 2 * HEAD_DIM), lambda i: (0, 0))
    return pl.pallas_call(
        body, name="qk_norm_bwd", grid=(T // tm,),
        in_specs=[pl.BlockSpec((tm, ATTN_WIDTH), lambda i: (i, 0)), pl.BlockSpec((tm, KV_WIDTH), lambda i: (i, 0)),
                  pl.BlockSpec((tm, KV_WIDTH), lambda i: (i, 0)), pl.BlockSpec((tm, V_END), lambda i: (i, 0)), vec, vec],
        out_specs=[pl.BlockSpec((tm, V_END), lambda i: (i, 0)), vec, vec],
        out_shape=[S((T, V_END), bf16), S((1, 2 * HEAD_DIM), f32), S((1, 2 * HEAD_DIM), f32)],
        compiler_params=_params("arbitrary"),
    )(dq, dk, dv, proj, qg, kg)


def _in_bwd(dqkv, dglu, dgates, w_in, x, g, dx1, tm, ride=None):
    T = x.shape[0]
    pieces = (dqkv, dglu, dgates)
    starts = [0, dqkv.shape[1], dqkv.shape[1] + dglu.shape[1]]

    def body(a0_ref, a1_ref, a2_ref, w_ref, x_ref, g_ref, d_ref, gx_ref, dg_ref):
        @pl.when(pl.program_id(0) == 0)
        def _():
            dg_ref[...] = jnp.zeros_like(dg_ref)

        du = jnp.zeros((tm, D_MODEL), f32)
        for a_ref, c0 in zip((a0_ref, a1_ref, a2_ref), starts):
            du = du + _nt(a_ref[...], w_ref[:, c0:c0 + a_ref.shape[1]])
        dx, dg = _rms_bwd(du, x_ref[...], g_ref[...])
        gx_ref[...] = d_ref[...] + dx
        dg_ref[...] += dg

    row = pl.BlockSpec((tm, D_MODEL), lambda i: (i, 0))
    return _call(
        body, name="in_bwd", grid=(T // tm,),
        in_specs=[pl.BlockSpec((tm, p.shape[1]), lambda i: (i, 0)) for p in pieces]
        + [_resident(w_in.shape), row, _resident((1, D_MODEL)), row],
        out_specs=[row, pl.BlockSpec((1, D_MODEL), lambda i: (0, 0))],
        out_shape=[S((T, D_MODEL), f32), S((1, D_MODEL), f32)],
        args=(dqkv, dglu, dgates, w_in, x, g, dx1), ride=ride)


def _adamw(name, parts, w, m, v, tr):
    R, C = w.shape
    bc1 = 1.0 - ADAM_B1 ** ADAM_STEP
    bc2 = 1.0 - ADAM_B2 ** ADAM_STEP

    def body(p_ref, w_ref, m_ref, v_ref, g_ref, d_ref, nm_ref, nv_ref):
        g = p_ref[0].astype(f32)
        for k in range(1, N_DEV):
            g = g + p_ref[k].astype(f32)
        nm = ADAM_B1 * m_ref[...] + (1.0 - ADAM_B1) * g
        nv = ADAM_B2 * v_ref[...] + (1.0 - ADAM_B2) * (g * g)
        g_ref[...] = g
        nm_ref[...] = nm
        nv_ref[...] = nv
        d_ref[...] = -ADAM_LR * ((nm / bc1) / (jnp.sqrt(nv / bc2) + ADAM_EPS) + ADAM_WD * w_ref[...])

    blk = pl.BlockSpec((tr, C), lambda i: (i, 0))
    return pl.pallas_call(
        body, name=name, grid=(R // tr,),
        in_specs=[pl.BlockSpec((N_DEV, tr, C), lambda i: (0, i, 0)), blk, blk, blk],
        out_specs=[blk, blk, blk, blk], out_shape=[S((R, C), f32)] * 4,
        compiler_params=_params("parallel"),
    )(parts, w, m, v)


def _tile(T, pref):
    return min(T, pref)


def _pad_rows(a, rows):
    return jnp.pad(a, ((0, rows - a.shape[0]), (0, 0)))


def kernel(x, norm_mix_g, w_in, q_norm_g, k_norm_g, attn_sinks, rel_bias, w_attn_o, w_dw, b_dw, conv_ln_g, conv_ln_b, w_conv_out, w_out, norm_mlp_g, w_ff1, w_ff2, loss_target, m_norm_mix_g, m_w_in, m_q_norm_g, m_k_norm_g, m_attn_sinks, m_rel_bias, m_w_attn_o, m_w_dw, m_b_dw, m_conv_ln_g, m_conv_ln_b, m_w_conv_out, m_w_out, m_norm_mlp_g, m_w_ff1, m_w_ff2, v_norm_mix_g, v_w_in, v_q_norm_g, v_k_norm_g, v_attn_sinks, v_rel_bias, v_w_attn_o, v_w_dw, v_b_dw, v_conv_ln_g, v_conv_ln_b, v_w_conv_out, v_w_out, v_norm_mlp_g, v_w_ff1, v_w_ff2):
    T = x.shape[1]
    xs = x[0]
    tgt = loss_target[0]
    in_shard = IN_WIDTH // N_DEV
    dw_rows = CONV_WIDTH + 1
    ch_shard = D_MODEL // N_DEV
    tc = _tile(T, 256)
    tb = _tile(T, 512)
    tt = _tile(T, 2048)
    bucket = jnp.asarray(_t5_bucket_table())

    g_in, g_dw = _exchange("gather_w_in", [w_in[0].astype(bf16), _pad_rows(w_dw[0], dw_rows)], gather=True, two_level=True)
    W_in = jnp.transpose(g_in, (1, 0, 2)).reshape(D_MODEL, IN_WIDTH)
    W_dw = jnp.transpose(g_dw, (1, 0, 2)).reshape(dw_rows, D_MODEL)[:CONV_WIDTH]

    mix_shards = _Gather([w_attn_o[0].astype(bf16), w_conv_out[0].astype(bf16), w_out[0].astype(bf16)])
    qg2, kg2 = jnp.tile(q_norm_g, (1, 2)), jnp.tile(k_norm_g, (1, 2))
    (proj, u, qn, kn, vb, h0), (g_ao, g_co, g_o) = _proj_fwd(xs, norm_mix_g, W_in, qg2, kg2, tb, ride=mix_shards)
    W_ao = g_ao.reshape(D_MODEL, D_MODEL)
    W_co = g_co.reshape(D_MODEL, D_MODEL)
    W_o = g_o.reshape(D_MODEL, D_MODEL)
    bias = _bias_table(rel_bias, bucket)
    (o, lse), (g_f1,) = _attn_fwd(qn, kn, vb, bias, attn_sinks, ride=_Gather([w_ff1[0].astype(bf16).T]))
    W_f1t = g_f1.reshape(D_FF, D_MODEL)
    (h1, h3), (g_f2,) = _conv_fwd(h0, W_dw, b_dw, conv_ln_g, conv_ln_b, tc, ride=_Gather([w_ff2[0].astype(bf16)]))
    x1, attn, conv, merged = _mix_fwd(xs, o, h3, proj, W_ao, W_co, W_o, tb)
    W_f2 = g_f2.reshape(D_FF, D_MODEL)
    a, u2, dy, dyb, loss_parts = _ffn_fwd(x1, norm_mlp_g, W_f1t, W_f2, tgt, tb)
    loss = lax.psum(jnp.sum(loss_parts[:, 0, 0]), ("x", "y", "c"))

    gw_f2 = _wgrad("wgrad_ff2", a, dyb, D_MODEL, D_MODEL, tt, relu2=True).reshape(N_DEV, FF_CHUNK, D_MODEL)
    (da, dx1, dx1b, d_norm_mlp_g), (l_f2,) = _ffn_bwd(dy, dyb, a, x1, norm_mlp_g, W_f1t, W_f2, tc,
                                                      ride=_Exchange([gw_f2], gather=False))
    gw_f1 = _wgrad("wgrad_ff1", u2, da, D_MODEL, FF_CHUNK, tt, col_shard=True)
    gw_o = _wgrad("wgrad_out", merged, dx1b, D_MODEL, D_MODEL, tt).reshape(N_DEV, ch_shard, D_MODEL)
    (dattn, dconv, do, dh1, dgates, d_ln_g, d_ln_b, d_b_dw), (l_o,) = _mix_bwd(
        dx1b, proj, attn, conv, h1, conv_ln_g, conv_ln_b, W_ao, W_co, W_o, tb, ride=_Exchange([gw_o], gather=False))
    gw_ao = _wgrad("wgrad_attn_o", o, dattn, D_MODEL, D_MODEL, tt).reshape(N_DEV, ch_shard, D_MODEL)
    gw_co = _wgrad("wgrad_conv_out", h3, dconv, D_MODEL, D_MODEL, tt).reshape(N_DEV, ch_shard, D_MODEL)
    (dglu, d_w_dw), (l_f1, l_ao, l_co) = _conv_bwd(dh1, h0, proj, W_dw, tc,
                                                   ride=_Exchange([gw_f1, gw_ao, gw_co], gather=False))
    (dq, dk, dv, dbias, d_sinks), _ = _attn_bwd(qn, kn, vb, o, do, lse, bias, attn_sinks)
    d_sinks = d_sinks[:, :N_Q_HEADS]
    d_rel_bias = _rel_bias_bwd(dbias, bucket)
    dqkv, d_qg, d_kg = _qk_norm_bwd(dq, dk, dv, proj, qg2, kg2, tb)
    d_qg = d_qg[:, :HEAD_DIM] + d_qg[:, HEAD_DIM:]
    d_kg = d_kg[:, :HEAD_DIM] + d_kg[:, HEAD_DIM:]
    gw_in = jnp.concatenate([_wgrad("wgrad_in_qkv", u, dqkv, D_MODEL, COL, tt),
                             _wgrad("wgrad_in_glu", u, dglu, D_MODEL, D_MODEL, tt),
                             _wgrad("wgrad_in_gates", u, dgates, D_MODEL, D_MODEL, tt)], axis=1)
    gw_in = jnp.transpose(gw_in.reshape(D_MODEL, N_DEV, in_shard), (1, 0, 2))
    gw_dw = jnp.transpose(d_w_dw.reshape(dw_rows, N_DEV, ch_shard), (1, 0, 2))
    (grad_x, d_norm_mix_g), (l_in, l_dw) = _in_bwd(dqkv, dglu, dgates, W_in, xs, norm_mix_g, dx1, tb,
                                                    ride=_Exchange([gw_in, gw_dw], gather=False))

    def row(vec):
        flat = vec.reshape(1, -1)
        return jnp.pad(flat, ((0, 0), (0, D_MODEL - flat.shape[1])))

    def pack_small(nm, qg, kg, sk, rb, bd, lg, lb, nl):
        tail = jnp.concatenate([qg.reshape(1, -1), kg.reshape(1, -1), sk.reshape(1, -1), rb.reshape(1, -1)], axis=1)
        return jnp.concatenate([row(nm), row(bd), row(lg), row(lb), row(nl), row(tail), jnp.zeros((2, D_MODEL), f32)], axis=0)

    def unpack_small(p):
        t = p[5]
        o0, o1, o2 = HEAD_DIM, 2 * HEAD_DIM, 2 * HEAD_DIM + N_Q_HEADS
        return dict(norm_mix_g=p[0:1], b_dw=p[1:2], conv_ln_g=p[2:3], conv_ln_b=p[3:4], norm_mlp_g=p[4:5],
                    q_norm_g=t[0:o0].reshape(1, HEAD_DIM), k_norm_g=t[o0:o1].reshape(1, HEAD_DIM),
                    attn_sinks=t[o1:o2].reshape(1, N_Q_HEADS),
                    rel_bias=t[o2:o2 + N_BUCKETS * N_Q_HEADS].reshape(N_BUCKETS, N_Q_HEADS))

    small_g = pack_small(d_norm_mix_g, d_qg, d_kg, d_sinks, d_rel_bias, d_b_dw, d_ln_g, d_ln_b, d_norm_mlp_g)
    (l_small,) = _exchange("gather_small_grads", [small_g], gather=True)


    res = {}
    res["w_in"] = _adamw("adamw_in", l_in, w_in[0], m_w_in[0], v_w_in[0], 256)
    res["w_attn_o"] = _adamw("adamw_attn_o", l_ao, w_attn_o[0], m_w_attn_o[0], v_w_attn_o[0], ch_shard)
    res["w_conv_out"] = _adamw("adamw_conv_out", l_co, w_conv_out[0], m_w_conv_out[0], v_w_conv_out[0], ch_shard)
    res["w_out"] = _adamw("adamw_out", l_o, w_out[0], m_w_out[0], v_w_out[0], ch_shard)
    res["w_ff1"] = _adamw("adamw_ff1", l_f1, w_ff1[0], m_w_ff1[0], v_w_ff1[0], 256)
    res["w_ff2"] = _adamw("adamw_ff2", l_f2, w_ff2[0], m_w_ff2[0], v_w_ff2[0], 256)
    dw4 = _adamw("adamw_dw", l_dw, _pad_rows(w_dw[0], dw_rows), _pad_rows(m_w_dw[0], dw_rows), _pad_rows(v_w_dw[0], dw_rows), dw_rows)
    res["w_dw"] = [t[:CONV_WIDTH] for t in dw4]
    small_w = pack_small(norm_mix_g, q_norm_g, k_norm_g, attn_sinks, rel_bias, b_dw, conv_ln_g, conv_ln_b, norm_mlp_g)
    small_m = pack_small(m_norm_mix_g, m_q_norm_g, m_k_norm_g, m_attn_sinks, m_rel_bias, m_b_dw, m_conv_ln_g, m_conv_ln_b, m_norm_mlp_g)
    small_v = pack_small(v_norm_mix_g, v_q_norm_g, v_k_norm_g, v_attn_sinks, v_rel_bias, v_b_dw, v_conv_ln_g, v_conv_ln_b, v_norm_mlp_g)
    small4 = [unpack_small(t) for t in _adamw("adamw_small", l_small, small_w, small_m, small_v, 8)]

    order = ["norm_mix_g", "w_in", "q_norm_g", "k_norm_g", "attn_sinks", "rel_bias", "w_attn_o", "w_dw", "b_dw",
             "conv_ln_g", "conv_ln_b", "w_conv_out", "w_out", "norm_mlp_g", "w_ff1", "w_ff2"]
    stacked = {"w_in", "w_attn_o", "w_dw", "w_conv_out", "w_out", "w_ff1", "w_ff2"}
    outs = [loss, grad_x[None]]
    for k in range(4):
        for nme in order:
            if nme in stacked:
                outs.append(res[nme][k][None])
            else:
                outs.append(small4[k][nme])
    return tuple(outs)
```

```python
import functools

import numpy as np
import jax
import jax.numpy as jnp
from jax import lax
from jax.experimental import pallas as pl
from jax.experimental.pallas import tpu as pltpu

f32 = jnp.float32
bf16 = jnp.bfloat16
S = jax.ShapeDtypeStruct

N_DEV = 8
D_MODEL = 1024
HEAD_DIM = 64
N_Q_HEADS = 16
N_KV_HEADS = 4
GROUP = N_Q_HEADS // N_KV_HEADS
ATTN_WIDTH = N_Q_HEADS * HEAD_DIM
KV_WIDTH = N_KV_HEADS * HEAD_DIM
QBLOCK = 128
CONV_WIDTH = 31
CONV_HALO = 32
CONV_UNIT = 64
D_FF = 4 * D_MODEL
N_BUCKETS = 32
MAX_DISTANCE = 128
EPS = 1e-6
NEG = -1e30
Q_END = ATTN_WIDTH
K_END = Q_END + KV_WIDTH
V_END = K_END + KV_WIDTH
GLU_END = V_END + 2 * D_MODEL
IN_WIDTH = GLU_END + 2 * D_MODEL
COL = 512
FF_CHUNK = D_FF // N_DEV

ADAM_LR = 0.001
ADAM_B1 = 0.9
ADAM_B2 = 0.999
ADAM_EPS = 1e-08
ADAM_WD = 0.01
ADAM_STEP = 10

VMEM_LIMIT = 56 * 1024 * 1024

MESH_ID = pl.DeviceIdType.MESH
ANY = pl.BlockSpec(memory_space=pl.ANY)
SMEM = pl.BlockSpec(memory_space=pltpu.SMEM)


def _params(*sem):
    return pltpu.CompilerParams(dimension_semantics=sem, vmem_limit_bytes=VMEM_LIMIT)


def _nt(a, b):
    return lax.dot_general(a, b, (((1,), (1,)), ((), ())), preferred_element_type=f32)


def _tn(a, b):
    return lax.dot_general(a, b, (((0,), (0,)), ((), ())), preferred_element_type=f32)


def _sigmoid(z):
    return 1.0 / (1.0 + jnp.exp(-z))


def _t5_bucket_table():
    qi = np.arange(QBLOCK, dtype=np.int32)[:, None]
    kj = np.arange(2 * QBLOCK, dtype=np.int32)[None, :]
    dist = qi + QBLOCK - kj
    n = np.maximum(dist, 0)
    max_exact = N_BUCKETS // 2
    nf = np.maximum(n, 1).astype(np.float32)
    large = max_exact + (np.log(nf / np.float32(max_exact)) / np.float32(np.log(MAX_DISTANCE / max_exact))
                         * np.float32(N_BUCKETS - max_exact)).astype(np.int32)
    large = np.minimum(large, N_BUCKETS - 1)
    bucket = np.where(n < max_exact, n, large)
    valid = (dist >= 0) & (dist < QBLOCK)
    return np.where(valid, bucket, -1).astype(np.int32)


def _peer(d):
    x, y, c = lax.axis_index("x"), lax.axis_index("y"), lax.axis_index("c")
    dx, dy, dc = (d >> 2) & 1, (d >> 1) & 1, d & 1
    px, py, pc = x ^ dx, y ^ dy, c ^ dc
    return (px, py, pc), 4 * px + 2 * py + pc


class _Exchange:
    def __init__(self, arrays, gather):
        self.arrays, self.gather, self.n = list(arrays), gather, len(arrays)
        self.out_shape = [S(((N_DEV,) + a.shape) if gather else a.shape, a.dtype) for a in self.arrays]
        self.scratch = [pltpu.SemaphoreType.DMA((self.n, N_DEV - 1)), pltpu.SemaphoreType.DMA((self.n, N_DEV - 1)),
                        pltpu.SemaphoreType.DMA((self.n,))]

    def _copies(self, ins, outs, sems):
        send_sems, recv_sems, local_sems = sems
        _, me = _peer(0)
        local, sends, recvs = [], [], []
        for k in range(self.n):
            src = ins[k] if self.gather else ins[k].at[me]
            local.append(pltpu.make_async_copy(src, outs[k].at[me], local_sems.at[k]))
        for d in range(1, N_DEV):
            peer, pidx = _peer(d)
            for k in range(self.n):
                src = ins[k] if self.gather else ins[k].at[pidx]
                common = dict(src_ref=src, send_sem=send_sems.at[k, d - 1], recv_sem=recv_sems.at[k, d - 1],
                              device_id=peer, device_id_type=MESH_ID)
                sends.append(pltpu.make_async_remote_copy(dst_ref=outs[k].at[me], **common))
                recvs.append(pltpu.make_async_remote_copy(dst_ref=outs[k].at[pidx], **common))
        return local, sends, recvs

    def start(self, ins, outs, sems):
        local, sends, _ = self._copies(ins, outs, sems)
        for cp in local + sends:
            cp.start()

    def wait(self, ins, outs, sems):
        local, sends, recvs = self._copies(ins, outs, sems)
        for cp in recvs:
            cp.wait_recv()
        for cp in sends:
            cp.wait_send()
        for cp in local:
            cp.wait()


class _Gather:
    CHIPS = (4, 2, 6)
    SLOTS = 1 + 2 * len(CHIPS)

    def __init__(self, arrays):
        self.arrays, self.n = list(arrays), len(arrays)
        self.out_shape = [S((N_DEV,) + a.shape, a.dtype) for a in self.arrays]
        self.scratch = [pltpu.SemaphoreType.DMA((self.n, self.SLOTS)), pltpu.SemaphoreType.DMA((self.n, self.SLOTS)),
                        pltpu.SemaphoreType.DMA((self.n,))]

    @staticmethod
    def _copy(outs, sems, k, slot, src, block, to):
        return pltpu.make_async_remote_copy(src_ref=src, dst_ref=outs[k].at[block], send_sem=sems[0].at[k, slot],
                                            recv_sem=sems[1].at[k, slot], device_id=to, device_id_type=MESH_ID)

    def _local(self, ins, outs, sems):
        _, me = _peer(0)
        return [pltpu.make_async_copy(ins[k], outs[k].at[me], sems[2].at[k]) for k in range(self.n)]

    def start(self, ins, outs, sems):
        _, me = _peer(0)
        sibling, _ = _peer(1)
        for cp in self._local(ins, outs, sems):
            cp.start()
        for k in range(self.n):
            self._copy(outs, sems, k, 0, ins[k], me, sibling).start()
            for j, d in enumerate(self.CHIPS):
                self._copy(outs, sems, k, 1 + j, ins[k], me, _peer(d)[0]).start()

    def mid(self, ins, outs, sems):
        sibling, _ = _peer(1)
        for j, d in enumerate(self.CHIPS):
            chip, block = _peer(d)
            for k in range(self.n):
                self._copy(outs, sems, k, 1 + j, ins[k], block, chip).wait_recv()
                self._copy(outs, sems, k, 4 + j, outs[k].at[block], block, sibling).start()

    def wait(self, ins, outs, sems):
        _, me = _peer(0)
        sibling, sib_block = _peer(1)
        for k in range(self.n):
            self._copy(outs, sems, k, 0, ins[k], sib_block, sibling).wait_recv()
            for j, d in enumerate(self.CHIPS):
                self._copy(outs, sems, k, 4 + j, ins[k], _peer(d ^ 1)[1], sibling).wait_recv()
        for k in range(self.n):
            self._copy(outs, sems, k, 0, ins[k], me, sibling).wait_send()
            for j, d in enumerate(self.CHIPS):
                chip, block = _peer(d)
                self._copy(outs, sems, k, 1 + j, ins[k], me, chip).wait_send()
                self._copy(outs, sems, k, 4 + j, outs[k].at[block], block, sibling).wait_send()
        for cp in self._local(ins, outs, sems):
            cp.wait()


def _exchange(name, arrays, gather, two_level=False):
    ex = _Gather(arrays) if two_level else _Exchange(arrays, gather)
    n = ex.n

    def body(*refs):
        ins, outs, sems = refs[:n], refs[n:2 * n], refs[2 * n:]
        ex.start(ins, outs, sems)
        if two_level:
            ex.mid(ins, outs, sems)
        ex.wait(ins, outs, sems)

    return pl.pallas_call(body, name=name, out_shape=ex.out_shape, in_specs=[ANY] * n, out_specs=[ANY] * n,
                          scratch_shapes=ex.scratch)(*arrays)


def _call(body, *, name, grid, in_specs, out_specs, out_shape, args, scratch_shapes=(), ride=None):
    n_in, n_out, n_sc = len(in_specs), len(out_specs), len(scratch_shapes)
    sem = ("arbitrary",) * len(grid)
    if ride is None:
        res = pl.pallas_call(body, name=name, grid=grid, in_specs=list(in_specs), out_specs=list(out_specs),
                             out_shape=list(out_shape), scratch_shapes=list(scratch_shapes), compiler_params=_params(*sem))(*args)
        return list(res), []
    nx = ride.n

    def riding(*refs):
        ins, xin = refs[:n_in], refs[n_in:n_in + nx]
        outs, xout = refs[n_in + nx:n_in + nx + n_out], refs[n_in + nx + n_out:n_in + 2 * nx + n_out]
        rest = refs[n_in + 2 * nx + n_out:]
        scratch, sems = rest[:n_sc], rest[n_sc:]
        ids = [pl.program_id(ax) for ax in range(len(grid))]
        first = functools.reduce(jnp.logical_and, [i == 0 for i in ids])
        last = functools.reduce(jnp.logical_and, [i == g - 1 for i, g in zip(ids, grid)])

        @pl.when(first)
        def _():
            ride.start(xin, xout, sems)

        if hasattr(ride, "mid"):
            halfway = functools.reduce(jnp.logical_and, [ids[0] == grid[0] // 2] + [i == 0 for i in ids[1:]])

            @pl.when(halfway)
            def _():
                ride.mid(xin, xout, sems)

        body(*ins, *outs, *scratch)

        @pl.when(last)
        def _():
            ride.wait(xin, xout, sems)

    res = pl.pallas_call(
        riding, name=name, grid=grid, in_specs=list(in_specs) + [ANY] * nx, out_specs=list(out_specs) + [ANY] * nx,
        out_shape=list(out_shape) + ride.out_shape, scratch_shapes=list(scratch_shapes) + ride.scratch,
        compiler_params=_params(*sem))(*args, *ride.arrays)
    return list(res[:n_out]), list(res[n_out:])


def _resident(shape):
    return pl.BlockSpec(shape, lambda *_: (0,) * len(shape), pipeline_mode=pl.Buffered(1))


def _proj_fwd(x, g, w, qg, kg, tm, ride=None):
    T, K = x.shape
    N = w.shape[1]
    per = COL // (2 * HEAD_DIM)
    assert Q_END % COL == 0 and V_END == Q_END + COL and (GLU_END - V_END) == 4 * COL and KV_WIDTH == COL // 2

    def body(x_ref, g_ref, w_ref, qg_ref, kg_ref, o_ref, u_ref, qn_ref, kn_ref, vb_ref, h0_ref):
        xv = x_ref[...]
        r = lax.rsqrt(jnp.mean(xv * xv, axis=-1, keepdims=True) + EPS)
        u = (xv * r * g_ref[...]).astype(bf16)
        u_ref[...] = u

        def block(c):
            cs = slice(c * COL, (c + 1) * COL)
            pc = jnp.dot(u, w_ref[:, cs], preferred_element_type=f32)
            o_ref[:, cs] = pc.astype(bf16)
            return pc

        qgv = qg_ref[...] * (HEAD_DIM ** -0.5)
        for c in range(Q_END // COL):
            pc = block(c)
            for t in range(per):
                xq = pc[:, _pair_cols(t)]
                qn_ref[:, _pair_cols(c * per + t)] = (xq * _pair_rstd(xq, False) * qgv).astype(bf16)
        pc = block(Q_END // COL)
        for t in range(KV_WIDTH // (2 * HEAD_DIM)):
            xk = pc[:, _pair_cols(t)]
            kn_ref[:, _pair_cols(t)] = (xk * _pair_rstd(xk, False) * kg_ref[...]).astype(bf16)
        vb_ref[...] = pc[:, KV_WIDTH:].astype(bf16)
        a0 = V_END // COL
        for half in range(2):
            gate = block(a0 + 2 + half)
            h0_ref[:, half * COL:(half + 1) * COL] = block(a0 + half) * _sigmoid(gate)
        for c in range(GLU_END // COL, N // COL):
            block(c)

    row = lambda width: pl.BlockSpec((tm, width), lambda i: (i, 0))
    return _call(
        body, name="proj_fwd", grid=(T // tm,),
        in_specs=[row(K), _resident((1, K)), _resident((K, N)), _resident((1, 2 * HEAD_DIM)), _resident((1, 2 * HEAD_DIM))],
        out_specs=[row(N), row(K), row(ATTN_WIDTH), row(KV_WIDTH), row(KV_WIDTH), row(D_MODEL)],
        out_shape=[S((T, N), bf16), S((T, K), bf16), S((T, ATTN_WIDTH), bf16), S((T, KV_WIDTH), bf16), S((T, KV_WIDTH), bf16),
                   S((T, D_MODEL), f32)],
        args=(x, g, w, qg, kg), ride=ride)


def _bias_table(rel_bias, bucket):
    def body(rb_ref, bk_ref, o_ref):
        b = bk_ref[...]
        absent = lax.broadcasted_iota(jnp.int32, (QBLOCK, 2 * QBLOCK), 1) < QBLOCK
        for h in range(N_Q_HEADS):
            acc = jnp.full((QBLOCK, 2 * QBLOCK), NEG, f32)
            for k in range(N_BUCKETS):
                acc = jnp.where(b == k, rb_ref[k, h], acc)
            o_ref[0, h * QBLOCK:(h + 1) * QBLOCK, :] = acc
            o_ref[1, h * QBLOCK:(h + 1) * QBLOCK, :] = jnp.where(absent, NEG, acc)

    return pl.pallas_call(
        body, name="bias_table", out_shape=S((2, N_Q_HEADS * QBLOCK, 2 * QBLOCK), f32),
        in_specs=[SMEM, pl.BlockSpec(memory_space=pltpu.VMEM)],
    )(rel_bias, bucket)


def _bias_spec():
    return pl.BlockSpec((None, N_Q_HEADS * QBLOCK, 2 * QBLOCK), lambda n: (jnp.where(n == 0, 1, 0), 0, 0))


def _swap_halves(t):
    return jnp.concatenate([t[:, HEAD_DIM:], t[:, :HEAD_DIM]], axis=1)


def _low_lanes():
    return lax.broadcasted_iota(jnp.int32, (1, 2 * HEAD_DIM), 1) < HEAD_DIM


def _one_head(pair, side):
    zero = jnp.zeros((), pair.dtype)
    return jnp.where(_low_lanes(), pair, zero) if side == 0 else jnp.where(_low_lanes(), zero, pair)


def _pair_mean(t, on_mxu):
    if not on_mxu:
        m_lo = jnp.sum(_one_head(t, 0), axis=-1, keepdims=True) * (1.0 / HEAD_DIM)
        m_hi = jnp.sum(_one_head(t, 1), axis=-1, keepdims=True) * (1.0 / HEAD_DIM)
        return jnp.where(_low_lanes(), m_lo, m_hi)
    width = 2 * HEAD_DIM
    same_head = ((lax.broadcasted_iota(jnp.int32, (width, width), 0) < HEAD_DIM)
                 == (lax.broadcasted_iota(jnp.int32, (width, width), 1) < HEAD_DIM))
    e = jnp.where(same_head, 1.0 / HEAD_DIM, 0.0).astype(bf16)
    hi = t.astype(bf16)
    lo = (t - hi.astype(f32)).astype(bf16)
    return jnp.dot(hi, e, preferred_element_type=f32) + jnp.dot(lo, e, preferred_element_type=f32)


def _pair_rstd(x, on_mxu):
    return lax.rsqrt(_pair_mean(x * x, on_mxu) + EPS)


def _kv_placements(band):
    out = {}
    for m in range(N_KV_HEADS // 2):
        pair = band[:, m * 2 * HEAD_DIM:(m + 1) * 2 * HEAD_DIM]
        swapped = _swap_halves(pair)
        for hh in range(2):
            out[2 * m + hh, 0] = _one_head(pair if hh == 0 else swapped, 0)
            out[2 * m + hh, 1] = _one_head(swapped if hh == 0 else pair, 1)
    return out


def _head_rows(hq):
    return slice(hq * QBLOCK, (hq + 1) * QBLOCK)


def _pair_cols(pr):
    return slice(pr * 2 * HEAD_DIM, (pr + 1) * 2 * HEAD_DIM)


def _attn_fwd(qn, kn, vb, bias, sinks, ride=None):
    T = qn.shape[0]
    nb = T // QBLOCK

    def body(q_ref, kc_ref, kp_ref, vc_ref, vp_ref, b_ref, s_ref, o_ref, lse_ref, s_scr, p_scr):
        lane = lax.broadcasted_iota(jnp.int32, (QBLOCK, 2 * HEAD_DIM), 1)
        kx = _kv_placements(jnp.concatenate([kp_ref[...], kc_ref[...]], axis=0))
        vx = _kv_placements(jnp.concatenate([vp_ref[...], vc_ref[...]], axis=0))
        for hq in range(N_Q_HEADS):
            qm = _one_head(q_ref[:, _pair_cols(hq // 2)], hq % 2)
            s_scr[_head_rows(hq), :] = _nt(qm, kx[hq // GROUP, hq % 2]) + b_ref[_head_rows(hq), :]
        lse_tile = jnp.zeros((QBLOCK, 2 * HEAD_DIM), f32)
        for hq in range(N_Q_HEADS):
            s = s_scr[_head_rows(hq), :]
            sink = s_ref[0, hq]
            m = jnp.maximum(jnp.max(s, axis=-1, keepdims=True), sink)
            p = jnp.exp(s - m)
            l = jnp.sum(p, axis=-1, keepdims=True) + jnp.exp(sink - m)
            p_scr[_head_rows(hq), :] = (p * (1.0 / l)).astype(bf16)
            lse_tile = jnp.where(lane == hq, m + jnp.log(l), lse_tile)
        lse_ref[...] = lse_tile
        for pr in range(N_Q_HEADS // 2):
            h = 2 * pr // GROUP
            o_pair = (jnp.dot(p_scr[_head_rows(2 * pr), :], vx[h, 0], preferred_element_type=f32)
                      + jnp.dot(p_scr[_head_rows(2 * pr + 1), :], vx[h, 1], preferred_element_type=f32))
            o_ref[:, _pair_cols(pr)] = o_pair.astype(bf16)

    cur = lambda n: (n, 0)
    prev = lambda n: (jnp.maximum(n - 1, 0), 0)
    return _call(
        body, name="attn_fwd", grid=(nb,),
        in_specs=[pl.BlockSpec((QBLOCK, ATTN_WIDTH), cur), pl.BlockSpec((QBLOCK, KV_WIDTH), cur),
                  pl.BlockSpec((QBLOCK, KV_WIDTH), prev), pl.BlockSpec((QBLOCK, KV_WIDTH), cur),
                  pl.BlockSpec((QBLOCK, KV_WIDTH), prev), _bias_spec(), SMEM],
        out_specs=[pl.BlockSpec((QBLOCK, ATTN_WIDTH), cur), pl.BlockSpec((QBLOCK, 2 * HEAD_DIM), cur)],
        out_shape=[S((T, ATTN_WIDTH), bf16), S((T, 2 * HEAD_DIM), f32)],
        scratch_shapes=[pltpu.VMEM((N_Q_HEADS * QBLOCK, 2 * QBLOCK), f32), pltpu.VMEM((N_Q_HEADS * QBLOCK, 2 * QBLOCK), bf16)],
        args=(qn, kn, kn, vb, vb, bias, sinks), ride=ride)


def _layer_norm_stats(h1):
    mu = jnp.mean(h1, axis=-1, keepdims=True)
    xc = h1 - mu
    rstd = lax.rsqrt(jnp.mean(xc * xc, axis=-1, keepdims=True) + EPS)
    return xc * rstd, rstd


def _advanced_windows(win):
    rows = win.shape[0]
    for r in range(8):
        yield r, (win if r == 0 else pltpu.roll(win, rows - r, 0))


def _tap_offsets(r, rows):
    for q in range((rows - CONV_UNIT) // 8 + 1):
        if r == 0 or 8 * q + r + CONV_UNIT <= rows:
            yield q, 8 * q + r


def _conv_fwd(h0, w_dw, b_dw, ln_g, ln_b, tm, ride=None):
    T = h0.shape[0]
    per = tm // CONV_HALO
    lead = CONV_HALO - (CONV_WIDTH - 1)

    def body(hc_ref, hp_ref, w_ref, b_ref, g_ref, bb_ref, h1_ref, h3_ref, cat):
        i = pl.program_id(0)
        cat[0:CONV_HALO, :] = jnp.where(i == 0, 0.0, hp_ref[...])
        cat[CONV_HALO:, :] = hc_ref[...]

        def unit_rows(c, carry):
            r0 = pl.multiple_of(c * CONV_UNIT, CONV_UNIT)
            for j in range(D_MODEL // 128):
                ls = slice(j * 128, (j + 1) * 128)
                win = cat[pl.ds(r0, CONV_UNIT + CONV_HALO), ls]
                acc = jnp.zeros((CONV_UNIT, 128), f32) + b_ref[:, ls]
                for r, adv in _advanced_windows(win):
                    for q, off in _tap_offsets(r, CONV_UNIT + CONV_HALO):
                        k = off - lead
                        if 0 <= k < CONV_WIDTH:
                            acc = acc + adv[8 * q:8 * q + CONV_UNIT] * w_ref[k:k + 1, ls]
                h1_ref[pl.ds(r0, CONV_UNIT), ls] = acc
            return carry

        lax.fori_loop(0, tm // CONV_UNIT, unit_rows, 0)
        acc = h1_ref[...]
        xhat, _ = _layer_norm_stats(acc)
        h2 = xhat * g_ref[...] + bb_ref[...]
        h3_ref[...] = (h2 * _sigmoid(h2)).astype(bf16)

    vec = pl.BlockSpec((1, D_MODEL), lambda i: (0, 0))
    return _call(
        body, name="conv_fwd", grid=(T // tm,),
        in_specs=[pl.BlockSpec((tm, D_MODEL), lambda i: (i, 0)),
                  pl.BlockSpec((CONV_HALO, D_MODEL), lambda i: (jnp.maximum(i * per - 1, 0), 0)),
                  pl.BlockSpec((CONV_WIDTH, D_MODEL), lambda i: (0, 0)), vec, vec, vec],
        out_specs=[pl.BlockSpec((tm, D_MODEL), lambda i: (i, 0)), pl.BlockSpec((tm, D_MODEL), lambda i: (i, 0))],
        out_shape=[S((T, D_MODEL), f32), S((T, D_MODEL), bf16)],
        scratch_shapes=[pltpu.VMEM((tm + CONV_HALO, D_MODEL), f32)],
        args=(h0, h0, w_dw, b_dw, ln_g, ln_b), ride=ride)


def _mix_fwd(x, o, h3, proj, w_ao, w_co, w_o, tm):
    T = x.shape[0]
    row = pl.BlockSpec((tm, D_MODEL), lambda i: (i, 0))
    wsp = _resident((D_MODEL, D_MODEL))
    g0 = GLU_END // COL

    def gate_spec(off):
        return pl.BlockSpec((tm, COL), lambda i: (i, g0 + off))

    def body(x_ref, o_ref, h3_ref, ga0, ga1, gc0, gc1, wa_ref, wc_ref, wo_ref, x1_ref, at_ref, cv_ref, mg_ref):
        attn = jnp.dot(o_ref[...], wa_ref[...], preferred_element_type=f32)
        conv = jnp.dot(h3_ref[...], wc_ref[...], preferred_element_type=f32)
        ga = jnp.concatenate([ga0[...], ga1[...]], axis=-1).astype(f32)
        gc = jnp.concatenate([gc0[...], gc1[...]], axis=-1).astype(f32)
        merged = (_sigmoid(ga) * attn + _sigmoid(gc) * conv).astype(bf16)
        at_ref[...] = attn.astype(bf16)
        cv_ref[...] = conv.astype(bf16)
        mg_ref[...] = merged
        x1_ref[...] = x_ref[...] + jnp.dot(merged, wo_ref[...], preferred_element_type=f32)

    return pl.pallas_call(
        body, name="mix_fwd", grid=(T // tm,),
        in_specs=[row, row, row, gate_spec(0), gate_spec(1), gate_spec(2), gate_spec(3), wsp, wsp, wsp],
        out_specs=[row, row, row, row],
        out_shape=[S((T, D_MODEL), f32), S((T, D_MODEL), bf16), S((T, D_MODEL), bf16), S((T, D_MODEL), bf16)],
        compiler_params=_params("parallel"),
    )(x, o, h3, proj, proj, proj, proj, w_ao, w_co, w_o)


def _ffn_fwd(x1, g, w1, w2, target, tm):
    T = x1.shape[0]
    nj = w1.shape[0] // FF_CHUNK

    def body(x_ref, g_ref, w1_ref, w2_ref, t_ref, a_ref, u_ref, dy_ref, dyb_ref, ls_ref, hm):
        xv = x_ref[...]
        r = lax.rsqrt(jnp.mean(xv * xv, axis=-1, keepdims=True) + EPS)
        u = (xv * r * g_ref[...]).astype(bf16)
        u_ref[...] = u
        for j in range(nj):
            js = slice(j * FF_CHUNK, (j + 1) * FF_CHUNK)
            a = _nt(u, w1_ref[js, :])
            a_ref[:, js] = a.astype(bf16)
            hm[:, js] = jnp.square(jnp.maximum(a, 0.0)).astype(bf16)
        err = xv + jnp.dot(hm[...], w2_ref[...], preferred_element_type=f32) - t_ref[...]
        dy = err * (1.0 / D_MODEL)
        dy_ref[...] = dy
        dyb_ref[...] = dy.astype(bf16)
        ls_ref[...] = jnp.zeros((8, 128), f32) + jnp.sum(err * err) * (0.5 / D_MODEL)

    row = pl.BlockSpec((tm, D_MODEL), lambda i: (i, 0))
    wide = pl.BlockSpec((tm, D_FF), lambda i: (i, 0))
    return pl.pallas_call(
        body, name="ffn_fwd", grid=(T // tm,),
        in_specs=[row, _resident((1, D_MODEL)), _resident(w1.shape), _resident(w2.shape), row],
        out_specs=[wide, row, row, row, pl.BlockSpec((None, 8, 128), lambda i: (i, 0, 0))],
        out_shape=[S((T, D_FF), bf16), S((T, D_MODEL), bf16), S((T, D_MODEL), f32), S((T, D_MODEL), bf16),
                   S((T // tm, 8, 128), f32)],
        scratch_shapes=[pltpu.VMEM((tm, D_FF), bf16)],
        compiler_params=_params("parallel"),
    )(x1, g, w1, w2, target)


def _rms_bwd(du, xv, gv):
    r = lax.rsqrt(jnp.mean(xv * xv, axis=-1, keepdims=True) + EPS)
    xn = xv * r
    dg = jnp.sum(du * xn, axis=0, keepdims=True)
    dxn = du * gv
    dx = r * (dxn - xn * jnp.mean(dxn * xn, axis=-1, keepdims=True))
    return dx, dg


def _ffn_bwd(dy, dyb, a, x1, g, w1, w2, tm, ride=None):
    T = dy.shape[0]
    nj = w1.shape[0] // FF_CHUNK

    def body(dy_ref, dyb_ref, a_ref, x_ref, g_ref, w1_ref, w2_ref, da_ref, dx_ref, dxb_ref, dg_ref):
        @pl.when(pl.program_id(0) == 0)
        def _():
            dg_ref[...] = jnp.zeros_like(dg_ref)

        dyb_v = dyb_ref[...]
        for j in range(nj):
            js = slice(j * FF_CHUNK, (j + 1) * FF_CHUNK)
            dh = _nt(dyb_v, w2_ref[js, :])
            da_ref[:, js] = (dh * (2.0 * jnp.maximum(a_ref[:, js].astype(f32), 0.0))).astype(bf16)
        du = jnp.dot(da_ref[...], w1_ref[...], preferred_element_type=f32)
        dx, dg = _rms_bwd(du, x_ref[...], g_ref[...])
        dx1 = dy_ref[...] + dx
        dx_ref[...] = dx1
        dxb_ref[...] = dx1.astype(bf16)
        dg_ref[...] += dg

    row = pl.BlockSpec((tm, D_MODEL), lambda i: (i, 0))
    wide = pl.BlockSpec((tm, D_FF), lambda i: (i, 0))
    vec = pl.BlockSpec((1, D_MODEL), lambda i: (0, 0))
    return _call(
        body, name="ffn_bwd", grid=(T // tm,),
        in_specs=[row, row, wide, row, _resident((1, D_MODEL)), _resident(w1.shape), _resident(w2.shape)],
        out_specs=[wide, row, row, vec],
        out_shape=[S((T, D_FF), bf16), S((T, D_MODEL), f32), S((T, D_MODEL), bf16), S((1, D_MODEL), f32)],
        args=(dy, dyb, a, x1, g, w1, w2), ride=ride)


def _wgrad(name, a, b, tk, tn, tt, relu2=False, slab=None, out_dtype=bf16):
    T, Ka = a.shape
    Nb = b.shape[1]
    nt = T // tt

    def body(a_ref, b_ref, o_ref, acc):
        t = pl.program_id(2)
        av = a_ref[...]
        if relu2:
            av = jnp.square(jnp.maximum(av.astype(f32), 0.0))
        prod = _tn(av.astype(bf16), b_ref[...].astype(bf16))

        @pl.when(t == 0)
        def _():
            acc[...] = prod

        @pl.when(t > 0)
        def _():
            acc[...] += prod

        @pl.when(t == nt - 1)
        def _():
            if slab is None:
                o_ref[...] = acc[...].astype(out_dtype)
            else:
                for s in range(tn // slab):
                    o_ref[s] = acc[:, s * slab:(s + 1) * slab].astype(out_dtype)

    if slab is not None:
        out_shape = S((Nb // slab, Ka, slab), out_dtype)
        out_spec = pl.BlockSpec((tn // slab, tk, slab), lambda i, j, t: (j, i, 0))
    else:
        out_shape = S((Ka, Nb), out_dtype)
        out_spec = pl.BlockSpec((tk, tn), lambda i, j, t: (i, j))
    return pl.pallas_call(
        body, name=name, grid=(Ka // tk, Nb // tn, nt),
        in_specs=[pl.BlockSpec((tt, tk), lambda i, j, t: (t, i)), pl.BlockSpec((tt, tn), lambda i, j, t: (t, j))],
        out_specs=out_spec, out_shape=out_shape, scratch_shapes=[pltpu.VMEM((tk, tn), f32)],
        compiler_params=_params("parallel", "parallel", "arbitrary"),
    )(a, b)


def _mix_bwd(dx1, proj, attn, conv, h1, ln_g, ln_b, w_ao, w_co, w_o, tm, ride=None):
    T = dx1.shape[0]
    g0 = GLU_END // COL

    def gate_spec(off):
        return pl.BlockSpec((tm, COL), lambda i: (i, g0 + off))

    def body(dx_ref, ga0, ga1, gc0, gc1, at_ref, cv_ref, h_ref, g_ref, b_ref, wa_ref, wc_ref, wo_ref,
             da_ref, dc_ref, do_ref, dh1_ref, dg_ref, dlg_ref, dlb_ref, dbd_ref):
        @pl.when(pl.program_id(0) == 0)
        def _():
            dlg_ref[...] = jnp.zeros_like(dlg_ref)
            dlb_ref[...] = jnp.zeros_like(dlb_ref)
            dbd_ref[...] = jnp.zeros_like(dbd_ref)

        dm = _nt(dx_ref[...].astype(bf16), wo_ref[...])
        sa = _sigmoid(jnp.concatenate([ga0[...], ga1[...]], axis=-1).astype(f32))
        sc = _sigmoid(jnp.concatenate([gc0[...], gc1[...]], axis=-1).astype(f32))
        dattn = (dm * sa).astype(bf16)
        dconv = (dm * sc).astype(bf16)
        da_ref[...] = dattn
        dc_ref[...] = dconv
        dg_ref[:, 0:D_MODEL] = (dm * at_ref[...].astype(f32) * sa * (1.0 - sa)).astype(bf16)
        dg_ref[:, D_MODEL:2 * D_MODEL] = (dm * cv_ref[...].astype(f32) * sc * (1.0 - sc)).astype(bf16)
        do_ref[...] = _nt(dattn, wa_ref[...]).astype(bf16)
        dh3 = _nt(dconv, wc_ref[...])
        xhat, rstd = _layer_norm_stats(h_ref[...])
        h2 = xhat * g_ref[...] + b_ref[...]
        sg = _sigmoid(h2)
        dh2 = dh3 * (sg * (1.0 + h2 * (1.0 - sg)))
        dlg_ref[...] += jnp.sum(dh2 * xhat, axis=0, keepdims=True)
        dlb_ref[...] += jnp.sum(dh2, axis=0, keepdims=True)
        dxh = dh2 * g_ref[...]
        dh1 = rstd * (dxh - jnp.mean(dxh, axis=-1, keepdims=True) - xhat * jnp.mean(dxh * xhat, axis=-1, keepdims=True))
        dh1_ref[...] = dh1
        dbd_ref[...] += jnp.sum(dh1, axis=0, keepdims=True)

    row = pl.BlockSpec((tm, D_MODEL), lambda i: (i, 0))
    vec = pl.BlockSpec((1, D_MODEL), lambda i: (0, 0))
    par = _resident((1, D_MODEL))
    wsp = _resident((D_MODEL, D_MODEL))
    return _call(
        body, name="mix_bwd", grid=(T // tm,),
        in_specs=[row, gate_spec(0), gate_spec(1), gate_spec(2), gate_spec(3), row, row, row, par, par, wsp, wsp, wsp],
        out_specs=[row, row, row, row, pl.BlockSpec((tm, 2 * D_MODEL), lambda i: (i, 0)), vec, vec, vec],
        out_shape=[S((T, D_MODEL), bf16), S((T, D_MODEL), bf16), S((T, D_MODEL), bf16), S((T, D_MODEL), f32),
                   S((T, 2 * D_MODEL), bf16), S((1, D_MODEL), f32), S((1, D_MODEL), f32), S((1, D_MODEL), f32)],
        args=(dx1, proj, proj, proj, proj, attn, conv, h1, ln_g, ln_b, w_ao, w_co, w_o), ride=ride)


def _conv_bwd(dh1, h0, proj, w_dw, tm, ride=None):
    T = dh1.shape[0]
    per = tm // CONV_HALO
    nh = T // CONV_HALO
    nt = T // tm
    a0 = V_END // COL
    lead = CONV_HALO - (CONV_WIDTH - 1)

    def body(dc_ref, dn_ref, hc_ref, hp_ref, a0_ref, a1_ref, g0_ref, g1_ref, w_ref, dglu_ref, dw_ref, dcat, hcat, wacc, dh0):
        i = pl.program_id(0)

        @pl.when(i == 0)
        def _():
            wacc[...] = jnp.zeros_like(wacc)

        dcat[0:tm, :] = dc_ref[...]
        dcat[tm:, :] = jnp.where(i == nt - 1, 0.0, dn_ref[...])
        hcat[0:CONV_HALO, :] = jnp.where(i == 0, 0.0, hp_ref[...])
        hcat[CONV_HALO:, :] = hc_ref[...]
        span = CONV_UNIT + CONV_HALO

        def unit_rows(c, carry):
            r0 = pl.multiple_of(c * CONV_UNIT, CONV_UNIT)
            for j in range(D_MODEL // 128):
                ls = slice(j * 128, (j + 1) * 128)
                dwin = dcat[pl.ds(r0, span), ls]
                acc = jnp.zeros((CONV_UNIT, 128), f32)
                for r, adv in _advanced_windows(dwin):
                    for q, off in _tap_offsets(r, span):
                        k = CONV_WIDTH - 1 - off
                        if 0 <= k < CONV_WIDTH:
                            acc = acc + adv[8 * q:8 * q + CONV_UNIT] * w_ref[k:k + 1, ls]
                dh0[pl.ds(r0, CONV_UNIT), ls] = acc
                dcur = dwin[0:CONV_UNIT]
                for r, adv in _advanced_windows(hcat[pl.ds(r0, span), ls]):
                    for q, off in _tap_offsets(r, span):
                        k = off - lead
                        if 0 <= k < CONV_WIDTH:
                            prod = dcur * adv[8 * q:8 * q + CONV_UNIT]
                            wacc[k, :, ls] += jnp.sum(prod.reshape(CONV_UNIT // 8, 8, 128), axis=0)
            return carry

        lax.fori_loop(0, tm // CONV_UNIT, unit_rows, 0)
        dh0v = dh0[...]
        av = jnp.concatenate([a0_ref[...], a1_ref[...]], axis=-1).astype(f32)
        sg = _sigmoid(jnp.concatenate([g0_ref[...], g1_ref[...]], axis=-1).astype(f32))
        dglu_ref[:, 0:D_MODEL] = (dh0v * sg).astype(bf16)
        dglu_ref[:, D_MODEL:2 * D_MODEL] = (dh0v * av * sg * (1.0 - sg)).astype(bf16)

        @pl.when(i == nt - 1)
        def _():
            for k in range(CONV_WIDTH):
                dw_ref[k:k + 1, :] = jnp.sum(wacc[k], axis=0, keepdims=True)
            dw_ref[CONV_WIDTH:CONV_WIDTH + 1, :] = jnp.zeros((1, D_MODEL), f32)

    row = pl.BlockSpec((tm, D_MODEL), lambda i: (i, 0))

    def col_spec(off):
        return pl.BlockSpec((tm, COL), lambda i: (i, a0 + off))

    return _call(
        body, name="conv_bwd", grid=(nt,),
        in_specs=[row, pl.BlockSpec((CONV_HALO, D_MODEL), lambda i: (jnp.minimum((i + 1) * per, nh - 1), 0)),
                  row, pl.BlockSpec((CONV_HALO, D_MODEL), lambda i: (jnp.maximum(i * per - 1, 0), 0)),
                  col_spec(0), col_spec(1), col_spec(2), col_spec(3),
                  pl.BlockSpec((CONV_WIDTH, D_MODEL), lambda i: (0, 0))],
        out_specs=[pl.BlockSpec((tm, 2 * D_MODEL), lambda i: (i, 0)), pl.BlockSpec((CONV_WIDTH + 1, D_MODEL), lambda i: (0, 0))],
        out_shape=[S((T, 2 * D_MODEL), bf16), S((CONV_WIDTH + 1, D_MODEL), f32)],
        scratch_shapes=[pltpu.VMEM((tm + CONV_HALO, D_MODEL), f32), pltpu.VMEM((tm + CONV_HALO, D_MODEL), f32),
                        pltpu.VMEM((CONV_WIDTH, 8, D_MODEL), f32), pltpu.VMEM((tm, D_MODEL), f32)],
        args=(dh1, dh1, h0, h0, proj, proj, proj, proj, w_dw), ride=ride)


def _attn_bwd(qn, kn, vb, o, do, lse, bias, sinks, ride=None):
    T = qn.shape[0]
    nb = T // QBLOCK

    def body(q_ref, kc_ref, kp_ref, vc_ref, vp_ref, o_ref, do_ref, lse_ref, b_ref, s_ref,
             dq_ref, dk_ref, dv_ref, db_ref, dsk_ref, kcar, vcar, s_scr, dp_scr, p_scr, ds_scr):
        n = pl.program_id(0)

        @pl.when(n == 0)
        def _():
            db_ref[...] = jnp.zeros_like(db_ref)
            dsk_ref[...] = jnp.zeros_like(dsk_ref)
            kcar[...] = jnp.zeros_like(kcar)
            vcar[...] = jnp.zeros_like(vcar)

        @pl.when(n < nb)
        def _():
            lane = lax.broadcasted_iota(jnp.int32, (QBLOCK, 2 * HEAD_DIM), 1)
            lane_row = lax.broadcasted_iota(jnp.int32, (1, 2 * HEAD_DIM), 1)
            kx = _kv_placements(jnp.concatenate([kp_ref[...], kc_ref[...]], axis=0))
            vx = _kv_placements(jnp.concatenate([vp_ref[...], vc_ref[...]], axis=0))
            lse_tile = lse_ref[...]
            delta, lse_c = {}, {}
            for pr in range(N_Q_HEADS // 2):
                dop = do_ref[:, _pair_cols(pr)]
                dl = dop.astype(f32) * o_ref[:, _pair_cols(pr)].astype(f32)
                for side in range(2):
                    hq = 2 * pr + side
                    h = hq // GROUP
                    qm = _one_head(q_ref[:, _pair_cols(pr)], side)
                    s_scr[_head_rows(hq), :] = _nt(qm, kx[h, side]) + b_ref[_head_rows(hq), :]
                    dp_scr[_head_rows(hq), :] = _nt(_one_head(dop, side), vx[h, side])
                    delta[hq] = jnp.sum(_one_head(dl, side), axis=-1, keepdims=True)
                    lse_c[hq] = jnp.sum(jnp.where(lane == hq, lse_tile, 0.0), axis=-1, keepdims=True)
            dsk = jnp.zeros((1, 2 * HEAD_DIM), f32)
            for hq in range(N_Q_HEADS):
                p = jnp.exp(s_scr[_head_rows(hq), :] - lse_c[hq])
                ds = p * (dp_scr[_head_rows(hq), :] - delta[hq])
                db_ref[_head_rows(hq), :] += ds
                p_scr[_head_rows(hq), :] = p.astype(bf16)
                ds_scr[_head_rows(hq), :] = ds.astype(bf16)
                psink = jnp.exp(s_ref[0, hq] - lse_c[hq])
                dsk = dsk - jnp.where(lane_row == hq, jnp.sum(psink * delta[hq], axis=0, keepdims=True), 0.0)
            dsk_ref[...] += dsk
            for pr in range(N_Q_HEADS // 2):
                h = 2 * pr // GROUP
                dq_ref[:, _pair_cols(pr)] = (jnp.dot(ds_scr[_head_rows(2 * pr), :], kx[h, 0], preferred_element_type=f32)
                                             + jnp.dot(ds_scr[_head_rows(2 * pr + 1), :], kx[h, 1], preferred_element_type=f32))
            folded_k, folded_v = [], []
            for h in range(N_KV_HEADS):
                ka = jnp.zeros((2 * QBLOCK, 2 * HEAD_DIM), f32)
                va = jnp.zeros((2 * QBLOCK, 2 * HEAD_DIM), f32)
                for g in range(GROUP):
                    hq = h * GROUP + g
                    ka = ka + _tn(ds_scr[_head_rows(hq), :], _one_head(q_ref[:, _pair_cols(hq // 2)], hq % 2))
                    va = va + _tn(p_scr[_head_rows(hq), :], _one_head(do_ref[:, _pair_cols(hq // 2)], hq % 2))
                folded_k.append(ka + _swap_halves(ka))
                folded_v.append(va + _swap_halves(va))
            low = _low_lanes()
            for m in range(N_KV_HEADS // 2):
                cs = _pair_cols(m)
                for folded, out_ref, car in ((folded_k, dk_ref, kcar), (folded_v, dv_ref, vcar)):
                    band = jnp.where(low, folded[2 * m], folded[2 * m + 1])
                    out_ref[:, cs] = car[:, cs] + band[0:QBLOCK, :]
                    car[:, cs] = band[QBLOCK:, :]

        @pl.when(n == nb)
        def _():
            dk_ref[...] = kcar[...]
            dv_ref[...] = vcar[...]

    cur = lambda n: (jnp.minimum(n, nb - 1), 0)
    prev = lambda n: (jnp.clip(n - 1, 0, nb - 1), 0)
    qspec = pl.BlockSpec((QBLOCK, ATTN_WIDTH), cur)
    kcur, kprev = pl.BlockSpec((QBLOCK, KV_WIDTH), cur), pl.BlockSpec((QBLOCK, KV_WIDTH), prev)
    whole = lambda shape: pl.BlockSpec(shape, lambda n: (0,) * len(shape))
    scores = (N_Q_HEADS * QBLOCK, 2 * QBLOCK)
    return _call(
        body, name="attn_bwd", grid=(nb + 1,),
        in_specs=[qspec, kcur, kprev, kcur, kprev, qspec, qspec, pl.BlockSpec((QBLOCK, 2 * HEAD_DIM), cur), _bias_spec(), SMEM],
        out_specs=[qspec, kprev, kprev, whole(scores), whole((1, 2 * HEAD_DIM))],
        out_shape=[S((T, ATTN_WIDTH), f32), S((T, KV_WIDTH), f32), S((T, KV_WIDTH), f32), S(scores, f32),
                   S((1, 2 * HEAD_DIM), f32)],
        scratch_shapes=[pltpu.VMEM((QBLOCK, KV_WIDTH), f32), pltpu.VMEM((QBLOCK, KV_WIDTH), f32),
                        pltpu.VMEM(scores, f32), pltpu.VMEM(scores, f32), pltpu.VMEM(scores, bf16), pltpu.VMEM(scores, bf16)],
        args=(qn, kn, kn, vb, vb, o, do, lse, bias, sinks), ride=ride)


def _rel_bias_bwd(dbias, bucket):
    def body(d_ref, bk_ref, o_ref):
        b = bk_ref[...]
        for k in range(N_BUCKETS):
            mk = b == k
            for h in range(N_Q_HEADS):
                o_ref[k, h] = jnp.sum(jnp.where(mk, d_ref[h * QBLOCK:(h + 1) * QBLOCK, :], 0.0))

    return pl.pallas_call(body, name="rel_bias_bwd", out_shape=S((N_BUCKETS, N_Q_HEADS), f32), out_specs=SMEM)(dbias, bucket)


def _qk_norm_bwd(dq, dk, dv, proj, qg, kg, tm):
    T = dq.shape[0]
    scale = HEAD_DIM ** -0.5

    def pair_bwd(dy, x, gv):
        r = _pair_rstd(x, True)
        xn = x * r
        dxn = dy * gv
        dx = r * (dxn - xn * _pair_mean(dxn * xn, True))
        return dx, jnp.sum(dy * xn, axis=0, keepdims=True)

    def body(dq_ref, dk_ref, dv_ref, p_ref, qg_ref, kg_ref, out_ref, dqg_ref, dkg_ref):
        @pl.when(pl.program_id(0) == 0)
        def _():
            dqg_ref[...] = jnp.zeros_like(dqg_ref)
            dkg_ref[...] = jnp.zeros_like(dkg_ref)

        qgv, kgv = qg_ref[...], kg_ref[...]
        dqg = jnp.zeros((1, 2 * HEAD_DIM), f32)
        for pr in range(N_Q_HEADS // 2):
            dx, dg = pair_bwd(dq_ref[:, _pair_cols(pr)] * scale, p_ref[:, _pair_cols(pr)].astype(f32), qgv)
            out_ref[:, _pair_cols(pr)] = dx.astype(bf16)
            dqg = dqg + dg
        dkg = jnp.zeros((1, 2 * HEAD_DIM), f32)
        for pr in range(N_KV_HEADS // 2):
            ps = slice(Q_END + pr * 2 * HEAD_DIM, Q_END + (pr + 1) * 2 * HEAD_DIM)
            dx, dg = pair_bwd(dk_ref[:, _pair_cols(pr)], p_ref[:, ps].astype(f32), kgv)
            out_ref[:, ps] = dx.astype(bf16)
            dkg = dkg + dg
        out_ref[:, K_END:V_END] = dv_ref[...].astype(bf16)
        dqg_ref[...] += dqg
        dkg_ref[...] += dkg

    vec = pl.BlockSpec((1, 2 * HEAD_DIM), lambda i: (0, 0))
    return pl.pallas_call(
        body, name="qk_norm_bwd", grid=(T // tm,),
        in_specs=[pl.BlockSpec((tm, ATTN_WIDTH), lambda i: (i, 0)), pl.BlockSpec((tm, KV_WIDTH), lambda i: (i, 0)),
                  pl.BlockSpec((tm, KV_WIDTH), lambda i: (i, 0)), pl.BlockSpec((tm, V_END), lambda i: (i, 0)), vec, vec],
        out_specs=[pl.BlockSpec((tm, V_END), lambda i: (i, 0)), vec, vec],
        out_shape=[S((T, V_END), bf16), S((1, 2 * HEAD_DIM), f32), S((1, 2 * HEAD_DIM), f32)],
        compiler_params=_params("arbitrary"),
    )(dq, dk, dv, proj, qg, kg)


def _in_bwd(dqkv, dglu, dgates, w_in, x, g, dx1, tm, ride=None):
    T = x.shape[0]
    pieces = (dqkv, dglu, dgates)
    starts = [0, dqkv.shape[1], dqkv.shape[1] + dglu.shape[1]]

    def body(a0_ref, a1_ref, a2_ref, w_ref, x_ref, g_ref, d_ref, gx_ref, dg_ref):
        @pl.when(pl.program_id(0) == 0)
        def _():
            dg_ref[...] = jnp.zeros_like(dg_ref)

        du = jnp.zeros((tm, D_MODEL), f32)
        for a_ref, c0 in zip((a0_ref, a1_ref, a2_ref), starts):
            du = du + _nt(a_ref[...], w_ref[:, c0:c0 + a_ref.shape[1]])
        dx, dg = _rms_bwd(du, x_ref[...], g_ref[...])
        gx_ref[...] = d_ref[...] + dx
        dg_ref[...] += dg

    row = pl.BlockSpec((tm, D_MODEL), lambda i: (i, 0))
    return _call(
        body, name="in_bwd", grid=(T // tm,),
        in_specs=[pl.BlockSpec((tm, p.shape[1]), lambda i: (i, 0)) for p in pieces]
        + [_resident(w_in.shape), row, _resident((1, D_MODEL)), row],
        out_specs=[row, pl.BlockSpec((1, D_MODEL), lambda i: (0, 0))],
        out_shape=[S((T, D_MODEL), f32), S((1, D_MODEL), f32)],
        args=(dqkv, dglu, dgates, w_in, x, g, dx1), ride=ride)


def _adamw(name, parts, w, m, v, tr):
    _, R, C = w.shape
    bc1 = 1.0 - ADAM_B1 ** ADAM_STEP
    bc2 = 1.0 - ADAM_B2 ** ADAM_STEP

    def body(p_ref, w_ref, m_ref, v_ref, g_ref, d_ref, nm_ref, nv_ref):
        g = p_ref[0].astype(f32)
        for k in range(1, N_DEV):
            g = g + p_ref[k].astype(f32)
        nm = ADAM_B1 * m_ref[...] + (1.0 - ADAM_B1) * g
        nv = ADAM_B2 * v_ref[...] + (1.0 - ADAM_B2) * (g * g)
        g_ref[...] = g
        nm_ref[...] = nm
        nv_ref[...] = nv
        d_ref[...] = -ADAM_LR * ((nm / bc1) / (jnp.sqrt(nv / bc2) + ADAM_EPS) + ADAM_WD * w_ref[...])

    blk = pl.BlockSpec((None, tr, C), lambda i: (0, i, 0))
    return pl.pallas_call(
        body, name=name, grid=(R // tr,),
        in_specs=[pl.BlockSpec((N_DEV, tr, C), lambda i: (0, i, 0)), blk, blk, blk],
        out_specs=[blk, blk, blk, blk], out_shape=[S((1, R, C), f32)] * 4,
        compiler_params=_params("parallel"),
    )(parts, w, m, v)


def _tile(T, pref):
    return min(T, pref)


def _pad_rows(a, rows):
    return jnp.pad(a, ((0, rows - a.shape[0]), (0, 0)))


def kernel(x, norm_mix_g, w_in, q_norm_g, k_norm_g, attn_sinks, rel_bias, w_attn_o, w_dw, b_dw, conv_ln_g, conv_ln_b, w_conv_out, w_out, norm_mlp_g, w_ff1, w_ff2, loss_target, m_norm_mix_g, m_w_in, m_q_norm_g, m_k_norm_g, m_attn_sinks, m_rel_bias, m_w_attn_o, m_w_dw, m_b_dw, m_conv_ln_g, m_conv_ln_b, m_w_conv_out, m_w_out, m_norm_mlp_g, m_w_ff1, m_w_ff2, v_norm_mix_g, v_w_in, v_q_norm_g, v_k_norm_g, v_attn_sinks, v_rel_bias, v_w_attn_o, v_w_dw, v_b_dw, v_conv_ln_g, v_conv_ln_b, v_w_conv_out, v_w_out, v_norm_mlp_g, v_w_ff1, v_w_ff2):
    T = x.shape[1]
    xs = x[0]
    tgt = loss_target[0]
    in_shard = IN_WIDTH // N_DEV
    dw_rows = CONV_WIDTH + 1
    ch_shard = D_MODEL // N_DEV
    tc = _tile(T, 256)
    tb = _tile(T, 512)
    tt = _tile(T, 2048)
    bucket = jnp.asarray(_t5_bucket_table())

    g_in, g_dw = _exchange("gather_w_in", [w_in[0].astype(bf16), _pad_rows(w_dw[0], dw_rows)], gather=True, two_level=True)
    W_in = jnp.transpose(g_in, (1, 0, 2)).reshape(D_MODEL, IN_WIDTH)
    W_dw = jnp.transpose(g_dw, (1, 0, 2)).reshape(dw_rows, D_MODEL)[:CONV_WIDTH]

    mix_shards = _Gather([w_attn_o[0].astype(bf16), w_conv_out[0].astype(bf16), w_out[0].astype(bf16)])
    qg2, kg2 = jnp.tile(q_norm_g, (1, 2)), jnp.tile(k_norm_g, (1, 2))
    (proj, u, qn, kn, vb, h0), (g_ao, g_co, g_o) = _proj_fwd(xs, norm_mix_g, W_in, qg2, kg2, tb, ride=mix_shards)
    W_ao = g_ao.reshape(D_MODEL, D_MODEL)
    W_co = g_co.reshape(D_MODEL, D_MODEL)
    W_o = g_o.reshape(D_MODEL, D_MODEL)
    bias = _bias_table(rel_bias, bucket)
    (o, lse), (g_f1,) = _attn_fwd(qn, kn, vb, bias, attn_sinks, ride=_Gather([w_ff1[0].astype(bf16).T]))
    W_f1t = g_f1.reshape(D_FF, D_MODEL)
    (h1, h3), (g_f2,) = _conv_fwd(h0, W_dw, b_dw, conv_ln_g, conv_ln_b, tc, ride=_Gather([w_ff2[0].astype(bf16)]))
    x1, attn, conv, merged = _mix_fwd(xs, o, h3, proj, W_ao, W_co, W_o, tb)
    W_f2 = g_f2.reshape(D_FF, D_MODEL)
    a, u2, dy, dyb, loss_parts = _ffn_fwd(x1, norm_mlp_g, W_f1t, W_f2, tgt, tb)
    loss = lax.psum(jnp.sum(loss_parts[:, 0, 0]), ("x", "y", "c"))

    gw_f2 = _wgrad("wgrad_ff2", a, dyb, D_MODEL, D_MODEL, tt, relu2=True).reshape(N_DEV, FF_CHUNK, D_MODEL)
    (da, dx1, dx1b, d_norm_mlp_g), (l_f2,) = _ffn_bwd(dy, dyb, a, x1, norm_mlp_g, W_f1t, W_f2, tc,
                                                      ride=_Exchange([gw_f2], gather=False))
    gw_f1 = _wgrad("wgrad_ff1", u2, da, D_MODEL, 2 * FF_CHUNK, tt, slab=FF_CHUNK)
    gw_o = _wgrad("wgrad_out", merged, dx1b, D_MODEL, D_MODEL, tt).reshape(N_DEV, ch_shard, D_MODEL)
    (dattn, dconv, do, dh1, dgates, d_ln_g, d_ln_b, d_b_dw), (l_o,) = _mix_bwd(
        dx1b, proj, attn, conv, h1, conv_ln_g, conv_ln_b, W_ao, W_co, W_o, tb, ride=_Exchange([gw_o], gather=False))
    gw_ao = _wgrad("wgrad_attn_o", o, dattn, D_MODEL, D_MODEL, tt).reshape(N_DEV, ch_shard, D_MODEL)
    gw_co = _wgrad("wgrad_conv_out", h3, dconv, D_MODEL, D_MODEL, tt).reshape(N_DEV, ch_shard, D_MODEL)
    (dglu, d_w_dw), (l_f1, l_ao, l_co) = _conv_bwd(dh1, h0, proj, W_dw, tc,
                                                   ride=_Exchange([gw_f1, gw_ao, gw_co], gather=False))
    (dq, dk, dv, dbias, d_sinks), _ = _attn_bwd(qn, kn, vb, o, do, lse, bias, attn_sinks)
    d_sinks = d_sinks[:, :N_Q_HEADS]
    d_rel_bias = _rel_bias_bwd(dbias, bucket)
    dqkv, d_qg, d_kg = _qk_norm_bwd(dq, dk, dv, proj, qg2, kg2, tb)
    d_qg = d_qg[:, :HEAD_DIM] + d_qg[:, HEAD_DIM:]
    d_kg = d_kg[:, :HEAD_DIM] + d_kg[:, HEAD_DIM:]
    gw_in = jnp.concatenate([_wgrad("wgrad_in_qkv", u, dqkv, D_MODEL, V_END, tt),
                             _wgrad("wgrad_in_glu", u, dglu, D_MODEL, D_MODEL, tt),
                             _wgrad("wgrad_in_gates", u, dgates, D_MODEL, D_MODEL, tt)], axis=1)
    gw_in = jnp.transpose(gw_in.reshape(D_MODEL, N_DEV, in_shard), (1, 0, 2))
    gw_dw = jnp.transpose(d_w_dw.reshape(dw_rows, N_DEV, ch_shard), (1, 0, 2))
    (grad_x, d_norm_mix_g), (l_in, l_dw) = _in_bwd(dqkv, dglu, dgates, W_in, xs, norm_mix_g, dx1, tb,
                                                    ride=_Exchange([gw_in, gw_dw], gather=False))

    def row(vec):
        flat = vec.reshape(1, -1)
        return jnp.pad(flat, ((0, 0), (0, D_MODEL - flat.shape[1])))

    def pack_small(nm, qg, kg, sk, rb, bd, lg, lb, nl):
        tail = jnp.concatenate([qg.reshape(1, -1), kg.reshape(1, -1), sk.reshape(1, -1), rb.reshape(1, -1)], axis=1)
        return jnp.concatenate([row(nm), row(bd), row(lg), row(lb), row(nl), row(tail), jnp.zeros((2, D_MODEL), f32)], axis=0)

    def unpack_small(p):
        t = p[5]
        o0, o1, o2 = HEAD_DIM, 2 * HEAD_DIM, 2 * HEAD_DIM + N_Q_HEADS
        return dict(norm_mix_g=p[0:1], b_dw=p[1:2], conv_ln_g=p[2:3], conv_ln_b=p[3:4], norm_mlp_g=p[4:5],
                    q_norm_g=t[0:o0].reshape(1, HEAD_DIM), k_norm_g=t[o0:o1].reshape(1, HEAD_DIM),
                    attn_sinks=t[o1:o2].reshape(1, N_Q_HEADS),
                    rel_bias=t[o2:o2 + N_BUCKETS * N_Q_HEADS].reshape(N_BUCKETS, N_Q_HEADS))

    small_g = pack_small(d_norm_mix_g, d_qg, d_kg, d_sinks, d_rel_bias, d_b_dw, d_ln_g, d_ln_b, d_norm_mlp_g)
    (l_small,) = _exchange("gather_small_grads", [small_g], gather=True)


    res = {}
    res["w_in"] = _adamw("adamw_in", l_in, w_in, m_w_in, v_w_in, 256)
    res["w_attn_o"] = _adamw("adamw_attn_o", l_ao, w_attn_o, m_w_attn_o, v_w_attn_o, ch_shard)
    res["w_conv_out"] = _adamw("adamw_conv_out", l_co, w_conv_out, m_w_conv_out, v_w_conv_out, ch_shard)
    res["w_out"] = _adamw("adamw_out", l_o, w_out, m_w_out, v_w_out, ch_shard)
    res["w_ff1"] = _adamw("adamw_ff1", l_f1, w_ff1, m_w_ff1, v_w_ff1, 256)
    res["w_ff2"] = _adamw("adamw_ff2", l_f2, w_ff2, m_w_ff2, v_w_ff2, 256)
    pad_dw = lambda t: _pad_rows(t[0], dw_rows)[None]
    res["w_dw"] = [t[:, :CONV_WIDTH] for t in _adamw("adamw_dw", l_dw, pad_dw(w_dw), pad_dw(m_w_dw), pad_dw(v_w_dw), dw_rows)]
    small_w = pack_small(norm_mix_g, q_norm_g, k_norm_g, attn_sinks, rel_bias, b_dw, conv_ln_g, conv_ln_b, norm_mlp_g)
    small_m = pack_small(m_norm_mix_g, m_q_norm_g, m_k_norm_g, m_attn_sinks, m_rel_bias, m_b_dw, m_conv_ln_g, m_conv_ln_b, m_norm_mlp_g)
    small_v = pack_small(v_norm_mix_g, v_q_norm_g, v_k_norm_g, v_attn_sinks, v_rel_bias, v_b_dw, v_conv_ln_g, v_conv_ln_b, v_norm_mlp_g)
    small4 = [unpack_small(t[0]) for t in _adamw("adamw_small", l_small, small_w[None], small_m[None], small_v[None], 8)]

    order = ["norm_mix_g", "w_in", "q_norm_g", "k_norm_g", "attn_sinks", "rel_bias", "w_attn_o", "w_dw", "b_dw",
             "conv_ln_g", "conv_ln_b", "w_conv_out", "w_out", "norm_mlp_g", "w_ff1", "w_ff2"]
    stacked = {"w_in", "w_attn_o", "w_dw", "w_conv_out", "w_out", "w_ff1", "w_ff2"}
    outs = [loss, grad_x[None]]
    for k in range(4):
        for nme in order:
            if nme in stacked:
                outs.append(res[nme][k])
            else:
                outs.append(small4[k][nme])
    return tuple(outs)
```

```python
import functools

import numpy as np
import jax
import jax.numpy as jnp
from jax import lax
from jax.experimental import pallas as pl
from jax.experimental.pallas import tpu as pltpu

f32 = jnp.float32
bf16 = jnp.bfloat16
S = jax.ShapeDtypeStruct

N_DEV = 8
D_MODEL = 1024
HEAD_DIM = 64
N_Q_HEADS = 16
N_KV_HEADS = 4
GROUP = N_Q_HEADS // N_KV_HEADS
ATTN_WIDTH = N_Q_HEADS * HEAD_DIM
KV_WIDTH = N_KV_HEADS * HEAD_DIM
QBLOCK = 128
CONV_WIDTH = 31
CONV_HALO = 32
CONV_UNIT = 64
D_FF = 4 * D_MODEL
N_BUCKETS = 32
MAX_DISTANCE = 128
EPS = 1e-6
NEG = -1e30
Q_END = ATTN_WIDTH
K_END = Q_END + KV_WIDTH
V_END = K_END + KV_WIDTH
GLU_END = V_END + 2 * D_MODEL
IN_WIDTH = GLU_END + 2 * D_MODEL
COL = 512
FF_CHUNK = D_FF // N_DEV

ADAM_LR = 0.001
ADAM_B1 = 0.9
ADAM_B2 = 0.999
ADAM_EPS = 1e-08
ADAM_WD = 0.01
ADAM_STEP = 10

VMEM_LIMIT = 56 * 1024 * 1024

MESH_ID = pl.DeviceIdType.MESH
ANY = pl.BlockSpec(memory_space=pl.ANY)
SMEM = pl.BlockSpec(memory_space=pltpu.SMEM)


def _params(*sem):
    return pltpu.CompilerParams(dimension_semantics=sem, vmem_limit_bytes=VMEM_LIMIT)


def _nt(a, b):
    return lax.dot_general(a, b, (((1,), (1,)), ((), ())), preferred_element_type=f32)


def _tn(a, b):
    return lax.dot_general(a, b, (((0,), (0,)), ((), ())), preferred_element_type=f32)


def _sigmoid(z):
    return 1.0 / (1.0 + jnp.exp(-z))


def _t5_bucket_table():
    qi = np.arange(QBLOCK, dtype=np.int32)[:, None]
    kj = np.arange(2 * QBLOCK, dtype=np.int32)[None, :]
    dist = qi + QBLOCK - kj
    n = np.maximum(dist, 0)
    max_exact = N_BUCKETS // 2
    nf = np.maximum(n, 1).astype(np.float32)
    large = max_exact + (np.log(nf / np.float32(max_exact)) / np.float32(np.log(MAX_DISTANCE / max_exact))
                         * np.float32(N_BUCKETS - max_exact)).astype(np.int32)
    large = np.minimum(large, N_BUCKETS - 1)
    bucket = np.where(n < max_exact, n, large)
    valid = (dist >= 0) & (dist < QBLOCK)
    return np.where(valid, bucket, -1).astype(np.int32)


def _peer(d):
    x, y, c = lax.axis_index("x"), lax.axis_index("y"), lax.axis_index("c")
    dx, dy, dc = (d >> 2) & 1, (d >> 1) & 1, d & 1
    px, py, pc = x ^ dx, y ^ dy, c ^ dc
    return (px, py, pc), 4 * px + 2 * py + pc


class _Exchange:
    def __init__(self, arrays, gather):
        self.arrays, self.gather, self.n = list(arrays), gather, len(arrays)
        self.out_shape = [S(((N_DEV,) + a.shape) if gather else a.shape, a.dtype) for a in self.arrays]
        self.scratch = [pltpu.SemaphoreType.DMA((self.n, N_DEV - 1)), pltpu.SemaphoreType.DMA((self.n, N_DEV - 1)),
                        pltpu.SemaphoreType.DMA((self.n,))]

    def _copies(self, ins, outs, sems):
        send_sems, recv_sems, local_sems = sems
        _, me = _peer(0)
        local, sends, recvs = [], [], []
        for k in range(self.n):
            src = ins[k] if self.gather else ins[k].at[me]
            local.append(pltpu.make_async_copy(src, outs[k].at[me], local_sems.at[k]))
        for d in range(1, N_DEV):
            peer, pidx = _peer(d)
            for k in range(self.n):
                src = ins[k] if self.gather else ins[k].at[pidx]
                common = dict(src_ref=src, send_sem=send_sems.at[k, d - 1], recv_sem=recv_sems.at[k, d - 1],
                              device_id=peer, device_id_type=MESH_ID)
                sends.append(pltpu.make_async_remote_copy(dst_ref=outs[k].at[me], **common))
                recvs.append(pltpu.make_async_remote_copy(dst_ref=outs[k].at[pidx], **common))
        return local, sends, recvs

    def start(self, ins, outs, sems):
        local, sends, _ = self._copies(ins, outs, sems)
        for cp in local + sends:
            cp.start()

    def wait(self, ins, outs, sems):
        local, sends, recvs = self._copies(ins, outs, sems)
        for cp in recvs:
            cp.wait_recv()
        for cp in sends:
            cp.wait_send()
        for cp in local:
            cp.wait()


class _Gather:
    CHIPS = (4, 2, 6)
    SLOTS = 1 + 2 * len(CHIPS)

    def __init__(self, arrays):
        self.arrays, self.n = list(arrays), len(arrays)
        self.out_shape = [S((N_DEV,) + a.shape, a.dtype) for a in self.arrays]
        self.scratch = [pltpu.SemaphoreType.DMA((self.n, self.SLOTS)), pltpu.SemaphoreType.DMA((self.n, self.SLOTS)),
                        pltpu.SemaphoreType.DMA((self.n,))]

    @staticmethod
    def _copy(outs, sems, k, slot, src, block, to):
        return pltpu.make_async_remote_copy(src_ref=src, dst_ref=outs[k].at[block], send_sem=sems[0].at[k, slot],
                                            recv_sem=sems[1].at[k, slot], device_id=to, device_id_type=MESH_ID)

    def _local(self, ins, outs, sems):
        _, me = _peer(0)
        return [pltpu.make_async_copy(ins[k], outs[k].at[me], sems[2].at[k]) for k in range(self.n)]

    def start(self, ins, outs, sems):
        _, me = _peer(0)
        sibling, _ = _peer(1)
        for cp in self._local(ins, outs, sems):
            cp.start()
        for k in range(self.n):
            self._copy(outs, sems, k, 0, ins[k], me, sibling).start()
            for j, d in enumerate(self.CHIPS):
                self._copy(outs, sems, k, 1 + j, ins[k], me, _peer(d)[0]).start()

    def mid(self, ins, outs, sems):
        sibling, _ = _peer(1)
        for j, d in enumerate(self.CHIPS):
            chip, block = _peer(d)
            for k in range(self.n):
                self._copy(outs, sems, k, 1 + j, ins[k], block, chip).wait_recv()
                self._copy(outs, sems, k, 4 + j, outs[k].at[block], block, sibling).start()

    def wait(self, ins, outs, sems):
        _, me = _peer(0)
        sibling, sib_block = _peer(1)
        for k in range(self.n):
            self._copy(outs, sems, k, 0, ins[k], sib_block, sibling).wait_recv()
            for j, d in enumerate(self.CHIPS):
                self._copy(outs, sems, k, 4 + j, ins[k], _peer(d ^ 1)[1], sibling).wait_recv()
        for k in range(self.n):
            self._copy(outs, sems, k, 0, ins[k], me, sibling).wait_send()
            for j, d in enumerate(self.CHIPS):
                chip, block = _peer(d)
                self._copy(outs, sems, k, 1 + j, ins[k], me, chip).wait_send()
                self._copy(outs, sems, k, 4 + j, outs[k].at[block], block, sibling).wait_send()
        for cp in self._local(ins, outs, sems):
            cp.wait()


def _exchange(name, arrays, gather, two_level=False):
    ex = _Gather(arrays) if two_level else _Exchange(arrays, gather)
    n = ex.n

    def body(*refs):
        ins, outs, sems = refs[:n], refs[n:2 * n], refs[2 * n:]
        ex.start(ins, outs, sems)
        if two_level:
            ex.mid(ins, outs, sems)
        ex.wait(ins, outs, sems)

    return pl.pallas_call(body, name=name, out_shape=ex.out_shape, in_specs=[ANY] * n, out_specs=[ANY] * n,
                          scratch_shapes=ex.scratch)(*arrays)


def _call(body, *, name, grid, in_specs, out_specs, out_shape, args, scratch_shapes=(), ride=None):
    n_in, n_out, n_sc = len(in_specs), len(out_specs), len(scratch_shapes)
    sem = ("arbitrary",) * len(grid)
    if ride is None:
        res = pl.pallas_call(body, name=name, grid=grid, in_specs=list(in_specs), out_specs=list(out_specs),
                             out_shape=list(out_shape), scratch_shapes=list(scratch_shapes), compiler_params=_params(*sem))(*args)
        return list(res), []
    nx = ride.n

    def riding(*refs):
        ins, xin = refs[:n_in], refs[n_in:n_in + nx]
        outs, xout = refs[n_in + nx:n_in + nx + n_out], refs[n_in + nx + n_out:n_in + 2 * nx + n_out]
        rest = refs[n_in + 2 * nx + n_out:]
        scratch, sems = rest[:n_sc], rest[n_sc:]
        ids = [pl.program_id(ax) for ax in range(len(grid))]
        first = functools.reduce(jnp.logical_and, [i == 0 for i in ids])
        last = functools.reduce(jnp.logical_and, [i == g - 1 for i, g in zip(ids, grid)])

        @pl.when(first)
        def _():
            ride.start(xin, xout, sems)

        if hasattr(ride, "mid"):
            halfway = functools.reduce(jnp.logical_and, [ids[0] == grid[0] // 2] + [i == 0 for i in ids[1:]])

            @pl.when(halfway)
            def _():
                ride.mid(xin, xout, sems)

        body(*ins, *outs, *scratch)

        @pl.when(last)
        def _():
            ride.wait(xin, xout, sems)

    res = pl.pallas_call(
        riding, name=name, grid=grid, in_specs=list(in_specs) + [ANY] * nx, out_specs=list(out_specs) + [ANY] * nx,
        out_shape=list(out_shape) + ride.out_shape, scratch_shapes=list(scratch_shapes) + ride.scratch,
        compiler_params=_params(*sem))(*args, *ride.arrays)
    return list(res[:n_out]), list(res[n_out:])


def _resident(shape):
    return pl.BlockSpec(shape, lambda *_: (0,) * len(shape), pipeline_mode=pl.Buffered(1))


def _proj_fwd(x, g, w, qg, kg, tm, ride=None):
    T, K = x.shape
    N = w.shape[1]
    per = COL // (2 * HEAD_DIM)
    assert Q_END % COL == 0 and V_END == Q_END + COL and (GLU_END - V_END) == 4 * COL and KV_WIDTH == COL // 2

    def body(x_ref, g_ref, w_ref, qg_ref, kg_ref, o_ref, u_ref, qn_ref, kn_ref, vb_ref, h0_ref):
        xv = x_ref[...]
        r = lax.rsqrt(jnp.mean(xv * xv, axis=-1, keepdims=True) + EPS)
        u = (xv * r * g_ref[...]).astype(bf16)
        u_ref[...] = u

        def block(c):
            cs = slice(c * COL, (c + 1) * COL)
            pc = jnp.dot(u, w_ref[:, cs], preferred_element_type=f32)
            o_ref[:, cs] = pc.astype(bf16)
            return pc

        qgv = qg_ref[...] * (HEAD_DIM ** -0.5)
        for c in range(Q_END // COL):
            pc = block(c)
            for t in range(per):
                xq = pc[:, _pair_cols(t)]
                qn_ref[:, _pair_cols(c * per + t)] = (xq * _pair_rstd(xq, False) * qgv).astype(bf16)
        pc = block(Q_END // COL)
        for t in range(KV_WIDTH // (2 * HEAD_DIM)):
            xk = pc[:, _pair_cols(t)]
            kn_ref[:, _pair_cols(t)] = (xk * _pair_rstd(xk, False) * kg_ref[...]).astype(bf16)
        vb_ref[...] = pc[:, KV_WIDTH:].astype(bf16)
        a0 = V_END // COL
        for half in range(2):
            gate = block(a0 + 2 + half)
            h0_ref[:, half * COL:(half + 1) * COL] = block(a0 + half) * _sigmoid(gate)
        for c in range(GLU_END // COL, N // COL):
            block(c)

    row = lambda width: pl.BlockSpec((tm, width), lambda i: (i, 0))
    return _call(
        body, name="proj_fwd", grid=(T // tm,),
        in_specs=[row(K), _resident((1, K)), _resident((K, N)), _resident((1, 2 * HEAD_DIM)), _resident((1, 2 * HEAD_DIM))],
        out_specs=[row(N), row(K), row(ATTN_WIDTH), row(KV_WIDTH), row(KV_WIDTH), row(D_MODEL)],
        out_shape=[S((T, N), bf16), S((T, K), bf16), S((T, ATTN_WIDTH), bf16), S((T, KV_WIDTH), bf16), S((T, KV_WIDTH), bf16),
                   S((T, D_MODEL), f32)],
        args=(x, g, w, qg, kg), ride=ride)


def _bias_table(rel_bias, bucket):
    def body(rb_ref, bk_ref, o_ref):
        b = bk_ref[...]
        absent = lax.broadcasted_iota(jnp.int32, (QBLOCK, 2 * QBLOCK), 1) < QBLOCK
        for h in range(N_Q_HEADS):
            acc = jnp.full((QBLOCK, 2 * QBLOCK), NEG, f32)
            for k in range(N_BUCKETS):
                acc = jnp.where(b == k, rb_ref[k, h], acc)
            o_ref[0, h * QBLOCK:(h + 1) * QBLOCK, :] = acc
            o_ref[1, h * QBLOCK:(h + 1) * QBLOCK, :] = jnp.where(absent, NEG, acc)

    return pl.pallas_call(
        body, name="bias_table", out_shape=S((2, N_Q_HEADS * QBLOCK, 2 * QBLOCK), f32),
        in_specs=[SMEM, pl.BlockSpec(memory_space=pltpu.VMEM)],
    )(rel_bias, bucket)


def _bias_spec():
    return pl.BlockSpec((None, N_Q_HEADS * QBLOCK, 2 * QBLOCK), lambda n: (jnp.where(n == 0, 1, 0), 0, 0))


def _swap_halves(t):
    return jnp.concatenate([t[:, HEAD_DIM:], t[:, :HEAD_DIM]], axis=1)


def _low_lanes():
    return lax.broadcasted_iota(jnp.int32, (1, 2 * HEAD_DIM), 1) < HEAD_DIM


def _one_head(pair, side):
    zero = jnp.zeros((), pair.dtype)
    return jnp.where(_low_lanes(), pair, zero) if side == 0 else jnp.where(_low_lanes(), zero, pair)


def _pair_mean(t, on_mxu):
    if not on_mxu:
        m_lo = jnp.sum(_one_head(t, 0), axis=-1, keepdims=True) * (1.0 / HEAD_DIM)
        m_hi = jnp.sum(_one_head(t, 1), axis=-1, keepdims=True) * (1.0 / HEAD_DIM)
        return jnp.where(_low_lanes(), m_lo, m_hi)
    width = 2 * HEAD_DIM
    same_head = ((lax.broadcasted_iota(jnp.int32, (width, width), 0) < HEAD_DIM)
                 == (lax.broadcasted_iota(jnp.int32, (width, width), 1) < HEAD_DIM))
    e = jnp.where(same_head, 1.0 / HEAD_DIM, 0.0).astype(bf16)
    hi = t.astype(bf16)
    lo = (t - hi.astype(f32)).astype(bf16)
    return jnp.dot(hi, e, preferred_element_type=f32) + jnp.dot(lo, e, preferred_element_type=f32)


def _pair_rstd(x, on_mxu):
    return lax.rsqrt(_pair_mean(x * x, on_mxu) + EPS)


def _kv_placements(band):
    out = {}
    for m in range(N_KV_HEADS // 2):
        pair = band[:, m * 2 * HEAD_DIM:(m + 1) * 2 * HEAD_DIM]
        swapped = _swap_halves(pair)
        for hh in range(2):
            out[2 * m + hh, 0] = _one_head(pair if hh == 0 else swapped, 0)
            out[2 * m + hh, 1] = _one_head(swapped if hh == 0 else pair, 1)
    return out


def _head_rows(hq):
    return slice(hq * QBLOCK, (hq + 1) * QBLOCK)


def _pair_cols(pr):
    return slice(pr * 2 * HEAD_DIM, (pr + 1) * 2 * HEAD_DIM)


def _attn_fwd(qn, kn, vb, bias, sinks, ride=None):
    T = qn.shape[0]
    nb = T // QBLOCK

    def body(q_ref, kc_ref, kp_ref, vc_ref, vp_ref, b_ref, s_ref, o_ref, lse_ref, s_scr, p_scr):
        lane = lax.broadcasted_iota(jnp.int32, (QBLOCK, 2 * HEAD_DIM), 1)
        kx = _kv_placements(jnp.concatenate([kp_ref[...], kc_ref[...]], axis=0))
        vx = _kv_placements(jnp.concatenate([vp_ref[...], vc_ref[...]], axis=0))
        for hq in range(N_Q_HEADS):
            qm = _one_head(q_ref[:, _pair_cols(hq // 2)], hq % 2)
            s_scr[_head_rows(hq), :] = _nt(qm, kx[hq // GROUP, hq % 2]) + b_ref[_head_rows(hq), :]
        lse_tile = jnp.zeros((QBLOCK, 2 * HEAD_DIM), f32)
        for hq in range(N_Q_HEADS):
            s = s_scr[_head_rows(hq), :]
            sink = s_ref[0, hq]
            m = jnp.maximum(jnp.max(s, axis=-1, keepdims=True), sink)
            p = jnp.exp(s - m)
            l = jnp.sum(p, axis=-1, keepdims=True) + jnp.exp(sink - m)
            p_scr[_head_rows(hq), :] = (p * (1.0 / l)).astype(bf16)
            lse_tile = jnp.where(lane == hq, m + jnp.log(l), lse_tile)
        lse_ref[...] = lse_tile
        for pr in range(N_Q_HEADS // 2):
            h = 2 * pr // GROUP
            o_pair = (jnp.dot(p_scr[_head_rows(2 * pr), :], vx[h, 0], preferred_element_type=f32)
                      + jnp.dot(p_scr[_head_rows(2 * pr + 1), :], vx[h, 1], preferred_element_type=f32))
            o_ref[:, _pair_cols(pr)] = o_pair.astype(bf16)

    cur = lambda n: (n, 0)
    prev = lambda n: (jnp.maximum(n - 1, 0), 0)
    return _call(
        body, name="attn_fwd", grid=(nb,),
        in_specs=[pl.BlockSpec((QBLOCK, ATTN_WIDTH), cur), pl.BlockSpec((QBLOCK, KV_WIDTH), cur),
                  pl.BlockSpec((QBLOCK, KV_WIDTH), prev), pl.BlockSpec((QBLOCK, KV_WIDTH), cur),
                  pl.BlockSpec((QBLOCK, KV_WIDTH), prev), _bias_spec(), SMEM],
        out_specs=[pl.BlockSpec((QBLOCK, ATTN_WIDTH), cur), pl.BlockSpec((QBLOCK, 2 * HEAD_DIM), cur)],
        out_shape=[S((T, ATTN_WIDTH), bf16), S((T, 2 * HEAD_DIM), f32)],
        scratch_shapes=[pltpu.VMEM((N_Q_HEADS * QBLOCK, 2 * QBLOCK), f32), pltpu.VMEM((N_Q_HEADS * QBLOCK, 2 * QBLOCK), bf16)],
        args=(qn, kn, kn, vb, vb, bias, sinks), ride=ride)


def _layer_norm_stats(h1):
    mu = jnp.mean(h1, axis=-1, keepdims=True)
    xc = h1 - mu
    rstd = lax.rsqrt(jnp.mean(xc * xc, axis=-1, keepdims=True) + EPS)
    return xc * rstd, rstd


def _advanced_windows(win):
    rows = win.shape[0]
    for r in range(8):
        yield r, (win if r == 0 else pltpu.roll(win, rows - r, 0))


def _tap_offsets(r, rows):
    for q in range((rows - CONV_UNIT) // 8 + 1):
        if r == 0 or 8 * q + r + CONV_UNIT <= rows:
            yield q, 8 * q + r


def _conv_fwd(h0, w_dw, b_dw, ln_g, ln_b, tm, ride=None):
    T = h0.shape[0]
    per = tm // CONV_HALO
    lead = CONV_HALO - (CONV_WIDTH - 1)

    def body(hc_ref, hp_ref, w_ref, b_ref, g_ref, bb_ref, h1_ref, h3_ref, cat):
        i = pl.program_id(0)
        cat[0:CONV_HALO, :] = jnp.where(i == 0, 0.0, hp_ref[...])
        cat[CONV_HALO:, :] = hc_ref[...]

        def unit_rows(c, carry):
            r0 = pl.multiple_of(c * CONV_UNIT, CONV_UNIT)
            for j in range(D_MODEL // 128):
                ls = slice(j * 128, (j + 1) * 128)
                win = cat[pl.ds(r0, CONV_UNIT + CONV_HALO), ls]
                acc = jnp.zeros((CONV_UNIT, 128), f32) + b_ref[:, ls]
                for r, adv in _advanced_windows(win):
                    for q, off in _tap_offsets(r, CONV_UNIT + CONV_HALO):
                        k = off - lead
                        if 0 <= k < CONV_WIDTH:
                            acc = acc + adv[8 * q:8 * q + CONV_UNIT] * w_ref[k:k + 1, ls]
                h1_ref[pl.ds(r0, CONV_UNIT), ls] = acc
            return carry

        lax.fori_loop(0, tm // CONV_UNIT, unit_rows, 0)
        acc = h1_ref[...]
        xhat, _ = _layer_norm_stats(acc)
        h2 = xhat * g_ref[...] + bb_ref[...]
        h3_ref[...] = (h2 * _sigmoid(h2)).astype(bf16)

    vec = pl.BlockSpec((1, D_MODEL), lambda i: (0, 0))
    return _call(
        body, name="conv_fwd", grid=(T // tm,),
        in_specs=[pl.BlockSpec((tm, D_MODEL), lambda i: (i, 0)),
                  pl.BlockSpec((CONV_HALO, D_MODEL), lambda i: (jnp.maximum(i * per - 1, 0), 0)),
                  pl.BlockSpec((CONV_WIDTH, D_MODEL), lambda i: (0, 0)), vec, vec, vec],
        out_specs=[pl.BlockSpec((tm, D_MODEL), lambda i: (i, 0)), pl.BlockSpec((tm, D_MODEL), lambda i: (i, 0))],
        out_shape=[S((T, D_MODEL), f32), S((T, D_MODEL), bf16)],
        scratch_shapes=[pltpu.VMEM((tm + CONV_HALO, D_MODEL), f32)],
        args=(h0, h0, w_dw, b_dw, ln_g, ln_b), ride=ride)


def _mix_fwd(x, o, h3, proj, w_ao, w_co, w_o, tm):
    T = x.shape[0]
    row = pl.BlockSpec((tm, D_MODEL), lambda i: (i, 0))
    wsp = _resident((D_MODEL, D_MODEL))
    g0 = GLU_END // COL

    def gate_spec(off):
        return pl.BlockSpec((tm, COL), lambda i: (i, g0 + off))

    def body(x_ref, o_ref, h3_ref, ga0, ga1, gc0, gc1, wa_ref, wc_ref, wo_ref, x1_ref, at_ref, cv_ref, mg_ref):
        attn = jnp.dot(o_ref[...], wa_ref[...], preferred_element_type=f32)
        conv = jnp.dot(h3_ref[...], wc_ref[...], preferred_element_type=f32)
        ga = jnp.concatenate([ga0[...], ga1[...]], axis=-1).astype(f32)
        gc = jnp.concatenate([gc0[...], gc1[...]], axis=-1).astype(f32)
        merged = (_sigmoid(ga) * attn + _sigmoid(gc) * conv).astype(bf16)
        at_ref[...] = attn.astype(bf16)
        cv_ref[...] = conv.astype(bf16)
        mg_ref[...] = merged
        x1_ref[...] = x_ref[...] + jnp.dot(merged, wo_ref[...], preferred_element_type=f32)

    return pl.pallas_call(
        body, name="mix_fwd", grid=(T // tm,),
        in_specs=[row, row, row, gate_spec(0), gate_spec(1), gate_spec(2), gate_spec(3), wsp, wsp, wsp],
        out_specs=[row, row, row, row],
        out_shape=[S((T, D_MODEL), f32), S((T, D_MODEL), bf16), S((T, D_MODEL), bf16), S((T, D_MODEL), bf16)],
        compiler_params=_params("parallel"),
    )(x, o, h3, proj, proj, proj, proj, w_ao, w_co, w_o)


def _ffn_fwd(x1, g, w1, w2, target, tm):
    T = x1.shape[0]
    nj = w1.shape[0] // FF_CHUNK

    def body(x_ref, g_ref, w1_ref, w2_ref, t_ref, a_ref, u_ref, dy_ref, dyb_ref, ls_ref, hm):
        xv = x_ref[...]
        r = lax.rsqrt(jnp.mean(xv * xv, axis=-1, keepdims=True) + EPS)
        u = (xv * r * g_ref[...]).astype(bf16)
        u_ref[...] = u
        for j in range(nj):
            js = slice(j * FF_CHUNK, (j + 1) * FF_CHUNK)
            a = _nt(u, w1_ref[js, :])
            a_ref[:, js] = a.astype(bf16)
            hm[:, js] = jnp.square(jnp.maximum(a, 0.0)).astype(bf16)
        err = xv + jnp.dot(hm[...], w2_ref[...], preferred_element_type=f32) - t_ref[...]
        dy = err * (1.0 / D_MODEL)
        dy_ref[...] = dy
        dyb_ref[...] = dy.astype(bf16)
        ls_ref[...] = jnp.zeros((8, 128), f32) + jnp.sum(err * err) * (0.5 / D_MODEL)

    row = pl.BlockSpec((tm, D_MODEL), lambda i: (i, 0))
    wide = pl.BlockSpec((tm, D_FF), lambda i: (i, 0))
    return pl.pallas_call(
        body, name="ffn_fwd", grid=(T // tm,),
        in_specs=[row, _resident((1, D_MODEL)), _resident(w1.shape), _resident(w2.shape), row],
        out_specs=[wide, row, row, row, pl.BlockSpec((None, 8, 128), lambda i: (i, 0, 0))],
        out_shape=[S((T, D_FF), bf16), S((T, D_MODEL), bf16), S((T, D_MODEL), f32), S((T, D_MODEL), bf16),
                   S((T // tm, 8, 128), f32)],
        scratch_shapes=[pltpu.VMEM((tm, D_FF), bf16)],
        compiler_params=_params("parallel"),
    )(x1, g, w1, w2, target)


def _rms_bwd(du, xv, gv):
    r = lax.rsqrt(jnp.mean(xv * xv, axis=-1, keepdims=True) + EPS)
    xn = xv * r
    dg = jnp.sum(du * xn, axis=0, keepdims=True)
    dxn = du * gv
    dx = r * (dxn - xn * jnp.mean(dxn * xn, axis=-1, keepdims=True))
    return dx, dg


def _ffn_bwd(dy, dyb, a, x1, g, w1, w2, tm, ride=None):
    T = dy.shape[0]
    nj = w1.shape[0] // FF_CHUNK

    def body(dy_ref, dyb_ref, a_ref, x_ref, g_ref, w1_ref, w2_ref, da_ref, dx_ref, dxb_ref, dg_ref):
        @pl.when(pl.program_id(0) == 0)
        def _():
            dg_ref[...] = jnp.zeros_like(dg_ref)

        dyb_v = dyb_ref[...]
        for j in range(nj):
            js = slice(j * FF_CHUNK, (j + 1) * FF_CHUNK)
            dh = _nt(dyb_v, w2_ref[js, :])
            da_ref[:, js] = (dh * (2.0 * jnp.maximum(a_ref[:, js].astype(f32), 0.0))).astype(bf16)
        du = jnp.dot(da_ref[...], w1_ref[...], preferred_element_type=f32)
        dx, dg = _rms_bwd(du, x_ref[...], g_ref[...])
        dx1 = dy_ref[...] + dx
        dx_ref[...] = dx1
        dxb_ref[...] = dx1.astype(bf16)
        dg_ref[...] += dg

    row = pl.BlockSpec((tm, D_MODEL), lambda i: (i, 0))
    wide = pl.BlockSpec((tm, D_FF), lambda i: (i, 0))
    vec = pl.BlockSpec((1, D_MODEL), lambda i: (0, 0))
    return _call(
        body, name="ffn_bwd", grid=(T // tm,),
        in_specs=[row, row, wide, row, _resident((1, D_MODEL)), _resident(w1.shape), _resident(w2.shape)],
        out_specs=[wide, row, row, vec],
        out_shape=[S((T, D_FF), bf16), S((T, D_MODEL), f32), S((T, D_MODEL), bf16), S((1, D_MODEL), f32)],
        args=(dy, dyb, a, x1, g, w1, w2), ride=ride)


def _wgrad(name, a, b, tk, tn, tt, relu2=False, slab=None, out_dtype=bf16):
    T, Ka = a.shape
    Nb = b.shape[1]
    nt = T // tt

    def body(a_ref, b_ref, o_ref, acc):
        t = pl.program_id(2)
        av = a_ref[...]
        if relu2:
            av = jnp.square(jnp.maximum(av.astype(f32), 0.0))
        prod = _tn(av.astype(bf16), b_ref[...].astype(bf16))

        @pl.when(t == 0)
        def _():
            acc[...] = prod

        @pl.when(t > 0)
        def _():
            acc[...] += prod

        @pl.when(t == nt - 1)
        def _():
            if slab is None:
                o_ref[...] = acc[...].astype(out_dtype)
            else:
                for s in range(tn // slab):
                    o_ref[s] = acc[:, s * slab:(s + 1) * slab].astype(out_dtype)

    if slab is not None:
        out_shape = S((Nb // slab, Ka, slab), out_dtype)
        out_spec = pl.BlockSpec((tn // slab, tk, slab), lambda i, j, t: (j, i, 0))
    else:
        out_shape = S((Ka, Nb), out_dtype)
        out_spec = pl.BlockSpec((tk, tn), lambda i, j, t: (i, j))
    return pl.pallas_call(
        body, name=name, grid=(Ka // tk, Nb // tn, nt),
        in_specs=[pl.BlockSpec((tt, tk), lambda i, j, t: (t, i)), pl.BlockSpec((tt, tn), lambda i, j, t: (t, j))],
        out_specs=out_spec, out_shape=out_shape, scratch_shapes=[pltpu.VMEM((tk, tn), f32)],
        compiler_params=_params("parallel", "parallel", "arbitrary"),
    )(a, b)


def _mix_bwd(dx1, proj, attn, conv, h1, ln_g, ln_b, w_ao, w_co, w_o, tm, ride=None):
    T = dx1.shape[0]
    g0 = GLU_END // COL

    def gate_spec(off):
        return pl.BlockSpec((tm, COL), lambda i: (i, g0 + off))

    def body(dx_ref, ga0, ga1, gc0, gc1, at_ref, cv_ref, h_ref, g_ref, b_ref, wa_ref, wc_ref, wo_ref,
             da_ref, dc_ref, do_ref, dh1_ref, dg_ref, dlg_ref, dlb_ref, dbd_ref):
        @pl.when(pl.program_id(0) == 0)
        def _():
            dlg_ref[...] = jnp.zeros_like(dlg_ref)
            dlb_ref[...] = jnp.zeros_like(dlb_ref)
            dbd_ref[...] = jnp.zeros_like(dbd_ref)

        dm = _nt(dx_ref[...].astype(bf16), wo_ref[...])
        sa = _sigmoid(jnp.concatenate([ga0[...], ga1[...]], axis=-1).astype(f32))
        sc = _sigmoid(jnp.concatenate([gc0[...], gc1[...]], axis=-1).astype(f32))
        dattn = (dm * sa).astype(bf16)
        dconv = (dm * sc).astype(bf16)
        da_ref[...] = dattn
        dc_ref[...] = dconv
        dg_ref[:, 0:D_MODEL] = (dm * at_ref[...].astype(f32) * sa * (1.0 - sa)).astype(bf16)
        dg_ref[:, D_MODEL:2 * D_MODEL] = (dm * cv_ref[...].astype(f32) * sc * (1.0 - sc)).astype(bf16)
        do_ref[...] = _nt(dattn, wa_ref[...]).astype(bf16)
        dh3 = _nt(dconv, wc_ref[...])
        xhat, rstd = _layer_norm_stats(h_ref[...])
        h2 = xhat * g_ref[...] + b_ref[...]
        sg = _sigmoid(h2)
        dh2 = dh3 * (sg * (1.0 + h2 * (1.0 - sg)))
        dlg_ref[...] += jnp.sum(dh2 * xhat, axis=0, keepdims=True)
        dlb_ref[...] += jnp.sum(dh2, axis=0, keepdims=True)
        dxh = dh2 * g_ref[...]
        dh1 = rstd * (dxh - jnp.mean(dxh, axis=-1, keepdims=True) - xhat * jnp.mean(dxh * xhat, axis=-1, keepdims=True))
        dh1_ref[...] = dh1
        dbd_ref[...] += jnp.sum(dh1, axis=0, keepdims=True)

    row = pl.BlockSpec((tm, D_MODEL), lambda i: (i, 0))
    vec = pl.BlockSpec((1, D_MODEL), lambda i: (0, 0))
    par = _resident((1, D_MODEL))
    wsp = _resident((D_MODEL, D_MODEL))
    return _call(
        body, name="mix_bwd", grid=(T // tm,),
        in_specs=[row, gate_spec(0), gate_spec(1), gate_spec(2), gate_spec(3), row, row, row, par, par, wsp, wsp, wsp],
        out_specs=[row, row, row, row, pl.BlockSpec((tm, 2 * D_MODEL), lambda i: (i, 0)), vec, vec, vec],
        out_shape=[S((T, D_MODEL), bf16), S((T, D_MODEL), bf16), S((T, D_MODEL), bf16), S((T, D_MODEL), f32),
                   S((T, 2 * D_MODEL), bf16), S((1, D_MODEL), f32), S((1, D_MODEL), f32), S((1, D_MODEL), f32)],
        args=(dx1, proj, proj, proj, proj, attn, conv, h1, ln_g, ln_b, w_ao, w_co, w_o), ride=ride)


def _conv_bwd(dh1, h0, proj, w_dw, tm, ride=None):
    T = dh1.shape[0]
    per = tm // CONV_HALO
    nh = T // CONV_HALO
    nt = T // tm
    a0 = V_END // COL
    lead = CONV_HALO - (CONV_WIDTH - 1)

    def body(dc_ref, dn_ref, hc_ref, hp_ref, a0_ref, a1_ref, g0_ref, g1_ref, w_ref, dglu_ref, dw_ref, dcat, hcat, wacc, dh0):
        i = pl.program_id(0)

        @pl.when(i == 0)
        def _():
            wacc[...] = jnp.zeros_like(wacc)

        dcat[0:tm, :] = dc_ref[...]
        dcat[tm:, :] = jnp.where(i == nt - 1, 0.0, dn_ref[...])
        hcat[0:CONV_HALO, :] = jnp.where(i == 0, 0.0, hp_ref[...])
        hcat[CONV_HALO:, :] = hc_ref[...]
        span = CONV_UNIT + CONV_HALO

        def unit_rows(c, carry):
            r0 = pl.multiple_of(c * CONV_UNIT, CONV_UNIT)
            for j in range(D_MODEL // 128):
                ls = slice(j * 128, (j + 1) * 128)
                dwin = dcat[pl.ds(r0, span), ls]
                acc = jnp.zeros((CONV_UNIT, 128), f32)
                for r, adv in _advanced_windows(dwin):
                    for q, off in _tap_offsets(r, span):
                        k = CONV_WIDTH - 1 - off
                        if 0 <= k < CONV_WIDTH:
                            acc = acc + adv[8 * q:8 * q + CONV_UNIT] * w_ref[k:k + 1, ls]
                dh0[pl.ds(r0, CONV_UNIT), ls] = acc
                dcur = dwin[0:CONV_UNIT]
                for r, adv in _advanced_windows(hcat[pl.ds(r0, span), ls]):
                    for q, off in _tap_offsets(r, span):
                        k = off - lead
                        if 0 <= k < CONV_WIDTH:
                            prod = dcur * adv[8 * q:8 * q + CONV_UNIT]
                            wacc[k, :, ls] += jnp.sum(prod.reshape(CONV_UNIT // 8, 8, 128), axis=0)
            return carry

        lax.fori_loop(0, tm // CONV_UNIT, unit_rows, 0)
        dh0v = dh0[...]
        av = jnp.concatenate([a0_ref[...], a1_ref[...]], axis=-1).astype(f32)
        sg = _sigmoid(jnp.concatenate([g0_ref[...], g1_ref[...]], axis=-1).astype(f32))
        dglu_ref[:, 0:D_MODEL] = (dh0v * sg).astype(bf16)
        dglu_ref[:, D_MODEL:2 * D_MODEL] = (dh0v * av * sg * (1.0 - sg)).astype(bf16)

        @pl.when(i == nt - 1)
        def _():
            for k in range(CONV_WIDTH):
                dw_ref[k:k + 1, :] = jnp.sum(wacc[k], axis=0, keepdims=True)
            dw_ref[CONV_WIDTH:CONV_WIDTH + 1, :] = jnp.zeros((1, D_MODEL), f32)

    row = pl.BlockSpec((tm, D_MODEL), lambda i: (i, 0))

    def col_spec(off):
        return pl.BlockSpec((tm, COL), lambda i: (i, a0 + off))

    return _call(
        body, name="conv_bwd", grid=(nt,),
        in_specs=[row, pl.BlockSpec((CONV_HALO, D_MODEL), lambda i: (jnp.minimum((i + 1) * per, nh - 1), 0)),
                  row, pl.BlockSpec((CONV_HALO, D_MODEL), lambda i: (jnp.maximum(i * per - 1, 0), 0)),
                  col_spec(0), col_spec(1), col_spec(2), col_spec(3),
                  pl.BlockSpec((CONV_WIDTH, D_MODEL), lambda i: (0, 0))],
        out_specs=[pl.BlockSpec((tm, 2 * D_MODEL), lambda i: (i, 0)), pl.BlockSpec((CONV_WIDTH + 1, D_MODEL), lambda i: (0, 0))],
        out_shape=[S((T, 2 * D_MODEL), bf16), S((CONV_WIDTH + 1, D_MODEL), f32)],
        scratch_shapes=[pltpu.VMEM((tm + CONV_HALO, D_MODEL), f32), pltpu.VMEM((tm + CONV_HALO, D_MODEL), f32),
                        pltpu.VMEM((CONV_WIDTH, 8, D_MODEL), f32), pltpu.VMEM((tm, D_MODEL), f32)],
        args=(dh1, dh1, h0, h0, proj, proj, proj, proj, w_dw), ride=ride)


def _attn_bwd(qn, kn, vb, o, do, lse, bias, sinks, ride=None):
    T = qn.shape[0]
    nb = T // QBLOCK

    def body(q_ref, kc_ref, kp_ref, vc_ref, vp_ref, o_ref, do_ref, lse_ref, b_ref, s_ref,
             dq_ref, dk_ref, dv_ref, db_ref, dsk_ref, kcar, vcar, s_scr, dp_scr, p_scr, ds_scr):
        n = pl.program_id(0)

        @pl.when(n == 0)
        def _():
            db_ref[...] = jnp.zeros_like(db_ref)
            dsk_ref[...] = jnp.zeros_like(dsk_ref)
            kcar[...] = jnp.zeros_like(kcar)
            vcar[...] = jnp.zeros_like(vcar)

        @pl.when(n < nb)
        def _():
            lane = lax.broadcasted_iota(jnp.int32, (QBLOCK, 2 * HEAD_DIM), 1)
            lane_row = lax.broadcasted_iota(jnp.int32, (1, 2 * HEAD_DIM), 1)
            kx = _kv_placements(jnp.concatenate([kp_ref[...], kc_ref[...]], axis=0))
            vx = _kv_placements(jnp.concatenate([vp_ref[...], vc_ref[...]], axis=0))
            lse_tile = lse_ref[...]
            delta, lse_c = {}, {}
            for pr in range(N_Q_HEADS // 2):
                dop = do_ref[:, _pair_cols(pr)]
                dl = dop.astype(f32) * o_ref[:, _pair_cols(pr)].astype(f32)
                for side in range(2):
                    hq = 2 * pr + side
                    h = hq // GROUP
                    qm = _one_head(q_ref[:, _pair_cols(pr)], side)
                    s_scr[_head_rows(hq), :] = _nt(qm, kx[h, side]) + b_ref[_head_rows(hq), :]
                    dp_scr[_head_rows(hq), :] = _nt(_one_head(dop, side), vx[h, side])
                    delta[hq] = jnp.sum(_one_head(dl, side), axis=-1, keepdims=True)
                    lse_c[hq] = jnp.sum(jnp.where(lane == hq, lse_tile, 0.0), axis=-1, keepdims=True)
            dsk = jnp.zeros((1, 2 * HEAD_DIM), f32)
            for hq in range(N_Q_HEADS):
                p = jnp.exp(s_scr[_head_rows(hq), :] - lse_c[hq])
                ds = p * (dp_scr[_head_rows(hq), :] - delta[hq])
                db_ref[_head_rows(hq), :] += ds
                p_scr[_head_rows(hq), :] = p.astype(bf16)
                ds_scr[_head_rows(hq), :] = ds.astype(bf16)
                psink = jnp.exp(s_ref[0, hq] - lse_c[hq])
                dsk = dsk - jnp.where(lane_row == hq, jnp.sum(psink * delta[hq], axis=0, keepdims=True), 0.0)
            dsk_ref[...] += dsk
            for pr in range(N_Q_HEADS // 2):
                h = 2 * pr // GROUP
                dq_ref[:, _pair_cols(pr)] = (jnp.dot(ds_scr[_head_rows(2 * pr), :], kx[h, 0], preferred_element_type=f32)
                                             + jnp.dot(ds_scr[_head_rows(2 * pr + 1), :], kx[h, 1], preferred_element_type=f32))
            folded_k, folded_v = [], []
            for h in range(N_KV_HEADS):
                ka = jnp.zeros((2 * QBLOCK, 2 * HEAD_DIM), f32)
                va = jnp.zeros((2 * QBLOCK, 2 * HEAD_DIM), f32)
                for g in range(GROUP):
                    hq = h * GROUP + g
                    ka = ka + _tn(ds_scr[_head_rows(hq), :], _one_head(q_ref[:, _pair_cols(hq // 2)], hq % 2))
                    va = va + _tn(p_scr[_head_rows(hq), :], _one_head(do_ref[:, _pair_cols(hq // 2)], hq % 2))
                folded_k.append(ka + _swap_halves(ka))
                folded_v.append(va + _swap_halves(va))
            low = _low_lanes()
            for m in range(N_KV_HEADS // 2):
                cs = _pair_cols(m)
                for folded, out_ref, car in ((folded_k, dk_ref, kcar), (folded_v, dv_ref, vcar)):
                    band = jnp.where(low, folded[2 * m], folded[2 * m + 1])
                    out_ref[:, cs] = car[:, cs] + band[0:QBLOCK, :]
                    car[:, cs] = band[QBLOCK:, :]

        @pl.when(n == nb)
        def _():
            dk_ref[...] = kcar[...]
            dv_ref[...] = vcar[...]

    cur = lambda n: (jnp.minimum(n, nb - 1), 0)
    prev = lambda n: (jnp.clip(n - 1, 0, nb - 1), 0)
    qspec = pl.BlockSpec((QBLOCK, ATTN_WIDTH), cur)
    kcur, kprev = pl.BlockSpec((QBLOCK, KV_WIDTH), cur), pl.BlockSpec((QBLOCK, KV_WIDTH), prev)
    whole = lambda shape: pl.BlockSpec(shape, lambda n: (0,) * len(shape))
    scores = (N_Q_HEADS * QBLOCK, 2 * QBLOCK)
    return _call(
        body, name="attn_bwd", grid=(nb + 1,),
        in_specs=[qspec, kcur, kprev, kcur, kprev, qspec, qspec, pl.BlockSpec((QBLOCK, 2 * HEAD_DIM), cur), _bias_spec(), SMEM],
        out_specs=[qspec, kprev, kprev, whole(scores), whole((1, 2 * HEAD_DIM))],
        out_shape=[S((T, ATTN_WIDTH), f32), S((T, KV_WIDTH), f32), S((T, KV_WIDTH), f32), S(scores, f32),
                   S((1, 2 * HEAD_DIM), f32)],
        scratch_shapes=[pltpu.VMEM((QBLOCK, KV_WIDTH), f32), pltpu.VMEM((QBLOCK, KV_WIDTH), f32),
                        pltpu.VMEM(scores, f32), pltpu.VMEM(scores, f32), pltpu.VMEM(scores, bf16), pltpu.VMEM(scores, bf16)],
        args=(qn, kn, kn, vb, vb, o, do, lse, bias, sinks), ride=ride)


def _rel_bias_bwd(dbias, bucket):
    def body(d_ref, bk_ref, o_ref):
        b = bk_ref[...]
        for k in range(N_BUCKETS):
            mk = b == k
            for h in range(N_Q_HEADS):
                o_ref[k, h] = jnp.sum(jnp.where(mk, d_ref[h * QBLOCK:(h + 1) * QBLOCK, :], 0.0))

    return pl.pallas_call(body, name="rel_bias_bwd", out_shape=S((N_BUCKETS, N_Q_HEADS), f32), out_specs=SMEM)(dbias, bucket)


def _qk_norm_bwd(dq, dk, dv, proj, qg, kg, tm):
    T = dq.shape[0]
    scale = HEAD_DIM ** -0.5

    def pair_bwd(dy, x, gv):
        r = _pair_rstd(x, True)
        xn = x * r
        dxn = dy * gv
        dx = r * (dxn - xn * _pair_mean(dxn * xn, True))
        return dx, jnp.sum(dy * xn, axis=0, keepdims=True)

    def body(dq_ref, dk_ref, dv_ref, p_ref, qg_ref, kg_ref, out_ref, dqg_ref, dkg_ref):
        @pl.when(pl.program_id(0) == 0)
        def _():
            dqg_ref[...] = jnp.zeros_like(dqg_ref)
            dkg_ref[...] = jnp.zeros_like(dkg_ref)

        qgv, kgv = qg_ref[...], kg_ref[...]
        dqg = jnp.zeros((1, 2 * HEAD_DIM), f32)
        for pr in range(N_Q_HEADS // 2):
            dx, dg = pair_bwd(dq_ref[:, _pair_cols(pr)] * scale, p_ref[:, _pair_cols(pr)].astype(f32), qgv)
            out_ref[:, _pair_cols(pr)] = dx.astype(bf16)
            dqg = dqg + dg
        dkg = jnp.zeros((1, 2 * HEAD_DIM), f32)
        for pr in range(N_KV_HEADS // 2):
            ps = slice(Q_END + pr * 2 * HEAD_DIM, Q_END + (pr + 1) * 2 * HEAD_DIM)
            dx, dg = pair_bwd(dk_ref[:, _pair_cols(pr)], p_ref[:, ps].astype(f32), kgv)
            out_ref[:, ps] = dx.astype(bf16)
            dkg = dkg + dg
        out_ref[:, K_END:V_END] = dv_ref[...].astype(bf16)
        dqg_ref[...] += dqg
        dkg_ref[...] += dkg

    vec = pl.BlockSpec((1, 2 * HEAD_DIM), lambda i: (0, 0))
    return pl.pallas_call(
        body, name="qk_norm_bwd", grid=(T // tm,),
        in_specs=[pl.BlockSpec((tm, ATTN_WIDTH), lambda i: (i, 0)), pl.BlockSpec((tm, KV_WIDTH), lambda i: (i, 0)),
                  pl.BlockSpec((tm, KV_WIDTH), lambda i: (i, 0)), pl.BlockSpec((tm, V_END), lambda i: (i, 0)), vec, vec],
        out_specs=[pl.BlockSpec((tm, V_END), lambda i: (i, 0)), vec, vec],
        out_shape=[S((T, V_END), bf16), S((1, 2 * HEAD_DIM), f32), S((1, 2 * HEAD_DIM), f32)],
        compiler_params=_params("arbitrary"),
    )(dq, dk, dv, proj, qg, kg)


def _in_bwd(dqkv, dglu, dgates, w_in, x, g, dx1, tm, ride=None):
    T = x.shape[0]
    pieces = (dqkv, dglu, dgates)
    starts = [0, dqkv.shape[1], dqkv.shape[1] + dglu.shape[1]]

    def body(a0_ref, a1_ref, a2_ref, w_ref, x_ref, g_ref, d_ref, gx_ref, dg_ref):
        @pl.when(pl.program_id(0) == 0)
        def _():
            dg_ref[...] = jnp.zeros_like(dg_ref)

        du = jnp.zeros((tm, D_MODEL), f32)
        for a_ref, c0 in zip((a0_ref, a1_ref, a2_ref), starts):
            du = du + _nt(a_ref[...], w_ref[:, c0:c0 + a_ref.shape[1]])
        dx, dg = _rms_bwd(du, x_ref[...], g_ref[...])
        gx_ref[...] = d_ref[...] + dx
        dg_ref[...] += dg

    row = pl.BlockSpec((tm, D_MODEL), lambda i: (i, 0))
    return _call(
        body, name="in_bwd", grid=(T // tm,),
        in_specs=[pl.BlockSpec((tm, p.shape[1]), lambda i: (i, 0)) for p in pieces]
        + [_resident(w_in.shape), row, _resident((1, D_MODEL)), row],
        out_specs=[row, pl.BlockSpec((1, D_MODEL), lambda i: (0, 0))],
        out_shape=[S((T, D_MODEL), f32), S((1, D_MODEL), f32)],
        args=(dqkv, dglu, dgates, w_in, x, g, dx1), ride=ride)


def _adamw(name, parts, w, m, v, tr):
    _, R, C = w.shape
    bc1 = 1.0 - ADAM_B1 ** ADAM_STEP
    bc2 = 1.0 - ADAM_B2 ** ADAM_STEP

    def body(p_ref, w_ref, m_ref, v_ref, g_ref, d_ref, nm_ref, nv_ref):
        g = p_ref[0].astype(f32)
        for k in range(1, N_DEV):
            g = g + p_ref[k].astype(f32)
        nm = ADAM_B1 * m_ref[...] + (1.0 - ADAM_B1) * g
        nv = ADAM_B2 * v_ref[...] + (1.0 - ADAM_B2) * (g * g)
        g_ref[...] = g
        nm_ref[...] = nm
        nv_ref[...] = nv
        d_ref[...] = -ADAM_LR * ((nm / bc1) / (jnp.sqrt(nv / bc2) + ADAM_EPS) + ADAM_WD * w_ref[...])

    blk = pl.BlockSpec((None, tr, C), lambda i: (0, i, 0))
    return pl.pallas_call(
        body, name=name, grid=(R // tr,),
        in_specs=[pl.BlockSpec((N_DEV, tr, C), lambda i: (0, i, 0)), blk, blk, blk],
        out_specs=[blk, blk, blk, blk], out_shape=[S((1, R, C), f32)] * 4,
        compiler_params=_params("parallel"),
    )(parts, w, m, v)


def _tile(T, pref):
    return min(T, pref)


def _pad_rows(a, rows):
    return jnp.pad(a, ((0, rows - a.shape[0]), (0, 0)))


def kernel(x, norm_mix_g, w_in, q_norm_g, k_norm_g, attn_sinks, rel_bias, w_attn_o, w_dw, b_dw, conv_ln_g, conv_ln_b, w_conv_out, w_out, norm_mlp_g, w_ff1, w_ff2, loss_target, m_norm_mix_g, m_w_in, m_q_norm_g, m_k_norm_g, m_attn_sinks, m_rel_bias, m_w_attn_o, m_w_dw, m_b_dw, m_conv_ln_g, m_conv_ln_b, m_w_conv_out, m_w_out, m_norm_mlp_g, m_w_ff1, m_w_ff2, v_norm_mix_g, v_w_in, v_q_norm_g, v_k_norm_g, v_attn_sinks, v_rel_bias, v_w_attn_o, v_w_dw, v_b_dw, v_conv_ln_g, v_conv_ln_b, v_w_conv_out, v_w_out, v_norm_mlp_g, v_w_ff1, v_w_ff2):
    T = x.shape[1]
    xs = x[0]
    tgt = loss_target[0]
    in_shard = IN_WIDTH // N_DEV
    dw_rows = CONV_WIDTH + 1
    ch_shard = D_MODEL // N_DEV
    tb = _tile(T, 512)
    tt = _tile(T, 2048)
    bucket = jnp.asarray(_t5_bucket_table())

    g_in, g_dw = _exchange("gather_w_in", [w_in[0].astype(bf16), _pad_rows(w_dw[0], dw_rows)], gather=True, two_level=True)
    W_in = jnp.transpose(g_in, (1, 0, 2)).reshape(D_MODEL, IN_WIDTH)
    W_dw = jnp.transpose(g_dw, (1, 0, 2)).reshape(dw_rows, D_MODEL)[:CONV_WIDTH]

    mix_shards = _Gather([w_attn_o[0].astype(bf16), w_conv_out[0].astype(bf16), w_out[0].astype(bf16)])
    qg2, kg2 = jnp.tile(q_norm_g, (1, 2)), jnp.tile(k_norm_g, (1, 2))
    (proj, u, qn, kn, vb, h0), (g_ao, g_co, g_o) = _proj_fwd(xs, norm_mix_g, W_in, qg2, kg2, tb, ride=mix_shards)
    W_ao = g_ao.reshape(D_MODEL, D_MODEL)
    W_co = g_co.reshape(D_MODEL, D_MODEL)
    W_o = g_o.reshape(D_MODEL, D_MODEL)
    bias = _bias_table(rel_bias, bucket)
    (o, lse), (g_f1,) = _attn_fwd(qn, kn, vb, bias, attn_sinks, ride=_Gather([w_ff1[0].astype(bf16).T]))
    W_f1t = g_f1.reshape(D_FF, D_MODEL)
    (h1, h3), (g_f2,) = _conv_fwd(h0, W_dw, b_dw, conv_ln_g, conv_ln_b, tb, ride=_Gather([w_ff2[0].astype(bf16)]))
    x1, attn, conv, merged = _mix_fwd(xs, o, h3, proj, W_ao, W_co, W_o, tb)
    W_f2 = g_f2.reshape(D_FF, D_MODEL)
    a, u2, dy, dyb, loss_parts = _ffn_fwd(x1, norm_mlp_g, W_f1t, W_f2, tgt, tb)
    loss = lax.psum(jnp.sum(loss_parts[:, 0, 0]), ("x", "y", "c"))

    gw_f2 = _wgrad("wgrad_ff2", a, dyb, D_MODEL, D_MODEL, tt, relu2=True).reshape(N_DEV, FF_CHUNK, D_MODEL)
    (da, dx1, dx1b, d_norm_mlp_g), (l_f2,) = _ffn_bwd(dy, dyb, a, x1, norm_mlp_g, W_f1t, W_f2, tb,
                                                      ride=_Exchange([gw_f2], gather=False))
    gw_f1 = _wgrad("wgrad_ff1", u2, da, D_MODEL, 4 * FF_CHUNK, tt, slab=FF_CHUNK)
    gw_o = _wgrad("wgrad_out", merged, dx1b, D_MODEL, D_MODEL, tt).reshape(N_DEV, ch_shard, D_MODEL)
    (dattn, dconv, do, dh1, dgates, d_ln_g, d_ln_b, d_b_dw), (l_o,) = _mix_bwd(
        dx1b, proj, attn, conv, h1, conv_ln_g, conv_ln_b, W_ao, W_co, W_o, tb, ride=_Exchange([gw_o], gather=False))
    gw_ao = _wgrad("wgrad_attn_o", o, dattn, D_MODEL, D_MODEL, tt).reshape(N_DEV, ch_shard, D_MODEL)
    gw_co = _wgrad("wgrad_conv_out", h3, dconv, D_MODEL, D_MODEL, tt).reshape(N_DEV, ch_shard, D_MODEL)
    (dglu, d_w_dw), (l_f1, l_ao, l_co) = _conv_bwd(dh1, h0, proj, W_dw, tb,
                                                   ride=_Exchange([gw_f1, gw_ao, gw_co], gather=False))
    (dq, dk, dv, dbias, d_sinks), _ = _attn_bwd(qn, kn, vb, o, do, lse, bias, attn_sinks)
    d_sinks = d_sinks[:, :N_Q_HEADS]
    d_rel_bias = _rel_bias_bwd(dbias, bucket)
    dqkv, d_qg, d_kg = _qk_norm_bwd(dq, dk, dv, proj, qg2, kg2, tb)
    d_qg = d_qg[:, :HEAD_DIM] + d_qg[:, HEAD_DIM:]
    d_kg = d_kg[:, :HEAD_DIM] + d_kg[:, HEAD_DIM:]
    gw_in = jnp.concatenate([_wgrad("wgrad_in_qkv", u, dqkv, D_MODEL, V_END, tt),
                             _wgrad("wgrad_in_glu", u, dglu, D_MODEL, 2 * D_MODEL, tt),
                             _wgrad("wgrad_in_gates", u, dgates, D_MODEL, 2 * D_MODEL, tt)], axis=1)
    gw_in = jnp.transpose(gw_in.reshape(D_MODEL, N_DEV, in_shard), (1, 0, 2))
    gw_dw = jnp.transpose(d_w_dw.reshape(dw_rows, N_DEV, ch_shard), (1, 0, 2))
    (grad_x, d_norm_mix_g), (l_in, l_dw) = _in_bwd(dqkv, dglu, dgates, W_in, xs, norm_mix_g, dx1, tb,
                                                    ride=_Exchange([gw_in, gw_dw], gather=False))

    def row(vec):
        flat = vec.reshape(1, -1)
        return jnp.pad(flat, ((0, 0), (0, D_MODEL - flat.shape[1])))

    def pack_small(nm, qg, kg, sk, rb, bd, lg, lb, nl):
        tail = jnp.concatenate([qg.reshape(1, -1), kg.reshape(1, -1), sk.reshape(1, -1), rb.reshape(1, -1)], axis=1)
        return jnp.concatenate([row(nm), row(bd), row(lg), row(lb), row(nl), row(tail), jnp.zeros((2, D_MODEL), f32)], axis=0)

    def unpack_small(p):
        t = p[5]
        o0, o1, o2 = HEAD_DIM, 2 * HEAD_DIM, 2 * HEAD_DIM + N_Q_HEADS
        return dict(norm_mix_g=p[0:1], b_dw=p[1:2], conv_ln_g=p[2:3], conv_ln_b=p[3:4], norm_mlp_g=p[4:5],
                    q_norm_g=t[0:o0].reshape(1, HEAD_DIM), k_norm_g=t[o0:o1].reshape(1, HEAD_DIM),
                    attn_sinks=t[o1:o2].reshape(1, N_Q_HEADS),
                    rel_bias=t[o2:o2 + N_BUCKETS * N_Q_HEADS].reshape(N_BUCKETS, N_Q_HEADS))

    small_g = pack_small(d_norm_mix_g, d_qg, d_kg, d_sinks, d_rel_bias, d_b_dw, d_ln_g, d_ln_b, d_norm_mlp_g)
    (l_small,) = _exchange("gather_small_grads", [small_g], gather=True)


    res = {}
    res["w_in"] = _adamw("adamw_in", l_in, w_in, m_w_in, v_w_in, 256)
    res["w_attn_o"] = _adamw("adamw_attn_o", l_ao, w_attn_o, m_w_attn_o, v_w_attn_o, ch_shard)
    res["w_conv_out"] = _adamw("adamw_conv_out", l_co, w_conv_out, m_w_conv_out, v_w_conv_out, ch_shard)
    res["w_out"] = _adamw("adamw_out", l_o, w_out, m_w_out, v_w_out, ch_shard)
    res["w_ff1"] = _adamw("adamw_ff1", l_f1, w_ff1, m_w_ff1, v_w_ff1, 256)
    res["w_ff2"] = _adamw("adamw_ff2", l_f2, w_ff2, m_w_ff2, v_w_ff2, 256)
    pad_dw = lambda t: _pad_rows(t[0], dw_rows)[None]
    res["w_dw"] = [t[:, :CONV_WIDTH] for t in _adamw("adamw_dw", l_dw, pad_dw(w_dw), pad_dw(m_w_dw), pad_dw(v_w_dw), dw_rows)]
    small_w = pack_small(norm_mix_g, q_norm_g, k_norm_g, attn_sinks, rel_bias, b_dw, conv_ln_g, conv_ln_b, norm_mlp_g)
    small_m = pack_small(m_norm_mix_g, m_q_norm_g, m_k_norm_g, m_attn_sinks, m_rel_bias, m_b_dw, m_conv_ln_g, m_conv_ln_b, m_norm_mlp_g)
    small_v = pack_small(v_norm_mix_g, v_q_norm_g, v_k_norm_g, v_attn_sinks, v_rel_bias, v_b_dw, v_conv_ln_g, v_conv_ln_b, v_norm_mlp_g)
    small4 = [unpack_small(t[0]) for t in _adamw("adamw_small", l_small, small_w[None], small_m[None], small_v[None], 8)]

    order = ["norm_mix_g", "w_in", "q_norm_g", "k_norm_g", "attn_sinks", "rel_bias", "w_attn_o", "w_dw", "b_dw",
             "conv_ln_g", "conv_ln_b", "w_conv_out", "w_out", "norm_mlp_g", "w_ff1", "w_ff2"]
    stacked = {"w_in", "w_attn_o", "w_dw", "w_conv_out", "w_out", "w_ff1", "w_ff2"}
    outs = [loss, grad_x[None]]
    for k in range(4):
        for nme in order:
            if nme in stacked:
                outs.append(res[nme][k])
            else:
                outs.append(small4[k][nme])
    return tuple(outs)
```

```python
import functools

import numpy as np
import jax
import jax.numpy as jnp
from jax import lax
from jax.experimental import pallas as pl
from jax.experimental.pallas import tpu as pltpu

f32 = jnp.float32
bf16 = jnp.bfloat16
S = jax.ShapeDtypeStruct

N_DEV = 8
D_MODEL = 1024
HEAD_DIM = 64
N_Q_HEADS = 16
N_KV_HEADS = 4
GROUP = N_Q_HEADS // N_KV_HEADS
ATTN_WIDTH = N_Q_HEADS * HEAD_DIM
KV_WIDTH = N_KV_HEADS * HEAD_DIM
QBLOCK = 128
CONV_WIDTH = 31
CONV_HALO = 32
CONV_UNIT = 64
D_FF = 4 * D_MODEL
N_BUCKETS = 32
MAX_DISTANCE = 128
EPS = 1e-6
NEG = -1e30
Q_END = ATTN_WIDTH
K_END = Q_END + KV_WIDTH
V_END = K_END + KV_WIDTH
GLU_END = V_END + 2 * D_MODEL
IN_WIDTH = GLU_END + 2 * D_MODEL
COL = 512
FF_CHUNK = D_FF // N_DEV

ADAM_LR = 0.001
ADAM_B1 = 0.9
ADAM_B2 = 0.999
ADAM_EPS = 1e-08
ADAM_WD = 0.01
ADAM_STEP = 10

VMEM_LIMIT = 56 * 1024 * 1024

MESH_ID = pl.DeviceIdType.MESH
ANY = pl.BlockSpec(memory_space=pl.ANY)
SMEM = pl.BlockSpec(memory_space=pltpu.SMEM)


def _params(*sem):
    return pltpu.CompilerParams(dimension_semantics=sem, vmem_limit_bytes=VMEM_LIMIT)


def _nt(a, b):
    return lax.dot_general(a, b, (((1,), (1,)), ((), ())), preferred_element_type=f32)


def _tn(a, b):
    return lax.dot_general(a, b, (((0,), (0,)), ((), ())), preferred_element_type=f32)


def _sigmoid(z):
    return 1.0 / (1.0 + jnp.exp(-z))


def _t5_bucket_table():
    qi = np.arange(QBLOCK, dtype=np.int32)[:, None]
    kj = np.arange(2 * QBLOCK, dtype=np.int32)[None, :]
    dist = qi + QBLOCK - kj
    n = np.maximum(dist, 0)
    max_exact = N_BUCKETS // 2
    nf = np.maximum(n, 1).astype(np.float32)
    large = max_exact + (np.log(nf / np.float32(max_exact)) / np.float32(np.log(MAX_DISTANCE / max_exact))
                         * np.float32(N_BUCKETS - max_exact)).astype(np.int32)
    large = np.minimum(large, N_BUCKETS - 1)
    bucket = np.where(n < max_exact, n, large)
    valid = (dist >= 0) & (dist < QBLOCK)
    return np.where(valid, bucket, -1).astype(np.int32)


def _peer(d):
    x, y, c = lax.axis_index("x"), lax.axis_index("y"), lax.axis_index("c")
    dx, dy, dc = (d >> 2) & 1, (d >> 1) & 1, d & 1
    px, py, pc = x ^ dx, y ^ dy, c ^ dc
    return (px, py, pc), 4 * px + 2 * py + pc


class _Exchange:
    def __init__(self, arrays, gather):
        self.arrays, self.gather, self.n = list(arrays), gather, len(arrays)
        self.out_shape = [S(((N_DEV,) + a.shape) if gather else a.shape, a.dtype) for a in self.arrays]
        self.scratch = [pltpu.SemaphoreType.DMA((self.n, N_DEV - 1)), pltpu.SemaphoreType.DMA((self.n, N_DEV - 1)),
                        pltpu.SemaphoreType.DMA((self.n,))]

    def _copies(self, ins, outs, sems):
        send_sems, recv_sems, local_sems = sems
        _, me = _peer(0)
        local, sends, recvs = [], [], []
        for k in range(self.n):
            src = ins[k] if self.gather else ins[k].at[me]
            local.append(pltpu.make_async_copy(src, outs[k].at[me], local_sems.at[k]))
        for d in range(1, N_DEV):
            peer, pidx = _peer(d)
            for k in range(self.n):
                src = ins[k] if self.gather else ins[k].at[pidx]
                common = dict(src_ref=src, send_sem=send_sems.at[k, d - 1], recv_sem=recv_sems.at[k, d - 1],
                              device_id=peer, device_id_type=MESH_ID)
                sends.append(pltpu.make_async_remote_copy(dst_ref=outs[k].at[me], **common))
                recvs.append(pltpu.make_async_remote_copy(dst_ref=outs[k].at[pidx], **common))
        return local, sends, recvs

    def start(self, ins, outs, sems):
        local, sends, _ = self._copies(ins, outs, sems)
        for cp in local + sends:
            cp.start()

    def wait(self, ins, outs, sems):
        local, sends, recvs = self._copies(ins, outs, sems)
        for cp in recvs:
            cp.wait_recv()
        for cp in sends:
            cp.wait_send()
        for cp in local:
            cp.wait()


class _PartExchange:
    def __init__(self, arrays, dests):
        self.arrays, self.dests, self.n = list(arrays), list(dests), len(arrays)
        for a, (lo, hi) in zip(self.arrays, self.dests):
            assert a.shape[0] == hi - lo
        self.out_shape = [S((N_DEV,) + a.shape[1:], a.dtype) for a in self.arrays]
        self.scratch = [pltpu.SemaphoreType.DMA((self.n, N_DEV - 1)), pltpu.SemaphoreType.DMA((self.n, N_DEV - 1)),
                        pltpu.SemaphoreType.DMA((self.n,))]

    def _for_each(self, ins, outs, sems, local_fn, send_fn, recv_fn):
        send_sems, recv_sems, local_sems = sems
        _, me = _peer(0)
        for k in range(self.n):
            lo, hi = self.dests[k]
            mine = (me >= lo) & (me < hi)
            if local_fn is not None:
                @pl.when(mine)
                def _(k=k, lo=lo):
                    local_fn(pltpu.make_async_copy(ins[k].at[me - lo], outs[k].at[me], local_sems.at[k]))
            for d in range(1, N_DEV):
                peer, pidx = _peer(d)
                common = dict(send_sem=send_sems.at[k, d - 1], recv_sem=recv_sems.at[k, d - 1], device_id=peer,
                              device_id_type=MESH_ID)
                if send_fn is not None:
                    @pl.when((pidx >= lo) & (pidx < hi))
                    def _(k=k, lo=lo, pidx=pidx, common=common):
                        send_fn(pltpu.make_async_remote_copy(src_ref=ins[k].at[pidx - lo], dst_ref=outs[k].at[me], **common))
                if recv_fn is not None:
                    @pl.when(mine)
                    def _(k=k, pidx=pidx, common=common):
                        recv_fn(pltpu.make_async_remote_copy(src_ref=ins[k].at[0], dst_ref=outs[k].at[pidx], **common))

    def start(self, ins, outs, sems):
        self._for_each(ins, outs, sems, lambda cp: cp.start(), lambda cp: cp.start(), None)

    def wait(self, ins, outs, sems):
        self._for_each(ins, outs, sems, None, None, lambda cp: cp.wait_recv())
        self._for_each(ins, outs, sems, lambda cp: cp.wait(), lambda cp: cp.wait_send(), None)


class _Gather:
    CHIPS = (4, 2, 6)
    SLOTS = 1 + 2 * len(CHIPS)

    def __init__(self, arrays):
        self.arrays, self.n = list(arrays), len(arrays)
        self.out_shape = [S((N_DEV,) + a.shape, a.dtype) for a in self.arrays]
        self.scratch = [pltpu.SemaphoreType.DMA((self.n, self.SLOTS)), pltpu.SemaphoreType.DMA((self.n, self.SLOTS)),
                        pltpu.SemaphoreType.DMA((self.n,))]

    @staticmethod
    def _copy(outs, sems, k, slot, src, block, to):
        return pltpu.make_async_remote_copy(src_ref=src, dst_ref=outs[k].at[block], send_sem=sems[0].at[k, slot],
                                            recv_sem=sems[1].at[k, slot], device_id=to, device_id_type=MESH_ID)

    def _local(self, ins, outs, sems):
        _, me = _peer(0)
        return [pltpu.make_async_copy(ins[k], outs[k].at[me], sems[2].at[k]) for k in range(self.n)]

    def start(self, ins, outs, sems):
        _, me = _peer(0)
        sibling, _ = _peer(1)
        for cp in self._local(ins, outs, sems):
            cp.start()
        for k in range(self.n):
            self._copy(outs, sems, k, 0, ins[k], me, sibling).start()
            for j, d in enumerate(self.CHIPS):
                self._copy(outs, sems, k, 1 + j, ins[k], me, _peer(d)[0]).start()

    def mid(self, ins, outs, sems):
        sibling, _ = _peer(1)
        for j, d in enumerate(self.CHIPS):
            chip, block = _peer(d)
            for k in range(self.n):
                self._copy(outs, sems, k, 1 + j, ins[k], block, chip).wait_recv()
                self._copy(outs, sems, k, 4 + j, outs[k].at[block], block, sibling).start()

    def wait(self, ins, outs, sems):
        _, me = _peer(0)
        sibling, sib_block = _peer(1)
        for k in range(self.n):
            self._copy(outs, sems, k, 0, ins[k], sib_block, sibling).wait_recv()
            for j, d in enumerate(self.CHIPS):
                self._copy(outs, sems, k, 4 + j, ins[k], _peer(d ^ 1)[1], sibling).wait_recv()
        for k in range(self.n):
            self._copy(outs, sems, k, 0, ins[k], me, sibling).wait_send()
            for j, d in enumerate(self.CHIPS):
                chip, block = _peer(d)
                self._copy(outs, sems, k, 1 + j, ins[k], me, chip).wait_send()
                self._copy(outs, sems, k, 4 + j, outs[k].at[block], block, sibling).wait_send()
        for cp in self._local(ins, outs, sems):
            cp.wait()


def _exchange(name, arrays, gather, two_level=False):
    ex = _Gather(arrays) if two_level else _Exchange(arrays, gather)
    n = ex.n

    def body(*refs):
        ins, outs, sems = refs[:n], refs[n:2 * n], refs[2 * n:]
        ex.start(ins, outs, sems)
        if two_level:
            ex.mid(ins, outs, sems)
        ex.wait(ins, outs, sems)

    return pl.pallas_call(body, name=name, out_shape=ex.out_shape, in_specs=[ANY] * n, out_specs=[ANY] * n,
                          scratch_shapes=ex.scratch)(*arrays)


def _call(body, *, name, grid, in_specs, out_specs, out_shape, args, scratch_shapes=(), ride=None):
    n_in, n_out, n_sc = len(in_specs), len(out_specs), len(scratch_shapes)
    sem = ("arbitrary",) * len(grid)
    if ride is None:
        res = pl.pallas_call(body, name=name, grid=grid, in_specs=list(in_specs), out_specs=list(out_specs),
                             out_shape=list(out_shape), scratch_shapes=list(scratch_shapes), compiler_params=_params(*sem))(*args)
        return list(res), []
    nx = ride.n

    def riding(*refs):
        ins, xin = refs[:n_in], refs[n_in:n_in + nx]
        outs, xout = refs[n_in + nx:n_in + nx + n_out], refs[n_in + nx + n_out:n_in + 2 * nx + n_out]
        rest = refs[n_in + 2 * nx + n_out:]
        scratch, sems = rest[:n_sc], rest[n_sc:]
        ids = [pl.program_id(ax) for ax in range(len(grid))]
        first = functools.reduce(jnp.logical_and, [i == 0 for i in ids])
        last = functools.reduce(jnp.logical_and, [i == g - 1 for i, g in zip(ids, grid)])

        @pl.when(first)
        def _():
            ride.start(xin, xout, sems)

        if hasattr(ride, "mid"):
            halfway = functools.reduce(jnp.logical_and, [ids[0] == grid[0] // 2] + [i == 0 for i in ids[1:]])

            @pl.when(halfway)
            def _():
                ride.mid(xin, xout, sems)

        body(*ins, *outs, *scratch)

        @pl.when(last)
        def _():
            ride.wait(xin, xout, sems)

    res = pl.pallas_call(
        riding, name=name, grid=grid, in_specs=list(in_specs) + [ANY] * nx, out_specs=list(out_specs) + [ANY] * nx,
        out_shape=list(out_shape) + ride.out_shape, scratch_shapes=list(scratch_shapes) + ride.scratch,
        compiler_params=_params(*sem))(*args, *ride.arrays)
    return list(res[:n_out]), list(res[n_out:])


def _resident(shape):
    return pl.BlockSpec(shape, lambda *_: (0,) * len(shape), pipeline_mode=pl.Buffered(1))


def _proj_fwd(x, g, w, qg, kg, tm, ride=None):
    T, K = x.shape
    N = w.shape[1]
    per = COL // (2 * HEAD_DIM)
    assert Q_END % COL == 0 and V_END == Q_END + COL and (GLU_END - V_END) == 4 * COL and KV_WIDTH == COL // 2

    def body(x_ref, g_ref, w_ref, qg_ref, kg_ref, o_ref, u_ref, qn_ref, kn_ref, vb_ref, h0_ref):
        xv = x_ref[...]
        r = lax.rsqrt(jnp.mean(xv * xv, axis=-1, keepdims=True) + EPS)
        u = (xv * r * g_ref[...]).astype(bf16)
        u_ref[...] = u

        def block(c):
            cs = slice(c * COL, (c + 1) * COL)
            pc = jnp.dot(u, w_ref[:, cs], preferred_element_type=f32)
            o_ref[:, cs] = pc.astype(bf16)
            return pc

        qgv = qg_ref[...] * (HEAD_DIM ** -0.5)
        for c in range(Q_END // COL):
            pc = block(c)
            for t in range(per):
                xq = pc[:, _pair_cols(t)]
                qn_ref[:, _pair_cols(c * per + t)] = (xq * _pair_rstd(xq, False) * qgv).astype(bf16)
        pc = block(Q_END // COL)
        for t in range(KV_WIDTH // (2 * HEAD_DIM)):
            xk = pc[:, _pair_cols(t)]
            kn_ref[:, _pair_cols(t)] = (xk * _pair_rstd(xk, False) * kg_ref[...]).astype(bf16)
        vb_ref[...] = pc[:, KV_WIDTH:].astype(bf16)
        a0 = V_END // COL
        for half in range(2):
            gate = block(a0 + 2 + half)
            h0_ref[:, half * COL:(half + 1) * COL] = block(a0 + half) * _sigmoid(gate)
        for c in range(GLU_END // COL, N // COL):
            block(c)

    row = lambda width: pl.BlockSpec((tm, width), lambda i: (i, 0))
    return _call(
        body, name="proj_fwd", grid=(T // tm,),
        in_specs=[row(K), _resident((1, K)), _resident((K, N)), _resident((1, 2 * HEAD_DIM)), _resident((1, 2 * HEAD_DIM))],
        out_specs=[row(N), row(K), row(ATTN_WIDTH), row(KV_WIDTH), row(KV_WIDTH), row(D_MODEL)],
        out_shape=[S((T, N), bf16), S((T, K), bf16), S((T, ATTN_WIDTH), bf16), S((T, KV_WIDTH), bf16), S((T, KV_WIDTH), bf16),
                   S((T, D_MODEL), f32)],
        args=(x, g, w, qg, kg), ride=ride)


def _bias_table(rel_bias, bucket):
    def body(rb_ref, bk_ref, o_ref):
        b = bk_ref[...]
        absent = lax.broadcasted_iota(jnp.int32, (QBLOCK, 2 * QBLOCK), 1) < QBLOCK
        for h in range(N_Q_HEADS):
            acc = jnp.full((QBLOCK, 2 * QBLOCK), NEG, f32)
            for k in range(N_BUCKETS):
                acc = jnp.where(b == k, rb_ref[k, h], acc)
            o_ref[0, h * QBLOCK:(h + 1) * QBLOCK, :] = acc
            o_ref[1, h * QBLOCK:(h + 1) * QBLOCK, :] = jnp.where(absent, NEG, acc)

    return pl.pallas_call(
        body, name="bias_table", out_shape=S((2, N_Q_HEADS * QBLOCK, 2 * QBLOCK), f32),
        in_specs=[SMEM, pl.BlockSpec(memory_space=pltpu.VMEM)],
    )(rel_bias, bucket)


def _bias_spec():
    return pl.BlockSpec((None, N_Q_HEADS * QBLOCK, 2 * QBLOCK), lambda n: (jnp.where(n == 0, 1, 0), 0, 0))


def _swap_halves(t):
    return jnp.concatenate([t[:, HEAD_DIM:], t[:, :HEAD_DIM]], axis=1)


def _low_lanes():
    return lax.broadcasted_iota(jnp.int32, (1, 2 * HEAD_DIM), 1) < HEAD_DIM


def _one_head(pair, side):
    zero = jnp.zeros((), pair.dtype)
    return jnp.where(_low_lanes(), pair, zero) if side == 0 else jnp.where(_low_lanes(), zero, pair)


def _pair_mean(t, on_mxu):
    if not on_mxu:
        m_lo = jnp.sum(_one_head(t, 0), axis=-1, keepdims=True) * (1.0 / HEAD_DIM)
        m_hi = jnp.sum(_one_head(t, 1), axis=-1, keepdims=True) * (1.0 / HEAD_DIM)
        return jnp.where(_low_lanes(), m_lo, m_hi)
    width = 2 * HEAD_DIM
    same_head = ((lax.broadcasted_iota(jnp.int32, (width, width), 0) < HEAD_DIM)
                 == (lax.broadcasted_iota(jnp.int32, (width, width), 1) < HEAD_DIM))
    e = jnp.where(same_head, 1.0 / HEAD_DIM, 0.0).astype(bf16)
    hi = t.astype(bf16)
    lo = (t - hi.astype(f32)).astype(bf16)
    return jnp.dot(hi, e, preferred_element_type=f32) + jnp.dot(lo, e, preferred_element_type=f32)


def _pair_rstd(x, on_mxu):
    return lax.rsqrt(_pair_mean(x * x, on_mxu) + EPS)


def _kv_placements(band):
    out = {}
    for m in range(N_KV_HEADS // 2):
        pair = band[:, m * 2 * HEAD_DIM:(m + 1) * 2 * HEAD_DIM]
        swapped = _swap_halves(pair)
        for hh in range(2):
            out[2 * m + hh, 0] = _one_head(pair if hh == 0 else swapped, 0)
            out[2 * m + hh, 1] = _one_head(swapped if hh == 0 else pair, 1)
    return out


def _head_rows(hq):
    return slice(hq * QBLOCK, (hq + 1) * QBLOCK)


def _pair_cols(pr):
    return slice(pr * 2 * HEAD_DIM, (pr + 1) * 2 * HEAD_DIM)


def _attn_fwd(qn, kn, vb, bias, sinks, ride=None):
    T = qn.shape[0]
    nb = T // QBLOCK

    def body(q_ref, kc_ref, kp_ref, vc_ref, vp_ref, b_ref, s_ref, o_ref, lse_ref, s_scr, p_scr):
        lane = lax.broadcasted_iota(jnp.int32, (QBLOCK, 2 * HEAD_DIM), 1)
        kx = _kv_placements(jnp.concatenate([kp_ref[...], kc_ref[...]], axis=0))
        vx = _kv_placements(jnp.concatenate([vp_ref[...], vc_ref[...]], axis=0))
        for hq in range(N_Q_HEADS):
            qm = _one_head(q_ref[:, _pair_cols(hq // 2)], hq % 2)
            s_scr[_head_rows(hq), :] = _nt(qm, kx[hq // GROUP, hq % 2]) + b_ref[_head_rows(hq), :]
        lse_tile = jnp.zeros((QBLOCK, 2 * HEAD_DIM), f32)
        for hq in range(N_Q_HEADS):
            s = s_scr[_head_rows(hq), :]
            sink = s_ref[0, hq]
            m = jnp.maximum(jnp.max(s, axis=-1, keepdims=True), sink)
            p = jnp.exp(s - m)
            l = jnp.sum(p, axis=-1, keepdims=True) + jnp.exp(sink - m)
            p_scr[_head_rows(hq), :] = (p * (1.0 / l)).astype(bf16)
            lse_tile = jnp.where(lane == hq, m + jnp.log(l), lse_tile)
        lse_ref[...] = lse_tile
        for pr in range(N_Q_HEADS // 2):
            h = 2 * pr // GROUP
            o_pair = (jnp.dot(p_scr[_head_rows(2 * pr), :], vx[h, 0], preferred_element_type=f32)
                      + jnp.dot(p_scr[_head_rows(2 * pr + 1), :], vx[h, 1], preferred_element_type=f32))
            o_ref[:, _pair_cols(pr)] = o_pair.astype(bf16)

    cur = lambda n: (n, 0)
    prev = lambda n: (jnp.maximum(n - 1, 0), 0)
    return _call(
        body, name="attn_fwd", grid=(nb,),
        in_specs=[pl.BlockSpec((QBLOCK, ATTN_WIDTH), cur), pl.BlockSpec((QBLOCK, KV_WIDTH), cur),
                  pl.BlockSpec((QBLOCK, KV_WIDTH), prev), pl.BlockSpec((QBLOCK, KV_WIDTH), cur),
                  pl.BlockSpec((QBLOCK, KV_WIDTH), prev), _bias_spec(), SMEM],
        out_specs=[pl.BlockSpec((QBLOCK, ATTN_WIDTH), cur), pl.BlockSpec((QBLOCK, 2 * HEAD_DIM), cur)],
        out_shape=[S((T, ATTN_WIDTH), bf16), S((T, 2 * HEAD_DIM), f32)],
        scratch_shapes=[pltpu.VMEM((N_Q_HEADS * QBLOCK, 2 * QBLOCK), f32), pltpu.VMEM((N_Q_HEADS * QBLOCK, 2 * QBLOCK), bf16)],
        args=(qn, kn, kn, vb, vb, bias, sinks), ride=ride)


def _layer_norm_stats(h1):
    mu = jnp.mean(h1, axis=-1, keepdims=True)
    xc = h1 - mu
    rstd = lax.rsqrt(jnp.mean(xc * xc, axis=-1, keepdims=True) + EPS)
    return xc * rstd, rstd


def _advanced_windows(win):
    rows = win.shape[0]
    for r in range(8):
        yield r, (win if r == 0 else pltpu.roll(win, rows - r, 0))


def _tap_offsets(r, rows):
    for q in range((rows - CONV_UNIT) // 8 + 1):
        if r == 0 or 8 * q + r + CONV_UNIT <= rows:
            yield q, 8 * q + r


def _conv_fwd(h0, w_dw, b_dw, ln_g, ln_b, tm, ride=None):
    T = h0.shape[0]
    per = tm // CONV_HALO
    lead = CONV_HALO - (CONV_WIDTH - 1)

    def body(hc_ref, hp_ref, w_ref, b_ref, g_ref, bb_ref, h1_ref, h3_ref, cat):
        i = pl.program_id(0)
        cat[0:CONV_HALO, :] = jnp.where(i == 0, 0.0, hp_ref[...])
        cat[CONV_HALO:, :] = hc_ref[...]

        def unit_rows(c, carry):
            r0 = pl.multiple_of(c * CONV_UNIT, CONV_UNIT)
            for j in range(D_MODEL // 128):
                ls = slice(j * 128, (j + 1) * 128)
                win = cat[pl.ds(r0, CONV_UNIT + CONV_HALO), ls]
                acc = jnp.zeros((CONV_UNIT, 128), f32) + b_ref[:, ls]
                for r, adv in _advanced_windows(win):
                    for q, off in _tap_offsets(r, CONV_UNIT + CONV_HALO):
                        k = off - lead
                        if 0 <= k < CONV_WIDTH:
                            acc = acc + adv[8 * q:8 * q + CONV_UNIT] * w_ref[k:k + 1, ls]
                h1_ref[pl.ds(r0, CONV_UNIT), ls] = acc
            return carry

        lax.fori_loop(0, tm // CONV_UNIT, unit_rows, 0)
        acc = h1_ref[...]
        xhat, _ = _layer_norm_stats(acc)
        h2 = xhat * g_ref[...] + bb_ref[...]
        h3_ref[...] = (h2 * _sigmoid(h2)).astype(bf16)

    vec = pl.BlockSpec((1, D_MODEL), lambda i: (0, 0))
    return _call(
        body, name="conv_fwd", grid=(T // tm,),
        in_specs=[pl.BlockSpec((tm, D_MODEL), lambda i: (i, 0)),
                  pl.BlockSpec((CONV_HALO, D_MODEL), lambda i: (jnp.maximum(i * per - 1, 0), 0)),
                  pl.BlockSpec((CONV_WIDTH, D_MODEL), lambda i: (0, 0)), vec, vec, vec],
        out_specs=[pl.BlockSpec((tm, D_MODEL), lambda i: (i, 0)), pl.BlockSpec((tm, D_MODEL), lambda i: (i, 0))],
        out_shape=[S((T, D_MODEL), f32), S((T, D_MODEL), bf16)],
        scratch_shapes=[pltpu.VMEM((tm + CONV_HALO, D_MODEL), f32)],
        args=(h0, h0, w_dw, b_dw, ln_g, ln_b), ride=ride)


def _mix_fwd(x, o, h3, proj, w_ao, w_co, w_o, tm):
    T = x.shape[0]
    row = pl.BlockSpec((tm, D_MODEL), lambda i: (i, 0))
    wsp = _resident((D_MODEL, D_MODEL))
    g0 = GLU_END // COL

    def gate_spec(off):
        return pl.BlockSpec((tm, COL), lambda i: (i, g0 + off))

    def body(x_ref, o_ref, h3_ref, ga0, ga1, gc0, gc1, wa_ref, wc_ref, wo_ref, x1_ref, at_ref, cv_ref, mg_ref):
        attn = jnp.dot(o_ref[...], wa_ref[...], preferred_element_type=f32)
        conv = jnp.dot(h3_ref[...], wc_ref[...], preferred_element_type=f32)
        ga = jnp.concatenate([ga0[...], ga1[...]], axis=-1).astype(f32)
        gc = jnp.concatenate([gc0[...], gc1[...]], axis=-1).astype(f32)
        merged = (_sigmoid(ga) * attn + _sigmoid(gc) * conv).astype(bf16)
        at_ref[...] = attn.astype(bf16)
        cv_ref[...] = conv.astype(bf16)
        mg_ref[...] = merged
        x1_ref[...] = x_ref[...] + jnp.dot(merged, wo_ref[...], preferred_element_type=f32)

    return pl.pallas_call(
        body, name="mix_fwd", grid=(T // tm,),
        in_specs=[row, row, row, gate_spec(0), gate_spec(1), gate_spec(2), gate_spec(3), wsp, wsp, wsp],
        out_specs=[row, row, row, row],
        out_shape=[S((T, D_MODEL), f32), S((T, D_MODEL), bf16), S((T, D_MODEL), bf16), S((T, D_MODEL), bf16)],
        compiler_params=_params("parallel"),
    )(x, o, h3, proj, proj, proj, proj, w_ao, w_co, w_o)


def _ffn_fwd(x1, g, w1, w2, target, tm):
    T = x1.shape[0]
    nj = w1.shape[0] // FF_CHUNK

    def body(x_ref, g_ref, w1_ref, w2_ref, t_ref, a_ref, u_ref, dy_ref, dyb_ref, ls_ref, hm):
        xv = x_ref[...]
        r = lax.rsqrt(jnp.mean(xv * xv, axis=-1, keepdims=True) + EPS)
        u = (xv * r * g_ref[...]).astype(bf16)
        u_ref[...] = u
        for j in range(nj):
            js = slice(j * FF_CHUNK, (j + 1) * FF_CHUNK)
            a = _nt(u, w1_ref[js, :])
            a_ref[:, js] = a.astype(bf16)
            hm[:, js] = jnp.square(jnp.maximum(a, 0.0)).astype(bf16)
        err = xv + jnp.dot(hm[...], w2_ref[...], preferred_element_type=f32) - t_ref[...]
        dy = err * (1.0 / D_MODEL)
        dy_ref[...] = dy
        dyb_ref[...] = dy.astype(bf16)
        ls_ref[...] = jnp.zeros((8, 128), f32) + jnp.sum(err * err) * (0.5 / D_MODEL)

    row = pl.BlockSpec((tm, D_MODEL), lambda i: (i, 0))
    wide = pl.BlockSpec((tm, D_FF), lambda i: (i, 0))
    return pl.pallas_call(
        body, name="ffn_fwd", grid=(T // tm,),
        in_specs=[row, _resident((1, D_MODEL)), _resident(w1.shape), _resident(w2.shape), row],
        out_specs=[wide, row, row, row, pl.BlockSpec((None, 8, 128), lambda i: (i, 0, 0))],
        out_shape=[S((T, D_FF), bf16), S((T, D_MODEL), bf16), S((T, D_MODEL), f32), S((T, D_MODEL), bf16),
                   S((T // tm, 8, 128), f32)],
        scratch_shapes=[pltpu.VMEM((tm, D_FF), bf16)],
        compiler_params=_params("parallel"),
    )(x1, g, w1, w2, target)


def _rms_bwd(du, xv, gv):
    r = lax.rsqrt(jnp.mean(xv * xv, axis=-1, keepdims=True) + EPS)
    xn = xv * r
    dg = jnp.sum(du * xn, axis=0, keepdims=True)
    dxn = du * gv
    dx = r * (dxn - xn * jnp.mean(dxn * xn, axis=-1, keepdims=True))
    return dx, dg


def _ffn_bwd(dy, dyb, a, x1, g, w1, w2, tm, ride=None):
    T = dy.shape[0]
    nj = w1.shape[0] // FF_CHUNK

    def body(dy_ref, dyb_ref, a_ref, x_ref, g_ref, w1_ref, w2_ref, da_ref, dx_ref, dxb_ref, dg_ref):
        @pl.when(pl.program_id(0) == 0)
        def _():
            dg_ref[...] = jnp.zeros_like(dg_ref)

        dyb_v = dyb_ref[...]
        for j in range(nj):
            js = slice(j * FF_CHUNK, (j + 1) * FF_CHUNK)
            dh = _nt(dyb_v, w2_ref[js, :])
            da_ref[:, js] = (dh * (2.0 * jnp.maximum(a_ref[:, js].astype(f32), 0.0))).astype(bf16)
        du = jnp.dot(da_ref[...], w1_ref[...], preferred_element_type=f32)
        dx, dg = _rms_bwd(du, x_ref[...], g_ref[...])
        dx1 = dy_ref[...] + dx
        dx_ref[...] = dx1
        dxb_ref[...] = dx1.astype(bf16)
        dg_ref[...] += dg

    row = pl.BlockSpec((tm, D_MODEL), lambda i: (i, 0))
    wide = pl.BlockSpec((tm, D_FF), lambda i: (i, 0))
    vec = pl.BlockSpec((1, D_MODEL), lambda i: (0, 0))
    return _call(
        body, name="ffn_bwd", grid=(T // tm,),
        in_specs=[row, row, wide, row, _resident((1, D_MODEL)), _resident(w1.shape), _resident(w2.shape)],
        out_specs=[wide, row, row, vec],
        out_shape=[S((T, D_FF), bf16), S((T, D_MODEL), f32), S((T, D_MODEL), bf16), S((1, D_MODEL), f32)],
        args=(dy, dyb, a, x1, g, w1, w2), ride=ride)


def _wgrad(name, a, b, tk, tn, tt, relu2=False, slab=None, out_dtype=bf16):
    T, Ka = a.shape
    Nb = b.shape[1]
    nt = T // tt

    def body(a_ref, b_ref, o_ref, acc):
        t = pl.program_id(2)
        av = a_ref[...]
        if relu2:
            av = jnp.square(jnp.maximum(av.astype(f32), 0.0))
        prod = _tn(av.astype(bf16), b_ref[...].astype(bf16))

        @pl.when(t == 0)
        def _():
            acc[...] = prod

        @pl.when(t > 0)
        def _():
            acc[...] += prod

        @pl.when(t == nt - 1)
        def _():
            if slab is None:
                o_ref[...] = acc[...].astype(out_dtype)
            else:
                for s in range(tn // slab):
                    o_ref[s] = acc[:, s * slab:(s + 1) * slab].astype(out_dtype)

    if slab is not None:
        out_shape = S((Nb // slab, Ka, slab), out_dtype)
        out_spec = pl.BlockSpec((tn // slab, tk, slab), lambda i, j, t: (j, i, 0))
    else:
        out_shape = S((Ka, Nb), out_dtype)
        out_spec = pl.BlockSpec((tk, tn), lambda i, j, t: (i, j))
    return pl.pallas_call(
        body, name=name, grid=(Ka // tk, Nb // tn, nt),
        in_specs=[pl.BlockSpec((tt, tk), lambda i, j, t: (t, i)), pl.BlockSpec((tt, tn), lambda i, j, t: (t, j))],
        out_specs=out_spec, out_shape=out_shape, scratch_shapes=[pltpu.VMEM((tk, tn), f32)],
        compiler_params=_params("parallel", "parallel", "arbitrary"),
    )(a, b)


def _mix_bwd(dx1, proj, attn, conv, h1, ln_g, ln_b, w_ao, w_co, w_o, tm, ride=None):
    T = dx1.shape[0]
    g0 = GLU_END // COL

    def gate_spec(off):
        return pl.BlockSpec((tm, COL), lambda i: (i, g0 + off))

    def body(dx_ref, ga0, ga1, gc0, gc1, at_ref, cv_ref, h_ref, g_ref, b_ref, wa_ref, wc_ref, wo_ref,
             da_ref, dc_ref, do_ref, dh1_ref, dg_ref, dlg_ref, dlb_ref, dbd_ref):
        @pl.when(pl.program_id(0) == 0)
        def _():
            dlg_ref[...] = jnp.zeros_like(dlg_ref)
            dlb_ref[...] = jnp.zeros_like(dlb_ref)
            dbd_ref[...] = jnp.zeros_like(dbd_ref)

        dm = _nt(dx_ref[...].astype(bf16), wo_ref[...])
        sa = _sigmoid(jnp.concatenate([ga0[...], ga1[...]], axis=-1).astype(f32))
        sc = _sigmoid(jnp.concatenate([gc0[...], gc1[...]], axis=-1).astype(f32))
        dattn = (dm * sa).astype(bf16)
        dconv = (dm * sc).astype(bf16)
        da_ref[...] = dattn
        dc_ref[...] = dconv
        dg_ref[:, 0:D_MODEL] = (dm * at_ref[...].astype(f32) * sa * (1.0 - sa)).astype(bf16)
        dg_ref[:, D_MODEL:2 * D_MODEL] = (dm * cv_ref[...].astype(f32) * sc * (1.0 - sc)).astype(bf16)
        do_ref[...] = _nt(dattn, wa_ref[...]).astype(bf16)
        dh3 = _nt(dconv, wc_ref[...])
        xhat, rstd = _layer_norm_stats(h_ref[...])
        h2 = xhat * g_ref[...] + b_ref[...]
        sg = _sigmoid(h2)
        dh2 = dh3 * (sg * (1.0 + h2 * (1.0 - sg)))
        dlg_ref[...] += jnp.sum(dh2 * xhat, axis=0, keepdims=True)
        dlb_ref[...] += jnp.sum(dh2, axis=0, keepdims=True)
        dxh = dh2 * g_ref[...]
        dh1 = rstd * (dxh - jnp.mean(dxh, axis=-1, keepdims=True) - xhat * jnp.mean(dxh * xhat, axis=-1, keepdims=True))
        dh1_ref[...] = dh1
        dbd_ref[...] += jnp.sum(dh1, axis=0, keepdims=True)

    row = pl.BlockSpec((tm, D_MODEL), lambda i: (i, 0))
    vec = pl.BlockSpec((1, D_MODEL), lambda i: (0, 0))
    par = _resident((1, D_MODEL))
    wsp = _resident((D_MODEL, D_MODEL))
    return _call(
        body, name="mix_bwd", grid=(T // tm,),
        in_specs=[row, gate_spec(0), gate_spec(1), gate_spec(2), gate_spec(3), row, row, row, par, par, wsp, wsp, wsp],
        out_specs=[row, row, row, row, pl.BlockSpec((tm, 2 * D_MODEL), lambda i: (i, 0)), vec, vec, vec],
        out_shape=[S((T, D_MODEL), bf16), S((T, D_MODEL), bf16), S((T, D_MODEL), bf16), S((T, D_MODEL), f32),
                   S((T, 2 * D_MODEL), bf16), S((1, D_MODEL), f32), S((1, D_MODEL), f32), S((1, D_MODEL), f32)],
        args=(dx1, proj, proj, proj, proj, attn, conv, h1, ln_g, ln_b, w_ao, w_co, w_o), ride=ride)


def _conv_bwd(dh1, h0, proj, w_dw, tm, ride=None):
    T = dh1.shape[0]
    per = tm // CONV_HALO
    nh = T // CONV_HALO
    nt = T // tm
    a0 = V_END // COL
    lead = CONV_HALO - (CONV_WIDTH - 1)

    def body(dc_ref, dn_ref, hc_ref, hp_ref, a0_ref, a1_ref, g0_ref, g1_ref, w_ref, dglu_ref, dw_ref, dcat, hcat, wacc, dh0):
        i = pl.program_id(0)

        @pl.when(i == 0)
        def _():
            wacc[...] = jnp.zeros_like(wacc)

        dcat[0:tm, :] = dc_ref[...]
        dcat[tm:, :] = jnp.where(i == nt - 1, 0.0, dn_ref[...])
        hcat[0:CONV_HALO, :] = jnp.where(i == 0, 0.0, hp_ref[...])
        hcat[CONV_HALO:, :] = hc_ref[...]
        span = CONV_UNIT + CONV_HALO

        def unit_rows(c, carry):
            r0 = pl.multiple_of(c * CONV_UNIT, CONV_UNIT)
            for j in range(D_MODEL // 128):
                ls = slice(j * 128, (j + 1) * 128)
                dwin = dcat[pl.ds(r0, span), ls]
                acc = jnp.zeros((CONV_UNIT, 128), f32)
                for r, adv in _advanced_windows(dwin):
                    for q, off in _tap_offsets(r, span):
                        k = CONV_WIDTH - 1 - off
                        if 0 <= k < CONV_WIDTH:
                            acc = acc + adv[8 * q:8 * q + CONV_UNIT] * w_ref[k:k + 1, ls]
                dh0[pl.ds(r0, CONV_UNIT), ls] = acc
                dcur = dwin[0:CONV_UNIT]
                for r, adv in _advanced_windows(hcat[pl.ds(r0, span), ls]):
                    for q, off in _tap_offsets(r, span):
                        k = off - lead
                        if 0 <= k < CONV_WIDTH:
                            prod = dcur * adv[8 * q:8 * q + CONV_UNIT]
                            wacc[k, :, ls] += jnp.sum(prod.reshape(CONV_UNIT // 8, 8, 128), axis=0)
            return carry

        lax.fori_loop(0, tm // CONV_UNIT, unit_rows, 0)
        dh0v = dh0[...]
        av = jnp.concatenate([a0_ref[...], a1_ref[...]], axis=-1).astype(f32)
        sg = _sigmoid(jnp.concatenate([g0_ref[...], g1_ref[...]], axis=-1).astype(f32))
        dglu_ref[:, 0:D_MODEL] = (dh0v * sg).astype(bf16)
        dglu_ref[:, D_MODEL:2 * D_MODEL] = (dh0v * av * sg * (1.0 - sg)).astype(bf16)

        @pl.when(i == nt - 1)
        def _():
            for k in range(CONV_WIDTH):
                dw_ref[k:k + 1, :] = jnp.sum(wacc[k], axis=0, keepdims=True)
            dw_ref[CONV_WIDTH:CONV_WIDTH + 1, :] = jnp.zeros((1, D_MODEL), f32)

    row = pl.BlockSpec((tm, D_MODEL), lambda i: (i, 0))

    def col_spec(off):
        return pl.BlockSpec((tm, COL), lambda i: (i, a0 + off))

    return _call(
        body, name="conv_bwd", grid=(nt,),
        in_specs=[row, pl.BlockSpec((CONV_HALO, D_MODEL), lambda i: (jnp.minimum((i + 1) * per, nh - 1), 0)),
                  row, pl.BlockSpec((CONV_HALO, D_MODEL), lambda i: (jnp.maximum(i * per - 1, 0), 0)),
                  col_spec(0), col_spec(1), col_spec(2), col_spec(3),
                  pl.BlockSpec((CONV_WIDTH, D_MODEL), lambda i: (0, 0))],
        out_specs=[pl.BlockSpec((tm, 2 * D_MODEL), lambda i: (i, 0)), pl.BlockSpec((CONV_WIDTH + 1, D_MODEL), lambda i: (0, 0))],
        out_shape=[S((T, 2 * D_MODEL), bf16), S((CONV_WIDTH + 1, D_MODEL), f32)],
        scratch_shapes=[pltpu.VMEM((tm + CONV_HALO, D_MODEL), f32), pltpu.VMEM((tm + CONV_HALO, D_MODEL), f32),
                        pltpu.VMEM((CONV_WIDTH, 8, D_MODEL), f32), pltpu.VMEM((tm, D_MODEL), f32)],
        args=(dh1, dh1, h0, h0, proj, proj, proj, proj, w_dw), ride=ride)


def _attn_bwd(qn, kn, vb, o, do, lse, bias, sinks, ride=None):
    T = qn.shape[0]
    nb = T // QBLOCK

    def body(q_ref, kc_ref, kp_ref, vc_ref, vp_ref, o_ref, do_ref, lse_ref, b_ref, s_ref,
             dq_ref, dk_ref, dv_ref, db_ref, dsk_ref, kcar, vcar, s_scr, dp_scr, p_scr, ds_scr):
        n = pl.program_id(0)

        @pl.when(n == 0)
        def _():
            db_ref[...] = jnp.zeros_like(db_ref)
            dsk_ref[...] = jnp.zeros_like(dsk_ref)
            kcar[...] = jnp.zeros_like(kcar)
            vcar[...] = jnp.zeros_like(vcar)

        @pl.when(n < nb)
        def _():
            lane = lax.broadcasted_iota(jnp.int32, (QBLOCK, 2 * HEAD_DIM), 1)
            lane_row = lax.broadcasted_iota(jnp.int32, (1, 2 * HEAD_DIM), 1)
            kx = _kv_placements(jnp.concatenate([kp_ref[...], kc_ref[...]], axis=0))
            vx = _kv_placements(jnp.concatenate([vp_ref[...], vc_ref[...]], axis=0))
            lse_tile = lse_ref[...]
            delta, lse_c = {}, {}
            for pr in range(N_Q_HEADS // 2):
                dop = do_ref[:, _pair_cols(pr)]
                dl = dop.astype(f32) * o_ref[:, _pair_cols(pr)].astype(f32)
                for side in range(2):
                    hq = 2 * pr + side
                    h = hq // GROUP
                    qm = _one_head(q_ref[:, _pair_cols(pr)], side)
                    s_scr[_head_rows(hq), :] = _nt(qm, kx[h, side]) + b_ref[_head_rows(hq), :]
                    dp_scr[_head_rows(hq), :] = _nt(_one_head(dop, side), vx[h, side])
                    delta[hq] = jnp.sum(_one_head(dl, side), axis=-1, keepdims=True)
                    lse_c[hq] = jnp.sum(jnp.where(lane == hq, lse_tile, 0.0), axis=-1, keepdims=True)
            dsk = jnp.zeros((1, 2 * HEAD_DIM), f32)
            for hq in range(N_Q_HEADS):
                p = jnp.exp(s_scr[_head_rows(hq), :] - lse_c[hq])
                ds = p * (dp_scr[_head_rows(hq), :] - delta[hq])
                db_ref[_head_rows(hq), :] += ds
                p_scr[_head_rows(hq), :] = p.astype(bf16)
                ds_scr[_head_rows(hq), :] = ds.astype(bf16)
                psink = jnp.exp(s_ref[0, hq] - lse_c[hq])
                dsk = dsk - jnp.where(lane_row == hq, jnp.sum(psink * delta[hq], axis=0, keepdims=True), 0.0)
            dsk_ref[...] += dsk
            for pr in range(N_Q_HEADS // 2):
                h = 2 * pr // GROUP
                dq_ref[:, _pair_cols(pr)] = (jnp.dot(ds_scr[_head_rows(2 * pr), :], kx[h, 0], preferred_element_type=f32)
                                             + jnp.dot(ds_scr[_head_rows(2 * pr + 1), :], kx[h, 1], preferred_element_type=f32))
            folded_k, folded_v = [], []
            for h in range(N_KV_HEADS):
                ka = jnp.zeros((2 * QBLOCK, 2 * HEAD_DIM), f32)
                va = jnp.zeros((2 * QBLOCK, 2 * HEAD_DIM), f32)
                for g in range(GROUP):
                    hq = h * GROUP + g
                    ka = ka + _tn(ds_scr[_head_rows(hq), :], _one_head(q_ref[:, _pair_cols(hq // 2)], hq % 2))
                    va = va + _tn(p_scr[_head_rows(hq), :], _one_head(do_ref[:, _pair_cols(hq // 2)], hq % 2))
                folded_k.append(ka + _swap_halves(ka))
                folded_v.append(va + _swap_halves(va))
            low = _low_lanes()
            for m in range(N_KV_HEADS // 2):
                cs = _pair_cols(m)
                for folded, out_ref, car in ((folded_k, dk_ref, kcar), (folded_v, dv_ref, vcar)):
                    band = jnp.where(low, folded[2 * m], folded[2 * m + 1])
                    out_ref[:, cs] = car[:, cs] + band[0:QBLOCK, :]
                    car[:, cs] = band[QBLOCK:, :]

        @pl.when(n == nb)
        def _():
            dk_ref[...] = kcar[...]
            dv_ref[...] = vcar[...]

    cur = lambda n: (jnp.minimum(n, nb - 1), 0)
    prev = lambda n: (jnp.clip(n - 1, 0, nb - 1), 0)
    qspec = pl.BlockSpec((QBLOCK, ATTN_WIDTH), cur)
    kcur, kprev = pl.BlockSpec((QBLOCK, KV_WIDTH), cur), pl.BlockSpec((QBLOCK, KV_WIDTH), prev)
    whole = lambda shape: pl.BlockSpec(shape, lambda n: (0,) * len(shape))
    scores = (N_Q_HEADS * QBLOCK, 2 * QBLOCK)
    return _call(
        body, name="attn_bwd", grid=(nb + 1,),
        in_specs=[qspec, kcur, kprev, kcur, kprev, qspec, qspec, pl.BlockSpec((QBLOCK, 2 * HEAD_DIM), cur), _bias_spec(), SMEM],
        out_specs=[qspec, kprev, kprev, whole(scores), whole((1, 2 * HEAD_DIM))],
        out_shape=[S((T, ATTN_WIDTH), f32), S((T, KV_WIDTH), f32), S((T, KV_WIDTH), f32), S(scores, f32),
                   S((1, 2 * HEAD_DIM), f32)],
        scratch_shapes=[pltpu.VMEM((QBLOCK, KV_WIDTH), f32), pltpu.VMEM((QBLOCK, KV_WIDTH), f32),
                        pltpu.VMEM(scores, f32), pltpu.VMEM(scores, f32), pltpu.VMEM(scores, bf16), pltpu.VMEM(scores, bf16)],
        args=(qn, kn, kn, vb, vb, o, do, lse, bias, sinks), ride=ride)


def _rel_bias_bwd(dbias, bucket):
    def body(d_ref, bk_ref, o_ref):
        b = bk_ref[...]
        for k in range(N_BUCKETS):
            mk = b == k
            for h in range(N_Q_HEADS):
                o_ref[k, h] = jnp.sum(jnp.where(mk, d_ref[h * QBLOCK:(h + 1) * QBLOCK, :], 0.0))

    return pl.pallas_call(body, name="rel_bias_bwd", out_shape=S((N_BUCKETS, N_Q_HEADS), f32), out_specs=SMEM)(dbias, bucket)


def _qk_norm_bwd(dq, dk, dv, proj, qg, kg, tm):
    T = dq.shape[0]
    scale = HEAD_DIM ** -0.5

    def pair_bwd(dy, x, gv):
        r = _pair_rstd(x, True)
        xn = x * r
        dxn = dy * gv
        dx = r * (dxn - xn * _pair_mean(dxn * xn, True))
        return dx, jnp.sum(dy * xn, axis=0, keepdims=True)

    def body(dq_ref, dk_ref, dv_ref, p_ref, qg_ref, kg_ref, out_ref, dqg_ref, dkg_ref):
        @pl.when(pl.program_id(0) == 0)
        def _():
            dqg_ref[...] = jnp.zeros_like(dqg_ref)
            dkg_ref[...] = jnp.zeros_like(dkg_ref)

        qgv, kgv = qg_ref[...], kg_ref[...]
        dqg = jnp.zeros((1, 2 * HEAD_DIM), f32)
        for pr in range(N_Q_HEADS // 2):
            dx, dg = pair_bwd(dq_ref[:, _pair_cols(pr)] * scale, p_ref[:, _pair_cols(pr)].astype(f32), qgv)
            out_ref[:, _pair_cols(pr)] = dx.astype(bf16)
            dqg = dqg + dg
        dkg = jnp.zeros((1, 2 * HEAD_DIM), f32)
        for pr in range(N_KV_HEADS // 2):
            ps = slice(Q_END + pr * 2 * HEAD_DIM, Q_END + (pr + 1) * 2 * HEAD_DIM)
            dx, dg = pair_bwd(dk_ref[:, _pair_cols(pr)], p_ref[:, ps].astype(f32), kgv)
            out_ref[:, ps] = dx.astype(bf16)
            dkg = dkg + dg
        out_ref[:, K_END:V_END] = dv_ref[...].astype(bf16)
        dqg_ref[...] += dqg
        dkg_ref[...] += dkg

    vec = pl.BlockSpec((1, 2 * HEAD_DIM), lambda i: (0, 0))
    return pl.pallas_call(
        body, name="qk_norm_bwd", grid=(T // tm,),
        in_specs=[pl.BlockSpec((tm, ATTN_WIDTH), lambda i: (i, 0)), pl.BlockSpec((tm, KV_WIDTH), lambda i: (i, 0)),
                  pl.BlockSpec((tm, KV_WIDTH), lambda i: (i, 0)), pl.BlockSpec((tm, V_END), lambda i: (i, 0)), vec, vec],
        out_specs=[pl.BlockSpec((tm, V_END), lambda i: (i, 0)), vec, vec],
        out_shape=[S((T, V_END), bf16), S((1, 2 * HEAD_DIM), f32), S((1, 2 * HEAD_DIM), f32)],
        compiler_params=_params("arbitrary"),
    )(dq, dk, dv, proj, qg, kg)


def _in_bwd(dqkv, dglu, dgates, w_in, x, g, dx1, tm, ride=None):
    T = x.shape[0]
    pieces = (dqkv, dglu, dgates)
    starts = [0, dqkv.shape[1], dqkv.shape[1] + dglu.shape[1]]

    def body(a0_ref, a1_ref, a2_ref, w_ref, x_ref, g_ref, d_ref, gx_ref, dg_ref):
        @pl.when(pl.program_id(0) == 0)
        def _():
            dg_ref[...] = jnp.zeros_like(dg_ref)

        du = jnp.zeros((tm, D_MODEL), f32)
        for a_ref, c0 in zip((a0_ref, a1_ref, a2_ref), starts):
            du = du + _nt(a_ref[...], w_ref[:, c0:c0 + a_ref.shape[1]])
        dx, dg = _rms_bwd(du, x_ref[...], g_ref[...])
        gx_ref[...] = d_ref[...] + dx
        dg_ref[...] += dg

    row = pl.BlockSpec((tm, D_MODEL), lambda i: (i, 0))
    return _call(
        body, name="in_bwd", grid=(T // tm,),
        in_specs=[pl.BlockSpec((tm, p.shape[1]), lambda i: (i, 0)) for p in pieces]
        + [_resident(w_in.shape), row, _resident((1, D_MODEL)), row],
        out_specs=[row, pl.BlockSpec((1, D_MODEL), lambda i: (0, 0))],
        out_shape=[S((T, D_MODEL), f32), S((1, D_MODEL), f32)],
        args=(dqkv, dglu, dgates, w_in, x, g, dx1), ride=ride)


def _adamw(name, parts, w, m, v, tr, other=None):
    _, R, C = w.shape
    bc1 = 1.0 - ADAM_B1 ** ADAM_STEP
    bc2 = 1.0 - ADAM_B2 ** ADAM_STEP

    def body(*refs):
        if other is None:
            p_ref, w_ref, m_ref, v_ref, g_ref, d_ref, nm_ref, nv_ref = refs
            part = lambda k: p_ref[k].astype(f32)
        else:
            p_ref, p2_ref, t_ref, w_ref, m_ref, v_ref, g_ref, d_ref, nm_ref, nv_ref = refs
            take2 = t_ref[0:1, 0:1] > 0.5
            part = lambda k: jnp.where(take2, p2_ref[k], p_ref[k]).astype(f32)
        g = part(0)
        for k in range(1, N_DEV):
            g = g + part(k)
        nm = ADAM_B1 * m_ref[...] + (1.0 - ADAM_B1) * g
        nv = ADAM_B2 * v_ref[...] + (1.0 - ADAM_B2) * (g * g)
        g_ref[...] = g
        nm_ref[...] = nm
        nv_ref[...] = nv
        d_ref[...] = -ADAM_LR * ((nm / bc1) / (jnp.sqrt(nv / bc2) + ADAM_EPS) + ADAM_WD * w_ref[...])

    blk = pl.BlockSpec((None, tr, C), lambda i: (0, i, 0))
    slabs = pl.BlockSpec((N_DEV, tr, C), lambda i: (0, i, 0))
    if other is None:
        lead_specs, lead_args = [slabs], (parts,)
    else:
        lead_specs = [slabs, slabs, pl.BlockSpec((1, 2 * HEAD_DIM), lambda i: (0, 0))]
        lead_args = (parts, other[0], other[1])
    return pl.pallas_call(
        body, name=name, grid=(R // tr,),
        in_specs=lead_specs + [blk, blk, blk],
        out_specs=[blk, blk, blk, blk], out_shape=[S((1, R, C), f32)] * 4,
        compiler_params=_params("parallel"),
    )(*lead_args, w, m, v)


def _tile(T, pref):
    return min(T, pref)


def _pad_rows(a, rows):
    return jnp.pad(a, ((0, rows - a.shape[0]), (0, 0)))


def kernel(x, norm_mix_g, w_in, q_norm_g, k_norm_g, attn_sinks, rel_bias, w_attn_o, w_dw, b_dw, conv_ln_g, conv_ln_b, w_conv_out, w_out, norm_mlp_g, w_ff1, w_ff2, loss_target, m_norm_mix_g, m_w_in, m_q_norm_g, m_k_norm_g, m_attn_sinks, m_rel_bias, m_w_attn_o, m_w_dw, m_b_dw, m_conv_ln_g, m_conv_ln_b, m_w_conv_out, m_w_out, m_norm_mlp_g, m_w_ff1, m_w_ff2, v_norm_mix_g, v_w_in, v_q_norm_g, v_k_norm_g, v_attn_sinks, v_rel_bias, v_w_attn_o, v_w_dw, v_b_dw, v_conv_ln_g, v_conv_ln_b, v_w_conv_out, v_w_out, v_norm_mlp_g, v_w_ff1, v_w_ff2):
    T = x.shape[1]
    xs = x[0]
    tgt = loss_target[0]
    in_shard = IN_WIDTH // N_DEV
    dw_rows = CONV_WIDTH + 1
    ch_shard = D_MODEL // N_DEV
    tb = _tile(T, 512)
    tt = _tile(T, 2048)
    bucket = jnp.asarray(_t5_bucket_table())

    g_in, g_dw = _exchange("gather_w_in", [w_in[0].astype(bf16), _pad_rows(w_dw[0], dw_rows)], gather=True, two_level=True)
    W_in = jnp.transpose(g_in, (1, 0, 2)).reshape(D_MODEL, IN_WIDTH)
    W_dw = jnp.transpose(g_dw, (1, 0, 2)).reshape(dw_rows, D_MODEL)[:CONV_WIDTH]

    mix_shards = _Gather([w_attn_o[0].astype(bf16), w_conv_out[0].astype(bf16), w_out[0].astype(bf16)])
    qg2, kg2 = jnp.tile(q_norm_g, (1, 2)), jnp.tile(k_norm_g, (1, 2))
    (proj, u, qn, kn, vb, h0), (g_ao, g_co, g_o) = _proj_fwd(xs, norm_mix_g, W_in, qg2, kg2, tb, ride=mix_shards)
    W_ao = g_ao.reshape(D_MODEL, D_MODEL)
    W_co = g_co.reshape(D_MODEL, D_MODEL)
    W_o = g_o.reshape(D_MODEL, D_MODEL)
    bias = _bias_table(rel_bias, bucket)
    (o, lse), (g_f1,) = _attn_fwd(qn, kn, vb, bias, attn_sinks, ride=_Gather([w_ff1[0].astype(bf16).T]))
    W_f1t = g_f1.reshape(D_FF, D_MODEL)
    (h1, h3), (g_f2,) = _conv_fwd(h0, W_dw, b_dw, conv_ln_g, conv_ln_b, tb, ride=_Gather([w_ff2[0].astype(bf16)]))
    x1, attn, conv, merged = _mix_fwd(xs, o, h3, proj, W_ao, W_co, W_o, tb)
    W_f2 = g_f2.reshape(D_FF, D_MODEL)
    a, u2, dy, dyb, loss_parts = _ffn_fwd(x1, norm_mlp_g, W_f1t, W_f2, tgt, tb)
    loss = lax.psum(jnp.sum(loss_parts[:, 0, 0]), ("x", "y", "c"))

    gw_f2 = _wgrad("wgrad_ff2", a, dyb, D_MODEL, D_MODEL, tt, relu2=True).reshape(N_DEV, FF_CHUNK, D_MODEL)
    (da, dx1, dx1b, d_norm_mlp_g), (l_f2,) = _ffn_bwd(dy, dyb, a, x1, norm_mlp_g, W_f1t, W_f2, tb,
                                                      ride=_Exchange([gw_f2], gather=False))
    gw_f1 = _wgrad("wgrad_ff1", u2, da, D_MODEL, 4 * FF_CHUNK, tt, slab=FF_CHUNK)
    gw_o = _wgrad("wgrad_out", merged, dx1b, D_MODEL, D_MODEL, tt).reshape(N_DEV, ch_shard, D_MODEL)
    (dattn, dconv, do, dh1, dgates, d_ln_g, d_ln_b, d_b_dw), (l_o,) = _mix_bwd(
        dx1b, proj, attn, conv, h1, conv_ln_g, conv_ln_b, W_ao, W_co, W_o, tb, ride=_Exchange([gw_o], gather=False))
    gw_ao = _wgrad("wgrad_attn_o", o, dattn, D_MODEL, D_MODEL, tt).reshape(N_DEV, ch_shard, D_MODEL)
    gw_co = _wgrad("wgrad_conv_out", h3, dconv, D_MODEL, D_MODEL, tt).reshape(N_DEV, ch_shard, D_MODEL)
    (dglu, d_w_dw), (l_f1, l_ao, l_co) = _conv_bwd(dh1, h0, proj, W_dw, tb,
                                                   ride=_Exchange([gw_f1, gw_ao, gw_co], gather=False))
    early = -(-V_END // in_shard)
    cut = early * in_shard - V_END
    g_glu = _wgrad("wgrad_in_glu", u, dglu, D_MODEL, 2 * D_MODEL, tt)
    g_gates = _wgrad("wgrad_in_gates", u, dgates, D_MODEL, 2 * D_MODEL, tt)
    gw_in_late = jnp.concatenate([g_glu[:, cut:], g_gates], axis=1)
    gw_in_late = jnp.transpose(gw_in_late.reshape(D_MODEL, N_DEV - early, in_shard), (1, 0, 2))
    (dq, dk, dv, dbias, d_sinks), (l_in_late,) = _attn_bwd(qn, kn, vb, o, do, lse, bias, attn_sinks,
                                                           ride=_PartExchange([gw_in_late], [(early, N_DEV)]))
    d_sinks = d_sinks[:, :N_Q_HEADS]
    d_rel_bias = _rel_bias_bwd(dbias, bucket)
    dqkv, d_qg, d_kg = _qk_norm_bwd(dq, dk, dv, proj, qg2, kg2, tb)
    d_qg = d_qg[:, :HEAD_DIM] + d_qg[:, HEAD_DIM:]
    d_kg = d_kg[:, :HEAD_DIM] + d_kg[:, HEAD_DIM:]
    gw_in_first = jnp.concatenate([_wgrad("wgrad_in_qkv", u, dqkv, D_MODEL, V_END, tt), g_glu[:, :cut]], axis=1)
    gw_in_first = jnp.transpose(gw_in_first.reshape(D_MODEL, early, in_shard), (1, 0, 2))
    gw_dw = jnp.transpose(d_w_dw.reshape(dw_rows, N_DEV, ch_shard), (1, 0, 2))
    (grad_x, d_norm_mix_g), (l_in_first, l_dw) = _in_bwd(
        dqkv, dglu, dgates, W_in, xs, norm_mix_g, dx1, tb,
        ride=_PartExchange([gw_in_first, gw_dw], [(0, early), (0, N_DEV)]))
    my_block = 4 * lax.axis_index("x") + 2 * lax.axis_index("y") + lax.axis_index("c")
    in_late = jnp.where(my_block >= early, 1.0, 0.0) * jnp.ones((1, 2 * HEAD_DIM), f32)

    def row(vec):
        flat = vec.reshape(1, -1)
        return jnp.pad(flat, ((0, 0), (0, D_MODEL - flat.shape[1])))

    def pack_small(nm, qg, kg, sk, rb, bd, lg, lb, nl):
        tail = jnp.concatenate([qg.reshape(1, -1), kg.reshape(1, -1), sk.reshape(1, -1), rb.reshape(1, -1)], axis=1)
        return jnp.concatenate([row(nm), row(bd), row(lg), row(lb), row(nl), row(tail), jnp.zeros((2, D_MODEL), f32)], axis=0)

    def unpack_small(p):
        t = p[5]
        o0, o1, o2 = HEAD_DIM, 2 * HEAD_DIM, 2 * HEAD_DIM + N_Q_HEADS
        return dict(norm_mix_g=p[0:1], b_dw=p[1:2], conv_ln_g=p[2:3], conv_ln_b=p[3:4], norm_mlp_g=p[4:5],
                    q_norm_g=t[0:o0].reshape(1, HEAD_DIM), k_norm_g=t[o0:o1].reshape(1, HEAD_DIM),
                    attn_sinks=t[o1:o2].reshape(1, N_Q_HEADS),
                    rel_bias=t[o2:o2 + N_BUCKETS * N_Q_HEADS].reshape(N_BUCKETS, N_Q_HEADS))

    small_g = pack_small(d_norm_mix_g, d_qg, d_kg, d_sinks, d_rel_bias, d_b_dw, d_ln_g, d_ln_b, d_norm_mlp_g)
    (l_small,) = _exchange("gather_small_grads", [small_g], gather=True)


    res = {}
    res["w_in"] = _adamw("adamw_in", l_in_first, w_in, m_w_in, v_w_in, 256, other=(l_in_late, in_late))
    res["w_attn_o"] = _adamw("adamw_attn_o", l_ao, w_attn_o, m_w_attn_o, v_w_attn_o, ch_shard)
    res["w_conv_out"] = _adamw("adamw_conv_out", l_co, w_conv_out, m_w_conv_out, v_w_conv_out, ch_shard)
    res["w_out"] = _adamw("adamw_out", l_o, w_out, m_w_out, v_w_out, ch_shard)
    res["w_ff1"] = _adamw("adamw_ff1", l_f1, w_ff1, m_w_ff1, v_w_ff1, 256)
    res["w_ff2"] = _adamw("adamw_ff2", l_f2, w_ff2, m_w_ff2, v_w_ff2, 256)
    pad_dw = lambda t: _pad_rows(t[0], dw_rows)[None]
    res["w_dw"] = [t[:, :CONV_WIDTH] for t in _adamw("adamw_dw", l_dw, pad_dw(w_dw), pad_dw(m_w_dw), pad_dw(v_w_dw), dw_rows)]
    small_w = pack_small(norm_mix_g, q_norm_g, k_norm_g, attn_sinks, rel_bias, b_dw, conv_ln_g, conv_ln_b, norm_mlp_g)
    small_m = pack_small(m_norm_mix_g, m_q_norm_g, m_k_norm_g, m_attn_sinks, m_rel_bias, m_b_dw, m_conv_ln_g, m_conv_ln_b, m_norm_mlp_g)
    small_v = pack_small(v_norm_mix_g, v_q_norm_g, v_k_norm_g, v_attn_sinks, v_rel_bias, v_b_dw, v_conv_ln_g, v_conv_ln_b, v_norm_mlp_g)
    small4 = [unpack_small(t[0]) for t in _adamw("adamw_small", l_small, small_w[None], small_m[None], small_v[None], 8)]

    order = ["norm_mix_g", "w_in", "q_norm_g", "k_norm_g", "attn_sinks", "rel_bias", "w_attn_o", "w_dw", "b_dw",
             "conv_ln_g", "conv_ln_b", "w_conv_out", "w_out", "norm_mlp_g", "w_ff1", "w_ff2"]
    stacked = {"w_in", "w_attn_o", "w_dw", "w_conv_out", "w_out", "w_ff1", "w_ff2"}
    outs = [loss, grad_x[None]]
    for k in range(4):
        for nme in order:
            if nme in stacked:
                outs.append(res[nme][k])
            else:
                outs.append(small4[k][nme])
    return tuple(outs)
```

```python
import functools

import numpy as np
import jax
import jax.numpy as jnp
from jax import lax
from jax.experimental import pallas as pl
from jax.experimental.pallas import tpu as pltpu

f32 = jnp.float32
bf16 = jnp.bfloat16
S = jax.ShapeDtypeStruct

N_DEV = 8
D_MODEL = 1024
HEAD_DIM = 64
N_Q_HEADS = 16
N_KV_HEADS = 4
GROUP = N_Q_HEADS // N_KV_HEADS
ATTN_WIDTH = N_Q_HEADS * HEAD_DIM
KV_WIDTH = N_KV_HEADS * HEAD_DIM
QBLOCK = 128
CONV_WIDTH = 31
CONV_HALO = 32
CONV_UNIT = 64
D_FF = 4 * D_MODEL
N_BUCKETS = 32
MAX_DISTANCE = 128
EPS = 1e-6
NEG = -1e30
Q_END = ATTN_WIDTH
K_END = Q_END + KV_WIDTH
V_END = K_END + KV_WIDTH
GLU_END = V_END + 2 * D_MODEL
IN_WIDTH = GLU_END + 2 * D_MODEL
COL = 512
FF_CHUNK = D_FF // N_DEV

ADAM_LR = 0.001
ADAM_B1 = 0.9
ADAM_B2 = 0.999
ADAM_EPS = 1e-08
ADAM_WD = 0.01
ADAM_STEP = 10

VMEM_LIMIT = 56 * 1024 * 1024

MESH_ID = pl.DeviceIdType.MESH
ANY = pl.BlockSpec(memory_space=pl.ANY)
SMEM = pl.BlockSpec(memory_space=pltpu.SMEM)


def _params(*sem):
    return pltpu.CompilerParams(dimension_semantics=sem, vmem_limit_bytes=VMEM_LIMIT)


def _nt(a, b):
    return lax.dot_general(a, b, (((1,), (1,)), ((), ())), preferred_element_type=f32)


def _tn(a, b):
    return lax.dot_general(a, b, (((0,), (0,)), ((), ())), preferred_element_type=f32)


def _sigmoid(z):
    return 1.0 / (1.0 + jnp.exp(-z))


def _t5_bucket_table():
    qi = np.arange(QBLOCK, dtype=np.int32)[:, None]
    kj = np.arange(2 * QBLOCK, dtype=np.int32)[None, :]
    dist = qi + QBLOCK - kj
    n = np.maximum(dist, 0)
    max_exact = N_BUCKETS // 2
    nf = np.maximum(n, 1).astype(np.float32)
    large = max_exact + (np.log(nf / np.float32(max_exact)) / np.float32(np.log(MAX_DISTANCE / max_exact))
                         * np.float32(N_BUCKETS - max_exact)).astype(np.int32)
    large = np.minimum(large, N_BUCKETS - 1)
    bucket = np.where(n < max_exact, n, large)
    valid = (dist >= 0) & (dist < QBLOCK)
    return np.where(valid, bucket, -1).astype(np.int32)


def _peer(d):
    x, y, c = lax.axis_index("x"), lax.axis_index("y"), lax.axis_index("c")
    dx, dy, dc = (d >> 2) & 1, (d >> 1) & 1, d & 1
    px, py, pc = x ^ dx, y ^ dy, c ^ dc
    return (px, py, pc), 4 * px + 2 * py + pc


class _Exchange:
    def __init__(self, arrays, gather):
        self.arrays, self.gather, self.n = list(arrays), gather, len(arrays)
        self.out_shape = [S(((N_DEV,) + a.shape) if gather else a.shape, a.dtype) for a in self.arrays]
        self.scratch = [pltpu.SemaphoreType.DMA((self.n, N_DEV - 1)), pltpu.SemaphoreType.DMA((self.n, N_DEV - 1)),
                        pltpu.SemaphoreType.DMA((self.n,))]

    def _copies(self, ins, outs, sems):
        send_sems, recv_sems, local_sems = sems
        _, me = _peer(0)
        local, sends, recvs = [], [], []
        for k in range(self.n):
            src = ins[k] if self.gather else ins[k].at[me]
            local.append(pltpu.make_async_copy(src, outs[k].at[me], local_sems.at[k]))
        for d in range(1, N_DEV):
            peer, pidx = _peer(d)
            for k in range(self.n):
                src = ins[k] if self.gather else ins[k].at[pidx]
                common = dict(src_ref=src, send_sem=send_sems.at[k, d - 1], recv_sem=recv_sems.at[k, d - 1],
                              device_id=peer, device_id_type=MESH_ID)
                sends.append(pltpu.make_async_remote_copy(dst_ref=outs[k].at[me], **common))
                recvs.append(pltpu.make_async_remote_copy(dst_ref=outs[k].at[pidx], **common))
        return local, sends, recvs

    def start(self, ins, outs, sems):
        local, sends, _ = self._copies(ins, outs, sems)
        for cp in local + sends:
            cp.start()

    def wait(self, ins, outs, sems):
        local, sends, recvs = self._copies(ins, outs, sems)
        for cp in recvs:
            cp.wait_recv()
        for cp in sends:
            cp.wait_send()
        for cp in local:
            cp.wait()


class _PartExchange:
    def __init__(self, arrays, dests):
        self.arrays, self.dests, self.n = list(arrays), list(dests), len(arrays)
        for a, (lo, hi) in zip(self.arrays, self.dests):
            assert a.shape[0] == hi - lo
        self.out_shape = [S((N_DEV,) + a.shape[1:], a.dtype) for a in self.arrays]
        self.scratch = [pltpu.SemaphoreType.DMA((self.n, N_DEV - 1)), pltpu.SemaphoreType.DMA((self.n, N_DEV - 1)),
                        pltpu.SemaphoreType.DMA((self.n,))]

    def _for_each(self, ins, outs, sems, local_fn, send_fn, recv_fn):
        send_sems, recv_sems, local_sems = sems
        _, me = _peer(0)
        for k in range(self.n):
            lo, hi = self.dests[k]
            mine = (me >= lo) & (me < hi)
            if local_fn is not None:
                @pl.when(mine)
                def _(k=k, lo=lo):
                    local_fn(pltpu.make_async_copy(ins[k].at[me - lo], outs[k].at[me], local_sems.at[k]))
            for d in range(1, N_DEV):
                peer, pidx = _peer(d)
                common = dict(send_sem=send_sems.at[k, d - 1], recv_sem=recv_sems.at[k, d - 1], device_id=peer,
                              device_id_type=MESH_ID)
                if send_fn is not None:
                    @pl.when((pidx >= lo) & (pidx < hi))
                    def _(k=k, lo=lo, pidx=pidx, common=common):
                        send_fn(pltpu.make_async_remote_copy(src_ref=ins[k].at[pidx - lo], dst_ref=outs[k].at[me], **common))
                if recv_fn is not None:
                    @pl.when(mine)
                    def _(k=k, pidx=pidx, common=common):
                        recv_fn(pltpu.make_async_remote_copy(src_ref=ins[k].at[0], dst_ref=outs[k].at[pidx], **common))

    def start(self, ins, outs, sems):
        self._for_each(ins, outs, sems, lambda cp: cp.start(), lambda cp: cp.start(), None)

    def wait(self, ins, outs, sems):
        self._for_each(ins, outs, sems, None, None, lambda cp: cp.wait_recv())
        self._for_each(ins, outs, sems, lambda cp: cp.wait(), lambda cp: cp.wait_send(), None)


class _Gather:
    CHIPS = (4, 2, 6)
    SLOTS = 1 + 2 * len(CHIPS)

    def __init__(self, arrays):
        self.arrays, self.n = list(arrays), len(arrays)
        self.out_shape = [S((N_DEV,) + a.shape, a.dtype) for a in self.arrays]
        self.scratch = [pltpu.SemaphoreType.DMA((self.n, self.SLOTS)), pltpu.SemaphoreType.DMA((self.n, self.SLOTS)),
                        pltpu.SemaphoreType.DMA((self.n,))]

    @staticmethod
    def _copy(outs, sems, k, slot, src, block, to):
        return pltpu.make_async_remote_copy(src_ref=src, dst_ref=outs[k].at[block], send_sem=sems[0].at[k, slot],
                                            recv_sem=sems[1].at[k, slot], device_id=to, device_id_type=MESH_ID)

    def _local(self, ins, outs, sems):
        _, me = _peer(0)
        return [pltpu.make_async_copy(ins[k], outs[k].at[me], sems[2].at[k]) for k in range(self.n)]

    def start(self, ins, outs, sems):
        _, me = _peer(0)
        sibling, _ = _peer(1)
        for cp in self._local(ins, outs, sems):
            cp.start()
        for k in range(self.n):
            self._copy(outs, sems, k, 0, ins[k], me, sibling).start()
            for j, d in enumerate(self.CHIPS):
                self._copy(outs, sems, k, 1 + j, ins[k], me, _peer(d)[0]).start()

    def mid(self, ins, outs, sems):
        sibling, _ = _peer(1)
        for j, d in enumerate(self.CHIPS):
            chip, block = _peer(d)
            for k in range(self.n):
                self._copy(outs, sems, k, 1 + j, ins[k], block, chip).wait_recv()
                self._copy(outs, sems, k, 4 + j, outs[k].at[block], block, sibling).start()

    def wait(self, ins, outs, sems):
        _, me = _peer(0)
        sibling, sib_block = _peer(1)
        for k in range(self.n):
            self._copy(outs, sems, k, 0, ins[k], sib_block, sibling).wait_recv()
            for j, d in enumerate(self.CHIPS):
                self._copy(outs, sems, k, 4 + j, ins[k], _peer(d ^ 1)[1], sibling).wait_recv()
        for k in range(self.n):
            self._copy(outs, sems, k, 0, ins[k], me, sibling).wait_send()
            for j, d in enumerate(self.CHIPS):
                chip, block = _peer(d)
                self._copy(outs, sems, k, 1 + j, ins[k], me, chip).wait_send()
                self._copy(outs, sems, k, 4 + j, outs[k].at[block], block, sibling).wait_send()
        for cp in self._local(ins, outs, sems):
            cp.wait()


def _exchange(name, arrays, gather, two_level=False):
    ex = _Gather(arrays) if two_level else _Exchange(arrays, gather)
    n = ex.n

    def body(*refs):
        ins, outs, sems = refs[:n], refs[n:2 * n], refs[2 * n:]
        ex.start(ins, outs, sems)
        if two_level:
            ex.mid(ins, outs, sems)
        ex.wait(ins, outs, sems)

    return pl.pallas_call(body, name=name, out_shape=ex.out_shape, in_specs=[ANY] * n, out_specs=[ANY] * n,
                          scratch_shapes=ex.scratch)(*arrays)


def _call(body, *, name, grid, in_specs, out_specs, out_shape, args, scratch_shapes=(), ride=None):
    n_in, n_out, n_sc = len(in_specs), len(out_specs), len(scratch_shapes)
    sem = ("arbitrary",) * len(grid)
    if ride is None:
        res = pl.pallas_call(body, name=name, grid=grid, in_specs=list(in_specs), out_specs=list(out_specs),
                             out_shape=list(out_shape), scratch_shapes=list(scratch_shapes), compiler_params=_params(*sem))(*args)
        return list(res), []
    nx = ride.n

    def riding(*refs):
        ins, xin = refs[:n_in], refs[n_in:n_in + nx]
        outs, xout = refs[n_in + nx:n_in + nx + n_out], refs[n_in + nx + n_out:n_in + 2 * nx + n_out]
        rest = refs[n_in + 2 * nx + n_out:]
        scratch, sems = rest[:n_sc], rest[n_sc:]
        ids = [pl.program_id(ax) for ax in range(len(grid))]
        first = functools.reduce(jnp.logical_and, [i == 0 for i in ids])
        last = functools.reduce(jnp.logical_and, [i == g - 1 for i, g in zip(ids, grid)])

        @pl.when(first)
        def _():
            ride.start(xin, xout, sems)

        if hasattr(ride, "mid"):
            halfway = functools.reduce(jnp.logical_and, [ids[0] == grid[0] // 2] + [i == 0 for i in ids[1:]])

            @pl.when(halfway)
            def _():
                ride.mid(xin, xout, sems)

        body(*ins, *outs, *scratch)

        @pl.when(last)
        def _():
            ride.wait(xin, xout, sems)

    res = pl.pallas_call(
        riding, name=name, grid=grid, in_specs=list(in_specs) + [ANY] * nx, out_specs=list(out_specs) + [ANY] * nx,
        out_shape=list(out_shape) + ride.out_shape, scratch_shapes=list(scratch_shapes) + ride.scratch,
        compiler_params=_params(*sem))(*args, *ride.arrays)
    return list(res[:n_out]), list(res[n_out:])


def _resident(shape):
    return pl.BlockSpec(shape, lambda *_: (0,) * len(shape), pipeline_mode=pl.Buffered(1))


def _proj_fwd(x, g, w, qg, kg, tm, ride=None):
    T, K = x.shape
    N = w.shape[1]
    per = COL // (2 * HEAD_DIM)
    assert Q_END % COL == 0 and V_END == Q_END + COL and (GLU_END - V_END) == 4 * COL and KV_WIDTH == COL // 2

    def body(x_ref, g_ref, w_ref, qg_ref, kg_ref, o_ref, u_ref, qn_ref, kn_ref, vb_ref, h0_ref):
        xv = x_ref[...]
        r = lax.rsqrt(jnp.mean(xv * xv, axis=-1, keepdims=True) + EPS)
        u = (xv * r * g_ref[...]).astype(bf16)
        u_ref[...] = u

        def block(c):
            cs = slice(c * COL, (c + 1) * COL)
            pc = jnp.dot(u, w_ref[:, cs], preferred_element_type=f32)
            o_ref[:, cs] = pc.astype(bf16)
            return pc

        qgv = qg_ref[...] * (HEAD_DIM ** -0.5)
        for c in range(Q_END // COL):
            pc = block(c)
            for t in range(per):
                xq = pc[:, _pair_cols(t)]
                qn_ref[:, _pair_cols(c * per + t)] = (xq * _pair_rstd(xq, False) * qgv).astype(bf16)
        pc = block(Q_END // COL)
        for t in range(KV_WIDTH // (2 * HEAD_DIM)):
            xk = pc[:, _pair_cols(t)]
            kn_ref[:, _pair_cols(t)] = (xk * _pair_rstd(xk, False) * kg_ref[...]).astype(bf16)
        vb_ref[...] = pc[:, KV_WIDTH:].astype(bf16)
        a0 = V_END // COL
        for half in range(2):
            gate = block(a0 + 2 + half)
            h0_ref[:, half * COL:(half + 1) * COL] = block(a0 + half) * _sigmoid(gate)
        for c in range(GLU_END // COL, N // COL):
            block(c)

    row = lambda width: pl.BlockSpec((tm, width), lambda i: (i, 0))
    return _call(
        body, name="proj_fwd", grid=(T // tm,),
        in_specs=[row(K), _resident((1, K)), _resident((K, N)), _resident((1, 2 * HEAD_DIM)), _resident((1, 2 * HEAD_DIM))],
        out_specs=[row(N), row(K), row(ATTN_WIDTH), row(KV_WIDTH), row(KV_WIDTH), row(D_MODEL)],
        out_shape=[S((T, N), bf16), S((T, K), bf16), S((T, ATTN_WIDTH), bf16), S((T, KV_WIDTH), bf16), S((T, KV_WIDTH), bf16),
                   S((T, D_MODEL), f32)],
        args=(x, g, w, qg, kg), ride=ride)


def _bias_table(rel_bias, bucket):
    def body(rb_ref, bk_ref, o_ref):
        b = bk_ref[...]
        absent = lax.broadcasted_iota(jnp.int32, (QBLOCK, 2 * QBLOCK), 1) < QBLOCK
        for h in range(N_Q_HEADS):
            acc = jnp.full((QBLOCK, 2 * QBLOCK), NEG, f32)
            for k in range(N_BUCKETS):
                acc = jnp.where(b == k, rb_ref[k, h], acc)
            o_ref[0, h * QBLOCK:(h + 1) * QBLOCK, :] = acc
            o_ref[1, h * QBLOCK:(h + 1) * QBLOCK, :] = jnp.where(absent, NEG, acc)

    return pl.pallas_call(
        body, name="bias_table", out_shape=S((2, N_Q_HEADS * QBLOCK, 2 * QBLOCK), f32),
        in_specs=[SMEM, pl.BlockSpec(memory_space=pltpu.VMEM)],
    )(rel_bias, bucket)


def _bias_spec():
    return pl.BlockSpec((None, N_Q_HEADS * QBLOCK, 2 * QBLOCK), lambda n: (jnp.where(n == 0, 1, 0), 0, 0))


def _swap_halves(t):
    return jnp.concatenate([t[:, HEAD_DIM:], t[:, :HEAD_DIM]], axis=1)


def _low_lanes():
    return lax.broadcasted_iota(jnp.int32, (1, 2 * HEAD_DIM), 1) < HEAD_DIM


def _one_head(pair, side):
    zero = jnp.zeros((), pair.dtype)
    return jnp.where(_low_lanes(), pair, zero) if side == 0 else jnp.where(_low_lanes(), zero, pair)


def _pair_mean(t, on_mxu):
    if not on_mxu:
        m_lo = jnp.sum(_one_head(t, 0), axis=-1, keepdims=True) * (1.0 / HEAD_DIM)
        m_hi = jnp.sum(_one_head(t, 1), axis=-1, keepdims=True) * (1.0 / HEAD_DIM)
        return jnp.where(_low_lanes(), m_lo, m_hi)
    width = 2 * HEAD_DIM
    same_head = ((lax.broadcasted_iota(jnp.int32, (width, width), 0) < HEAD_DIM)
                 == (lax.broadcasted_iota(jnp.int32, (width, width), 1) < HEAD_DIM))
    e = jnp.where(same_head, 1.0 / HEAD_DIM, 0.0).astype(bf16)
    hi = t.astype(bf16)
    lo = (t - hi.astype(f32)).astype(bf16)
    return jnp.dot(hi, e, preferred_element_type=f32) + jnp.dot(lo, e, preferred_element_type=f32)


def _pair_rstd(x, on_mxu):
    return lax.rsqrt(_pair_mean(x * x, on_mxu) + EPS)


def _kv_placements(band):
    out = {}
    for m in range(N_KV_HEADS // 2):
        pair = band[:, m * 2 * HEAD_DIM:(m + 1) * 2 * HEAD_DIM]
        swapped = _swap_halves(pair)
        for hh in range(2):
            out[2 * m + hh, 0] = _one_head(pair if hh == 0 else swapped, 0)
            out[2 * m + hh, 1] = _one_head(swapped if hh == 0 else pair, 1)
    return out


def _head_rows(hq):
    return slice(hq * QBLOCK, (hq + 1) * QBLOCK)


def _pair_cols(pr):
    return slice(pr * 2 * HEAD_DIM, (pr + 1) * 2 * HEAD_DIM)


def _attn_fwd(qn, kn, vb, bias, sinks, ride=None):
    T = qn.shape[0]
    nb = T // QBLOCK

    def body(q_ref, kc_ref, kp_ref, vc_ref, vp_ref, b_ref, s_ref, o_ref, lse_ref, s_scr, p_scr):
        lane = lax.broadcasted_iota(jnp.int32, (QBLOCK, 2 * HEAD_DIM), 1)
        kx = _kv_placements(jnp.concatenate([kp_ref[...], kc_ref[...]], axis=0))
        vx = _kv_placements(jnp.concatenate([vp_ref[...], vc_ref[...]], axis=0))
        for hq in range(N_Q_HEADS):
            qm = _one_head(q_ref[:, _pair_cols(hq // 2)], hq % 2)
            s_scr[_head_rows(hq), :] = _nt(qm, kx[hq // GROUP, hq % 2]) + b_ref[_head_rows(hq), :]
        lse_tile = jnp.zeros((QBLOCK, 2 * HEAD_DIM), f32)
        for hq in range(N_Q_HEADS):
            s = s_scr[_head_rows(hq), :]
            sink = s_ref[0, hq]
            m = jnp.maximum(jnp.max(s, axis=-1, keepdims=True), sink)
            p = jnp.exp(s - m)
            l = jnp.sum(p, axis=-1, keepdims=True) + jnp.exp(sink - m)
            p_scr[_head_rows(hq), :] = (p * (1.0 / l)).astype(bf16)
            lse_tile = jnp.where(lane == hq, m + jnp.log(l), lse_tile)
        lse_ref[...] = lse_tile
        for pr in range(N_Q_HEADS // 2):
            h = 2 * pr // GROUP
            o_pair = (jnp.dot(p_scr[_head_rows(2 * pr), :], vx[h, 0], preferred_element_type=f32)
                      + jnp.dot(p_scr[_head_rows(2 * pr + 1), :], vx[h, 1], preferred_element_type=f32))
            o_ref[:, _pair_cols(pr)] = o_pair.astype(bf16)

    cur = lambda n: (n, 0)
    prev = lambda n: (jnp.maximum(n - 1, 0), 0)
    return _call(
        body, name="attn_fwd", grid=(nb,),
        in_specs=[pl.BlockSpec((QBLOCK, ATTN_WIDTH), cur), pl.BlockSpec((QBLOCK, KV_WIDTH), cur),
                  pl.BlockSpec((QBLOCK, KV_WIDTH), prev), pl.BlockSpec((QBLOCK, KV_WIDTH), cur),
                  pl.BlockSpec((QBLOCK, KV_WIDTH), prev), _bias_spec(), SMEM],
        out_specs=[pl.BlockSpec((QBLOCK, ATTN_WIDTH), cur), pl.BlockSpec((QBLOCK, 2 * HEAD_DIM), cur)],
        out_shape=[S((T, ATTN_WIDTH), bf16), S((T, 2 * HEAD_DIM), f32)],
        scratch_shapes=[pltpu.VMEM((N_Q_HEADS * QBLOCK, 2 * QBLOCK), f32), pltpu.VMEM((N_Q_HEADS * QBLOCK, 2 * QBLOCK), bf16)],
        args=(qn, kn, kn, vb, vb, bias, sinks), ride=ride)


def _layer_norm_stats(h1):
    mu = jnp.mean(h1, axis=-1, keepdims=True)
    xc = h1 - mu
    rstd = lax.rsqrt(jnp.mean(xc * xc, axis=-1, keepdims=True) + EPS)
    return xc * rstd, rstd


def _advanced_windows(win):
    rows = win.shape[0]
    for r in range(8):
        yield r, (win if r == 0 else pltpu.roll(win, rows - r, 0))


def _tap_offsets(r, rows):
    for q in range((rows - CONV_UNIT) // 8 + 1):
        if r == 0 or 8 * q + r + CONV_UNIT <= rows:
            yield q, 8 * q + r


def _conv_fwd(h0, w_dw, b_dw, ln_g, ln_b, tm, ride=None):
    T = h0.shape[0]
    per = tm // CONV_HALO
    lead = CONV_HALO - (CONV_WIDTH - 1)

    def body(hc_ref, hp_ref, w_ref, b_ref, g_ref, bb_ref, h1_ref, h3_ref, cat):
        i = pl.program_id(0)
        cat[0:CONV_HALO, :] = jnp.where(i == 0, 0.0, hp_ref[...])
        cat[CONV_HALO:, :] = hc_ref[...]

        def unit_rows(c, carry):
            r0 = pl.multiple_of(c * CONV_UNIT, CONV_UNIT)
            for j in range(D_MODEL // 128):
                ls = slice(j * 128, (j + 1) * 128)
                win = cat[pl.ds(r0, CONV_UNIT + CONV_HALO), ls]
                acc = jnp.zeros((CONV_UNIT, 128), f32) + b_ref[:, ls]
                for r, adv in _advanced_windows(win):
                    for q, off in _tap_offsets(r, CONV_UNIT + CONV_HALO):
                        k = off - lead
                        if 0 <= k < CONV_WIDTH:
                            acc = acc + adv[8 * q:8 * q + CONV_UNIT] * w_ref[k:k + 1, ls]
                h1_ref[pl.ds(r0, CONV_UNIT), ls] = acc
            return carry

        lax.fori_loop(0, tm // CONV_UNIT, unit_rows, 0)
        acc = h1_ref[...]
        xhat, _ = _layer_norm_stats(acc)
        h2 = xhat * g_ref[...] + bb_ref[...]
        h3_ref[...] = (h2 * _sigmoid(h2)).astype(bf16)

    vec = pl.BlockSpec((1, D_MODEL), lambda i: (0, 0))
    return _call(
        body, name="conv_fwd", grid=(T // tm,),
        in_specs=[pl.BlockSpec((tm, D_MODEL), lambda i: (i, 0)),
                  pl.BlockSpec((CONV_HALO, D_MODEL), lambda i: (jnp.maximum(i * per - 1, 0), 0)),
                  pl.BlockSpec((CONV_WIDTH, D_MODEL), lambda i: (0, 0)), vec, vec, vec],
        out_specs=[pl.BlockSpec((tm, D_MODEL), lambda i: (i, 0)), pl.BlockSpec((tm, D_MODEL), lambda i: (i, 0))],
        out_shape=[S((T, D_MODEL), f32), S((T, D_MODEL), bf16)],
        scratch_shapes=[pltpu.VMEM((tm + CONV_HALO, D_MODEL), f32)],
        args=(h0, h0, w_dw, b_dw, ln_g, ln_b), ride=ride)


def _mix_fwd(x, o, h3, proj, w_ao, w_co, w_o, tm):
    T = x.shape[0]
    row = pl.BlockSpec((tm, D_MODEL), lambda i: (i, 0))
    wsp = _resident((D_MODEL, D_MODEL))
    g0 = GLU_END // COL

    def gate_spec(off):
        return pl.BlockSpec((tm, COL), lambda i: (i, g0 + off))

    def body(x_ref, o_ref, h3_ref, ga0, ga1, gc0, gc1, wa_ref, wc_ref, wo_ref, x1_ref, at_ref, cv_ref, mg_ref):
        attn = jnp.dot(o_ref[...], wa_ref[...], preferred_element_type=f32)
        conv = jnp.dot(h3_ref[...], wc_ref[...], preferred_element_type=f32)
        ga = jnp.concatenate([ga0[...], ga1[...]], axis=-1).astype(f32)
        gc = jnp.concatenate([gc0[...], gc1[...]], axis=-1).astype(f32)
        merged = (_sigmoid(ga) * attn + _sigmoid(gc) * conv).astype(bf16)
        at_ref[...] = attn.astype(bf16)
        cv_ref[...] = conv.astype(bf16)
        mg_ref[...] = merged
        x1_ref[...] = x_ref[...] + jnp.dot(merged, wo_ref[...], preferred_element_type=f32)

    return pl.pallas_call(
        body, name="mix_fwd", grid=(T // tm,),
        in_specs=[row, row, row, gate_spec(0), gate_spec(1), gate_spec(2), gate_spec(3), wsp, wsp, wsp],
        out_specs=[row, row, row, row],
        out_shape=[S((T, D_MODEL), f32), S((T, D_MODEL), bf16), S((T, D_MODEL), bf16), S((T, D_MODEL), bf16)],
        compiler_params=_params("parallel"),
    )(x, o, h3, proj, proj, proj, proj, w_ao, w_co, w_o)


def _ffn_fwd(x1, g, w1, w2, target, tm):
    T = x1.shape[0]
    nj = w1.shape[1] // FF_CHUNK

    def body(x_ref, g_ref, w1_ref, w2_ref, t_ref, a_ref, u_ref, dy_ref, dyb_ref, ls_ref, hm):
        xv = x_ref[...]
        r = lax.rsqrt(jnp.mean(xv * xv, axis=-1, keepdims=True) + EPS)
        u = (xv * r * g_ref[...]).astype(bf16)
        u_ref[...] = u
        for j in range(nj):
            js = slice(j * FF_CHUNK, (j + 1) * FF_CHUNK)
            a = jnp.dot(u, w1_ref[:, js], preferred_element_type=f32)
            a_ref[:, js] = a.astype(bf16)
            hm[:, js] = jnp.square(jnp.maximum(a, 0.0)).astype(bf16)
        err = xv + jnp.dot(hm[...], w2_ref[...], preferred_element_type=f32) - t_ref[...]
        dy = err * (1.0 / D_MODEL)
        dy_ref[...] = dy
        dyb_ref[...] = dy.astype(bf16)
        ls_ref[...] = jnp.zeros((8, 128), f32) + jnp.sum(err * err) * (0.5 / D_MODEL)

    row = pl.BlockSpec((tm, D_MODEL), lambda i: (i, 0))
    wide = pl.BlockSpec((tm, D_FF), lambda i: (i, 0))
    return pl.pallas_call(
        body, name="ffn_fwd", grid=(T // tm,),
        in_specs=[row, _resident((1, D_MODEL)), _resident(w1.shape), _resident(w2.shape), row],
        out_specs=[wide, row, row, row, pl.BlockSpec((None, 8, 128), lambda i: (i, 0, 0))],
        out_shape=[S((T, D_FF), bf16), S((T, D_MODEL), bf16), S((T, D_MODEL), f32), S((T, D_MODEL), bf16),
                   S((T // tm, 8, 128), f32)],
        scratch_shapes=[pltpu.VMEM((tm, D_FF), bf16)],
        compiler_params=_params("parallel"),
    )(x1, g, w1, w2, target)


def _rms_bwd(du, xv, gv):
    r = lax.rsqrt(jnp.mean(xv * xv, axis=-1, keepdims=True) + EPS)
    xn = xv * r
    dg = jnp.sum(du * xn, axis=0, keepdims=True)
    dxn = du * gv
    dx = r * (dxn - xn * jnp.mean(dxn * xn, axis=-1, keepdims=True))
    return dx, dg


def _ffn_bwd(dy, dyb, a, x1, g, w1, w2, tm, ride=None):
    T = dy.shape[0]
    nj = w1.shape[0] // FF_CHUNK

    def body(dy_ref, dyb_ref, a_ref, x_ref, g_ref, w1_ref, w2_ref, da_ref, dx_ref, dxb_ref, dg_ref):
        @pl.when(pl.program_id(0) == 0)
        def _():
            dg_ref[...] = jnp.zeros_like(dg_ref)

        dyb_v = dyb_ref[...]
        for j in range(nj):
            js = slice(j * FF_CHUNK, (j + 1) * FF_CHUNK)
            dh = jnp.dot(dyb_v, w2_ref[:, js], preferred_element_type=f32)
            da_ref[:, js] = (dh * (2.0 * jnp.maximum(a_ref[:, js].astype(f32), 0.0))).astype(bf16)
        du = jnp.dot(da_ref[...], w1_ref[...], preferred_element_type=f32)
        dx, dg = _rms_bwd(du, x_ref[...], g_ref[...])
        dx1 = dy_ref[...] + dx
        dx_ref[...] = dx1
        dxb_ref[...] = dx1.astype(bf16)
        dg_ref[...] += dg

    row = pl.BlockSpec((tm, D_MODEL), lambda i: (i, 0))
    wide = pl.BlockSpec((tm, D_FF), lambda i: (i, 0))
    vec = pl.BlockSpec((1, D_MODEL), lambda i: (0, 0))
    return _call(
        body, name="ffn_bwd", grid=(T // tm,),
        in_specs=[row, row, wide, row, _resident((1, D_MODEL)), _resident(w1.shape), _resident(w2.shape)],
        out_specs=[wide, row, row, vec],
        out_shape=[S((T, D_FF), bf16), S((T, D_MODEL), f32), S((T, D_MODEL), bf16), S((1, D_MODEL), f32)],
        args=(dy, dyb, a, x1, g, w1, w2), ride=ride)


def _wgrad(name, a, b, tk, tn, tt, relu2=False, slab=None, out_dtype=bf16):
    T, Ka = a.shape
    Nb = b.shape[1]
    nt = T // tt

    def body(a_ref, b_ref, o_ref, acc):
        t = pl.program_id(2)
        av = a_ref[...]
        if relu2:
            av = jnp.square(jnp.maximum(av.astype(f32), 0.0))
        prod = _tn(av.astype(bf16), b_ref[...].astype(bf16))

        @pl.when(t == 0)
        def _():
            acc[...] = prod

        @pl.when(t > 0)
        def _():
            acc[...] += prod

        @pl.when(t == nt - 1)
        def _():
            if slab is None:
                o_ref[...] = acc[...].astype(out_dtype)
            else:
                for s in range(tn // slab):
                    o_ref[s] = acc[:, s * slab:(s + 1) * slab].astype(out_dtype)

    if slab is not None:
        out_shape = S((Nb // slab, Ka, slab), out_dtype)
        out_spec = pl.BlockSpec((tn // slab, tk, slab), lambda i, j, t: (j, i, 0))
    else:
        out_shape = S((Ka, Nb), out_dtype)
        out_spec = pl.BlockSpec((tk, tn), lambda i, j, t: (i, j))
    return pl.pallas_call(
        body, name=name, grid=(Ka // tk, Nb // tn, nt),
        in_specs=[pl.BlockSpec((tt, tk), lambda i, j, t: (t, i)), pl.BlockSpec((tt, tn), lambda i, j, t: (t, j))],
        out_specs=out_spec, out_shape=out_shape, scratch_shapes=[pltpu.VMEM((tk, tn), f32)],
        compiler_params=_params("parallel", "parallel", "arbitrary"),
    )(a, b)


def _mix_bwd(dx1, proj, attn, conv, h1, ln_g, ln_b, w_ao, w_co, w_o, tm, ride=None):
    T = dx1.shape[0]
    g0 = GLU_END // COL

    def gate_spec(off):
        return pl.BlockSpec((tm, COL), lambda i: (i, g0 + off))

    def body(dx_ref, ga0, ga1, gc0, gc1, at_ref, cv_ref, h_ref, g_ref, b_ref, wa_ref, wc_ref, wo_ref,
             da_ref, dc_ref, do_ref, dh1_ref, dg_ref, dlg_ref, dlb_ref, dbd_ref):
        @pl.when(pl.program_id(0) == 0)
        def _():
            dlg_ref[...] = jnp.zeros_like(dlg_ref)
            dlb_ref[...] = jnp.zeros_like(dlb_ref)
            dbd_ref[...] = jnp.zeros_like(dbd_ref)

        dm = jnp.dot(dx_ref[...].astype(bf16), wo_ref[...], preferred_element_type=f32)
        sa = _sigmoid(jnp.concatenate([ga0[...], ga1[...]], axis=-1).astype(f32))
        sc = _sigmoid(jnp.concatenate([gc0[...], gc1[...]], axis=-1).astype(f32))
        dattn = (dm * sa).astype(bf16)
        dconv = (dm * sc).astype(bf16)
        da_ref[...] = dattn
        dc_ref[...] = dconv
        dg_ref[:, 0:D_MODEL] = (dm * at_ref[...].astype(f32) * sa * (1.0 - sa)).astype(bf16)
        dg_ref[:, D_MODEL:2 * D_MODEL] = (dm * cv_ref[...].astype(f32) * sc * (1.0 - sc)).astype(bf16)
        do_ref[...] = jnp.dot(dattn, wa_ref[...], preferred_element_type=f32).astype(bf16)
        dh3 = jnp.dot(dconv, wc_ref[...], preferred_element_type=f32)
        xhat, rstd = _layer_norm_stats(h_ref[...])
        h2 = xhat * g_ref[...] + b_ref[...]
        sg = _sigmoid(h2)
        dh2 = dh3 * (sg * (1.0 + h2 * (1.0 - sg)))
        dlg_ref[...] += jnp.sum(dh2 * xhat, axis=0, keepdims=True)
        dlb_ref[...] += jnp.sum(dh2, axis=0, keepdims=True)
        dxh = dh2 * g_ref[...]
        dh1 = rstd * (dxh - jnp.mean(dxh, axis=-1, keepdims=True) - xhat * jnp.mean(dxh * xhat, axis=-1, keepdims=True))
        dh1_ref[...] = dh1
        dbd_ref[...] += jnp.sum(dh1, axis=0, keepdims=True)

    row = pl.BlockSpec((tm, D_MODEL), lambda i: (i, 0))
    vec = pl.BlockSpec((1, D_MODEL), lambda i: (0, 0))
    par = _resident((1, D_MODEL))
    wsp = _resident((D_MODEL, D_MODEL))
    return _call(
        body, name="mix_bwd", grid=(T // tm,),
        in_specs=[row, gate_spec(0), gate_spec(1), gate_spec(2), gate_spec(3), row, row, row, par, par, wsp, wsp, wsp],
        out_specs=[row, row, row, row, pl.BlockSpec((tm, 2 * D_MODEL), lambda i: (i, 0)), vec, vec, vec],
        out_shape=[S((T, D_MODEL), bf16), S((T, D_MODEL), bf16), S((T, D_MODEL), bf16), S((T, D_MODEL), f32),
                   S((T, 2 * D_MODEL), bf16), S((1, D_MODEL), f32), S((1, D_MODEL), f32), S((1, D_MODEL), f32)],
        args=(dx1, proj, proj, proj, proj, attn, conv, h1, ln_g, ln_b, w_ao, w_co, w_o), ride=ride)


def _conv_bwd(dh1, h0, proj, w_dw, tm, ride=None):
    T = dh1.shape[0]
    per = tm // CONV_HALO
    nh = T // CONV_HALO
    nt = T // tm
    a0 = V_END // COL
    lead = CONV_HALO - (CONV_WIDTH - 1)

    def body(dc_ref, dn_ref, hc_ref, hp_ref, a0_ref, a1_ref, g0_ref, g1_ref, w_ref, dglu_ref, dw_ref, dcat, hcat, wacc, dh0):
        i = pl.program_id(0)

        @pl.when(i == 0)
        def _():
            wacc[...] = jnp.zeros_like(wacc)

        dcat[0:tm, :] = dc_ref[...]
        dcat[tm:, :] = jnp.where(i == nt - 1, 0.0, dn_ref[...])
        hcat[0:CONV_HALO, :] = jnp.where(i == 0, 0.0, hp_ref[...])
        hcat[CONV_HALO:, :] = hc_ref[...]
        span = CONV_UNIT + CONV_HALO

        def unit_rows(c, carry):
            r0 = pl.multiple_of(c * CONV_UNIT, CONV_UNIT)
            for j in range(D_MODEL // 128):
                ls = slice(j * 128, (j + 1) * 128)
                dwin = dcat[pl.ds(r0, span), ls]
                acc = jnp.zeros((CONV_UNIT, 128), f32)
                for r, adv in _advanced_windows(dwin):
                    for q, off in _tap_offsets(r, span):
                        k = CONV_WIDTH - 1 - off
                        if 0 <= k < CONV_WIDTH:
                            acc = acc + adv[8 * q:8 * q + CONV_UNIT] * w_ref[k:k + 1, ls]
                dh0[pl.ds(r0, CONV_UNIT), ls] = acc
                dcur = dwin[0:CONV_UNIT]
                for r, adv in _advanced_windows(hcat[pl.ds(r0, span), ls]):
                    for q, off in _tap_offsets(r, span):
                        k = off - lead
                        if 0 <= k < CONV_WIDTH:
                            prod = dcur * adv[8 * q:8 * q + CONV_UNIT]
                            wacc[k, :, ls] += jnp.sum(prod.reshape(CONV_UNIT // 8, 8, 128), axis=0)
            return carry

        lax.fori_loop(0, tm // CONV_UNIT, unit_rows, 0)
        dh0v = dh0[...]
        av = jnp.concatenate([a0_ref[...], a1_ref[...]], axis=-1).astype(f32)
        sg = _sigmoid(jnp.concatenate([g0_ref[...], g1_ref[...]], axis=-1).astype(f32))
        dglu_ref[:, 0:D_MODEL] = (dh0v * sg).astype(bf16)
        dglu_ref[:, D_MODEL:2 * D_MODEL] = (dh0v * av * sg * (1.0 - sg)).astype(bf16)

        @pl.when(i == nt - 1)
        def _():
            for k in range(CONV_WIDTH):
                dw_ref[k:k + 1, :] = jnp.sum(wacc[k], axis=0, keepdims=True)
            dw_ref[CONV_WIDTH:CONV_WIDTH + 1, :] = jnp.zeros((1, D_MODEL), f32)

    row = pl.BlockSpec((tm, D_MODEL), lambda i: (i, 0))

    def col_spec(off):
        return pl.BlockSpec((tm, COL), lambda i: (i, a0 + off))

    return _call(
        body, name="conv_bwd", grid=(nt,),
        in_specs=[row, pl.BlockSpec((CONV_HALO, D_MODEL), lambda i: (jnp.minimum((i + 1) * per, nh - 1), 0)),
                  row, pl.BlockSpec((CONV_HALO, D_MODEL), lambda i: (jnp.maximum(i * per - 1, 0), 0)),
                  col_spec(0), col_spec(1), col_spec(2), col_spec(3),
                  pl.BlockSpec((CONV_WIDTH, D_MODEL), lambda i: (0, 0))],
        out_specs=[pl.BlockSpec((tm, 2 * D_MODEL), lambda i: (i, 0)), pl.BlockSpec((CONV_WIDTH + 1, D_MODEL), lambda i: (0, 0))],
        out_shape=[S((T, 2 * D_MODEL), bf16), S((CONV_WIDTH + 1, D_MODEL), f32)],
        scratch_shapes=[pltpu.VMEM((tm + CONV_HALO, D_MODEL), f32), pltpu.VMEM((tm + CONV_HALO, D_MODEL), f32),
                        pltpu.VMEM((CONV_WIDTH, 8, D_MODEL), f32), pltpu.VMEM((tm, D_MODEL), f32)],
        args=(dh1, dh1, h0, h0, proj, proj, proj, proj, w_dw), ride=ride)


def _attn_bwd(qn, kn, vb, o, do, lse, bias, sinks, ride=None):
    T = qn.shape[0]
    nb = T // QBLOCK

    def body(q_ref, kc_ref, kp_ref, vc_ref, vp_ref, o_ref, do_ref, lse_ref, b_ref, s_ref,
             dq_ref, dk_ref, dv_ref, db_ref, dsk_ref, kcar, vcar, s_scr, dp_scr, p_scr, ds_scr):
        n = pl.program_id(0)

        @pl.when(n == 0)
        def _():
            db_ref[...] = jnp.zeros_like(db_ref)
            dsk_ref[...] = jnp.zeros_like(dsk_ref)
            kcar[...] = jnp.zeros_like(kcar)
            vcar[...] = jnp.zeros_like(vcar)

        @pl.when(n < nb)
        def _():
            lane = lax.broadcasted_iota(jnp.int32, (QBLOCK, 2 * HEAD_DIM), 1)
            lane_row = lax.broadcasted_iota(jnp.int32, (1, 2 * HEAD_DIM), 1)
            kx = _kv_placements(jnp.concatenate([kp_ref[...], kc_ref[...]], axis=0))
            vx = _kv_placements(jnp.concatenate([vp_ref[...], vc_ref[...]], axis=0))
            lse_tile = lse_ref[...]
            delta, lse_c = {}, {}
            for pr in range(N_Q_HEADS // 2):
                dop = do_ref[:, _pair_cols(pr)]
                dl = dop.astype(f32) * o_ref[:, _pair_cols(pr)].astype(f32)
                for side in range(2):
                    hq = 2 * pr + side
                    h = hq // GROUP
                    qm = _one_head(q_ref[:, _pair_cols(pr)], side)
                    s_scr[_head_rows(hq), :] = _nt(qm, kx[h, side]) + b_ref[_head_rows(hq), :]
                    dp_scr[_head_rows(hq), :] = _nt(_one_head(dop, side), vx[h, side])
                    delta[hq] = jnp.sum(_one_head(dl, side), axis=-1, keepdims=True)
                    lse_c[hq] = jnp.sum(jnp.where(lane == hq, lse_tile, 0.0), axis=-1, keepdims=True)
            dsk = jnp.zeros((1, 2 * HEAD_DIM), f32)
            for hq in range(N_Q_HEADS):
                p = jnp.exp(s_scr[_head_rows(hq), :] - lse_c[hq])
                ds = p * (dp_scr[_head_rows(hq), :] - delta[hq])
                db_ref[_head_rows(hq), :] += ds
                p_scr[_head_rows(hq), :] = p.astype(bf16)
                ds_scr[_head_rows(hq), :] = ds.astype(bf16)
                psink = jnp.exp(s_ref[0, hq] - lse_c[hq])
                dsk = dsk - jnp.where(lane_row == hq, jnp.sum(psink * delta[hq], axis=0, keepdims=True), 0.0)
            dsk_ref[...] += dsk
            for pr in range(N_Q_HEADS // 2):
                h = 2 * pr // GROUP
                dq_ref[:, _pair_cols(pr)] = (jnp.dot(ds_scr[_head_rows(2 * pr), :], kx[h, 0], preferred_element_type=f32)
                                             + jnp.dot(ds_scr[_head_rows(2 * pr + 1), :], kx[h, 1], preferred_element_type=f32))
            folded_k, folded_v = [], []
            for h in range(N_KV_HEADS):
                ka = jnp.zeros((2 * QBLOCK, 2 * HEAD_DIM), f32)
                va = jnp.zeros((2 * QBLOCK, 2 * HEAD_DIM), f32)
                for g in range(GROUP):
                    hq = h * GROUP + g
                    ka = ka + _tn(ds_scr[_head_rows(hq), :], _one_head(q_ref[:, _pair_cols(hq // 2)], hq % 2))
                    va = va + _tn(p_scr[_head_rows(hq), :], _one_head(do_ref[:, _pair_cols(hq // 2)], hq % 2))
                folded_k.append(ka + _swap_halves(ka))
                folded_v.append(va + _swap_halves(va))
            low = _low_lanes()
            for m in range(N_KV_HEADS // 2):
                cs = _pair_cols(m)
                for folded, out_ref, car in ((folded_k, dk_ref, kcar), (folded_v, dv_ref, vcar)):
                    band = jnp.where(low, folded[2 * m], folded[2 * m + 1])
                    out_ref[:, cs] = car[:, cs] + band[0:QBLOCK, :]
                    car[:, cs] = band[QBLOCK:, :]

        @pl.when(n == nb)
        def _():
            dk_ref[...] = kcar[...]
            dv_ref[...] = vcar[...]

    cur = lambda n: (jnp.minimum(n, nb - 1), 0)
    prev = lambda n: (jnp.clip(n - 1, 0, nb - 1), 0)
    qspec = pl.BlockSpec((QBLOCK, ATTN_WIDTH), cur)
    kcur, kprev = pl.BlockSpec((QBLOCK, KV_WIDTH), cur), pl.BlockSpec((QBLOCK, KV_WIDTH), prev)
    whole = lambda shape: pl.BlockSpec(shape, lambda n: (0,) * len(shape))
    scores = (N_Q_HEADS * QBLOCK, 2 * QBLOCK)
    return _call(
        body, name="attn_bwd", grid=(nb + 1,),
        in_specs=[qspec, kcur, kprev, kcur, kprev, qspec, qspec, pl.BlockSpec((QBLOCK, 2 * HEAD_DIM), cur), _bias_spec(), SMEM],
        out_specs=[qspec, kprev, kprev, whole(scores), whole((1, 2 * HEAD_DIM))],
        out_shape=[S((T, ATTN_WIDTH), f32), S((T, KV_WIDTH), f32), S((T, KV_WIDTH), f32), S(scores, f32),
                   S((1, 2 * HEAD_DIM), f32)],
        scratch_shapes=[pltpu.VMEM((QBLOCK, KV_WIDTH), f32), pltpu.VMEM((QBLOCK, KV_WIDTH), f32),
                        pltpu.VMEM(scores, f32), pltpu.VMEM(scores, f32), pltpu.VMEM(scores, bf16), pltpu.VMEM(scores, bf16)],
        args=(qn, kn, kn, vb, vb, o, do, lse, bias, sinks), ride=ride)


def _rel_bias_bwd(dbias, bucket):
    def body(d_ref, bk_ref, o_ref):
        b = bk_ref[...]
        for k in range(N_BUCKETS):
            mk = b == k
            for h in range(N_Q_HEADS):
                o_ref[k, h] = jnp.sum(jnp.where(mk, d_ref[h * QBLOCK:(h + 1) * QBLOCK, :], 0.0))

    return pl.pallas_call(body, name="rel_bias_bwd", out_shape=S((N_BUCKETS, N_Q_HEADS), f32), out_specs=SMEM)(dbias, bucket)


def _qk_norm_bwd(dq, dk, dv, proj, qg, kg, tm):
    T = dq.shape[0]
    scale = HEAD_DIM ** -0.5

    def pair_bwd(dy, x, gv):
        r = _pair_rstd(x, True)
        xn = x * r
        dxn = dy * gv
        dx = r * (dxn - xn * _pair_mean(dxn * xn, True))
        return dx, jnp.sum(dy * xn, axis=0, keepdims=True)

    def body(dq_ref, dk_ref, dv_ref, p_ref, qg_ref, kg_ref, out_ref, dqg_ref, dkg_ref):
        @pl.when(pl.program_id(0) == 0)
        def _():
            dqg_ref[...] = jnp.zeros_like(dqg_ref)
            dkg_ref[...] = jnp.zeros_like(dkg_ref)

        qgv, kgv = qg_ref[...], kg_ref[...]
        dqg = jnp.zeros((1, 2 * HEAD_DIM), f32)
        for pr in range(N_Q_HEADS // 2):
            dx, dg = pair_bwd(dq_ref[:, _pair_cols(pr)] * scale, p_ref[:, _pair_cols(pr)].astype(f32), qgv)
            out_ref[:, _pair_cols(pr)] = dx.astype(bf16)
            dqg = dqg + dg
        dkg = jnp.zeros((1, 2 * HEAD_DIM), f32)
        for pr in range(N_KV_HEADS // 2):
            ps = slice(Q_END + pr * 2 * HEAD_DIM, Q_END + (pr + 1) * 2 * HEAD_DIM)
            dx, dg = pair_bwd(dk_ref[:, _pair_cols(pr)], p_ref[:, ps].astype(f32), kgv)
            out_ref[:, ps] = dx.astype(bf16)
            dkg = dkg + dg
        out_ref[:, K_END:V_END] = dv_ref[...].astype(bf16)
        dqg_ref[...] += dqg
        dkg_ref[...] += dkg

    vec = pl.BlockSpec((1, 2 * HEAD_DIM), lambda i: (0, 0))
    return pl.pallas_call(
        body, name="qk_norm_bwd", grid=(T // tm,),
        in_specs=[pl.BlockSpec((tm, ATTN_WIDTH), lambda i: (i, 0)), pl.BlockSpec((tm, KV_WIDTH), lambda i: (i, 0)),
                  pl.BlockSpec((tm, KV_WIDTH), lambda i: (i, 0)), pl.BlockSpec((tm, V_END), lambda i: (i, 0)), vec, vec],
        out_specs=[pl.BlockSpec((tm, V_END), lambda i: (i, 0)), vec, vec],
        out_shape=[S((T, V_END), bf16), S((1, 2 * HEAD_DIM), f32), S((1, 2 * HEAD_DIM), f32)],
        compiler_params=_params("arbitrary"),
    )(dq, dk, dv, proj, qg, kg)


def _in_bwd(dqkv, dglu, dgates, w_in, x, g, dx1, tm, ride=None):
    T = x.shape[0]
    pieces = (dqkv, dglu, dgates)
    starts = [0, dqkv.shape[1], dqkv.shape[1] + dglu.shape[1]]

    def body(a0_ref, a1_ref, a2_ref, w_ref, x_ref, g_ref, d_ref, gx_ref, dg_ref):
        @pl.when(pl.program_id(0) == 0)
        def _():
            dg_ref[...] = jnp.zeros_like(dg_ref)

        du = jnp.zeros((tm, D_MODEL), f32)
        for a_ref, c0 in zip((a0_ref, a1_ref, a2_ref), starts):
            du = du + jnp.dot(a_ref[...], w_ref[c0:c0 + a_ref.shape[1], :], preferred_element_type=f32)
        dx, dg = _rms_bwd(du, x_ref[...], g_ref[...])
        gx_ref[...] = d_ref[...] + dx
        dg_ref[...] += dg

    row = pl.BlockSpec((tm, D_MODEL), lambda i: (i, 0))
    return _call(
        body, name="in_bwd", grid=(T // tm,),
        in_specs=[pl.BlockSpec((tm, p.shape[1]), lambda i: (i, 0)) for p in pieces]
        + [_resident(w_in.shape), row, _resident((1, D_MODEL)), row],
        out_specs=[row, pl.BlockSpec((1, D_MODEL), lambda i: (0, 0))],
        out_shape=[S((T, D_MODEL), f32), S((1, D_MODEL), f32)],
        args=(dqkv, dglu, dgates, w_in, x, g, dx1), ride=ride)


def _adamw(name, parts, w, m, v, tr, other=None):
    _, R, C = w.shape
    bc1 = 1.0 - ADAM_B1 ** ADAM_STEP
    bc2 = 1.0 - ADAM_B2 ** ADAM_STEP

    def body(*refs):
        if other is None:
            p_ref, w_ref, m_ref, v_ref, g_ref, d_ref, nm_ref, nv_ref = refs
            part = lambda k: p_ref[k].astype(f32)
        else:
            p_ref, p2_ref, t_ref, w_ref, m_ref, v_ref, g_ref, d_ref, nm_ref, nv_ref = refs
            take2 = t_ref[0:1, 0:1] > 0.5
            part = lambda k: jnp.where(take2, p2_ref[k], p_ref[k]).astype(f32)
        g = part(0)
        for k in range(1, N_DEV):
            g = g + part(k)
        nm = ADAM_B1 * m_ref[...] + (1.0 - ADAM_B1) * g
        nv = ADAM_B2 * v_ref[...] + (1.0 - ADAM_B2) * (g * g)
        g_ref[...] = g
        nm_ref[...] = nm
        nv_ref[...] = nv
        d_ref[...] = -ADAM_LR * ((nm / bc1) / (jnp.sqrt(nv / bc2) + ADAM_EPS) + ADAM_WD * w_ref[...])

    blk = pl.BlockSpec((None, tr, C), lambda i: (0, i, 0))
    slabs = pl.BlockSpec((N_DEV, tr, C), lambda i: (0, i, 0))
    if other is None:
        lead_specs, lead_args = [slabs], (parts,)
    else:
        lead_specs = [slabs, slabs, pl.BlockSpec((1, 2 * HEAD_DIM), lambda i: (0, 0))]
        lead_args = (parts, other[0], other[1])
    return pl.pallas_call(
        body, name=name, grid=(R // tr,),
        in_specs=lead_specs + [blk, blk, blk],
        out_specs=[blk, blk, blk, blk], out_shape=[S((1, R, C), f32)] * 4,
        compiler_params=_params("parallel"),
    )(*lead_args, w, m, v)


def _tile(T, pref):
    return min(T, pref)


def _pad_rows(a, rows):
    return jnp.pad(a, ((0, rows - a.shape[0]), (0, 0)))


def kernel(x, norm_mix_g, w_in, q_norm_g, k_norm_g, attn_sinks, rel_bias, w_attn_o, w_dw, b_dw, conv_ln_g, conv_ln_b, w_conv_out, w_out, norm_mlp_g, w_ff1, w_ff2, loss_target, m_norm_mix_g, m_w_in, m_q_norm_g, m_k_norm_g, m_attn_sinks, m_rel_bias, m_w_attn_o, m_w_dw, m_b_dw, m_conv_ln_g, m_conv_ln_b, m_w_conv_out, m_w_out, m_norm_mlp_g, m_w_ff1, m_w_ff2, v_norm_mix_g, v_w_in, v_q_norm_g, v_k_norm_g, v_attn_sinks, v_rel_bias, v_w_attn_o, v_w_dw, v_b_dw, v_conv_ln_g, v_conv_ln_b, v_w_conv_out, v_w_out, v_norm_mlp_g, v_w_ff1, v_w_ff2):
    T = x.shape[1]
    xs = x[0]
    tgt = loss_target[0]
    in_shard = IN_WIDTH // N_DEV
    dw_rows = CONV_WIDTH + 1
    ch_shard = D_MODEL // N_DEV
    tb = _tile(T, 512)
    tt = _tile(T, 2048)
    bucket = jnp.asarray(_t5_bucket_table())

    g_in, g_dw = _exchange("gather_w_in", [w_in[0].astype(bf16).T, _pad_rows(w_dw[0], dw_rows)], gather=True, two_level=True)
    W_in_t = g_in.reshape(IN_WIDTH, D_MODEL)
    W_in = W_in_t.T
    W_dw = jnp.transpose(g_dw, (1, 0, 2)).reshape(dw_rows, D_MODEL)[:CONV_WIDTH]

    mix_shards = _Gather([w_attn_o[0].astype(bf16), w_conv_out[0].astype(bf16), w_out[0].astype(bf16)])
    qg2, kg2 = jnp.tile(q_norm_g, (1, 2)), jnp.tile(k_norm_g, (1, 2))
    (proj, u, qn, kn, vb, h0), (g_ao, g_co, g_o) = _proj_fwd(xs, norm_mix_g, W_in, qg2, kg2, tb, ride=mix_shards)
    W_ao = g_ao.reshape(D_MODEL, D_MODEL)
    W_co = g_co.reshape(D_MODEL, D_MODEL)
    W_o = g_o.reshape(D_MODEL, D_MODEL)
    bias = _bias_table(rel_bias, bucket)
    (o, lse), (g_f1,) = _attn_fwd(qn, kn, vb, bias, attn_sinks, ride=_Gather([w_ff1[0].astype(bf16).T]))
    W_f1t = g_f1.reshape(D_FF, D_MODEL)
    W_f1 = W_f1t.T
    (h1, h3), (g_f2,) = _conv_fwd(h0, W_dw, b_dw, conv_ln_g, conv_ln_b, tb, ride=_Gather([w_ff2[0].astype(bf16)]))
    x1, attn, conv, merged = _mix_fwd(xs, o, h3, proj, W_ao, W_co, W_o, tb)
    W_f2 = g_f2.reshape(D_FF, D_MODEL)
    W_f2t = W_f2.T
    a, u2, dy, dyb, loss_parts = _ffn_fwd(x1, norm_mlp_g, W_f1, W_f2, tgt, tb)
    loss = lax.psum(jnp.sum(loss_parts[:, 0, 0]), ("x", "y", "c"))

    gw_f2 = _wgrad("wgrad_ff2", a, dyb, D_MODEL, D_MODEL, tt, relu2=True).reshape(N_DEV, FF_CHUNK, D_MODEL)
    (da, dx1, dx1b, d_norm_mlp_g), (l_f2,) = _ffn_bwd(dy, dyb, a, x1, norm_mlp_g, W_f1t, W_f2t, tb,
                                                      ride=_Exchange([gw_f2], gather=False))
    gw_f1 = _wgrad("wgrad_ff1", u2, da, D_MODEL, 4 * FF_CHUNK, tt, slab=FF_CHUNK)
    gw_o = _wgrad("wgrad_out", merged, dx1b, D_MODEL, D_MODEL, tt).reshape(N_DEV, ch_shard, D_MODEL)
    (dattn, dconv, do, dh1, dgates, d_ln_g, d_ln_b, d_b_dw), (l_o,) = _mix_bwd(
        dx1b, proj, attn, conv, h1, conv_ln_g, conv_ln_b, W_ao.T, W_co.T, W_o.T, tb, ride=_Exchange([gw_o], gather=False))
    gw_ao = _wgrad("wgrad_attn_o", o, dattn, D_MODEL, D_MODEL, tt).reshape(N_DEV, ch_shard, D_MODEL)
    gw_co = _wgrad("wgrad_conv_out", h3, dconv, D_MODEL, D_MODEL, tt).reshape(N_DEV, ch_shard, D_MODEL)
    (dglu, d_w_dw), (l_f1, l_ao, l_co) = _conv_bwd(dh1, h0, proj, W_dw, tb,
                                                   ride=_Exchange([gw_f1, gw_ao, gw_co], gather=False))
    early = -(-V_END // in_shard)
    cut = early * in_shard - V_END
    g_glu = _wgrad("wgrad_in_glu", u, dglu, D_MODEL, 2 * D_MODEL, tt)
    g_gates = _wgrad("wgrad_in_gates", u, dgates, D_MODEL, 2 * D_MODEL, tt)
    gw_in_late = jnp.concatenate([g_glu[:, cut:], g_gates], axis=1)
    gw_in_late = jnp.transpose(gw_in_late.reshape(D_MODEL, N_DEV - early, in_shard), (1, 0, 2))
    (dq, dk, dv, dbias, d_sinks), (l_in_late,) = _attn_bwd(qn, kn, vb, o, do, lse, bias, attn_sinks,
                                                           ride=_PartExchange([gw_in_late], [(early, N_DEV)]))
    d_sinks = d_sinks[:, :N_Q_HEADS]
    d_rel_bias = _rel_bias_bwd(dbias, bucket)
    dqkv, d_qg, d_kg = _qk_norm_bwd(dq, dk, dv, proj, qg2, kg2, tb)
    d_qg = d_qg[:, :HEAD_DIM] + d_qg[:, HEAD_DIM:]
    d_kg = d_kg[:, :HEAD_DIM] + d_kg[:, HEAD_DIM:]
    gw_in_first = jnp.concatenate([_wgrad("wgrad_in_qkv", u, dqkv, D_MODEL, V_END, tt), g_glu[:, :cut]], axis=1)
    gw_in_first = jnp.transpose(gw_in_first.reshape(D_MODEL, early, in_shard), (1, 0, 2))
    gw_dw = jnp.transpose(d_w_dw.reshape(dw_rows, N_DEV, ch_shard), (1, 0, 2))
    (grad_x, d_norm_mix_g), (l_in_first, l_dw) = _in_bwd(
        dqkv, dglu, dgates, W_in_t, xs, norm_mix_g, dx1, tb,
        ride=_PartExchange([gw_in_first, gw_dw], [(0, early), (0, N_DEV)]))
    my_block = 4 * lax.axis_index("x") + 2 * lax.axis_index("y") + lax.axis_index("c")
    in_late = jnp.where(my_block >= early, 1.0, 0.0) * jnp.ones((1, 2 * HEAD_DIM), f32)

    def row(vec):
        flat = vec.reshape(1, -1)
        return jnp.pad(flat, ((0, 0), (0, D_MODEL - flat.shape[1])))

    def pack_small(nm, qg, kg, sk, rb, bd, lg, lb, nl):
        tail = jnp.concatenate([qg.reshape(1, -1), kg.reshape(1, -1), sk.reshape(1, -1), rb.reshape(1, -1)], axis=1)
        return jnp.concatenate([row(nm), row(bd), row(lg), row(lb), row(nl), row(tail), jnp.zeros((2, D_MODEL), f32)], axis=0)

    def unpack_small(p):
        t = p[5]
        o0, o1, o2 = HEAD_DIM, 2 * HEAD_DIM, 2 * HEAD_DIM + N_Q_HEADS
        return dict(norm_mix_g=p[0:1], b_dw=p[1:2], conv_ln_g=p[2:3], conv_ln_b=p[3:4], norm_mlp_g=p[4:5],
                    q_norm_g=t[0:o0].reshape(1, HEAD_DIM), k_norm_g=t[o0:o1].reshape(1, HEAD_DIM),
                    attn_sinks=t[o1:o2].reshape(1, N_Q_HEADS),
                    rel_bias=t[o2:o2 + N_BUCKETS * N_Q_HEADS].reshape(N_BUCKETS, N_Q_HEADS))

    small_g = pack_small(d_norm_mix_g, d_qg, d_kg, d_sinks, d_rel_bias, d_b_dw, d_ln_g, d_ln_b, d_norm_mlp_g)
    (l_small,) = _exchange("gather_small_grads", [small_g], gather=True)


    res = {}
    res["w_in"] = _adamw("adamw_in", l_in_first, w_in, m_w_in, v_w_in, 256, other=(l_in_late, in_late))
    res["w_attn_o"] = _adamw("adamw_attn_o", l_ao, w_attn_o, m_w_attn_o, v_w_attn_o, ch_shard)
    res["w_conv_out"] = _adamw("adamw_conv_out", l_co, w_conv_out, m_w_conv_out, v_w_conv_out, ch_shard)
    res["w_out"] = _adamw("adamw_out", l_o, w_out, m_w_out, v_w_out, ch_shard)
    res["w_ff1"] = _adamw("adamw_ff1", l_f1, w_ff1, m_w_ff1, v_w_ff1, 256)
    res["w_ff2"] = _adamw("adamw_ff2", l_f2, w_ff2, m_w_ff2, v_w_ff2, 256)
    pad_dw = lambda t: _pad_rows(t[0], dw_rows)[None]
    res["w_dw"] = [t[:, :CONV_WIDTH] for t in _adamw("adamw_dw", l_dw, pad_dw(w_dw), pad_dw(m_w_dw), pad_dw(v_w_dw), dw_rows)]
    small_w = pack_small(norm_mix_g, q_norm_g, k_norm_g, attn_sinks, rel_bias, b_dw, conv_ln_g, conv_ln_b, norm_mlp_g)
    small_m = pack_small(m_norm_mix_g, m_q_norm_g, m_k_norm_g, m_attn_sinks, m_rel_bias, m_b_dw, m_conv_ln_g, m_conv_ln_b, m_norm_mlp_g)
    small_v = pack_small(v_norm_mix_g, v_q_norm_g, v_k_norm_g, v_attn_sinks, v_rel_bias, v_b_dw, v_conv_ln_g, v_conv_ln_b, v_norm_mlp_g)
    small4 = [unpack_small(t[0]) for t in _adamw("adamw_small", l_small, small_w[None], small_m[None], small_v[None], 8)]

    order = ["norm_mix_g", "w_in", "q_norm_g", "k_norm_g", "attn_sinks", "rel_bias", "w_attn_o", "w_dw", "b_dw",
             "conv_ln_g", "conv_ln_b", "w_conv_out", "w_out", "norm_mlp_g", "w_ff1", "w_ff2"]
    stacked = {"w_in", "w_attn_o", "w_dw", "w_conv_out", "w_out", "w_ff1", "w_ff2"}
    outs = [loss, grad_x[None]]
    for k in range(4):
        for nme in order:
            if nme in stacked:
                outs.append(res[nme][k])
            else:
                outs.append(small4[k][nme])
    return tuple(outs)
```

```python
import functools

import numpy as np
import jax
import jax.numpy as jnp
from jax import lax
from jax.experimental import pallas as pl
from jax.experimental.pallas import tpu as pltpu

f32 = jnp.float32
bf16 = jnp.bfloat16
S = jax.ShapeDtypeStruct

N_DEV = 8
D_MODEL = 1024
HEAD_DIM = 64
N_Q_HEADS = 16
N_KV_HEADS = 4
GROUP = N_Q_HEADS // N_KV_HEADS
ATTN_WIDTH = N_Q_HEADS * HEAD_DIM
KV_WIDTH = N_KV_HEADS * HEAD_DIM
QBLOCK = 128
CONV_WIDTH = 31
CONV_HALO = 32
CONV_UNIT = 64
D_FF = 4 * D_MODEL
N_BUCKETS = 32
MAX_DISTANCE = 128
EPS = 1e-6
NEG = -1e30
Q_END = ATTN_WIDTH
K_END = Q_END + KV_WIDTH
V_END = K_END + KV_WIDTH
GLU_END = V_END + 2 * D_MODEL
IN_WIDTH = GLU_END + 2 * D_MODEL
COL = 512
FF_CHUNK = D_FF // N_DEV

ADAM_LR = 0.001
ADAM_B1 = 0.9
ADAM_B2 = 0.999
ADAM_EPS = 1e-08
ADAM_WD = 0.01
ADAM_STEP = 10

VMEM_LIMIT = 56 * 1024 * 1024

MESH_ID = pl.DeviceIdType.MESH
ANY = pl.BlockSpec(memory_space=pl.ANY)
SMEM = pl.BlockSpec(memory_space=pltpu.SMEM)


def _params(*sem):
    return pltpu.CompilerParams(dimension_semantics=sem, vmem_limit_bytes=VMEM_LIMIT)


def _nt(a, b):
    return lax.dot_general(a, b, (((1,), (1,)), ((), ())), preferred_element_type=f32)


def _tn(a, b):
    return lax.dot_general(a, b, (((0,), (0,)), ((), ())), preferred_element_type=f32)


def _sigmoid(z):
    return 1.0 / (1.0 + jnp.exp(-z))


def _t5_bucket_table():
    qi = np.arange(QBLOCK, dtype=np.int32)[:, None]
    kj = np.arange(2 * QBLOCK, dtype=np.int32)[None, :]
    dist = qi + QBLOCK - kj
    n = np.maximum(dist, 0)
    max_exact = N_BUCKETS // 2
    nf = np.maximum(n, 1).astype(np.float32)
    large = max_exact + (np.log(nf / np.float32(max_exact)) / np.float32(np.log(MAX_DISTANCE / max_exact))
                         * np.float32(N_BUCKETS - max_exact)).astype(np.int32)
    large = np.minimum(large, N_BUCKETS - 1)
    bucket = np.where(n < max_exact, n, large)
    valid = (dist >= 0) & (dist < QBLOCK)
    return np.where(valid, bucket, -1).astype(np.int32)


def _peer(d):
    x, y, c = lax.axis_index("x"), lax.axis_index("y"), lax.axis_index("c")
    dx, dy, dc = (d >> 2) & 1, (d >> 1) & 1, d & 1
    px, py, pc = x ^ dx, y ^ dy, c ^ dc
    return (px, py, pc), 4 * px + 2 * py + pc


class _Exchange:
    def __init__(self, arrays, gather):
        self.arrays, self.gather, self.n = list(arrays), gather, len(arrays)
        self.out_shape = [S(((N_DEV,) + a.shape) if gather else a.shape, a.dtype) for a in self.arrays]
        self.scratch = [pltpu.SemaphoreType.DMA((self.n, N_DEV - 1)), pltpu.SemaphoreType.DMA((self.n, N_DEV - 1)),
                        pltpu.SemaphoreType.DMA((self.n,))]

    def _copies(self, ins, outs, sems):
        send_sems, recv_sems, local_sems = sems
        _, me = _peer(0)
        local, sends, recvs = [], [], []
        for k in range(self.n):
            src = ins[k] if self.gather else ins[k].at[me]
            local.append(pltpu.make_async_copy(src, outs[k].at[me], local_sems.at[k]))
        for d in range(1, N_DEV):
            peer, pidx = _peer(d)
            for k in range(self.n):
                src = ins[k] if self.gather else ins[k].at[pidx]
                common = dict(src_ref=src, send_sem=send_sems.at[k, d - 1], recv_sem=recv_sems.at[k, d - 1],
                              device_id=peer, device_id_type=MESH_ID)
                sends.append(pltpu.make_async_remote_copy(dst_ref=outs[k].at[me], **common))
                recvs.append(pltpu.make_async_remote_copy(dst_ref=outs[k].at[pidx], **common))
        return local, sends, recvs

    def start(self, ins, outs, sems):
        local, sends, _ = self._copies(ins, outs, sems)
        for cp in local + sends:
            cp.start()

    def wait(self, ins, outs, sems):
        local, sends, recvs = self._copies(ins, outs, sems)
        for cp in recvs:
            cp.wait_recv()
        for cp in sends:
            cp.wait_send()
        for cp in local:
            cp.wait()


class _Gather:
    CHIPS = (4, 2, 6)
    SLOTS = 1 + 2 * len(CHIPS)

    def __init__(self, arrays):
        self.arrays, self.n = list(arrays), len(arrays)
        self.out_shape = [S((N_DEV,) + a.shape, a.dtype) for a in self.arrays]
        self.scratch = [pltpu.SemaphoreType.DMA((self.n, self.SLOTS)), pltpu.SemaphoreType.DMA((self.n, self.SLOTS)),
                        pltpu.SemaphoreType.DMA((self.n,))]

    @staticmethod
    def _copy(outs, sems, k, slot, src, block, to):
        return pltpu.make_async_remote_copy(src_ref=src, dst_ref=outs[k].at[block], send_sem=sems[0].at[k, slot],
                                            recv_sem=sems[1].at[k, slot], device_id=to, device_id_type=MESH_ID)

    def _local(self, ins, outs, sems):
        _, me = _peer(0)
        return [pltpu.make_async_copy(ins[k], outs[k].at[me], sems[2].at[k]) for k in range(self.n)]

    def start(self, ins, outs, sems):
        _, me = _peer(0)
        sibling, _ = _peer(1)
        for cp in self._local(ins, outs, sems):
            cp.start()
        for k in range(self.n):
            self._copy(outs, sems, k, 0, ins[k], me, sibling).start()
            for j, d in enumerate(self.CHIPS):
                self._copy(outs, sems, k, 1 + j, ins[k], me, _peer(d)[0]).start()

    def mid(self, ins, outs, sems):
        sibling, _ = _peer(1)
        for j, d in enumerate(self.CHIPS):
            chip, block = _peer(d)
            for k in range(self.n):
                self._copy(outs, sems, k, 1 + j, ins[k], block, chip).wait_recv()
                self._copy(outs, sems, k, 4 + j, outs[k].at[block], block, sibling).start()

    def wait(self, ins, outs, sems):
        _, me = _peer(0)
        sibling, sib_block = _peer(1)
        for k in range(self.n):
            self._copy(outs, sems, k, 0, ins[k], sib_block, sibling).wait_recv()
            for j, d in enumerate(self.CHIPS):
                self._copy(outs, sems, k, 4 + j, ins[k], _peer(d ^ 1)[1], sibling).wait_recv()
        for k in range(self.n):
            self._copy(outs, sems, k, 0, ins[k], me, sibling).wait_send()
            for j, d in enumerate(self.CHIPS):
                chip, block = _peer(d)
                self._copy(outs, sems, k, 1 + j, ins[k], me, chip).wait_send()
                self._copy(outs, sems, k, 4 + j, outs[k].at[block], block, sibling).wait_send()
        for cp in self._local(ins, outs, sems):
            cp.wait()


def _exchange(name, arrays, gather, two_level=False):
    ex = _Gather(arrays) if two_level else _Exchange(arrays, gather)
    n = ex.n

    def body(*refs):
        ins, outs, sems = refs[:n], refs[n:2 * n], refs[2 * n:]
        ex.start(ins, outs, sems)
        if two_level:
            ex.mid(ins, outs, sems)
        ex.wait(ins, outs, sems)

    return pl.pallas_call(body, name=name, out_shape=ex.out_shape, in_specs=[ANY] * n, out_specs=[ANY] * n,
                          scratch_shapes=ex.scratch)(*arrays)


def _call(body, *, name, grid, in_specs, out_specs, out_shape, args, scratch_shapes=(), ride=None):
    n_in, n_out, n_sc = len(in_specs), len(out_specs), len(scratch_shapes)
    sem = ("arbitrary",) * len(grid)
    if ride is None:
        res = pl.pallas_call(body, name=name, grid=grid, in_specs=list(in_specs), out_specs=list(out_specs),
                             out_shape=list(out_shape), scratch_shapes=list(scratch_shapes), compiler_params=_params(*sem))(*args)
        return list(res), []
    nx = ride.n

    def riding(*refs):
        ins, xin = refs[:n_in], refs[n_in:n_in + nx]
        outs, xout = refs[n_in + nx:n_in + nx + n_out], refs[n_in + nx + n_out:n_in + 2 * nx + n_out]
        rest = refs[n_in + 2 * nx + n_out:]
        scratch, sems = rest[:n_sc], rest[n_sc:]
        ids = [pl.program_id(ax) for ax in range(len(grid))]
        first = functools.reduce(jnp.logical_and, [i == 0 for i in ids])
        last = functools.reduce(jnp.logical_and, [i == g - 1 for i, g in zip(ids, grid)])

        @pl.when(first)
        def _():
            ride.start(xin, xout, sems)

        if hasattr(ride, "mid"):
            halfway = functools.reduce(jnp.logical_and, [ids[0] == grid[0] // 2] + [i == 0 for i in ids[1:]])

            @pl.when(halfway)
            def _():
                ride.mid(xin, xout, sems)

        body(*ins, *outs, *scratch)

        @pl.when(last)
        def _():
            ride.wait(xin, xout, sems)

    res = pl.pallas_call(
        riding, name=name, grid=grid, in_specs=list(in_specs) + [ANY] * nx, out_specs=list(out_specs) + [ANY] * nx,
        out_shape=list(out_shape) + ride.out_shape, scratch_shapes=list(scratch_shapes) + ride.scratch,
        compiler_params=_params(*sem))(*args, *ride.arrays)
    return list(res[:n_out]), list(res[n_out:])


def _resident(shape):
    return pl.BlockSpec(shape, lambda *_: (0,) * len(shape), pipeline_mode=pl.Buffered(1))


def _proj_fwd(x, g, w, qg, kg, tm, ride=None):
    T, K = x.shape
    N = w.shape[1]
    per = COL // (2 * HEAD_DIM)
    assert Q_END % COL == 0 and V_END == Q_END + COL and (GLU_END - V_END) == 4 * COL and KV_WIDTH == COL // 2

    def body(x_ref, g_ref, w_ref, qg_ref, kg_ref, o_ref, u_ref, qn_ref, kn_ref, vb_ref, h0_ref):
        xv = x_ref[...]
        r = lax.rsqrt(jnp.mean(xv * xv, axis=-1, keepdims=True) + EPS)
        u = (xv * r * g_ref[...]).astype(bf16)
        u_ref[...] = u

        def block(c):
            cs = slice(c * COL, (c + 1) * COL)
            pc = jnp.dot(u, w_ref[:, cs], preferred_element_type=f32)
            o_ref[:, cs] = pc.astype(bf16)
            return pc

        qgv = qg_ref[...] * (HEAD_DIM ** -0.5)
        for c in range(Q_END // COL):
            pc = block(c)
            for t in range(per):
                xq = pc[:, _pair_cols(t)]
                qn_ref[:, _pair_cols(c * per + t)] = (xq * _pair_rstd(xq, False) * qgv).astype(bf16)
        pc = block(Q_END // COL)
        for t in range(KV_WIDTH // (2 * HEAD_DIM)):
            xk = pc[:, _pair_cols(t)]
            kn_ref[:, _pair_cols(t)] = (xk * _pair_rstd(xk, False) * kg_ref[...]).astype(bf16)
        vb_ref[...] = pc[:, KV_WIDTH:].astype(bf16)
        a0 = V_END // COL
        for half in range(2):
            gate = block(a0 + 2 + half)
            h0_ref[:, half * COL:(half + 1) * COL] = block(a0 + half) * _sigmoid(gate)
        for c in range(GLU_END // COL, N // COL):
            block(c)

    row = lambda width: pl.BlockSpec((tm, width), lambda i: (i, 0))
    return _call(
        body, name="proj_fwd", grid=(T // tm,),
        in_specs=[row(K), _resident((1, K)), _resident((K, N)), _resident((1, 2 * HEAD_DIM)), _resident((1, 2 * HEAD_DIM))],
        out_specs=[row(N), row(K), row(ATTN_WIDTH), row(KV_WIDTH), row(KV_WIDTH), row(D_MODEL)],
        out_shape=[S((T, N), bf16), S((T, K), bf16), S((T, ATTN_WIDTH), bf16), S((T, KV_WIDTH), bf16), S((T, KV_WIDTH), bf16),
                   S((T, D_MODEL), f32)],
        args=(x, g, w, qg, kg), ride=ride)


def _bias_table(rel_bias, bucket):
    def body(rb_ref, bk_ref, o_ref):
        b = bk_ref[...]
        absent = lax.broadcasted_iota(jnp.int32, (QBLOCK, 2 * QBLOCK), 1) < QBLOCK
        for h in range(N_Q_HEADS):
            acc = jnp.full((QBLOCK, 2 * QBLOCK), NEG, f32)
            for k in range(N_BUCKETS):
                acc = jnp.where(b == k, rb_ref[k, h], acc)
            o_ref[0, h * QBLOCK:(h + 1) * QBLOCK, :] = acc
            o_ref[1, h * QBLOCK:(h + 1) * QBLOCK, :] = jnp.where(absent, NEG, acc)

    return pl.pallas_call(
        body, name="bias_table", out_shape=S((2, N_Q_HEADS * QBLOCK, 2 * QBLOCK), f32),
        in_specs=[SMEM, pl.BlockSpec(memory_space=pltpu.VMEM)],
    )(rel_bias, bucket)


def _bias_spec():
    return pl.BlockSpec((None, N_Q_HEADS * QBLOCK, 2 * QBLOCK), lambda n: (jnp.where(n == 0, 1, 0), 0, 0))


def _swap_halves(t):
    return jnp.concatenate([t[:, HEAD_DIM:], t[:, :HEAD_DIM]], axis=1)


def _low_lanes():
    return lax.broadcasted_iota(jnp.int32, (1, 2 * HEAD_DIM), 1) < HEAD_DIM


def _one_head(pair, side):
    zero = jnp.zeros((), pair.dtype)
    return jnp.where(_low_lanes(), pair, zero) if side == 0 else jnp.where(_low_lanes(), zero, pair)


def _pair_mean(t, on_mxu):
    if not on_mxu:
        m_lo = jnp.sum(_one_head(t, 0), axis=-1, keepdims=True) * (1.0 / HEAD_DIM)
        m_hi = jnp.sum(_one_head(t, 1), axis=-1, keepdims=True) * (1.0 / HEAD_DIM)
        return jnp.where(_low_lanes(), m_lo, m_hi)
    width = 2 * HEAD_DIM
    same_head = ((lax.broadcasted_iota(jnp.int32, (width, width), 0) < HEAD_DIM)
                 == (lax.broadcasted_iota(jnp.int32, (width, width), 1) < HEAD_DIM))
    e = jnp.where(same_head, 1.0 / HEAD_DIM, 0.0).astype(bf16)
    hi = t.astype(bf16)
    lo = (t - hi.astype(f32)).astype(bf16)
    return jnp.dot(hi, e, preferred_element_type=f32) + jnp.dot(lo, e, preferred_element_type=f32)


def _pair_rstd(x, on_mxu):
    return lax.rsqrt(_pair_mean(x * x, on_mxu) + EPS)


def _kv_placements(band):
    out = {}
    for m in range(N_KV_HEADS // 2):
        pair = band[:, m * 2 * HEAD_DIM:(m + 1) * 2 * HEAD_DIM]
        swapped = _swap_halves(pair)
        for hh in range(2):
            out[2 * m + hh, 0] = _one_head(pair if hh == 0 else swapped, 0)
            out[2 * m + hh, 1] = _one_head(swapped if hh == 0 else pair, 1)
    return out


def _head_rows(hq):
    return slice(hq * QBLOCK, (hq + 1) * QBLOCK)


def _pair_cols(pr):
    return slice(pr * 2 * HEAD_DIM, (pr + 1) * 2 * HEAD_DIM)


def _attn_fwd(qn, kn, vb, bias, sinks, ride=None):
    T = qn.shape[0]
    nb = T // QBLOCK

    def body(q_ref, kc_ref, kp_ref, vc_ref, vp_ref, b_ref, s_ref, o_ref, lse_ref, s_scr, p_scr):
        lane = lax.broadcasted_iota(jnp.int32, (QBLOCK, 2 * HEAD_DIM), 1)
        kx = _kv_placements(jnp.concatenate([kp_ref[...], kc_ref[...]], axis=0))
        vx = _kv_placements(jnp.concatenate([vp_ref[...], vc_ref[...]], axis=0))
        for hq in range(N_Q_HEADS):
            qm = _one_head(q_ref[:, _pair_cols(hq // 2)], hq % 2)
            s_scr[_head_rows(hq), :] = _nt(qm, kx[hq // GROUP, hq % 2]) + b_ref[_head_rows(hq), :]
        lse_tile = jnp.zeros((QBLOCK, 2 * HEAD_DIM), f32)
        for hq in range(N_Q_HEADS):
            s = s_scr[_head_rows(hq), :]
            sink = s_ref[0, hq]
            m = jnp.maximum(jnp.max(s, axis=-1, keepdims=True), sink)
            p = jnp.exp(s - m)
            l = jnp.sum(p, axis=-1, keepdims=True) + jnp.exp(sink - m)
            p_scr[_head_rows(hq), :] = (p * (1.0 / l)).astype(bf16)
            lse_tile = jnp.where(lane == hq, m + jnp.log(l), lse_tile)
        lse_ref[...] = lse_tile
        for pr in range(N_Q_HEADS // 2):
            h = 2 * pr // GROUP
            o_pair = (jnp.dot(p_scr[_head_rows(2 * pr), :], vx[h, 0], preferred_element_type=f32)
                      + jnp.dot(p_scr[_head_rows(2 * pr + 1), :], vx[h, 1], preferred_element_type=f32))
            o_ref[:, _pair_cols(pr)] = o_pair.astype(bf16)

    cur = lambda n: (n, 0)
    prev = lambda n: (jnp.maximum(n - 1, 0), 0)
    return _call(
        body, name="attn_fwd", grid=(nb,),
        in_specs=[pl.BlockSpec((QBLOCK, ATTN_WIDTH), cur), pl.BlockSpec((QBLOCK, KV_WIDTH), cur),
                  pl.BlockSpec((QBLOCK, KV_WIDTH), prev), pl.BlockSpec((QBLOCK, KV_WIDTH), cur),
                  pl.BlockSpec((QBLOCK, KV_WIDTH), prev), _bias_spec(), SMEM],
        out_specs=[pl.BlockSpec((QBLOCK, ATTN_WIDTH), cur), pl.BlockSpec((QBLOCK, 2 * HEAD_DIM), cur)],
        out_shape=[S((T, ATTN_WIDTH), bf16), S((T, 2 * HEAD_DIM), f32)],
        scratch_shapes=[pltpu.VMEM((N_Q_HEADS * QBLOCK, 2 * QBLOCK), f32), pltpu.VMEM((N_Q_HEADS * QBLOCK, 2 * QBLOCK), bf16)],
        args=(qn, kn, kn, vb, vb, bias, sinks), ride=ride)


def _layer_norm_stats(h1):
    mu = jnp.mean(h1, axis=-1, keepdims=True)
    xc = h1 - mu
    rstd = lax.rsqrt(jnp.mean(xc * xc, axis=-1, keepdims=True) + EPS)
    return xc * rstd, rstd


def _advanced_windows(win):
    rows = win.shape[0]
    for r in range(8):
        yield r, (win if r == 0 else pltpu.roll(win, rows - r, 0))


def _tap_offsets(r, rows):
    for q in range((rows - CONV_UNIT) // 8 + 1):
        if r == 0 or 8 * q + r + CONV_UNIT <= rows:
            yield q, 8 * q + r


def _conv_fwd(h0, w_dw, b_dw, ln_g, ln_b, tm, ride=None):
    T = h0.shape[0]
    per = tm // CONV_HALO
    lead = CONV_HALO - (CONV_WIDTH - 1)

    def body(hc_ref, hp_ref, w_ref, b_ref, g_ref, bb_ref, h1_ref, h3_ref, cat):
        i = pl.program_id(0)
        cat[0:CONV_HALO, :] = jnp.where(i == 0, 0.0, hp_ref[...])
        cat[CONV_HALO:, :] = hc_ref[...]

        def unit_rows(c, carry):
            r0 = pl.multiple_of(c * CONV_UNIT, CONV_UNIT)
            for j in range(D_MODEL // 128):
                ls = slice(j * 128, (j + 1) * 128)
                win = cat[pl.ds(r0, CONV_UNIT + CONV_HALO), ls]
                acc = jnp.zeros((CONV_UNIT, 128), f32) + b_ref[:, ls]
                for r, adv in _advanced_windows(win):
                    for q, off in _tap_offsets(r, CONV_UNIT + CONV_HALO):
                        k = off - lead
                        if 0 <= k < CONV_WIDTH:
                            acc = acc + adv[8 * q:8 * q + CONV_UNIT] * w_ref[k:k + 1, ls]
                h1_ref[pl.ds(r0, CONV_UNIT), ls] = acc
            return carry

        lax.fori_loop(0, tm // CONV_UNIT, unit_rows, 0)
        acc = h1_ref[...]
        xhat, _ = _layer_norm_stats(acc)
        h2 = xhat * g_ref[...] + bb_ref[...]
        h3_ref[...] = (h2 * _sigmoid(h2)).astype(bf16)

    vec = pl.BlockSpec((1, D_MODEL), lambda i: (0, 0))
    return _call(
        body, name="conv_fwd", grid=(T // tm,),
        in_specs=[pl.BlockSpec((tm, D_MODEL), lambda i: (i, 0)),
                  pl.BlockSpec((CONV_HALO, D_MODEL), lambda i: (jnp.maximum(i * per - 1, 0), 0)),
                  pl.BlockSpec((CONV_WIDTH, D_MODEL), lambda i: (0, 0)), vec, vec, vec],
        out_specs=[pl.BlockSpec((tm, D_MODEL), lambda i: (i, 0)), pl.BlockSpec((tm, D_MODEL), lambda i: (i, 0))],
        out_shape=[S((T, D_MODEL), f32), S((T, D_MODEL), bf16)],
        scratch_shapes=[pltpu.VMEM((tm + CONV_HALO, D_MODEL), f32)],
        args=(h0, h0, w_dw, b_dw, ln_g, ln_b), ride=ride)


def _mix_fwd(x, o, h3, proj, w_ao, w_co, w_o, tm):
    T = x.shape[0]
    row = pl.BlockSpec((tm, D_MODEL), lambda i: (i, 0))
    wsp = _resident((D_MODEL, D_MODEL))
    g0 = GLU_END // COL

    def gate_spec(off):
        return pl.BlockSpec((tm, COL), lambda i: (i, g0 + off))

    def body(x_ref, o_ref, h3_ref, ga0, ga1, gc0, gc1, wa_ref, wc_ref, wo_ref, x1_ref, at_ref, cv_ref, mg_ref):
        attn = jnp.dot(o_ref[...], wa_ref[...], preferred_element_type=f32)
        conv = jnp.dot(h3_ref[...], wc_ref[...], preferred_element_type=f32)
        ga = jnp.concatenate([ga0[...], ga1[...]], axis=-1).astype(f32)
        gc = jnp.concatenate([gc0[...], gc1[...]], axis=-1).astype(f32)
        merged = (_sigmoid(ga) * attn + _sigmoid(gc) * conv).astype(bf16)
        at_ref[...] = attn.astype(bf16)
        cv_ref[...] = conv.astype(bf16)
        mg_ref[...] = merged
        x1_ref[...] = x_ref[...] + jnp.dot(merged, wo_ref[...], preferred_element_type=f32)

    return pl.pallas_call(
        body, name="mix_fwd", grid=(T // tm,),
        in_specs=[row, row, row, gate_spec(0), gate_spec(1), gate_spec(2), gate_spec(3), wsp, wsp, wsp],
        out_specs=[row, row, row, row],
        out_shape=[S((T, D_MODEL), f32), S((T, D_MODEL), bf16), S((T, D_MODEL), bf16), S((T, D_MODEL), bf16)],
        compiler_params=_params("parallel"),
    )(x, o, h3, proj, proj, proj, proj, w_ao, w_co, w_o)


def _ffn_fwd(x1, g, w1, w2, target, tm):
    T = x1.shape[0]
    nj = w1.shape[0] // FF_CHUNK

    def body(x_ref, g_ref, w1_ref, w2_ref, t_ref, a_ref, u_ref, dy_ref, dyb_ref, ls_ref, hm):
        xv = x_ref[...]
        r = lax.rsqrt(jnp.mean(xv * xv, axis=-1, keepdims=True) + EPS)
        u = (xv * r * g_ref[...]).astype(bf16)
        u_ref[...] = u
        for j in range(nj):
            js = slice(j * FF_CHUNK, (j + 1) * FF_CHUNK)
            a = _nt(u, w1_ref[js, :])
            a_ref[:, js] = a.astype(bf16)
            hm[:, js] = jnp.square(jnp.maximum(a, 0.0)).astype(bf16)
        err = xv + jnp.dot(hm[...], w2_ref[...], preferred_element_type=f32) - t_ref[...]
        dy = err * (1.0 / D_MODEL)
        dy_ref[...] = dy
        dyb_ref[...] = dy.astype(bf16)
        ls_ref[...] = jnp.zeros((8, 128), f32) + jnp.sum(err * err) * (0.5 / D_MODEL)

    row = pl.BlockSpec((tm, D_MODEL), lambda i: (i, 0))
    wide = pl.BlockSpec((tm, D_FF), lambda i: (i, 0))
    return pl.pallas_call(
        body, name="ffn_fwd", grid=(T // tm,),
        in_specs=[row, _resident((1, D_MODEL)), _resident(w1.shape), _resident(w2.shape), row],
        out_specs=[wide, row, row, row, pl.BlockSpec((None, 8, 128), lambda i: (i, 0, 0))],
        out_shape=[S((T, D_FF), bf16), S((T, D_MODEL), bf16), S((T, D_MODEL), f32), S((T, D_MODEL), bf16),
                   S((T // tm, 8, 128), f32)],
        scratch_shapes=[pltpu.VMEM((tm, D_FF), bf16)],
        compiler_params=_params("parallel"),
    )(x1, g, w1, w2, target)


def _rms_bwd(du, xv, gv):
    r = lax.rsqrt(jnp.mean(xv * xv, axis=-1, keepdims=True) + EPS)
    xn = xv * r
    dg = jnp.sum(du * xn, axis=0, keepdims=True)
    dxn = du * gv
    dx = r * (dxn - xn * jnp.mean(dxn * xn, axis=-1, keepdims=True))
    return dx, dg


def _ffn_bwd(dy, dyb, a, x1, g, w1, w2, tm, ride=None):
    T = dy.shape[0]
    nj = w1.shape[0] // FF_CHUNK

    def body(dy_ref, dyb_ref, a_ref, x_ref, g_ref, w1_ref, w2_ref, da_ref, dx_ref, dxb_ref, dg_ref):
        @pl.when(pl.program_id(0) == 0)
        def _():
            dg_ref[...] = jnp.zeros_like(dg_ref)

        dyb_v = dyb_ref[...]
        for j in range(nj):
            js = slice(j * FF_CHUNK, (j + 1) * FF_CHUNK)
            dh = _nt(dyb_v, w2_ref[js, :])
            da_ref[:, js] = (dh * (2.0 * jnp.maximum(a_ref[:, js].astype(f32), 0.0))).astype(bf16)
        du = jnp.dot(da_ref[...], w1_ref[...], preferred_element_type=f32)
        dx, dg = _rms_bwd(du, x_ref[...], g_ref[...])
        dx1 = dy_ref[...] + dx
        dx_ref[...] = dx1
        dxb_ref[...] = dx1.astype(bf16)
        dg_ref[...] += dg

    row = pl.BlockSpec((tm, D_MODEL), lambda i: (i, 0))
    wide = pl.BlockSpec((tm, D_FF), lambda i: (i, 0))
    vec = pl.BlockSpec((1, D_MODEL), lambda i: (0, 0))
    return _call(
        body, name="ffn_bwd", grid=(T // tm,),
        in_specs=[row, row, wide, row, _resident((1, D_MODEL)), _resident(w1.shape), _resident(w2.shape)],
        out_specs=[wide, row, row, vec],
        out_shape=[S((T, D_FF), bf16), S((T, D_MODEL), f32), S((T, D_MODEL), bf16), S((1, D_MODEL), f32)],
        args=(dy, dyb, a, x1, g, w1, w2), ride=ride)


def _wgrad(name, a, b, tk, tn, tt, relu2=False, slab=None, out_dtype=bf16):
    T, Ka = a.shape
    Nb = b.shape[1]
    nt = T // tt

    def body(a_ref, b_ref, o_ref, acc):
        t = pl.program_id(2)
        av = a_ref[...]
        if relu2:
            av = jnp.square(jnp.maximum(av.astype(f32), 0.0))
        prod = _tn(av.astype(bf16), b_ref[...].astype(bf16))

        @pl.when(t == 0)
        def _():
            acc[...] = prod

        @pl.when(t > 0)
        def _():
            acc[...] += prod

        @pl.when(t == nt - 1)
        def _():
            if slab is None:
                o_ref[...] = acc[...].astype(out_dtype)
            else:
                for s in range(tn // slab):
                    o_ref[s] = acc[:, s * slab:(s + 1) * slab].astype(out_dtype)

    if slab is not None:
        out_shape = S((Nb // slab, Ka, slab), out_dtype)
        out_spec = pl.BlockSpec((tn // slab, tk, slab), lambda i, j, t: (j, i, 0))
    else:
        out_shape = S((Ka, Nb), out_dtype)
        out_spec = pl.BlockSpec((tk, tn), lambda i, j, t: (i, j))
    return pl.pallas_call(
        body, name=name, grid=(Ka // tk, Nb // tn, nt),
        in_specs=[pl.BlockSpec((tt, tk), lambda i, j, t: (t, i)), pl.BlockSpec((tt, tn), lambda i, j, t: (t, j))],
        out_specs=out_spec, out_shape=out_shape, scratch_shapes=[pltpu.VMEM((tk, tn), f32)],
        compiler_params=_params("parallel", "parallel", "arbitrary"),
    )(a, b)


def _mix_bwd(dx1, proj, attn, conv, h1, ln_g, ln_b, w_ao, w_co, w_o, tm, ride=None):
    T = dx1.shape[0]
    g0 = GLU_END // COL

    def gate_spec(off):
        return pl.BlockSpec((tm, COL), lambda i: (i, g0 + off))

    def body(dx_ref, ga0, ga1, gc0, gc1, at_ref, cv_ref, h_ref, g_ref, b_ref, wa_ref, wc_ref, wo_ref,
             da_ref, dc_ref, do_ref, dh1_ref, dg_ref, dlg_ref, dlb_ref, dbd_ref):
        @pl.when(pl.program_id(0) == 0)
        def _():
            dlg_ref[...] = jnp.zeros_like(dlg_ref)
            dlb_ref[...] = jnp.zeros_like(dlb_ref)
            dbd_ref[...] = jnp.zeros_like(dbd_ref)

        dm = _nt(dx_ref[...].astype(bf16), wo_ref[...])
        sa = _sigmoid(jnp.concatenate([ga0[...], ga1[...]], axis=-1).astype(f32))
        sc = _sigmoid(jnp.concatenate([gc0[...], gc1[...]], axis=-1).astype(f32))
        dattn = (dm * sa).astype(bf16)
        dconv = (dm * sc).astype(bf16)
        da_ref[...] = dattn
        dc_ref[...] = dconv
        dg_ref[:, 0:D_MODEL] = (dm * at_ref[...].astype(f32) * sa * (1.0 - sa)).astype(bf16)
        dg_ref[:, D_MODEL:2 * D_MODEL] = (dm * cv_ref[...].astype(f32) * sc * (1.0 - sc)).astype(bf16)
        do_ref[...] = _nt(dattn, wa_ref[...]).astype(bf16)
        dh3 = _nt(dconv, wc_ref[...])
        xhat, rstd = _layer_norm_stats(h_ref[...])
        h2 = xhat * g_ref[...] + b_ref[...]
        sg = _sigmoid(h2)
        dh2 = dh3 * (sg * (1.0 + h2 * (1.0 - sg)))
        dlg_ref[...] += jnp.sum(dh2 * xhat, axis=0, keepdims=True)
        dlb_ref[...] += jnp.sum(dh2, axis=0, keepdims=True)
        dxh = dh2 * g_ref[...]
        dh1 = rstd * (dxh - jnp.mean(dxh, axis=-1, keepdims=True) - xhat * jnp.mean(dxh * xhat, axis=-1, keepdims=True))
        dh1_ref[...] = dh1
        dbd_ref[...] += jnp.sum(dh1, axis=0, keepdims=True)

    row = pl.BlockSpec((tm, D_MODEL), lambda i: (i, 0))
    vec = pl.BlockSpec((1, D_MODEL), lambda i: (0, 0))
    par = _resident((1, D_MODEL))
    wsp = _resident((D_MODEL, D_MODEL))
    return _call(
        body, name="mix_bwd", grid=(T // tm,),
        in_specs=[row, gate_spec(0), gate_spec(1), gate_spec(2), gate_spec(3), row, row, row, par, par, wsp, wsp, wsp],
        out_specs=[row, row, row, row, pl.BlockSpec((tm, 2 * D_MODEL), lambda i: (i, 0)), vec, vec, vec],
        out_shape=[S((T, D_MODEL), bf16), S((T, D_MODEL), bf16), S((T, D_MODEL), bf16), S((T, D_MODEL), f32),
                   S((T, 2 * D_MODEL), bf16), S((1, D_MODEL), f32), S((1, D_MODEL), f32), S((1, D_MODEL), f32)],
        args=(dx1, proj, proj, proj, proj, attn, conv, h1, ln_g, ln_b, w_ao, w_co, w_o), ride=ride)


def _conv_bwd(dh1, h0, proj, w_dw, tm, ride=None):
    T = dh1.shape[0]
    per = tm // CONV_HALO
    nh = T // CONV_HALO
    nt = T // tm
    a0 = V_END // COL
    lead = CONV_HALO - (CONV_WIDTH - 1)

    def body(dc_ref, dn_ref, hc_ref, hp_ref, a0_ref, a1_ref, g0_ref, g1_ref, w_ref, dglu_ref, dw_ref, dcat, hcat, wacc, dh0):
        i = pl.program_id(0)

        @pl.when(i == 0)
        def _():
            wacc[...] = jnp.zeros_like(wacc)

        dcat[0:tm, :] = dc_ref[...]
        dcat[tm:, :] = jnp.where(i == nt - 1, 0.0, dn_ref[...])
        hcat[0:CONV_HALO, :] = jnp.where(i == 0, 0.0, hp_ref[...])
        hcat[CONV_HALO:, :] = hc_ref[...]
        span = CONV_UNIT + CONV_HALO

        def unit_rows(c, carry):
            r0 = pl.multiple_of(c * CONV_UNIT, CONV_UNIT)
            for j in range(D_MODEL // 128):
                ls = slice(j * 128, (j + 1) * 128)
                dwin = dcat[pl.ds(r0, span), ls]
                acc = jnp.zeros((CONV_UNIT, 128), f32)
                for r, adv in _advanced_windows(dwin):
                    for q, off in _tap_offsets(r, span):
                        k = CONV_WIDTH - 1 - off
                        if 0 <= k < CONV_WIDTH:
                            acc = acc + adv[8 * q:8 * q + CONV_UNIT] * w_ref[k:k + 1, ls]
                dh0[pl.ds(r0, CONV_UNIT), ls] = acc
                dcur = dwin[0:CONV_UNIT]
                for r, adv in _advanced_windows(hcat[pl.ds(r0, span), ls]):
                    for q, off in _tap_offsets(r, span):
                        k = off - lead
                        if 0 <= k < CONV_WIDTH:
                            prod = dcur * adv[8 * q:8 * q + CONV_UNIT]
                            wacc[k, :, ls] += jnp.sum(prod.reshape(CONV_UNIT // 8, 8, 128), axis=0)
            return carry

        lax.fori_loop(0, tm // CONV_UNIT, unit_rows, 0)
        dh0v = dh0[...]
        av = jnp.concatenate([a0_ref[...], a1_ref[...]], axis=-1).astype(f32)
        sg = _sigmoid(jnp.concatenate([g0_ref[...], g1_ref[...]], axis=-1).astype(f32))
        dglu_ref[:, 0:D_MODEL] = (dh0v * sg).astype(bf16)
        dglu_ref[:, D_MODEL:2 * D_MODEL] = (dh0v * av * sg * (1.0 - sg)).astype(bf16)

        @pl.when(i == nt - 1)
        def _():
            for k in range(CONV_WIDTH):
                dw_ref[k:k + 1, :] = jnp.sum(wacc[k], axis=0, keepdims=True)
            dw_ref[CONV_WIDTH:CONV_WIDTH + 1, :] = jnp.zeros((1, D_MODEL), f32)

    row = pl.BlockSpec((tm, D_MODEL), lambda i: (i, 0))

    def col_spec(off):
        return pl.BlockSpec((tm, COL), lambda i: (i, a0 + off))

    return _call(
        body, name="conv_bwd", grid=(nt,),
        in_specs=[row, pl.BlockSpec((CONV_HALO, D_MODEL), lambda i: (jnp.minimum((i + 1) * per, nh - 1), 0)),
                  row, pl.BlockSpec((CONV_HALO, D_MODEL), lambda i: (jnp.maximum(i * per - 1, 0), 0)),
                  col_spec(0), col_spec(1), col_spec(2), col_spec(3),
                  pl.BlockSpec((CONV_WIDTH, D_MODEL), lambda i: (0, 0))],
        out_specs=[pl.BlockSpec((tm, 2 * D_MODEL), lambda i: (i, 0)), pl.BlockSpec((CONV_WIDTH + 1, D_MODEL), lambda i: (0, 0))],
        out_shape=[S((T, 2 * D_MODEL), bf16), S((CONV_WIDTH + 1, D_MODEL), f32)],
        scratch_shapes=[pltpu.VMEM((tm + CONV_HALO, D_MODEL), f32), pltpu.VMEM((tm + CONV_HALO, D_MODEL), f32),
                        pltpu.VMEM((CONV_WIDTH, 8, D_MODEL), f32), pltpu.VMEM((tm, D_MODEL), f32)],
        args=(dh1, dh1, h0, h0, proj, proj, proj, proj, w_dw), ride=ride)


def _attn_bwd(qn, kn, vb, o, do, lse, bias, sinks, ride=None):
    T = qn.shape[0]
    nb = T // QBLOCK

    def body(q_ref, kc_ref, kp_ref, vc_ref, vp_ref, o_ref, do_ref, lse_ref, b_ref, s_ref,
             dq_ref, dk_ref, dv_ref, db_ref, dsk_ref, kcar, vcar, s_scr, dp_scr, p_scr, ds_scr):
        n = pl.program_id(0)

        @pl.when(n == 0)
        def _():
            db_ref[...] = jnp.zeros_like(db_ref)
            dsk_ref[...] = jnp.zeros_like(dsk_ref)
            kcar[...] = jnp.zeros_like(kcar)
            vcar[...] = jnp.zeros_like(vcar)

        @pl.when(n < nb)
        def _():
            lane = lax.broadcasted_iota(jnp.int32, (QBLOCK, 2 * HEAD_DIM), 1)
            lane_row = lax.broadcasted_iota(jnp.int32, (1, 2 * HEAD_DIM), 1)
            kx = _kv_placements(jnp.concatenate([kp_ref[...], kc_ref[...]], axis=0))
            vx = _kv_placements(jnp.concatenate([vp_ref[...], vc_ref[...]], axis=0))
            lse_tile = lse_ref[...]
            delta, lse_c = {}, {}
            for pr in range(N_Q_HEADS // 2):
                dop = do_ref[:, _pair_cols(pr)]
                dl = dop.astype(f32) * o_ref[:, _pair_cols(pr)].astype(f32)
                for side in range(2):
                    hq = 2 * pr + side
                    h = hq // GROUP
                    qm = _one_head(q_ref[:, _pair_cols(pr)], side)
                    s_scr[_head_rows(hq), :] = _nt(qm, kx[h, side]) + b_ref[_head_rows(hq), :]
                    dp_scr[_head_rows(hq), :] = _nt(_one_head(dop, side), vx[h, side])
                    delta[hq] = jnp.sum(_one_head(dl, side), axis=-1, keepdims=True)
                    lse_c[hq] = jnp.sum(jnp.where(lane == hq, lse_tile, 0.0), axis=-1, keepdims=True)
            dsk = jnp.zeros((1, 2 * HEAD_DIM), f32)
            for hq in range(N_Q_HEADS):
                p = jnp.exp(s_scr[_head_rows(hq), :] - lse_c[hq])
                ds = p * (dp_scr[_head_rows(hq), :] - delta[hq])
                db_ref[_head_rows(hq), :] += ds
                p_scr[_head_rows(hq), :] = p.astype(bf16)
                ds_scr[_head_rows(hq), :] = ds.astype(bf16)
                psink = jnp.exp(s_ref[0, hq] - lse_c[hq])
                dsk = dsk - jnp.where(lane_row == hq, jnp.sum(psink * delta[hq], axis=0, keepdims=True), 0.0)
            dsk_ref[...] += dsk
            for pr in range(N_Q_HEADS // 2):
                h = 2 * pr // GROUP
                dq_ref[:, _pair_cols(pr)] = (jnp.dot(ds_scr[_head_rows(2 * pr), :], kx[h, 0], preferred_element_type=f32)
                                             + jnp.dot(ds_scr[_head_rows(2 * pr + 1), :], kx[h, 1], preferred_element_type=f32))
            folded_k, folded_v = [], []
            for h in range(N_KV_HEADS):
                ka = jnp.zeros((2 * QBLOCK, 2 * HEAD_DIM), f32)
                va = jnp.zeros((2 * QBLOCK, 2 * HEAD_DIM), f32)
                for g in range(GROUP):
                    hq = h * GROUP + g
                    ka = ka + _tn(ds_scr[_head_rows(hq), :], _one_head(q_ref[:, _pair_cols(hq // 2)], hq % 2))
                    va = va + _tn(p_scr[_head_rows(hq), :], _one_head(do_ref[:, _pair_cols(hq // 2)], hq % 2))
                folded_k.append(ka + _swap_halves(ka))
                folded_v.append(va + _swap_halves(va))
            low = _low_lanes()
            for m in range(N_KV_HEADS // 2):
                cs = _pair_cols(m)
                for folded, out_ref, car in ((folded_k, dk_ref, kcar), (folded_v, dv_ref, vcar)):
                    band = jnp.where(low, folded[2 * m], folded[2 * m + 1])
                    out_ref[:, cs] = car[:, cs] + band[0:QBLOCK, :]
                    car[:, cs] = band[QBLOCK:, :]

        @pl.when(n == nb)
        def _():
            dk_ref[...] = kcar[...]
            dv_ref[...] = vcar[...]

    cur = lambda n: (jnp.minimum(n, nb - 1), 0)
    prev = lambda n: (jnp.clip(n - 1, 0, nb - 1), 0)
    qspec = pl.BlockSpec((QBLOCK, ATTN_WIDTH), cur)
    kcur, kprev = pl.BlockSpec((QBLOCK, KV_WIDTH), cur), pl.BlockSpec((QBLOCK, KV_WIDTH), prev)
    whole = lambda shape: pl.BlockSpec(shape, lambda n: (0,) * len(shape))
    scores = (N_Q_HEADS * QBLOCK, 2 * QBLOCK)
    return _call(
        body, name="attn_bwd", grid=(nb + 1,),
        in_specs=[qspec, kcur, kprev, kcur, kprev, qspec, qspec, pl.BlockSpec((QBLOCK, 2 * HEAD_DIM), cur), _bias_spec(), SMEM],
        out_specs=[qspec, kprev, kprev, whole(scores), whole((1, 2 * HEAD_DIM))],
        out_shape=[S((T, ATTN_WIDTH), f32), S((T, KV_WIDTH), f32), S((T, KV_WIDTH), f32), S(scores, f32),
                   S((1, 2 * HEAD_DIM), f32)],
        scratch_shapes=[pltpu.VMEM((QBLOCK, KV_WIDTH), f32), pltpu.VMEM((QBLOCK, KV_WIDTH), f32),
                        pltpu.VMEM(scores, f32), pltpu.VMEM(scores, f32), pltpu.VMEM(scores, bf16), pltpu.VMEM(scores, bf16)],
        args=(qn, kn, kn, vb, vb, o, do, lse, bias, sinks), ride=ride)


def _rel_bias_bwd(dbias, bucket):
    def body(d_ref, bk_ref, o_ref):
        b = bk_ref[...]
        for k in range(N_BUCKETS):
            mk = b == k
            for h in range(N_Q_HEADS):
                o_ref[k, h] = jnp.sum(jnp.where(mk, d_ref[h * QBLOCK:(h + 1) * QBLOCK, :], 0.0))

    return pl.pallas_call(body, name="rel_bias_bwd", out_shape=S((N_BUCKETS, N_Q_HEADS), f32), out_specs=SMEM)(dbias, bucket)


def _qk_norm_bwd(dq, dk, dv, proj, qg, kg, tm):
    T = dq.shape[0]
    scale = HEAD_DIM ** -0.5

    def pair_bwd(dy, x, gv):
        r = _pair_rstd(x, True)
        xn = x * r
        dxn = dy * gv
        dx = r * (dxn - xn * _pair_mean(dxn * xn, True))
        return dx, jnp.sum(dy * xn, axis=0, keepdims=True)

    def body(dq_ref, dk_ref, dv_ref, p_ref, qg_ref, kg_ref, out_ref, dqg_ref, dkg_ref):
        @pl.when(pl.program_id(0) == 0)
        def _():
            dqg_ref[...] = jnp.zeros_like(dqg_ref)
            dkg_ref[...] = jnp.zeros_like(dkg_ref)

        qgv, kgv = qg_ref[...], kg_ref[...]
        dqg = jnp.zeros((1, 2 * HEAD_DIM), f32)
        for pr in range(N_Q_HEADS // 2):
            dx, dg = pair_bwd(dq_ref[:, _pair_cols(pr)] * scale, p_ref[:, _pair_cols(pr)].astype(f32), qgv)
            out_ref[:, _pair_cols(pr)] = dx.astype(bf16)
            dqg = dqg + dg
        dkg = jnp.zeros((1, 2 * HEAD_DIM), f32)
        for pr in range(N_KV_HEADS // 2):
            ps = slice(Q_END + pr * 2 * HEAD_DIM, Q_END + (pr + 1) * 2 * HEAD_DIM)
            dx, dg = pair_bwd(dk_ref[:, _pair_cols(pr)], p_ref[:, ps].astype(f32), kgv)
            out_ref[:, ps] = dx.astype(bf16)
            dkg = dkg + dg
        out_ref[:, K_END:V_END] = dv_ref[...].astype(bf16)
        dqg_ref[...] += dqg
        dkg_ref[...] += dkg

    vec = pl.BlockSpec((1, 2 * HEAD_DIM), lambda i: (0, 0))
    return pl.pallas_call(
        body, name="qk_norm_bwd", grid=(T // tm,),
        in_specs=[pl.BlockSpec((tm, ATTN_WIDTH), lambda i: (i, 0)), pl.BlockSpec((tm, KV_WIDTH), lambda i: (i, 0)),
                  pl.BlockSpec((tm, KV_WIDTH), lambda i: (i, 0)), pl.BlockSpec((tm, V_END), lambda i: (i, 0)), vec, vec],
        out_specs=[pl.BlockSpec((tm, V_END), lambda i: (i, 0)), vec, vec],
        out_shape=[S((T, V_END), bf16), S((1, 2 * HEAD_DIM), f32), S((1, 2 * HEAD_DIM), f32)],
        compiler_params=_params("arbitrary"),
    )(dq, dk, dv, proj, qg, kg)


def _in_bwd(dqkv, dglu, dgates, w_in, x, g, dx1, tm, ride=None):
    T = x.shape[0]
    pieces = (dqkv, dglu, dgates)
    starts = [0, dqkv.shape[1], dqkv.shape[1] + dglu.shape[1]]

    def body(a0_ref, a1_ref, a2_ref, w_ref, x_ref, g_ref, d_ref, gx_ref, dg_ref):
        @pl.when(pl.program_id(0) == 0)
        def _():
            dg_ref[...] = jnp.zeros_like(dg_ref)

        du = jnp.zeros((tm, D_MODEL), f32)
        for a_ref, c0 in zip((a0_ref, a1_ref, a2_ref), starts):
            du = du + _nt(a_ref[...], w_ref[:, c0:c0 + a_ref.shape[1]])
        dx, dg = _rms_bwd(du, x_ref[...], g_ref[...])
        gx_ref[...] = d_ref[...] + dx
        dg_ref[...] += dg

    row = pl.BlockSpec((tm, D_MODEL), lambda i: (i, 0))
    return _call(
        body, name="in_bwd", grid=(T // tm,),
        in_specs=[pl.BlockSpec((tm, p.shape[1]), lambda i: (i, 0)) for p in pieces]
        + [_resident(w_in.shape), row, _resident((1, D_MODEL)), row],
        out_specs=[row, pl.BlockSpec((1, D_MODEL), lambda i: (0, 0))],
        out_shape=[S((T, D_MODEL), f32), S((1, D_MODEL), f32)],
        args=(dqkv, dglu, dgates, w_in, x, g, dx1), ride=ride)


def _adamw(name, parts, w, m, v, tr):
    _, R, C = w.shape
    bc1 = 1.0 - ADAM_B1 ** ADAM_STEP
    bc2 = 1.0 - ADAM_B2 ** ADAM_STEP

    def body(p_ref, w_ref, m_ref, v_ref, g_ref, d_ref, nm_ref, nv_ref):
        g = p_ref[0].astype(f32)
        for k in range(1, N_DEV):
            g = g + p_ref[k].astype(f32)
        nm = ADAM_B1 * m_ref[...] + (1.0 - ADAM_B1) * g
        nv = ADAM_B2 * v_ref[...] + (1.0 - ADAM_B2) * (g * g)
        g_ref[...] = g
        nm_ref[...] = nm
        nv_ref[...] = nv
        d_ref[...] = -ADAM_LR * ((nm / bc1) / (jnp.sqrt(nv / bc2) + ADAM_EPS) + ADAM_WD * w_ref[...])

    blk = pl.BlockSpec((None, tr, C), lambda i: (0, i, 0))
    return pl.pallas_call(
        body, name=name, grid=(R // tr,),
        in_specs=[pl.BlockSpec((N_DEV, tr, C), lambda i: (0, i, 0)), blk, blk, blk],
        out_specs=[blk, blk, blk, blk], out_shape=[S((1, R, C), f32)] * 4,
        compiler_params=_params("parallel"),
    )(parts, w, m, v)


def _tile(T, pref):
    return min(T, pref)


def _pad_rows(a, rows):
    return jnp.pad(a, ((0, rows - a.shape[0]), (0, 0)))


def kernel(x, norm_mix_g, w_in, q_norm_g, k_norm_g, attn_sinks, rel_bias, w_attn_o, w_dw, b_dw, conv_ln_g, conv_ln_b, w_conv_out, w_out, norm_mlp_g, w_ff1, w_ff2, loss_target, m_norm_mix_g, m_w_in, m_q_norm_g, m_k_norm_g, m_attn_sinks, m_rel_bias, m_w_attn_o, m_w_dw, m_b_dw, m_conv_ln_g, m_conv_ln_b, m_w_conv_out, m_w_out, m_norm_mlp_g, m_w_ff1, m_w_ff2, v_norm_mix_g, v_w_in, v_q_norm_g, v_k_norm_g, v_attn_sinks, v_rel_bias, v_w_attn_o, v_w_dw, v_b_dw, v_conv_ln_g, v_conv_ln_b, v_w_conv_out, v_w_out, v_norm_mlp_g, v_w_ff1, v_w_ff2):
    T = x.shape[1]
    xs = x[0]
    tgt = loss_target[0]
    in_shard = IN_WIDTH // N_DEV
    dw_rows = CONV_WIDTH + 1
    ch_shard = D_MODEL // N_DEV
    tb = _tile(T, 512)
    tt = _tile(T, 2048)
    bucket = jnp.asarray(_t5_bucket_table())

    g_in, g_dw = _exchange("gather_w_in", [w_in[0].astype(bf16), _pad_rows(w_dw[0], dw_rows)], gather=True, two_level=True)
    W_in = jnp.transpose(g_in, (1, 0, 2)).reshape(D_MODEL, IN_WIDTH)
    W_dw = jnp.transpose(g_dw, (1, 0, 2)).reshape(dw_rows, D_MODEL)[:CONV_WIDTH]

    mix_shards = _Gather([w_attn_o[0].astype(bf16), w_conv_out[0].astype(bf16), w_out[0].astype(bf16)])
    qg2, kg2 = jnp.tile(q_norm_g, (1, 2)), jnp.tile(k_norm_g, (1, 2))
    (proj, u, qn, kn, vb, h0), (g_ao, g_co, g_o) = _proj_fwd(xs, norm_mix_g, W_in, qg2, kg2, tb, ride=mix_shards)
    W_ao = g_ao.reshape(D_MODEL, D_MODEL)
    W_co = g_co.reshape(D_MODEL, D_MODEL)
    W_o = g_o.reshape(D_MODEL, D_MODEL)
    bias = _bias_table(rel_bias, bucket)
    (o, lse), (g_f1,) = _attn_fwd(qn, kn, vb, bias, attn_sinks, ride=_Gather([w_ff1[0].astype(bf16).T]))
    W_f1t = g_f1.reshape(D_FF, D_MODEL)
    (h1, h3), (g_f2,) = _conv_fwd(h0, W_dw, b_dw, conv_ln_g, conv_ln_b, tb, ride=_Gather([w_ff2[0].astype(bf16)]))
    x1, attn, conv, merged = _mix_fwd(xs, o, h3, proj, W_ao, W_co, W_o, tb)
    W_f2 = g_f2.reshape(D_FF, D_MODEL)
    a, u2, dy, dyb, loss_parts = _ffn_fwd(x1, norm_mlp_g, W_f1t, W_f2, tgt, tb)

    gw_f2 = _wgrad("wgrad_ff2", a, dyb, D_MODEL, D_MODEL, tt, relu2=True).reshape(N_DEV, FF_CHUNK, D_MODEL)
    (da, dx1, dx1b, d_norm_mlp_g), (l_f2,) = _ffn_bwd(dy, dyb, a, x1, norm_mlp_g, W_f1t, W_f2, tb,
                                                      ride=_Exchange([gw_f2], gather=False))
    gw_f1 = _wgrad("wgrad_ff1", u2, da, D_MODEL, 4 * FF_CHUNK, tt, slab=FF_CHUNK)
    gw_o = _wgrad("wgrad_out", merged, dx1b, D_MODEL, D_MODEL, tt).reshape(N_DEV, ch_shard, D_MODEL)
    (dattn, dconv, do, dh1, dgates, d_ln_g, d_ln_b, d_b_dw), (l_o,) = _mix_bwd(
        dx1b, proj, attn, conv, h1, conv_ln_g, conv_ln_b, W_ao, W_co, W_o, tb, ride=_Exchange([gw_o], gather=False))
    gw_ao = _wgrad("wgrad_attn_o", o, dattn, D_MODEL, D_MODEL, tt).reshape(N_DEV, ch_shard, D_MODEL)
    gw_co = _wgrad("wgrad_conv_out", h3, dconv, D_MODEL, D_MODEL, tt).reshape(N_DEV, ch_shard, D_MODEL)
    (dglu, d_w_dw), (l_f1, l_ao, l_co) = _conv_bwd(dh1, h0, proj, W_dw, tb,
                                                   ride=_Exchange([gw_f1, gw_ao, gw_co], gather=False))
    (dq, dk, dv, dbias, d_sinks), _ = _attn_bwd(qn, kn, vb, o, do, lse, bias, attn_sinks)
    d_sinks = d_sinks[:, :N_Q_HEADS]
    d_rel_bias = _rel_bias_bwd(dbias, bucket)
    dqkv, d_qg, d_kg = _qk_norm_bwd(dq, dk, dv, proj, qg2, kg2, tb)
    d_qg = d_qg[:, :HEAD_DIM] + d_qg[:, HEAD_DIM:]
    d_kg = d_kg[:, :HEAD_DIM] + d_kg[:, HEAD_DIM:]
    gw_in = jnp.concatenate([_wgrad("wgrad_in_qkv", u, dqkv, D_MODEL, V_END, tt),
                             _wgrad("wgrad_in_glu", u, dglu, D_MODEL, 2 * D_MODEL, tt),
                             _wgrad("wgrad_in_gates", u, dgates, D_MODEL, 2 * D_MODEL, tt)], axis=1)
    gw_in = jnp.transpose(gw_in.reshape(D_MODEL, N_DEV, in_shard), (1, 0, 2))
    gw_dw = jnp.transpose(d_w_dw.reshape(dw_rows, N_DEV, ch_shard), (1, 0, 2))
    (grad_x, d_norm_mix_g), (l_in, l_dw) = _in_bwd(dqkv, dglu, dgates, W_in, xs, norm_mix_g, dx1, tb,
                                                    ride=_Exchange([gw_in, gw_dw], gather=False))

    def row(vec):
        flat = vec.reshape(1, -1)
        return jnp.pad(flat, ((0, 0), (0, D_MODEL - flat.shape[1])))

    def pack_small(nm, qg, kg, sk, rb, bd, lg, lb, nl, extra=None):
        tail = jnp.concatenate([qg.reshape(1, -1), kg.reshape(1, -1), sk.reshape(1, -1), rb.reshape(1, -1)], axis=1)
        spare = jnp.zeros((1, D_MODEL), f32) if extra is None else row(extra)
        return jnp.concatenate([row(nm), row(bd), row(lg), row(lb), row(nl), row(tail), spare, jnp.zeros((1, D_MODEL), f32)], axis=0)

    def unpack_small(p):
        t = p[5]
        o0, o1, o2 = HEAD_DIM, 2 * HEAD_DIM, 2 * HEAD_DIM + N_Q_HEADS
        return dict(norm_mix_g=p[0:1], b_dw=p[1:2], conv_ln_g=p[2:3], conv_ln_b=p[3:4], norm_mlp_g=p[4:5],
                    q_norm_g=t[0:o0].reshape(1, HEAD_DIM), k_norm_g=t[o0:o1].reshape(1, HEAD_DIM),
                    attn_sinks=t[o1:o2].reshape(1, N_Q_HEADS),
                    rel_bias=t[o2:o2 + N_BUCKETS * N_Q_HEADS].reshape(N_BUCKETS, N_Q_HEADS))

    small_g = pack_small(d_norm_mix_g, d_qg, d_kg, d_sinks, d_rel_bias, d_b_dw, d_ln_g, d_ln_b, d_norm_mlp_g,
                         extra=jnp.sum(loss_parts[:, 0, 0]))
    (l_small,) = _exchange("gather_small_grads", [small_g], gather=True)


    res = {}
    res["w_in"] = _adamw("adamw_in", l_in, w_in, m_w_in, v_w_in, 256)
    res["w_attn_o"] = _adamw("adamw_attn_o", l_ao, w_attn_o, m_w_attn_o, v_w_attn_o, ch_shard)
    res["w_conv_out"] = _adamw("adamw_conv_out", l_co, w_conv_out, m_w_conv_out, v_w_conv_out, ch_shard)
    res["w_out"] = _adamw("adamw_out", l_o, w_out, m_w_out, v_w_out, ch_shard)
    res["w_ff1"] = _adamw("adamw_ff1", l_f1, w_ff1, m_w_ff1, v_w_ff1, 256)
    res["w_ff2"] = _adamw("adamw_ff2", l_f2, w_ff2, m_w_ff2, v_w_ff2, 256)
    pad_dw = lambda t: _pad_rows(t[0], dw_rows)[None]
    res["w_dw"] = [t[:, :CONV_WIDTH] for t in _adamw("adamw_dw", l_dw, pad_dw(w_dw), pad_dw(m_w_dw), pad_dw(v_w_dw), dw_rows)]
    small_w = pack_small(norm_mix_g, q_norm_g, k_norm_g, attn_sinks, rel_bias, b_dw, conv_ln_g, conv_ln_b, norm_mlp_g)
    small_m = pack_small(m_norm_mix_g, m_q_norm_g, m_k_norm_g, m_attn_sinks, m_rel_bias, m_b_dw, m_conv_ln_g, m_conv_ln_b, m_norm_mlp_g)
    small_v = pack_small(v_norm_mix_g, v_q_norm_g, v_k_norm_g, v_attn_sinks, v_rel_bias, v_b_dw, v_conv_ln_g, v_conv_ln_b, v_norm_mlp_g)
    small_out = _adamw("adamw_small", l_small, small_w[None], small_m[None], small_v[None], 8)
    small4 = [unpack_small(t[0]) for t in small_out]
    loss = small_out[0][0, 6, 0]

    order = ["norm_mix_g", "w_in", "q_norm_g", "k_norm_g", "attn_sinks", "rel_bias", "w_attn_o", "w_dw", "b_dw",
             "conv_ln_g", "conv_ln_b", "w_conv_out", "w_out", "norm_mlp_g", "w_ff1", "w_ff2"]
    stacked = {"w_in", "w_attn_o", "w_dw", "w_conv_out", "w_out", "w_ff1", "w_ff2"}
    outs = [loss, grad_x[None]]
    for k in range(4):
        for nme in order:
            if nme in stacked:
                outs.append(res[nme][k])
            else:
                outs.append(small4[k][nme])
    return tuple(outs)
```

```python
import functools

import numpy as np
import jax
import jax.numpy as jnp
from jax import lax
from jax.experimental import pallas as pl
from jax.experimental.pallas import tpu as pltpu

f32 = jnp.float32
bf16 = jnp.bfloat16
S = jax.ShapeDtypeStruct

N_DEV = 8
D_MODEL = 1024
HEAD_DIM = 64
N_Q_HEADS = 16
N_KV_HEADS = 4
GROUP = N_Q_HEADS // N_KV_HEADS
ATTN_WIDTH = N_Q_HEADS * HEAD_DIM
KV_WIDTH = N_KV_HEADS * HEAD_DIM
QBLOCK = 128
CONV_WIDTH = 31
CONV_HALO = 32
CONV_UNIT = 64
D_FF = 4 * D_MODEL
N_BUCKETS = 32
MAX_DISTANCE = 128
EPS = 1e-6
NEG = -1e30
Q_END = ATTN_WIDTH
K_END = Q_END + KV_WIDTH
V_END = K_END + KV_WIDTH
GLU_END = V_END + 2 * D_MODEL
IN_WIDTH = GLU_END + 2 * D_MODEL
COL = 512
FF_CHUNK = D_FF // N_DEV

ADAM_LR = 0.001
ADAM_B1 = 0.9
ADAM_B2 = 0.999
ADAM_EPS = 1e-08
ADAM_WD = 0.01
ADAM_STEP = 10

VMEM_LIMIT = 56 * 1024 * 1024

MESH_ID = pl.DeviceIdType.MESH
ANY = pl.BlockSpec(memory_space=pl.ANY)
SMEM = pl.BlockSpec(memory_space=pltpu.SMEM)


def _params(*sem):
    return pltpu.CompilerParams(dimension_semantics=sem, vmem_limit_bytes=VMEM_LIMIT)


def _nt(a, b):
    return lax.dot_general(a, b, (((1,), (1,)), ((), ())), preferred_element_type=f32)


def _tn(a, b):
    return lax.dot_general(a, b, (((0,), (0,)), ((), ())), preferred_element_type=f32)


def _sigmoid(z):
    return 1.0 / (1.0 + jnp.exp(-z))


def _t5_bucket_table():
    qi = np.arange(QBLOCK, dtype=np.int32)[:, None]
    kj = np.arange(2 * QBLOCK, dtype=np.int32)[None, :]
    dist = qi + QBLOCK - kj
    n = np.maximum(dist, 0)
    max_exact = N_BUCKETS // 2
    nf = np.maximum(n, 1).astype(np.float32)
    large = max_exact + (np.log(nf / np.float32(max_exact)) / np.float32(np.log(MAX_DISTANCE / max_exact))
                         * np.float32(N_BUCKETS - max_exact)).astype(np.int32)
    large = np.minimum(large, N_BUCKETS - 1)
    bucket = np.where(n < max_exact, n, large)
    valid = (dist >= 0) & (dist < QBLOCK)
    return np.where(valid, bucket, -1).astype(np.int32)


def _peer(d):
    x, y, c = lax.axis_index("x"), lax.axis_index("y"), lax.axis_index("c")
    dx, dy, dc = (d >> 2) & 1, (d >> 1) & 1, d & 1
    px, py, pc = x ^ dx, y ^ dy, c ^ dc
    return (px, py, pc), 4 * px + 2 * py + pc


class _Exchange:
    def __init__(self, arrays, gather):
        self.arrays, self.gather, self.n = list(arrays), gather, len(arrays)
        self.out_shape = [S(((N_DEV,) + a.shape) if gather else a.shape, a.dtype) for a in self.arrays]
        self.scratch = [pltpu.SemaphoreType.DMA((self.n, N_DEV - 1)), pltpu.SemaphoreType.DMA((self.n, N_DEV - 1)),
                        pltpu.SemaphoreType.DMA((self.n,))]

    def _copies(self, ins, outs, sems):
        send_sems, recv_sems, local_sems = sems
        _, me = _peer(0)
        local, sends, recvs = [], [], []
        for k in range(self.n):
            src = ins[k] if self.gather else ins[k].at[me]
            local.append(pltpu.make_async_copy(src, outs[k].at[me], local_sems.at[k]))
        for d in range(1, N_DEV):
            peer, pidx = _peer(d)
            for k in range(self.n):
                src = ins[k] if self.gather else ins[k].at[pidx]
                common = dict(src_ref=src, send_sem=send_sems.at[k, d - 1], recv_sem=recv_sems.at[k, d - 1],
                              device_id=peer, device_id_type=MESH_ID)
                sends.append(pltpu.make_async_remote_copy(dst_ref=outs[k].at[me], **common))
                recvs.append(pltpu.make_async_remote_copy(dst_ref=outs[k].at[pidx], **common))
        return local, sends, recvs

    def start(self, ins, outs, sems):
        local, sends, _ = self._copies(ins, outs, sems)
        for cp in local + sends:
            cp.start()

    def wait(self, ins, outs, sems):
        local, sends, recvs = self._copies(ins, outs, sems)
        for cp in recvs:
            cp.wait_recv()
        for cp in sends:
            cp.wait_send()
        for cp in local:
            cp.wait()


class _Gather:
    CHIPS = (4, 2, 6)
    SLOTS = 1 + 2 * len(CHIPS)

    def __init__(self, arrays):
        self.arrays, self.n = list(arrays), len(arrays)
        self.out_shape = [S((N_DEV,) + a.shape, a.dtype) for a in self.arrays]
        self.scratch = [pltpu.SemaphoreType.DMA((self.n, self.SLOTS)), pltpu.SemaphoreType.DMA((self.n, self.SLOTS)),
                        pltpu.SemaphoreType.DMA((self.n,))]

    @staticmethod
    def _copy(outs, sems, k, slot, src, block, to):
        return pltpu.make_async_remote_copy(src_ref=src, dst_ref=outs[k].at[block], send_sem=sems[0].at[k, slot],
                                            recv_sem=sems[1].at[k, slot], device_id=to, device_id_type=MESH_ID)

    def _local(self, ins, outs, sems):
        _, me = _peer(0)
        return [pltpu.make_async_copy(ins[k], outs[k].at[me], sems[2].at[k]) for k in range(self.n)]

    def start(self, ins, outs, sems):
        _, me = _peer(0)
        sibling, _ = _peer(1)
        for cp in self._local(ins, outs, sems):
            cp.start()
        for k in range(self.n):
            self._copy(outs, sems, k, 0, ins[k], me, sibling).start()
            for j, d in enumerate(self.CHIPS):
                self._copy(outs, sems, k, 1 + j, ins[k], me, _peer(d)[0]).start()

    def mid(self, ins, outs, sems):
        sibling, _ = _peer(1)
        for j, d in enumerate(self.CHIPS):
            chip, block = _peer(d)
            for k in range(self.n):
                self._copy(outs, sems, k, 1 + j, ins[k], block, chip).wait_recv()
                self._copy(outs, sems, k, 4 + j, outs[k].at[block], block, sibling).start()

    def wait(self, ins, outs, sems):
        _, me = _peer(0)
        sibling, sib_block = _peer(1)
        for k in range(self.n):
            self._copy(outs, sems, k, 0, ins[k], sib_block, sibling).wait_recv()
            for j, d in enumerate(self.CHIPS):
                self._copy(outs, sems, k, 4 + j, ins[k], _peer(d ^ 1)[1], sibling).wait_recv()
        for k in range(self.n):
            self._copy(outs, sems, k, 0, ins[k], me, sibling).wait_send()
            for j, d in enumerate(self.CHIPS):
                chip, block = _peer(d)
                self._copy(outs, sems, k, 1 + j, ins[k], me, chip).wait_send()
                self._copy(outs, sems, k, 4 + j, outs[k].at[block], block, sibling).wait_send()
        for cp in self._local(ins, outs, sems):
            cp.wait()


def _exchange(name, arrays, gather, two_level=False):
    ex = _Gather(arrays) if two_level else _Exchange(arrays, gather)
    n = ex.n

    def body(*refs):
        ins, outs, sems = refs[:n], refs[n:2 * n], refs[2 * n:]
        ex.start(ins, outs, sems)
        if two_level:
            ex.mid(ins, outs, sems)
        ex.wait(ins, outs, sems)

    return pl.pallas_call(body, name=name, out_shape=ex.out_shape, in_specs=[ANY] * n, out_specs=[ANY] * n,
                          scratch_shapes=ex.scratch)(*arrays)


def _call(body, *, name, grid, in_specs, out_specs, out_shape, args, scratch_shapes=(), ride=None):
    n_in, n_out, n_sc = len(in_specs), len(out_specs), len(scratch_shapes)
    sem = ("arbitrary",) * len(grid)
    if ride is None:
        res = pl.pallas_call(body, name=name, grid=grid, in_specs=list(in_specs), out_specs=list(out_specs),
                             out_shape=list(out_shape), scratch_shapes=list(scratch_shapes), compiler_params=_params(*sem))(*args)
        return list(res), []
    nx = ride.n

    def riding(*refs):
        ins, xin = refs[:n_in], refs[n_in:n_in + nx]
        outs, xout = refs[n_in + nx:n_in + nx + n_out], refs[n_in + nx + n_out:n_in + 2 * nx + n_out]
        rest = refs[n_in + 2 * nx + n_out:]
        scratch, sems = rest[:n_sc], rest[n_sc:]
        ids = [pl.program_id(ax) for ax in range(len(grid))]
        first = functools.reduce(jnp.logical_and, [i == 0 for i in ids])
        last = functools.reduce(jnp.logical_and, [i == g - 1 for i, g in zip(ids, grid)])

        @pl.when(first)
        def _():
            ride.start(xin, xout, sems)

        if hasattr(ride, "mid"):
            halfway = functools.reduce(jnp.logical_and, [ids[0] == grid[0] // 2] + [i == 0 for i in ids[1:]])

            @pl.when(halfway)
            def _():
                ride.mid(xin, xout, sems)

        body(*ins, *outs, *scratch)

        @pl.when(last)
        def _():
            ride.wait(xin, xout, sems)

    res = pl.pallas_call(
        riding, name=name, grid=grid, in_specs=list(in_specs) + [ANY] * nx, out_specs=list(out_specs) + [ANY] * nx,
        out_shape=list(out_shape) + ride.out_shape, scratch_shapes=list(scratch_shapes) + ride.scratch,
        compiler_params=_params(*sem))(*args, *ride.arrays)
    return list(res[:n_out]), list(res[n_out:])


def _resident(shape):
    return pl.BlockSpec(shape, lambda *_: (0,) * len(shape), pipeline_mode=pl.Buffered(1))


def _proj_fwd(x, g, w, qg, kg, tm, ride=None):
    T, K = x.shape
    N = w.shape[1]
    per = COL // (2 * HEAD_DIM)
    assert Q_END % COL == 0 and V_END == Q_END + COL and (GLU_END - V_END) == 4 * COL and KV_WIDTH == COL // 2

    def body(x_ref, g_ref, w_ref, qg_ref, kg_ref, o_ref, u_ref, qn_ref, kn_ref, vb_ref, h0_ref):
        xv = x_ref[...]
        r = lax.rsqrt(jnp.mean(xv * xv, axis=-1, keepdims=True) + EPS)
        u = (xv * r * g_ref[...]).astype(bf16)
        u_ref[...] = u

        def block(c):
            cs = slice(c * COL, (c + 1) * COL)
            pc = jnp.dot(u, w_ref[:, cs], preferred_element_type=f32)
            o_ref[:, cs] = pc.astype(bf16)
            return pc

        qgv = qg_ref[...] * (HEAD_DIM ** -0.5)
        for c in range(Q_END // COL):
            pc = block(c)
            for t in range(per):
                xq = pc[:, _pair_cols(t)]
                qn_ref[:, _pair_cols(c * per + t)] = (xq * _pair_rstd(xq, False) * qgv).astype(bf16)
        pc = block(Q_END // COL)
        for t in range(KV_WIDTH // (2 * HEAD_DIM)):
            xk = pc[:, _pair_cols(t)]
            kn_ref[:, _pair_cols(t)] = (xk * _pair_rstd(xk, False) * kg_ref[...]).astype(bf16)
        vb_ref[...] = pc[:, KV_WIDTH:].astype(bf16)
        a0 = V_END // COL
        for half in range(2):
            gate = block(a0 + 2 + half)
            h0_ref[:, half * COL:(half + 1) * COL] = block(a0 + half) * _sigmoid(gate)
        for c in range(GLU_END // COL, N // COL):
            block(c)

    row = lambda width: pl.BlockSpec((tm, width), lambda i: (i, 0))
    return _call(
        body, name="proj_fwd", grid=(T // tm,),
        in_specs=[row(K), _resident((1, K)), _resident((K, N)), _resident((1, 2 * HEAD_DIM)), _resident((1, 2 * HEAD_DIM))],
        out_specs=[row(N), row(K), row(ATTN_WIDTH), row(KV_WIDTH), row(KV_WIDTH), row(D_MODEL)],
        out_shape=[S((T, N), bf16), S((T, K), bf16), S((T, ATTN_WIDTH), bf16), S((T, KV_WIDTH), bf16), S((T, KV_WIDTH), bf16),
                   S((T, D_MODEL), f32)],
        args=(x, g, w, qg, kg), ride=ride)


def _bias_table(rel_bias, bucket):
    def body(rb_ref, bk_ref, o_ref):
        b = bk_ref[...]
        absent = lax.broadcasted_iota(jnp.int32, (QBLOCK, 2 * QBLOCK), 1) < QBLOCK
        for h in range(N_Q_HEADS):
            acc = jnp.full((QBLOCK, 2 * QBLOCK), NEG, f32)
            for k in range(N_BUCKETS):
                acc = jnp.where(b == k, rb_ref[k, h], acc)
            o_ref[0, h * QBLOCK:(h + 1) * QBLOCK, :] = acc
            o_ref[1, h * QBLOCK:(h + 1) * QBLOCK, :] = jnp.where(absent, NEG, acc)

    return pl.pallas_call(
        body, name="bias_table", out_shape=S((2, N_Q_HEADS * QBLOCK, 2 * QBLOCK), f32),
        in_specs=[SMEM, pl.BlockSpec(memory_space=pltpu.VMEM)],
    )(rel_bias, bucket)


def _bias_spec():
    return pl.BlockSpec((None, N_Q_HEADS * QBLOCK, 2 * QBLOCK), lambda n: (jnp.where(n == 0, 1, 0), 0, 0))


def _swap_halves(t):
    return jnp.concatenate([t[:, HEAD_DIM:], t[:, :HEAD_DIM]], axis=1)


def _low_lanes():
    return lax.broadcasted_iota(jnp.int32, (1, 2 * HEAD_DIM), 1) < HEAD_DIM


def _one_head(pair, side):
    zero = jnp.zeros((), pair.dtype)
    return jnp.where(_low_lanes(), pair, zero) if side == 0 else jnp.where(_low_lanes(), zero, pair)


def _pair_mean(t, on_mxu):
    if not on_mxu:
        m_lo = jnp.sum(_one_head(t, 0), axis=-1, keepdims=True) * (1.0 / HEAD_DIM)
        m_hi = jnp.sum(_one_head(t, 1), axis=-1, keepdims=True) * (1.0 / HEAD_DIM)
        return jnp.where(_low_lanes(), m_lo, m_hi)
    width = 2 * HEAD_DIM
    same_head = ((lax.broadcasted_iota(jnp.int32, (width, width), 0) < HEAD_DIM)
                 == (lax.broadcasted_iota(jnp.int32, (width, width), 1) < HEAD_DIM))
    e = jnp.where(same_head, 1.0 / HEAD_DIM, 0.0).astype(bf16)
    hi = t.astype(bf16)
    lo = (t - hi.astype(f32)).astype(bf16)
    return jnp.dot(hi, e, preferred_element_type=f32) + jnp.dot(lo, e, preferred_element_type=f32)


def _pair_rstd(x, on_mxu):
    return lax.rsqrt(_pair_mean(x * x, on_mxu) + EPS)


def _kv_placements(band):
    out = {}
    for m in range(N_KV_HEADS // 2):
        pair = band[:, m * 2 * HEAD_DIM:(m + 1) * 2 * HEAD_DIM]
        swapped = _swap_halves(pair)
        for hh in range(2):
            out[2 * m + hh, 0] = _one_head(pair if hh == 0 else swapped, 0)
            out[2 * m + hh, 1] = _one_head(swapped if hh == 0 else pair, 1)
    return out


def _head_rows(hq):
    return slice(hq * QBLOCK, (hq + 1) * QBLOCK)


def _pair_cols(pr):
    return slice(pr * 2 * HEAD_DIM, (pr + 1) * 2 * HEAD_DIM)


def _attn_fwd(qn, kn, vb, bias, sinks, ride=None):
    T = qn.shape[0]
    nb = T // QBLOCK

    def body(q_ref, kc_ref, kp_ref, vc_ref, vp_ref, b_ref, s_ref, o_ref, lse_ref, s_scr, p_scr):
        lane = lax.broadcasted_iota(jnp.int32, (QBLOCK, 2 * HEAD_DIM), 1)
        kx = _kv_placements(jnp.concatenate([kp_ref[...], kc_ref[...]], axis=0))
        vx = _kv_placements(jnp.concatenate([vp_ref[...], vc_ref[...]], axis=0))
        for hq in range(N_Q_HEADS):
            qm = _one_head(q_ref[:, _pair_cols(hq // 2)], hq % 2)
            s_scr[_head_rows(hq), :] = _nt(qm, kx[hq // GROUP, hq % 2]) + b_ref[_head_rows(hq), :]
        lse_tile = jnp.zeros((QBLOCK, 2 * HEAD_DIM), f32)
        for hq in range(N_Q_HEADS):
            s = s_scr[_head_rows(hq), :]
            sink = s_ref[0, hq]
            m = jnp.maximum(jnp.max(s, axis=-1, keepdims=True), sink)
            p = jnp.exp(s - m)
            l = jnp.sum(p, axis=-1, keepdims=True) + jnp.exp(sink - m)
            p_scr[_head_rows(hq), :] = (p * (1.0 / l)).astype(bf16)
            lse_tile = jnp.where(lane == hq, m + jnp.log(l), lse_tile)
        lse_ref[...] = lse_tile
        for pr in range(N_Q_HEADS // 2):
            h = 2 * pr // GROUP
            o_pair = (jnp.dot(p_scr[_head_rows(2 * pr), :], vx[h, 0], preferred_element_type=f32)
                      + jnp.dot(p_scr[_head_rows(2 * pr + 1), :], vx[h, 1], preferred_element_type=f32))
            o_ref[:, _pair_cols(pr)] = o_pair.astype(bf16)

    cur = lambda n: (n, 0)
    prev = lambda n: (jnp.maximum(n - 1, 0), 0)
    return _call(
        body, name="attn_fwd", grid=(nb,),
        in_specs=[pl.BlockSpec((QBLOCK, ATTN_WIDTH), cur), pl.BlockSpec((QBLOCK, KV_WIDTH), cur),
                  pl.BlockSpec((QBLOCK, KV_WIDTH), prev), pl.BlockSpec((QBLOCK, KV_WIDTH), cur),
                  pl.BlockSpec((QBLOCK, KV_WIDTH), prev), _bias_spec(), SMEM],
        out_specs=[pl.BlockSpec((QBLOCK, ATTN_WIDTH), cur), pl.BlockSpec((QBLOCK, 2 * HEAD_DIM), cur)],
        out_shape=[S((T, ATTN_WIDTH), bf16), S((T, 2 * HEAD_DIM), f32)],
        scratch_shapes=[pltpu.VMEM((N_Q_HEADS * QBLOCK, 2 * QBLOCK), f32), pltpu.VMEM((N_Q_HEADS * QBLOCK, 2 * QBLOCK), bf16)],
        args=(qn, kn, kn, vb, vb, bias, sinks), ride=ride)


def _layer_norm_stats(h1):
    mu = jnp.mean(h1, axis=-1, keepdims=True)
    xc = h1 - mu
    rstd = lax.rsqrt(jnp.mean(xc * xc, axis=-1, keepdims=True) + EPS)
    return xc * rstd, rstd


def _advanced_windows(win):
    rows = win.shape[0]
    for r in range(8):
        yield r, (win if r == 0 else pltpu.roll(win, rows - r, 0))


def _tap_offsets(r, rows):
    for q in range((rows - CONV_UNIT) // 8 + 1):
        if r == 0 or 8 * q + r + CONV_UNIT <= rows:
            yield q, 8 * q + r


def _conv_fwd(h0, w_dw, b_dw, ln_g, ln_b, tm, ride=None):
    T = h0.shape[0]
    per = tm // CONV_HALO
    lead = CONV_HALO - (CONV_WIDTH - 1)

    def body(hc_ref, hp_ref, w_ref, b_ref, g_ref, bb_ref, h1_ref, h3_ref, cat):
        i = pl.program_id(0)
        cat[0:CONV_HALO, :] = jnp.where(i == 0, 0.0, hp_ref[...])
        cat[CONV_HALO:, :] = hc_ref[...]

        def unit_rows(c, carry):
            r0 = pl.multiple_of(c * CONV_UNIT, CONV_UNIT)
            for j in range(D_MODEL // 128):
                ls = slice(j * 128, (j + 1) * 128)
                win = cat[pl.ds(r0, CONV_UNIT + CONV_HALO), ls]
                acc = jnp.zeros((CONV_UNIT, 128), f32) + b_ref[:, ls]
                for r, adv in _advanced_windows(win):
                    for q, off in _tap_offsets(r, CONV_UNIT + CONV_HALO):
                        k = off - lead
                        if 0 <= k < CONV_WIDTH:
                            acc = acc + adv[8 * q:8 * q + CONV_UNIT] * w_ref[k:k + 1, ls]
                h1_ref[pl.ds(r0, CONV_UNIT), ls] = acc
            return carry

        lax.fori_loop(0, tm // CONV_UNIT, unit_rows, 0)
        acc = h1_ref[...]
        xhat, _ = _layer_norm_stats(acc)
        h2 = xhat * g_ref[...] + bb_ref[...]
        h3_ref[...] = (h2 * _sigmoid(h2)).astype(bf16)

    vec = pl.BlockSpec((1, D_MODEL), lambda i: (0, 0))
    return _call(
        body, name="conv_fwd", grid=(T // tm,),
        in_specs=[pl.BlockSpec((tm, D_MODEL), lambda i: (i, 0)),
                  pl.BlockSpec((CONV_HALO, D_MODEL), lambda i: (jnp.maximum(i * per - 1, 0), 0)),
                  pl.BlockSpec((CONV_WIDTH, D_MODEL), lambda i: (0, 0)), vec, vec, vec],
        out_specs=[pl.BlockSpec((tm, D_MODEL), lambda i: (i, 0)), pl.BlockSpec((tm, D_MODEL), lambda i: (i, 0))],
        out_shape=[S((T, D_MODEL), f32), S((T, D_MODEL), bf16)],
        scratch_shapes=[pltpu.VMEM((tm + CONV_HALO, D_MODEL), f32)],
        args=(h0, h0, w_dw, b_dw, ln_g, ln_b), ride=ride)


def _mix_fwd(x, o, h3, proj, w_ao, w_co, w_o, tm):
    T = x.shape[0]
    row = pl.BlockSpec((tm, D_MODEL), lambda i: (i, 0))
    wsp = _resident((D_MODEL, D_MODEL))
    g0 = GLU_END // COL

    def gate_spec(off):
        return pl.BlockSpec((tm, COL), lambda i: (i, g0 + off))

    def body(x_ref, o_ref, h3_ref, ga0, ga1, gc0, gc1, wa_ref, wc_ref, wo_ref, x1_ref, at_ref, cv_ref, mg_ref):
        attn = jnp.dot(o_ref[...], wa_ref[...], preferred_element_type=f32)
        conv = jnp.dot(h3_ref[...], wc_ref[...], preferred_element_type=f32)
        ga = jnp.concatenate([ga0[...], ga1[...]], axis=-1).astype(f32)
        gc = jnp.concatenate([gc0[...], gc1[...]], axis=-1).astype(f32)
        merged = (_sigmoid(ga) * attn + _sigmoid(gc) * conv).astype(bf16)
        at_ref[...] = attn.astype(bf16)
        cv_ref[...] = conv.astype(bf16)
        mg_ref[...] = merged
        x1_ref[...] = x_ref[...] + jnp.dot(merged, wo_ref[...], preferred_element_type=f32)

    return pl.pallas_call(
        body, name="mix_fwd", grid=(T // tm,),
        in_specs=[row, row, row, gate_spec(0), gate_spec(1), gate_spec(2), gate_spec(3), wsp, wsp, wsp],
        out_specs=[row, row, row, row],
        out_shape=[S((T, D_MODEL), f32), S((T, D_MODEL), bf16), S((T, D_MODEL), bf16), S((T, D_MODEL), bf16)],
        compiler_params=_params("parallel"),
    )(x, o, h3, proj, proj, proj, proj, w_ao, w_co, w_o)


def _ffn_fwd(x1, g, w1, w2, target, tm):
    T = x1.shape[0]
    nj = w1.shape[0] // FF_CHUNK

    def body(x_ref, g_ref, w1_ref, w2_ref, t_ref, a_ref, u_ref, dy_ref, dyb_ref, ls_ref, hm):
        xv = x_ref[...]
        r = lax.rsqrt(jnp.mean(xv * xv, axis=-1, keepdims=True) + EPS)
        u = (xv * r * g_ref[...]).astype(bf16)
        u_ref[...] = u
        for j in range(nj):
            js = slice(j * FF_CHUNK, (j + 1) * FF_CHUNK)
            a = _nt(u, w1_ref[js, :])
            a_ref[:, js] = a.astype(bf16)
            hm[:, js] = jnp.square(jnp.maximum(a, 0.0)).astype(bf16)
        err = xv + jnp.dot(hm[...], w2_ref[...], preferred_element_type=f32) - t_ref[...]
        dy = err * (1.0 / D_MODEL)
        dy_ref[...] = dy
        dyb_ref[...] = dy.astype(bf16)
        ls_ref[...] = jnp.zeros((8, 128), f32) + jnp.sum(err * err) * (0.5 / D_MODEL)

    row = pl.BlockSpec((tm, D_MODEL), lambda i: (i, 0))
    wide = pl.BlockSpec((tm, D_FF), lambda i: (i, 0))
    return pl.pallas_call(
        body, name="ffn_fwd", grid=(T // tm,),
        in_specs=[row, _resident((1, D_MODEL)), _resident(w1.shape), _resident(w2.shape), row],
        out_specs=[wide, row, row, row, pl.BlockSpec((None, 8, 128), lambda i: (i, 0, 0))],
        out_shape=[S((T, D_FF), bf16), S((T, D_MODEL), bf16), S((T, D_MODEL), f32), S((T, D_MODEL), bf16),
                   S((T // tm, 8, 128), f32)],
        scratch_shapes=[pltpu.VMEM((tm, D_FF), bf16)],
        compiler_params=_params("parallel"),
    )(x1, g, w1, w2, target)


def _rms_bwd(du, xv, gv):
    r = lax.rsqrt(jnp.mean(xv * xv, axis=-1, keepdims=True) + EPS)
    xn = xv * r
    dg = jnp.sum(du * xn, axis=0, keepdims=True)
    dxn = du * gv
    dx = r * (dxn - xn * jnp.mean(dxn * xn, axis=-1, keepdims=True))
    return dx, dg


def _ffn_bwd(dy, dyb, a, x1, g, w1, w2, tm, ride=None):
    T = dy.shape[0]
    nj = w1.shape[0] // FF_CHUNK

    def body(dy_ref, dyb_ref, a_ref, x_ref, g_ref, w1_ref, w2_ref, da_ref, dx_ref, dxb_ref, dg_ref):
        @pl.when(pl.program_id(0) == 0)
        def _():
            dg_ref[...] = jnp.zeros_like(dg_ref)

        dyb_v = dyb_ref[...]
        for j in range(nj):
            js = slice(j * FF_CHUNK, (j + 1) * FF_CHUNK)
            dh = _nt(dyb_v, w2_ref[js, :])
            da_ref[:, js] = (dh * (2.0 * jnp.maximum(a_ref[:, js].astype(f32), 0.0))).astype(bf16)
        du = jnp.dot(da_ref[...], w1_ref[...], preferred_element_type=f32)
        dx, dg = _rms_bwd(du, x_ref[...], g_ref[...])
        dx1 = dy_ref[...] + dx
        dx_ref[...] = dx1
        dxb_ref[...] = dx1.astype(bf16)
        dg_ref[...] += dg

    row = pl.BlockSpec((tm, D_MODEL), lambda i: (i, 0))
    wide = pl.BlockSpec((tm, D_FF), lambda i: (i, 0))
    vec = pl.BlockSpec((1, D_MODEL), lambda i: (0, 0))
    return _call(
        body, name="ffn_bwd", grid=(T // tm,),
        in_specs=[row, row, wide, row, _resident((1, D_MODEL)), _resident(w1.shape), _resident(w2.shape)],
        out_specs=[wide, row, row, vec],
        out_shape=[S((T, D_FF), bf16), S((T, D_MODEL), f32), S((T, D_MODEL), bf16), S((1, D_MODEL), f32)],
        args=(dy, dyb, a, x1, g, w1, w2), ride=ride)


def _wgrad(name, a, b, tk, tn, tt, relu2=False, slab=None, out_dtype=bf16):
    T, Ka = a.shape
    Nb = b.shape[1]
    nt = T // tt

    def body(a_ref, b_ref, o_ref, acc):
        t = pl.program_id(2)
        av = a_ref[...]
        if relu2:
            av = jnp.square(jnp.maximum(av.astype(f32), 0.0))
        prod = _tn(av.astype(bf16), b_ref[...].astype(bf16))

        @pl.when(t == 0)
        def _():
            acc[...] = prod

        @pl.when(t > 0)
        def _():
            acc[...] += prod

        @pl.when(t == nt - 1)
        def _():
            if slab is None:
                o_ref[...] = acc[...].astype(out_dtype)
            else:
                for s in range(tn // slab):
                    o_ref[s] = acc[:, s * slab:(s + 1) * slab].astype(out_dtype)

    if slab is not None:
        out_shape = S((Nb // slab, Ka, slab), out_dtype)
        out_spec = pl.BlockSpec((tn // slab, tk, slab), lambda i, j, t: (j, i, 0))
    else:
        out_shape = S((Ka, Nb), out_dtype)
        out_spec = pl.BlockSpec((tk, tn), lambda i, j, t: (i, j))
    return pl.pallas_call(
        body, name=name, grid=(Ka // tk, Nb // tn, nt),
        in_specs=[pl.BlockSpec((tt, tk), lambda i, j, t: (t, i)), pl.BlockSpec((tt, tn), lambda i, j, t: (t, j))],
        out_specs=out_spec, out_shape=out_shape, scratch_shapes=[pltpu.VMEM((tk, tn), f32)],
        compiler_params=_params("parallel", "parallel", "arbitrary"),
    )(a, b)


def _mix_bwd(dx1, proj, attn, conv, h1, ln_g, ln_b, w_ao, w_co, w_o, tm, ride=None):
    T = dx1.shape[0]
    g0 = GLU_END // COL

    def gate_spec(off):
        return pl.BlockSpec((tm, COL), lambda i: (i, g0 + off))

    def body(dx_ref, ga0, ga1, gc0, gc1, at_ref, cv_ref, h_ref, g_ref, b_ref, wa_ref, wc_ref, wo_ref,
             da_ref, dc_ref, do_ref, dh1_ref, dg_ref, dlg_ref, dlb_ref, dbd_ref):
        @pl.when(pl.program_id(0) == 0)
        def _():
            dlg_ref[...] = jnp.zeros_like(dlg_ref)
            dlb_ref[...] = jnp.zeros_like(dlb_ref)
            dbd_ref[...] = jnp.zeros_like(dbd_ref)

        dm = _nt(dx_ref[...].astype(bf16), wo_ref[...])
        sa = _sigmoid(jnp.concatenate([ga0[...], ga1[...]], axis=-1).astype(f32))
        sc = _sigmoid(jnp.concatenate([gc0[...], gc1[...]], axis=-1).astype(f32))
        dattn = (dm * sa).astype(bf16)
        dconv = (dm * sc).astype(bf16)
        da_ref[...] = dattn
        dc_ref[...] = dconv
        dg_ref[:, 0:D_MODEL] = (dm * at_ref[...].astype(f32) * sa * (1.0 - sa)).astype(bf16)
        dg_ref[:, D_MODEL:2 * D_MODEL] = (dm * cv_ref[...].astype(f32) * sc * (1.0 - sc)).astype(bf16)
        do_ref[...] = _nt(dattn, wa_ref[...]).astype(bf16)
        dh3 = _nt(dconv, wc_ref[...])
        xhat, rstd = _layer_norm_stats(h_ref[...])
        h2 = xhat * g_ref[...] + b_ref[...]
        sg = _sigmoid(h2)
        dh2 = dh3 * (sg * (1.0 + h2 * (1.0 - sg)))
        dlg_ref[...] += jnp.sum(dh2 * xhat, axis=0, keepdims=True)
        dlb_ref[...] += jnp.sum(dh2, axis=0, keepdims=True)
        dxh = dh2 * g_ref[...]
        dh1 = rstd * (dxh - jnp.mean(dxh, axis=-1, keepdims=True) - xhat * jnp.mean(dxh * xhat, axis=-1, keepdims=True))
        dh1_ref[...] = dh1
        dbd_ref[...] += jnp.sum(dh1, axis=0, keepdims=True)

    row = pl.BlockSpec((tm, D_MODEL), lambda i: (i, 0))
    vec = pl.BlockSpec((1, D_MODEL), lambda i: (0, 0))
    par = _resident((1, D_MODEL))
    wsp = _resident((D_MODEL, D_MODEL))
    return _call(
        body, name="mix_bwd", grid=(T // tm,),
        in_specs=[row, gate_spec(0), gate_spec(1), gate_spec(2), gate_spec(3), row, row, row, par, par, wsp, wsp, wsp],
        out_specs=[row, row, row, row, pl.BlockSpec((tm, 2 * D_MODEL), lambda i: (i, 0)), vec, vec, vec],
        out_shape=[S((T, D_MODEL), bf16), S((T, D_MODEL), bf16), S((T, D_MODEL), bf16), S((T, D_MODEL), f32),
                   S((T, 2 * D_MODEL), bf16), S((1, D_MODEL), f32), S((1, D_MODEL), f32), S((1, D_MODEL), f32)],
        args=(dx1, proj, proj, proj, proj, attn, conv, h1, ln_g, ln_b, w_ao, w_co, w_o), ride=ride)


def _conv_bwd(dh1, h0, proj, w_dw, tm, ride=None):
    T = dh1.shape[0]
    per = tm // CONV_HALO
    nh = T // CONV_HALO
    nt = T // tm
    a0 = V_END // COL
    lead = CONV_HALO - (CONV_WIDTH - 1)

    def body(dc_ref, dn_ref, hc_ref, hp_ref, a0_ref, a1_ref, g0_ref, g1_ref, w_ref, dglu_ref, dw_ref, dcat, hcat, wacc, dh0):
        i = pl.program_id(0)

        @pl.when(i == 0)
        def _():
            wacc[...] = jnp.zeros_like(wacc)

        dcat[0:tm, :] = dc_ref[...]
        dcat[tm:, :] = jnp.where(i == nt - 1, 0.0, dn_ref[...])
        hcat[0:CONV_HALO, :] = jnp.where(i == 0, 0.0, hp_ref[...])
        hcat[CONV_HALO:, :] = hc_ref[...]
        span = CONV_UNIT + CONV_HALO

        def unit_rows(c, carry):
            r0 = pl.multiple_of(c * CONV_UNIT, CONV_UNIT)
            for j in range(D_MODEL // 128):
                ls = slice(j * 128, (j + 1) * 128)
                dwin = dcat[pl.ds(r0, span), ls]
                acc = jnp.zeros((CONV_UNIT, 128), f32)
                for r, adv in _advanced_windows(dwin):
                    for q, off in _tap_offsets(r, span):
                        k = CONV_WIDTH - 1 - off
                        if 0 <= k < CONV_WIDTH:
                            acc = acc + adv[8 * q:8 * q + CONV_UNIT] * w_ref[k:k + 1, ls]
                dh0[pl.ds(r0, CONV_UNIT), ls] = acc
                dcur = dwin[0:CONV_UNIT]
                for r, adv in _advanced_windows(hcat[pl.ds(r0, span), ls]):
                    for q, off in _tap_offsets(r, span):
                        k = off - lead
                        if 0 <= k < CONV_WIDTH:
                            prod = dcur * adv[8 * q:8 * q + CONV_UNIT]
                            wacc[k, :, ls] += jnp.sum(prod.reshape(CONV_UNIT // 8, 8, 128), axis=0)
            return carry

        lax.fori_loop(0, tm // CONV_UNIT, unit_rows, 0)
        dh0v = dh0[...]
        av = jnp.concatenate([a0_ref[...], a1_ref[...]], axis=-1).astype(f32)
        sg = _sigmoid(jnp.concatenate([g0_ref[...], g1_ref[...]], axis=-1).astype(f32))
        dglu_ref[:, 0:D_MODEL] = (dh0v * sg).astype(bf16)
        dglu_ref[:, D_MODEL:2 * D_MODEL] = (dh0v * av * sg * (1.0 - sg)).astype(bf16)

        @pl.when(i == nt - 1)
        def _():
            for k in range(CONV_WIDTH):
                dw_ref[k:k + 1, :] = jnp.sum(wacc[k], axis=0, keepdims=True)
            dw_ref[CONV_WIDTH:CONV_WIDTH + 1, :] = jnp.zeros((1, D_MODEL), f32)

    row = pl.BlockSpec((tm, D_MODEL), lambda i: (i, 0))

    def col_spec(off):
        return pl.BlockSpec((tm, COL), lambda i: (i, a0 + off))

    return _call(
        body, name="conv_bwd", grid=(nt,),
        in_specs=[row, pl.BlockSpec((CONV_HALO, D_MODEL), lambda i: (jnp.minimum((i + 1) * per, nh - 1), 0)),
                  row, pl.BlockSpec((CONV_HALO, D_MODEL), lambda i: (jnp.maximum(i * per - 1, 0), 0)),
                  col_spec(0), col_spec(1), col_spec(2), col_spec(3),
                  pl.BlockSpec((CONV_WIDTH, D_MODEL), lambda i: (0, 0))],
        out_specs=[pl.BlockSpec((tm, 2 * D_MODEL), lambda i: (i, 0)), pl.BlockSpec((CONV_WIDTH + 1, D_MODEL), lambda i: (0, 0))],
        out_shape=[S((T, 2 * D_MODEL), bf16), S((CONV_WIDTH + 1, D_MODEL), f32)],
        scratch_shapes=[pltpu.VMEM((tm + CONV_HALO, D_MODEL), f32), pltpu.VMEM((tm + CONV_HALO, D_MODEL), f32),
                        pltpu.VMEM((CONV_WIDTH, 8, D_MODEL), f32), pltpu.VMEM((tm, D_MODEL), f32)],
        args=(dh1, dh1, h0, h0, proj, proj, proj, proj, w_dw), ride=ride)


def _attn_bwd(qn, kn, vb, o, do, lse, bias, sinks, ride=None):
    T = qn.shape[0]
    nb = T // QBLOCK

    def body(q_ref, kc_ref, kp_ref, vc_ref, vp_ref, o_ref, do_ref, lse_ref, b_ref, s_ref,
             dq_ref, dk_ref, dv_ref, db_ref, dsk_ref, kcar, vcar, s_scr, dp_scr, p_scr, ds_scr):
        n = pl.program_id(0)

        @pl.when(n == 0)
        def _():
            db_ref[...] = jnp.zeros_like(db_ref)
            dsk_ref[...] = jnp.zeros_like(dsk_ref)
            kcar[...] = jnp.zeros_like(kcar)
            vcar[...] = jnp.zeros_like(vcar)

        @pl.when(n < nb)
        def _():
            lane = lax.broadcasted_iota(jnp.int32, (QBLOCK, 2 * HEAD_DIM), 1)
            lane_row = lax.broadcasted_iota(jnp.int32, (1, 2 * HEAD_DIM), 1)
            kx = _kv_placements(jnp.concatenate([kp_ref[...], kc_ref[...]], axis=0))
            vx = _kv_placements(jnp.concatenate([vp_ref[...], vc_ref[...]], axis=0))
            lse_tile = lse_ref[...]
            delta, lse_c = {}, {}
            for pr in range(N_Q_HEADS // 2):
                dop = do_ref[:, _pair_cols(pr)]
                dl = dop.astype(f32) * o_ref[:, _pair_cols(pr)].astype(f32)
                for side in range(2):
                    hq = 2 * pr + side
                    h = hq // GROUP
                    qm = _one_head(q_ref[:, _pair_cols(pr)], side)
                    s_scr[_head_rows(hq), :] = _nt(qm, kx[h, side]) + b_ref[_head_rows(hq), :]
                    dp_scr[_head_rows(hq), :] = _nt(_one_head(dop, side), vx[h, side])
                    delta[hq] = jnp.sum(_one_head(dl, side), axis=-1, keepdims=True)
                    lse_c[hq] = jnp.sum(jnp.where(lane == hq, lse_tile, 0.0), axis=-1, keepdims=True)
            dsk = jnp.zeros((1, 2 * HEAD_DIM), f32)
            for hq in range(N_Q_HEADS):
                p = jnp.exp(s_scr[_head_rows(hq), :] - lse_c[hq])
                ds = p * (dp_scr[_head_rows(hq), :] - delta[hq])
                db_ref[_head_rows(hq), :] += ds
                p_scr[_head_rows(hq), :] = p.astype(bf16)
                ds_scr[_head_rows(hq), :] = ds.astype(bf16)
                psink = jnp.exp(s_ref[0, hq] - lse_c[hq])
                dsk = dsk - jnp.where(lane_row == hq, jnp.sum(psink * delta[hq], axis=0, keepdims=True), 0.0)
            dsk_ref[...] += dsk
            for pr in range(N_Q_HEADS // 2):
                h = 2 * pr // GROUP
                dq_ref[:, _pair_cols(pr)] = (jnp.dot(ds_scr[_head_rows(2 * pr), :], kx[h, 0], preferred_element_type=f32)
                                             + jnp.dot(ds_scr[_head_rows(2 * pr + 1), :], kx[h, 1], preferred_element_type=f32))
            folded_k, folded_v = [], []
            for h in range(N_KV_HEADS):
                ka = jnp.zeros((2 * QBLOCK, 2 * HEAD_DIM), f32)
                va = jnp.zeros((2 * QBLOCK, 2 * HEAD_DIM), f32)
                for g in range(GROUP):
                    hq = h * GROUP + g
                    ka = ka + _tn(ds_scr[_head_rows(hq), :], _one_head(q_ref[:, _pair_cols(hq // 2)], hq % 2))
                    va = va + _tn(p_scr[_head_rows(hq), :], _one_head(do_ref[:, _pair_cols(hq // 2)], hq % 2))
                folded_k.append(ka + _swap_halves(ka))
                folded_v.append(va + _swap_halves(va))
            low = _low_lanes()
            for m in range(N_KV_HEADS // 2):
                cs = _pair_cols(m)
                for folded, out_ref, car in ((folded_k, dk_ref, kcar), (folded_v, dv_ref, vcar)):
                    band = jnp.where(low, folded[2 * m], folded[2 * m + 1])
                    out_ref[:, cs] = car[:, cs] + band[0:QBLOCK, :]
                    car[:, cs] = band[QBLOCK:, :]

        @pl.when(n == nb)
        def _():
            dk_ref[...] = kcar[...]
            dv_ref[...] = vcar[...]

    cur = lambda n: (jnp.minimum(n, nb - 1), 0)
    prev = lambda n: (jnp.clip(n - 1, 0, nb - 1), 0)
    qspec = pl.BlockSpec((QBLOCK, ATTN_WIDTH), cur)
    kcur, kprev = pl.BlockSpec((QBLOCK, KV_WIDTH), cur), pl.BlockSpec((QBLOCK, KV_WIDTH), prev)
    whole = lambda shape: pl.BlockSpec(shape, lambda n: (0,) * len(shape))
    scores = (N_Q_HEADS * QBLOCK, 2 * QBLOCK)
    return _call(
        body, name="attn_bwd", grid=(nb + 1,),
        in_specs=[qspec, kcur, kprev, kcur, kprev, qspec, qspec, pl.BlockSpec((QBLOCK, 2 * HEAD_DIM), cur), _bias_spec(), SMEM],
        out_specs=[qspec, kprev, kprev, whole(scores), whole((1, 2 * HEAD_DIM))],
        out_shape=[S((T, ATTN_WIDTH), f32), S((T, KV_WIDTH), f32), S((T, KV_WIDTH), f32), S(scores, f32),
                   S((1, 2 * HEAD_DIM), f32)],
        scratch_shapes=[pltpu.VMEM((QBLOCK, KV_WIDTH), f32), pltpu.VMEM((QBLOCK, KV_WIDTH), f32),
                        pltpu.VMEM(scores, f32), pltpu.VMEM(scores, f32), pltpu.VMEM(scores, bf16), pltpu.VMEM(scores, bf16)],
        args=(qn, kn, kn, vb, vb, o, do, lse, bias, sinks), ride=ride)


def _rel_bias_bwd(dbias, bucket):
    def body(d_ref, bk_ref, o_ref):
        b = bk_ref[...]
        for k in range(N_BUCKETS):
            mk = b == k
            for h in range(N_Q_HEADS):
                o_ref[k, h] = jnp.sum(jnp.where(mk, d_ref[h * QBLOCK:(h + 1) * QBLOCK, :], 0.0))

    return pl.pallas_call(body, name="rel_bias_bwd", out_shape=S((N_BUCKETS, N_Q_HEADS), f32), out_specs=SMEM)(dbias, bucket)


def _qk_norm_bwd(dq, dk, dv, proj, qg, kg, tm):
    T = dq.shape[0]
    scale = HEAD_DIM ** -0.5

    def pair_bwd(dy, x, gv):
        r = _pair_rstd(x, True)
        xn = x * r
        dxn = dy * gv
        dx = r * (dxn - xn * _pair_mean(dxn * xn, True))
        return dx, jnp.sum(dy * xn, axis=0, keepdims=True)

    def body(dq_ref, dk_ref, dv_ref, p_ref, qg_ref, kg_ref, out_ref, dqg_ref, dkg_ref):
        @pl.when(pl.program_id(0) == 0)
        def _():
            dqg_ref[...] = jnp.zeros_like(dqg_ref)
            dkg_ref[...] = jnp.zeros_like(dkg_ref)

        qgv, kgv = qg_ref[...], kg_ref[...]
        dqg = jnp.zeros((1, 2 * HEAD_DIM), f32)
        for pr in range(N_Q_HEADS // 2):
            dx, dg = pair_bwd(dq_ref[:, _pair_cols(pr)] * scale, p_ref[:, _pair_cols(pr)].astype(f32), qgv)
            out_ref[:, _pair_cols(pr)] = dx.astype(bf16)
            dqg = dqg + dg
        dkg = jnp.zeros((1, 2 * HEAD_DIM), f32)
        for pr in range(N_KV_HEADS // 2):
            ps = slice(Q_END + pr * 2 * HEAD_DIM, Q_END + (pr + 1) * 2 * HEAD_DIM)
            dx, dg = pair_bwd(dk_ref[:, _pair_cols(pr)], p_ref[:, ps].astype(f32), kgv)
            out_ref[:, ps] = dx.astype(bf16)
            dkg = dkg + dg
        out_ref[:, K_END:V_END] = dv_ref[...].astype(bf16)
        dqg_ref[...] += dqg
        dkg_ref[...] += dkg

    vec = pl.BlockSpec((1, 2 * HEAD_DIM), lambda i: (0, 0))
    return pl.pallas_call(
        body, name="qk_norm_bwd", grid=(T // tm,),
        in_specs=[pl.BlockSpec((tm, ATTN_WIDTH), lambda i: (i, 0)), pl.BlockSpec((tm, KV_WIDTH), lambda i: (i, 0)),
                  pl.BlockSpec((tm, KV_WIDTH), lambda i: (i, 0)), pl.BlockSpec((tm, V_END), lambda i: (i, 0)), vec, vec],
        out_specs=[pl.BlockSpec((tm, V_END), lambda i: (i, 0)), vec, vec],
        out_shape=[S((T, V_END), bf16), S((1, 2 * HEAD_DIM), f32), S((1, 2 * HEAD_DIM), f32)],
        compiler_params=_params("arbitrary"),
    )(dq, dk, dv, proj, qg, kg)


def _in_bwd(dqkv, dglu, dgates, w_in, x, g, dx1, tm, ride=None):
    T = x.shape[0]
    pieces = (dqkv, dglu, dgates)
    starts = [0, dqkv.shape[1], dqkv.shape[1] + dglu.shape[1]]

    def body(a0_ref, a1_ref, a2_ref, w_ref, x_ref, g_ref, d_ref, gx_ref, dg_ref):
        @pl.when(pl.program_id(0) == 0)
        def _():
            dg_ref[...] = jnp.zeros_like(dg_ref)

        du = jnp.zeros((tm, D_MODEL), f32)
        for a_ref, c0 in zip((a0_ref, a1_ref, a2_ref), starts):
            du = du + _nt(a_ref[...], w_ref[:, c0:c0 + a_ref.shape[1]])
        dx, dg = _rms_bwd(du, x_ref[...], g_ref[...])
        gx_ref[...] = d_ref[...] + dx
        dg_ref[...] += dg

    row = pl.BlockSpec((tm, D_MODEL), lambda i: (i, 0))
    return _call(
        body, name="in_bwd", grid=(T // tm,),
        in_specs=[pl.BlockSpec((tm, p.shape[1]), lambda i: (i, 0)) for p in pieces]
        + [_resident(w_in.shape), row, _resident((1, D_MODEL)), row],
        out_specs=[row, pl.BlockSpec((1, D_MODEL), lambda i: (0, 0))],
        out_shape=[S((T, D_MODEL), f32), S((1, D_MODEL), f32)],
        args=(dqkv, dglu, dgates, w_in, x, g, dx1), ride=ride)


def _adamw(name, parts, w, m, v, tr):
    _, R, C = w.shape
    bc1 = 1.0 - ADAM_B1 ** ADAM_STEP
    bc2 = 1.0 - ADAM_B2 ** ADAM_STEP

    def body(p_ref, w_ref, m_ref, v_ref, g_ref, d_ref, nm_ref, nv_ref):
        g = p_ref[0].astype(f32)
        for k in range(1, N_DEV):
            g = g + p_ref[k].astype(f32)
        nm = ADAM_B1 * m_ref[...] + (1.0 - ADAM_B1) * g
        nv = ADAM_B2 * v_ref[...] + (1.0 - ADAM_B2) * (g * g)
        g_ref[...] = g
        nm_ref[...] = nm
        nv_ref[...] = nv
        d_ref[...] = -ADAM_LR * ((nm / bc1) / (jnp.sqrt(nv / bc2) + ADAM_EPS) + ADAM_WD * w_ref[...])

    blk = pl.BlockSpec((None, tr, C), lambda i: (0, i, 0))
    return pl.pallas_call(
        body, name=name, grid=(R // tr,),
        in_specs=[pl.BlockSpec((N_DEV, tr, C), lambda i: (0, i, 0)), blk, blk, blk],
        out_specs=[blk, blk, blk, blk], out_shape=[S((1, R, C), f32)] * 4,
        compiler_params=_params("parallel"),
    )(parts, w, m, v)


def _tile(T, pref):
    return min(T, pref)


def _pad_rows(a, rows):
    return jnp.pad(a, ((0, rows - a.shape[0]), (0, 0)))


def kernel(x, norm_mix_g, w_in, q_norm_g, k_norm_g, attn_sinks, rel_bias, w_attn_o, w_dw, b_dw, conv_ln_g, conv_ln_b, w_conv_out, w_out, norm_mlp_g, w_ff1, w_ff2, loss_target, m_norm_mix_g, m_w_in, m_q_norm_g, m_k_norm_g, m_attn_sinks, m_rel_bias, m_w_attn_o, m_w_dw, m_b_dw, m_conv_ln_g, m_conv_ln_b, m_w_conv_out, m_w_out, m_norm_mlp_g, m_w_ff1, m_w_ff2, v_norm_mix_g, v_w_in, v_q_norm_g, v_k_norm_g, v_attn_sinks, v_rel_bias, v_w_attn_o, v_w_dw, v_b_dw, v_conv_ln_g, v_conv_ln_b, v_w_conv_out, v_w_out, v_norm_mlp_g, v_w_ff1, v_w_ff2):
    T = x.shape[1]
    xs = x[0]
    tgt = loss_target[0]
    in_shard = IN_WIDTH // N_DEV
    dw_rows = CONV_WIDTH + 1
    ch_shard = D_MODEL // N_DEV
    tb = _tile(T, 512)
    tt = _tile(T, 2048)
    bucket = jnp.asarray(_t5_bucket_table())

    g_in, g_dw = _exchange("gather_w_in", [w_in[0].astype(bf16), _pad_rows(w_dw[0], dw_rows)], gather=True, two_level=True)
    W_in = jnp.transpose(g_in, (1, 0, 2)).reshape(D_MODEL, IN_WIDTH)
    W_dw = jnp.transpose(g_dw, (1, 0, 2)).reshape(dw_rows, D_MODEL)[:CONV_WIDTH]

    mix_shards = _Gather([w_attn_o[0].astype(bf16), w_conv_out[0].astype(bf16), w_out[0].astype(bf16)])
    qg2, kg2 = jnp.tile(q_norm_g, (1, 2)), jnp.tile(k_norm_g, (1, 2))
    (proj, u, qn, kn, vb, h0), (g_ao, g_co, g_o) = _proj_fwd(xs, norm_mix_g, W_in, qg2, kg2, tb, ride=mix_shards)
    W_ao = g_ao.reshape(D_MODEL, D_MODEL)
    W_co = g_co.reshape(D_MODEL, D_MODEL)
    W_o = g_o.reshape(D_MODEL, D_MODEL)
    bias = _bias_table(rel_bias, bucket)
    (o, lse), (g_f1, g_f2) = _attn_fwd(qn, kn, vb, bias, attn_sinks,
                                       ride=_Gather([w_ff1[0].astype(bf16).T, w_ff2[0].astype(bf16)]))
    W_f1t = g_f1.reshape(D_FF, D_MODEL)
    (h1, h3), _ = _conv_fwd(h0, W_dw, b_dw, conv_ln_g, conv_ln_b, tb)
    x1, attn, conv, merged = _mix_fwd(xs, o, h3, proj, W_ao, W_co, W_o, tb)
    W_f2 = g_f2.reshape(D_FF, D_MODEL)
    a, u2, dy, dyb, loss_parts = _ffn_fwd(x1, norm_mlp_g, W_f1t, W_f2, tgt, tb)

    gw_f2 = _wgrad("wgrad_ff2", a, dyb, D_MODEL, D_MODEL, tt, relu2=True).reshape(N_DEV, FF_CHUNK, D_MODEL)
    (da, dx1, dx1b, d_norm_mlp_g), (l_f2,) = _ffn_bwd(dy, dyb, a, x1, norm_mlp_g, W_f1t, W_f2, tb,
                                                      ride=_Exchange([gw_f2], gather=False))
    gw_f1 = _wgrad("wgrad_ff1", u2, da, D_MODEL, 4 * FF_CHUNK, tt, slab=FF_CHUNK)
    gw_o = _wgrad("wgrad_out", merged, dx1b, D_MODEL, D_MODEL, tt).reshape(N_DEV, ch_shard, D_MODEL)
    (dattn, dconv, do, dh1, dgates, d_ln_g, d_ln_b, d_b_dw), _ = _mix_bwd(
        dx1b, proj, attn, conv, h1, conv_ln_g, conv_ln_b, W_ao, W_co, W_o, tb)
    gw_ao = _wgrad("wgrad_attn_o", o, dattn, D_MODEL, D_MODEL, tt).reshape(N_DEV, ch_shard, D_MODEL)
    gw_co = _wgrad("wgrad_conv_out", h3, dconv, D_MODEL, D_MODEL, tt).reshape(N_DEV, ch_shard, D_MODEL)
    (dglu, d_w_dw), (l_f1, l_o, l_ao, l_co) = _conv_bwd(dh1, h0, proj, W_dw, tb,
                                                        ride=_Exchange([gw_f1, gw_o, gw_ao, gw_co], gather=False))
    (dq, dk, dv, dbias, d_sinks), _ = _attn_bwd(qn, kn, vb, o, do, lse, bias, attn_sinks)
    d_sinks = d_sinks[:, :N_Q_HEADS]
    d_rel_bias = _rel_bias_bwd(dbias, bucket)
    dqkv, d_qg, d_kg = _qk_norm_bwd(dq, dk, dv, proj, qg2, kg2, tb)
    d_qg = d_qg[:, :HEAD_DIM] + d_qg[:, HEAD_DIM:]
    d_kg = d_kg[:, :HEAD_DIM] + d_kg[:, HEAD_DIM:]
    gw_in = jnp.concatenate([_wgrad("wgrad_in_qkv", u, dqkv, D_MODEL, V_END, tt),
                             _wgrad("wgrad_in_glu", u, dglu, D_MODEL, 2 * D_MODEL, tt),
                             _wgrad("wgrad_in_gates", u, dgates, D_MODEL, 2 * D_MODEL, tt)], axis=1)
    gw_in = jnp.transpose(gw_in.reshape(D_MODEL, N_DEV, in_shard), (1, 0, 2))
    gw_dw = jnp.transpose(d_w_dw.reshape(dw_rows, N_DEV, ch_shard), (1, 0, 2))
    (grad_x, d_norm_mix_g), (l_in, l_dw) = _in_bwd(dqkv, dglu, dgates, W_in, xs, norm_mix_g, dx1, tb,
                                                    ride=_Exchange([gw_in, gw_dw], gather=False))

    def row(vec):
        flat = vec.reshape(1, -1)
        return jnp.pad(flat, ((0, 0), (0, D_MODEL - flat.shape[1])))

    def pack_small(nm, qg, kg, sk, rb, bd, lg, lb, nl, extra=None):
        tail = jnp.concatenate([qg.reshape(1, -1), kg.reshape(1, -1), sk.reshape(1, -1), rb.reshape(1, -1)], axis=1)
        spare = jnp.zeros((1, D_MODEL), f32) if extra is None else row(extra)
        return jnp.concatenate([row(nm), row(bd), row(lg), row(lb), row(nl), row(tail), spare, jnp.zeros((1, D_MODEL), f32)], axis=0)

    def unpack_small(p):
        t = p[5]
        o0, o1, o2 = HEAD_DIM, 2 * HEAD_DIM, 2 * HEAD_DIM + N_Q_HEADS
        return dict(norm_mix_g=p[0:1], b_dw=p[1:2], conv_ln_g=p[2:3], conv_ln_b=p[3:4], norm_mlp_g=p[4:5],
                    q_norm_g=t[0:o0].reshape(1, HEAD_DIM), k_norm_g=t[o0:o1].reshape(1, HEAD_DIM),
                    attn_sinks=t[o1:o2].reshape(1, N_Q_HEADS),
                    rel_bias=t[o2:o2 + N_BUCKETS * N_Q_HEADS].reshape(N_BUCKETS, N_Q_HEADS))

    small_g = pack_small(d_norm_mix_g, d_qg, d_kg, d_sinks, d_rel_bias, d_b_dw, d_ln_g, d_ln_b, d_norm_mlp_g,
                         extra=jnp.sum(loss_parts[:, 0, 0]))
    (l_small,) = _exchange("gather_small_grads", [small_g], gather=True)


    res = {}
    res["w_in"] = _adamw("adamw_in", l_in, w_in, m_w_in, v_w_in, 256)
    res["w_attn_o"] = _adamw("adamw_attn_o", l_ao, w_attn_o, m_w_attn_o, v_w_attn_o, ch_shard)
    res["w_conv_out"] = _adamw("adamw_conv_out", l_co, w_conv_out, m_w_conv_out, v_w_conv_out, ch_shard)
    res["w_out"] = _adamw("adamw_out", l_o, w_out, m_w_out, v_w_out, ch_shard)
    res["w_ff1"] = _adamw("adamw_ff1", l_f1, w_ff1, m_w_ff1, v_w_ff1, 256)
    res["w_ff2"] = _adamw("adamw_ff2", l_f2, w_ff2, m_w_ff2, v_w_ff2, 256)
    pad_dw = lambda t: _pad_rows(t[0], dw_rows)[None]
    res["w_dw"] = [t[:, :CONV_WIDTH] for t in _adamw("adamw_dw", l_dw, pad_dw(w_dw), pad_dw(m_w_dw), pad_dw(v_w_dw), dw_rows)]
    small_w = pack_small(norm_mix_g, q_norm_g, k_norm_g, attn_sinks, rel_bias, b_dw, conv_ln_g, conv_ln_b, norm_mlp_g)
    small_m = pack_small(m_norm_mix_g, m_q_norm_g, m_k_norm_g, m_attn_sinks, m_rel_bias, m_b_dw, m_conv_ln_g, m_conv_ln_b, m_norm_mlp_g)
    small_v = pack_small(v_norm_mix_g, v_q_norm_g, v_k_norm_g, v_attn_sinks, v_rel_bias, v_b_dw, v_conv_ln_g, v_conv_ln_b, v_norm_mlp_g)
    small_out = _adamw("adamw_small", l_small, small_w[None], small_m[None], small_v[None], 8)
    small4 = [unpack_small(t[0]) for t in small_out]
    loss = small_out[0][0, 6, 0]

    order = ["norm_mix_g", "w_in", "q_norm_g", "k_norm_g", "attn_sinks", "rel_bias", "w_attn_o", "w_dw", "b_dw",
             "conv_ln_g", "conv_ln_b", "w_conv_out", "w_out", "norm_mlp_g", "w_ff1", "w_ff2"]
    stacked = {"w_in", "w_attn_o", "w_dw", "w_conv_out", "w_out", "w_ff1", "w_ff2"}
    outs = [loss, grad_x[None]]
    for k in range(4):
        for nme in order:
            if nme in stacked:
                outs.append(res[nme][k])
            else:
                outs.append(small4[k][nme])
    return tuple(outs)
```

```python
import functools

import numpy as np
import jax
import jax.numpy as jnp
from jax import lax
from jax.experimental import pallas as pl
from jax.experimental.pallas import tpu as pltpu

f32 = jnp.float32
bf16 = jnp.bfloat16
S = jax.ShapeDtypeStruct

N_DEV = 8
D_MODEL = 1024
HEAD_DIM = 64
N_Q_HEADS = 16
N_KV_HEADS = 4
GROUP = N_Q_HEADS // N_KV_HEADS
ATTN_WIDTH = N_Q_HEADS * HEAD_DIM
KV_WIDTH = N_KV_HEADS * HEAD_DIM
QBLOCK = 128
CONV_WIDTH = 31
CONV_HALO = 32
CONV_UNIT = 64
D_FF = 4 * D_MODEL
N_BUCKETS = 32
MAX_DISTANCE = 128
EPS = 1e-6
NEG = -1e30
Q_END = ATTN_WIDTH
K_END = Q_END + KV_WIDTH
V_END = K_END + KV_WIDTH
GLU_END = V_END + 2 * D_MODEL
IN_WIDTH = GLU_END + 2 * D_MODEL
COL = 512
FF_CHUNK = D_FF // N_DEV

ADAM_LR = 0.001
ADAM_B1 = 0.9
ADAM_B2 = 0.999
ADAM_EPS = 1e-08
ADAM_WD = 0.01
ADAM_STEP = 10

VMEM_LIMIT = 56 * 1024 * 1024

MESH_ID = pl.DeviceIdType.MESH
ANY = pl.BlockSpec(memory_space=pl.ANY)
SMEM = pl.BlockSpec(memory_space=pltpu.SMEM)


def _params(*sem):
    return pltpu.CompilerParams(dimension_semantics=sem, vmem_limit_bytes=VMEM_LIMIT)


def _nt(a, b):
    return lax.dot_general(a, b, (((1,), (1,)), ((), ())), preferred_element_type=f32)


def _tn(a, b):
    return lax.dot_general(a, b, (((0,), (0,)), ((), ())), preferred_element_type=f32)


def _sigmoid(z):
    return 1.0 / (1.0 + jnp.exp(-z))


def _t5_bucket_table():
    qi = np.arange(QBLOCK, dtype=np.int32)[:, None]
    kj = np.arange(2 * QBLOCK, dtype=np.int32)[None, :]
    dist = qi + QBLOCK - kj
    n = np.maximum(dist, 0)
    max_exact = N_BUCKETS // 2
    nf = np.maximum(n, 1).astype(np.float32)
    large = max_exact + (np.log(nf / np.float32(max_exact)) / np.float32(np.log(MAX_DISTANCE / max_exact))
                         * np.float32(N_BUCKETS - max_exact)).astype(np.int32)
    large = np.minimum(large, N_BUCKETS - 1)
    bucket = np.where(n < max_exact, n, large)
    valid = (dist >= 0) & (dist < QBLOCK)
    return np.where(valid, bucket, -1).astype(np.int32)


def _peer(d):
    x, y, c = lax.axis_index("x"), lax.axis_index("y"), lax.axis_index("c")
    dx, dy, dc = (d >> 2) & 1, (d >> 1) & 1, d & 1
    px, py, pc = x ^ dx, y ^ dy, c ^ dc
    return (px, py, pc), 4 * px + 2 * py + pc


class _Exchange:
    def __init__(self, arrays, gather):
        self.arrays, self.gather, self.n = list(arrays), gather, len(arrays)
        self.out_shape = [S(((N_DEV,) + a.shape) if gather else a.shape, a.dtype) for a in self.arrays]
        self.scratch = [pltpu.SemaphoreType.DMA((self.n, N_DEV - 1)), pltpu.SemaphoreType.DMA((self.n, N_DEV - 1)),
                        pltpu.SemaphoreType.DMA((self.n,))]

    def _copies(self, ins, outs, sems):
        send_sems, recv_sems, local_sems = sems
        _, me = _peer(0)
        local, sends, recvs = [], [], []
        for k in range(self.n):
            src = ins[k] if self.gather else ins[k].at[me]
            local.append(pltpu.make_async_copy(src, outs[k].at[me], local_sems.at[k]))
        for d in range(1, N_DEV):
            peer, pidx = _peer(d)
            for k in range(self.n):
                src = ins[k] if self.gather else ins[k].at[pidx]
                common = dict(src_ref=src, send_sem=send_sems.at[k, d - 1], recv_sem=recv_sems.at[k, d - 1],
                              device_id=peer, device_id_type=MESH_ID)
                sends.append(pltpu.make_async_remote_copy(dst_ref=outs[k].at[me], **common))
                recvs.append(pltpu.make_async_remote_copy(dst_ref=outs[k].at[pidx], **common))
        return local, sends, recvs

    def start(self, ins, outs, sems):
        local, sends, _ = self._copies(ins, outs, sems)
        for cp in local + sends:
            cp.start()

    def wait(self, ins, outs, sems):
        local, sends, recvs = self._copies(ins, outs, sems)
        for cp in recvs:
            cp.wait_recv()
        for cp in sends:
            cp.wait_send()
        for cp in local:
            cp.wait()


class _Gather:
    CHIPS = (4, 2, 6)
    SLOTS = 1 + 2 * len(CHIPS)

    def __init__(self, arrays):
        self.arrays, self.n = list(arrays), len(arrays)
        self.out_shape = [S((N_DEV,) + a.shape, a.dtype) for a in self.arrays]
        self.scratch = [pltpu.SemaphoreType.DMA((self.n, self.SLOTS)), pltpu.SemaphoreType.DMA((self.n, self.SLOTS)),
                        pltpu.SemaphoreType.DMA((self.n,))]

    @staticmethod
    def _copy(outs, sems, k, slot, src, block, to):
        return pltpu.make_async_remote_copy(src_ref=src, dst_ref=outs[k].at[block], send_sem=sems[0].at[k, slot],
                                            recv_sem=sems[1].at[k, slot], device_id=to, device_id_type=MESH_ID)

    def _local(self, ins, outs, sems):
        _, me = _peer(0)
        return [pltpu.make_async_copy(ins[k], outs[k].at[me], sems[2].at[k]) for k in range(self.n)]

    def start(self, ins, outs, sems):
        _, me = _peer(0)
        sibling, _ = _peer(1)
        for cp in self._local(ins, outs, sems):
            cp.start()
        for k in range(self.n):
            self._copy(outs, sems, k, 0, ins[k], me, sibling).start()
            for j, d in enumerate(self.CHIPS):
                self._copy(outs, sems, k, 1 + j, ins[k], me, _peer(d)[0]).start()

    def mid(self, ins, outs, sems):
        sibling, _ = _peer(1)
        for j, d in enumerate(self.CHIPS):
            chip, block = _peer(d)
            for k in range(self.n):
                self._copy(outs, sems, k, 1 + j, ins[k], block, chip).wait_recv()
                self._copy(outs, sems, k, 4 + j, outs[k].at[block], block, sibling).start()

    def wait(self, ins, outs, sems):
        _, me = _peer(0)
        sibling, sib_block = _peer(1)
        for k in range(self.n):
            self._copy(outs, sems, k, 0, ins[k], sib_block, sibling).wait_recv()
            for j, d in enumerate(self.CHIPS):
                self._copy(outs, sems, k, 4 + j, ins[k], _peer(d ^ 1)[1], sibling).wait_recv()
        for k in range(self.n):
            self._copy(outs, sems, k, 0, ins[k], me, sibling).wait_send()
            for j, d in enumerate(self.CHIPS):
                chip, block = _peer(d)
                self._copy(outs, sems, k, 1 + j, ins[k], me, chip).wait_send()
                self._copy(outs, sems, k, 4 + j, outs[k].at[block], block, sibling).wait_send()
        for cp in self._local(ins, outs, sems):
            cp.wait()


def _exchange(name, arrays, gather, two_level=False):
    ex = _Gather(arrays) if two_level else _Exchange(arrays, gather)
    n = ex.n

    def body(*refs):
        ins, outs, sems = refs[:n], refs[n:2 * n], refs[2 * n:]
        ex.start(ins, outs, sems)
        if two_level:
            ex.mid(ins, outs, sems)
        ex.wait(ins, outs, sems)

    return pl.pallas_call(body, name=name, out_shape=ex.out_shape, in_specs=[ANY] * n, out_specs=[ANY] * n,
                          scratch_shapes=ex.scratch)(*arrays)


def _call(body, *, name, grid, in_specs, out_specs, out_shape, args, scratch_shapes=(), ride=None):
    n_in, n_out, n_sc = len(in_specs), len(out_specs), len(scratch_shapes)
    sem = ("arbitrary",) * len(grid)
    if ride is None:
        res = pl.pallas_call(body, name=name, grid=grid, in_specs=list(in_specs), out_specs=list(out_specs),
                             out_shape=list(out_shape), scratch_shapes=list(scratch_shapes), compiler_params=_params(*sem))(*args)
        return list(res), []
    nx = ride.n

    def riding(*refs):
        ins, xin = refs[:n_in], refs[n_in:n_in + nx]
        outs, xout = refs[n_in + nx:n_in + nx + n_out], refs[n_in + nx + n_out:n_in + 2 * nx + n_out]
        rest = refs[n_in + 2 * nx + n_out:]
        scratch, sems = rest[:n_sc], rest[n_sc:]
        ids = [pl.program_id(ax) for ax in range(len(grid))]
        first = functools.reduce(jnp.logical_and, [i == 0 for i in ids])
        last = functools.reduce(jnp.logical_and, [i == g - 1 for i, g in zip(ids, grid)])

        @pl.when(first)
        def _():
            ride.start(xin, xout, sems)

        if hasattr(ride, "mid"):
            halfway = functools.reduce(jnp.logical_and, [ids[0] == grid[0] // 2] + [i == 0 for i in ids[1:]])

            @pl.when(halfway)
            def _():
                ride.mid(xin, xout, sems)

        body(*ins, *outs, *scratch)

        @pl.when(last)
        def _():
            ride.wait(xin, xout, sems)

    res = pl.pallas_call(
        riding, name=name, grid=grid, in_specs=list(in_specs) + [ANY] * nx, out_specs=list(out_specs) + [ANY] * nx,
        out_shape=list(out_shape) + ride.out_shape, scratch_shapes=list(scratch_shapes) + ride.scratch,
        compiler_params=_params(*sem))(*args, *ride.arrays)
    return list(res[:n_out]), list(res[n_out:])


def _resident(shape):
    return pl.BlockSpec(shape, lambda *_: (0,) * len(shape), pipeline_mode=pl.Buffered(1))


def _proj_fwd(x, g, w, qg, kg, tm, ride=None):
    T, K = x.shape
    N = w.shape[1]
    per = COL // (2 * HEAD_DIM)
    assert Q_END % COL == 0 and V_END == Q_END + COL and (GLU_END - V_END) == 4 * COL and KV_WIDTH == COL // 2

    def body(x_ref, g_ref, w_ref, qg_ref, kg_ref, o_ref, u_ref, qn_ref, kn_ref, vb_ref, h0_ref):
        xv = x_ref[...]
        r = lax.rsqrt(jnp.mean(xv * xv, axis=-1, keepdims=True) + EPS)
        u = (xv * r * g_ref[...]).astype(bf16)
        u_ref[...] = u

        def block(c):
            cs = slice(c * COL, (c + 1) * COL)
            pc = jnp.dot(u, w_ref[:, cs], preferred_element_type=f32)
            o_ref[:, cs] = pc.astype(bf16)
            return pc

        qgv = qg_ref[...] * (HEAD_DIM ** -0.5)
        for c in range(Q_END // COL):
            pc = block(c)
            for t in range(per):
                xq = pc[:, _pair_cols(t)]
                qn_ref[:, _pair_cols(c * per + t)] = (xq * _pair_rstd(xq, False) * qgv).astype(bf16)
        pc = block(Q_END // COL)
        for t in range(KV_WIDTH // (2 * HEAD_DIM)):
            xk = pc[:, _pair_cols(t)]
            kn_ref[:, _pair_cols(t)] = (xk * _pair_rstd(xk, False) * kg_ref[...]).astype(bf16)
        vb_ref[...] = pc[:, KV_WIDTH:].astype(bf16)
        a0 = V_END // COL
        for half in range(2):
            gate = block(a0 + 2 + half)
            h0_ref[:, half * COL:(half + 1) * COL] = block(a0 + half) * _sigmoid(gate)
        for c in range(GLU_END // COL, N // COL):
            block(c)

    row = lambda width: pl.BlockSpec((tm, width), lambda i: (i, 0))
    return _call(
        body, name="proj_fwd", grid=(T // tm,),
        in_specs=[row(K), _resident((1, K)), _resident((K, N)), _resident((1, 2 * HEAD_DIM)), _resident((1, 2 * HEAD_DIM))],
        out_specs=[row(N), row(K), row(ATTN_WIDTH), row(KV_WIDTH), row(KV_WIDTH), row(D_MODEL)],
        out_shape=[S((T, N), bf16), S((T, K), bf16), S((T, ATTN_WIDTH), bf16), S((T, KV_WIDTH), bf16), S((T, KV_WIDTH), bf16),
                   S((T, D_MODEL), f32)],
        args=(x, g, w, qg, kg), ride=ride)


def _bias_table(rel_bias, bucket):
    def body(rb_ref, bk_ref, o_ref):
        b = bk_ref[...]
        absent = lax.broadcasted_iota(jnp.int32, (QBLOCK, 2 * QBLOCK), 1) < QBLOCK
        for h in range(N_Q_HEADS):
            acc = jnp.full((QBLOCK, 2 * QBLOCK), NEG, f32)
            for k in range(N_BUCKETS):
                acc = jnp.where(b == k, rb_ref[k, h], acc)
            o_ref[0, h * QBLOCK:(h + 1) * QBLOCK, :] = acc
            o_ref[1, h * QBLOCK:(h + 1) * QBLOCK, :] = jnp.where(absent, NEG, acc)

    return pl.pallas_call(
        body, name="bias_table", out_shape=S((2, N_Q_HEADS * QBLOCK, 2 * QBLOCK), f32),
        in_specs=[SMEM, pl.BlockSpec(memory_space=pltpu.VMEM)],
    )(rel_bias, bucket)


def _bias_spec():
    return pl.BlockSpec((None, N_Q_HEADS * QBLOCK, 2 * QBLOCK), lambda n: (jnp.where(n == 0, 1, 0), 0, 0))


def _swap_halves(t):
    return jnp.concatenate([t[:, HEAD_DIM:], t[:, :HEAD_DIM]], axis=1)


def _low_lanes():
    return lax.broadcasted_iota(jnp.int32, (1, 2 * HEAD_DIM), 1) < HEAD_DIM


def _one_head(pair, side):
    zero = jnp.zeros((), pair.dtype)
    return jnp.where(_low_lanes(), pair, zero) if side == 0 else jnp.where(_low_lanes(), zero, pair)


def _pair_mean(t, on_mxu):
    if not on_mxu:
        m_lo = jnp.sum(_one_head(t, 0), axis=-1, keepdims=True) * (1.0 / HEAD_DIM)
        m_hi = jnp.sum(_one_head(t, 1), axis=-1, keepdims=True) * (1.0 / HEAD_DIM)
        return jnp.where(_low_lanes(), m_lo, m_hi)
    width = 2 * HEAD_DIM
    same_head = ((lax.broadcasted_iota(jnp.int32, (width, width), 0) < HEAD_DIM)
                 == (lax.broadcasted_iota(jnp.int32, (width, width), 1) < HEAD_DIM))
    e = jnp.where(same_head, 1.0 / HEAD_DIM, 0.0).astype(bf16)
    hi = t.astype(bf16)
    lo = (t - hi.astype(f32)).astype(bf16)
    return jnp.dot(hi, e, preferred_element_type=f32) + jnp.dot(lo, e, preferred_element_type=f32)


def _pair_rstd(x, on_mxu):
    return lax.rsqrt(_pair_mean(x * x, on_mxu) + EPS)


def _kv_placements(band):
    out = {}
    for m in range(N_KV_HEADS // 2):
        pair = band[:, m * 2 * HEAD_DIM:(m + 1) * 2 * HEAD_DIM]
        swapped = _swap_halves(pair)
        for hh in range(2):
            out[2 * m + hh, 0] = _one_head(pair if hh == 0 else swapped, 0)
            out[2 * m + hh, 1] = _one_head(swapped if hh == 0 else pair, 1)
    return out


def _head_rows(hq):
    return slice(hq * QBLOCK, (hq + 1) * QBLOCK)


def _pair_cols(pr):
    return slice(pr * 2 * HEAD_DIM, (pr + 1) * 2 * HEAD_DIM)


def _attn_fwd(qn, kn, vb, bias, sinks, ride=None):
    T = qn.shape[0]
    nb = T // QBLOCK

    def body(q_ref, kc_ref, kp_ref, vc_ref, vp_ref, b_ref, s_ref, o_ref, lse_ref, s_scr, p_scr):
        lane = lax.broadcasted_iota(jnp.int32, (QBLOCK, 2 * HEAD_DIM), 1)
        kx = _kv_placements(jnp.concatenate([kp_ref[...], kc_ref[...]], axis=0))
        vx = _kv_placements(jnp.concatenate([vp_ref[...], vc_ref[...]], axis=0))
        for hq in range(N_Q_HEADS):
            qm = _one_head(q_ref[:, _pair_cols(hq // 2)], hq % 2)
            s_scr[_head_rows(hq), :] = _nt(qm, kx[hq // GROUP, hq % 2]) + b_ref[_head_rows(hq), :]
        lse_tile = jnp.zeros((QBLOCK, 2 * HEAD_DIM), f32)
        for hq in range(N_Q_HEADS):
            s = s_scr[_head_rows(hq), :]
            sink = s_ref[0, hq]
            m = jnp.maximum(jnp.max(s, axis=-1, keepdims=True), sink)
            p = jnp.exp(s - m)
            l = jnp.sum(p, axis=-1, keepdims=True) + jnp.exp(sink - m)
            p_scr[_head_rows(hq), :] = (p * (1.0 / l)).astype(bf16)
            lse_tile = jnp.where(lane == hq, m + jnp.log(l), lse_tile)
        lse_ref[...] = lse_tile
        for pr in range(N_Q_HEADS // 2):
            h = 2 * pr // GROUP
            o_pair = (jnp.dot(p_scr[_head_rows(2 * pr), :], vx[h, 0], preferred_element_type=f32)
                      + jnp.dot(p_scr[_head_rows(2 * pr + 1), :], vx[h, 1], preferred_element_type=f32))
            o_ref[:, _pair_cols(pr)] = o_pair.astype(bf16)

    cur = lambda n: (n, 0)
    prev = lambda n: (jnp.maximum(n - 1, 0), 0)
    return _call(
        body, name="attn_fwd", grid=(nb,),
        in_specs=[pl.BlockSpec((QBLOCK, ATTN_WIDTH), cur), pl.BlockSpec((QBLOCK, KV_WIDTH), cur),
                  pl.BlockSpec((QBLOCK, KV_WIDTH), prev), pl.BlockSpec((QBLOCK, KV_WIDTH), cur),
                  pl.BlockSpec((QBLOCK, KV_WIDTH), prev), _bias_spec(), SMEM],
        out_specs=[pl.BlockSpec((QBLOCK, ATTN_WIDTH), cur), pl.BlockSpec((QBLOCK, 2 * HEAD_DIM), cur)],
        out_shape=[S((T, ATTN_WIDTH), bf16), S((T, 2 * HEAD_DIM), f32)],
        scratch_shapes=[pltpu.VMEM((N_Q_HEADS * QBLOCK, 2 * QBLOCK), f32), pltpu.VMEM((N_Q_HEADS * QBLOCK, 2 * QBLOCK), bf16)],
        args=(qn, kn, kn, vb, vb, bias, sinks), ride=ride)


def _layer_norm_stats(h1):
    mu = jnp.mean(h1, axis=-1, keepdims=True)
    xc = h1 - mu
    rstd = lax.rsqrt(jnp.mean(xc * xc, axis=-1, keepdims=True) + EPS)
    return xc * rstd, rstd


def _advanced_windows(win):
    rows = win.shape[0]
    for r in range(8):
        yield r, (win if r == 0 else pltpu.roll(win, rows - r, 0))


def _tap_offsets(r, rows):
    for q in range((rows - CONV_UNIT) // 8 + 1):
        if r == 0 or 8 * q + r + CONV_UNIT <= rows:
            yield q, 8 * q + r


def _conv_fwd(h0, w_dw, b_dw, ln_g, ln_b, tm, ride=None):
    T = h0.shape[0]
    per = tm // CONV_HALO
    lead = CONV_HALO - (CONV_WIDTH - 1)

    def body(hc_ref, hp_ref, w_ref, b_ref, g_ref, bb_ref, h1_ref, h3_ref, cat):
        i = pl.program_id(0)
        cat[0:CONV_HALO, :] = jnp.where(i == 0, 0.0, hp_ref[...])
        cat[CONV_HALO:, :] = hc_ref[...]

        def unit_rows(c, carry):
            r0 = pl.multiple_of(c * CONV_UNIT, CONV_UNIT)
            for j in range(D_MODEL // 128):
                ls = slice(j * 128, (j + 1) * 128)
                win = cat[pl.ds(r0, CONV_UNIT + CONV_HALO), ls]
                acc = jnp.zeros((CONV_UNIT, 128), f32) + b_ref[:, ls]
                for r, adv in _advanced_windows(win):
                    for q, off in _tap_offsets(r, CONV_UNIT + CONV_HALO):
                        k = off - lead
                        if 0 <= k < CONV_WIDTH:
                            acc = acc + adv[8 * q:8 * q + CONV_UNIT] * w_ref[k:k + 1, ls]
                h1_ref[pl.ds(r0, CONV_UNIT), ls] = acc
            return carry

        lax.fori_loop(0, tm // CONV_UNIT, unit_rows, 0)
        acc = h1_ref[...]
        xhat, _ = _layer_norm_stats(acc)
        h2 = xhat * g_ref[...] + bb_ref[...]
        h3_ref[...] = (h2 * _sigmoid(h2)).astype(bf16)

    vec = pl.BlockSpec((1, D_MODEL), lambda i: (0, 0))
    return _call(
        body, name="conv_fwd", grid=(T // tm,),
        in_specs=[pl.BlockSpec((tm, D_MODEL), lambda i: (i, 0)),
                  pl.BlockSpec((CONV_HALO, D_MODEL), lambda i: (jnp.maximum(i * per - 1, 0), 0)),
                  pl.BlockSpec((CONV_WIDTH, D_MODEL), lambda i: (0, 0)), vec, vec, vec],
        out_specs=[pl.BlockSpec((tm, D_MODEL), lambda i: (i, 0)), pl.BlockSpec((tm, D_MODEL), lambda i: (i, 0))],
        out_shape=[S((T, D_MODEL), f32), S((T, D_MODEL), bf16)],
        scratch_shapes=[pltpu.VMEM((tm + CONV_HALO, D_MODEL), f32)],
        args=(h0, h0, w_dw, b_dw, ln_g, ln_b), ride=ride)


def _mix_fwd(x, o, h3, proj, w_ao, w_co, w_o, tm):
    T = x.shape[0]
    row = pl.BlockSpec((tm, D_MODEL), lambda i: (i, 0))
    wsp = _resident((D_MODEL, D_MODEL))
    g0 = GLU_END // COL

    def gate_spec(off):
        return pl.BlockSpec((tm, COL), lambda i: (i, g0 + off))

    def body(x_ref, o_ref, h3_ref, ga0, ga1, gc0, gc1, wa_ref, wc_ref, wo_ref, x1_ref, at_ref, cv_ref, mg_ref):
        attn = jnp.dot(o_ref[...], wa_ref[...], preferred_element_type=f32)
        conv = jnp.dot(h3_ref[...], wc_ref[...], preferred_element_type=f32)
        ga = jnp.concatenate([ga0[...], ga1[...]], axis=-1).astype(f32)
        gc = jnp.concatenate([gc0[...], gc1[...]], axis=-1).astype(f32)
        merged = (_sigmoid(ga) * attn + _sigmoid(gc) * conv).astype(bf16)
        at_ref[...] = attn.astype(bf16)
        cv_ref[...] = conv.astype(bf16)
        mg_ref[...] = merged
        x1_ref[...] = x_ref[...] + jnp.dot(merged, wo_ref[...], preferred_element_type=f32)

    return pl.pallas_call(
        body, name="mix_fwd", grid=(T // tm,),
        in_specs=[row, row, row, gate_spec(0), gate_spec(1), gate_spec(2), gate_spec(3), wsp, wsp, wsp],
        out_specs=[row, row, row, row],
        out_shape=[S((T, D_MODEL), f32), S((T, D_MODEL), bf16), S((T, D_MODEL), bf16), S((T, D_MODEL), bf16)],
        compiler_params=_params("parallel"),
    )(x, o, h3, proj, proj, proj, proj, w_ao, w_co, w_o)


def _ffn_fwd(x1, g, w1, w2, target, tm):
    T = x1.shape[0]
    nj = w1.shape[0] // FF_CHUNK

    def body(x_ref, g_ref, w1_ref, w2_ref, t_ref, a_ref, u_ref, dy_ref, dyb_ref, ls_ref, hm):
        xv = x_ref[...]
        r = lax.rsqrt(jnp.mean(xv * xv, axis=-1, keepdims=True) + EPS)
        u = (xv * r * g_ref[...]).astype(bf16)
        u_ref[...] = u
        for j in range(nj):
            js = slice(j * FF_CHUNK, (j + 1) * FF_CHUNK)
            a = _nt(u, w1_ref[js, :])
            a_ref[:, js] = a.astype(bf16)
            hm[:, js] = jnp.square(jnp.maximum(a, 0.0)).astype(bf16)
        err = xv + jnp.dot(hm[...], w2_ref[...], preferred_element_type=f32) - t_ref[...]
        dy = err * (1.0 / D_MODEL)
        dy_ref[...] = dy
        dyb_ref[...] = dy.astype(bf16)
        ls_ref[...] = jnp.zeros((8, 128), f32) + jnp.sum(err * err) * (0.5 / D_MODEL)

    row = pl.BlockSpec((tm, D_MODEL), lambda i: (i, 0))
    wide = pl.BlockSpec((tm, D_FF), lambda i: (i, 0))
    return pl.pallas_call(
        body, name="ffn_fwd", grid=(T // tm,),
        in_specs=[row, _resident((1, D_MODEL)), _resident(w1.shape), _resident(w2.shape), row],
        out_specs=[wide, row, row, row, pl.BlockSpec((None, 8, 128), lambda i: (i, 0, 0))],
        out_shape=[S((T, D_FF), bf16), S((T, D_MODEL), bf16), S((T, D_MODEL), f32), S((T, D_MODEL), bf16),
                   S((T // tm, 8, 128), f32)],
        scratch_shapes=[pltpu.VMEM((tm, D_FF), bf16)],
        compiler_params=_params("parallel"),
    )(x1, g, w1, w2, target)


def _rms_bwd(du, xv, gv):
    r = lax.rsqrt(jnp.mean(xv * xv, axis=-1, keepdims=True) + EPS)
    xn = xv * r
    dg = jnp.sum(du * xn, axis=0, keepdims=True)
    dxn = du * gv
    dx = r * (dxn - xn * jnp.mean(dxn * xn, axis=-1, keepdims=True))
    return dx, dg


def _ffn_bwd(dy, dyb, a, x1, g, w1, w2, tm, ride=None):
    T = dy.shape[0]
    nj = w1.shape[0] // FF_CHUNK

    def body(dy_ref, dyb_ref, a_ref, x_ref, g_ref, w1_ref, w2_ref, da_ref, dx_ref, dxb_ref, dg_ref):
        @pl.when(pl.program_id(0) == 0)
        def _():
            dg_ref[...] = jnp.zeros_like(dg_ref)

        dyb_v = dyb_ref[...]
        for j in range(nj):
            js = slice(j * FF_CHUNK, (j + 1) * FF_CHUNK)
            dh = _nt(dyb_v, w2_ref[js, :])
            da_ref[:, js] = (dh * (2.0 * jnp.maximum(a_ref[:, js].astype(f32), 0.0))).astype(bf16)
        du = jnp.dot(da_ref[...], w1_ref[...], preferred_element_type=f32)
        dx, dg = _rms_bwd(du, x_ref[...], g_ref[...])
        dx1 = dy_ref[...] + dx
        dx_ref[...] = dx1
        dxb_ref[...] = dx1.astype(bf16)
        dg_ref[...] += dg

    row = pl.BlockSpec((tm, D_MODEL), lambda i: (i, 0))
    wide = pl.BlockSpec((tm, D_FF), lambda i: (i, 0))
    vec = pl.BlockSpec((1, D_MODEL), lambda i: (0, 0))
    return _call(
        body, name="ffn_bwd", grid=(T // tm,),
        in_specs=[row, row, wide, row, _resident((1, D_MODEL)), _resident(w1.shape), _resident(w2.shape)],
        out_specs=[wide, row, row, vec],
        out_shape=[S((T, D_FF), bf16), S((T, D_MODEL), f32), S((T, D_MODEL), bf16), S((1, D_MODEL), f32)],
        args=(dy, dyb, a, x1, g, w1, w2), ride=ride)


def _wgrad(name, a, b, tk, tn, tt, relu2=False, slab=None, out_dtype=bf16):
    T, Ka = a.shape
    Nb = b.shape[1]
    nt = T // tt

    def body(a_ref, b_ref, o_ref, acc):
        t = pl.program_id(2)
        av = a_ref[...]
        if relu2:
            av = jnp.square(jnp.maximum(av.astype(f32), 0.0))
        prod = _tn(av.astype(bf16), b_ref[...].astype(bf16))

        @pl.when(t == 0)
        def _():
            acc[...] = prod

        @pl.when(t > 0)
        def _():
            acc[...] += prod

        @pl.when(t == nt - 1)
        def _():
            if slab is None:
                o_ref[...] = acc[...].astype(out_dtype)
            else:
                for s in range(tn // slab):
                    o_ref[s] = acc[:, s * slab:(s + 1) * slab].astype(out_dtype)

    if slab is not None:
        out_shape = S((Nb // slab, Ka, slab), out_dtype)
        out_spec = pl.BlockSpec((tn // slab, tk, slab), lambda i, j, t: (j, i, 0))
    else:
        out_shape = S((Ka, Nb), out_dtype)
        out_spec = pl.BlockSpec((tk, tn), lambda i, j, t: (i, j))
    return pl.pallas_call(
        body, name=name, grid=(Ka // tk, Nb // tn, nt),
        in_specs=[pl.BlockSpec((tt, tk), lambda i, j, t: (t, i)), pl.BlockSpec((tt, tn), lambda i, j, t: (t, j))],
        out_specs=out_spec, out_shape=out_shape, scratch_shapes=[pltpu.VMEM((tk, tn), f32)],
        compiler_params=_params("parallel", "parallel", "arbitrary"),
    )(a, b)


def _mix_bwd(dx1, proj, attn, conv, h1, ln_g, ln_b, w_ao, w_co, w_o, tm, ride=None):
    T = dx1.shape[0]
    g0 = GLU_END // COL

    def gate_spec(off):
        return pl.BlockSpec((tm, COL), lambda i: (i, g0 + off))

    def body(dx_ref, ga0, ga1, gc0, gc1, at_ref, cv_ref, h_ref, g_ref, b_ref, wa_ref, wc_ref, wo_ref,
             da_ref, dc_ref, do_ref, dh1_ref, dg_ref, dlg_ref, dlb_ref, dbd_ref):
        @pl.when(pl.program_id(0) == 0)
        def _():
            dlg_ref[...] = jnp.zeros_like(dlg_ref)
            dlb_ref[...] = jnp.zeros_like(dlb_ref)
            dbd_ref[...] = jnp.zeros_like(dbd_ref)

        dm = _nt(dx_ref[...].astype(bf16), wo_ref[...])
        sa = _sigmoid(jnp.concatenate([ga0[...], ga1[...]], axis=-1).astype(f32))
        sc = _sigmoid(jnp.concatenate([gc0[...], gc1[...]], axis=-1).astype(f32))
        dattn = (dm * sa).astype(bf16)
        dconv = (dm * sc).astype(bf16)
        da_ref[...] = dattn
        dc_ref[...] = dconv
        dg_ref[:, 0:D_MODEL] = (dm * at_ref[...].astype(f32) * sa * (1.0 - sa)).astype(bf16)
        dg_ref[:, D_MODEL:2 * D_MODEL] = (dm * cv_ref[...].astype(f32) * sc * (1.0 - sc)).astype(bf16)
        do_ref[...] = _nt(dattn, wa_ref[...]).astype(bf16)
        dh3 = _nt(dconv, wc_ref[...])
        xhat, rstd = _layer_norm_stats(h_ref[...])
        h2 = xhat * g_ref[...] + b_ref[...]
        sg = _sigmoid(h2)
        dh2 = dh3 * (sg * (1.0 + h2 * (1.0 - sg)))
        dlg_ref[...] += jnp.sum(dh2 * xhat, axis=0, keepdims=True)
        dlb_ref[...] += jnp.sum(dh2, axis=0, keepdims=True)
        dxh = dh2 * g_ref[...]
        dh1 = rstd * (dxh - jnp.mean(dxh, axis=-1, keepdims=True) - xhat * jnp.mean(dxh * xhat, axis=-1, keepdims=True))
        dh1_ref[...] = dh1
        dbd_ref[...] += jnp.sum(dh1, axis=0, keepdims=True)

    row = pl.BlockSpec((tm, D_MODEL), lambda i: (i, 0))
    vec = pl.BlockSpec((1, D_MODEL), lambda i: (0, 0))
    par = _resident((1, D_MODEL))
    wsp = _resident((D_MODEL, D_MODEL))
    return _call(
        body, name="mix_bwd", grid=(T // tm,),
        in_specs=[row, gate_spec(0), gate_spec(1), gate_spec(2), gate_spec(3), row, row, row, par, par, wsp, wsp, wsp],
        out_specs=[row, row, row, row, pl.BlockSpec((tm, 2 * D_MODEL), lambda i: (i, 0)), vec, vec, vec],
        out_shape=[S((T, D_MODEL), bf16), S((T, D_MODEL), bf16), S((T, D_MODEL), bf16), S((T, D_MODEL), f32),
                   S((T, 2 * D_MODEL), bf16), S((1, D_MODEL), f32), S((1, D_MODEL), f32), S((1, D_MODEL), f32)],
        args=(dx1, proj, proj, proj, proj, attn, conv, h1, ln_g, ln_b, w_ao, w_co, w_o), ride=ride)


def _conv_bwd(dh1, h0, proj, w_dw, tm, ride=None):
    T = dh1.shape[0]
    per = tm // CONV_HALO
    nh = T // CONV_HALO
    nt = T // tm
    a0 = V_END // COL
    lead = CONV_HALO - (CONV_WIDTH - 1)

    def body(dc_ref, dn_ref, hc_ref, hp_ref, a0_ref, a1_ref, g0_ref, g1_ref, w_ref, dglu_ref, dw_ref, dcat, hcat, wacc, dh0):
        i = pl.program_id(0)

        @pl.when(i == 0)
        def _():
            wacc[...] = jnp.zeros_like(wacc)

        dcat[0:tm, :] = dc_ref[...]
        dcat[tm:, :] = jnp.where(i == nt - 1, 0.0, dn_ref[...])
        hcat[0:CONV_HALO, :] = jnp.where(i == 0, 0.0, hp_ref[...])
        hcat[CONV_HALO:, :] = hc_ref[...]
        span = CONV_UNIT + CONV_HALO

        def unit_rows(c, carry):
            r0 = pl.multiple_of(c * CONV_UNIT, CONV_UNIT)
            for j in range(D_MODEL // 128):
                ls = slice(j * 128, (j + 1) * 128)
                dwin = dcat[pl.ds(r0, span), ls]
                acc = jnp.zeros((CONV_UNIT, 128), f32)
                for r, adv in _advanced_windows(dwin):
                    for q, off in _tap_offsets(r, span):
                        k = CONV_WIDTH - 1 - off
                        if 0 <= k < CONV_WIDTH:
                            acc = acc + adv[8 * q:8 * q + CONV_UNIT] * w_ref[k:k + 1, ls]
                dh0[pl.ds(r0, CONV_UNIT), ls] = acc
                dcur = dwin[0:CONV_UNIT]
                for r, adv in _advanced_windows(hcat[pl.ds(r0, span), ls]):
                    for q, off in _tap_offsets(r, span):
                        k = off - lead
                        if 0 <= k < CONV_WIDTH:
                            prod = dcur * adv[8 * q:8 * q + CONV_UNIT]
                            wacc[k, :, ls] += jnp.sum(prod.reshape(CONV_UNIT // 8, 8, 128), axis=0)
            return carry

        lax.fori_loop(0, tm // CONV_UNIT, unit_rows, 0)
        dh0v = dh0[...]
        av = jnp.concatenate([a0_ref[...], a1_ref[...]], axis=-1).astype(f32)
        sg = _sigmoid(jnp.concatenate([g0_ref[...], g1_ref[...]], axis=-1).astype(f32))
        dglu_ref[:, 0:D_MODEL] = (dh0v * sg).astype(bf16)
        dglu_ref[:, D_MODEL:2 * D_MODEL] = (dh0v * av * sg * (1.0 - sg)).astype(bf16)

        @pl.when(i == nt - 1)
        def _():
            for k in range(CONV_WIDTH):
                dw_ref[k:k + 1, :] = jnp.sum(wacc[k], axis=0, keepdims=True)
            dw_ref[CONV_WIDTH:CONV_WIDTH + 1, :] = jnp.zeros((1, D_MODEL), f32)

    row = pl.BlockSpec((tm, D_MODEL), lambda i: (i, 0))

    def col_spec(off):
        return pl.BlockSpec((tm, COL), lambda i: (i, a0 + off))

    return _call(
        body, name="conv_bwd", grid=(nt,),
        in_specs=[row, pl.BlockSpec((CONV_HALO, D_MODEL), lambda i: (jnp.minimum((i + 1) * per, nh - 1), 0)),
                  row, pl.BlockSpec((CONV_HALO, D_MODEL), lambda i: (jnp.maximum(i * per - 1, 0), 0)),
                  col_spec(0), col_spec(1), col_spec(2), col_spec(3),
                  pl.BlockSpec((CONV_WIDTH, D_MODEL), lambda i: (0, 0))],
        out_specs=[pl.BlockSpec((tm, 2 * D_MODEL), lambda i: (i, 0)), pl.BlockSpec((CONV_WIDTH + 1, D_MODEL), lambda i: (0, 0))],
        out_shape=[S((T, 2 * D_MODEL), bf16), S((CONV_WIDTH + 1, D_MODEL), f32)],
        scratch_shapes=[pltpu.VMEM((tm + CONV_HALO, D_MODEL), f32), pltpu.VMEM((tm + CONV_HALO, D_MODEL), f32),
                        pltpu.VMEM((CONV_WIDTH, 8, D_MODEL), f32), pltpu.VMEM((tm, D_MODEL), f32)],
        args=(dh1, dh1, h0, h0, proj, proj, proj, proj, w_dw), ride=ride)


def _attn_bwd(qn, kn, vb, o, do, lse, bias, sinks, ride=None):
    T = qn.shape[0]
    nb = T // QBLOCK

    def body(q_ref, kc_ref, kp_ref, vc_ref, vp_ref, o_ref, do_ref, lse_ref, b_ref, s_ref,
             dq_ref, dk_ref, dv_ref, db_ref, dsk_ref, kcar, vcar, s_scr, dp_scr, p_scr, ds_scr):
        n = pl.program_id(0)

        @pl.when(n == 0)
        def _():
            db_ref[...] = jnp.zeros_like(db_ref)
            dsk_ref[...] = jnp.zeros_like(dsk_ref)
            kcar[...] = jnp.zeros_like(kcar)
            vcar[...] = jnp.zeros_like(vcar)

        @pl.when(n < nb)
        def _():
            lane = lax.broadcasted_iota(jnp.int32, (QBLOCK, 2 * HEAD_DIM), 1)
            lane_row = lax.broadcasted_iota(jnp.int32, (1, 2 * HEAD_DIM), 1)
            kx = _kv_placements(jnp.concatenate([kp_ref[...], kc_ref[...]], axis=0))
            vx = _kv_placements(jnp.concatenate([vp_ref[...], vc_ref[...]], axis=0))
            lse_tile = lse_ref[...]
            delta, lse_c = {}, {}
            for pr in range(N_Q_HEADS // 2):
                dop = do_ref[:, _pair_cols(pr)]
                dl = dop.astype(f32) * o_ref[:, _pair_cols(pr)].astype(f32)
                for side in range(2):
                    hq = 2 * pr + side
                    h = hq // GROUP
                    qm = _one_head(q_ref[:, _pair_cols(pr)], side)
                    s_scr[_head_rows(hq), :] = _nt(qm, kx[h, side]) + b_ref[_head_rows(hq), :]
                    dp_scr[_head_rows(hq), :] = _nt(_one_head(dop, side), vx[h, side])
                    delta[hq] = jnp.sum(_one_head(dl, side), axis=-1, keepdims=True)
                    lse_c[hq] = jnp.sum(jnp.where(lane == hq, lse_tile, 0.0), axis=-1, keepdims=True)
            dsk = jnp.zeros((1, 2 * HEAD_DIM), f32)
            for hq in range(N_Q_HEADS):
                p = jnp.exp(s_scr[_head_rows(hq), :] - lse_c[hq])
                ds = p * (dp_scr[_head_rows(hq), :] - delta[hq])
                db_ref[_head_rows(hq), :] += ds
                p_scr[_head_rows(hq), :] = p.astype(bf16)
                ds_scr[_head_rows(hq), :] = ds.astype(bf16)
                psink = jnp.exp(s_ref[0, hq] - lse_c[hq])
                dsk = dsk - jnp.where(lane_row == hq, jnp.sum(psink * delta[hq], axis=0, keepdims=True), 0.0)
            dsk_ref[...] += dsk
            for pr in range(N_Q_HEADS // 2):
                h = 2 * pr // GROUP
                dq_ref[:, _pair_cols(pr)] = (jnp.dot(ds_scr[_head_rows(2 * pr), :], kx[h, 0], preferred_element_type=f32)
                                             + jnp.dot(ds_scr[_head_rows(2 * pr + 1), :], kx[h, 1], preferred_element_type=f32))
            folded_k, folded_v = [], []
            for h in range(N_KV_HEADS):
                ka = jnp.zeros((2 * QBLOCK, 2 * HEAD_DIM), f32)
                va = jnp.zeros((2 * QBLOCK, 2 * HEAD_DIM), f32)
                for g in range(GROUP):
                    hq = h * GROUP + g
                    ka = ka + _tn(ds_scr[_head_rows(hq), :], _one_head(q_ref[:, _pair_cols(hq // 2)], hq % 2))
                    va = va + _tn(p_scr[_head_rows(hq), :], _one_head(do_ref[:, _pair_cols(hq // 2)], hq % 2))
                folded_k.append(ka + _swap_halves(ka))
                folded_v.append(va + _swap_halves(va))
            low = _low_lanes()
            for m in range(N_KV_HEADS // 2):
                cs = _pair_cols(m)
                for folded, out_ref, car in ((folded_k, dk_ref, kcar), (folded_v, dv_ref, vcar)):
                    band = jnp.where(low, folded[2 * m], folded[2 * m + 1])
                    out_ref[:, cs] = car[:, cs] + band[0:QBLOCK, :]
                    car[:, cs] = band[QBLOCK:, :]

        @pl.when(n == nb)
        def _():
            dk_ref[...] = kcar[...]
            dv_ref[...] = vcar[...]

    cur = lambda n: (jnp.minimum(n, nb - 1), 0)
    prev = lambda n: (jnp.clip(n - 1, 0, nb - 1), 0)
    qspec = pl.BlockSpec((QBLOCK, ATTN_WIDTH), cur)
    kcur, kprev = pl.BlockSpec((QBLOCK, KV_WIDTH), cur), pl.BlockSpec((QBLOCK, KV_WIDTH), prev)
    whole = lambda shape: pl.BlockSpec(shape, lambda n: (0,) * len(shape))
    scores = (N_Q_HEADS * QBLOCK, 2 * QBLOCK)
    return _call(
        body, name="attn_bwd", grid=(nb + 1,),
        in_specs=[qspec, kcur, kprev, kcur, kprev, qspec, qspec, pl.BlockSpec((QBLOCK, 2 * HEAD_DIM), cur), _bias_spec(), SMEM],
        out_specs=[qspec, kprev, kprev, whole(scores), whole((1, 2 * HEAD_DIM))],
        out_shape=[S((T, ATTN_WIDTH), f32), S((T, KV_WIDTH), f32), S((T, KV_WIDTH), f32), S(scores, f32),
                   S((1, 2 * HEAD_DIM), f32)],
        scratch_shapes=[pltpu.VMEM((QBLOCK, KV_WIDTH), f32), pltpu.VMEM((QBLOCK, KV_WIDTH), f32),
                        pltpu.VMEM(scores, f32), pltpu.VMEM(scores, f32), pltpu.VMEM(scores, bf16), pltpu.VMEM(scores, bf16)],
        args=(qn, kn, kn, vb, vb, o, do, lse, bias, sinks), ride=ride)


def _rel_bias_bwd(dbias, bucket):
    def body(d_ref, bk_ref, o_ref):
        b = bk_ref[...]
        for k in range(N_BUCKETS):
            mk = b == k
            for h in range(N_Q_HEADS):
                o_ref[k, h] = jnp.sum(jnp.where(mk, d_ref[h * QBLOCK:(h + 1) * QBLOCK, :], 0.0))

    return pl.pallas_call(body, name="rel_bias_bwd", out_shape=S((N_BUCKETS, N_Q_HEADS), f32), out_specs=SMEM)(dbias, bucket)


def _qk_norm_bwd(dq, dk, dv, proj, qg, kg, tm):
    T = dq.shape[0]
    scale = HEAD_DIM ** -0.5

    def pair_bwd(dy, x, gv):
        r = _pair_rstd(x, True)
        xn = x * r
        dxn = dy * gv
        dx = r * (dxn - xn * _pair_mean(dxn * xn, True))
        return dx, jnp.sum(dy * xn, axis=0, keepdims=True)

    def body(dq_ref, dk_ref, dv_ref, p_ref, qg_ref, kg_ref, out_ref, dqg_ref, dkg_ref):
        @pl.when(pl.program_id(0) == 0)
        def _():
            dqg_ref[...] = jnp.zeros_like(dqg_ref)
            dkg_ref[...] = jnp.zeros_like(dkg_ref)

        qgv, kgv = qg_ref[...], kg_ref[...]
        dqg = jnp.zeros((1, 2 * HEAD_DIM), f32)
        for pr in range(N_Q_HEADS // 2):
            dx, dg = pair_bwd(dq_ref[:, _pair_cols(pr)] * scale, p_ref[:, _pair_cols(pr)].astype(f32), qgv)
            out_ref[:, _pair_cols(pr)] = dx.astype(bf16)
            dqg = dqg + dg
        dkg = jnp.zeros((1, 2 * HEAD_DIM), f32)
        for pr in range(N_KV_HEADS // 2):
            ps = slice(Q_END + pr * 2 * HEAD_DIM, Q_END + (pr + 1) * 2 * HEAD_DIM)
            dx, dg = pair_bwd(dk_ref[:, _pair_cols(pr)], p_ref[:, ps].astype(f32), kgv)
            out_ref[:, ps] = dx.astype(bf16)
            dkg = dkg + dg
        out_ref[:, K_END:V_END] = dv_ref[...].astype(bf16)
        dqg_ref[...] += dqg
        dkg_ref[...] += dkg

    vec = pl.BlockSpec((1, 2 * HEAD_DIM), lambda i: (0, 0))
    return pl.pallas_call(
        body, name="qk_norm_bwd", grid=(T // tm,),
        in_specs=[pl.BlockSpec((tm, ATTN_WIDTH), lambda i: (i, 0)), pl.BlockSpec((tm, KV_WIDTH), lambda i: (i, 0)),
                  pl.BlockSpec((tm, KV_WIDTH), lambda i: (i, 0)), pl.BlockSpec((tm, V_END), lambda i: (i, 0)), vec, vec],
        out_specs=[pl.BlockSpec((tm, V_END), lambda i: (i, 0)), vec, vec],
        out_shape=[S((T, V_END), bf16), S((1, 2 * HEAD_DIM), f32), S((1, 2 * HEAD_DIM), f32)],
        compiler_params=_params("arbitrary"),
    )(dq, dk, dv, proj, qg, kg)


def _in_bwd(dqkv, dglu, dgates, w_in, x, g, dx1, tm, ride=None):
    T = x.shape[0]
    pieces = (dqkv, dglu, dgates)
    starts = [0, dqkv.shape[1], dqkv.shape[1] + dglu.shape[1]]

    def body(a0_ref, a1_ref, a2_ref, w_ref, x_ref, g_ref, d_ref, gx_ref, dg_ref):
        @pl.when(pl.program_id(0) == 0)
        def _():
            dg_ref[...] = jnp.zeros_like(dg_ref)

        du = jnp.zeros((tm, D_MODEL), f32)
        for a_ref, c0 in zip((a0_ref, a1_ref, a2_ref), starts):
            du = du + _nt(a_ref[...], w_ref[:, c0:c0 + a_ref.shape[1]])
        dx, dg = _rms_bwd(du, x_ref[...], g_ref[...])
        gx_ref[...] = d_ref[...] + dx
        dg_ref[...] += dg

    row = pl.BlockSpec((tm, D_MODEL), lambda i: (i, 0))
    return _call(
        body, name="in_bwd", grid=(T // tm,),
        in_specs=[pl.BlockSpec((tm, p.shape[1]), lambda i: (i, 0)) for p in pieces]
        + [_resident(w_in.shape), row, _resident((1, D_MODEL)), row],
        out_specs=[row, pl.BlockSpec((1, D_MODEL), lambda i: (0, 0))],
        out_shape=[S((T, D_MODEL), f32), S((1, D_MODEL), f32)],
        args=(dqkv, dglu, dgates, w_in, x, g, dx1), ride=ride)


def _adamw(name, parts, w, m, v, tr):
    _, R, C = w.shape
    bc1 = 1.0 - ADAM_B1 ** ADAM_STEP
    bc2 = 1.0 - ADAM_B2 ** ADAM_STEP

    def body(p_ref, w_ref, m_ref, v_ref, g_ref, d_ref, nm_ref, nv_ref):
        g = p_ref[0].astype(f32)
        for k in range(1, N_DEV):
            g = g + p_ref[k].astype(f32)
        nm = ADAM_B1 * m_ref[...] + (1.0 - ADAM_B1) * g
        nv = ADAM_B2 * v_ref[...] + (1.0 - ADAM_B2) * (g * g)
        g_ref[...] = g
        nm_ref[...] = nm
        nv_ref[...] = nv
        d_ref[...] = -ADAM_LR * ((nm / bc1) / (jnp.sqrt(nv / bc2) + ADAM_EPS) + ADAM_WD * w_ref[...])

    blk = pl.BlockSpec((None, tr, C), lambda i: (0, i, 0))
    return pl.pallas_call(
        body, name=name, grid=(R // tr,),
        in_specs=[pl.BlockSpec((N_DEV, tr, C), lambda i: (0, i, 0)), blk, blk, blk],
        out_specs=[blk, blk, blk, blk], out_shape=[S((1, R, C), f32)] * 4,
        compiler_params=_params("parallel"),
    )(parts, w, m, v)


def _tile(T, pref):
    return min(T, pref)


def _pad_rows(a, rows):
    return jnp.pad(a, ((0, rows - a.shape[0]), (0, 0)))


def kernel(x, norm_mix_g, w_in, q_norm_g, k_norm_g, attn_sinks, rel_bias, w_attn_o, w_dw, b_dw, conv_ln_g, conv_ln_b, w_conv_out, w_out, norm_mlp_g, w_ff1, w_ff2, loss_target, m_norm_mix_g, m_w_in, m_q_norm_g, m_k_norm_g, m_attn_sinks, m_rel_bias, m_w_attn_o, m_w_dw, m_b_dw, m_conv_ln_g, m_conv_ln_b, m_w_conv_out, m_w_out, m_norm_mlp_g, m_w_ff1, m_w_ff2, v_norm_mix_g, v_w_in, v_q_norm_g, v_k_norm_g, v_attn_sinks, v_rel_bias, v_w_attn_o, v_w_dw, v_b_dw, v_conv_ln_g, v_conv_ln_b, v_w_conv_out, v_w_out, v_norm_mlp_g, v_w_ff1, v_w_ff2):
    T = x.shape[1]
    xs = x[0]
    tgt = loss_target[0]
    in_shard = IN_WIDTH // N_DEV
    dw_rows = CONV_WIDTH + 1
    ch_shard = D_MODEL // N_DEV
    tb = _tile(T, 512)
    tt = _tile(T, 2048)
    bucket = jnp.asarray(_t5_bucket_table())

    g_in, g_dw = _exchange("gather_w_in", [w_in[0].astype(bf16), _pad_rows(w_dw[0], dw_rows)], gather=True, two_level=True)
    W_in = jnp.transpose(g_in, (1, 0, 2)).reshape(D_MODEL, IN_WIDTH)
    W_dw = jnp.transpose(g_dw, (1, 0, 2)).reshape(dw_rows, D_MODEL)[:CONV_WIDTH]

    mix_shards = _Gather([w_attn_o[0].astype(bf16), w_conv_out[0].astype(bf16), w_out[0].astype(bf16)])
    qg2, kg2 = jnp.tile(q_norm_g, (1, 2)), jnp.tile(k_norm_g, (1, 2))
    (proj, u, qn, kn, vb, h0), (g_ao, g_co, g_o) = _proj_fwd(xs, norm_mix_g, W_in, qg2, kg2, tb, ride=mix_shards)
    W_ao = g_ao.reshape(D_MODEL, D_MODEL)
    W_co = g_co.reshape(D_MODEL, D_MODEL)
    W_o = g_o.reshape(D_MODEL, D_MODEL)
    bias = _bias_table(rel_bias, bucket)
    (o, lse), (g_f1,) = _attn_fwd(qn, kn, vb, bias, attn_sinks, ride=_Gather([w_ff1[0].astype(bf16).T]))
    W_f1t = g_f1.reshape(D_FF, D_MODEL)
    (h1, h3), (g_f2,) = _conv_fwd(h0, W_dw, b_dw, conv_ln_g, conv_ln_b, tb, ride=_Gather([w_ff2[0].astype(bf16)]))
    x1, attn, conv, merged = _mix_fwd(xs, o, h3, proj, W_ao, W_co, W_o, tb)
    W_f2 = g_f2.reshape(D_FF, D_MODEL)
    a, u2, dy, dyb, loss_parts = _ffn_fwd(x1, norm_mlp_g, W_f1t, W_f2, tgt, tb)

    gw_f2 = _wgrad("wgrad_ff2", a, dyb, D_MODEL, D_MODEL, tt, relu2=True).reshape(N_DEV, FF_CHUNK, D_MODEL)
    (da, dx1, dx1b, d_norm_mlp_g), (l_f2,) = _ffn_bwd(dy, dyb, a, x1, norm_mlp_g, W_f1t, W_f2, tb,
                                                      ride=_Exchange([gw_f2], gather=False))
    gw_f1 = _wgrad("wgrad_ff1", u2, da, D_MODEL, 4 * FF_CHUNK, tt, slab=FF_CHUNK)
    gw_o = _wgrad("wgrad_out", merged, dx1b, D_MODEL, D_MODEL, tt).reshape(N_DEV, ch_shard, D_MODEL)
    (dattn, dconv, do, dh1, dgates, d_ln_g, d_ln_b, d_b_dw), _ = _mix_bwd(
        dx1b, proj, attn, conv, h1, conv_ln_g, conv_ln_b, W_ao, W_co, W_o, tb)
    gw_ao = _wgrad("wgrad_attn_o", o, dattn, D_MODEL, D_MODEL, tt).reshape(N_DEV, ch_shard, D_MODEL)
    gw_co = _wgrad("wgrad_conv_out", h3, dconv, D_MODEL, D_MODEL, tt).reshape(N_DEV, ch_shard, D_MODEL)
    (dglu, d_w_dw), (l_f1, l_o, l_ao, l_co) = _conv_bwd(dh1, h0, proj, W_dw, tb,
                                                        ride=_Exchange([gw_f1, gw_o, gw_ao, gw_co], gather=False))
    (dq, dk, dv, dbias, d_sinks), _ = _attn_bwd(qn, kn, vb, o, do, lse, bias, attn_sinks)
    d_sinks = d_sinks[:, :N_Q_HEADS]
    d_rel_bias = _rel_bias_bwd(dbias, bucket)
    dqkv, d_qg, d_kg = _qk_norm_bwd(dq, dk, dv, proj, qg2, kg2, tb)
    d_qg = d_qg[:, :HEAD_DIM] + d_qg[:, HEAD_DIM:]
    d_kg = d_kg[:, :HEAD_DIM] + d_kg[:, HEAD_DIM:]
    gw_in = jnp.concatenate([_wgrad("wgrad_in_qkv", u, dqkv, D_MODEL, V_END, tt),
                             _wgrad("wgrad_in_glu", u, dglu, D_MODEL, 2 * D_MODEL, tt),
                             _wgrad("wgrad_in_gates", u, dgates, D_MODEL, 2 * D_MODEL, tt)], axis=1)
    gw_in = jnp.transpose(gw_in.reshape(D_MODEL, N_DEV, in_shard), (1, 0, 2))
    gw_dw = jnp.transpose(d_w_dw.reshape(dw_rows, N_DEV, ch_shard), (1, 0, 2))
    (grad_x, d_norm_mix_g), (l_in, l_dw) = _in_bwd(dqkv, dglu, dgates, W_in, xs, norm_mix_g, dx1, tb,
                                                    ride=_Exchange([gw_in, gw_dw], gather=False))

    def row(vec):
        flat = vec.reshape(1, -1)
        return jnp.pad(flat, ((0, 0), (0, D_MODEL - flat.shape[1])))

    def pack_small(nm, qg, kg, sk, rb, bd, lg, lb, nl, extra=None):
        tail = jnp.concatenate([qg.reshape(1, -1), kg.reshape(1, -1), sk.reshape(1, -1), rb.reshape(1, -1)], axis=1)
        spare = jnp.zeros((1, D_MODEL), f32) if extra is None else row(extra)
        return jnp.concatenate([row(nm), row(bd), row(lg), row(lb), row(nl), row(tail), spare, jnp.zeros((1, D_MODEL), f32)], axis=0)

    def unpack_small(p):
        t = p[5]
        o0, o1, o2 = HEAD_DIM, 2 * HEAD_DIM, 2 * HEAD_DIM + N_Q_HEADS
        return dict(norm_mix_g=p[0:1], b_dw=p[1:2], conv_ln_g=p[2:3], conv_ln_b=p[3:4], norm_mlp_g=p[4:5],
                    q_norm_g=t[0:o0].reshape(1, HEAD_DIM), k_norm_g=t[o0:o1].reshape(1, HEAD_DIM),
                    attn_sinks=t[o1:o2].reshape(1, N_Q_HEADS),
                    rel_bias=t[o2:o2 + N_BUCKETS * N_Q_HEADS].reshape(N_BUCKETS, N_Q_HEADS))

    small_g = pack_small(d_norm_mix_g, d_qg, d_kg, d_sinks, d_rel_bias, d_b_dw, d_ln_g, d_ln_b, d_norm_mlp_g,
                         extra=jnp.sum(loss_parts[:, 0, 0]))
    (l_small,) = _exchange("gather_small_grads", [small_g], gather=True)


    res = {}
    res["w_in"] = _adamw("adamw_in", l_in, w_in, m_w_in, v_w_in, 256)
    res["w_attn_o"] = _adamw("adamw_attn_o", l_ao, w_attn_o, m_w_attn_o, v_w_attn_o, ch_shard)
    res["w_conv_out"] = _adamw("adamw_conv_out", l_co, w_conv_out, m_w_conv_out, v_w_conv_out, ch_shard)
    res["w_out"] = _adamw("adamw_out", l_o, w_out, m_w_out, v_w_out, ch_shard)
    res["w_ff1"] = _adamw("adamw_ff1", l_f1, w_ff1, m_w_ff1, v_w_ff1, 256)
    res["w_ff2"] = _adamw("adamw_ff2", l_f2, w_ff2, m_w_ff2, v_w_ff2, 256)
    pad_dw = lambda t: _pad_rows(t[0], dw_rows)[None]
    res["w_dw"] = [t[:, :CONV_WIDTH] for t in _adamw("adamw_dw", l_dw, pad_dw(w_dw), pad_dw(m_w_dw), pad_dw(v_w_dw), dw_rows)]
    small_w = pack_small(norm_mix_g, q_norm_g, k_norm_g, attn_sinks, rel_bias, b_dw, conv_ln_g, conv_ln_b, norm_mlp_g)
    small_m = pack_small(m_norm_mix_g, m_q_norm_g, m_k_norm_g, m_attn_sinks, m_rel_bias, m_b_dw, m_conv_ln_g, m_conv_ln_b, m_norm_mlp_g)
    small_v = pack_small(v_norm_mix_g, v_q_norm_g, v_k_norm_g, v_attn_sinks, v_rel_bias, v_b_dw, v_conv_ln_g, v_conv_ln_b, v_norm_mlp_g)
    small_out = _adamw("adamw_small", l_small, small_w[None], small_m[None], small_v[None], 8)
    small4 = [unpack_small(t[0]) for t in small_out]
    loss = small_out[0][0, 6, 0]

    order = ["norm_mix_g", "w_in", "q_norm_g", "k_norm_g", "attn_sinks", "rel_bias", "w_attn_o", "w_dw", "b_dw",
             "conv_ln_g", "conv_ln_b", "w_conv_out", "w_out", "norm_mlp_g", "w_ff1", "w_ff2"]
    stacked = {"w_in", "w_attn_o", "w_dw", "w_conv_out", "w_out", "w_ff1", "w_ff2"}
    outs = [loss, grad_x[None]]
    for k in range(4):
        for nme in order:
            if nme in stacked:
                outs.append(res[nme][k])
            else:
                outs.append(small4[k][nme])
    return tuple(outs)
```

```python
import functools

import numpy as np
import jax
import jax.numpy as jnp
from jax import lax
from jax.experimental import pallas as pl
from jax.experimental.pallas import tpu as pltpu

f32 = jnp.float32
bf16 = jnp.bfloat16
S = jax.ShapeDtypeStruct

N_DEV = 8
D_MODEL = 1024
HEAD_DIM = 64
N_Q_HEADS = 16
N_KV_HEADS = 4
GROUP = N_Q_HEADS // N_KV_HEADS
ATTN_WIDTH = N_Q_HEADS * HEAD_DIM
KV_WIDTH = N_KV_HEADS * HEAD_DIM
QBLOCK = 128
CONV_WIDTH = 31
CONV_HALO = 32
CONV_UNIT = 64
D_FF = 4 * D_MODEL
N_BUCKETS = 32
MAX_DISTANCE = 128
EPS = 1e-6
NEG = -1e30
Q_END = ATTN_WIDTH
K_END = Q_END + KV_WIDTH
V_END = K_END + KV_WIDTH
GLU_END = V_END + 2 * D_MODEL
IN_WIDTH = GLU_END + 2 * D_MODEL
COL = 512
FF_CHUNK = D_FF // N_DEV

ADAM_LR = 0.001
ADAM_B1 = 0.9
ADAM_B2 = 0.999
ADAM_EPS = 1e-08
ADAM_WD = 0.01
ADAM_STEP = 10

VMEM_LIMIT = 56 * 1024 * 1024

MESH_ID = pl.DeviceIdType.MESH
ANY = pl.BlockSpec(memory_space=pl.ANY)
SMEM = pl.BlockSpec(memory_space=pltpu.SMEM)


def _params(*sem):
    return pltpu.CompilerParams(dimension_semantics=sem, vmem_limit_bytes=VMEM_LIMIT)


def _nt(a, b):
    return lax.dot_general(a, b, (((1,), (1,)), ((), ())), preferred_element_type=f32)


def _tn(a, b):
    return lax.dot_general(a, b, (((0,), (0,)), ((), ())), preferred_element_type=f32)


def _sigmoid(z):
    return 1.0 / (1.0 + jnp.exp(-z))


def _t5_bucket_table():
    qi = np.arange(QBLOCK, dtype=np.int32)[:, None]
    kj = np.arange(2 * QBLOCK, dtype=np.int32)[None, :]
    dist = qi + QBLOCK - kj
    n = np.maximum(dist, 0)
    max_exact = N_BUCKETS // 2
    nf = np.maximum(n, 1).astype(np.float32)
    large = max_exact + (np.log(nf / np.float32(max_exact)) / np.float32(np.log(MAX_DISTANCE / max_exact))
                         * np.float32(N_BUCKETS - max_exact)).astype(np.int32)
    large = np.minimum(large, N_BUCKETS - 1)
    bucket = np.where(n < max_exact, n, large)
    valid = (dist >= 0) & (dist < QBLOCK)
    return np.where(valid, bucket, -1).astype(np.int32)


def _peer(d):
    x, y, c = lax.axis_index("x"), lax.axis_index("y"), lax.axis_index("c")
    dx, dy, dc = (d >> 2) & 1, (d >> 1) & 1, d & 1
    px, py, pc = x ^ dx, y ^ dy, c ^ dc
    return (px, py, pc), 4 * px + 2 * py + pc


class _Exchange:
    def __init__(self, arrays, gather):
        self.arrays, self.gather, self.n = list(arrays), gather, len(arrays)
        self.out_shape = [S(((N_DEV,) + a.shape) if gather else a.shape, a.dtype) for a in self.arrays]
        self.scratch = [pltpu.SemaphoreType.DMA((self.n, N_DEV - 1)), pltpu.SemaphoreType.DMA((self.n, N_DEV - 1)),
                        pltpu.SemaphoreType.DMA((self.n,))]

    def _copies(self, ins, outs, sems):
        send_sems, recv_sems, local_sems = sems
        _, me = _peer(0)
        local, sends, recvs = [], [], []
        for k in range(self.n):
            src = ins[k] if self.gather else ins[k].at[me]
            local.append(pltpu.make_async_copy(src, outs[k].at[me], local_sems.at[k]))
        for d in range(1, N_DEV):
            peer, pidx = _peer(d)
            for k in range(self.n):
                src = ins[k] if self.gather else ins[k].at[pidx]
                common = dict(src_ref=src, send_sem=send_sems.at[k, d - 1], recv_sem=recv_sems.at[k, d - 1],
                              device_id=peer, device_id_type=MESH_ID)
                sends.append(pltpu.make_async_remote_copy(dst_ref=outs[k].at[me], **common))
                recvs.append(pltpu.make_async_remote_copy(dst_ref=outs[k].at[pidx], **common))
        return local, sends, recvs

    def start(self, ins, outs, sems):
        local, sends, _ = self._copies(ins, outs, sems)
        for cp in local + sends:
            cp.start()

    def wait(self, ins, outs, sems):
        local, sends, recvs = self._copies(ins, outs, sems)
        for cp in recvs:
            cp.wait_recv()
        for cp in sends:
            cp.wait_send()
        for cp in local:
            cp.wait()


class _Gather:
    CHIPS = (4, 2, 6)
    SLOTS = 1 + 2 * len(CHIPS)

    def __init__(self, arrays):
        self.arrays, self.n = list(arrays), len(arrays)
        self.out_shape = [S((N_DEV,) + a.shape, a.dtype) for a in self.arrays]
        self.scratch = [pltpu.SemaphoreType.DMA((self.n, self.SLOTS)), pltpu.SemaphoreType.DMA((self.n, self.SLOTS)),
                        pltpu.SemaphoreType.DMA((self.n,))]

    @staticmethod
    def _copy(outs, sems, k, slot, src, block, to):
        return pltpu.make_async_remote_copy(src_ref=src, dst_ref=outs[k].at[block], send_sem=sems[0].at[k, slot],
                                            recv_sem=sems[1].at[k, slot], device_id=to, device_id_type=MESH_ID)

    def _local(self, ins, outs, sems):
        _, me = _peer(0)
        return [pltpu.make_async_copy(ins[k], outs[k].at[me], sems[2].at[k]) for k in range(self.n)]

    def start(self, ins, outs, sems):
        _, me = _peer(0)
        sibling, _ = _peer(1)
        for cp in self._local(ins, outs, sems):
            cp.start()
        for k in range(self.n):
            self._copy(outs, sems, k, 0, ins[k], me, sibling).start()
            for j, d in enumerate(self.CHIPS):
                self._copy(outs, sems, k, 1 + j, ins[k], me, _peer(d)[0]).start()

    def mid(self, ins, outs, sems):
        sibling, _ = _peer(1)
        for j, d in enumerate(self.CHIPS):
            chip, block = _peer(d)
            for k in range(self.n):
                self._copy(outs, sems, k, 1 + j, ins[k], block, chip).wait_recv()
                self._copy(outs, sems, k, 4 + j, outs[k].at[block], block, sibling).start()

    def wait(self, ins, outs, sems):
        _, me = _peer(0)
        sibling, sib_block = _peer(1)
        for k in range(self.n):
            self._copy(outs, sems, k, 0, ins[k], sib_block, sibling).wait_recv()
            for j, d in enumerate(self.CHIPS):
                self._copy(outs, sems, k, 4 + j, ins[k], _peer(d ^ 1)[1], sibling).wait_recv()
        for k in range(self.n):
            self._copy(outs, sems, k, 0, ins[k], me, sibling).wait_send()
            for j, d in enumerate(self.CHIPS):
                chip, block = _peer(d)
                self._copy(outs, sems, k, 1 + j, ins[k], me, chip).wait_send()
                self._copy(outs, sems, k, 4 + j, outs[k].at[block], block, sibling).wait_send()
        for cp in self._local(ins, outs, sems):
            cp.wait()


def _exchange(name, arrays, gather, two_level=False):
    ex = _Gather(arrays) if two_level else _Exchange(arrays, gather)
    n = ex.n

    def body(*refs):
        ins, outs, sems = refs[:n], refs[n:2 * n], refs[2 * n:]
        ex.start(ins, outs, sems)
        if two_level:
            ex.mid(ins, outs, sems)
        ex.wait(ins, outs, sems)

    return pl.pallas_call(body, name=name, out_shape=ex.out_shape, in_specs=[ANY] * n, out_specs=[ANY] * n,
                          scratch_shapes=ex.scratch)(*arrays)


def _call(body, *, name, grid, in_specs, out_specs, out_shape, args, scratch_shapes=(), ride=None):
    n_in, n_out, n_sc = len(in_specs), len(out_specs), len(scratch_shapes)
    sem = ("arbitrary",) * len(grid)
    if ride is None:
        res = pl.pallas_call(body, name=name, grid=grid, in_specs=list(in_specs), out_specs=list(out_specs),
                             out_shape=list(out_shape), scratch_shapes=list(scratch_shapes), compiler_params=_params(*sem))(*args)
        return list(res), []
    nx = ride.n

    def riding(*refs):
        ins, xin = refs[:n_in], refs[n_in:n_in + nx]
        outs, xout = refs[n_in + nx:n_in + nx + n_out], refs[n_in + nx + n_out:n_in + 2 * nx + n_out]
        rest = refs[n_in + 2 * nx + n_out:]
        scratch, sems = rest[:n_sc], rest[n_sc:]
        ids = [pl.program_id(ax) for ax in range(len(grid))]
        first = functools.reduce(jnp.logical_and, [i == 0 for i in ids])
        last = functools.reduce(jnp.logical_and, [i == g - 1 for i, g in zip(ids, grid)])

        @pl.when(first)
        def _():
            ride.start(xin, xout, sems)

        if hasattr(ride, "mid"):
            halfway = functools.reduce(jnp.logical_and, [ids[0] == 3 * grid[0] // 4] + [i == 0 for i in ids[1:]])

            @pl.when(halfway)
            def _():
                ride.mid(xin, xout, sems)

        body(*ins, *outs, *scratch)

        @pl.when(last)
        def _():
            ride.wait(xin, xout, sems)

    res = pl.pallas_call(
        riding, name=name, grid=grid, in_specs=list(in_specs) + [ANY] * nx, out_specs=list(out_specs) + [ANY] * nx,
        out_shape=list(out_shape) + ride.out_shape, scratch_shapes=list(scratch_shapes) + ride.scratch,
        compiler_params=_params(*sem))(*args, *ride.arrays)
    return list(res[:n_out]), list(res[n_out:])


def _resident(shape):
    return pl.BlockSpec(shape, lambda *_: (0,) * len(shape), pipeline_mode=pl.Buffered(1))


def _proj_fwd(x, g, w, qg, kg, tm, ride=None):
    T, K = x.shape
    N = w.shape[1]
    per = COL // (2 * HEAD_DIM)
    assert Q_END % COL == 0 and V_END == Q_END + COL and (GLU_END - V_END) == 4 * COL and KV_WIDTH == COL // 2

    def body(x_ref, g_ref, w_ref, qg_ref, kg_ref, o_ref, u_ref, qn_ref, kn_ref, vb_ref, h0_ref):
        xv = x_ref[...]
        r = lax.rsqrt(jnp.mean(xv * xv, axis=-1, keepdims=True) + EPS)
        u = (xv * r * g_ref[...]).astype(bf16)
        u_ref[...] = u

        def block(c):
            cs = slice(c * COL, (c + 1) * COL)
            pc = jnp.dot(u, w_ref[:, cs], preferred_element_type=f32)
            o_ref[:, cs] = pc.astype(bf16)
            return pc

        qgv = qg_ref[...] * (HEAD_DIM ** -0.5)
        for c in range(Q_END // COL):
            pc = block(c)
            for t in range(per):
                xq = pc[:, _pair_cols(t)]
                qn_ref[:, _pair_cols(c * per + t)] = (xq * _pair_rstd(xq, False) * qgv).astype(bf16)
        pc = block(Q_END // COL)
        for t in range(KV_WIDTH // (2 * HEAD_DIM)):
            xk = pc[:, _pair_cols(t)]
            kn_ref[:, _pair_cols(t)] = (xk * _pair_rstd(xk, False) * kg_ref[...]).astype(bf16)
        vb_ref[...] = pc[:, KV_WIDTH:].astype(bf16)
        a0 = V_END // COL
        for half in range(2):
            gate = block(a0 + 2 + half)
            h0_ref[:, half * COL:(half + 1) * COL] = block(a0 + half) * _sigmoid(gate)
        for c in range(GLU_END // COL, N // COL):
            block(c)

    row = lambda width: pl.BlockSpec((tm, width), lambda i: (i, 0))
    return _call(
        body, name="proj_fwd", grid=(T // tm,),
        in_specs=[row(K), _resident((1, K)), _resident((K, N)), _resident((1, 2 * HEAD_DIM)), _resident((1, 2 * HEAD_DIM))],
        out_specs=[row(N), row(K), row(ATTN_WIDTH), row(KV_WIDTH), row(KV_WIDTH), row(D_MODEL)],
        out_shape=[S((T, N), bf16), S((T, K), bf16), S((T, ATTN_WIDTH), bf16), S((T, KV_WIDTH), bf16), S((T, KV_WIDTH), bf16),
                   S((T, D_MODEL), f32)],
        args=(x, g, w, qg, kg), ride=ride)


def _bias_table(rel_bias, bucket):
    def body(rb_ref, bk_ref, o_ref):
        b = bk_ref[...]
        absent = lax.broadcasted_iota(jnp.int32, (QBLOCK, 2 * QBLOCK), 1) < QBLOCK
        for h in range(N_Q_HEADS):
            acc = jnp.full((QBLOCK, 2 * QBLOCK), NEG, f32)
            for k in range(N_BUCKETS):
                acc = jnp.where(b == k, rb_ref[k, h], acc)
            o_ref[0, h * QBLOCK:(h + 1) * QBLOCK, :] = acc
            o_ref[1, h * QBLOCK:(h + 1) * QBLOCK, :] = jnp.where(absent, NEG, acc)

    return pl.pallas_call(
        body, name="bias_table", out_shape=S((2, N_Q_HEADS * QBLOCK, 2 * QBLOCK), f32),
        in_specs=[SMEM, pl.BlockSpec(memory_space=pltpu.VMEM)],
    )(rel_bias, bucket)


def _bias_spec():
    return pl.BlockSpec((None, N_Q_HEADS * QBLOCK, 2 * QBLOCK), lambda n: (jnp.where(n == 0, 1, 0), 0, 0))


def _swap_halves(t):
    return jnp.concatenate([t[:, HEAD_DIM:], t[:, :HEAD_DIM]], axis=1)


def _low_lanes():
    return lax.broadcasted_iota(jnp.int32, (1, 2 * HEAD_DIM), 1) < HEAD_DIM


def _one_head(pair, side):
    zero = jnp.zeros((), pair.dtype)
    return jnp.where(_low_lanes(), pair, zero) if side == 0 else jnp.where(_low_lanes(), zero, pair)


def _pair_mean(t, on_mxu):
    if not on_mxu:
        m_lo = jnp.sum(_one_head(t, 0), axis=-1, keepdims=True) * (1.0 / HEAD_DIM)
        m_hi = jnp.sum(_one_head(t, 1), axis=-1, keepdims=True) * (1.0 / HEAD_DIM)
        return jnp.where(_low_lanes(), m_lo, m_hi)
    width = 2 * HEAD_DIM
    same_head = ((lax.broadcasted_iota(jnp.int32, (width, width), 0) < HEAD_DIM)
                 == (lax.broadcasted_iota(jnp.int32, (width, width), 1) < HEAD_DIM))
    e = jnp.where(same_head, 1.0 / HEAD_DIM, 0.0).astype(bf16)
    hi = t.astype(bf16)
    lo = (t - hi.astype(f32)).astype(bf16)
    return jnp.dot(hi, e, preferred_element_type=f32) + jnp.dot(lo, e, preferred_element_type=f32)


def _pair_rstd(x, on_mxu):
    return lax.rsqrt(_pair_mean(x * x, on_mxu) + EPS)


def _kv_placements(band):
    out = {}
    for m in range(N_KV_HEADS // 2):
        pair = band[:, m * 2 * HEAD_DIM:(m + 1) * 2 * HEAD_DIM]
        swapped = _swap_halves(pair)
        for hh in range(2):
            out[2 * m + hh, 0] = _one_head(pair if hh == 0 else swapped, 0)
            out[2 * m + hh, 1] = _one_head(swapped if hh == 0 else pair, 1)
    return out


def _head_rows(hq):
    return slice(hq * QBLOCK, (hq + 1) * QBLOCK)


def _pair_cols(pr):
    return slice(pr * 2 * HEAD_DIM, (pr + 1) * 2 * HEAD_DIM)


def _attn_fwd(qn, kn, vb, bias, sinks, ride=None):
    T = qn.shape[0]
    nb = T // QBLOCK

    def body(q_ref, kc_ref, kp_ref, vc_ref, vp_ref, b_ref, s_ref, o_ref, lse_ref, s_scr, p_scr):
        lane = lax.broadcasted_iota(jnp.int32, (QBLOCK, 2 * HEAD_DIM), 1)
        kx = _kv_placements(jnp.concatenate([kp_ref[...], kc_ref[...]], axis=0))
        vx = _kv_placements(jnp.concatenate([vp_ref[...], vc_ref[...]], axis=0))
        for hq in range(N_Q_HEADS):
            qm = _one_head(q_ref[:, _pair_cols(hq // 2)], hq % 2)
            s_scr[_head_rows(hq), :] = _nt(qm, kx[hq // GROUP, hq % 2]) + b_ref[_head_rows(hq), :]
        lse_tile = jnp.zeros((QBLOCK, 2 * HEAD_DIM), f32)
        for hq in range(N_Q_HEADS):
            s = s_scr[_head_rows(hq), :]
            sink = s_ref[0, hq]
            m = jnp.maximum(jnp.max(s, axis=-1, keepdims=True), sink)
            p = jnp.exp(s - m)
            l = jnp.sum(p, axis=-1, keepdims=True) + jnp.exp(sink - m)
            p_scr[_head_rows(hq), :] = (p * (1.0 / l)).astype(bf16)
            lse_tile = jnp.where(lane == hq, m + jnp.log(l), lse_tile)
        lse_ref[...] = lse_tile
        for pr in range(N_Q_HEADS // 2):
            h = 2 * pr // GROUP
            o_pair = (jnp.dot(p_scr[_head_rows(2 * pr), :], vx[h, 0], preferred_element_type=f32)
                      + jnp.dot(p_scr[_head_rows(2 * pr + 1), :], vx[h, 1], preferred_element_type=f32))
            o_ref[:, _pair_cols(pr)] = o_pair.astype(bf16)

    cur = lambda n: (n, 0)
    prev = lambda n: (jnp.maximum(n - 1, 0), 0)
    return _call(
        body, name="attn_fwd", grid=(nb,),
        in_specs=[pl.BlockSpec((QBLOCK, ATTN_WIDTH), cur), pl.BlockSpec((QBLOCK, KV_WIDTH), cur),
                  pl.BlockSpec((QBLOCK, KV_WIDTH), prev), pl.BlockSpec((QBLOCK, KV_WIDTH), cur),
                  pl.BlockSpec((QBLOCK, KV_WIDTH), prev), _bias_spec(), SMEM],
        out_specs=[pl.BlockSpec((QBLOCK, ATTN_WIDTH), cur), pl.BlockSpec((QBLOCK, 2 * HEAD_DIM), cur)],
        out_shape=[S((T, ATTN_WIDTH), bf16), S((T, 2 * HEAD_DIM), f32)],
        scratch_shapes=[pltpu.VMEM((N_Q_HEADS * QBLOCK, 2 * QBLOCK), f32), pltpu.VMEM((N_Q_HEADS * QBLOCK, 2 * QBLOCK), bf16)],
        args=(qn, kn, kn, vb, vb, bias, sinks), ride=ride)


def _layer_norm_stats(h1):
    mu = jnp.mean(h1, axis=-1, keepdims=True)
    xc = h1 - mu
    rstd = lax.rsqrt(jnp.mean(xc * xc, axis=-1, keepdims=True) + EPS)
    return xc * rstd, rstd


def _advanced_windows(win):
    rows = win.shape[0]
    for r in range(8):
        yield r, (win if r == 0 else pltpu.roll(win, rows - r, 0))


def _tap_offsets(r, rows):
    for q in range((rows - CONV_UNIT) // 8 + 1):
        if r == 0 or 8 * q + r + CONV_UNIT <= rows:
            yield q, 8 * q + r


def _conv_fwd(h0, w_dw, b_dw, ln_g, ln_b, tm, ride=None):
    T = h0.shape[0]
    per = tm // CONV_HALO
    lead = CONV_HALO - (CONV_WIDTH - 1)

    def body(hc_ref, hp_ref, w_ref, b_ref, g_ref, bb_ref, h1_ref, h3_ref, cat):
        i = pl.program_id(0)
        cat[0:CONV_HALO, :] = jnp.where(i == 0, 0.0, hp_ref[...])
        cat[CONV_HALO:, :] = hc_ref[...]

        def unit_rows(c, carry):
            r0 = pl.multiple_of(c * CONV_UNIT, CONV_UNIT)
            for j in range(D_MODEL // 128):
                ls = slice(j * 128, (j + 1) * 128)
                win = cat[pl.ds(r0, CONV_UNIT + CONV_HALO), ls]
                acc = jnp.zeros((CONV_UNIT, 128), f32) + b_ref[:, ls]
                for r, adv in _advanced_windows(win):
                    for q, off in _tap_offsets(r, CONV_UNIT + CONV_HALO):
                        k = off - lead
                        if 0 <= k < CONV_WIDTH:
                            acc = acc + adv[8 * q:8 * q + CONV_UNIT] * w_ref[k:k + 1, ls]
                h1_ref[pl.ds(r0, CONV_UNIT), ls] = acc
            return carry

        lax.fori_loop(0, tm // CONV_UNIT, unit_rows, 0)
        acc = h1_ref[...]
        xhat, _ = _layer_norm_stats(acc)
        h2 = xhat * g_ref[...] + bb_ref[...]
        h3_ref[...] = (h2 * _sigmoid(h2)).astype(bf16)

    vec = pl.BlockSpec((1, D_MODEL), lambda i: (0, 0))
    return _call(
        body, name="conv_fwd", grid=(T // tm,),
        in_specs=[pl.BlockSpec((tm, D_MODEL), lambda i: (i, 0)),
                  pl.BlockSpec((CONV_HALO, D_MODEL), lambda i: (jnp.maximum(i * per - 1, 0), 0)),
                  pl.BlockSpec((CONV_WIDTH, D_MODEL), lambda i: (0, 0)), vec, vec, vec],
        out_specs=[pl.BlockSpec((tm, D_MODEL), lambda i: (i, 0)), pl.BlockSpec((tm, D_MODEL), lambda i: (i, 0))],
        out_shape=[S((T, D_MODEL), f32), S((T, D_MODEL), bf16)],
        scratch_shapes=[pltpu.VMEM((tm + CONV_HALO, D_MODEL), f32)],
        args=(h0, h0, w_dw, b_dw, ln_g, ln_b), ride=ride)


def _mix_fwd(x, o, h3, proj, w_ao, w_co, w_o, tm):
    T = x.shape[0]
    row = pl.BlockSpec((tm, D_MODEL), lambda i: (i, 0))
    wsp = _resident((D_MODEL, D_MODEL))
    g0 = GLU_END // COL

    def gate_spec(off):
        return pl.BlockSpec((tm, COL), lambda i: (i, g0 + off))

    def body(x_ref, o_ref, h3_ref, ga0, ga1, gc0, gc1, wa_ref, wc_ref, wo_ref, x1_ref, at_ref, cv_ref, mg_ref):
        attn = jnp.dot(o_ref[...], wa_ref[...], preferred_element_type=f32)
        conv = jnp.dot(h3_ref[...], wc_ref[...], preferred_element_type=f32)
        ga = jnp.concatenate([ga0[...], ga1[...]], axis=-1).astype(f32)
        gc = jnp.concatenate([gc0[...], gc1[...]], axis=-1).astype(f32)
        merged = (_sigmoid(ga) * attn + _sigmoid(gc) * conv).astype(bf16)
        at_ref[...] = attn.astype(bf16)
        cv_ref[...] = conv.astype(bf16)
        mg_ref[...] = merged
        x1_ref[...] = x_ref[...] + jnp.dot(merged, wo_ref[...], preferred_element_type=f32)

    return pl.pallas_call(
        body, name="mix_fwd", grid=(T // tm,),
        in_specs=[row, row, row, gate_spec(0), gate_spec(1), gate_spec(2), gate_spec(3), wsp, wsp, wsp],
        out_specs=[row, row, row, row],
        out_shape=[S((T, D_MODEL), f32), S((T, D_MODEL), bf16), S((T, D_MODEL), bf16), S((T, D_MODEL), bf16)],
        compiler_params=_params("parallel"),
    )(x, o, h3, proj, proj, proj, proj, w_ao, w_co, w_o)


def _ffn_fwd(x1, g, w1, w2, target, tm):
    T = x1.shape[0]
    nj = w1.shape[0] // FF_CHUNK

    def body(x_ref, g_ref, w1_ref, w2_ref, t_ref, a_ref, u_ref, dy_ref, dyb_ref, ls_ref, hm):
        xv = x_ref[...]
        r = lax.rsqrt(jnp.mean(xv * xv, axis=-1, keepdims=True) + EPS)
        u = (xv * r * g_ref[...]).astype(bf16)
        u_ref[...] = u
        for j in range(nj):
            js = slice(j * FF_CHUNK, (j + 1) * FF_CHUNK)
            a = _nt(u, w1_ref[js, :])
            a_ref[:, js] = a.astype(bf16)
            hm[:, js] = jnp.square(jnp.maximum(a, 0.0)).astype(bf16)
        err = xv + jnp.dot(hm[...], w2_ref[...], preferred_element_type=f32) - t_ref[...]
        dy = err * (1.0 / D_MODEL)
        dy_ref[...] = dy
        dyb_ref[...] = dy.astype(bf16)
        ls_ref[...] = jnp.zeros((8, 128), f32) + jnp.sum(err * err) * (0.5 / D_MODEL)

    row = pl.BlockSpec((tm, D_MODEL), lambda i: (i, 0))
    wide = pl.BlockSpec((tm, D_FF), lambda i: (i, 0))
    return pl.pallas_call(
        body, name="ffn_fwd", grid=(T // tm,),
        in_specs=[row, _resident((1, D_MODEL)), _resident(w1.shape), _resident(w2.shape), row],
        out_specs=[wide, row, row, row, pl.BlockSpec((None, 8, 128), lambda i: (i, 0, 0))],
        out_shape=[S((T, D_FF), bf16), S((T, D_MODEL), bf16), S((T, D_MODEL), f32), S((T, D_MODEL), bf16),
                   S((T // tm, 8, 128), f32)],
        scratch_shapes=[pltpu.VMEM((tm, D_FF), bf16)],
        compiler_params=_params("parallel"),
    )(x1, g, w1, w2, target)


def _rms_bwd(du, xv, gv):
    r = lax.rsqrt(jnp.mean(xv * xv, axis=-1, keepdims=True) + EPS)
    xn = xv * r
    dg = jnp.sum(du * xn, axis=0, keepdims=True)
    dxn = du * gv
    dx = r * (dxn - xn * jnp.mean(dxn * xn, axis=-1, keepdims=True))
    return dx, dg


def _ffn_bwd(dy, dyb, a, x1, g, w1, w2, tm, ride=None):
    T = dy.shape[0]
    nj = w1.shape[0] // FF_CHUNK

    def body(dy_ref, dyb_ref, a_ref, x_ref, g_ref, w1_ref, w2_ref, da_ref, dx_ref, dxb_ref, dg_ref):
        @pl.when(pl.program_id(0) == 0)
        def _():
            dg_ref[...] = jnp.zeros_like(dg_ref)

        dyb_v = dyb_ref[...]
        for j in range(nj):
            js = slice(j * FF_CHUNK, (j + 1) * FF_CHUNK)
            dh = _nt(dyb_v, w2_ref[js, :])
            da_ref[:, js] = (dh * (2.0 * jnp.maximum(a_ref[:, js].astype(f32), 0.0))).astype(bf16)
        du = jnp.dot(da_ref[...], w1_ref[...], preferred_element_type=f32)
        dx, dg = _rms_bwd(du, x_ref[...], g_ref[...])
        dx1 = dy_ref[...] + dx
        dx_ref[...] = dx1
        dxb_ref[...] = dx1.astype(bf16)
        dg_ref[...] += dg

    row = pl.BlockSpec((tm, D_MODEL), lambda i: (i, 0))
    wide = pl.BlockSpec((tm, D_FF), lambda i: (i, 0))
    vec = pl.BlockSpec((1, D_MODEL), lambda i: (0, 0))
    return _call(
        body, name="ffn_bwd", grid=(T // tm,),
        in_specs=[row, row, wide, row, _resident((1, D_MODEL)), _resident(w1.shape), _resident(w2.shape)],
        out_specs=[wide, row, row, vec],
        out_shape=[S((T, D_FF), bf16), S((T, D_MODEL), f32), S((T, D_MODEL), bf16), S((1, D_MODEL), f32)],
        args=(dy, dyb, a, x1, g, w1, w2), ride=ride)


def _wgrad(name, a, b, tk, tn, tt, relu2=False, slab=None, out_dtype=bf16):
    T, Ka = a.shape
    Nb = b.shape[1]
    nt = T // tt

    def body(a_ref, b_ref, o_ref, acc):
        t = pl.program_id(2)
        av = a_ref[...]
        if relu2:
            av = jnp.square(jnp.maximum(av.astype(f32), 0.0))
        prod = _tn(av.astype(bf16), b_ref[...].astype(bf16))

        @pl.when(t == 0)
        def _():
            acc[...] = prod

        @pl.when(t > 0)
        def _():
            acc[...] += prod

        @pl.when(t == nt - 1)
        def _():
            if slab is None:
                o_ref[...] = acc[...].astype(out_dtype)
            else:
                for s in range(tn // slab):
                    o_ref[s] = acc[:, s * slab:(s + 1) * slab].astype(out_dtype)

    if slab is not None:
        out_shape = S((Nb // slab, Ka, slab), out_dtype)
        out_spec = pl.BlockSpec((tn // slab, tk, slab), lambda i, j, t: (j, i, 0))
    else:
        out_shape = S((Ka, Nb), out_dtype)
        out_spec = pl.BlockSpec((tk, tn), lambda i, j, t: (i, j))
    return pl.pallas_call(
        body, name=name, grid=(Ka // tk, Nb // tn, nt),
        in_specs=[pl.BlockSpec((tt, tk), lambda i, j, t: (t, i)), pl.BlockSpec((tt, tn), lambda i, j, t: (t, j))],
        out_specs=out_spec, out_shape=out_shape, scratch_shapes=[pltpu.VMEM((tk, tn), f32)],
        compiler_params=_params("parallel", "parallel", "arbitrary"),
    )(a, b)


def _mix_bwd(dx1, proj, attn, conv, h1, ln_g, ln_b, w_ao, w_co, w_o, tm, ride=None):
    T = dx1.shape[0]
    g0 = GLU_END // COL

    def gate_spec(off):
        return pl.BlockSpec((tm, COL), lambda i: (i, g0 + off))

    def body(dx_ref, ga0, ga1, gc0, gc1, at_ref, cv_ref, h_ref, g_ref, b_ref, wa_ref, wc_ref, wo_ref,
             da_ref, dc_ref, do_ref, dh1_ref, dg_ref, dlg_ref, dlb_ref, dbd_ref):
        @pl.when(pl.program_id(0) == 0)
        def _():
            dlg_ref[...] = jnp.zeros_like(dlg_ref)
            dlb_ref[...] = jnp.zeros_like(dlb_ref)
            dbd_ref[...] = jnp.zeros_like(dbd_ref)

        dm = _nt(dx_ref[...].astype(bf16), wo_ref[...])
        sa = _sigmoid(jnp.concatenate([ga0[...], ga1[...]], axis=-1).astype(f32))
        sc = _sigmoid(jnp.concatenate([gc0[...], gc1[...]], axis=-1).astype(f32))
        dattn = (dm * sa).astype(bf16)
        dconv = (dm * sc).astype(bf16)
        da_ref[...] = dattn
        dc_ref[...] = dconv
        dg_ref[:, 0:D_MODEL] = (dm * at_ref[...].astype(f32) * sa * (1.0 - sa)).astype(bf16)
        dg_ref[:, D_MODEL:2 * D_MODEL] = (dm * cv_ref[...].astype(f32) * sc * (1.0 - sc)).astype(bf16)
        do_ref[...] = _nt(dattn, wa_ref[...]).astype(bf16)
        dh3 = _nt(dconv, wc_ref[...])
        xhat, rstd = _layer_norm_stats(h_ref[...])
        h2 = xhat * g_ref[...] + b_ref[...]
        sg = _sigmoid(h2)
        dh2 = dh3 * (sg * (1.0 + h2 * (1.0 - sg)))
        dlg_ref[...] += jnp.sum(dh2 * xhat, axis=0, keepdims=True)
        dlb_ref[...] += jnp.sum(dh2, axis=0, keepdims=True)
        dxh = dh2 * g_ref[...]
        dh1 = rstd * (dxh - jnp.mean(dxh, axis=-1, keepdims=True) - xhat * jnp.mean(dxh * xhat, axis=-1, keepdims=True))
        dh1_ref[...] = dh1
        dbd_ref[...] += jnp.sum(dh1, axis=0, keepdims=True)

    row = pl.BlockSpec((tm, D_MODEL), lambda i: (i, 0))
    vec = pl.BlockSpec((1, D_MODEL), lambda i: (0, 0))
    par = _resident((1, D_MODEL))
    wsp = _resident((D_MODEL, D_MODEL))
    return _call(
        body, name="mix_bwd", grid=(T // tm,),
        in_specs=[row, gate_spec(0), gate_spec(1), gate_spec(2), gate_spec(3), row, row, row, par, par, wsp, wsp, wsp],
        out_specs=[row, row, row, row, pl.BlockSpec((tm, 2 * D_MODEL), lambda i: (i, 0)), vec, vec, vec],
        out_shape=[S((T, D_MODEL), bf16), S((T, D_MODEL), bf16), S((T, D_MODEL), bf16), S((T, D_MODEL), f32),
                   S((T, 2 * D_MODEL), bf16), S((1, D_MODEL), f32), S((1, D_MODEL), f32), S((1, D_MODEL), f32)],
        args=(dx1, proj, proj, proj, proj, attn, conv, h1, ln_g, ln_b, w_ao, w_co, w_o), ride=ride)


def _conv_bwd(dh1, h0, proj, w_dw, tm, ride=None):
    T = dh1.shape[0]
    per = tm // CONV_HALO
    nh = T // CONV_HALO
    nt = T // tm
    a0 = V_END // COL
    lead = CONV_HALO - (CONV_WIDTH - 1)

    def body(dc_ref, dn_ref, hc_ref, hp_ref, a0_ref, a1_ref, g0_ref, g1_ref, w_ref, dglu_ref, dw_ref, dcat, hcat, wacc, dh0):
        i = pl.program_id(0)

        @pl.when(i == 0)
        def _():
            wacc[...] = jnp.zeros_like(wacc)

        dcat[0:tm, :] = dc_ref[...]
        dcat[tm:, :] = jnp.where(i == nt - 1, 0.0, dn_ref[...])
        hcat[0:CONV_HALO, :] = jnp.where(i == 0, 0.0, hp_ref[...])
        hcat[CONV_HALO:, :] = hc_ref[...]
        span = CONV_UNIT + CONV_HALO

        def unit_rows(c, carry):
            r0 = pl.multiple_of(c * CONV_UNIT, CONV_UNIT)
            for j in range(D_MODEL // 128):
                ls = slice(j * 128, (j + 1) * 128)
                dwin = dcat[pl.ds(r0, span), ls]
                acc = jnp.zeros((CONV_UNIT, 128), f32)
                for r, adv in _advanced_windows(dwin):
                    for q, off in _tap_offsets(r, span):
                        k = CONV_WIDTH - 1 - off
                        if 0 <= k < CONV_WIDTH:
                            acc = acc + adv[8 * q:8 * q + CONV_UNIT] * w_ref[k:k + 1, ls]
                dh0[pl.ds(r0, CONV_UNIT), ls] = acc
                dcur = dwin[0:CONV_UNIT]
                for r, adv in _advanced_windows(hcat[pl.ds(r0, span), ls]):
                    for q, off in _tap_offsets(r, span):
                        k = off - lead
                        if 0 <= k < CONV_WIDTH:
                            prod = dcur * adv[8 * q:8 * q + CONV_UNIT]
                            wacc[k, :, ls] += jnp.sum(prod.reshape(CONV_UNIT // 8, 8, 128), axis=0)
            return carry

        lax.fori_loop(0, tm // CONV_UNIT, unit_rows, 0)
        dh0v = dh0[...]
        av = jnp.concatenate([a0_ref[...], a1_ref[...]], axis=-1).astype(f32)
        sg = _sigmoid(jnp.concatenate([g0_ref[...], g1_ref[...]], axis=-1).astype(f32))
        dglu_ref[:, 0:D_MODEL] = (dh0v * sg).astype(bf16)
        dglu_ref[:, D_MODEL:2 * D_MODEL] = (dh0v * av * sg * (1.0 - sg)).astype(bf16)

        @pl.when(i == nt - 1)
        def _():
            for k in range(CONV_WIDTH):
                dw_ref[k:k + 1, :] = jnp.sum(wacc[k], axis=0, keepdims=True)
            dw_ref[CONV_WIDTH:CONV_WIDTH + 1, :] = jnp.zeros((1, D_MODEL), f32)

    row = pl.BlockSpec((tm, D_MODEL), lambda i: (i, 0))

    def col_spec(off):
        return pl.BlockSpec((tm, COL), lambda i: (i, a0 + off))

    return _call(
        body, name="conv_bwd", grid=(nt,),
        in_specs=[row, pl.BlockSpec((CONV_HALO, D_MODEL), lambda i: (jnp.minimum((i + 1) * per, nh - 1), 0)),
                  row, pl.BlockSpec((CONV_HALO, D_MODEL), lambda i: (jnp.maximum(i * per - 1, 0), 0)),
                  col_spec(0), col_spec(1), col_spec(2), col_spec(3),
                  pl.BlockSpec((CONV_WIDTH, D_MODEL), lambda i: (0, 0))],
        out_specs=[pl.BlockSpec((tm, 2 * D_MODEL), lambda i: (i, 0)), pl.BlockSpec((CONV_WIDTH + 1, D_MODEL), lambda i: (0, 0))],
        out_shape=[S((T, 2 * D_MODEL), bf16), S((CONV_WIDTH + 1, D_MODEL), f32)],
        scratch_shapes=[pltpu.VMEM((tm + CONV_HALO, D_MODEL), f32), pltpu.VMEM((tm + CONV_HALO, D_MODEL), f32),
                        pltpu.VMEM((CONV_WIDTH, 8, D_MODEL), f32), pltpu.VMEM((tm, D_MODEL), f32)],
        args=(dh1, dh1, h0, h0, proj, proj, proj, proj, w_dw), ride=ride)


def _attn_bwd(qn, kn, vb, o, do, lse, bias, sinks, ride=None):
    T = qn.shape[0]
    nb = T // QBLOCK

    def body(q_ref, kc_ref, kp_ref, vc_ref, vp_ref, o_ref, do_ref, lse_ref, b_ref, s_ref,
             dq_ref, dk_ref, dv_ref, db_ref, dsk_ref, kcar, vcar, s_scr, dp_scr, p_scr, ds_scr):
        n = pl.program_id(0)

        @pl.when(n == 0)
        def _():
            db_ref[...] = jnp.zeros_like(db_ref)
            dsk_ref[...] = jnp.zeros_like(dsk_ref)
            kcar[...] = jnp.zeros_like(kcar)
            vcar[...] = jnp.zeros_like(vcar)

        @pl.when(n < nb)
        def _():
            lane = lax.broadcasted_iota(jnp.int32, (QBLOCK, 2 * HEAD_DIM), 1)
            lane_row = lax.broadcasted_iota(jnp.int32, (1, 2 * HEAD_DIM), 1)
            kx = _kv_placements(jnp.concatenate([kp_ref[...], kc_ref[...]], axis=0))
            vx = _kv_placements(jnp.concatenate([vp_ref[...], vc_ref[...]], axis=0))
            lse_tile = lse_ref[...]
            delta, lse_c = {}, {}
            for pr in range(N_Q_HEADS // 2):
                dop = do_ref[:, _pair_cols(pr)]
                dl = dop.astype(f32) * o_ref[:, _pair_cols(pr)].astype(f32)
                for side in range(2):
                    hq = 2 * pr + side
                    h = hq // GROUP
                    qm = _one_head(q_ref[:, _pair_cols(pr)], side)
                    s_scr[_head_rows(hq), :] = _nt(qm, kx[h, side]) + b_ref[_head_rows(hq), :]
                    dp_scr[_head_rows(hq), :] = _nt(_one_head(dop, side), vx[h, side])
                    delta[hq] = jnp.sum(_one_head(dl, side), axis=-1, keepdims=True)
                    lse_c[hq] = jnp.sum(jnp.where(lane == hq, lse_tile, 0.0), axis=-1, keepdims=True)
            dsk = jnp.zeros((1, 2 * HEAD_DIM), f32)
            for hq in range(N_Q_HEADS):
                p = jnp.exp(s_scr[_head_rows(hq), :] - lse_c[hq])
                ds = p * (dp_scr[_head_rows(hq), :] - delta[hq])
                db_ref[_head_rows(hq), :] += ds
                p_scr[_head_rows(hq), :] = p.astype(bf16)
                ds_scr[_head_rows(hq), :] = ds.astype(bf16)
                psink = jnp.exp(s_ref[0, hq] - lse_c[hq])
                dsk = dsk - jnp.where(lane_row == hq, jnp.sum(psink * delta[hq], axis=0, keepdims=True), 0.0)
            dsk_ref[...] += dsk
            for pr in range(N_Q_HEADS // 2):
                h = 2 * pr // GROUP
                dq_ref[:, _pair_cols(pr)] = (jnp.dot(ds_scr[_head_rows(2 * pr), :], kx[h, 0], preferred_element_type=f32)
                                             + jnp.dot(ds_scr[_head_rows(2 * pr + 1), :], kx[h, 1], preferred_element_type=f32))
            folded_k, folded_v = [], []
            for h in range(N_KV_HEADS):
                ka = jnp.zeros((2 * QBLOCK, 2 * HEAD_DIM), f32)
                va = jnp.zeros((2 * QBLOCK, 2 * HEAD_DIM), f32)
                for g in range(GROUP):
                    hq = h * GROUP + g
                    ka = ka + _tn(ds_scr[_head_rows(hq), :], _one_head(q_ref[:, _pair_cols(hq // 2)], hq % 2))
                    va = va + _tn(p_scr[_head_rows(hq), :], _one_head(do_ref[:, _pair_cols(hq // 2)], hq % 2))
                folded_k.append(ka + _swap_halves(ka))
                folded_v.append(va + _swap_halves(va))
            low = _low_lanes()
            for m in range(N_KV_HEADS // 2):
                cs = _pair_cols(m)
                for folded, out_ref, car in ((folded_k, dk_ref, kcar), (folded_v, dv_ref, vcar)):
                    band = jnp.where(low, folded[2 * m], folded[2 * m + 1])
                    out_ref[:, cs] = car[:, cs] + band[0:QBLOCK, :]
                    car[:, cs] = band[QBLOCK:, :]

        @pl.when(n == nb)
        def _():
            dk_ref[...] = kcar[...]
            dv_ref[...] = vcar[...]

    cur = lambda n: (jnp.minimum(n, nb - 1), 0)
    prev = lambda n: (jnp.clip(n - 1, 0, nb - 1), 0)
    qspec = pl.BlockSpec((QBLOCK, ATTN_WIDTH), cur)
    kcur, kprev = pl.BlockSpec((QBLOCK, KV_WIDTH), cur), pl.BlockSpec((QBLOCK, KV_WIDTH), prev)
    whole = lambda shape: pl.BlockSpec(shape, lambda n: (0,) * len(shape))
    scores = (N_Q_HEADS * QBLOCK, 2 * QBLOCK)
    return _call(
        body, name="attn_bwd", grid=(nb + 1,),
        in_specs=[qspec, kcur, kprev, kcur, kprev, qspec, qspec, pl.BlockSpec((QBLOCK, 2 * HEAD_DIM), cur), _bias_spec(), SMEM],
        out_specs=[qspec, kprev, kprev, whole(scores), whole((1, 2 * HEAD_DIM))],
        out_shape=[S((T, ATTN_WIDTH), f32), S((T, KV_WIDTH), f32), S((T, KV_WIDTH), f32), S(scores, f32),
                   S((1, 2 * HEAD_DIM), f32)],
        scratch_shapes=[pltpu.VMEM((QBLOCK, KV_WIDTH), f32), pltpu.VMEM((QBLOCK, KV_WIDTH), f32),
                        pltpu.VMEM(scores, f32), pltpu.VMEM(scores, f32), pltpu.VMEM(scores, bf16), pltpu.VMEM(scores, bf16)],
        args=(qn, kn, kn, vb, vb, o, do, lse, bias, sinks), ride=ride)


def _rel_bias_bwd(dbias, bucket):
    def body(d_ref, bk_ref, o_ref):
        b = bk_ref[...]
        for k in range(N_BUCKETS):
            mk = b == k
            for h in range(N_Q_HEADS):
                o_ref[k, h] = jnp.sum(jnp.where(mk, d_ref[h * QBLOCK:(h + 1) * QBLOCK, :], 0.0))

    return pl.pallas_call(body, name="rel_bias_bwd", out_shape=S((N_BUCKETS, N_Q_HEADS), f32), out_specs=SMEM)(dbias, bucket)


def _qk_norm_bwd(dq, dk, dv, proj, qg, kg, tm):
    T = dq.shape[0]
    scale = HEAD_DIM ** -0.5

    def pair_bwd(dy, x, gv):
        r = _pair_rstd(x, True)
        xn = x * r
        dxn = dy * gv
        dx = r * (dxn - xn * _pair_mean(dxn * xn, True))
        return dx, jnp.sum(dy * xn, axis=0, keepdims=True)

    def body(dq_ref, dk_ref, dv_ref, p_ref, qg_ref, kg_ref, out_ref, dqg_ref, dkg_ref):
        @pl.when(pl.program_id(0) == 0)
        def _():
            dqg_ref[...] = jnp.zeros_like(dqg_ref)
            dkg_ref[...] = jnp.zeros_like(dkg_ref)

        qgv, kgv = qg_ref[...], kg_ref[...]
        dqg = jnp.zeros((1, 2 * HEAD_DIM), f32)
        for pr in range(N_Q_HEADS // 2):
            dx, dg = pair_bwd(dq_ref[:, _pair_cols(pr)] * scale, p_ref[:, _pair_cols(pr)].astype(f32), qgv)
            out_ref[:, _pair_cols(pr)] = dx.astype(bf16)
            dqg = dqg + dg
        dkg = jnp.zeros((1, 2 * HEAD_DIM), f32)
        for pr in range(N_KV_HEADS // 2):
            ps = slice(Q_END + pr * 2 * HEAD_DIM, Q_END + (pr + 1) * 2 * HEAD_DIM)
            dx, dg = pair_bwd(dk_ref[:, _pair_cols(pr)], p_ref[:, ps].astype(f32), kgv)
            out_ref[:, ps] = dx.astype(bf16)
            dkg = dkg + dg
        out_ref[:, K_END:V_END] = dv_ref[...].astype(bf16)
        dqg_ref[...] += dqg
        dkg_ref[...] += dkg

    vec = pl.BlockSpec((1, 2 * HEAD_DIM), lambda i: (0, 0))
    return pl.pallas_call(
        body, name="qk_norm_bwd", grid=(T // tm,),
        in_specs=[pl.BlockSpec((tm, ATTN_WIDTH), lambda i: (i, 0)), pl.BlockSpec((tm, KV_WIDTH), lambda i: (i, 0)),
                  pl.BlockSpec((tm, KV_WIDTH), lambda i: (i, 0)), pl.BlockSpec((tm, V_END), lambda i: (i, 0)), vec, vec],
        out_specs=[pl.BlockSpec((tm, V_END), lambda i: (i, 0)), vec, vec],
        out_shape=[S((T, V_END), bf16), S((1, 2 * HEAD_DIM), f32), S((1, 2 * HEAD_DIM), f32)],
        compiler_params=_params("arbitrary"),
    )(dq, dk, dv, proj, qg, kg)


def _in_bwd(dqkv, dglu, dgates, w_in, x, g, dx1, tm, ride=None):
    T = x.shape[0]
    pieces = (dqkv, dglu, dgates)
    starts = [0, dqkv.shape[1], dqkv.shape[1] + dglu.shape[1]]

    def body(a0_ref, a1_ref, a2_ref, w_ref, x_ref, g_ref, d_ref, gx_ref, dg_ref):
        @pl.when(pl.program_id(0) == 0)
        def _():
            dg_ref[...] = jnp.zeros_like(dg_ref)

        du = jnp.zeros((tm, D_MODEL), f32)
        for a_ref, c0 in zip((a0_ref, a1_ref, a2_ref), starts):
            du = du + _nt(a_ref[...], w_ref[:, c0:c0 + a_ref.shape[1]])
        dx, dg = _rms_bwd(du, x_ref[...], g_ref[...])
        gx_ref[...] = d_ref[...] + dx
        dg_ref[...] += dg

    row = pl.BlockSpec((tm, D_MODEL), lambda i: (i, 0))
    return _call(
        body, name="in_bwd", grid=(T // tm,),
        in_specs=[pl.BlockSpec((tm, p.shape[1]), lambda i: (i, 0)) for p in pieces]
        + [_resident(w_in.shape), row, _resident((1, D_MODEL)), row],
        out_specs=[row, pl.BlockSpec((1, D_MODEL), lambda i: (0, 0))],
        out_shape=[S((T, D_MODEL), f32), S((1, D_MODEL), f32)],
        args=(dqkv, dglu, dgates, w_in, x, g, dx1), ride=ride)


def _adamw(name, parts, w, m, v, tr):
    _, R, C = w.shape
    bc1 = 1.0 - ADAM_B1 ** ADAM_STEP
    bc2 = 1.0 - ADAM_B2 ** ADAM_STEP

    def body(p_ref, w_ref, m_ref, v_ref, g_ref, d_ref, nm_ref, nv_ref):
        g = p_ref[0].astype(f32)
        for k in range(1, N_DEV):
            g = g + p_ref[k].astype(f32)
        nm = ADAM_B1 * m_ref[...] + (1.0 - ADAM_B1) * g
        nv = ADAM_B2 * v_ref[...] + (1.0 - ADAM_B2) * (g * g)
        g_ref[...] = g
        nm_ref[...] = nm
        nv_ref[...] = nv
        d_ref[...] = -ADAM_LR * ((nm / bc1) / (jnp.sqrt(nv / bc2) + ADAM_EPS) + ADAM_WD * w_ref[...])

    blk = pl.BlockSpec((None, tr, C), lambda i: (0, i, 0))
    return pl.pallas_call(
        body, name=name, grid=(R // tr,),
        in_specs=[pl.BlockSpec((N_DEV, tr, C), lambda i: (0, i, 0)), blk, blk, blk],
        out_specs=[blk, blk, blk, blk], out_shape=[S((1, R, C), f32)] * 4,
        compiler_params=_params("parallel"),
    )(parts, w, m, v)


def _tile(T, pref):
    return min(T, pref)


def _pad_rows(a, rows):
    return jnp.pad(a, ((0, rows - a.shape[0]), (0, 0)))


def kernel(x, norm_mix_g, w_in, q_norm_g, k_norm_g, attn_sinks, rel_bias, w_attn_o, w_dw, b_dw, conv_ln_g, conv_ln_b, w_conv_out, w_out, norm_mlp_g, w_ff1, w_ff2, loss_target, m_norm_mix_g, m_w_in, m_q_norm_g, m_k_norm_g, m_attn_sinks, m_rel_bias, m_w_attn_o, m_w_dw, m_b_dw, m_conv_ln_g, m_conv_ln_b, m_w_conv_out, m_w_out, m_norm_mlp_g, m_w_ff1, m_w_ff2, v_norm_mix_g, v_w_in, v_q_norm_g, v_k_norm_g, v_attn_sinks, v_rel_bias, v_w_attn_o, v_w_dw, v_b_dw, v_conv_ln_g, v_conv_ln_b, v_w_conv_out, v_w_out, v_norm_mlp_g, v_w_ff1, v_w_ff2):
    T = x.shape[1]
    xs = x[0]
    tgt = loss_target[0]
    in_shard = IN_WIDTH // N_DEV
    dw_rows = CONV_WIDTH + 1
    ch_shard = D_MODEL // N_DEV
    tb = _tile(T, 512)
    tt = _tile(T, 2048)
    bucket = jnp.asarray(_t5_bucket_table())

    g_in, g_dw = _exchange("gather_w_in", [w_in[0].astype(bf16), _pad_rows(w_dw[0], dw_rows)], gather=True, two_level=True)
    W_in = jnp.transpose(g_in, (1, 0, 2)).reshape(D_MODEL, IN_WIDTH)
    W_dw = jnp.transpose(g_dw, (1, 0, 2)).reshape(dw_rows, D_MODEL)[:CONV_WIDTH]

    mix_shards = _Gather([w_attn_o[0].astype(bf16), w_conv_out[0].astype(bf16), w_out[0].astype(bf16), w_ff2[0].astype(bf16)])
    qg2, kg2 = jnp.tile(q_norm_g, (1, 2)), jnp.tile(k_norm_g, (1, 2))
    (proj, u, qn, kn, vb, h0), (g_ao, g_co, g_o, g_f2) = _proj_fwd(xs, norm_mix_g, W_in, qg2, kg2, tb, ride=mix_shards)
    W_ao = g_ao.reshape(D_MODEL, D_MODEL)
    W_co = g_co.reshape(D_MODEL, D_MODEL)
    W_o = g_o.reshape(D_MODEL, D_MODEL)
    bias = _bias_table(rel_bias, bucket)
    (o, lse), (g_f1,) = _attn_fwd(qn, kn, vb, bias, attn_sinks, ride=_Gather([w_ff1[0].astype(bf16).T]))
    W_f1t = g_f1.reshape(D_FF, D_MODEL)
    (h1, h3), _ = _conv_fwd(h0, W_dw, b_dw, conv_ln_g, conv_ln_b, tb)
    x1, attn, conv, merged = _mix_fwd(xs, o, h3, proj, W_ao, W_co, W_o, tb)
    W_f2 = g_f2.reshape(D_FF, D_MODEL)
    a, u2, dy, dyb, loss_parts = _ffn_fwd(x1, norm_mlp_g, W_f1t, W_f2, tgt, tb)

    gw_f2 = _wgrad("wgrad_ff2", a, dyb, D_MODEL, D_MODEL, tt, relu2=True).reshape(N_DEV, FF_CHUNK, D_MODEL)
    (da, dx1, dx1b, d_norm_mlp_g), (l_f2,) = _ffn_bwd(dy, dyb, a, x1, norm_mlp_g, W_f1t, W_f2, tb,
                                                      ride=_Exchange([gw_f2], gather=False))
    gw_f1 = _wgrad("wgrad_ff1", u2, da, D_MODEL, 4 * FF_CHUNK, tt, slab=FF_CHUNK)
    gw_o = _wgrad("wgrad_out", merged, dx1b, D_MODEL, D_MODEL, tt).reshape(N_DEV, ch_shard, D_MODEL)
    (dattn, dconv, do, dh1, dgates, d_ln_g, d_ln_b, d_b_dw), _ = _mix_bwd(
        dx1b, proj, attn, conv, h1, conv_ln_g, conv_ln_b, W_ao, W_co, W_o, tb)
    gw_ao = _wgrad("wgrad_attn_o", o, dattn, D_MODEL, D_MODEL, tt).reshape(N_DEV, ch_shard, D_MODEL)
    gw_co = _wgrad("wgrad_conv_out", h3, dconv, D_MODEL, D_MODEL, tt).reshape(N_DEV, ch_shard, D_MODEL)
    (dglu, d_w_dw), (l_f1, l_o, l_ao, l_co) = _conv_bwd(dh1, h0, proj, W_dw, tb,
                                                        ride=_Exchange([gw_f1, gw_o, gw_ao, gw_co], gather=False))
    (dq, dk, dv, dbias, d_sinks), _ = _attn_bwd(qn, kn, vb, o, do, lse, bias, attn_sinks)
    d_sinks = d_sinks[:, :N_Q_HEADS]
    d_rel_bias = _rel_bias_bwd(dbias, bucket)
    dqkv, d_qg, d_kg = _qk_norm_bwd(dq, dk, dv, proj, qg2, kg2, tb)
    d_qg = d_qg[:, :HEAD_DIM] + d_qg[:, HEAD_DIM:]
    d_kg = d_kg[:, :HEAD_DIM] + d_kg[:, HEAD_DIM:]
    gw_in = jnp.concatenate([_wgrad("wgrad_in_qkv", u, dqkv, D_MODEL, V_END, tt),
                             _wgrad("wgrad_in_glu", u, dglu, D_MODEL, 2 * D_MODEL, tt),
                             _wgrad("wgrad_in_gates", u, dgates, D_MODEL, 2 * D_MODEL, tt)], axis=1)
    gw_in = jnp.transpose(gw_in.reshape(D_MODEL, N_DEV, in_shard), (1, 0, 2))
    gw_dw = jnp.transpose(d_w_dw.reshape(dw_rows, N_DEV, ch_shard), (1, 0, 2))
    (grad_x, d_norm_mix_g), (l_in, l_dw) = _in_bwd(dqkv, dglu, dgates, W_in, xs, norm_mix_g, dx1, tb,
                                                    ride=_Exchange([gw_in, gw_dw], gather=False))

    def row(vec):
        flat = vec.reshape(1, -1)
        return jnp.pad(flat, ((0, 0), (0, D_MODEL - flat.shape[1])))

    def pack_small(nm, qg, kg, sk, rb, bd, lg, lb, nl, extra=None):
        tail = jnp.concatenate([qg.reshape(1, -1), kg.reshape(1, -1), sk.reshape(1, -1), rb.reshape(1, -1)], axis=1)
        spare = jnp.zeros((1, D_MODEL), f32) if extra is None else row(extra)
        return jnp.concatenate([row(nm), row(bd), row(lg), row(lb), row(nl), row(tail), spare, jnp.zeros((1, D_MODEL), f32)], axis=0)

    def unpack_small(p):
        t = p[5]
        o0, o1, o2 = HEAD_DIM, 2 * HEAD_DIM, 2 * HEAD_DIM + N_Q_HEADS
        return dict(norm_mix_g=p[0:1], b_dw=p[1:2], conv_ln_g=p[2:3], conv_ln_b=p[3:4], norm_mlp_g=p[4:5],
                    q_norm_g=t[0:o0].reshape(1, HEAD_DIM), k_norm_g=t[o0:o1].reshape(1, HEAD_DIM),
                    attn_sinks=t[o1:o2].reshape(1, N_Q_HEADS),
                    rel_bias=t[o2:o2 + N_BUCKETS * N_Q_HEADS].reshape(N_BUCKETS, N_Q_HEADS))

    small_g = pack_small(d_norm_mix_g, d_qg, d_kg, d_sinks, d_rel_bias, d_b_dw, d_ln_g, d_ln_b, d_norm_mlp_g,
                         extra=jnp.sum(loss_parts[:, 0, 0]))
    (l_small,) = _exchange("gather_small_grads", [small_g], gather=True)


    res = {}
    res["w_in"] = _adamw("adamw_in", l_in, w_in, m_w_in, v_w_in, 256)
    res["w_attn_o"] = _adamw("adamw_attn_o", l_ao, w_attn_o, m_w_attn_o, v_w_attn_o, ch_shard)
    res["w_conv_out"] = _adamw("adamw_conv_out", l_co, w_conv_out, m_w_conv_out, v_w_conv_out, ch_shard)
    res["w_out"] = _adamw("adamw_out", l_o, w_out, m_w_out, v_w_out, ch_shard)
    res["w_ff1"] = _adamw("adamw_ff1", l_f1, w_ff1, m_w_ff1, v_w_ff1, 256)
    res["w_ff2"] = _adamw("adamw_ff2", l_f2, w_ff2, m_w_ff2, v_w_ff2, 256)
    pad_dw = lambda t: _pad_rows(t[0], dw_rows)[None]
    res["w_dw"] = [t[:, :CONV_WIDTH] for t in _adamw("adamw_dw", l_dw, pad_dw(w_dw), pad_dw(m_w_dw), pad_dw(v_w_dw), dw_rows)]
    small_w = pack_small(norm_mix_g, q_norm_g, k_norm_g, attn_sinks, rel_bias, b_dw, conv_ln_g, conv_ln_b, norm_mlp_g)
    small_m = pack_small(m_norm_mix_g, m_q_norm_g, m_k_norm_g, m_attn_sinks, m_rel_bias, m_b_dw, m_conv_ln_g, m_conv_ln_b, m_norm_mlp_g)
    small_v = pack_small(v_norm_mix_g, v_q_norm_g, v_k_norm_g, v_attn_sinks, v_rel_bias, v_b_dw, v_conv_ln_g, v_conv_ln_b, v_norm_mlp_g)
    small_out = _adamw("adamw_small", l_small, small_w[None], small_m[None], small_v[None], 8)
    small4 = [unpack_small(t[0]) for t in small_out]
    loss = small_out[0][0, 6, 0]

    order = ["norm_mix_g", "w_in", "q_norm_g", "k_norm_g", "attn_sinks", "rel_bias", "w_attn_o", "w_dw", "b_dw",
             "conv_ln_g", "conv_ln_b", "w_conv_out", "w_out", "norm_mlp_g", "w_ff1", "w_ff2"]
    stacked = {"w_in", "w_attn_o", "w_dw", "w_conv_out", "w_out", "w_ff1", "w_ff2"}
    outs = [loss, grad_x[None]]
    for k in range(4):
        for nme in order:
            if nme in stacked:
                outs.append(res[nme][k])
            else:
                outs.append(small4[k][nme])
    return tuple(outs)
```

```python
import functools

import numpy as np
import jax
import jax.numpy as jnp
from jax import lax
from jax.experimental import pallas as pl
from jax.experimental.pallas import tpu as pltpu

f32 = jnp.float32
bf16 = jnp.bfloat16
S = jax.ShapeDtypeStruct

N_DEV = 8
D_MODEL = 1024
HEAD_DIM = 64
N_Q_HEADS = 16
N_KV_HEADS = 4
GROUP = N_Q_HEADS // N_KV_HEADS
ATTN_WIDTH = N_Q_HEADS * HEAD_DIM
KV_WIDTH = N_KV_HEADS * HEAD_DIM
QBLOCK = 128
CONV_WIDTH = 31
CONV_HALO = 32
CONV_UNIT = 64
D_FF = 4 * D_MODEL
N_BUCKETS = 32
MAX_DISTANCE = 128
EPS = 1e-6
NEG = -1e30
Q_END = ATTN_WIDTH
K_END = Q_END + KV_WIDTH
V_END = K_END + KV_WIDTH
GLU_END = V_END + 2 * D_MODEL
IN_WIDTH = GLU_END + 2 * D_MODEL
COL = 512
FF_CHUNK = D_FF // N_DEV

ADAM_LR = 0.001
ADAM_B1 = 0.9
ADAM_B2 = 0.999
ADAM_EPS = 1e-08
ADAM_WD = 0.01
ADAM_STEP = 10

VMEM_LIMIT = 56 * 1024 * 1024

MESH_ID = pl.DeviceIdType.MESH
ANY = pl.BlockSpec(memory_space=pl.ANY)
SMEM = pl.BlockSpec(memory_space=pltpu.SMEM)


def _params(*sem):
    return pltpu.CompilerParams(dimension_semantics=sem, vmem_limit_bytes=VMEM_LIMIT)


def _nt(a, b):
    return lax.dot_general(a, b, (((1,), (1,)), ((), ())), preferred_element_type=f32)


def _tn(a, b):
    return lax.dot_general(a, b, (((0,), (0,)), ((), ())), preferred_element_type=f32)


def _sigmoid(z):
    return 1.0 / (1.0 + jnp.exp(-z))


def _t5_bucket_table():
    qi = np.arange(QBLOCK, dtype=np.int32)[:, None]
    kj = np.arange(2 * QBLOCK, dtype=np.int32)[None, :]
    dist = qi + QBLOCK - kj
    n = np.maximum(dist, 0)
    max_exact = N_BUCKETS // 2
    nf = np.maximum(n, 1).astype(np.float32)
    large = max_exact + (np.log(nf / np.float32(max_exact)) / np.float32(np.log(MAX_DISTANCE / max_exact))
                         * np.float32(N_BUCKETS - max_exact)).astype(np.int32)
    large = np.minimum(large, N_BUCKETS - 1)
    bucket = np.where(n < max_exact, n, large)
    valid = (dist >= 0) & (dist < QBLOCK)
    return np.where(valid, bucket, -1).astype(np.int32)


def _peer(d):
    x, y, c = lax.axis_index("x"), lax.axis_index("y"), lax.axis_index("c")
    dx, dy, dc = (d >> 2) & 1, (d >> 1) & 1, d & 1
    px, py, pc = x ^ dx, y ^ dy, c ^ dc
    return (px, py, pc), 4 * px + 2 * py + pc


class _Exchange:
    def __init__(self, arrays, gather):
        self.arrays, self.gather, self.n = list(arrays), gather, len(arrays)
        self.out_shape = [S(((N_DEV,) + a.shape) if gather else a.shape, a.dtype) for a in self.arrays]
        self.scratch = [pltpu.SemaphoreType.DMA((self.n, N_DEV - 1)), pltpu.SemaphoreType.DMA((self.n, N_DEV - 1)),
                        pltpu.SemaphoreType.DMA((self.n,))]

    def _copies(self, ins, outs, sems):
        send_sems, recv_sems, local_sems = sems
        _, me = _peer(0)
        local, sends, recvs = [], [], []
        for k in range(self.n):
            src = ins[k] if self.gather else ins[k].at[me]
            local.append(pltpu.make_async_copy(src, outs[k].at[me], local_sems.at[k]))
        for d in range(1, N_DEV):
            peer, pidx = _peer(d)
            for k in range(self.n):
                src = ins[k] if self.gather else ins[k].at[pidx]
                common = dict(src_ref=src, send_sem=send_sems.at[k, d - 1], recv_sem=recv_sems.at[k, d - 1],
                              device_id=peer, device_id_type=MESH_ID)
                sends.append(pltpu.make_async_remote_copy(dst_ref=outs[k].at[me], **common))
                recvs.append(pltpu.make_async_remote_copy(dst_ref=outs[k].at[pidx], **common))
        return local, sends, recvs

    def start(self, ins, outs, sems):
        local, sends, _ = self._copies(ins, outs, sems)
        for cp in local + sends:
            cp.start()

    def wait(self, ins, outs, sems):
        local, sends, recvs = self._copies(ins, outs, sems)
        for cp in recvs:
            cp.wait_recv()
        for cp in sends:
            cp.wait_send()
        for cp in local:
            cp.wait()


class _Gather:
    CHIPS = (4, 2, 6)
    SLOTS = 1 + 2 * len(CHIPS)

    def __init__(self, arrays):
        self.arrays, self.n = list(arrays), len(arrays)
        self.out_shape = [S((N_DEV,) + a.shape, a.dtype) for a in self.arrays]
        self.scratch = [pltpu.SemaphoreType.DMA((self.n, self.SLOTS)), pltpu.SemaphoreType.DMA((self.n, self.SLOTS)),
                        pltpu.SemaphoreType.DMA((self.n,))]

    @staticmethod
    def _copy(outs, sems, k, slot, src, block, to):
        return pltpu.make_async_remote_copy(src_ref=src, dst_ref=outs[k].at[block], send_sem=sems[0].at[k, slot],
                                            recv_sem=sems[1].at[k, slot], device_id=to, device_id_type=MESH_ID)

    def _local(self, ins, outs, sems):
        _, me = _peer(0)
        return [pltpu.make_async_copy(ins[k], outs[k].at[me], sems[2].at[k]) for k in range(self.n)]

    def start(self, ins, outs, sems):
        _, me = _peer(0)
        sibling, _ = _peer(1)
        for cp in self._local(ins, outs, sems):
            cp.start()
        for k in range(self.n):
            self._copy(outs, sems, k, 0, ins[k], me, sibling).start()
            for j, d in enumerate(self.CHIPS):
                self._copy(outs, sems, k, 1 + j, ins[k], me, _peer(d)[0]).start()

    def mid(self, ins, outs, sems):
        sibling, _ = _peer(1)
        for j, d in enumerate(self.CHIPS):
            chip, block = _peer(d)
            for k in range(self.n):
                self._copy(outs, sems, k, 1 + j, ins[k], block, chip).wait_recv()
                self._copy(outs, sems, k, 4 + j, outs[k].at[block], block, sibling).start()

    def wait(self, ins, outs, sems):
        _, me = _peer(0)
        sibling, sib_block = _peer(1)
        for k in range(self.n):
            self._copy(outs, sems, k, 0, ins[k], sib_block, sibling).wait_recv()
            for j, d in enumerate(self.CHIPS):
                self._copy(outs, sems, k, 4 + j, ins[k], _peer(d ^ 1)[1], sibling).wait_recv()
        for k in range(self.n):
            self._copy(outs, sems, k, 0, ins[k], me, sibling).wait_send()
            for j, d in enumerate(self.CHIPS):
                chip, block = _peer(d)
                self._copy(outs, sems, k, 1 + j, ins[k], me, chip).wait_send()
                self._copy(outs, sems, k, 4 + j, outs[k].at[block], block, sibling).wait_send()
        for cp in self._local(ins, outs, sems):
            cp.wait()


def _exchange(name, arrays, gather, two_level=False):
    ex = _Gather(arrays) if two_level else _Exchange(arrays, gather)
    n = ex.n

    def body(*refs):
        ins, outs, sems = refs[:n], refs[n:2 * n], refs[2 * n:]
        ex.start(ins, outs, sems)
        if two_level:
            ex.mid(ins, outs, sems)
        ex.wait(ins, outs, sems)

    return pl.pallas_call(body, name=name, out_shape=ex.out_shape, in_specs=[ANY] * n, out_specs=[ANY] * n,
                          scratch_shapes=ex.scratch)(*arrays)


def _call(body, *, name, grid, in_specs, out_specs, out_shape, args, scratch_shapes=(), ride=None):
    n_in, n_out, n_sc = len(in_specs), len(out_specs), len(scratch_shapes)
    sem = ("arbitrary",) * len(grid)
    if ride is None:
        res = pl.pallas_call(body, name=name, grid=grid, in_specs=list(in_specs), out_specs=list(out_specs),
                             out_shape=list(out_shape), scratch_shapes=list(scratch_shapes), compiler_params=_params(*sem))(*args)
        return list(res), []
    nx = ride.n

    def riding(*refs):
        ins, xin = refs[:n_in], refs[n_in:n_in + nx]
        outs, xout = refs[n_in + nx:n_in + nx + n_out], refs[n_in + nx + n_out:n_in + 2 * nx + n_out]
        rest = refs[n_in + 2 * nx + n_out:]
        scratch, sems = rest[:n_sc], rest[n_sc:]
        ids = [pl.program_id(ax) for ax in range(len(grid))]
        first = functools.reduce(jnp.logical_and, [i == 0 for i in ids])
        last = functools.reduce(jnp.logical_and, [i == g - 1 for i, g in zip(ids, grid)])

        @pl.when(first)
        def _():
            ride.start(xin, xout, sems)

        if hasattr(ride, "mid"):
            halfway = functools.reduce(jnp.logical_and, [ids[0] == 3 * grid[0] // 4] + [i == 0 for i in ids[1:]])

            @pl.when(halfway)
            def _():
                ride.mid(xin, xout, sems)

        body(*ins, *outs, *scratch)

        @pl.when(last)
        def _():
            ride.wait(xin, xout, sems)

    res = pl.pallas_call(
        riding, name=name, grid=grid, in_specs=list(in_specs) + [ANY] * nx, out_specs=list(out_specs) + [ANY] * nx,
        out_shape=list(out_shape) + ride.out_shape, scratch_shapes=list(scratch_shapes) + ride.scratch,
        compiler_params=_params(*sem))(*args, *ride.arrays)
    return list(res[:n_out]), list(res[n_out:])


def _resident(shape):
    return pl.BlockSpec(shape, lambda *_: (0,) * len(shape), pipeline_mode=pl.Buffered(1))


def _proj_fwd(x, g, w, qg, kg, tm, ride=None):
    T, K = x.shape
    N = w.shape[1]
    per = COL // (2 * HEAD_DIM)
    assert Q_END % COL == 0 and V_END == Q_END + COL and (GLU_END - V_END) == 4 * COL and KV_WIDTH == COL // 2

    def body(x_ref, g_ref, w_ref, qg_ref, kg_ref, o_ref, u_ref, qn_ref, kn_ref, vb_ref, h0_ref):
        xv = x_ref[...]
        r = lax.rsqrt(jnp.mean(xv * xv, axis=-1, keepdims=True) + EPS)
        u = (xv * r * g_ref[...]).astype(bf16)
        u_ref[...] = u

        def block(c):
            cs = slice(c * COL, (c + 1) * COL)
            pc = jnp.dot(u, w_ref[:, cs], preferred_element_type=f32)
            o_ref[:, cs] = pc.astype(bf16)
            return pc

        qgv = qg_ref[...] * (HEAD_DIM ** -0.5)
        for c in range(Q_END // COL):
            pc = block(c)
            for t in range(per):
                xq = pc[:, _pair_cols(t)]
                qn_ref[:, _pair_cols(c * per + t)] = (xq * _pair_rstd(xq, False) * qgv).astype(bf16)
        pc = block(Q_END // COL)
        for t in range(KV_WIDTH // (2 * HEAD_DIM)):
            xk = pc[:, _pair_cols(t)]
            kn_ref[:, _pair_cols(t)] = (xk * _pair_rstd(xk, False) * kg_ref[...]).astype(bf16)
        vb_ref[...] = pc[:, KV_WIDTH:].astype(bf16)
        a0 = V_END // COL
        for half in range(2):
            gate = block(a0 + 2 + half)
            h0_ref[:, half * COL:(half + 1) * COL] = block(a0 + half) * _sigmoid(gate)
        for c in range(GLU_END // COL, N // COL):
            block(c)

    row = lambda width: pl.BlockSpec((tm, width), lambda i: (i, 0))
    return _call(
        body, name="proj_fwd", grid=(T // tm,),
        in_specs=[row(K), _resident((1, K)), _resident((K, N)), _resident((1, 2 * HEAD_DIM)), _resident((1, 2 * HEAD_DIM))],
        out_specs=[row(N), row(K), row(ATTN_WIDTH), row(KV_WIDTH), row(KV_WIDTH), row(D_MODEL)],
        out_shape=[S((T, N), bf16), S((T, K), bf16), S((T, ATTN_WIDTH), bf16), S((T, KV_WIDTH), bf16), S((T, KV_WIDTH), bf16),
                   S((T, D_MODEL), f32)],
        args=(x, g, w, qg, kg), ride=ride)


def _bias_table(rel_bias, bucket):
    def body(rb_ref, bk_ref, o_ref):
        b = bk_ref[...]
        absent = lax.broadcasted_iota(jnp.int32, (QBLOCK, 2 * QBLOCK), 1) < QBLOCK
        for h in range(N_Q_HEADS):
            acc = jnp.full((QBLOCK, 2 * QBLOCK), NEG, f32)
            for k in range(N_BUCKETS):
                acc = jnp.where(b == k, rb_ref[k, h], acc)
            o_ref[0, h * QBLOCK:(h + 1) * QBLOCK, :] = acc
            o_ref[1, h * QBLOCK:(h + 1) * QBLOCK, :] = jnp.where(absent, NEG, acc)

    return pl.pallas_call(
        body, name="bias_table", out_shape=S((2, N_Q_HEADS * QBLOCK, 2 * QBLOCK), f32),
        in_specs=[SMEM, pl.BlockSpec(memory_space=pltpu.VMEM)],
    )(rel_bias, bucket)


def _bias_spec():
    return pl.BlockSpec((None, N_Q_HEADS * QBLOCK, 2 * QBLOCK), lambda n: (jnp.where(n == 0, 1, 0), 0, 0))


def _swap_halves(t):
    return jnp.concatenate([t[:, HEAD_DIM:], t[:, :HEAD_DIM]], axis=1)


def _low_lanes():
    return lax.broadcasted_iota(jnp.int32, (1, 2 * HEAD_DIM), 1) < HEAD_DIM


def _one_head(pair, side):
    zero = jnp.zeros((), pair.dtype)
    return jnp.where(_low_lanes(), pair, zero) if side == 0 else jnp.where(_low_lanes(), zero, pair)


def _pair_mean(t, on_mxu):
    if not on_mxu:
        m_lo = jnp.sum(_one_head(t, 0), axis=-1, keepdims=True) * (1.0 / HEAD_DIM)
        m_hi = jnp.sum(_one_head(t, 1), axis=-1, keepdims=True) * (1.0 / HEAD_DIM)
        return jnp.where(_low_lanes(), m_lo, m_hi)
    width = 2 * HEAD_DIM
    same_head = ((lax.broadcasted_iota(jnp.int32, (width, width), 0) < HEAD_DIM)
                 == (lax.broadcasted_iota(jnp.int32, (width, width), 1) < HEAD_DIM))
    e = jnp.where(same_head, 1.0 / HEAD_DIM, 0.0).astype(bf16)
    hi = t.astype(bf16)
    lo = (t - hi.astype(f32)).astype(bf16)
    return jnp.dot(hi, e, preferred_element_type=f32) + jnp.dot(lo, e, preferred_element_type=f32)


def _pair_rstd(x, on_mxu):
    return lax.rsqrt(_pair_mean(x * x, on_mxu) + EPS)


def _kv_placements(band):
    out = {}
    for m in range(N_KV_HEADS // 2):
        pair = band[:, m * 2 * HEAD_DIM:(m + 1) * 2 * HEAD_DIM]
        swapped = _swap_halves(pair)
        for hh in range(2):
            out[2 * m + hh, 0] = _one_head(pair if hh == 0 else swapped, 0)
            out[2 * m + hh, 1] = _one_head(swapped if hh == 0 else pair, 1)
    return out


def _head_rows(hq):
    return slice(hq * QBLOCK, (hq + 1) * QBLOCK)


def _pair_cols(pr):
    return slice(pr * 2 * HEAD_DIM, (pr + 1) * 2 * HEAD_DIM)


def _attn_fwd(qn, kn, vb, bias, sinks, ride=None):
    T = qn.shape[0]
    nb = T // QBLOCK

    def body(q_ref, kc_ref, kp_ref, vc_ref, vp_ref, b_ref, s_ref, o_ref, lse_ref, s_scr, p_scr):
        lane = lax.broadcasted_iota(jnp.int32, (QBLOCK, 2 * HEAD_DIM), 1)
        kx = _kv_placements(jnp.concatenate([kp_ref[...], kc_ref[...]], axis=0))
        vx = _kv_placements(jnp.concatenate([vp_ref[...], vc_ref[...]], axis=0))
        for hq in range(N_Q_HEADS):
            qm = _one_head(q_ref[:, _pair_cols(hq // 2)], hq % 2)
            s_scr[_head_rows(hq), :] = _nt(qm, kx[hq // GROUP, hq % 2]) + b_ref[_head_rows(hq), :]
        lse_tile = jnp.zeros((QBLOCK, 2 * HEAD_DIM), f32)
        for hq in range(N_Q_HEADS):
            s = s_scr[_head_rows(hq), :]
            sink = s_ref[0, hq]
            m = jnp.maximum(jnp.max(s, axis=-1, keepdims=True), sink)
            p = jnp.exp(s - m)
            l = jnp.sum(p, axis=-1, keepdims=True) + jnp.exp(sink - m)
            p_scr[_head_rows(hq), :] = (p * (1.0 / l)).astype(bf16)
            lse_tile = jnp.where(lane == hq, m + jnp.log(l), lse_tile)
        lse_ref[...] = lse_tile
        for pr in range(N_Q_HEADS // 2):
            h = 2 * pr // GROUP
            o_pair = (jnp.dot(p_scr[_head_rows(2 * pr), :], vx[h, 0], preferred_element_type=f32)
                      + jnp.dot(p_scr[_head_rows(2 * pr + 1), :], vx[h, 1], preferred_element_type=f32))
            o_ref[:, _pair_cols(pr)] = o_pair.astype(bf16)

    cur = lambda n: (n, 0)
    prev = lambda n: (jnp.maximum(n - 1, 0), 0)
    return _call(
        body, name="attn_fwd", grid=(nb,),
        in_specs=[pl.BlockSpec((QBLOCK, ATTN_WIDTH), cur), pl.BlockSpec((QBLOCK, KV_WIDTH), cur),
                  pl.BlockSpec((QBLOCK, KV_WIDTH), prev), pl.BlockSpec((QBLOCK, KV_WIDTH), cur),
                  pl.BlockSpec((QBLOCK, KV_WIDTH), prev), _bias_spec(), SMEM],
        out_specs=[pl.BlockSpec((QBLOCK, ATTN_WIDTH), cur), pl.BlockSpec((QBLOCK, 2 * HEAD_DIM), cur)],
        out_shape=[S((T, ATTN_WIDTH), bf16), S((T, 2 * HEAD_DIM), f32)],
        scratch_shapes=[pltpu.VMEM((N_Q_HEADS * QBLOCK, 2 * QBLOCK), f32), pltpu.VMEM((N_Q_HEADS * QBLOCK, 2 * QBLOCK), bf16)],
        args=(qn, kn, kn, vb, vb, bias, sinks), ride=ride)


def _layer_norm_stats(h1):
    mu = jnp.mean(h1, axis=-1, keepdims=True)
    xc = h1 - mu
    rstd = lax.rsqrt(jnp.mean(xc * xc, axis=-1, keepdims=True) + EPS)
    return xc * rstd, rstd


def _advanced_windows(win):
    rows = win.shape[0]
    for r in range(8):
        yield r, (win if r == 0 else pltpu.roll(win, rows - r, 0))


def _tap_offsets(r, rows):
    for q in range((rows - CONV_UNIT) // 8 + 1):
        if r == 0 or 8 * q + r + CONV_UNIT <= rows:
            yield q, 8 * q + r


def _conv_fwd(h0, w_dw, b_dw, ln_g, ln_b, tm, ride=None):
    T = h0.shape[0]
    per = tm // CONV_HALO
    lead = CONV_HALO - (CONV_WIDTH - 1)

    def body(hc_ref, hp_ref, w_ref, b_ref, g_ref, bb_ref, h1_ref, h3_ref, cat):
        i = pl.program_id(0)
        cat[0:CONV_HALO, :] = jnp.where(i == 0, 0.0, hp_ref[...])
        cat[CONV_HALO:, :] = hc_ref[...]

        def unit_rows(c, carry):
            r0 = pl.multiple_of(c * CONV_UNIT, CONV_UNIT)
            for j in range(D_MODEL // 128):
                ls = slice(j * 128, (j + 1) * 128)
                win = cat[pl.ds(r0, CONV_UNIT + CONV_HALO), ls]
                acc = jnp.zeros((CONV_UNIT, 128), f32) + b_ref[:, ls]
                for r, adv in _advanced_windows(win):
                    for q, off in _tap_offsets(r, CONV_UNIT + CONV_HALO):
                        k = off - lead
                        if 0 <= k < CONV_WIDTH:
                            acc = acc + adv[8 * q:8 * q + CONV_UNIT] * w_ref[k:k + 1, ls]
                h1_ref[pl.ds(r0, CONV_UNIT), ls] = acc
            return carry

        lax.fori_loop(0, tm // CONV_UNIT, unit_rows, 0)
        acc = h1_ref[...]
        xhat, _ = _layer_norm_stats(acc)
        h2 = xhat * g_ref[...] + bb_ref[...]
        h3_ref[...] = (h2 * _sigmoid(h2)).astype(bf16)

    vec = pl.BlockSpec((1, D_MODEL), lambda i: (0, 0))
    return _call(
        body, name="conv_fwd", grid=(T // tm,),
        in_specs=[pl.BlockSpec((tm, D_MODEL), lambda i: (i, 0)),
                  pl.BlockSpec((CONV_HALO, D_MODEL), lambda i: (jnp.maximum(i * per - 1, 0), 0)),
                  pl.BlockSpec((CONV_WIDTH, D_MODEL), lambda i: (0, 0)), vec, vec, vec],
        out_specs=[pl.BlockSpec((tm, D_MODEL), lambda i: (i, 0)), pl.BlockSpec((tm, D_MODEL), lambda i: (i, 0))],
        out_shape=[S((T, D_MODEL), f32), S((T, D_MODEL), bf16)],
        scratch_shapes=[pltpu.VMEM((tm + CONV_HALO, D_MODEL), f32)],
        args=(h0, h0, w_dw, b_dw, ln_g, ln_b), ride=ride)


def _mix_fwd(x, o, h3, proj, w_ao, w_co, w_o, tm):
    T = x.shape[0]
    row = pl.BlockSpec((tm, D_MODEL), lambda i: (i, 0))
    wsp = _resident((D_MODEL, D_MODEL))
    g0 = GLU_END // COL

    def gate_spec(off):
        return pl.BlockSpec((tm, COL), lambda i: (i, g0 + off))

    def body(x_ref, o_ref, h3_ref, ga0, ga1, gc0, gc1, wa_ref, wc_ref, wo_ref, x1_ref, at_ref, cv_ref, mg_ref):
        attn = jnp.dot(o_ref[...], wa_ref[...], preferred_element_type=f32)
        conv = jnp.dot(h3_ref[...], wc_ref[...], preferred_element_type=f32)
        ga = jnp.concatenate([ga0[...], ga1[...]], axis=-1).astype(f32)
        gc = jnp.concatenate([gc0[...], gc1[...]], axis=-1).astype(f32)
        merged = (_sigmoid(ga) * attn + _sigmoid(gc) * conv).astype(bf16)
        at_ref[...] = attn.astype(bf16)
        cv_ref[...] = conv.astype(bf16)
        mg_ref[...] = merged
        x1_ref[...] = x_ref[...] + jnp.dot(merged, wo_ref[...], preferred_element_type=f32)

    return pl.pallas_call(
        body, name="mix_fwd", grid=(T // tm,),
        in_specs=[row, row, row, gate_spec(0), gate_spec(1), gate_spec(2), gate_spec(3), wsp, wsp, wsp],
        out_specs=[row, row, row, row],
        out_shape=[S((T, D_MODEL), f32), S((T, D_MODEL), bf16), S((T, D_MODEL), bf16), S((T, D_MODEL), bf16)],
        compiler_params=_params("parallel"),
    )(x, o, h3, proj, proj, proj, proj, w_ao, w_co, w_o)


def _mix_ffn_fwd(x, o, h3, proj, w_ao, w_co, w_o, g, w1, w2, target, tm):
    T = x.shape[0]
    nj = w1.shape[0] // FF_CHUNK
    g0 = GLU_END // COL

    def gate_spec(off):
        return pl.BlockSpec((tm, COL), lambda i: (i, g0 + off))

    def body(x_ref, o_ref, h3_ref, ga0, ga1, gc0, gc1, wa_ref, wc_ref, wo_ref, g_ref, w1_ref, w2_ref, t_ref,
             x1_ref, at_ref, cv_ref, mg_ref, a_ref, u_ref, dy_ref, dyb_ref, ls_ref, hm):
        attn = jnp.dot(o_ref[...], wa_ref[...], preferred_element_type=f32)
        conv = jnp.dot(h3_ref[...], wc_ref[...], preferred_element_type=f32)
        ga = jnp.concatenate([ga0[...], ga1[...]], axis=-1).astype(f32)
        gc = jnp.concatenate([gc0[...], gc1[...]], axis=-1).astype(f32)
        merged = (_sigmoid(ga) * attn + _sigmoid(gc) * conv).astype(bf16)
        at_ref[...] = attn.astype(bf16)
        cv_ref[...] = conv.astype(bf16)
        mg_ref[...] = merged
        xv = x_ref[...] + jnp.dot(merged, wo_ref[...], preferred_element_type=f32)
        x1_ref[...] = xv
        r = lax.rsqrt(jnp.mean(xv * xv, axis=-1, keepdims=True) + EPS)
        u = (xv * r * g_ref[...]).astype(bf16)
        u_ref[...] = u
        for j in range(nj):
            js = slice(j * FF_CHUNK, (j + 1) * FF_CHUNK)
            a = _nt(u, w1_ref[js, :])
            a_ref[:, js] = a.astype(bf16)
            hm[:, js] = jnp.square(jnp.maximum(a, 0.0)).astype(bf16)
        err = xv + jnp.dot(hm[...], w2_ref[...], preferred_element_type=f32) - t_ref[...]
        dy = err * (1.0 / D_MODEL)
        dy_ref[...] = dy
        dyb_ref[...] = dy.astype(bf16)
        ls_ref[...] = jnp.zeros((8, 128), f32) + jnp.sum(err * err) * (0.5 / D_MODEL)

    row = pl.BlockSpec((tm, D_MODEL), lambda i: (i, 0))
    wide = pl.BlockSpec((tm, D_FF), lambda i: (i, 0))
    wsp = _resident((D_MODEL, D_MODEL))
    return pl.pallas_call(
        body, name="mix_ffn_fwd", grid=(T // tm,),
        in_specs=[row, row, row, gate_spec(0), gate_spec(1), gate_spec(2), gate_spec(3), wsp, wsp, wsp,
                  _resident((1, D_MODEL)), _resident(w1.shape), _resident(w2.shape), row],
        out_specs=[row, row, row, row, wide, row, row, row, pl.BlockSpec((None, 8, 128), lambda i: (i, 0, 0))],
        out_shape=[S((T, D_MODEL), f32), S((T, D_MODEL), bf16), S((T, D_MODEL), bf16), S((T, D_MODEL), bf16),
                   S((T, D_FF), bf16), S((T, D_MODEL), bf16), S((T, D_MODEL), f32), S((T, D_MODEL), bf16),
                   S((T // tm, 8, 128), f32)],
        scratch_shapes=[pltpu.VMEM((tm, D_FF), bf16)],
        compiler_params=_params("parallel"),
    )(x, o, h3, proj, proj, proj, proj, w_ao, w_co, w_o, g, w1, w2, target)


def _ffn_fwd(x1, g, w1, w2, target, tm):
    T = x1.shape[0]
    nj = w1.shape[0] // FF_CHUNK

    def body(x_ref, g_ref, w1_ref, w2_ref, t_ref, a_ref, u_ref, dy_ref, dyb_ref, ls_ref, hm):
        xv = x_ref[...]
        r = lax.rsqrt(jnp.mean(xv * xv, axis=-1, keepdims=True) + EPS)
        u = (xv * r * g_ref[...]).astype(bf16)
        u_ref[...] = u
        for j in range(nj):
            js = slice(j * FF_CHUNK, (j + 1) * FF_CHUNK)
            a = _nt(u, w1_ref[js, :])
            a_ref[:, js] = a.astype(bf16)
            hm[:, js] = jnp.square(jnp.maximum(a, 0.0)).astype(bf16)
        err = xv + jnp.dot(hm[...], w2_ref[...], preferred_element_type=f32) - t_ref[...]
        dy = err * (1.0 / D_MODEL)
        dy_ref[...] = dy
        dyb_ref[...] = dy.astype(bf16)
        ls_ref[...] = jnp.zeros((8, 128), f32) + jnp.sum(err * err) * (0.5 / D_MODEL)

    row = pl.BlockSpec((tm, D_MODEL), lambda i: (i, 0))
    wide = pl.BlockSpec((tm, D_FF), lambda i: (i, 0))
    return pl.pallas_call(
        body, name="ffn_fwd", grid=(T // tm,),
        in_specs=[row, _resident((1, D_MODEL)), _resident(w1.shape), _resident(w2.shape), row],
        out_specs=[wide, row, row, row, pl.BlockSpec((None, 8, 128), lambda i: (i, 0, 0))],
        out_shape=[S((T, D_FF), bf16), S((T, D_MODEL), bf16), S((T, D_MODEL), f32), S((T, D_MODEL), bf16),
                   S((T // tm, 8, 128), f32)],
        scratch_shapes=[pltpu.VMEM((tm, D_FF), bf16)],
        compiler_params=_params("parallel"),
    )(x1, g, w1, w2, target)


def _rms_bwd(du, xv, gv):
    r = lax.rsqrt(jnp.mean(xv * xv, axis=-1, keepdims=True) + EPS)
    xn = xv * r
    dg = jnp.sum(du * xn, axis=0, keepdims=True)
    dxn = du * gv
    dx = r * (dxn - xn * jnp.mean(dxn * xn, axis=-1, keepdims=True))
    return dx, dg


def _ffn_bwd(dy, dyb, a, x1, g, w1, w2, tm, ride=None):
    T = dy.shape[0]
    nj = w1.shape[0] // FF_CHUNK

    def body(dy_ref, dyb_ref, a_ref, x_ref, g_ref, w1_ref, w2_ref, da_ref, dx_ref, dxb_ref, dg_ref):
        @pl.when(pl.program_id(0) == 0)
        def _():
            dg_ref[...] = jnp.zeros_like(dg_ref)

        dyb_v = dyb_ref[...]
        for j in range(nj):
            js = slice(j * FF_CHUNK, (j + 1) * FF_CHUNK)
            dh = _nt(dyb_v, w2_ref[js, :])
            da_ref[:, js] = (dh * (2.0 * jnp.maximum(a_ref[:, js].astype(f32), 0.0))).astype(bf16)
        du = jnp.dot(da_ref[...], w1_ref[...], preferred_element_type=f32)
        dx, dg = _rms_bwd(du, x_ref[...], g_ref[...])
        dx1 = dy_ref[...] + dx
        dx_ref[...] = dx1
        dxb_ref[...] = dx1.astype(bf16)
        dg_ref[...] += dg

    row = pl.BlockSpec((tm, D_MODEL), lambda i: (i, 0))
    wide = pl.BlockSpec((tm, D_FF), lambda i: (i, 0))
    vec = pl.BlockSpec((1, D_MODEL), lambda i: (0, 0))
    return _call(
        body, name="ffn_bwd", grid=(T // tm,),
        in_specs=[row, row, wide, row, _resident((1, D_MODEL)), _resident(w1.shape), _resident(w2.shape)],
        out_specs=[wide, row, row, vec],
        out_shape=[S((T, D_FF), bf16), S((T, D_MODEL), f32), S((T, D_MODEL), bf16), S((1, D_MODEL), f32)],
        args=(dy, dyb, a, x1, g, w1, w2), ride=ride)


def _wgrad(name, a, b, tk, tn, tt, relu2=False, slab=None, out_dtype=bf16):
    T, Ka = a.shape
    Nb = b.shape[1]
    nt = T // tt

    def body(a_ref, b_ref, o_ref, acc):
        t = pl.program_id(2)
        av = a_ref[...]
        if relu2:
            av = jnp.square(jnp.maximum(av.astype(f32), 0.0))
        prod = _tn(av.astype(bf16), b_ref[...].astype(bf16))

        @pl.when(t == 0)
        def _():
            acc[...] = prod

        @pl.when(t > 0)
        def _():
            acc[...] += prod

        @pl.when(t == nt - 1)
        def _():
            if slab is None:
                o_ref[...] = acc[...].astype(out_dtype)
            else:
                for s in range(tn // slab):
                    o_ref[s] = acc[:, s * slab:(s + 1) * slab].astype(out_dtype)

    if slab is not None:
        out_shape = S((Nb // slab, Ka, slab), out_dtype)
        out_spec = pl.BlockSpec((tn // slab, tk, slab), lambda i, j, t: (j, i, 0))
    else:
        out_shape = S((Ka, Nb), out_dtype)
        out_spec = pl.BlockSpec((tk, tn), lambda i, j, t: (i, j))
    return pl.pallas_call(
        body, name=name, grid=(Ka // tk, Nb // tn, nt),
        in_specs=[pl.BlockSpec((tt, tk), lambda i, j, t: (t, i)), pl.BlockSpec((tt, tn), lambda i, j, t: (t, j))],
        out_specs=out_spec, out_shape=out_shape, scratch_shapes=[pltpu.VMEM((tk, tn), f32)],
        compiler_params=_params("parallel", "parallel", "arbitrary"),
    )(a, b)


def _mix_bwd(dx1, proj, attn, conv, h1, ln_g, ln_b, w_ao, w_co, w_o, tm, ride=None):
    T = dx1.shape[0]
    g0 = GLU_END // COL

    def gate_spec(off):
        return pl.BlockSpec((tm, COL), lambda i: (i, g0 + off))

    def body(dx_ref, ga0, ga1, gc0, gc1, at_ref, cv_ref, h_ref, g_ref, b_ref, wa_ref, wc_ref, wo_ref,
             da_ref, dc_ref, do_ref, dh1_ref, dg_ref, dlg_ref, dlb_ref, dbd_ref):
        @pl.when(pl.program_id(0) == 0)
        def _():
            dlg_ref[...] = jnp.zeros_like(dlg_ref)
            dlb_ref[...] = jnp.zeros_like(dlb_ref)
            dbd_ref[...] = jnp.zeros_like(dbd_ref)

        dm = _nt(dx_ref[...].astype(bf16), wo_ref[...])
        sa = _sigmoid(jnp.concatenate([ga0[...], ga1[...]], axis=-1).astype(f32))
        sc = _sigmoid(jnp.concatenate([gc0[...], gc1[...]], axis=-1).astype(f32))
        dattn = (dm * sa).astype(bf16)
        dconv = (dm * sc).astype(bf16)
        da_ref[...] = dattn
        dc_ref[...] = dconv
        dg_ref[:, 0:D_MODEL] = (dm * at_ref[...].astype(f32) * sa * (1.0 - sa)).astype(bf16)
        dg_ref[:, D_MODEL:2 * D_MODEL] = (dm * cv_ref[...].astype(f32) * sc * (1.0 - sc)).astype(bf16)
        do_ref[...] = _nt(dattn, wa_ref[...]).astype(bf16)
        dh3 = _nt(dconv, wc_ref[...])
        xhat, rstd = _layer_norm_stats(h_ref[...])
        h2 = xhat * g_ref[...] + b_ref[...]
        sg = _sigmoid(h2)
        dh2 = dh3 * (sg * (1.0 + h2 * (1.0 - sg)))
        dlg_ref[...] += jnp.sum(dh2 * xhat, axis=0, keepdims=True)
        dlb_ref[...] += jnp.sum(dh2, axis=0, keepdims=True)
        dxh = dh2 * g_ref[...]
        dh1 = rstd * (dxh - jnp.mean(dxh, axis=-1, keepdims=True) - xhat * jnp.mean(dxh * xhat, axis=-1, keepdims=True))
        dh1_ref[...] = dh1
        dbd_ref[...] += jnp.sum(dh1, axis=0, keepdims=True)

    row = pl.BlockSpec((tm, D_MODEL), lambda i: (i, 0))
    vec = pl.BlockSpec((1, D_MODEL), lambda i: (0, 0))
    par = _resident((1, D_MODEL))
    wsp = _resident((D_MODEL, D_MODEL))
    return _call(
        body, name="mix_bwd", grid=(T // tm,),
        in_specs=[row, gate_spec(0), gate_spec(1), gate_spec(2), gate_spec(3), row, row, row, par, par, wsp, wsp, wsp],
        out_specs=[row, row, row, row, pl.BlockSpec((tm, 2 * D_MODEL), lambda i: (i, 0)), vec, vec, vec],
        out_shape=[S((T, D_MODEL), bf16), S((T, D_MODEL), bf16), S((T, D_MODEL), bf16), S((T, D_MODEL), f32),
                   S((T, 2 * D_MODEL), bf16), S((1, D_MODEL), f32), S((1, D_MODEL), f32), S((1, D_MODEL), f32)],
        args=(dx1, proj, proj, proj, proj, attn, conv, h1, ln_g, ln_b, w_ao, w_co, w_o), ride=ride)


def _conv_bwd(dh1, h0, proj, w_dw, tm, ride=None):
    T = dh1.shape[0]
    per = tm // CONV_HALO
    nh = T // CONV_HALO
    nt = T // tm
    a0 = V_END // COL
    lead = CONV_HALO - (CONV_WIDTH - 1)

    def body(dc_ref, dn_ref, hc_ref, hp_ref, a0_ref, a1_ref, g0_ref, g1_ref, w_ref, dglu_ref, dw_ref, dcat, hcat, wacc, dh0):
        i = pl.program_id(0)

        @pl.when(i == 0)
        def _():
            wacc[...] = jnp.zeros_like(wacc)

        dcat[0:tm, :] = dc_ref[...]
        dcat[tm:, :] = jnp.where(i == nt - 1, 0.0, dn_ref[...])
        hcat[0:CONV_HALO, :] = jnp.where(i == 0, 0.0, hp_ref[...])
        hcat[CONV_HALO:, :] = hc_ref[...]
        span = CONV_UNIT + CONV_HALO

        def unit_rows(c, carry):
            r0 = pl.multiple_of(c * CONV_UNIT, CONV_UNIT)
            for j in range(D_MODEL // 128):
                ls = slice(j * 128, (j + 1) * 128)
                dwin = dcat[pl.ds(r0, span), ls]
                acc = jnp.zeros((CONV_UNIT, 128), f32)
                for r, adv in _advanced_windows(dwin):
                    for q, off in _tap_offsets(r, span):
                        k = CONV_WIDTH - 1 - off
                        if 0 <= k < CONV_WIDTH:
                            acc = acc + adv[8 * q:8 * q + CONV_UNIT] * w_ref[k:k + 1, ls]
                dh0[pl.ds(r0, CONV_UNIT), ls] = acc
                dcur = dwin[0:CONV_UNIT]
                for r, adv in _advanced_windows(hcat[pl.ds(r0, span), ls]):
                    for q, off in _tap_offsets(r, span):
                        k = off - lead
                        if 0 <= k < CONV_WIDTH:
                            prod = dcur * adv[8 * q:8 * q + CONV_UNIT]
                            wacc[k, :, ls] += jnp.sum(prod.reshape(CONV_UNIT // 8, 8, 128), axis=0)
            return carry

        lax.fori_loop(0, tm // CONV_UNIT, unit_rows, 0)
        dh0v = dh0[...]
        av = jnp.concatenate([a0_ref[...], a1_ref[...]], axis=-1).astype(f32)
        sg = _sigmoid(jnp.concatenate([g0_ref[...], g1_ref[...]], axis=-1).astype(f32))
        dglu_ref[:, 0:D_MODEL] = (dh0v * sg).astype(bf16)
        dglu_ref[:, D_MODEL:2 * D_MODEL] = (dh0v * av * sg * (1.0 - sg)).astype(bf16)

        @pl.when(i == nt - 1)
        def _():
            for k in range(CONV_WIDTH):
                dw_ref[k:k + 1, :] = jnp.sum(wacc[k], axis=0, keepdims=True)
            dw_ref[CONV_WIDTH:CONV_WIDTH + 1, :] = jnp.zeros((1, D_MODEL), f32)

    row = pl.BlockSpec((tm, D_MODEL), lambda i: (i, 0))

    def col_spec(off):
        return pl.BlockSpec((tm, COL), lambda i: (i, a0 + off))

    return _call(
        body, name="conv_bwd", grid=(nt,),
        in_specs=[row, pl.BlockSpec((CONV_HALO, D_MODEL), lambda i: (jnp.minimum((i + 1) * per, nh - 1), 0)),
                  row, pl.BlockSpec((CONV_HALO, D_MODEL), lambda i: (jnp.maximum(i * per - 1, 0), 0)),
                  col_spec(0), col_spec(1), col_spec(2), col_spec(3),
                  pl.BlockSpec((CONV_WIDTH, D_MODEL), lambda i: (0, 0))],
        out_specs=[pl.BlockSpec((tm, 2 * D_MODEL), lambda i: (i, 0)), pl.BlockSpec((CONV_WIDTH + 1, D_MODEL), lambda i: (0, 0))],
        out_shape=[S((T, 2 * D_MODEL), bf16), S((CONV_WIDTH + 1, D_MODEL), f32)],
        scratch_shapes=[pltpu.VMEM((tm + CONV_HALO, D_MODEL), f32), pltpu.VMEM((tm + CONV_HALO, D_MODEL), f32),
                        pltpu.VMEM((CONV_WIDTH, 8, D_MODEL), f32), pltpu.VMEM((tm, D_MODEL), f32)],
        args=(dh1, dh1, h0, h0, proj, proj, proj, proj, w_dw), ride=ride)


def _attn_bwd(qn, kn, vb, o, do, lse, bias, sinks, ride=None):
    T = qn.shape[0]
    nb = T // QBLOCK

    def body(q_ref, kc_ref, kp_ref, vc_ref, vp_ref, o_ref, do_ref, lse_ref, b_ref, s_ref,
             dq_ref, dk_ref, dv_ref, db_ref, dsk_ref, kcar, vcar, s_scr, dp_scr, p_scr, ds_scr):
        n = pl.program_id(0)

        @pl.when(n == 0)
        def _():
            db_ref[...] = jnp.zeros_like(db_ref)
            dsk_ref[...] = jnp.zeros_like(dsk_ref)
            kcar[...] = jnp.zeros_like(kcar)
            vcar[...] = jnp.zeros_like(vcar)

        @pl.when(n < nb)
        def _():
            lane = lax.broadcasted_iota(jnp.int32, (QBLOCK, 2 * HEAD_DIM), 1)
            lane_row = lax.broadcasted_iota(jnp.int32, (1, 2 * HEAD_DIM), 1)
            kx = _kv_placements(jnp.concatenate([kp_ref[...], kc_ref[...]], axis=0))
            vx = _kv_placements(jnp.concatenate([vp_ref[...], vc_ref[...]], axis=0))
            lse_tile = lse_ref[...]
            delta, lse_c = {}, {}
            for pr in range(N_Q_HEADS // 2):
                dop = do_ref[:, _pair_cols(pr)]
                dl = dop.astype(f32) * o_ref[:, _pair_cols(pr)].astype(f32)
                for side in range(2):
                    hq = 2 * pr + side
                    h = hq // GROUP
                    qm = _one_head(q_ref[:, _pair_cols(pr)], side)
                    s_scr[_head_rows(hq), :] = _nt(qm, kx[h, side]) + b_ref[_head_rows(hq), :]
                    dp_scr[_head_rows(hq), :] = _nt(_one_head(dop, side), vx[h, side])
                    delta[hq] = jnp.sum(_one_head(dl, side), axis=-1, keepdims=True)
                    lse_c[hq] = jnp.sum(jnp.where(lane == hq, lse_tile, 0.0), axis=-1, keepdims=True)
            dsk = jnp.zeros((1, 2 * HEAD_DIM), f32)
            for hq in range(N_Q_HEADS):
                p = jnp.exp(s_scr[_head_rows(hq), :] - lse_c[hq])
                ds = p * (dp_scr[_head_rows(hq), :] - delta[hq])
                db_ref[_head_rows(hq), :] += ds
                p_scr[_head_rows(hq), :] = p.astype(bf16)
                ds_scr[_head_rows(hq), :] = ds.astype(bf16)
                psink = jnp.exp(s_ref[0, hq] - lse_c[hq])
                dsk = dsk - jnp.where(lane_row == hq, jnp.sum(psink * delta[hq], axis=0, keepdims=True), 0.0)
            dsk_ref[...] += dsk
            for pr in range(N_Q_HEADS // 2):
                h = 2 * pr // GROUP
                dq_ref[:, _pair_cols(pr)] = (jnp.dot(ds_scr[_head_rows(2 * pr), :], kx[h, 0], preferred_element_type=f32)
                                             + jnp.dot(ds_scr[_head_rows(2 * pr + 1), :], kx[h, 1], preferred_element_type=f32))
            folded_k, folded_v = [], []
            for h in range(N_KV_HEADS):
                ka = jnp.zeros((2 * QBLOCK, 2 * HEAD_DIM), f32)
                va = jnp.zeros((2 * QBLOCK, 2 * HEAD_DIM), f32)
                for g in range(GROUP):
                    hq = h * GROUP + g
                    ka = ka + _tn(ds_scr[_head_rows(hq), :], _one_head(q_ref[:, _pair_cols(hq // 2)], hq % 2))
                    va = va + _tn(p_scr[_head_rows(hq), :], _one_head(do_ref[:, _pair_cols(hq // 2)], hq % 2))
                folded_k.append(ka + _swap_halves(ka))
                folded_v.append(va + _swap_halves(va))
            low = _low_lanes()
            for m in range(N_KV_HEADS // 2):
                cs = _pair_cols(m)
                for folded, out_ref, car in ((folded_k, dk_ref, kcar), (folded_v, dv_ref, vcar)):
                    band = jnp.where(low, folded[2 * m], folded[2 * m + 1])
                    out_ref[:, cs] = car[:, cs] + band[0:QBLOCK, :]
                    car[:, cs] = band[QBLOCK:, :]

        @pl.when(n == nb)
        def _():
            dk_ref[...] = kcar[...]
            dv_ref[...] = vcar[...]

    cur = lambda n: (jnp.minimum(n, nb - 1), 0)
    prev = lambda n: (jnp.clip(n - 1, 0, nb - 1), 0)
    qspec = pl.BlockSpec((QBLOCK, ATTN_WIDTH), cur)
    kcur, kprev = pl.BlockSpec((QBLOCK, KV_WIDTH), cur), pl.BlockSpec((QBLOCK, KV_WIDTH), prev)
    whole = lambda shape: pl.BlockSpec(shape, lambda n: (0,) * len(shape))
    scores = (N_Q_HEADS * QBLOCK, 2 * QBLOCK)
    return _call(
        body, name="attn_bwd", grid=(nb + 1,),
        in_specs=[qspec, kcur, kprev, kcur, kprev, qspec, qspec, pl.BlockSpec((QBLOCK, 2 * HEAD_DIM), cur), _bias_spec(), SMEM],
        out_specs=[qspec, kprev, kprev, whole(scores), whole((1, 2 * HEAD_DIM))],
        out_shape=[S((T, ATTN_WIDTH), f32), S((T, KV_WIDTH), f32), S((T, KV_WIDTH), f32), S(scores, f32),
                   S((1, 2 * HEAD_DIM), f32)],
        scratch_shapes=[pltpu.VMEM((QBLOCK, KV_WIDTH), f32), pltpu.VMEM((QBLOCK, KV_WIDTH), f32),
                        pltpu.VMEM(scores, f32), pltpu.VMEM(scores, f32), pltpu.VMEM(scores, bf16), pltpu.VMEM(scores, bf16)],
        args=(qn, kn, kn, vb, vb, o, do, lse, bias, sinks), ride=ride)


def _rel_bias_bwd(dbias, bucket):
    def body(d_ref, bk_ref, o_ref):
        b = bk_ref[...]
        for k in range(N_BUCKETS):
            mk = b == k
            for h in range(N_Q_HEADS):
                o_ref[k, h] = jnp.sum(jnp.where(mk, d_ref[h * QBLOCK:(h + 1) * QBLOCK, :], 0.0))

    return pl.pallas_call(body, name="rel_bias_bwd", out_shape=S((N_BUCKETS, N_Q_HEADS), f32), out_specs=SMEM)(dbias, bucket)


def _qk_norm_bwd(dq, dk, dv, proj, qg, kg, tm):
    T = dq.shape[0]
    scale = HEAD_DIM ** -0.5

    def pair_bwd(dy, x, gv):
        r = _pair_rstd(x, True)
        xn = x * r
        dxn = dy * gv
        dx = r * (dxn - xn * _pair_mean(dxn * xn, True))
        return dx, jnp.sum(dy * xn, axis=0, keepdims=True)

    def body(dq_ref, dk_ref, dv_ref, p_ref, qg_ref, kg_ref, out_ref, dqg_ref, dkg_ref):
        @pl.when(pl.program_id(0) == 0)
        def _():
            dqg_ref[...] = jnp.zeros_like(dqg_ref)
            dkg_ref[...] = jnp.zeros_like(dkg_ref)

        qgv, kgv = qg_ref[...], kg_ref[...]
        dqg = jnp.zeros((1, 2 * HEAD_DIM), f32)
        for pr in range(N_Q_HEADS // 2):
            dx, dg = pair_bwd(dq_ref[:, _pair_cols(pr)] * scale, p_ref[:, _pair_cols(pr)].astype(f32), qgv)
            out_ref[:, _pair_cols(pr)] = dx.astype(bf16)
            dqg = dqg + dg
        dkg = jnp.zeros((1, 2 * HEAD_DIM), f32)
        for pr in range(N_KV_HEADS // 2):
            ps = slice(Q_END + pr * 2 * HEAD_DIM, Q_END + (pr + 1) * 2 * HEAD_DIM)
            dx, dg = pair_bwd(dk_ref[:, _pair_cols(pr)], p_ref[:, ps].astype(f32), kgv)
            out_ref[:, ps] = dx.astype(bf16)
            dkg = dkg + dg
        out_ref[:, K_END:V_END] = dv_ref[...].astype(bf16)
        dqg_ref[...] += dqg
        dkg_ref[...] += dkg

    vec = pl.BlockSpec((1, 2 * HEAD_DIM), lambda i: (0, 0))
    return pl.pallas_call(
        body, name="qk_norm_bwd", grid=(T // tm,),
        in_specs=[pl.BlockSpec((tm, ATTN_WIDTH), lambda i: (i, 0)), pl.BlockSpec((tm, KV_WIDTH), lambda i: (i, 0)),
                  pl.BlockSpec((tm, KV_WIDTH), lambda i: (i, 0)), pl.BlockSpec((tm, V_END), lambda i: (i, 0)), vec, vec],
        out_specs=[pl.BlockSpec((tm, V_END), lambda i: (i, 0)), vec, vec],
        out_shape=[S((T, V_END), bf16), S((1, 2 * HEAD_DIM), f32), S((1, 2 * HEAD_DIM), f32)],
        compiler_params=_params("arbitrary"),
    )(dq, dk, dv, proj, qg, kg)


def _in_bwd(dqkv, dglu, dgates, w_in, x, g, dx1, tm, ride=None):
    T = x.shape[0]
    pieces = (dqkv, dglu, dgates)
    starts = [0, dqkv.shape[1], dqkv.shape[1] + dglu.shape[1]]

    def body(a0_ref, a1_ref, a2_ref, w_ref, x_ref, g_ref, d_ref, gx_ref, dg_ref):
        @pl.when(pl.program_id(0) == 0)
        def _():
            dg_ref[...] = jnp.zeros_like(dg_ref)

        du = jnp.zeros((tm, D_MODEL), f32)
        for a_ref, c0 in zip((a0_ref, a1_ref, a2_ref), starts):
            du = du + _nt(a_ref[...], w_ref[:, c0:c0 + a_ref.shape[1]])
        dx, dg = _rms_bwd(du, x_ref[...], g_ref[...])
        gx_ref[...] = d_ref[...] + dx
        dg_ref[...] += dg

    row = pl.BlockSpec((tm, D_MODEL), lambda i: (i, 0))
    return _call(
        body, name="in_bwd", grid=(T // tm,),
        in_specs=[pl.BlockSpec((tm, p.shape[1]), lambda i: (i, 0)) for p in pieces]
        + [_resident(w_in.shape), row, _resident((1, D_MODEL)), row],
        out_specs=[row, pl.BlockSpec((1, D_MODEL), lambda i: (0, 0))],
        out_shape=[S((T, D_MODEL), f32), S((1, D_MODEL), f32)],
        args=(dqkv, dglu, dgates, w_in, x, g, dx1), ride=ride)


def _adamw(name, parts, w, m, v, tr):
    _, R, C = w.shape
    bc1 = 1.0 - ADAM_B1 ** ADAM_STEP
    bc2 = 1.0 - ADAM_B2 ** ADAM_STEP

    def body(p_ref, w_ref, m_ref, v_ref, g_ref, d_ref, nm_ref, nv_ref):
        g = p_ref[0].astype(f32)
        for k in range(1, N_DEV):
            g = g + p_ref[k].astype(f32)
        nm = ADAM_B1 * m_ref[...] + (1.0 - ADAM_B1) * g
        nv = ADAM_B2 * v_ref[...] + (1.0 - ADAM_B2) * (g * g)
        g_ref[...] = g
        nm_ref[...] = nm
        nv_ref[...] = nv
        d_ref[...] = -ADAM_LR * ((nm / bc1) / (jnp.sqrt(nv / bc2) + ADAM_EPS) + ADAM_WD * w_ref[...])

    blk = pl.BlockSpec((None, tr, C), lambda i: (0, i, 0))
    return pl.pallas_call(
        body, name=name, grid=(R // tr,),
        in_specs=[pl.BlockSpec((N_DEV, tr, C), lambda i: (0, i, 0)), blk, blk, blk],
        out_specs=[blk, blk, blk, blk], out_shape=[S((1, R, C), f32)] * 4,
        compiler_params=_params("parallel"),
    )(parts, w, m, v)


def _tile(T, pref):
    return min(T, pref)


def _pad_rows(a, rows):
    return jnp.pad(a, ((0, rows - a.shape[0]), (0, 0)))


def kernel(x, norm_mix_g, w_in, q_norm_g, k_norm_g, attn_sinks, rel_bias, w_attn_o, w_dw, b_dw, conv_ln_g, conv_ln_b, w_conv_out, w_out, norm_mlp_g, w_ff1, w_ff2, loss_target, m_norm_mix_g, m_w_in, m_q_norm_g, m_k_norm_g, m_attn_sinks, m_rel_bias, m_w_attn_o, m_w_dw, m_b_dw, m_conv_ln_g, m_conv_ln_b, m_w_conv_out, m_w_out, m_norm_mlp_g, m_w_ff1, m_w_ff2, v_norm_mix_g, v_w_in, v_q_norm_g, v_k_norm_g, v_attn_sinks, v_rel_bias, v_w_attn_o, v_w_dw, v_b_dw, v_conv_ln_g, v_conv_ln_b, v_w_conv_out, v_w_out, v_norm_mlp_g, v_w_ff1, v_w_ff2):
    T = x.shape[1]
    xs = x[0]
    tgt = loss_target[0]
    in_shard = IN_WIDTH // N_DEV
    dw_rows = CONV_WIDTH + 1
    ch_shard = D_MODEL // N_DEV
    tb = _tile(T, 512)
    tt = _tile(T, 2048)
    bucket = jnp.asarray(_t5_bucket_table())

    g_in, g_dw = _exchange("gather_w_in", [w_in[0].astype(bf16), _pad_rows(w_dw[0], dw_rows)], gather=True, two_level=True)
    W_in = jnp.transpose(g_in, (1, 0, 2)).reshape(D_MODEL, IN_WIDTH)
    W_dw = jnp.transpose(g_dw, (1, 0, 2)).reshape(dw_rows, D_MODEL)[:CONV_WIDTH]

    mix_shards = _Gather([w_attn_o[0].astype(bf16), w_conv_out[0].astype(bf16), w_out[0].astype(bf16), w_ff2[0].astype(bf16)])
    qg2, kg2 = jnp.tile(q_norm_g, (1, 2)), jnp.tile(k_norm_g, (1, 2))
    (proj, u, qn, kn, vb, h0), (g_ao, g_co, g_o, g_f2) = _proj_fwd(xs, norm_mix_g, W_in, qg2, kg2, tb, ride=mix_shards)
    W_ao = g_ao.reshape(D_MODEL, D_MODEL)
    W_co = g_co.reshape(D_MODEL, D_MODEL)
    W_o = g_o.reshape(D_MODEL, D_MODEL)
    bias = _bias_table(rel_bias, bucket)
    (o, lse), (g_f1,) = _attn_fwd(qn, kn, vb, bias, attn_sinks, ride=_Gather([w_ff1[0].astype(bf16).T]))
    W_f1t = g_f1.reshape(D_FF, D_MODEL)
    (h1, h3), _ = _conv_fwd(h0, W_dw, b_dw, conv_ln_g, conv_ln_b, tb)
    W_f2 = g_f2.reshape(D_FF, D_MODEL)
    x1, attn, conv, merged, a, u2, dy, dyb, loss_parts = _mix_ffn_fwd(
        xs, o, h3, proj, W_ao, W_co, W_o, norm_mlp_g, W_f1t, W_f2, tgt, _tile(T, 256))

    gw_f2 = _wgrad("wgrad_ff2", a, dyb, D_MODEL, D_MODEL, tt, relu2=True).reshape(N_DEV, FF_CHUNK, D_MODEL)
    (da, dx1, dx1b, d_norm_mlp_g), (l_f2,) = _ffn_bwd(dy, dyb, a, x1, norm_mlp_g, W_f1t, W_f2, tb,
                                                      ride=_Exchange([gw_f2], gather=False))
    gw_f1 = _wgrad("wgrad_ff1", u2, da, D_MODEL, 4 * FF_CHUNK, tt, slab=FF_CHUNK)
    gw_o = _wgrad("wgrad_out", merged, dx1b, D_MODEL, D_MODEL, tt).reshape(N_DEV, ch_shard, D_MODEL)
    (dattn, dconv, do, dh1, dgates, d_ln_g, d_ln_b, d_b_dw), _ = _mix_bwd(
        dx1b, proj, attn, conv, h1, conv_ln_g, conv_ln_b, W_ao, W_co, W_o, tb)
    gw_ao = _wgrad("wgrad_attn_o", o, dattn, D_MODEL, D_MODEL, tt).reshape(N_DEV, ch_shard, D_MODEL)
    gw_co = _wgrad("wgrad_conv_out", h3, dconv, D_MODEL, D_MODEL, tt).reshape(N_DEV, ch_shard, D_MODEL)
    (dglu, d_w_dw), (l_f1, l_o, l_ao, l_co) = _conv_bwd(dh1, h0, proj, W_dw, tb,
                                                        ride=_Exchange([gw_f1, gw_o, gw_ao, gw_co], gather=False))
    (dq, dk, dv, dbias, d_sinks), _ = _attn_bwd(qn, kn, vb, o, do, lse, bias, attn_sinks)
    d_sinks = d_sinks[:, :N_Q_HEADS]
    d_rel_bias = _rel_bias_bwd(dbias, bucket)
    dqkv, d_qg, d_kg = _qk_norm_bwd(dq, dk, dv, proj, qg2, kg2, tb)
    d_qg = d_qg[:, :HEAD_DIM] + d_qg[:, HEAD_DIM:]
    d_kg = d_kg[:, :HEAD_DIM] + d_kg[:, HEAD_DIM:]
    gw_in = jnp.concatenate([_wgrad("wgrad_in_qkv", u, dqkv, D_MODEL, V_END, tt),
                             _wgrad("wgrad_in_glu", u, dglu, D_MODEL, 2 * D_MODEL, tt),
                             _wgrad("wgrad_in_gates", u, dgates, D_MODEL, 2 * D_MODEL, tt)], axis=1)
    gw_in = jnp.transpose(gw_in.reshape(D_MODEL, N_DEV, in_shard), (1, 0, 2))
    gw_dw = jnp.transpose(d_w_dw.reshape(dw_rows, N_DEV, ch_shard), (1, 0, 2))
    (grad_x, d_norm_mix_g), (l_in, l_dw) = _in_bwd(dqkv, dglu, dgates, W_in, xs, norm_mix_g, dx1, tb,
                                                    ride=_Exchange([gw_in, gw_dw], gather=False))

    def row(vec):
        flat = vec.reshape(1, -1)
        return jnp.pad(flat, ((0, 0), (0, D_MODEL - flat.shape[1])))

    def pack_small(nm, qg, kg, sk, rb, bd, lg, lb, nl, extra=None):
        tail = jnp.concatenate([qg.reshape(1, -1), kg.reshape(1, -1), sk.reshape(1, -1), rb.reshape(1, -1)], axis=1)
        spare = jnp.zeros((1, D_MODEL), f32) if extra is None else row(extra)
        return jnp.concatenate([row(nm), row(bd), row(lg), row(lb), row(nl), row(tail), spare, jnp.zeros((1, D_MODEL), f32)], axis=0)

    def unpack_small(p):
        t = p[5]
        o0, o1, o2 = HEAD_DIM, 2 * HEAD_DIM, 2 * HEAD_DIM + N_Q_HEADS
        return dict(norm_mix_g=p[0:1], b_dw=p[1:2], conv_ln_g=p[2:3], conv_ln_b=p[3:4], norm_mlp_g=p[4:5],
                    q_norm_g=t[0:o0].reshape(1, HEAD_DIM), k_norm_g=t[o0:o1].reshape(1, HEAD_DIM),
                    attn_sinks=t[o1:o2].reshape(1, N_Q_HEADS),
                    rel_bias=t[o2:o2 + N_BUCKETS * N_Q_HEADS].reshape(N_BUCKETS, N_Q_HEADS))

    small_g = pack_small(d_norm_mix_g, d_qg, d_kg, d_sinks, d_rel_bias, d_b_dw, d_ln_g, d_ln_b, d_norm_mlp_g,
                         extra=jnp.sum(loss_parts[:, 0, 0]))
    (l_small,) = _exchange("gather_small_grads", [small_g], gather=True)


    res = {}
    res["w_in"] = _adamw("adamw_in", l_in, w_in, m_w_in, v_w_in, 256)
    res["w_attn_o"] = _adamw("adamw_attn_o", l_ao, w_attn_o, m_w_attn_o, v_w_attn_o, ch_shard)
    res["w_conv_out"] = _adamw("adamw_conv_out", l_co, w_conv_out, m_w_conv_out, v_w_conv_out, ch_shard)
    res["w_out"] = _adamw("adamw_out", l_o, w_out, m_w_out, v_w_out, ch_shard)
    res["w_ff1"] = _adamw("adamw_ff1", l_f1, w_ff1, m_w_ff1, v_w_ff1, 256)
    res["w_ff2"] = _adamw("adamw_ff2", l_f2, w_ff2, m_w_ff2, v_w_ff2, 256)
    pad_dw = lambda t: _pad_rows(t[0], dw_rows)[None]
    res["w_dw"] = [t[:, :CONV_WIDTH] for t in _adamw("adamw_dw", l_dw, pad_dw(w_dw), pad_dw(m_w_dw), pad_dw(v_w_dw), dw_rows)]
    small_w = pack_small(norm_mix_g, q_norm_g, k_norm_g, attn_sinks, rel_bias, b_dw, conv_ln_g, conv_ln_b, norm_mlp_g)
    small_m = pack_small(m_norm_mix_g, m_q_norm_g, m_k_norm_g, m_attn_sinks, m_rel_bias, m_b_dw, m_conv_ln_g, m_conv_ln_b, m_norm_mlp_g)
    small_v = pack_small(v_norm_mix_g, v_q_norm_g, v_k_norm_g, v_attn_sinks, v_rel_bias, v_b_dw, v_conv_ln_g, v_conv_ln_b, v_norm_mlp_g)
    small_out = _adamw("adamw_small", l_small, small_w[None], small_m[None], small_v[None], 8)
    small4 = [unpack_small(t[0]) for t in small_out]
    loss = small_out[0][0, 6, 0]

    order = ["norm_mix_g", "w_in", "q_norm_g", "k_norm_g", "attn_sinks", "rel_bias", "w_attn_o", "w_dw", "b_dw",
             "conv_ln_g", "conv_ln_b", "w_conv_out", "w_out", "norm_mlp_g", "w_ff1", "w_ff2"]
    stacked = {"w_in", "w_attn_o", "w_dw", "w_conv_out", "w_out", "w_ff1", "w_ff2"}
    outs = [loss, grad_x[None]]
    for k in range(4):
        for nme in order:
            if nme in stacked:
                outs.append(res[nme][k])
            else:
                outs.append(small4[k][nme])
    return tuple(outs)
```

```python
import functools

import numpy as np
import jax
import jax.numpy as jnp
from jax import lax
from jax.experimental import pallas as pl
from jax.experimental.pallas import tpu as pltpu

f32 = jnp.float32
bf16 = jnp.bfloat16
S = jax.ShapeDtypeStruct

N_DEV = 8
D_MODEL = 1024
HEAD_DIM = 64
N_Q_HEADS = 16
N_KV_HEADS = 4
GROUP = N_Q_HEADS // N_KV_HEADS
ATTN_WIDTH = N_Q_HEADS * HEAD_DIM
KV_WIDTH = N_KV_HEADS * HEAD_DIM
QBLOCK = 128
CONV_WIDTH = 31
CONV_HALO = 32
CONV_UNIT = 64
D_FF = 4 * D_MODEL
N_BUCKETS = 32
MAX_DISTANCE = 128
EPS = 1e-6
NEG = -1e30
Q_END = ATTN_WIDTH
K_END = Q_END + KV_WIDTH
V_END = K_END + KV_WIDTH
GLU_END = V_END + 2 * D_MODEL
IN_WIDTH = GLU_END + 2 * D_MODEL
COL = 512
FF_CHUNK = D_FF // N_DEV

ADAM_LR = 0.001
ADAM_B1 = 0.9
ADAM_B2 = 0.999
ADAM_EPS = 1e-08
ADAM_WD = 0.01
ADAM_STEP = 10

VMEM_LIMIT = 56 * 1024 * 1024

MESH_ID = pl.DeviceIdType.MESH
ANY = pl.BlockSpec(memory_space=pl.ANY)
SMEM = pl.BlockSpec(memory_space=pltpu.SMEM)


def _params(*sem):
    return pltpu.CompilerParams(dimension_semantics=sem, vmem_limit_bytes=VMEM_LIMIT)


def _nt(a, b):
    return lax.dot_general(a, b, (((1,), (1,)), ((), ())), preferred_element_type=f32)


def _tn(a, b):
    return lax.dot_general(a, b, (((0,), (0,)), ((), ())), preferred_element_type=f32)


def _sigmoid(z):
    return 1.0 / (1.0 + jnp.exp(-z))


def _t5_bucket_table():
    qi = np.arange(QBLOCK, dtype=np.int32)[:, None]
    kj = np.arange(2 * QBLOCK, dtype=np.int32)[None, :]
    dist = qi + QBLOCK - kj
    n = np.maximum(dist, 0)
    max_exact = N_BUCKETS // 2
    nf = np.maximum(n, 1).astype(np.float32)
    large = max_exact + (np.log(nf / np.float32(max_exact)) / np.float32(np.log(MAX_DISTANCE / max_exact))
                         * np.float32(N_BUCKETS - max_exact)).astype(np.int32)
    large = np.minimum(large, N_BUCKETS - 1)
    bucket = np.where(n < max_exact, n, large)
    valid = (dist >= 0) & (dist < QBLOCK)
    return np.where(valid, bucket, -1).astype(np.int32)


def _peer(d):
    x, y, c = lax.axis_index("x"), lax.axis_index("y"), lax.axis_index("c")
    dx, dy, dc = (d >> 2) & 1, (d >> 1) & 1, d & 1
    px, py, pc = x ^ dx, y ^ dy, c ^ dc
    return (px, py, pc), 4 * px + 2 * py + pc


class _Exchange:
    def __init__(self, arrays, gather):
        self.arrays, self.gather, self.n = list(arrays), gather, len(arrays)
        self.out_shape = [S(((N_DEV,) + a.shape) if gather else a.shape, a.dtype) for a in self.arrays]
        self.scratch = [pltpu.SemaphoreType.DMA((self.n, N_DEV - 1)), pltpu.SemaphoreType.DMA((self.n, N_DEV - 1)),
                        pltpu.SemaphoreType.DMA((self.n,))]

    def _copies(self, ins, outs, sems):
        send_sems, recv_sems, local_sems = sems
        _, me = _peer(0)
        local, sends, recvs = [], [], []
        for k in range(self.n):
            src = ins[k] if self.gather else ins[k].at[me]
            local.append(pltpu.make_async_copy(src, outs[k].at[me], local_sems.at[k]))
        for d in range(1, N_DEV):
            peer, pidx = _peer(d)
            for k in range(self.n):
                src = ins[k] if self.gather else ins[k].at[pidx]
                common = dict(src_ref=src, send_sem=send_sems.at[k, d - 1], recv_sem=recv_sems.at[k, d - 1],
                              device_id=peer, device_id_type=MESH_ID)
                sends.append(pltpu.make_async_remote_copy(dst_ref=outs[k].at[me], **common))
                recvs.append(pltpu.make_async_remote_copy(dst_ref=outs[k].at[pidx], **common))
        return local, sends, recvs

    def start(self, ins, outs, sems):
        local, sends, _ = self._copies(ins, outs, sems)
        for cp in local + sends:
            cp.start()

    def wait(self, ins, outs, sems):
        local, sends, recvs = self._copies(ins, outs, sems)
        for cp in recvs:
            cp.wait_recv()
        for cp in sends:
            cp.wait_send()
        for cp in local:
            cp.wait()


class _Gather:
    CHIPS = (4, 2, 6)
    SLOTS = 1 + 2 * len(CHIPS)

    def __init__(self, arrays):
        self.arrays, self.n = list(arrays), len(arrays)
        self.out_shape = [S((N_DEV,) + a.shape, a.dtype) for a in self.arrays]
        self.scratch = [pltpu.SemaphoreType.DMA((self.n, self.SLOTS)), pltpu.SemaphoreType.DMA((self.n, self.SLOTS)),
                        pltpu.SemaphoreType.DMA((self.n,))]

    @staticmethod
    def _copy(outs, sems, k, slot, src, block, to):
        return pltpu.make_async_remote_copy(src_ref=src, dst_ref=outs[k].at[block], send_sem=sems[0].at[k, slot],
                                            recv_sem=sems[1].at[k, slot], device_id=to, device_id_type=MESH_ID)

    def _local(self, ins, outs, sems):
        _, me = _peer(0)
        return [pltpu.make_async_copy(ins[k], outs[k].at[me], sems[2].at[k]) for k in range(self.n)]

    def start(self, ins, outs, sems):
        _, me = _peer(0)
        sibling, _ = _peer(1)
        for cp in self._local(ins, outs, sems):
            cp.start()
        for k in range(self.n):
            self._copy(outs, sems, k, 0, ins[k], me, sibling).start()
            for j, d in enumerate(self.CHIPS):
                self._copy(outs, sems, k, 1 + j, ins[k], me, _peer(d)[0]).start()

    def mid(self, ins, outs, sems):
        sibling, _ = _peer(1)
        for j, d in enumerate(self.CHIPS):
            chip, block = _peer(d)
            for k in range(self.n):
                self._copy(outs, sems, k, 1 + j, ins[k], block, chip).wait_recv()
                self._copy(outs, sems, k, 4 + j, outs[k].at[block], block, sibling).start()

    def wait(self, ins, outs, sems):
        _, me = _peer(0)
        sibling, sib_block = _peer(1)
        for k in range(self.n):
            self._copy(outs, sems, k, 0, ins[k], sib_block, sibling).wait_recv()
            for j, d in enumerate(self.CHIPS):
                self._copy(outs, sems, k, 4 + j, ins[k], _peer(d ^ 1)[1], sibling).wait_recv()
        for k in range(self.n):
            self._copy(outs, sems, k, 0, ins[k], me, sibling).wait_send()
            for j, d in enumerate(self.CHIPS):
                chip, block = _peer(d)
                self._copy(outs, sems, k, 1 + j, ins[k], me, chip).wait_send()
                self._copy(outs, sems, k, 4 + j, outs[k].at[block], block, sibling).wait_send()
        for cp in self._local(ins, outs, sems):
            cp.wait()


def _exchange(name, arrays, gather, two_level=False):
    ex = _Gather(arrays) if two_level else _Exchange(arrays, gather)
    n = ex.n

    def body(*refs):
        ins, outs, sems = refs[:n], refs[n:2 * n], refs[2 * n:]
        ex.start(ins, outs, sems)
        if two_level:
            ex.mid(ins, outs, sems)
        ex.wait(ins, outs, sems)

    return pl.pallas_call(body, name=name, out_shape=ex.out_shape, in_specs=[ANY] * n, out_specs=[ANY] * n,
                          scratch_shapes=ex.scratch)(*arrays)


def _call(body, *, name, grid, in_specs, out_specs, out_shape, args, scratch_shapes=(), ride=None):
    n_in, n_out, n_sc = len(in_specs), len(out_specs), len(scratch_shapes)
    sem = ("arbitrary",) * len(grid)
    if ride is None:
        res = pl.pallas_call(body, name=name, grid=grid, in_specs=list(in_specs), out_specs=list(out_specs),
                             out_shape=list(out_shape), scratch_shapes=list(scratch_shapes), compiler_params=_params(*sem))(*args)
        return list(res), []
    nx = ride.n

    def riding(*refs):
        ins, xin = refs[:n_in], refs[n_in:n_in + nx]
        outs, xout = refs[n_in + nx:n_in + nx + n_out], refs[n_in + nx + n_out:n_in + 2 * nx + n_out]
        rest = refs[n_in + 2 * nx + n_out:]
        scratch, sems = rest[:n_sc], rest[n_sc:]
        ids = [pl.program_id(ax) for ax in range(len(grid))]
        first = functools.reduce(jnp.logical_and, [i == 0 for i in ids])
        last = functools.reduce(jnp.logical_and, [i == g - 1 for i, g in zip(ids, grid)])

        @pl.when(first)
        def _():
            ride.start(xin, xout, sems)

        if hasattr(ride, "mid"):
            halfway = functools.reduce(jnp.logical_and, [ids[0] == 3 * grid[0] // 4] + [i == 0 for i in ids[1:]])

            @pl.when(halfway)
            def _():
                ride.mid(xin, xout, sems)

        body(*ins, *outs, *scratch)

        @pl.when(last)
        def _():
            ride.wait(xin, xout, sems)

    res = pl.pallas_call(
        riding, name=name, grid=grid, in_specs=list(in_specs) + [ANY] * nx, out_specs=list(out_specs) + [ANY] * nx,
        out_shape=list(out_shape) + ride.out_shape, scratch_shapes=list(scratch_shapes) + ride.scratch,
        compiler_params=_params(*sem))(*args, *ride.arrays)
    return list(res[:n_out]), list(res[n_out:])


def _resident(shape):
    return pl.BlockSpec(shape, lambda *_: (0,) * len(shape), pipeline_mode=pl.Buffered(1))


def _proj_fwd(x, g, w, qg, kg, tm, ride=None):
    T, K = x.shape
    N = w.shape[1]
    per = COL // (2 * HEAD_DIM)
    assert Q_END % COL == 0 and V_END == Q_END + COL and (GLU_END - V_END) == 4 * COL and KV_WIDTH == COL // 2

    def body(x_ref, g_ref, w_ref, qg_ref, kg_ref, o_ref, u_ref, qn_ref, kn_ref, vb_ref, h0_ref):
        xv = x_ref[...]
        r = lax.rsqrt(jnp.mean(xv * xv, axis=-1, keepdims=True) + EPS)
        u = (xv * r * g_ref[...]).astype(bf16)
        u_ref[...] = u

        def block(c):
            cs = slice(c * COL, (c + 1) * COL)
            pc = jnp.dot(u, w_ref[:, cs], preferred_element_type=f32)
            o_ref[:, cs] = pc.astype(bf16)
            return pc

        qgv = qg_ref[...] * (HEAD_DIM ** -0.5)
        for c in range(Q_END // COL):
            pc = block(c)
            for t in range(per):
                xq = pc[:, _pair_cols(t)]
                qn_ref[:, _pair_cols(c * per + t)] = (xq * _pair_rstd(xq, False) * qgv).astype(bf16)
        pc = block(Q_END // COL)
        for t in range(KV_WIDTH // (2 * HEAD_DIM)):
            xk = pc[:, _pair_cols(t)]
            kn_ref[:, _pair_cols(t)] = (xk * _pair_rstd(xk, False) * kg_ref[...]).astype(bf16)
        vb_ref[...] = pc[:, KV_WIDTH:].astype(bf16)
        a0 = V_END // COL
        for half in range(2):
            gate = block(a0 + 2 + half)
            h0_ref[:, half * COL:(half + 1) * COL] = block(a0 + half) * _sigmoid(gate)
        for c in range(GLU_END // COL, N // COL):
            block(c)

    row = lambda width: pl.BlockSpec((tm, width), lambda i: (i, 0))
    return _call(
        body, name="proj_fwd", grid=(T // tm,),
        in_specs=[row(K), _resident((1, K)), _resident((K, N)), _resident((1, 2 * HEAD_DIM)), _resident((1, 2 * HEAD_DIM))],
        out_specs=[row(N), row(K), row(ATTN_WIDTH), row(KV_WIDTH), row(KV_WIDTH), row(D_MODEL)],
        out_shape=[S((T, N), bf16), S((T, K), bf16), S((T, ATTN_WIDTH), bf16), S((T, KV_WIDTH), bf16), S((T, KV_WIDTH), bf16),
                   S((T, D_MODEL), f32)],
        args=(x, g, w, qg, kg), ride=ride)


def _bias_table(rel_bias, bucket):
    def body(rb_ref, bk_ref, o_ref):
        b = bk_ref[...]
        absent = lax.broadcasted_iota(jnp.int32, (QBLOCK, 2 * QBLOCK), 1) < QBLOCK
        for h in range(N_Q_HEADS):
            acc = jnp.full((QBLOCK, 2 * QBLOCK), NEG, f32)
            for k in range(N_BUCKETS):
                acc = jnp.where(b == k, rb_ref[k, h], acc)
            o_ref[0, h * QBLOCK:(h + 1) * QBLOCK, :] = acc
            o_ref[1, h * QBLOCK:(h + 1) * QBLOCK, :] = jnp.where(absent, NEG, acc)

    return pl.pallas_call(
        body, name="bias_table", out_shape=S((2, N_Q_HEADS * QBLOCK, 2 * QBLOCK), f32),
        in_specs=[SMEM, pl.BlockSpec(memory_space=pltpu.VMEM)],
    )(rel_bias, bucket)


def _bias_spec():
    return pl.BlockSpec((None, N_Q_HEADS * QBLOCK, 2 * QBLOCK), lambda n: (jnp.where(n == 0, 1, 0), 0, 0))


def _swap_halves(t):
    return jnp.concatenate([t[:, HEAD_DIM:], t[:, :HEAD_DIM]], axis=1)


def _low_lanes():
    return lax.broadcasted_iota(jnp.int32, (1, 2 * HEAD_DIM), 1) < HEAD_DIM


def _one_head(pair, side):
    zero = jnp.zeros((), pair.dtype)
    return jnp.where(_low_lanes(), pair, zero) if side == 0 else jnp.where(_low_lanes(), zero, pair)


def _pair_mean(t, on_mxu):
    if not on_mxu:
        m_lo = jnp.sum(_one_head(t, 0), axis=-1, keepdims=True) * (1.0 / HEAD_DIM)
        m_hi = jnp.sum(_one_head(t, 1), axis=-1, keepdims=True) * (1.0 / HEAD_DIM)
        return jnp.where(_low_lanes(), m_lo, m_hi)
    width = 2 * HEAD_DIM
    same_head = ((lax.broadcasted_iota(jnp.int32, (width, width), 0) < HEAD_DIM)
                 == (lax.broadcasted_iota(jnp.int32, (width, width), 1) < HEAD_DIM))
    e = jnp.where(same_head, 1.0 / HEAD_DIM, 0.0).astype(bf16)
    hi = t.astype(bf16)
    lo = (t - hi.astype(f32)).astype(bf16)
    return jnp.dot(hi, e, preferred_element_type=f32) + jnp.dot(lo, e, preferred_element_type=f32)


def _pair_rstd(x, on_mxu):
    return lax.rsqrt(_pair_mean(x * x, on_mxu) + EPS)


def _kv_placements(band):
    out = {}
    for m in range(N_KV_HEADS // 2):
        pair = band[:, m * 2 * HEAD_DIM:(m + 1) * 2 * HEAD_DIM]
        swapped = _swap_halves(pair)
        for hh in range(2):
            out[2 * m + hh, 0] = _one_head(pair if hh == 0 else swapped, 0)
            out[2 * m + hh, 1] = _one_head(swapped if hh == 0 else pair, 1)
    return out


def _head_rows(hq):
    return slice(hq * QBLOCK, (hq + 1) * QBLOCK)


def _pair_cols(pr):
    return slice(pr * 2 * HEAD_DIM, (pr + 1) * 2 * HEAD_DIM)


def _attn_fwd(qn, kn, vb, bias, sinks, ride=None):
    T = qn.shape[0]
    nb = T // QBLOCK

    def body(q_ref, kc_ref, kp_ref, vc_ref, vp_ref, b_ref, s_ref, o_ref, lse_ref, s_scr, p_scr):
        lane = lax.broadcasted_iota(jnp.int32, (QBLOCK, 2 * HEAD_DIM), 1)
        kx = _kv_placements(jnp.concatenate([kp_ref[...], kc_ref[...]], axis=0))
        vx = _kv_placements(jnp.concatenate([vp_ref[...], vc_ref[...]], axis=0))
        for hq in range(N_Q_HEADS):
            qm = _one_head(q_ref[:, _pair_cols(hq // 2)], hq % 2)
            s_scr[_head_rows(hq), :] = _nt(qm, kx[hq // GROUP, hq % 2]) + b_ref[_head_rows(hq), :]
        lse_tile = jnp.zeros((QBLOCK, 2 * HEAD_DIM), f32)
        for hq in range(N_Q_HEADS):
            s = s_scr[_head_rows(hq), :]
            sink = s_ref[0, hq]
            m = jnp.maximum(jnp.max(s, axis=-1, keepdims=True), sink)
            p = jnp.exp(s - m)
            l = jnp.sum(p, axis=-1, keepdims=True) + jnp.exp(sink - m)
            p_scr[_head_rows(hq), :] = (p * (1.0 / l)).astype(bf16)
            lse_tile = jnp.where(lane == hq, m + jnp.log(l), lse_tile)
        lse_ref[...] = lse_tile
        for pr in range(N_Q_HEADS // 2):
            h = 2 * pr // GROUP
            o_pair = (jnp.dot(p_scr[_head_rows(2 * pr), :], vx[h, 0], preferred_element_type=f32)
                      + jnp.dot(p_scr[_head_rows(2 * pr + 1), :], vx[h, 1], preferred_element_type=f32))
            o_ref[:, _pair_cols(pr)] = o_pair.astype(bf16)

    cur = lambda n: (n, 0)
    prev = lambda n: (jnp.maximum(n - 1, 0), 0)
    return _call(
        body, name="attn_fwd", grid=(nb,),
        in_specs=[pl.BlockSpec((QBLOCK, ATTN_WIDTH), cur), pl.BlockSpec((QBLOCK, KV_WIDTH), cur),
                  pl.BlockSpec((QBLOCK, KV_WIDTH), prev), pl.BlockSpec((QBLOCK, KV_WIDTH), cur),
                  pl.BlockSpec((QBLOCK, KV_WIDTH), prev), _bias_spec(), SMEM],
        out_specs=[pl.BlockSpec((QBLOCK, ATTN_WIDTH), cur), pl.BlockSpec((QBLOCK, 2 * HEAD_DIM), cur)],
        out_shape=[S((T, ATTN_WIDTH), bf16), S((T, 2 * HEAD_DIM), f32)],
        scratch_shapes=[pltpu.VMEM((N_Q_HEADS * QBLOCK, 2 * QBLOCK), f32), pltpu.VMEM((N_Q_HEADS * QBLOCK, 2 * QBLOCK), bf16)],
        args=(qn, kn, kn, vb, vb, bias, sinks), ride=ride)


def _layer_norm_stats(h1):
    mu = jnp.mean(h1, axis=-1, keepdims=True)
    xc = h1 - mu
    rstd = lax.rsqrt(jnp.mean(xc * xc, axis=-1, keepdims=True) + EPS)
    return xc * rstd, rstd


def _advanced_windows(win):
    rows = win.shape[0]
    for r in range(8):
        yield r, (win if r == 0 else pltpu.roll(win, rows - r, 0))


def _tap_offsets(r, rows):
    for q in range((rows - CONV_UNIT) // 8 + 1):
        if r == 0 or 8 * q + r + CONV_UNIT <= rows:
            yield q, 8 * q + r


def _conv_fwd(h0, w_dw, b_dw, ln_g, ln_b, tm, ride=None):
    T = h0.shape[0]
    per = tm // CONV_HALO
    lead = CONV_HALO - (CONV_WIDTH - 1)

    def body(hc_ref, hp_ref, w_ref, b_ref, g_ref, bb_ref, h1_ref, h3_ref, cat):
        i = pl.program_id(0)
        cat[0:CONV_HALO, :] = jnp.where(i == 0, 0.0, hp_ref[...])
        cat[CONV_HALO:, :] = hc_ref[...]

        def unit_rows(c, carry):
            r0 = pl.multiple_of(c * CONV_UNIT, CONV_UNIT)
            for j in range(D_MODEL // 128):
                ls = slice(j * 128, (j + 1) * 128)
                win = cat[pl.ds(r0, CONV_UNIT + CONV_HALO), ls]
                acc = jnp.zeros((CONV_UNIT, 128), f32) + b_ref[:, ls]
                for r, adv in _advanced_windows(win):
                    for q, off in _tap_offsets(r, CONV_UNIT + CONV_HALO):
                        k = off - lead
                        if 0 <= k < CONV_WIDTH:
                            acc = acc + adv[8 * q:8 * q + CONV_UNIT] * w_ref[k:k + 1, ls]
                h1_ref[pl.ds(r0, CONV_UNIT), ls] = acc
            return carry

        lax.fori_loop(0, tm // CONV_UNIT, unit_rows, 0)
        acc = h1_ref[...]
        xhat, _ = _layer_norm_stats(acc)
        h2 = xhat * g_ref[...] + bb_ref[...]
        h3_ref[...] = (h2 * _sigmoid(h2)).astype(bf16)

    vec = pl.BlockSpec((1, D_MODEL), lambda i: (0, 0))
    return _call(
        body, name="conv_fwd", grid=(T // tm,),
        in_specs=[pl.BlockSpec((tm, D_MODEL), lambda i: (i, 0)),
                  pl.BlockSpec((CONV_HALO, D_MODEL), lambda i: (jnp.maximum(i * per - 1, 0), 0)),
                  pl.BlockSpec((CONV_WIDTH, D_MODEL), lambda i: (0, 0)), vec, vec, vec],
        out_specs=[pl.BlockSpec((tm, D_MODEL), lambda i: (i, 0)), pl.BlockSpec((tm, D_MODEL), lambda i: (i, 0))],
        out_shape=[S((T, D_MODEL), f32), S((T, D_MODEL), bf16)],
        scratch_shapes=[pltpu.VMEM((tm + CONV_HALO, D_MODEL), f32)],
        args=(h0, h0, w_dw, b_dw, ln_g, ln_b), ride=ride)


def _mix_ffn_fwd(x, o, h3, proj, w_ao, w_co, w_o, g, w1, w2, target, tm):
    T = x.shape[0]
    nj = w1.shape[0] // FF_CHUNK
    g0 = GLU_END // COL

    def gate_spec(off):
        return pl.BlockSpec((tm, COL), lambda i: (i, g0 + off))

    def body(x_ref, o_ref, h3_ref, ga0, ga1, gc0, gc1, wa_ref, wc_ref, wo_ref, g_ref, w1_ref, w2_ref, t_ref,
             x1_ref, at_ref, cv_ref, mg_ref, a_ref, u_ref, dy_ref, dyb_ref, ls_ref, hm):
        attn = jnp.dot(o_ref[...], wa_ref[...], preferred_element_type=f32)
        conv = jnp.dot(h3_ref[...], wc_ref[...], preferred_element_type=f32)
        ga = jnp.concatenate([ga0[...], ga1[...]], axis=-1).astype(f32)
        gc = jnp.concatenate([gc0[...], gc1[...]], axis=-1).astype(f32)
        merged = (_sigmoid(ga) * attn + _sigmoid(gc) * conv).astype(bf16)
        at_ref[...] = attn.astype(bf16)
        cv_ref[...] = conv.astype(bf16)
        mg_ref[...] = merged
        xv = x_ref[...] + jnp.dot(merged, wo_ref[...], preferred_element_type=f32)
        x1_ref[...] = xv
        r = lax.rsqrt(jnp.mean(xv * xv, axis=-1, keepdims=True) + EPS)
        u = (xv * r * g_ref[...]).astype(bf16)
        u_ref[...] = u
        for j in range(nj):
            js = slice(j * FF_CHUNK, (j + 1) * FF_CHUNK)
            a = _nt(u, w1_ref[js, :])
            a_ref[:, js] = a.astype(bf16)
            hm[:, js] = jnp.square(jnp.maximum(a, 0.0)).astype(bf16)
        err = xv + jnp.dot(hm[...], w2_ref[...], preferred_element_type=f32) - t_ref[...]
        dy = err * (1.0 / D_MODEL)
        dy_ref[...] = dy
        dyb_ref[...] = dy.astype(bf16)
        ls_ref[...] = jnp.zeros((8, 128), f32) + jnp.sum(err * err) * (0.5 / D_MODEL)

    row = pl.BlockSpec((tm, D_MODEL), lambda i: (i, 0))
    wide = pl.BlockSpec((tm, D_FF), lambda i: (i, 0))
    wsp = _resident((D_MODEL, D_MODEL))
    return pl.pallas_call(
        body, name="mix_ffn_fwd", grid=(T // tm,),
        in_specs=[row, row, row, gate_spec(0), gate_spec(1), gate_spec(2), gate_spec(3), wsp, wsp, wsp,
                  _resident((1, D_MODEL)), _resident(w1.shape), _resident(w2.shape), row],
        out_specs=[row, row, row, row, wide, row, row, row, pl.BlockSpec((None, 8, 128), lambda i: (i, 0, 0))],
        out_shape=[S((T, D_MODEL), f32), S((T, D_MODEL), bf16), S((T, D_MODEL), bf16), S((T, D_MODEL), bf16),
                   S((T, D_FF), bf16), S((T, D_MODEL), bf16), S((T, D_MODEL), f32), S((T, D_MODEL), bf16),
                   S((T // tm, 8, 128), f32)],
        scratch_shapes=[pltpu.VMEM((tm, D_FF), bf16)],
        compiler_params=_params("parallel"),
    )(x, o, h3, proj, proj, proj, proj, w_ao, w_co, w_o, g, w1, w2, target)


def _rms_bwd(du, xv, gv):
    r = lax.rsqrt(jnp.mean(xv * xv, axis=-1, keepdims=True) + EPS)
    xn = xv * r
    dg = jnp.sum(du * xn, axis=0, keepdims=True)
    dxn = du * gv
    dx = r * (dxn - xn * jnp.mean(dxn * xn, axis=-1, keepdims=True))
    return dx, dg


def _ffn_bwd(dy, dyb, a, x1, g, w1, w2, tm, ride=None):
    T = dy.shape[0]
    nj = w1.shape[0] // FF_CHUNK

    def body(dy_ref, dyb_ref, a_ref, x_ref, g_ref, w1_ref, w2_ref, da_ref, dx_ref, dxb_ref, dg_ref):
        @pl.when(pl.program_id(0) == 0)
        def _():
            dg_ref[...] = jnp.zeros_like(dg_ref)

        dyb_v = dyb_ref[...]
        for j in range(nj):
            js = slice(j * FF_CHUNK, (j + 1) * FF_CHUNK)
            dh = _nt(dyb_v, w2_ref[js, :])
            da_ref[:, js] = (dh * (2.0 * jnp.maximum(a_ref[:, js].astype(f32), 0.0))).astype(bf16)
        du = jnp.dot(da_ref[...], w1_ref[...], preferred_element_type=f32)
        dx, dg = _rms_bwd(du, x_ref[...], g_ref[...])
        dx1 = dy_ref[...] + dx
        dx_ref[...] = dx1
        dxb_ref[...] = dx1.astype(bf16)
        dg_ref[...] += dg

    row = pl.BlockSpec((tm, D_MODEL), lambda i: (i, 0))
    wide = pl.BlockSpec((tm, D_FF), lambda i: (i, 0))
    vec = pl.BlockSpec((1, D_MODEL), lambda i: (0, 0))
    return _call(
        body, name="ffn_bwd", grid=(T // tm,),
        in_specs=[row, row, wide, row, _resident((1, D_MODEL)), _resident(w1.shape), _resident(w2.shape)],
        out_specs=[wide, row, row, vec],
        out_shape=[S((T, D_FF), bf16), S((T, D_MODEL), f32), S((T, D_MODEL), bf16), S((1, D_MODEL), f32)],
        args=(dy, dyb, a, x1, g, w1, w2), ride=ride)


def _wgrad(name, a, b, tk, tn, tt, relu2=False, slab=None, out_dtype=bf16):
    T, Ka = a.shape
    Nb = b.shape[1]
    nt = T // tt

    def body(a_ref, b_ref, o_ref, acc):
        t = pl.program_id(2)
        av = a_ref[...]
        if relu2:
            av = jnp.square(jnp.maximum(av.astype(f32), 0.0))
        prod = _tn(av.astype(bf16), b_ref[...].astype(bf16))

        @pl.when(t == 0)
        def _():
            acc[...] = prod

        @pl.when(t > 0)
        def _():
            acc[...] += prod

        @pl.when(t == nt - 1)
        def _():
            if slab is None:
                o_ref[...] = acc[...].astype(out_dtype)
            else:
                for s in range(tn // slab):
                    o_ref[s] = acc[:, s * slab:(s + 1) * slab].astype(out_dtype)

    if slab is not None:
        out_shape = S((Nb // slab, Ka, slab), out_dtype)
        out_spec = pl.BlockSpec((tn // slab, tk, slab), lambda i, j, t: (j, i, 0))
    else:
        out_shape = S((Ka, Nb), out_dtype)
        out_spec = pl.BlockSpec((tk, tn), lambda i, j, t: (i, j))
    return pl.pallas_call(
        body, name=name, grid=(Ka // tk, Nb // tn, nt),
        in_specs=[pl.BlockSpec((tt, tk), lambda i, j, t: (t, i)), pl.BlockSpec((tt, tn), lambda i, j, t: (t, j))],
        out_specs=out_spec, out_shape=out_shape, scratch_shapes=[pltpu.VMEM((tk, tn), f32)],
        compiler_params=_params("parallel", "parallel", "arbitrary"),
    )(a, b)


def _wgrad_square(name, pairs, tt):
    T = pairs[0][0].shape[0]
    nt = T // tt
    n = len(pairs)

    def body(*refs):
        ins, outs, acc = refs[:2 * n], refs[2 * n:3 * n], refs[3 * n]
        t = pl.program_id(0)
        for k in range(n):
            prod = _tn(ins[2 * k][...], ins[2 * k + 1][...])

            @pl.when(t == 0)
            def _(k=k, prod=prod):
                acc[k] = prod

            @pl.when(t > 0)
            def _(k=k, prod=prod):
                acc[k] += prod

            @pl.when(t == nt - 1)
            def _(k=k):
                outs[k][...] = acc[k].astype(bf16)

    tile = pl.BlockSpec((tt, D_MODEL), lambda t: (t, 0))
    whole = pl.BlockSpec((D_MODEL, D_MODEL), lambda t: (0, 0))
    return pl.pallas_call(
        body, name=name, grid=(nt,), in_specs=[tile] * (2 * n), out_specs=[whole] * n,
        out_shape=[S((D_MODEL, D_MODEL), bf16)] * n, scratch_shapes=[pltpu.VMEM((n, D_MODEL, D_MODEL), f32)],
        compiler_params=_params("arbitrary"),
    )(*[t for pair in pairs for t in pair])


def _mix_bwd(dx1, proj, attn, conv, h1, ln_g, ln_b, w_ao, w_co, w_o, tm, ride=None):
    T = dx1.shape[0]
    g0 = GLU_END // COL

    def gate_spec(off):
        return pl.BlockSpec((tm, COL), lambda i: (i, g0 + off))

    def body(dx_ref, ga0, ga1, gc0, gc1, at_ref, cv_ref, h_ref, g_ref, b_ref, wa_ref, wc_ref, wo_ref,
             da_ref, dc_ref, do_ref, dh1_ref, dg_ref, dlg_ref, dlb_ref, dbd_ref):
        @pl.when(pl.program_id(0) == 0)
        def _():
            dlg_ref[...] = jnp.zeros_like(dlg_ref)
            dlb_ref[...] = jnp.zeros_like(dlb_ref)
            dbd_ref[...] = jnp.zeros_like(dbd_ref)

        dm = _nt(dx_ref[...].astype(bf16), wo_ref[...])
        sa = _sigmoid(jnp.concatenate([ga0[...], ga1[...]], axis=-1).astype(f32))
        sc = _sigmoid(jnp.concatenate([gc0[...], gc1[...]], axis=-1).astype(f32))
        dattn = (dm * sa).astype(bf16)
        dconv = (dm * sc).astype(bf16)
        da_ref[...] = dattn
        dc_ref[...] = dconv
        dg_ref[:, 0:D_MODEL] = (dm * at_ref[...].astype(f32) * sa * (1.0 - sa)).astype(bf16)
        dg_ref[:, D_MODEL:2 * D_MODEL] = (dm * cv_ref[...].astype(f32) * sc * (1.0 - sc)).astype(bf16)
        do_ref[...] = _nt(dattn, wa_ref[...]).astype(bf16)
        dh3 = _nt(dconv, wc_ref[...])
        xhat, rstd = _layer_norm_stats(h_ref[...])
        h2 = xhat * g_ref[...] + b_ref[...]
        sg = _sigmoid(h2)
        dh2 = dh3 * (sg * (1.0 + h2 * (1.0 - sg)))
        dlg_ref[...] += jnp.sum(dh2 * xhat, axis=0, keepdims=True)
        dlb_ref[...] += jnp.sum(dh2, axis=0, keepdims=True)
        dxh = dh2 * g_ref[...]
        dh1 = rstd * (dxh - jnp.mean(dxh, axis=-1, keepdims=True) - xhat * jnp.mean(dxh * xhat, axis=-1, keepdims=True))
        dh1_ref[...] = dh1
        dbd_ref[...] += jnp.sum(dh1, axis=0, keepdims=True)

    row = pl.BlockSpec((tm, D_MODEL), lambda i: (i, 0))
    vec = pl.BlockSpec((1, D_MODEL), lambda i: (0, 0))
    par = _resident((1, D_MODEL))
    wsp = _resident((D_MODEL, D_MODEL))
    return _call(
        body, name="mix_bwd", grid=(T // tm,),
        in_specs=[row, gate_spec(0), gate_spec(1), gate_spec(2), gate_spec(3), row, row, row, par, par, wsp, wsp, wsp],
        out_specs=[row, row, row, row, pl.BlockSpec((tm, 2 * D_MODEL), lambda i: (i, 0)), vec, vec, vec],
        out_shape=[S((T, D_MODEL), bf16), S((T, D_MODEL), bf16), S((T, D_MODEL), bf16), S((T, D_MODEL), f32),
                   S((T, 2 * D_MODEL), bf16), S((1, D_MODEL), f32), S((1, D_MODEL), f32), S((1, D_MODEL), f32)],
        args=(dx1, proj, proj, proj, proj, attn, conv, h1, ln_g, ln_b, w_ao, w_co, w_o), ride=ride)


def _conv_bwd(dh1, h0, proj, w_dw, tm, ride=None):
    T = dh1.shape[0]
    per = tm // CONV_HALO
    nh = T // CONV_HALO
    nt = T // tm
    a0 = V_END // COL
    lead = CONV_HALO - (CONV_WIDTH - 1)

    def body(dc_ref, dn_ref, hc_ref, hp_ref, a0_ref, a1_ref, g0_ref, g1_ref, w_ref, dglu_ref, dw_ref, dcat, hcat, wacc, dh0):
        i = pl.program_id(0)

        @pl.when(i == 0)
        def _():
            wacc[...] = jnp.zeros_like(wacc)

        dcat[0:tm, :] = dc_ref[...]
        dcat[tm:, :] = jnp.where(i == nt - 1, 0.0, dn_ref[...])
        hcat[0:CONV_HALO, :] = jnp.where(i == 0, 0.0, hp_ref[...])
        hcat[CONV_HALO:, :] = hc_ref[...]
        span = CONV_UNIT + CONV_HALO

        def unit_rows(c, carry):
            r0 = pl.multiple_of(c * CONV_UNIT, CONV_UNIT)
            for j in range(D_MODEL // 128):
                ls = slice(j * 128, (j + 1) * 128)
                dwin = dcat[pl.ds(r0, span), ls]
                acc = jnp.zeros((CONV_UNIT, 128), f32)
                for r, adv in _advanced_windows(dwin):
                    for q, off in _tap_offsets(r, span):
                        k = CONV_WIDTH - 1 - off
                        if 0 <= k < CONV_WIDTH:
                            acc = acc + adv[8 * q:8 * q + CONV_UNIT] * w_ref[k:k + 1, ls]
                dh0[pl.ds(r0, CONV_UNIT), ls] = acc
                dcur = dwin[0:CONV_UNIT]
                for r, adv in _advanced_windows(hcat[pl.ds(r0, span), ls]):
                    for q, off in _tap_offsets(r, span):
                        k = off - lead
                        if 0 <= k < CONV_WIDTH:
                            prod = dcur * adv[8 * q:8 * q + CONV_UNIT]
                            wacc[k, :, ls] += jnp.sum(prod.reshape(CONV_UNIT // 8, 8, 128), axis=0)
            return carry

        lax.fori_loop(0, tm // CONV_UNIT, unit_rows, 0)
        dh0v = dh0[...]
        av = jnp.concatenate([a0_ref[...], a1_ref[...]], axis=-1).astype(f32)
        sg = _sigmoid(jnp.concatenate([g0_ref[...], g1_ref[...]], axis=-1).astype(f32))
        dglu_ref[:, 0:D_MODEL] = (dh0v * sg).astype(bf16)
        dglu_ref[:, D_MODEL:2 * D_MODEL] = (dh0v * av * sg * (1.0 - sg)).astype(bf16)

        @pl.when(i == nt - 1)
        def _():
            for k in range(CONV_WIDTH):
                dw_ref[k:k + 1, :] = jnp.sum(wacc[k], axis=0, keepdims=True)
            dw_ref[CONV_WIDTH:CONV_WIDTH + 1, :] = jnp.zeros((1, D_MODEL), f32)

    row = pl.BlockSpec((tm, D_MODEL), lambda i: (i, 0))

    def col_spec(off):
        return pl.BlockSpec((tm, COL), lambda i: (i, a0 + off))

    return _call(
        body, name="conv_bwd", grid=(nt,),
        in_specs=[row, pl.BlockSpec((CONV_HALO, D_MODEL), lambda i: (jnp.minimum((i + 1) * per, nh - 1), 0)),
                  row, pl.BlockSpec((CONV_HALO, D_MODEL), lambda i: (jnp.maximum(i * per - 1, 0), 0)),
                  col_spec(0), col_spec(1), col_spec(2), col_spec(3),
                  pl.BlockSpec((CONV_WIDTH, D_MODEL), lambda i: (0, 0))],
        out_specs=[pl.BlockSpec((tm, 2 * D_MODEL), lambda i: (i, 0)), pl.BlockSpec((CONV_WIDTH + 1, D_MODEL), lambda i: (0, 0))],
        out_shape=[S((T, 2 * D_MODEL), bf16), S((CONV_WIDTH + 1, D_MODEL), f32)],
        scratch_shapes=[pltpu.VMEM((tm + CONV_HALO, D_MODEL), f32), pltpu.VMEM((tm + CONV_HALO, D_MODEL), f32),
                        pltpu.VMEM((CONV_WIDTH, 8, D_MODEL), f32), pltpu.VMEM((tm, D_MODEL), f32)],
        args=(dh1, dh1, h0, h0, proj, proj, proj, proj, w_dw), ride=ride)


def _attn_bwd(qn, kn, vb, o, do, lse, bias, sinks, ride=None):
    T = qn.shape[0]
    nb = T // QBLOCK

    def body(q_ref, kc_ref, kp_ref, vc_ref, vp_ref, o_ref, do_ref, lse_ref, b_ref, s_ref,
             dq_ref, dk_ref, dv_ref, db_ref, dsk_ref, kcar, vcar, s_scr, dp_scr, p_scr, ds_scr):
        n = pl.program_id(0)

        @pl.when(n == 0)
        def _():
            db_ref[...] = jnp.zeros_like(db_ref)
            dsk_ref[...] = jnp.zeros_like(dsk_ref)
            kcar[...] = jnp.zeros_like(kcar)
            vcar[...] = jnp.zeros_like(vcar)

        @pl.when(n < nb)
        def _():
            lane = lax.broadcasted_iota(jnp.int32, (QBLOCK, 2 * HEAD_DIM), 1)
            lane_row = lax.broadcasted_iota(jnp.int32, (1, 2 * HEAD_DIM), 1)
            kx = _kv_placements(jnp.concatenate([kp_ref[...], kc_ref[...]], axis=0))
            vx = _kv_placements(jnp.concatenate([vp_ref[...], vc_ref[...]], axis=0))
            lse_tile = lse_ref[...]
            delta, lse_c = {}, {}
            for pr in range(N_Q_HEADS // 2):
                dop = do_ref[:, _pair_cols(pr)]
                dl = dop.astype(f32) * o_ref[:, _pair_cols(pr)].astype(f32)
                for side in range(2):
                    hq = 2 * pr + side
                    h = hq // GROUP
                    qm = _one_head(q_ref[:, _pair_cols(pr)], side)
                    s_scr[_head_rows(hq), :] = _nt(qm, kx[h, side]) + b_ref[_head_rows(hq), :]
                    dp_scr[_head_rows(hq), :] = _nt(_one_head(dop, side), vx[h, side])
                    delta[hq] = jnp.sum(_one_head(dl, side), axis=-1, keepdims=True)
                    lse_c[hq] = jnp.sum(jnp.where(lane == hq, lse_tile, 0.0), axis=-1, keepdims=True)
            dsk = jnp.zeros((1, 2 * HEAD_DIM), f32)
            for hq in range(N_Q_HEADS):
                p = jnp.exp(s_scr[_head_rows(hq), :] - lse_c[hq])
                ds = p * (dp_scr[_head_rows(hq), :] - delta[hq])
                db_ref[_head_rows(hq), :] += ds
                p_scr[_head_rows(hq), :] = p.astype(bf16)
                ds_scr[_head_rows(hq), :] = ds.astype(bf16)
                psink = jnp.exp(s_ref[0, hq] - lse_c[hq])
                dsk = dsk - jnp.where(lane_row == hq, jnp.sum(psink * delta[hq], axis=0, keepdims=True), 0.0)
            dsk_ref[...] += dsk
            for pr in range(N_Q_HEADS // 2):
                h = 2 * pr // GROUP
                dq_ref[:, _pair_cols(pr)] = (jnp.dot(ds_scr[_head_rows(2 * pr), :], kx[h, 0], preferred_element_type=f32)
                                             + jnp.dot(ds_scr[_head_rows(2 * pr + 1), :], kx[h, 1], preferred_element_type=f32))
            folded_k, folded_v = [], []
            for h in range(N_KV_HEADS):
                ka = jnp.zeros((2 * QBLOCK, 2 * HEAD_DIM), f32)
                va = jnp.zeros((2 * QBLOCK, 2 * HEAD_DIM), f32)
                for g in range(GROUP):
                    hq = h * GROUP + g
                    ka = ka + _tn(ds_scr[_head_rows(hq), :], _one_head(q_ref[:, _pair_cols(hq // 2)], hq % 2))
                    va = va + _tn(p_scr[_head_rows(hq), :], _one_head(do_ref[:, _pair_cols(hq // 2)], hq % 2))
                folded_k.append(ka + _swap_halves(ka))
                folded_v.append(va + _swap_halves(va))
            low = _low_lanes()
            for m in range(N_KV_HEADS // 2):
                cs = _pair_cols(m)
                for folded, out_ref, car in ((folded_k, dk_ref, kcar), (folded_v, dv_ref, vcar)):
                    band = jnp.where(low, folded[2 * m], folded[2 * m + 1])
                    out_ref[:, cs] = car[:, cs] + band[0:QBLOCK, :]
                    car[:, cs] = band[QBLOCK:, :]

        @pl.when(n == nb)
        def _():
            dk_ref[...] = kcar[...]
            dv_ref[...] = vcar[...]

    cur = lambda n: (jnp.minimum(n, nb - 1), 0)
    prev = lambda n: (jnp.clip(n - 1, 0, nb - 1), 0)
    qspec = pl.BlockSpec((QBLOCK, ATTN_WIDTH), cur)
    kcur, kprev = pl.BlockSpec((QBLOCK, KV_WIDTH), cur), pl.BlockSpec((QBLOCK, KV_WIDTH), prev)
    whole = lambda shape: pl.BlockSpec(shape, lambda n: (0,) * len(shape))
    scores = (N_Q_HEADS * QBLOCK, 2 * QBLOCK)
    return _call(
        body, name="attn_bwd", grid=(nb + 1,),
        in_specs=[qspec, kcur, kprev, kcur, kprev, qspec, qspec, pl.BlockSpec((QBLOCK, 2 * HEAD_DIM), cur), _bias_spec(), SMEM],
        out_specs=[qspec, kprev, kprev, whole(scores), whole((1, 2 * HEAD_DIM))],
        out_shape=[S((T, ATTN_WIDTH), f32), S((T, KV_WIDTH), f32), S((T, KV_WIDTH), f32), S(scores, f32),
                   S((1, 2 * HEAD_DIM), f32)],
        scratch_shapes=[pltpu.VMEM((QBLOCK, KV_WIDTH), f32), pltpu.VMEM((QBLOCK, KV_WIDTH), f32),
                        pltpu.VMEM(scores, f32), pltpu.VMEM(scores, f32), pltpu.VMEM(scores, bf16), pltpu.VMEM(scores, bf16)],
        args=(qn, kn, kn, vb, vb, o, do, lse, bias, sinks), ride=ride)


def _rel_bias_bwd(dbias, bucket):
    def body(d_ref, bk_ref, o_ref):
        b = bk_ref[...]
        for k in range(N_BUCKETS):
            mk = b == k
            for h in range(N_Q_HEADS):
                o_ref[k, h] = jnp.sum(jnp.where(mk, d_ref[h * QBLOCK:(h + 1) * QBLOCK, :], 0.0))

    return pl.pallas_call(body, name="rel_bias_bwd", out_shape=S((N_BUCKETS, N_Q_HEADS), f32), out_specs=SMEM)(dbias, bucket)


def _qk_norm_bwd(dq, dk, dv, proj, qg, kg, tm):
    T = dq.shape[0]
    scale = HEAD_DIM ** -0.5

    def pair_bwd(dy, x, gv):
        r = _pair_rstd(x, True)
        xn = x * r
        dxn = dy * gv
        dx = r * (dxn - xn * _pair_mean(dxn * xn, True))
        return dx, jnp.sum(dy * xn, axis=0, keepdims=True)

    def body(dq_ref, dk_ref, dv_ref, p_ref, qg_ref, kg_ref, out_ref, dqg_ref, dkg_ref):
        @pl.when(pl.program_id(0) == 0)
        def _():
            dqg_ref[...] = jnp.zeros_like(dqg_ref)
            dkg_ref[...] = jnp.zeros_like(dkg_ref)

        qgv, kgv = qg_ref[...], kg_ref[...]
        dqg = jnp.zeros((1, 2 * HEAD_DIM), f32)
        for pr in range(N_Q_HEADS // 2):
            dx, dg = pair_bwd(dq_ref[:, _pair_cols(pr)] * scale, p_ref[:, _pair_cols(pr)].astype(f32), qgv)
            out_ref[:, _pair_cols(pr)] = dx.astype(bf16)
            dqg = dqg + dg
        dkg = jnp.zeros((1, 2 * HEAD_DIM), f32)
        for pr in range(N_KV_HEADS // 2):
            ps = slice(Q_END + pr * 2 * HEAD_DIM, Q_END + (pr + 1) * 2 * HEAD_DIM)
            dx, dg = pair_bwd(dk_ref[:, _pair_cols(pr)], p_ref[:, ps].astype(f32), kgv)
            out_ref[:, ps] = dx.astype(bf16)
            dkg = dkg + dg
        out_ref[:, K_END:V_END] = dv_ref[...].astype(bf16)
        dqg_ref[...] += dqg
        dkg_ref[...] += dkg

    vec = pl.BlockSpec((1, 2 * HEAD_DIM), lambda i: (0, 0))
    return pl.pallas_call(
        body, name="qk_norm_bwd", grid=(T // tm,),
        in_specs=[pl.BlockSpec((tm, ATTN_WIDTH), lambda i: (i, 0)), pl.BlockSpec((tm, KV_WIDTH), lambda i: (i, 0)),
                  pl.BlockSpec((tm, KV_WIDTH), lambda i: (i, 0)), pl.BlockSpec((tm, V_END), lambda i: (i, 0)), vec, vec],
        out_specs=[pl.BlockSpec((tm, V_END), lambda i: (i, 0)), vec, vec],
        out_shape=[S((T, V_END), bf16), S((1, 2 * HEAD_DIM), f32), S((1, 2 * HEAD_DIM), f32)],
        compiler_params=_params("arbitrary"),
    )(dq, dk, dv, proj, qg, kg)


def _in_bwd(dqkv, dglu, dgates, w_in, x, g, dx1, tm, ride=None):
    T = x.shape[0]
    pieces = (dqkv, dglu, dgates)
    starts = [0, dqkv.shape[1], dqkv.shape[1] + dglu.shape[1]]

    def body(a0_ref, a1_ref, a2_ref, w_ref, x_ref, g_ref, d_ref, gx_ref, dg_ref):
        @pl.when(pl.program_id(0) == 0)
        def _():
            dg_ref[...] = jnp.zeros_like(dg_ref)

        du = jnp.zeros((tm, D_MODEL), f32)
        for a_ref, c0 in zip((a0_ref, a1_ref, a2_ref), starts):
            du = du + _nt(a_ref[...], w_ref[:, c0:c0 + a_ref.shape[1]])
        dx, dg = _rms_bwd(du, x_ref[...], g_ref[...])
        gx_ref[...] = d_ref[...] + dx
        dg_ref[...] += dg

    row = pl.BlockSpec((tm, D_MODEL), lambda i: (i, 0))
    return _call(
        body, name="in_bwd", grid=(T // tm,),
        in_specs=[pl.BlockSpec((tm, p.shape[1]), lambda i: (i, 0)) for p in pieces]
        + [_resident(w_in.shape), row, _resident((1, D_MODEL)), row],
        out_specs=[row, pl.BlockSpec((1, D_MODEL), lambda i: (0, 0))],
        out_shape=[S((T, D_MODEL), f32), S((1, D_MODEL), f32)],
        args=(dqkv, dglu, dgates, w_in, x, g, dx1), ride=ride)


def _adamw(name, parts, w, m, v, tr):
    _, R, C = w.shape
    bc1 = 1.0 - ADAM_B1 ** ADAM_STEP
    bc2 = 1.0 - ADAM_B2 ** ADAM_STEP

    def body(p_ref, w_ref, m_ref, v_ref, g_ref, d_ref, nm_ref, nv_ref):
        g = p_ref[0].astype(f32)
        for k in range(1, N_DEV):
            g = g + p_ref[k].astype(f32)
        nm = ADAM_B1 * m_ref[...] + (1.0 - ADAM_B1) * g
        nv = ADAM_B2 * v_ref[...] + (1.0 - ADAM_B2) * (g * g)
        g_ref[...] = g
        nm_ref[...] = nm
        nv_ref[...] = nv
        d_ref[...] = -ADAM_LR * ((nm / bc1) / (jnp.sqrt(nv / bc2) + ADAM_EPS) + ADAM_WD * w_ref[...])

    blk = pl.BlockSpec((None, tr, C), lambda i: (0, i, 0))
    return pl.pallas_call(
        body, name=name, grid=(R // tr,),
        in_specs=[pl.BlockSpec((N_DEV, tr, C), lambda i: (0, i, 0)), blk, blk, blk],
        out_specs=[blk, blk, blk, blk], out_shape=[S((1, R, C), f32)] * 4,
        compiler_params=_params("parallel"),
    )(parts, w, m, v)


def _tile(T, pref):
    return min(T, pref)


def _pad_rows(a, rows):
    return jnp.pad(a, ((0, rows - a.shape[0]), (0, 0)))


def kernel(x, norm_mix_g, w_in, q_norm_g, k_norm_g, attn_sinks, rel_bias, w_attn_o, w_dw, b_dw, conv_ln_g, conv_ln_b, w_conv_out, w_out, norm_mlp_g, w_ff1, w_ff2, loss_target, m_norm_mix_g, m_w_in, m_q_norm_g, m_k_norm_g, m_attn_sinks, m_rel_bias, m_w_attn_o, m_w_dw, m_b_dw, m_conv_ln_g, m_conv_ln_b, m_w_conv_out, m_w_out, m_norm_mlp_g, m_w_ff1, m_w_ff2, v_norm_mix_g, v_w_in, v_q_norm_g, v_k_norm_g, v_attn_sinks, v_rel_bias, v_w_attn_o, v_w_dw, v_b_dw, v_conv_ln_g, v_conv_ln_b, v_w_conv_out, v_w_out, v_norm_mlp_g, v_w_ff1, v_w_ff2):
    T = x.shape[1]
    xs = x[0]
    tgt = loss_target[0]
    in_shard = IN_WIDTH // N_DEV
    dw_rows = CONV_WIDTH + 1
    ch_shard = D_MODEL // N_DEV
    tb = _tile(T, 512)
    tt = _tile(T, 2048)
    bucket = jnp.asarray(_t5_bucket_table())

    g_in, g_dw = _exchange("gather_w_in", [w_in[0].astype(bf16), _pad_rows(w_dw[0], dw_rows)], gather=True, two_level=True)
    W_in = jnp.transpose(g_in, (1, 0, 2)).reshape(D_MODEL, IN_WIDTH)
    W_dw = jnp.transpose(g_dw, (1, 0, 2)).reshape(dw_rows, D_MODEL)[:CONV_WIDTH]

    mix_shards = _Gather([w_attn_o[0].astype(bf16), w_conv_out[0].astype(bf16), w_out[0].astype(bf16), w_ff2[0].astype(bf16)])
    qg2, kg2 = jnp.tile(q_norm_g, (1, 2)), jnp.tile(k_norm_g, (1, 2))
    (proj, u, qn, kn, vb, h0), (g_ao, g_co, g_o, g_f2) = _proj_fwd(xs, norm_mix_g, W_in, qg2, kg2, tb, ride=mix_shards)
    W_ao = g_ao.reshape(D_MODEL, D_MODEL)
    W_co = g_co.reshape(D_MODEL, D_MODEL)
    W_o = g_o.reshape(D_MODEL, D_MODEL)
    bias = _bias_table(rel_bias, bucket)
    (o, lse), (g_f1,) = _attn_fwd(qn, kn, vb, bias, attn_sinks, ride=_Gather([w_ff1[0].astype(bf16).T]))
    W_f1t = g_f1.reshape(D_FF, D_MODEL)
    (h1, h3), _ = _conv_fwd(h0, W_dw, b_dw, conv_ln_g, conv_ln_b, tb)
    W_f2 = g_f2.reshape(D_FF, D_MODEL)
    x1, attn, conv, merged, a, u2, dy, dyb, loss_parts = _mix_ffn_fwd(
        xs, o, h3, proj, W_ao, W_co, W_o, norm_mlp_g, W_f1t, W_f2, tgt, _tile(T, 256))

    gw_f2 = _wgrad("wgrad_ff2", a, dyb, D_MODEL, D_MODEL, tt, relu2=True).reshape(N_DEV, FF_CHUNK, D_MODEL)
    (da, dx1, dx1b, d_norm_mlp_g), (l_f2,) = _ffn_bwd(dy, dyb, a, x1, norm_mlp_g, W_f1t, W_f2, tb,
                                                      ride=_Exchange([gw_f2], gather=False))
    gw_f1 = _wgrad("wgrad_ff1", u2, da, D_MODEL, 4 * FF_CHUNK, tt, slab=FF_CHUNK)
    (dattn, dconv, do, dh1, dgates, d_ln_g, d_ln_b, d_b_dw), _ = _mix_bwd(
        dx1b, proj, attn, conv, h1, conv_ln_g, conv_ln_b, W_ao, W_co, W_o, tb)
    gw_o, gw_ao, gw_co = [g.reshape(N_DEV, ch_shard, D_MODEL) for g in _wgrad_square(
        "wgrad_mix", [(merged, dx1b), (o, dattn), (h3, dconv)], _tile(T, 1024))]
    (dglu, d_w_dw), (l_f1, l_o, l_ao, l_co) = _conv_bwd(dh1, h0, proj, W_dw, tb,
                                                        ride=_Exchange([gw_f1, gw_o, gw_ao, gw_co], gather=False))
    (dq, dk, dv, dbias, d_sinks), _ = _attn_bwd(qn, kn, vb, o, do, lse, bias, attn_sinks)
    d_sinks = d_sinks[:, :N_Q_HEADS]
    d_rel_bias = _rel_bias_bwd(dbias, bucket)
    dqkv, d_qg, d_kg = _qk_norm_bwd(dq, dk, dv, proj, qg2, kg2, tb)
    d_qg = d_qg[:, :HEAD_DIM] + d_qg[:, HEAD_DIM:]
    d_kg = d_kg[:, :HEAD_DIM] + d_kg[:, HEAD_DIM:]
    gw_in = jnp.concatenate([_wgrad("wgrad_in_qkv", u, dqkv, D_MODEL, V_END, tt),
                             _wgrad("wgrad_in_glu", u, dglu, D_MODEL, 2 * D_MODEL, tt),
                             _wgrad("wgrad_in_gates", u, dgates, D_MODEL, 2 * D_MODEL, tt)], axis=1)
    gw_in = jnp.transpose(gw_in.reshape(D_MODEL, N_DEV, in_shard), (1, 0, 2))
    gw_dw = jnp.transpose(d_w_dw.reshape(dw_rows, N_DEV, ch_shard), (1, 0, 2))
    (grad_x, d_norm_mix_g), (l_in, l_dw) = _in_bwd(dqkv, dglu, dgates, W_in, xs, norm_mix_g, dx1, tb,
                                                    ride=_Exchange([gw_in, gw_dw], gather=False))

    def row(vec):
        flat = vec.reshape(1, -1)
        return jnp.pad(flat, ((0, 0), (0, D_MODEL - flat.shape[1])))

    def pack_small(nm, qg, kg, sk, rb, bd, lg, lb, nl, extra=None):
        tail = jnp.concatenate([qg.reshape(1, -1), kg.reshape(1, -1), sk.reshape(1, -1), rb.reshape(1, -1)], axis=1)
        spare = jnp.zeros((1, D_MODEL), f32) if extra is None else row(extra)
        return jnp.concatenate([row(nm), row(bd), row(lg), row(lb), row(nl), row(tail), spare, jnp.zeros((1, D_MODEL), f32)], axis=0)

    def unpack_small(p):
        t = p[5]
        o0, o1, o2 = HEAD_DIM, 2 * HEAD_DIM, 2 * HEAD_DIM + N_Q_HEADS
        return dict(norm_mix_g=p[0:1], b_dw=p[1:2], conv_ln_g=p[2:3], conv_ln_b=p[3:4], norm_mlp_g=p[4:5],
                    q_norm_g=t[0:o0].reshape(1, HEAD_DIM), k_norm_g=t[o0:o1].reshape(1, HEAD_DIM),
                    attn_sinks=t[o1:o2].reshape(1, N_Q_HEADS),
                    rel_bias=t[o2:o2 + N_BUCKETS * N_Q_HEADS].reshape(N_BUCKETS, N_Q_HEADS))

    small_g = pack_small(d_norm_mix_g, d_qg, d_kg, d_sinks, d_rel_bias, d_b_dw, d_ln_g, d_ln_b, d_norm_mlp_g,
                         extra=jnp.sum(loss_parts[:, 0, 0]))
    (l_small,) = _exchange("gather_small_grads", [small_g], gather=True)


    res = {}
    res["w_in"] = _adamw("adamw_in", l_in, w_in, m_w_in, v_w_in, 256)
    res["w_attn_o"] = _adamw("adamw_attn_o", l_ao, w_attn_o, m_w_attn_o, v_w_attn_o, ch_shard)
    res["w_conv_out"] = _adamw("adamw_conv_out", l_co, w_conv_out, m_w_conv_out, v_w_conv_out, ch_shard)
    res["w_out"] = _adamw("adamw_out", l_o, w_out, m_w_out, v_w_out, ch_shard)
    res["w_ff1"] = _adamw("adamw_ff1", l_f1, w_ff1, m_w_ff1, v_w_ff1, 256)
    res["w_ff2"] = _adamw("adamw_ff2", l_f2, w_ff2, m_w_ff2, v_w_ff2, 256)
    pad_dw = lambda t: _pad_rows(t[0], dw_rows)[None]
    res["w_dw"] = [t[:, :CONV_WIDTH] for t in _adamw("adamw_dw", l_dw, pad_dw(w_dw), pad_dw(m_w_dw), pad_dw(v_w_dw), dw_rows)]
    small_w = pack_small(norm_mix_g, q_norm_g, k_norm_g, attn_sinks, rel_bias, b_dw, conv_ln_g, conv_ln_b, norm_mlp_g)
    small_m = pack_small(m_norm_mix_g, m_q_norm_g, m_k_norm_g, m_attn_sinks, m_rel_bias, m_b_dw, m_conv_ln_g, m_conv_ln_b, m_norm_mlp_g)
    small_v = pack_small(v_norm_mix_g, v_q_norm_g, v_k_norm_g, v_attn_sinks, v_rel_bias, v_b_dw, v_conv_ln_g, v_conv_ln_b, v_norm_mlp_g)
    small_out = _adamw("adamw_small", l_small, small_w[None], small_m[None], small_v[None], 8)
    small4 = [unpack_small(t[0]) for t in small_out]
    loss = small_out[0][0, 6, 0]

    order = ["norm_mix_g", "w_in", "q_norm_g", "k_norm_g", "attn_sinks", "rel_bias", "w_attn_o", "w_dw", "b_dw",
             "conv_ln_g", "conv_ln_b", "w_conv_out", "w_out", "norm_mlp_g", "w_ff1", "w_ff2"]
    stacked = {"w_in", "w_attn_o", "w_dw", "w_conv_out", "w_out", "w_ff1", "w_ff2"}
    outs = [loss, grad_x[None]]
    for k in range(4):
        for nme in order:
            if nme in stacked:
                outs.append(res[nme][k])
            else:
                outs.append(small4[k][nme])
    return tuple(outs)
```
